```python
import jax, jax.numpy as jnp
from jax import lax
import numpy as np

D_MODEL = 1024
BATCH = 8
SEQ = 4096
DEPTH = 1

D_MIX = D_MODEL
D_CONV = D_MIX // 2
N_CONV_HEADS = 8
CONV_WIDTH = 31
D_POOL = D_MIX - D_CONV
POOL_WINDOWS = (2, 4, 8, 16)
N_POOL_GROUPS = len(POOL_WINDOWS)
POOL_GROUP_DIM = D_POOL // N_POOL_GROUPS
D_IN = 2 * D_CONV + D_POOL
D_FF = 2816
FFN_CONV_WIDTH = 3
EPS = 1e-6

kernel_name = "hybrid_conformer_pool_convffn_block"


def _rmsnorm(x, g):
    xf = x.astype(jnp.float32)
    y = xf * lax.rsqrt(jnp.mean(xf * xf, axis=-1, keepdims=True) + EPS)
    return (y * g.astype(jnp.float32)).astype(x.dtype)


def _layernorm(x, g, b):
    xf = x.astype(jnp.float32)
    mu = jnp.mean(xf, axis=-1, keepdims=True)
    xc = xf - mu
    var = jnp.mean(xc * xc, axis=-1, keepdims=True)
    y = xc * lax.rsqrt(var + EPS) * g.astype(jnp.float32) + b.astype(jnp.float32)
    return y.astype(x.dtype)


def _causal_dwconv(x, w, b):
    k, c = w.shape
    y = lax.conv_general_dilated(
        x, w[:, None, :].astype(x.dtype), window_strides=(1,),
        padding=((k - 1, 0),), dimension_numbers=("NWC", "WIO", "NWC"),
        feature_group_count=c)
    return y + b


def _conformer_conv_mixer(a_val, a_gate, conv_w, conv_b, ln_g, ln_b):
    h = a_val * jax.nn.sigmoid(a_gate)
    h = _causal_dwconv(h, conv_w, conv_b)
    h = _layernorm(h, ln_g, ln_b)
    return jax.nn.silu(h)


def _multiscale_pool_mixer(xb, pool_w, pool_scale):
    bsz, s, _ = xb.shape
    xg = xb.reshape(bsz, s, N_POOL_GROUPS, POOL_GROUP_DIM).astype(jnp.float32)
    cs = jnp.cumsum(xg, axis=1)
    pos = jnp.arange(1, s + 1, dtype=jnp.float32)
    outs = []
    for gi, w in enumerate(POOL_WINDOWS):
        c = cs[:, :, gi]
        lag = jnp.pad(c[:, : s - w], ((0, 0), (w, 0), (0, 0)))
        cnt = jnp.minimum(pos, float(w))[None, :, None]
        outs.append((c - lag) / cnt - xg[:, :, gi])
    d = jnp.stack(outs, axis=2).astype(xb.dtype)
    y = jnp.einsum("bsgc,gcd->bsgd", d, pool_w).reshape(bsz, s, D_POOL)
    return y * pool_scale


def _fwd_setup_inputs(seed: int = 0) -> dict:
    key = jax.random.key(seed)
    ks = jax.random.split(key, 17)
    f32 = jnp.float32
    L = DEPTH

    def nrm(k, shape, scale):
        return jax.random.normal(k, shape, f32) * scale

    return {
        "x": jax.random.normal(ks[0], (BATCH, SEQ, D_MODEL), f32),
        "norm_mix_g": 1.0 + nrm(ks[1], (L, D_MODEL), 0.05),
        "w_in": nrm(ks[2], (L, D_MODEL, D_IN), D_MODEL ** -0.5),
        "conv_a_w": nrm(ks[3], (L, CONV_WIDTH, D_CONV), CONV_WIDTH ** -0.5),
        "conv_a_b": nrm(ks[4], (L, D_CONV), 0.02),
        "ln_a_g": 1.0 + nrm(ks[5], (L, D_CONV), 0.05),
        "ln_a_b": nrm(ks[6], (L, D_CONV), 0.02),
        "pool_w": nrm(ks[7], (L, N_POOL_GROUPS, POOL_GROUP_DIM, POOL_GROUP_DIM), POOL_GROUP_DIM ** -0.5),
        "pool_scale": 1.0 + nrm(ks[8], (L, D_POOL), 0.1),
        "w_out": nrm(ks[9], (L, D_MIX, D_MODEL), D_MIX ** -0.5),
        "norm_ffn_g": 1.0 + nrm(ks[10], (L, D_MODEL), 0.05),
        "w_up": nrm(ks[11], (L, D_MODEL, 2 * D_FF), D_MODEL ** -0.5),
        "conv_f_w": nrm(ks[12], (L, FFN_CONV_WIDTH, D_FF), FFN_CONV_WIDTH ** -0.5),
        "conv_f_b": nrm(ks[13], (L, D_FF), 0.02),
        "w_down": nrm(ks[14], (L, D_FF, D_MODEL), D_FF ** -0.5),
        "norm_final_g": 1.0 + nrm(ks[15], (D_MODEL,), 0.05),
    }


def _fwd_reference(x, norm_mix_g, w_in, conv_a_w, conv_a_b, ln_a_g, ln_a_b, pool_w,
              pool_scale, w_out, norm_ffn_g, w_up, conv_f_w, conv_f_b, w_down,
              norm_final_g):
    for l in range(DEPTH):
        h = _rmsnorm(x, norm_mix_g[l])
        proj = h @ w_in[l]
        a_val, a_gate, b_in = jnp.split(proj, [D_CONV, 2 * D_CONV], axis=-1)
        ya = _conformer_conv_mixer(a_val, a_gate, conv_a_w[l], conv_a_b[l],
                                   ln_a_g[l], ln_a_b[l])
        yb = _multiscale_pool_mixer(b_in, pool_w[l], pool_scale[l])
        x = x + jnp.concatenate([ya, yb], axis=-1) @ w_out[l]
        h = _rmsnorm(x, norm_ffn_g[l])
        gate, val = jnp.split(h @ w_up[l], 2, axis=-1)
        gate = _causal_dwconv(gate, conv_f_w[l], conv_f_b[l])
        x = x + (jax.nn.silu(gate) * val) @ w_down[l]
    return _rmsnorm(x, norm_final_g)


import jax as _jax
import jax.numpy as _jnp

TWIN_FORMAT = 'train_step'
FWD_PARAMS = ['x', 'norm_mix_g', 'w_in', 'conv_a_w', 'conv_a_b', 'ln_a_g', 'ln_a_b', 'pool_w', 'pool_scale', 'w_out', 'norm_ffn_g', 'w_up', 'conv_f_w', 'conv_f_b', 'w_down', 'norm_final_g']
TWIN_WEIGHTS = ['norm_mix_g', 'w_in', 'conv_a_w', 'conv_a_b', 'ln_a_g', 'ln_a_b', 'pool_w', 'pool_scale', 'w_out', 'norm_ffn_g', 'w_up', 'conv_f_w', 'conv_f_b', 'w_down', 'norm_final_g']
TWIN_DIFF_INPUT = 'x'
TWIN_INPUTS = ['x', 'norm_mix_g', 'w_in', 'conv_a_w', 'conv_a_b', 'ln_a_g', 'ln_a_b', 'pool_w', 'pool_scale', 'w_out', 'norm_ffn_g', 'w_up', 'conv_f_w', 'conv_f_b', 'w_down', 'norm_final_g', 'loss_target', 'm_norm_mix_g', 'm_w_in', 'm_conv_a_w', 'm_conv_a_b', 'm_ln_a_g', 'm_ln_a_b', 'm_pool_w', 'm_pool_scale', 'm_w_out', 'm_norm_ffn_g', 'm_w_up', 'm_conv_f_w', 'm_conv_f_b', 'm_w_down', 'm_norm_final_g', 'v_norm_mix_g', 'v_w_in', 'v_conv_a_w', 'v_conv_a_b', 'v_ln_a_g', 'v_ln_a_b', 'v_pool_w', 'v_pool_scale', 'v_w_out', 'v_norm_ffn_g', 'v_w_up', 'v_conv_f_w', 'v_conv_f_b', 'v_w_down', 'v_norm_final_g']
TWIN_OUTPUTS = ['loss', 'grad_x', 'grad_norm_mix_g', 'grad_w_in', 'grad_conv_a_w', 'grad_conv_a_b', 'grad_ln_a_g', 'grad_ln_a_b', 'grad_pool_w', 'grad_pool_scale', 'grad_w_out', 'grad_norm_ffn_g', 'grad_w_up', 'grad_conv_f_w', 'grad_conv_f_b', 'grad_w_down', 'grad_norm_final_g', 'delta_norm_mix_g', 'delta_w_in', 'delta_conv_a_w', 'delta_conv_a_b', 'delta_ln_a_g', 'delta_ln_a_b', 'delta_pool_w', 'delta_pool_scale', 'delta_w_out', 'delta_norm_ffn_g', 'delta_w_up', 'delta_conv_f_w', 'delta_conv_f_b', 'delta_w_down', 'delta_norm_final_g', 'new_m_norm_mix_g', 'new_m_w_in', 'new_m_conv_a_w', 'new_m_conv_a_b', 'new_m_ln_a_g', 'new_m_ln_a_b', 'new_m_pool_w', 'new_m_pool_scale', 'new_m_w_out', 'new_m_norm_ffn_g', 'new_m_w_up', 'new_m_conv_f_w', 'new_m_conv_f_b', 'new_m_w_down', 'new_m_norm_final_g', 'new_v_norm_mix_g', 'new_v_w_in', 'new_v_conv_a_w', 'new_v_conv_a_b', 'new_v_ln_a_g', 'new_v_ln_a_b', 'new_v_pool_w', 'new_v_pool_scale', 'new_v_w_out', 'new_v_norm_ffn_g', 'new_v_w_up', 'new_v_conv_f_w', 'new_v_conv_f_b', 'new_v_w_down', 'new_v_norm_final_g']
TWIN_LEAF_KINDS = {'loss': 'loss', 'grad_x': 'grad_x', 'grad_norm_mix_g': 'grad_w', 'grad_w_in': 'grad_w', 'grad_conv_a_w': 'grad_w', 'grad_conv_a_b': 'grad_w', 'grad_ln_a_g': 'grad_w', 'grad_ln_a_b': 'grad_w', 'grad_pool_w': 'grad_w', 'grad_pool_scale': 'grad_w', 'grad_w_out': 'grad_w', 'grad_norm_ffn_g': 'grad_w', 'grad_w_up': 'grad_w', 'grad_conv_f_w': 'grad_w', 'grad_conv_f_b': 'grad_w', 'grad_w_down': 'grad_w', 'grad_norm_final_g': 'grad_w', 'delta_norm_mix_g': 'delta_w', 'delta_w_in': 'delta_w', 'delta_conv_a_w': 'delta_w', 'delta_conv_a_b': 'delta_w', 'delta_ln_a_g': 'delta_w', 'delta_ln_a_b': 'delta_w', 'delta_pool_w': 'delta_w', 'delta_pool_scale': 'delta_w', 'delta_w_out': 'delta_w', 'delta_norm_ffn_g': 'delta_w', 'delta_w_up': 'delta_w', 'delta_conv_f_w': 'delta_w', 'delta_conv_f_b': 'delta_w', 'delta_w_down': 'delta_w', 'delta_norm_final_g': 'delta_w', 'new_m_norm_mix_g': 'new_m', 'new_m_w_in': 'new_m', 'new_m_conv_a_w': 'new_m', 'new_m_conv_a_b': 'new_m', 'new_m_ln_a_g': 'new_m', 'new_m_ln_a_b': 'new_m', 'new_m_pool_w': 'new_m', 'new_m_pool_scale': 'new_m', 'new_m_w_out': 'new_m', 'new_m_norm_ffn_g': 'new_m', 'new_m_w_up': 'new_m', 'new_m_conv_f_w': 'new_m', 'new_m_conv_f_b': 'new_m', 'new_m_w_down': 'new_m', 'new_m_norm_final_g': 'new_m', 'new_v_norm_mix_g': 'new_v', 'new_v_w_in': 'new_v', 'new_v_conv_a_w': 'new_v', 'new_v_conv_a_b': 'new_v', 'new_v_ln_a_g': 'new_v', 'new_v_ln_a_b': 'new_v', 'new_v_pool_w': 'new_v', 'new_v_pool_scale': 'new_v', 'new_v_w_out': 'new_v', 'new_v_norm_ffn_g': 'new_v', 'new_v_w_up': 'new_v', 'new_v_conv_f_w': 'new_v', 'new_v_conv_f_b': 'new_v', 'new_v_w_down': 'new_v', 'new_v_norm_final_g': 'new_v'}


def _forward(args):
    return _fwd_reference(*[args[k] for k in FWD_PARAMS])


def _output_shape():
    out = _jax.eval_shape(lambda: _forward(_fwd_setup_inputs(0)))
    return out.shape, out.dtype

N_MICROBATCH = 1
ADAM_LR = 0.001
ADAM_B1 = 0.9
ADAM_B2 = 0.999
ADAM_EPS = 1e-08
ADAM_WD = 0.01
ADAM_STEP = 10
PER_EXAMPLE_BATCH_AXIS = {'x': 0, 'loss_target': 0}
SHARED_INPUTS = []
_WEIGHT_DTYPES = {'norm_mix_g': _jnp.float32, 'w_in': _jnp.float32, 'conv_a_w': _jnp.float32, 'conv_a_b': _jnp.float32, 'ln_a_g': _jnp.float32, 'ln_a_b': _jnp.float32, 'pool_w': _jnp.float32, 'pool_scale': _jnp.float32, 'w_out': _jnp.float32, 'norm_ffn_g': _jnp.float32, 'w_up': _jnp.float32, 'conv_f_w': _jnp.float32, 'conv_f_b': _jnp.float32, 'w_down': _jnp.float32, 'norm_final_g': _jnp.float32}
MOMENT_SCALE = {'norm_mix_g': 1.223864e-01, 'w_in': 1.004763e-01, 'conv_a_w': 1.017114e-01, 'conv_a_b': 2.103845e-01, 'ln_a_g': 1.254437e-01, 'ln_a_b': 1.134595e-01, 'pool_w': 1.429426e-01, 'pool_scale': 1.386956e-01, 'w_out': 1.229322e-01, 'norm_ffn_g': 1.232462e-01, 'w_up': 4.827157e-02, 'conv_f_w': 4.964566e-02, 'conv_f_b': 4.766006e-02, 'w_down': 7.937363e-02, 'norm_final_g': 3.199926e+01}


def _to_microbatches(a, axis):
    t = _jnp.moveaxis(a, axis, 0)
    t = t.reshape((N_MICROBATCH, t.shape[0] // N_MICROBATCH) + t.shape[1:])
    return _jnp.moveaxis(t, 1, axis + 1)


def setup_inputs(seed: int = 0) -> dict:
    inp = _fwd_setup_inputs(seed)
    key = _jax.random.fold_in(_jax.random.key(seed), 7919)
    shape, _ = _output_shape()
    out = dict(inp)
    out["loss_target"] = _jax.random.normal(_jax.random.fold_in(key, 0), shape, _jnp.float32)
    for i, name in enumerate(TWIN_WEIGHTS):
        w = inp[name].astype(_jnp.float32)
        if MOMENT_SCALE is None:
            s = _jnp.sqrt(_jnp.mean(_jnp.square(w)) + 1e-30)
        else:
            s = MOMENT_SCALE[name]
        km, kv = _jax.random.split(_jax.random.fold_in(key, i + 1))
        out[name] = w
        out["m_" + name] = s * _jax.random.normal(km, w.shape, _jnp.float32)
        out["v_" + name] = (s * s) * _jax.random.uniform(kv, w.shape, _jnp.float32, 0.5, 1.5)
    if N_MICROBATCH > 1:
        for name, axis in PER_EXAMPLE_BATCH_AXIS.items():
            out[name] = _to_microbatches(out[name], axis)
    return {'x': out['x'], 'norm_mix_g': out['norm_mix_g'], 'w_in': out['w_in'], 'conv_a_w': out['conv_a_w'], 'conv_a_b': out['conv_a_b'], 'ln_a_g': out['ln_a_g'], 'ln_a_b': out['ln_a_b'], 'pool_w': out['pool_w'], 'pool_scale': out['pool_scale'], 'w_out': out['w_out'], 'norm_ffn_g': out['norm_ffn_g'], 'w_up': out['w_up'], 'conv_f_w': out['conv_f_w'], 'conv_f_b': out['conv_f_b'], 'w_down': out['w_down'], 'norm_final_g': out['norm_final_g'], 'loss_target': out['loss_target'], 'm_norm_mix_g': out['m_norm_mix_g'], 'm_w_in': out['m_w_in'], 'm_conv_a_w': out['m_conv_a_w'], 'm_conv_a_b': out['m_conv_a_b'], 'm_ln_a_g': out['m_ln_a_g'], 'm_ln_a_b': out['m_ln_a_b'], 'm_pool_w': out['m_pool_w'], 'm_pool_scale': out['m_pool_scale'], 'm_w_out': out['m_w_out'], 'm_norm_ffn_g': out['m_norm_ffn_g'], 'm_w_up': out['m_w_up'], 'm_conv_f_w': out['m_conv_f_w'], 'm_conv_f_b': out['m_conv_f_b'], 'm_w_down': out['m_w_down'], 'm_norm_final_g': out['m_norm_final_g'], 'v_norm_mix_g': out['v_norm_mix_g'], 'v_w_in': out['v_w_in'], 'v_conv_a_w': out['v_conv_a_w'], 'v_conv_a_b': out['v_conv_a_b'], 'v_ln_a_g': out['v_ln_a_g'], 'v_ln_a_b': out['v_ln_a_b'], 'v_pool_w': out['v_pool_w'], 'v_pool_scale': out['v_pool_scale'], 'v_w_out': out['v_w_out'], 'v_norm_ffn_g': out['v_norm_ffn_g'], 'v_w_up': out['v_w_up'], 'v_conv_f_w': out['v_conv_f_w'], 'v_conv_f_b': out['v_conv_f_b'], 'v_w_down': out['v_w_down'], 'v_norm_final_g': out['v_norm_final_g']}


def _loss(weights, diff, rest, loss_target):
    with _jax.named_scope("forward"):
        args = {**rest, TWIN_DIFF_INPUT: diff, **{k: w.astype(_WEIGHT_DTYPES[k]) for k, w in weights.items()}}
        y = _forward(args)
    with _jax.named_scope("loss_head"):
        err = _jnp.square(y.astype(_jnp.float32) - loss_target)
        return 0.5 * _jnp.sum(_jnp.mean(err, axis=-1)) if err.ndim else 0.5 * err


def _adamw(w, g, m, v):
    m = ADAM_B1 * m + (1.0 - ADAM_B1) * g
    v = ADAM_B2 * v + (1.0 - ADAM_B2) * _jnp.square(g)
    m_hat = m / (1.0 - ADAM_B1 ** ADAM_STEP)
    v_hat = v / (1.0 - ADAM_B2 ** ADAM_STEP)
    delta = -ADAM_LR * (m_hat / (_jnp.sqrt(v_hat) + ADAM_EPS) + ADAM_WD * w)
    return delta, m, v


def reference(x, norm_mix_g, w_in, conv_a_w, conv_a_b, ln_a_g, ln_a_b, pool_w, pool_scale, w_out, norm_ffn_g, w_up, conv_f_w, conv_f_b, w_down, norm_final_g, loss_target, m_norm_mix_g, m_w_in, m_conv_a_w, m_conv_a_b, m_ln_a_g, m_ln_a_b, m_pool_w, m_pool_scale, m_w_out, m_norm_ffn_g, m_w_up, m_conv_f_w, m_conv_f_b, m_w_down, m_norm_final_g, v_norm_mix_g, v_w_in, v_conv_a_w, v_conv_a_b, v_ln_a_g, v_ln_a_b, v_pool_w, v_pool_scale, v_w_out, v_norm_ffn_g, v_w_up, v_conv_f_w, v_conv_f_b, v_w_down, v_norm_final_g):
    given = dict(x=x, norm_mix_g=norm_mix_g, w_in=w_in, conv_a_w=conv_a_w, conv_a_b=conv_a_b, ln_a_g=ln_a_g, ln_a_b=ln_a_b, pool_w=pool_w, pool_scale=pool_scale, w_out=w_out, norm_ffn_g=norm_ffn_g, w_up=w_up, conv_f_w=conv_f_w, conv_f_b=conv_f_b, w_down=w_down, norm_final_g=norm_final_g, loss_target=loss_target, m_norm_mix_g=m_norm_mix_g, m_w_in=m_w_in, m_conv_a_w=m_conv_a_w, m_conv_a_b=m_conv_a_b, m_ln_a_g=m_ln_a_g, m_ln_a_b=m_ln_a_b, m_pool_w=m_pool_w, m_pool_scale=m_pool_scale, m_w_out=m_w_out, m_norm_ffn_g=m_norm_ffn_g, m_w_up=m_w_up, m_conv_f_w=m_conv_f_w, m_conv_f_b=m_conv_f_b, m_w_down=m_w_down, m_norm_final_g=m_norm_final_g, v_norm_mix_g=v_norm_mix_g, v_w_in=v_w_in, v_conv_a_w=v_conv_a_w, v_conv_a_b=v_conv_a_b, v_ln_a_g=v_ln_a_g, v_ln_a_b=v_ln_a_b, v_pool_w=v_pool_w, v_pool_scale=v_pool_scale, v_w_out=v_w_out, v_norm_ffn_g=v_norm_ffn_g, v_w_up=v_w_up, v_conv_f_w=v_conv_f_w, v_conv_f_b=v_conv_f_b, v_w_down=v_w_down, v_norm_final_g=v_norm_final_g)
    weights = {n: given[n] for n in TWIN_WEIGHTS}
    shared = {n: given[n] for n in SHARED_INPUTS}
    per_example = {n: given[n] for n in ['x']}
    grad_fn = _jax.value_and_grad(_loss, argnums=(0, 1))

    def one_microbatch(ex, loss_target):
        ex = dict(ex)
        diff = ex.pop(TWIN_DIFF_INPUT)
        return grad_fn(weights, diff, {**shared, **ex}, loss_target)

    if N_MICROBATCH == 1:
        loss, (grad_w, grad_x) = one_microbatch(per_example, given["loss_target"])
    else:
        def body(carry, xs):
            loss_sum, grad_sum = carry
            l_k, (gw_k, gx_k) = one_microbatch(xs[0], xs[1])
            with _jax.named_scope("update"):
                return (loss_sum + l_k, _jax.tree.map(_jnp.add, grad_sum, gw_k)), gx_k

        init = (_jnp.zeros((), _jnp.float32), _jax.tree.map(_jnp.zeros_like, weights))
        (loss, grad_w), grad_x = _jax.lax.scan(body, init, (per_example, given["loss_target"]))
    with _jax.named_scope("update"):
        delta_w, new_m, new_v = {}, {}, {}
        for n in TWIN_WEIGHTS:
            delta_w[n], new_m[n], new_v[n] = _adamw(weights[n], grad_w[n], given["m_" + n], given["v_" + n])
    return (loss, grad_x, *[grad_w[n] for n in TWIN_WEIGHTS], *[delta_w[n] for n in TWIN_WEIGHTS],
            *[new_m[n] for n in TWIN_WEIGHTS], *[new_v[n] for n in TWIN_WEIGHTS])
```

```python
import functools

import jax
import jax.numpy as jnp
from jax import lax
from jax.experimental import pallas as pl
from jax.experimental.pallas import tpu as pltpu

F32 = jnp.float32
BF16 = jnp.bfloat16
EPS = 1e-6
ADAM_LR = 0.001
ADAM_B1 = 0.9
ADAM_B2 = 0.999
ADAM_EPS = 1e-08
ADAM_WD = 0.01
ADAM_STEP = 10

D_MODEL = 1024
D_CONV = 512
D_POOL = 512
D_IN = 1536
D_FF = 2816
CONV_A = 31
CONV_F = 3
POOL_WINDOWS = (2, 4, 8, 16)
POOL_GROUP = 128
N_CHIPS = 4
FF_CHUNK = 256
N_FF_CHUNKS = D_FF // FF_CHUNK
A_HALO = 32
F_HALO = 16
P_HALO = 16
VMEM_LIMIT = 56 * 1024 * 1024
MESH = pl.DeviceIdType.MESH

ANY = pl.BlockSpec(memory_space=pl.ANY)
VMEM = pl.BlockSpec(memory_space=pltpu.VMEM)


def _dot(a, b):
    return jnp.dot(a, b, preferred_element_type=F32)


def _dot_nt(a, b):
    return lax.dot_general(a, b, (((1,), (1,)), ((), ())), preferred_element_type=F32)


def _dot_tn(a, b):
    return lax.dot_general(a, b, (((0,), (0,)), ((), ())), preferred_element_type=F32)


def _sigmoid(v):
    return jax.nn.sigmoid(v)


def _colsum(v):
    return jnp.sum(v, axis=0, keepdims=True)


def _rowmean(v):
    return jnp.mean(v, axis=-1, keepdims=True)


def _place():
    x, y, c = lax.axis_index("x"), lax.axis_index("y"), lax.axis_index("c")
    chips = [(1 - x, y), (x, 1 - y), (1 - x, 1 - y)]
    return x, y, c, 2 * x + y, chips


def _gather_weights(win_s, wout_s, wup_s, wdown_s, wa_s, wf_s):
    shards = (win_s, wout_s, wup_s, wdown_s)
    col_sharded = (True, False, True, False)
    n_big = len(shards)

    def body(win_r, wout_r, wup_r, wdown_r, wa_r, wf_r, win_f, wout_f, wup_f, wdown_f, wa_g, wf_g,
             b0, b1, b2, b3, ici_send, ici_recv, fwd_send, fwd_recv, cv_send, cv_recv, loc_sem):
        x, y, c, k, chips = _place()
        srcs = (win_r, wout_r, wup_r, wdown_r)
        bufs = (b0, b1, b2, b3)
        fulls = (win_f, wout_f, wup_f, wdown_f)
        for s, b in zip(srcs, bufs):
            b[...] = s[...].astype(BF16)

        def block(i, kk, half=None):
            rows, cols = shards[i].shape
            if col_sharded[i]:
                rs = slice(None) if half is None else pl.ds(pl.multiple_of(half * (rows // 2), 16), rows // 2)
                return fulls[i].at[rs, pl.ds(pl.multiple_of(kk * cols, 128), cols)]
            if half is None:
                return fulls[i].at[pl.ds(pl.multiple_of(kk * rows, 16), rows), :]
            return fulls[i].at[pl.ds(pl.multiple_of(kk * rows + half * (rows // 2), 16), rows // 2), :]

        def my_half(i):
            rows = shards[i].shape[0]
            return bufs[i].at[pl.ds(pl.multiple_of(c * (rows // 2), 16), rows // 2), :]

        local = [pltpu.make_async_copy(bufs[i], block(i, k), loc_sem.at[i]) for i in range(n_big)]
        local.append(pltpu.make_async_copy(wa_r, wa_g.at[k], loc_sem.at[n_big]))
        local.append(pltpu.make_async_copy(wf_r, wf_g.at[k], loc_sem.at[n_big + 1]))
        for cp in local:
            cp.start()

        def ici(i, j, kk, src):
            return pltpu.make_async_remote_copy(
                src_ref=src, dst_ref=block(i, kk, c), send_sem=ici_send.at[i * 3 + j], recv_sem=ici_recv.at[i * 3 + j],
                device_id=(*chips[j], c), device_id_type=MESH)

        def fwd(i, j, kk, half):
            return pltpu.make_async_remote_copy(
                src_ref=block(i, kk, half), dst_ref=block(i, kk, half), send_sem=fwd_send.at[i * 3 + j],
                recv_sem=fwd_recv.at[i * 3 + j], device_id=(x, y, 1 - c), device_id_type=MESH)

        def conv(t, j, kk, src, dst):
            return pltpu.make_async_remote_copy(
                src_ref=src, dst_ref=dst.at[kk], send_sem=cv_send.at[t * 3 + j], recv_sem=cv_recv.at[t * 3 + j],
                device_id=(*chips[j], c), device_id_type=MESH)

        sends = [ici(i, j, k, my_half(i)) for i in range(n_big) for j in range(3)]
        sends += [conv(t, j, k, src, dst) for t, (src, dst) in enumerate(((wa_r, wa_g), (wf_r, wf_g))) for j in range(3)]
        for cp in sends:
            cp.start()
        passed = []
        for i in range(n_big):
            for j, (qx, qy) in enumerate(chips):
                kq = 2 * qx + qy
                ici(i, j, kq, my_half(i)).wait_recv()
                cp = fwd(i, j, kq, c)
                cp.start()
                passed.append(cp)
        for i in range(n_big):
            for j, (qx, qy) in enumerate(chips):
                fwd(i, j, 2 * qx + qy, 1 - c).wait_recv()
        for t, (src, dst) in enumerate(((wa_r, wa_g), (wf_r, wf_g))):
            for j, (qx, qy) in enumerate(chips):
                conv(t, j, 2 * qx + qy, src, dst).wait_recv()
        for cp in sends + passed:
            cp.wait_send()
        for cp in local:
            cp.wait()

    s_len = D_MODEL
    out_shape = (
        jax.ShapeDtypeStruct((s_len, D_IN), BF16),
        jax.ShapeDtypeStruct((D_MODEL, D_MODEL), BF16),
        jax.ShapeDtypeStruct((s_len, 2 * D_FF), BF16),
        jax.ShapeDtypeStruct((D_FF, D_MODEL), BF16),
        jax.ShapeDtypeStruct((N_CHIPS,) + wa_s.shape, F32),
        jax.ShapeDtypeStruct((N_CHIPS,) + wf_s.shape, F32),
    )
    return pl.pallas_call(
        body, name="gather_weights", out_shape=out_shape,
        in_specs=[VMEM] * 6, out_specs=[ANY] * 6,
        scratch_shapes=[pltpu.VMEM(s.shape, BF16) for s in shards] + [
            pltpu.SemaphoreType.DMA((12,)), pltpu.SemaphoreType.DMA((12,)),
            pltpu.SemaphoreType.DMA((12,)), pltpu.SemaphoreType.DMA((12,)),
            pltpu.SemaphoreType.DMA((6,)), pltpu.SemaphoreType.DMA((6,)),
            pltpu.SemaphoreType.DMA((6,)),
        ],
        compiler_params=pltpu.CompilerParams(vmem_limit_bytes=VMEM_LIMIT),
    )(win_s, wout_s, wup_s, wdown_s, wa_s, wf_s)


def _load_weights(pairs, sem):
    cps = [pltpu.make_async_copy(src, dst, sem.at[i]) for i, (src, dst) in enumerate(pairs)]
    for cp in cps:
        cp.start()
    for cp in cps:
        cp.wait()


def _pool_count(tile, t_rows, w):
    row = lax.broadcasted_iota(jnp.int32, (t_rows, POOL_GROUP), 0) + tile * t_rows
    return jnp.minimum(row + 1, w).astype(F32)


def _mixer_fwd(x, g1, win, wa, cb, lg, lb, pw, ps, wout, tile_rows):
    seq = x.shape[0]
    tr = tile_rows
    n = seq // tr

    def body(x_ref, g1_ref, win_hbm, wa_ref, cb_ref, lg_ref, lb_ref, pw_ref, ps_ref, wout_hbm,
             h1_ref, proj_ref, c_ref, d_ref, m_ref, x1_ref, win_v, wout_v, ubuf, bbuf, sem):
        i = pl.program_id(0)

        @pl.when(i == 0)
        def _():
            _load_weights(((win_hbm, win_v), (wout_hbm, wout_v)), sem)
            ubuf[0:A_HALO, :] = jnp.zeros((A_HALO, D_CONV), F32)
            bbuf[0:P_HALO, :] = jnp.zeros((P_HALO, D_POOL), F32)

        xv = x_ref[...]
        r = lax.rsqrt(_rowmean(xv * xv) + EPS)
        h1 = (xv * r * g1_ref[...]).astype(BF16)
        h1_ref[...] = h1
        proj = _dot(h1, win_v[...])
        proj_ref[...] = proj.astype(BF16)
        av, ag, bi = proj[:, :D_CONV], proj[:, D_CONV:2 * D_CONV], proj[:, 2 * D_CONV:]
        ubuf[A_HALO:A_HALO + tr, :] = av * _sigmoid(ag)
        off = A_HALO - (CONV_A - 1)
        acc = wa_ref[0:1, :] * ubuf[off:off + tr, :]
        for j in range(1, CONV_A):
            acc = acc + wa_ref[j:j + 1, :] * ubuf[off + j:off + j + tr, :]
        cv = acc + cb_ref[...]
        ubuf[0:A_HALO, :] = ubuf[tr:tr + A_HALO, :]
        c_ref[...] = cv.astype(BF16)
        xc = cv - _rowmean(cv)
        z = xc * lax.rsqrt(_rowmean(xc * xc) + EPS)
        ln = z * lg_ref[...] + lb_ref[...]
        ya = ln * _sigmoid(ln)
        bbuf[P_HALO:P_HALO + tr, :] = bi
        ds, ybs = [], []
        for g, w in enumerate(POOL_WINDOWS):
            cols = slice(g * POOL_GROUP, (g + 1) * POOL_GROUP)
            s = bi[:, cols]
            for kk in range(1, w):
                s = s + bbuf[P_HALO - kk:P_HALO - kk + tr, cols]
            dg = s / _pool_count(i, tr, w) - bi[:, cols]
            ds.append(dg)
            ybs.append(_dot(dg.astype(BF16), pw_ref[g].astype(BF16)))
        bbuf[0:P_HALO, :] = bbuf[tr:tr + P_HALO, :]
        d_ref[...] = jnp.concatenate(ds, axis=1).astype(BF16)
        yb = jnp.concatenate(ybs, axis=1) * ps_ref[...]
        m = jnp.concatenate([ya, yb], axis=1).astype(BF16)
        m_ref[...] = m
        x1_ref[...] = xv + _dot(m, wout_v[...])

    tile = lambda w: pl.BlockSpec((tr, w), lambda i: (i, 0))
    full = lambda a: pl.BlockSpec(a.shape, lambda i: (0,) * a.ndim)
    return pl.pallas_call(
        body, name="mixer_fwd", grid=(n,),
        in_specs=[tile(D_MODEL), full(g1), ANY, full(wa), full(cb), full(lg), full(lb), full(pw), full(ps), ANY],
        out_specs=[tile(D_MODEL), tile(D_IN), tile(D_CONV), tile(D_POOL), tile(D_MODEL), tile(D_MODEL)],
        out_shape=[
            jax.ShapeDtypeStruct((seq, D_MODEL), BF16), jax.ShapeDtypeStruct((seq, D_IN), BF16),
            jax.ShapeDtypeStruct((seq, D_CONV), BF16), jax.ShapeDtypeStruct((seq, D_POOL), BF16),
            jax.ShapeDtypeStruct((seq, D_MODEL), BF16), jax.ShapeDtypeStruct((seq, D_MODEL), F32),
        ],
        scratch_shapes=[
            pltpu.VMEM(win.shape, BF16), pltpu.VMEM(wout.shape, BF16),
            pltpu.VMEM((tr + A_HALO, D_CONV), F32), pltpu.VMEM((tr + P_HALO, D_POOL), F32),
            pltpu.SemaphoreType.DMA((2,)),
        ],
        compiler_params=pltpu.CompilerParams(dimension_semantics=("arbitrary",), vmem_limit_bytes=VMEM_LIMIT),
    )(x, g1, win, wa, cb, lg, lb, pw, ps, wout)


def _ffn_fwd(x1, g2, wup, wf, fb, wdown, g3, target, tile_rows):
    seq = x1.shape[0]
    tr = tile_rows
    n = seq // tr

    def body(x1_ref, g2_ref, wup_hbm, wf_ref, fb_ref, wdown_hbm, g3_ref, t_ref,
             h2_ref, up_ref, act_ref, dx2_ref, sm_ref, wup_v, wdown_v, gbuf, sem):
        i = pl.program_id(0)

        @pl.when(i == 0)
        def _():
            _load_weights(((wup_hbm, wup_v), (wdown_hbm, wdown_v)), sem)
            gbuf[0:8, :] = jnp.zeros((8, D_FF), F32)
            sm_ref[...] = jnp.zeros(sm_ref.shape, F32)

        x1v = x1_ref[...]
        r2 = lax.rsqrt(_rowmean(x1v * x1v) + EPS)
        h2 = (x1v * r2 * g2_ref[...]).astype(BF16)
        h2_ref[...] = h2
        x2 = x1v
        for j in range(N_FF_CHUNKS):
            cs = slice(j * FF_CHUNK, (j + 1) * FF_CHUNK)
            vs = slice(D_FF + j * FF_CHUNK, D_FF + (j + 1) * FF_CHUNK)
            gate = _dot(h2, wup_v[:, cs])
            val = _dot(h2, wup_v[:, vs])
            up_ref[:, cs] = gate.astype(BF16)
            up_ref[:, vs] = val.astype(BF16)
            gbuf[8:8 + tr, cs] = gate
            gc = (wf_ref[0:1, cs] * gbuf[6:6 + tr, cs] + wf_ref[1:2, cs] * gbuf[7:7 + tr, cs]
                  + wf_ref[2:3, cs] * gate + fb_ref[:, cs])
            gbuf[0:8, cs] = gbuf[tr:tr + 8, cs]
            act = (gc * _sigmoid(gc) * val).astype(BF16)
            act_ref[:, cs] = act
            x2 = x2 + _dot(act, wdown_v[cs, :])
        r3 = lax.rsqrt(_rowmean(x2 * x2) + EPS)
        n3 = x2 * r3
        err = n3 * g3_ref[...] - t_ref[...]
        dy = err / D_MODEL
        sm_ref[2:3, :] += _colsum(dy * n3)
        loss = 0.5 * _colsum(_rowmean(err * err))
        sm_ref[3:4, :] += jnp.broadcast_to(loss, (1, D_MODEL))
        dn = dy * g3_ref[...]
        dx2_ref[...] = r3 * (dn - n3 * _rowmean(dn * n3))

    tile = lambda w: pl.BlockSpec((tr, w), lambda i: (i, 0))
    full = lambda a: pl.BlockSpec(a.shape, lambda i: (0,) * a.ndim)
    return pl.pallas_call(
        body, name="ffn_fwd", grid=(n,),
        in_specs=[tile(D_MODEL), full(g2), ANY, full(wf), full(fb), ANY, full(g3), tile(D_MODEL)],
        out_specs=[tile(D_MODEL), tile(2 * D_FF), tile(D_FF), tile(D_MODEL), pl.BlockSpec((8, D_MODEL), lambda i: (0, 0))],
        out_shape=[
            jax.ShapeDtypeStruct((seq, D_MODEL), BF16), jax.ShapeDtypeStruct((seq, 2 * D_FF), BF16),
            jax.ShapeDtypeStruct((seq, D_FF), BF16), jax.ShapeDtypeStruct((seq, D_MODEL), F32),
            jax.ShapeDtypeStruct((8, D_MODEL), F32),
        ],
        scratch_shapes=[
            pltpu.VMEM(wup.shape, BF16), pltpu.VMEM(wdown.shape, BF16),
            pltpu.VMEM((tr + 8, D_FF), F32), pltpu.SemaphoreType.DMA((2,)),
        ],
        compiler_params=pltpu.CompilerParams(dimension_semantics=("arbitrary",), vmem_limit_bytes=VMEM_LIMIT),
    )(x1, g2, wup, wf, fb, wdown, g3, target)


def _ffn_bwd(dx2, up, x1, g2, wup, wf, fb, wdown, tile_rows):
    seq = x1.shape[0]
    tr = tile_rows
    n = seq // tr

    def body(dx2_ref, up_ref, uph_ref, x1_ref, g2_ref, wup_hbm, wf_ref, fb_ref, wdown_hbm,
             dup_ref, dx1_ref, sm_ref, sf_ref, wup_v, wdown_v, gbuf, dbuf, dcar, sem):
        i = pl.program_id(0)
        tile = n - 1 - i

        @pl.when(i == 0)
        def _():
            _load_weights(((wup_hbm, wup_v), (wdown_hbm, wdown_v)), sem)
            dcar[...] = jnp.zeros(dcar.shape, F32)
            sm_ref[...] = jnp.zeros(sm_ref.shape, F32)
            sf_ref[...] = jnp.zeros(sf_ref.shape, F32)

        dx2v = dx2_ref[...]
        dx2b = dx2v.astype(BF16)
        keep = (tile > 0).astype(F32)
        dh2 = jnp.zeros((tr, D_MODEL), F32)
        for j in range(N_FF_CHUNKS):
            cs = slice(j * FF_CHUNK, (j + 1) * FF_CHUNK)
            vs = slice(D_FF + j * FF_CHUNK, D_FF + (j + 1) * FF_CHUNK)
            dact = _dot_nt(dx2b, wdown_v[cs, :])
            gate = up_ref[:, cs].astype(F32)
            val = up_ref[:, vs].astype(F32)
            gbuf[0:F_HALO, :] = uph_ref[:, cs].astype(F32) * keep
            gbuf[F_HALO:F_HALO + tr, :] = gate
            g_m2 = gbuf[F_HALO - 2:F_HALO - 2 + tr, :]
            g_m1 = gbuf[F_HALO - 1:F_HALO - 1 + tr, :]
            gc = wf_ref[0:1, cs] * g_m2 + wf_ref[1:2, cs] * g_m1 + wf_ref[2:3, cs] * gate + fb_ref[:, cs]
            sg = _sigmoid(gc)
            dval = dact * (gc * sg)
            dgc = dact * val * (sg * (1.0 + gc * (1.0 - sg)))
            dbuf[0:tr, :] = dgc
            dbuf[tr:tr + 8, :] = dcar[:, cs]
            dgate = wf_ref[2:3, cs] * dgc + wf_ref[1:2, cs] * dbuf[1:1 + tr, :] + wf_ref[0:1, cs] * dbuf[2:2 + tr, :]
            dcar[:, cs] = dgc[0:8, :]
            sf_ref[0:1, cs] += _colsum(dgc * g_m2)
            sf_ref[1:2, cs] += _colsum(dgc * g_m1)
            sf_ref[2:3, cs] += _colsum(dgc * gate)
            sf_ref[3:4, cs] += _colsum(dgc)
            dgb, dvb = dgate.astype(BF16), dval.astype(BF16)
            dup_ref[:, cs] = dgb
            dup_ref[:, vs] = dvb
            dh2 = dh2 + _dot_nt(dgb, wup_v[:, cs]) + _dot_nt(dvb, wup_v[:, vs])
        x1v = x1_ref[...]
        r2 = lax.rsqrt(_rowmean(x1v * x1v) + EPS)
        n2 = x1v * r2
        sm_ref[1:2, :] += _colsum(dh2 * n2)
        dn2 = dh2 * g2_ref[...]
        dx1_ref[...] = dx2v + r2 * (dn2 - n2 * _rowmean(dn2 * n2))

    tile = lambda w: pl.BlockSpec((tr, w), lambda i: (n - 1 - i, 0))
    full = lambda a: pl.BlockSpec(a.shape, lambda i: (0,) * a.ndim)
    halo = pl.BlockSpec((F_HALO, D_FF), lambda i: (jnp.maximum((n - 1 - i) * (tr // F_HALO) - 1, 0), 0))
    acc = lambda rows, w: pl.BlockSpec((rows, w), lambda i: (0, 0))
    return pl.pallas_call(
        body, name="ffn_bwd", grid=(n,),
        in_specs=[tile(D_MODEL), tile(2 * D_FF), halo, tile(D_MODEL), full(g2), ANY, full(wf), full(fb), ANY],
        out_specs=[tile(2 * D_FF), tile(D_MODEL), acc(8, D_MODEL), acc(8, D_FF)],
        out_shape=[
            jax.ShapeDtypeStruct((seq, 2 * D_FF), BF16), jax.ShapeDtypeStruct((seq, D_MODEL), F32),
            jax.ShapeDtypeStruct((8, D_MODEL), F32), jax.ShapeDtypeStruct((8, D_FF), F32),
        ],
        scratch_shapes=[
            pltpu.VMEM(wup.shape, BF16), pltpu.VMEM(wdown.shape, BF16),
            pltpu.VMEM((tr + F_HALO, FF_CHUNK), F32), pltpu.VMEM((tr + 8, FF_CHUNK), F32),
            pltpu.VMEM((8, D_FF), F32), pltpu.SemaphoreType.DMA((2,)),
        ],
        compiler_params=pltpu.CompilerParams(dimension_semantics=("arbitrary",), vmem_limit_bytes=VMEM_LIMIT),
    )(dx2, up, up, x1, g2, wup, wf, fb, wdown)


def _mixer_bwd(dx1, x, proj, cpre, d, g1, win, wa, lg, lb, pw, ps, wout, tile_rows):
    seq = x.shape[0]
    tr = tile_rows
    n = seq // tr
    row_cb, row_lg, row_lb, row_ps = 32, 33, 34, 35

    def body(dx1_ref, x_ref, proj_ref, projh_ref, c_ref, d_ref, g1_ref, win_hbm, wa_ref, lg_ref, lb_ref, pw_ref, ps_ref,
             wout_hbm, dproj_ref, gx_ref, sm_ref, s5_ref, sp_ref, win_v, wout_v, ubuf, dcbuf, ebuf, sem):
        i = pl.program_id(0)
        tile = n - 1 - i

        @pl.when(i == 0)
        def _():
            _load_weights(((win_hbm, win_v), (wout_hbm, wout_v)), sem)
            dcbuf[tr:tr + A_HALO, :] = jnp.zeros((A_HALO, D_CONV), F32)
            ebuf[tr:tr + P_HALO, :] = jnp.zeros((P_HALO, D_POOL), F32)
            sm_ref[...] = jnp.zeros(sm_ref.shape, F32)
            s5_ref[...] = jnp.zeros(s5_ref.shape, F32)
            sp_ref[...] = jnp.zeros(sp_ref.shape, F32)

        dx1v = dx1_ref[...]
        dm = _dot_nt(dx1v.astype(BF16), wout_v[...])
        dya, dyb = dm[:, :D_CONV], dm[:, D_CONV:]
        dbis = []
        for g, w in enumerate(POOL_WINDOWS):
            cols = slice(g * POOL_GROUP, (g + 1) * POOL_GROUP)
            dgb = d_ref[:, cols]
            pwb = pw_ref[g].astype(BF16)
            dyg = dyb[:, cols]
            s5_ref[row_ps:row_ps + 1, cols] += _colsum(dyg * _dot(dgb, pwb))
            dqb = (dyg * ps_ref[:, cols]).astype(BF16)
            sp_ref[g] += _dot_tn(dgb, dqb)
            dd = _dot_nt(dqb, pwb)
            e = dd / _pool_count(tile, tr, w)
            ebuf[0:tr, cols] = e
            s = e
            for kk in range(1, w):
                s = s + ebuf[kk:kk + tr, cols]
            dbis.append(s - dd)
        ebuf[tr:tr + P_HALO, :] = ebuf[0:P_HALO, :]
        cv = c_ref[...].astype(F32)
        xc = cv - _rowmean(cv)
        rs = lax.rsqrt(_rowmean(xc * xc) + EPS)
        z = xc * rs
        ln = z * lg_ref[...] + lb_ref[...]
        sl = _sigmoid(ln)
        dl = dya * (sl * (1.0 + ln * (1.0 - sl)))
        s5_ref[row_lg:row_lg + 1, :] += _colsum(dl * z)
        s5_ref[row_lb:row_lb + 1, :] += _colsum(dl)
        dz = dl * lg_ref[...]
        dc = rs * (dz - _rowmean(dz) - z * _rowmean(dz * z))
        s5_ref[row_cb:row_cb + 1, :] += _colsum(dc)
        dcbuf[0:tr, :] = dc
        keep = (tile > 0).astype(F32)
        avh = projh_ref[:, :D_CONV].astype(F32)
        agh = projh_ref[:, D_CONV:].astype(F32)
        ubuf[0:A_HALO, :] = avh * _sigmoid(agh) * keep
        av = proj_ref[:, :D_CONV].astype(F32)
        ag = proj_ref[:, D_CONV:2 * D_CONV].astype(F32)
        sg = _sigmoid(ag)
        ubuf[A_HALO:A_HALO + tr, :] = av * sg
        off = A_HALO - (CONV_A - 1)
        du = wa_ref[CONV_A - 1:CONV_A, :] * dc
        for j in range(CONV_A - 1):
            du = du + wa_ref[j:j + 1, :] * dcbuf[CONV_A - 1 - j:CONV_A - 1 - j + tr, :]
        for j in range(CONV_A):
            s5_ref[j:j + 1, :] += _colsum(dc * ubuf[off + j:off + j + tr, :])
        dcbuf[tr:tr + A_HALO, :] = dcbuf[0:A_HALO, :]
        dav = du * sg
        dag = du * av * (sg * (1.0 - sg))
        dprojb = jnp.concatenate([dav, dag] + dbis, axis=1).astype(BF16)
        dproj_ref[...] = dprojb
        dh1 = _dot_nt(dprojb, win_v[...])
        xv = x_ref[...]
        r1 = lax.rsqrt(_rowmean(xv * xv) + EPS)
        n1 = xv * r1
        sm_ref[0:1, :] += _colsum(dh1 * n1)
        dn1 = dh1 * g1_ref[...]
        gx_ref[...] = dx1v + r1 * (dn1 - n1 * _rowmean(dn1 * n1))

    tile = lambda w: pl.BlockSpec((tr, w), lambda i: (n - 1 - i, 0))
    full = lambda a: pl.BlockSpec(a.shape, lambda i: (0,) * a.ndim)
    halo = pl.BlockSpec((A_HALO, 2 * D_CONV), lambda i: (jnp.maximum((n - 1 - i) * (tr // A_HALO) - 1, 0), 0))
    acc = lambda shape: pl.BlockSpec(shape, lambda i: (0,) * len(shape))
    return pl.pallas_call(
        body, name="mixer_bwd", grid=(n,),
        in_specs=[tile(D_MODEL), tile(D_MODEL), tile(D_IN), halo, tile(D_CONV), tile(D_POOL), full(g1), ANY, full(wa),
                  full(lg), full(lb), full(pw), full(ps), ANY],
        out_specs=[tile(D_IN), tile(D_MODEL), acc((8, D_MODEL)), acc((40, D_CONV)), acc(pw.shape)],
        out_shape=[
            jax.ShapeDtypeStruct((seq, D_IN), BF16), jax.ShapeDtypeStruct((seq, D_MODEL), F32),
            jax.ShapeDtypeStruct((8, D_MODEL), F32), jax.ShapeDtypeStruct((40, D_CONV), F32),
            jax.ShapeDtypeStruct(pw.shape, F32),
        ],
        scratch_shapes=[
            pltpu.VMEM(win.shape, BF16), pltpu.VMEM(wout.shape, BF16),
            pltpu.VMEM((tr + A_HALO, D_CONV), F32), pltpu.VMEM((tr + A_HALO, D_CONV), F32),
            pltpu.VMEM((tr + P_HALO, D_POOL), F32), pltpu.SemaphoreType.DMA((2,)),
        ],
        compiler_params=pltpu.CompilerParams(dimension_semantics=("arbitrary",), vmem_limit_bytes=VMEM_LIMIT),
    )(dx1, x, proj, proj, cpre, d, g1, win, wa, lg, lb, pw, ps, wout)


def _weight_grad(a, b, row_sharded, k_rows):
    seq, m_dim = a.shape
    n_dim = b.shape[1]
    steps = seq // k_rows
    if row_sharded:
        groups = 2
        rows = m_dim // N_CHIPS // 2
        a_w = m_dim // groups

        def body(a_ref, b_ref, o_ref, acc):
            s = pl.program_id(1)

            @pl.when(s == 0)
            def _():
                acc[...] = jnp.zeros(acc.shape, F32)

            acc[...] += _dot_tn(a_ref[...], b_ref[...].astype(BF16))

            @pl.when(s == steps - 1)
            def _():
                for p in range(N_CHIPS // groups):
                    for h in range(2):
                        o_ref[p, h] = acc[(2 * p + h) * rows:(2 * p + h + 1) * rows, :]

        return pl.pallas_call(
            body, name=f"weight_grad_rows_{m_dim}", grid=(groups, steps),
            in_specs=[pl.BlockSpec((k_rows, a_w), lambda g, s: (s, g)), pl.BlockSpec((k_rows, n_dim), lambda g, s: (s, 0))],
            out_specs=pl.BlockSpec((N_CHIPS // groups, 2, rows, n_dim), lambda g, s: (g, 0, 0, 0)),
            out_shape=jax.ShapeDtypeStruct((N_CHIPS, 2, rows, n_dim), F32),
            scratch_shapes=[pltpu.VMEM((a_w, n_dim), F32)],
            compiler_params=pltpu.CompilerParams(dimension_semantics=("arbitrary", "arbitrary"), vmem_limit_bytes=VMEM_LIMIT),
        )(a, b)

    rows = m_dim // 2
    cols = n_dim // N_CHIPS

    def body(a_ref, b_ref, o_ref):
        s = pl.program_id(2)

        @pl.when(s == 0)
        def _():
            o_ref[...] = jnp.zeros(o_ref.shape, F32)

        o_ref[...] += _dot_tn(a_ref[...], b_ref[...])

    return pl.pallas_call(
        body, name=f"weight_grad_cols_{n_dim}", grid=(2, N_CHIPS, steps),
        in_specs=[pl.BlockSpec((k_rows, rows), lambda h, k, s: (s, h)), pl.BlockSpec((k_rows, cols), lambda h, k, s: (s, k))],
        out_specs=pl.BlockSpec((None, None, rows, cols), lambda h, k, s: (k, h, 0, 0)),
        out_shape=jax.ShapeDtypeStruct((N_CHIPS, 2, rows, cols), F32),
        compiler_params=pltpu.CompilerParams(dimension_semantics=("arbitrary", "arbitrary", "arbitrary"),
                                             vmem_limit_bytes=VMEM_LIMIT),
    )(a, b)


def _sibling_exchange(bigs, smalls):
    nb, ns = len(bigs), len(smalls)

    def body(*refs):
        ins, outs = refs[:nb + ns], refs[nb + ns:2 * (nb + ns)]
        send, recv = refs[2 * (nb + ns):]
        x, y, c, _, _ = _place()
        cps = []
        for t in range(nb + ns):
            src = ins[t].at[:, 1 - c] if t < nb else ins[t]
            cps.append(pltpu.make_async_remote_copy(
                src_ref=src, dst_ref=outs[t], send_sem=send.at[t], recv_sem=recv.at[t],
                device_id=(x, y, 1 - c), device_id_type=MESH))
        for cp in cps:
            cp.start()
        for cp in cps:
            cp.wait()

    out_shape = [jax.ShapeDtypeStruct((N_CHIPS,) + b.shape[2:], F32) for b in bigs]
    out_shape += [jax.ShapeDtypeStruct(s.shape, F32) for s in smalls]
    return pl.pallas_call(
        body, name="sibling_exchange", out_shape=out_shape,
        in_specs=[ANY] * (nb + ns), out_specs=[ANY] * (nb + ns),
        scratch_shapes=[pltpu.SemaphoreType.DMA((nb + ns,)), pltpu.SemaphoreType.DMA((nb + ns,))],
    )(*bigs, *smalls)


def _pair_sum(core, mine, theirs, tag, block_rows):
    _, _, rows, cols = mine.shape
    steps = rows // block_rows

    def body(core_ref, a_ref, b_ref, o_ref):
        o_ref[...] = (a_ref[...] + b_ref[...]).astype(BF16)

    grid_spec = pltpu.PrefetchScalarGridSpec(
        num_scalar_prefetch=1, grid=(N_CHIPS, steps),
        in_specs=[pl.BlockSpec((None, None, block_rows, cols), lambda k, r, core_ref: (k, core_ref[0], r, 0)),
                  pl.BlockSpec((None, block_rows, cols), lambda k, r, core_ref: (k, r, 0))],
        out_specs=pl.BlockSpec((None, block_rows, cols), lambda k, r, core_ref: (k, r, 0)),
    )
    return pl.pallas_call(
        body, name=f"pair_sum_{tag}", grid_spec=grid_spec,
        out_shape=jax.ShapeDtypeStruct((N_CHIPS, rows, cols), BF16),
        compiler_params=pltpu.CompilerParams(dimension_semantics=("arbitrary", "arbitrary"), vmem_limit_bytes=VMEM_LIMIT),
    )(core, mine, theirs)


def _pair_sum_small(mine, theirs):
    (m_f2, m_b1, m_b2, m_sf, m_s5, m_sp) = mine

    def body(a0, a1, a2, a3, a4, a5, b0, b1, b2, b3, b4, b5, o_m, o_f, o_5, o_p):
        sm = (a0[...] + a1[...] + a2[...]) + (b0[...] + b1[...] + b2[...])
        sf = a3[...] + b3[...]
        s5 = a4[...] + b4[...]
        for h in range(2):
            o_m[h] = sm[:, h * (D_MODEL // 2):(h + 1) * (D_MODEL // 2)]
            o_f[h] = sf[:, h * (D_FF // 2):(h + 1) * (D_FF // 2)]
            o_5[h] = s5[:, h * (D_CONV // 2):(h + 1) * (D_CONV // 2)]
            for g in range(2):
                o_p[h, g] = a5[2 * h + g] + b5[2 * h + g]

    out_shape = [
        jax.ShapeDtypeStruct((2, 8, D_MODEL // 2), F32), jax.ShapeDtypeStruct((2, 8, D_FF // 2), F32),
        jax.ShapeDtypeStruct((2, 40, D_CONV // 2), F32), jax.ShapeDtypeStruct((2, 2, POOL_GROUP, POOL_GROUP), F32),
    ]
    return pl.pallas_call(body, name="pair_sum_small", out_shape=out_shape, in_specs=[VMEM] * 12, out_specs=[VMEM] * 4)(
        *mine, *theirs)


def _chip_scatter(parts, smalls):
    nt = len(parts) + len(smalls)
    nb = len(parts)

    def body(*refs):
        ins, outs = refs[:nt], refs[nt:2 * nt]
        ici_send, ici_recv, fwd_send, fwd_recv, loc_sem = refs[2 * nt:]
        x, y, c, k, chips = _place()

        def src_of(t, kk):
            return ins[t].at[kk] if t < nb else ins[t].at[c]

        def ici(t, j, kk):
            return pltpu.make_async_remote_copy(
                src_ref=src_of(t, kk), dst_ref=outs[t].at[k, c], send_sem=ici_send.at[t * 3 + j],
                recv_sem=ici_recv.at[t * 3 + j], device_id=(*chips[j], c), device_id_type=MESH)

        def fwd(t, j, q, half, src=None):
            slot = outs[t].at[q, half]
            return pltpu.make_async_remote_copy(
                src_ref=slot if src is None else src, dst_ref=slot, send_sem=fwd_send.at[t * 4 + j],
                recv_sem=fwd_recv.at[t * 4 + j], device_id=(x, y, 1 - c), device_id_type=MESH)

        local = [pltpu.make_async_copy(src_of(t, k), outs[t].at[k, c], loc_sem.at[t]) for t in range(nt)]
        for cp in local:
            cp.start()
        sends = []
        for t in range(nt):
            sends.append(fwd(t, 3, k, c, src=src_of(t, k)))
            for j, (qx, qy) in enumerate(chips):
                cp = ici(t, j, 2 * qx + qy)
                sends.append(cp)
        for cp in sends:
            cp.start()
        passed = []
        for t in range(nt):
            for j, (qx, qy) in enumerate(chips):
                kq = 2 * qx + qy
                pltpu.make_async_remote_copy(
                    src_ref=src_of(t, kq), dst_ref=outs[t].at[kq, c], send_sem=ici_send.at[t * 3 + j],
                    recv_sem=ici_recv.at[t * 3 + j], device_id=(*chips[j], c), device_id_type=MESH).wait_recv()
                cp = fwd(t, j, kq, c)
                cp.start()
                passed.append(cp)
        for t in range(nt):
            fwd(t, 3, k, 1 - c).wait_recv()
            for j, (qx, qy) in enumerate(chips):
                fwd(t, j, 2 * qx + qy, 1 - c).wait_recv()
        for cp in sends + passed:
            cp.wait_send()
        for cp in local:
            cp.wait()

    out_shape = [jax.ShapeDtypeStruct((N_CHIPS, 2) + p.shape[1:], p.dtype) for p in parts]
    out_shape += [jax.ShapeDtypeStruct((N_CHIPS, 2) + s.shape[1:], s.dtype) for s in smalls]
    return pl.pallas_call(
        body, name="chip_scatter", out_shape=out_shape, in_specs=[ANY] * nt, out_specs=[ANY] * nt,
        scratch_shapes=[
            pltpu.SemaphoreType.DMA((3 * nt,)), pltpu.SemaphoreType.DMA((3 * nt,)),
            pltpu.SemaphoreType.DMA((4 * nt,)), pltpu.SemaphoreType.DMA((4 * nt,)),
            pltpu.SemaphoreType.DMA((nt,)),
        ],
    )(*parts, *smalls)


def _adamw(w, g, m, v):
    m = ADAM_B1 * m + (1.0 - ADAM_B1) * g
    v = ADAM_B2 * v + (1.0 - ADAM_B2) * (g * g)
    m_hat = m / (1.0 - ADAM_B1 ** ADAM_STEP)
    v_hat = v / (1.0 - ADAM_B2 ** ADAM_STEP)
    delta = -ADAM_LR * (m_hat / (jnp.sqrt(v_hat) + ADAM_EPS) + ADAM_WD * w)
    return delta, m, v


def _adam_big(parts, w, m, v, tag, block_rows):
    _, _, rows, cols = parts.shape
    steps = rows // block_rows

    def body(p_ref, w_ref, m_ref, v_ref, g_out, d_out, m_out, v_out):
        g = p_ref[0].astype(F32)
        for q in range(1, N_CHIPS):
            g = g + p_ref[q].astype(F32)
        delta, m_new, v_new = _adamw(w_ref[...], g, m_ref[...], v_ref[...])
        g_out[...] = g
        d_out[...] = delta
        m_out[...] = m_new
        v_out[...] = v_new

    blk = pl.BlockSpec((block_rows, cols), lambda h, r: (h * steps + r, 0))
    return pl.pallas_call(
        body, name=f"adam_{tag}", grid=(2, steps),
        in_specs=[pl.BlockSpec((N_CHIPS, None, block_rows, cols), lambda h, r: (0, h, r, 0)), blk, blk, blk],
        out_specs=[blk] * 4, out_shape=[jax.ShapeDtypeStruct(w.shape, F32)] * 4,
        compiler_params=pltpu.CompilerParams(dimension_semantics=("arbitrary", "arbitrary"), vmem_limit_bytes=VMEM_LIMIT),
    )(parts, w, m, v)


def _reduce_small(l_m, l_f, l_5, l_p):
    def total(ref):
        t = ref[0]
        for q in range(1, N_CHIPS):
            t = t + ref[q]
        return t

    def body(m_ref, f_ref, s_ref, p_ref, g1_o, g2_o, g3_o, loss_o, wf_o, fb_o, wa_o, cb_o, lg_o, lb_o, ps_o, pw_o):
        tm, tf, t5, tp = total(m_ref), total(f_ref), total(s_ref), total(p_ref)
        sm = jnp.concatenate([tm[0], tm[1]], axis=1)
        sf = jnp.concatenate([tf[0], tf[1]], axis=1)
        s5 = jnp.concatenate([t5[0], t5[1]], axis=1)
        g1_o[...] = sm[0:1]
        g2_o[...] = sm[1:2]
        g3_o[...] = sm[2:3]
        loss_o[...] = sm[3:4, 0:128]
        wf_o[...] = sf
        fb_o[...] = sf[3:4]
        wa_o[...] = s5[0:32]
        cb_o[...] = s5[32:33]
        lg_o[...] = s5[33:34]
        lb_o[...] = s5[34:35]
        ps_o[...] = s5[35:36]
        for h in range(2):
            for g in range(2):
                pw_o[2 * h + g] = tp[h, g]

    row = lambda w: jax.ShapeDtypeStruct((1, w), F32)
    out_shape = [row(D_MODEL), row(D_MODEL), row(D_MODEL), row(128), jax.ShapeDtypeStruct((8, D_FF), F32), row(D_FF),
                 jax.ShapeDtypeStruct((32, D_CONV), F32), row(D_CONV), row(D_CONV), row(D_CONV), row(D_POOL),
                 jax.ShapeDtypeStruct((4, POOL_GROUP, POOL_GROUP), F32)]
    return pl.pallas_call(body, name="reduce_small", out_shape=out_shape, in_specs=[VMEM] * 4, out_specs=[VMEM] * 12)(
        l_m, l_f, l_5, l_p)


def _adam_small(ws, gs, ms, vs):
    count = len(ws)

    def body(*refs):
        w_r, g_r, m_r, v_r = (refs[t * count:(t + 1) * count] for t in range(4))
        d_o, m_o, v_o = (refs[(4 + t) * count:(5 + t) * count] for t in range(3))
        for t in range(count):
            delta, m_new, v_new = _adamw(w_r[t][...], g_r[t][...], m_r[t][...], v_r[t][...])
            d_o[t][...] = delta
            m_o[t][...] = m_new
            v_o[t][...] = v_new

    out_shape = [jax.ShapeDtypeStruct(w.shape, F32) for w in ws] * 3
    outs = pl.pallas_call(body, name="adam_small", out_shape=out_shape, in_specs=[VMEM] * (4 * count),
                          out_specs=[VMEM] * (3 * count))(*ws, *gs, *ms, *vs)
    return outs[:count], outs[count:2 * count], outs[2 * count:]


MIX_TILE = 512
FFN_TILE = 256
GRAD_K = 512


def kernel(x, norm_mix_g, w_in, conv_a_w, conv_a_b, ln_a_g, ln_a_b, pool_w, pool_scale, w_out, norm_ffn_g, w_up, conv_f_w, conv_f_b, w_down, norm_final_g, loss_target, m_norm_mix_g, m_w_in, m_conv_a_w, m_conv_a_b, m_ln_a_g, m_ln_a_b, m_pool_w, m_pool_scale, m_w_out, m_norm_ffn_g, m_w_up, m_conv_f_w, m_conv_f_b, m_w_down, m_norm_final_g, v_norm_mix_g, v_w_in, v_conv_a_w, v_conv_a_b, v_ln_a_g, v_ln_a_b, v_pool_w, v_pool_scale, v_w_out, v_norm_ffn_g, v_w_up, v_conv_f_w, v_conv_f_b, v_w_down, v_norm_final_g):
    seq = x.shape[1]
    xs, ts = x[0], loss_target[0]
    mix_tile, ffn_tile, grad_k = min(MIX_TILE, seq), min(FFN_TILE, seq), min(GRAD_K, seq)
    chip = 2 * lax.axis_index("x") + lax.axis_index("y")
    core = lax.axis_index("c").astype(jnp.int32).reshape(1)

    wa_s = jnp.pad(conv_a_w[0], ((0, 32 - CONV_A), (0, 0)))
    wf_s = jnp.pad(conv_f_w[0], ((0, 8 - CONV_F), (0, 0)))
    win, wout, wup, wdown, wa_g, wf_g = _gather_weights(w_in[0], w_out[0], w_up[0], w_down[0], wa_s, wf_s)
    wa = jnp.transpose(wa_g, (1, 0, 2)).reshape(32, D_CONV)
    wf = jnp.transpose(wf_g, (1, 0, 2)).reshape(8, D_FF)
    g3 = norm_final_g.reshape(1, D_MODEL)
    pw = pool_w[0]

    h1, proj, cpre, dpool, mcat, x1 = _mixer_fwd(
        xs, norm_mix_g, win, wa, conv_a_b, ln_a_g, ln_a_b, pw, pool_scale, wout, mix_tile)
    h2, up, act, dx2, sm_f2 = _ffn_fwd(x1, norm_ffn_g, wup, wf, conv_f_b, wdown, g3, ts, ffn_tile)
    g_wdown = _weight_grad(act, dx2, True, grad_k)
    dup, dx1, sm_b1, sf = _ffn_bwd(dx2, up, x1, norm_ffn_g, wup, wf, conv_f_b, wdown, ffn_tile)
    g_wup = _weight_grad(h2, dup, False, grad_k)
    g_wout = _weight_grad(mcat, dx1, True, grad_k)
    dproj, grad_x, sm_b2, s5, sp = _mixer_bwd(
        dx1, xs, proj, cpre, dpool, norm_mix_g, win, wa, ln_a_g, ln_a_b, pw, pool_scale, wout, mix_tile)
    g_win = _weight_grad(h1, dproj, False, grad_k)

    bigs = (g_win, g_wout, g_wup, g_wdown)
    smalls = (sm_f2, sm_b1, sm_b2, sf, s5, sp)
    landed = _sibling_exchange(bigs, smalls)
    tags = ("w_in", "w_out", "w_up", "w_down")
    blocks = (128, 128, 128, 176)
    parts = [_pair_sum(core, b, l, tag, br) for b, l, tag, br in zip(bigs, landed[:4], tags, blocks)]
    small_parts = _pair_sum_small(smalls, landed[4:])
    scattered = _chip_scatter(parts, small_parts)

    big_w = (w_in[0], w_out[0], w_up[0], w_down[0])
    big_m = (m_w_in[0], m_w_out[0], m_w_up[0], m_w_down[0])
    big_v = (v_w_in[0], v_w_out[0], v_w_up[0], v_w_down[0])
    big = {}
    for tag, p, w, m, v, br in zip(tags, scattered[:4], big_w, big_m, big_v, blocks):
        big[tag] = [a[None] for a in _adam_big(p, w, m, v, tag, br)]

    (g_g1, g_g2, g_g3, loss_row, g_wf_all, g_fb, g_wa_all, g_cb, g_lg, g_lb, g_ps, g_pw) = _reduce_small(*scattered[4:])
    g_wa = lax.dynamic_slice(g_wa_all, (0, chip * (D_CONV // N_CHIPS)), (32, D_CONV // N_CHIPS))[:CONV_A]
    g_wf = lax.dynamic_slice(g_wf_all, (0, chip * (D_FF // N_CHIPS)), (8, D_FF // N_CHIPS))[:CONV_F]
    small_names = ("norm_mix_g", "conv_a_w", "conv_a_b", "ln_a_g", "ln_a_b", "pool_w", "pool_scale", "norm_ffn_g",
                   "conv_f_w", "conv_f_b", "norm_final_g")
    small_w = (norm_mix_g, conv_a_w[0], conv_a_b, ln_a_g, ln_a_b, pw, pool_scale, norm_ffn_g, conv_f_w[0], conv_f_b, g3)
    small_m = (m_norm_mix_g, m_conv_a_w[0], m_conv_a_b, m_ln_a_g, m_ln_a_b, m_pool_w[0], m_pool_scale, m_norm_ffn_g,
               m_conv_f_w[0], m_conv_f_b, m_norm_final_g.reshape(1, D_MODEL))
    small_v = (v_norm_mix_g, v_conv_a_w[0], v_conv_a_b, v_ln_a_g, v_ln_a_b, v_pool_w[0], v_pool_scale, v_norm_ffn_g,
               v_conv_f_w[0], v_conv_f_b, v_norm_final_g.reshape(1, D_MODEL))
    small_g = (g_g1, g_wa, g_cb, g_lg, g_lb, g_pw, g_ps, g_g2, g_wf, g_fb, g_g3)
    s_delta, s_m, s_v = _adam_small(small_w, small_g, small_m, small_v)
    shapes = {"conv_a_w": conv_a_w.shape, "pool_w": pool_w.shape, "conv_f_w": conv_f_w.shape, "norm_final_g": norm_final_g.shape}
    small = {}
    for t, name in enumerate(small_names):
        shp = shapes.get(name)
        small[name] = [a if shp is None else a.reshape(shp) for a in (small_g[t], s_delta[t], s_m[t], s_v[t])]

    order = ("norm_mix_g", "w_in", "conv_a_w", "conv_a_b", "ln_a_g", "ln_a_b", "pool_w", "pool_scale", "w_out", "norm_ffn_g",
             "w_up", "conv_f_w", "conv_f_b", "w_down", "norm_final_g")
    table = {**big, **small}
    loss = loss_row[0, 0]
    outs = [loss, grad_x[None]]
    for t in range(4):
        outs += [table[name][t] for name in order]
    return tuple(outs)
```

```python
import functools

import jax
import jax.numpy as jnp
from jax import lax
from jax.experimental import pallas as pl
from jax.experimental.pallas import tpu as pltpu

F32 = jnp.float32
BF16 = jnp.bfloat16
EPS = 1e-6
ADAM_LR = 0.001
ADAM_B1 = 0.9
ADAM_B2 = 0.999
ADAM_EPS = 1e-08
ADAM_WD = 0.01
ADAM_STEP = 10

D_MODEL = 1024
D_CONV = 512
D_POOL = 512
D_IN = 1536
D_FF = 2816
CONV_A = 31
CONV_F = 3
POOL_WINDOWS = (2, 4, 8, 16)
POOL_GROUP = 128
N_CHIPS = 4
FF_CHUNK = 256
N_FF_CHUNKS = D_FF // FF_CHUNK
A_HALO = 32
F_HALO = 16
P_HALO = 16
VMEM_LIMIT = 56 * 1024 * 1024
MESH = pl.DeviceIdType.MESH

ANY = pl.BlockSpec(memory_space=pl.ANY)
VMEM = pl.BlockSpec(memory_space=pltpu.VMEM)


def _dot(a, b):
    return jnp.dot(a, b, preferred_element_type=F32)


def _dot_nt(a, b):
    return lax.dot_general(a, b, (((1,), (1,)), ((), ())), preferred_element_type=F32)


def _dot_tn(a, b):
    return lax.dot_general(a, b, (((0,), (0,)), ((), ())), preferred_element_type=F32)


def _sigmoid(v):
    return jax.nn.sigmoid(v)


def _colsum(v):
    return jnp.sum(v, axis=0, keepdims=True)


def _rowmean(v):
    return jnp.mean(v, axis=-1, keepdims=True)


def _place():
    x, y, c = lax.axis_index("x"), lax.axis_index("y"), lax.axis_index("c")
    chips = [(1 - x, y), (x, 1 - y), (1 - x, 1 - y)]
    return x, y, c, 2 * x + y, chips


def _gather_weights(win_s, wout_s, wup_s, wdown_s, wa_s, wf_s):
    shards = (win_s, wout_s, wup_s, wdown_s)
    col_sharded = (True, False, True, False)
    n_big = len(shards)

    def body(win_r, wout_r, wup_r, wdown_r, wa_r, wf_r, win_f, wout_f, wup_f, wdown_f, wa_g, wf_g,
             b0, b1, b2, b3, ici_send, ici_recv, fwd_send, fwd_recv, cv_send, cv_recv, loc_sem):
        x, y, c, k, chips = _place()
        srcs = (win_r, wout_r, wup_r, wdown_r)
        bufs = (b0, b1, b2, b3)
        fulls = (win_f, wout_f, wup_f, wdown_f)
        for s, b in zip(srcs, bufs):
            b[...] = s[...].astype(BF16)

        def block(i, kk, half=None):
            rows, cols = shards[i].shape
            if col_sharded[i]:
                rs = slice(None) if half is None else pl.ds(pl.multiple_of(half * (rows // 2), 16), rows // 2)
                return fulls[i].at[rs, pl.ds(pl.multiple_of(kk * cols, 128), cols)]
            if half is None:
                return fulls[i].at[pl.ds(pl.multiple_of(kk * rows, 16), rows), :]
            return fulls[i].at[pl.ds(pl.multiple_of(kk * rows + half * (rows // 2), 16), rows // 2), :]

        def my_half(i):
            rows = shards[i].shape[0]
            return bufs[i].at[pl.ds(pl.multiple_of(c * (rows // 2), 16), rows // 2), :]

        local = [pltpu.make_async_copy(bufs[i], block(i, k), loc_sem.at[i]) for i in range(n_big)]
        local.append(pltpu.make_async_copy(wa_r, wa_g.at[k], loc_sem.at[n_big]))
        local.append(pltpu.make_async_copy(wf_r, wf_g.at[k], loc_sem.at[n_big + 1]))
        for cp in local:
            cp.start()

        def ici(i, j, kk, src):
            return pltpu.make_async_remote_copy(
                src_ref=src, dst_ref=block(i, kk, c), send_sem=ici_send.at[i * 3 + j], recv_sem=ici_recv.at[i * 3 + j],
                device_id=(*chips[j], c), device_id_type=MESH)

        def fwd(i, j, kk, half):
            return pltpu.make_async_remote_copy(
                src_ref=block(i, kk, half), dst_ref=block(i, kk, half), send_sem=fwd_send.at[i * 3 + j],
                recv_sem=fwd_recv.at[i * 3 + j], device_id=(x, y, 1 - c), device_id_type=MESH)

        def conv(t, j, kk, src, dst):
            return pltpu.make_async_remote_copy(
                src_ref=src, dst_ref=dst.at[kk], send_sem=cv_send.at[t * 3 + j], recv_sem=cv_recv.at[t * 3 + j],
                device_id=(*chips[j], c), device_id_type=MESH)

        sends = [ici(i, j, k, my_half(i)) for i in range(n_big) for j in range(3)]
        sends += [conv(t, j, k, src, dst) for t, (src, dst) in enumerate(((wa_r, wa_g), (wf_r, wf_g))) for j in range(3)]
        for cp in sends:
            cp.start()
        passed = []
        for i in range(n_big):
            for j, (qx, qy) in enumerate(chips):
                kq = 2 * qx + qy
                ici(i, j, kq, my_half(i)).wait_recv()
                cp = fwd(i, j, kq, c)
                cp.start()
                passed.append(cp)
        for i in range(n_big):
            for j, (qx, qy) in enumerate(chips):
                fwd(i, j, 2 * qx + qy, 1 - c).wait_recv()
        for t, (src, dst) in enumerate(((wa_r, wa_g), (wf_r, wf_g))):
            for j, (qx, qy) in enumerate(chips):
                conv(t, j, 2 * qx + qy, src, dst).wait_recv()
        for cp in sends + passed:
            cp.wait_send()
        for cp in local:
            cp.wait()

    s_len = D_MODEL
    out_shape = (
        jax.ShapeDtypeStruct((s_len, D_IN), BF16),
        jax.ShapeDtypeStruct((D_MODEL, D_MODEL), BF16),
        jax.ShapeDtypeStruct((s_len, 2 * D_FF), BF16),
        jax.ShapeDtypeStruct((D_FF, D_MODEL), BF16),
        jax.ShapeDtypeStruct((N_CHIPS,) + wa_s.shape, F32),
        jax.ShapeDtypeStruct((N_CHIPS,) + wf_s.shape, F32),
    )
    return pl.pallas_call(
        body, name="gather_weights", out_shape=out_shape,
        in_specs=[VMEM] * 6, out_specs=[ANY] * 6,
        scratch_shapes=[pltpu.VMEM(s.shape, BF16) for s in shards] + [
            pltpu.SemaphoreType.DMA((12,)), pltpu.SemaphoreType.DMA((12,)),
            pltpu.SemaphoreType.DMA((12,)), pltpu.SemaphoreType.DMA((12,)),
            pltpu.SemaphoreType.DMA((6,)), pltpu.SemaphoreType.DMA((6,)),
            pltpu.SemaphoreType.DMA((6,)),
        ],
        compiler_params=pltpu.CompilerParams(vmem_limit_bytes=VMEM_LIMIT),
    )(win_s, wout_s, wup_s, wdown_s, wa_s, wf_s)


def _load_weights(pairs, sem):
    cps = [pltpu.make_async_copy(src, dst, sem.at[i]) for i, (src, dst) in enumerate(pairs)]
    for cp in cps:
        cp.start()
    for cp in cps:
        cp.wait()


def _shifted_views(buf, shifted, t_rows):
    n = t_rows + A_HALO - 8
    for b in range(1, 8):
        shifted[b - 1] = buf[b:b + n, :]

    def view(offset):
        a, b = divmod(offset, 8)
        if b == 0:
            return buf[8 * a:8 * a + t_rows, :]
        return shifted[b - 1, 8 * a:8 * a + t_rows, :]

    return view


def _pool_count(tile, t_rows, w):
    row = lax.broadcasted_iota(jnp.int32, (t_rows, POOL_GROUP), 0) + tile * t_rows
    return jnp.minimum(row + 1, w).astype(F32)


def _mixer_fwd(x, g1, win, wa, cb, lg, lb, pw, ps, wout, tile_rows):
    seq = x.shape[0]
    tr = tile_rows
    n = seq // tr

    def body(x_ref, g1_ref, win_hbm, wa_ref, cb_ref, lg_ref, lb_ref, pw_ref, ps_ref, wout_hbm,
             h1_ref, proj_ref, c_ref, d_ref, m_ref, x1_ref, win_v, wout_v, ubuf, ushift, bbuf, sem):
        i = pl.program_id(0)

        @pl.when(i == 0)
        def _():
            _load_weights(((win_hbm, win_v), (wout_hbm, wout_v)), sem)
            ubuf[0:A_HALO, :] = jnp.zeros((A_HALO, D_CONV), F32)
            bbuf[0:P_HALO, :] = jnp.zeros((P_HALO, D_POOL), F32)

        xv = x_ref[...]
        r = lax.rsqrt(_rowmean(xv * xv) + EPS)
        h1 = (xv * r * g1_ref[...]).astype(BF16)
        h1_ref[...] = h1
        proj = _dot(h1, win_v[...])
        proj_ref[...] = proj.astype(BF16)
        av, ag, bi = proj[:, :D_CONV], proj[:, D_CONV:2 * D_CONV], proj[:, 2 * D_CONV:]
        ubuf[A_HALO:A_HALO + tr, :] = av * _sigmoid(ag)
        off = A_HALO - (CONV_A - 1)
        uview = _shifted_views(ubuf, ushift, tr)
        acc = wa_ref[0:1, :] * uview(off)
        for j in range(1, CONV_A):
            acc = acc + wa_ref[j:j + 1, :] * uview(off + j)
        cv = acc + cb_ref[...]
        ubuf[0:A_HALO, :] = ubuf[tr:tr + A_HALO, :]
        c_ref[...] = cv.astype(BF16)
        xc = cv - _rowmean(cv)
        z = xc * lax.rsqrt(_rowmean(xc * xc) + EPS)
        ln = z * lg_ref[...] + lb_ref[...]
        ya = ln * _sigmoid(ln)
        bbuf[P_HALO:P_HALO + tr, :] = bi
        ds, ybs = [], []
        for g, w in enumerate(POOL_WINDOWS):
            cols = slice(g * POOL_GROUP, (g + 1) * POOL_GROUP)
            s = bi[:, cols]
            for kk in range(1, w):
                s = s + bbuf[P_HALO - kk:P_HALO - kk + tr, cols]
            dg = s / _pool_count(i, tr, w) - bi[:, cols]
            ds.append(dg)
            ybs.append(_dot(dg.astype(BF16), pw_ref[g].astype(BF16)))
        bbuf[0:P_HALO, :] = bbuf[tr:tr + P_HALO, :]
        d_ref[...] = jnp.concatenate(ds, axis=1).astype(BF16)
        yb = jnp.concatenate(ybs, axis=1) * ps_ref[...]
        m = jnp.concatenate([ya, yb], axis=1).astype(BF16)
        m_ref[...] = m
        x1_ref[...] = xv + _dot(m, wout_v[...])

    tile = lambda w: pl.BlockSpec((tr, w), lambda i: (i, 0))
    full = lambda a: pl.BlockSpec(a.shape, lambda i: (0,) * a.ndim)
    return pl.pallas_call(
        body, name="mixer_fwd", grid=(n,),
        in_specs=[tile(D_MODEL), full(g1), ANY, full(wa), full(cb), full(lg), full(lb), full(pw), full(ps), ANY],
        out_specs=[tile(D_MODEL), tile(D_IN), tile(D_CONV), tile(D_POOL), tile(D_MODEL), tile(D_MODEL)],
        out_shape=[
            jax.ShapeDtypeStruct((seq, D_MODEL), BF16), jax.ShapeDtypeStruct((seq, D_IN), BF16),
            jax.ShapeDtypeStruct((seq, D_CONV), BF16), jax.ShapeDtypeStruct((seq, D_POOL), BF16),
            jax.ShapeDtypeStruct((seq, D_MODEL), BF16), jax.ShapeDtypeStruct((seq, D_MODEL), F32),
        ],
        scratch_shapes=[
            pltpu.VMEM(win.shape, BF16), pltpu.VMEM(wout.shape, BF16),
            pltpu.VMEM((tr + A_HALO, D_CONV), F32), pltpu.VMEM((7, tr + A_HALO - 8, D_CONV), F32),
            pltpu.VMEM((tr + P_HALO, D_POOL), F32), pltpu.SemaphoreType.DMA((2,)),
        ],
        compiler_params=pltpu.CompilerParams(dimension_semantics=("arbitrary",), vmem_limit_bytes=VMEM_LIMIT),
    )(x, g1, win, wa, cb, lg, lb, pw, ps, wout)


def _ffn_fwd(x1, g2, wup, wf, fb, wdown, g3, target, tile_rows):
    seq = x1.shape[0]
    tr = tile_rows
    n = seq // tr

    def body(x1_ref, g2_ref, wup_hbm, wf_ref, fb_ref, wdown_hbm, g3_ref, t_ref,
             h2_ref, up_ref, act_ref, dx2_ref, dx2b_ref, sm_ref, wup_v, wdown_v, gbuf, sem):
        i = pl.program_id(0)

        @pl.when(i == 0)
        def _():
            _load_weights(((wup_hbm, wup_v), (wdown_hbm, wdown_v)), sem)
            gbuf[0:8, :] = jnp.zeros((8, D_FF), F32)
            sm_ref[...] = jnp.zeros(sm_ref.shape, F32)

        x1v = x1_ref[...]
        r2 = lax.rsqrt(_rowmean(x1v * x1v) + EPS)
        h2 = (x1v * r2 * g2_ref[...]).astype(BF16)
        h2_ref[...] = h2
        x2 = x1v

        def up_proj(j):
            return (_dot(h2, wup_v[:, j * FF_CHUNK:(j + 1) * FF_CHUNK]),
                    _dot(h2, wup_v[:, D_FF + j * FF_CHUNK:D_FF + (j + 1) * FF_CHUNK]))

        ahead = up_proj(0)
        for j in range(N_FF_CHUNKS):
            cs = slice(j * FF_CHUNK, (j + 1) * FF_CHUNK)
            vs = slice(D_FF + j * FF_CHUNK, D_FF + (j + 1) * FF_CHUNK)
            gate, val = ahead
            if j + 1 < N_FF_CHUNKS:
                ahead = up_proj(j + 1)
            up_ref[:, cs] = gate.astype(BF16)
            up_ref[:, vs] = val.astype(BF16)
            gbuf[8:8 + tr, cs] = gate
            gc = (wf_ref[0:1, cs] * gbuf[6:6 + tr, cs] + wf_ref[1:2, cs] * gbuf[7:7 + tr, cs]
                  + wf_ref[2:3, cs] * gate + fb_ref[:, cs])
            gbuf[0:8, cs] = gbuf[tr:tr + 8, cs]
            act = (gc * _sigmoid(gc) * val).astype(BF16)
            act_ref[:, cs] = act
            x2 = x2 + _dot(act, wdown_v[cs, :])
        r3 = lax.rsqrt(_rowmean(x2 * x2) + EPS)
        n3 = x2 * r3
        err = n3 * g3_ref[...] - t_ref[...]
        dy = err / D_MODEL
        sm_ref[2:3, :] += _colsum(dy * n3)
        loss = 0.5 * _colsum(_rowmean(err * err))
        sm_ref[3:4, :] += jnp.broadcast_to(loss, (1, D_MODEL))
        dn = dy * g3_ref[...]
        dx2v = r3 * (dn - n3 * _rowmean(dn * n3))
        dx2_ref[...] = dx2v
        dx2b_ref[...] = dx2v.astype(BF16)

    tile = lambda w: pl.BlockSpec((tr, w), lambda i: (i, 0))
    full = lambda a: pl.BlockSpec(a.shape, lambda i: (0,) * a.ndim)
    return pl.pallas_call(
        body, name="ffn_fwd", grid=(n,),
        in_specs=[tile(D_MODEL), full(g2), ANY, full(wf), full(fb), ANY, full(g3), tile(D_MODEL)],
        out_specs=[tile(D_MODEL), tile(2 * D_FF), tile(D_FF), tile(D_MODEL), tile(D_MODEL),
                   pl.BlockSpec((8, D_MODEL), lambda i: (0, 0))],
        out_shape=[
            jax.ShapeDtypeStruct((seq, D_MODEL), BF16), jax.ShapeDtypeStruct((seq, 2 * D_FF), BF16),
            jax.ShapeDtypeStruct((seq, D_FF), BF16), jax.ShapeDtypeStruct((seq, D_MODEL), F32),
            jax.ShapeDtypeStruct((seq, D_MODEL), BF16), jax.ShapeDtypeStruct((8, D_MODEL), F32),
        ],
        scratch_shapes=[
            pltpu.VMEM(wup.shape, BF16), pltpu.VMEM(wdown.shape, BF16),
            pltpu.VMEM((tr + 8, D_FF), F32), pltpu.SemaphoreType.DMA((2,)),
        ],
        compiler_params=pltpu.CompilerParams(dimension_semantics=("arbitrary",), vmem_limit_bytes=VMEM_LIMIT),
    )(x1, g2, wup, wf, fb, wdown, g3, target)


def _ffn_bwd(dx2, up, x1, g2, wup, wf, fb, wdown, tile_rows):
    seq = x1.shape[0]
    tr = tile_rows
    n = seq // tr

    def body(dx2_ref, up_ref, uph_ref, x1_ref, g2_ref, wup_hbm, wf_ref, fb_ref, wdown_hbm,
             dup_ref, dx1_ref, dx1b_ref, sm_ref, sf_ref, wup_v, wdown_v, gbuf, dbuf, dcar, sem):
        i = pl.program_id(0)
        tile = n - 1 - i

        @pl.when(i == 0)
        def _():
            _load_weights(((wup_hbm, wup_v), (wdown_hbm, wdown_v)), sem)
            dcar[...] = jnp.zeros(dcar.shape, F32)
            sm_ref[...] = jnp.zeros(sm_ref.shape, F32)
            sf_ref[...] = jnp.zeros(sf_ref.shape, F32)

        dx2v = dx2_ref[...]
        dx2b = dx2v.astype(BF16)
        keep = (tile > 0).astype(F32)
        dh2 = jnp.zeros((tr, D_MODEL), F32)

        def down_t(j):
            return _dot_nt(dx2b, wdown_v[j * FF_CHUNK:(j + 1) * FF_CHUNK, :])

        ahead = down_t(0)
        for j in range(N_FF_CHUNKS):
            cs = slice(j * FF_CHUNK, (j + 1) * FF_CHUNK)
            vs = slice(D_FF + j * FF_CHUNK, D_FF + (j + 1) * FF_CHUNK)
            dact = ahead
            if j + 1 < N_FF_CHUNKS:
                ahead = down_t(j + 1)
            gate = up_ref[:, cs].astype(F32)
            val = up_ref[:, vs].astype(F32)
            gbuf[0:F_HALO, :] = uph_ref[:, cs].astype(F32) * keep
            gbuf[F_HALO:F_HALO + tr, :] = gate
            g_m2 = gbuf[F_HALO - 2:F_HALO - 2 + tr, :]
            g_m1 = gbuf[F_HALO - 1:F_HALO - 1 + tr, :]
            gc = wf_ref[0:1, cs] * g_m2 + wf_ref[1:2, cs] * g_m1 + wf_ref[2:3, cs] * gate + fb_ref[:, cs]
            sg = _sigmoid(gc)
            dval = dact * (gc * sg)
            dgc = dact * val * (sg * (1.0 + gc * (1.0 - sg)))
            dbuf[0:tr, :] = dgc
            dbuf[tr:tr + 8, :] = dcar[:, cs]
            dgate = wf_ref[2:3, cs] * dgc + wf_ref[1:2, cs] * dbuf[1:1 + tr, :] + wf_ref[0:1, cs] * dbuf[2:2 + tr, :]
            dcar[:, cs] = dgc[0:8, :]
            sf_ref[0:1, cs] += _colsum(dgc * g_m2)
            sf_ref[1:2, cs] += _colsum(dgc * g_m1)
            sf_ref[2:3, cs] += _colsum(dgc * gate)
            sf_ref[3:4, cs] += _colsum(dgc)
            dgb, dvb = dgate.astype(BF16), dval.astype(BF16)
            dup_ref[:, cs] = dgb
            dup_ref[:, vs] = dvb
            dh2 = dh2 + _dot_nt(dgb, wup_v[:, cs]) + _dot_nt(dvb, wup_v[:, vs])
        x1v = x1_ref[...]
        r2 = lax.rsqrt(_rowmean(x1v * x1v) + EPS)
        n2 = x1v * r2
        sm_ref[1:2, :] += _colsum(dh2 * n2)
        dn2 = dh2 * g2_ref[...]
        dx1v = dx2v + r2 * (dn2 - n2 * _rowmean(dn2 * n2))
        dx1_ref[...] = dx1v
        dx1b_ref[...] = dx1v.astype(BF16)

    tile = lambda w: pl.BlockSpec((tr, w), lambda i: (n - 1 - i, 0))
    full = lambda a: pl.BlockSpec(a.shape, lambda i: (0,) * a.ndim)
    halo = pl.BlockSpec((F_HALO, D_FF), lambda i: (jnp.maximum((n - 1 - i) * (tr // F_HALO) - 1, 0), 0))
    acc = lambda rows, w: pl.BlockSpec((rows, w), lambda i: (0, 0))
    return pl.pallas_call(
        body, name="ffn_bwd", grid=(n,),
        in_specs=[tile(D_MODEL), tile(2 * D_FF), halo, tile(D_MODEL), full(g2), ANY, full(wf), full(fb), ANY],
        out_specs=[tile(2 * D_FF), tile(D_MODEL), tile(D_MODEL), acc(8, D_MODEL), acc(8, D_FF)],
        out_shape=[
            jax.ShapeDtypeStruct((seq, 2 * D_FF), BF16), jax.ShapeDtypeStruct((seq, D_MODEL), F32),
            jax.ShapeDtypeStruct((seq, D_MODEL), BF16), jax.ShapeDtypeStruct((8, D_MODEL), F32),
            jax.ShapeDtypeStruct((8, D_FF), F32),
        ],
        scratch_shapes=[
            pltpu.VMEM(wup.shape, BF16), pltpu.VMEM(wdown.shape, BF16),
            pltpu.VMEM((tr + F_HALO, FF_CHUNK), F32), pltpu.VMEM((tr + 8, FF_CHUNK), F32),
            pltpu.VMEM((8, D_FF), F32), pltpu.SemaphoreType.DMA((2,)),
        ],
        compiler_params=pltpu.CompilerParams(dimension_semantics=("arbitrary",), vmem_limit_bytes=VMEM_LIMIT),
    )(dx2, up, up, x1, g2, wup, wf, fb, wdown)


def _mixer_bwd(dx1, x, proj, cpre, d, g1, win, wa, lg, lb, pw, ps, wout, tile_rows):
    seq = x.shape[0]
    tr = tile_rows
    n = seq // tr
    row_cb, row_lg, row_lb, row_ps = 32, 33, 34, 35

    def body(dx1_ref, x_ref, proj_ref, projh_ref, c_ref, d_ref, g1_ref, win_hbm, wa_ref, lg_ref, lb_ref, pw_ref, ps_ref,
             wout_hbm, dproj_ref, gx_ref, sm_ref, s5_ref, sp_ref, win_v, wout_v, ubuf, ushift, dcbuf, dshift, ebuf, sem):
        i = pl.program_id(0)
        tile = n - 1 - i

        @pl.when(i == 0)
        def _():
            _load_weights(((win_hbm, win_v), (wout_hbm, wout_v)), sem)
            dcbuf[tr:tr + A_HALO, :] = jnp.zeros((A_HALO, D_CONV), F32)
            ebuf[tr:tr + P_HALO, :] = jnp.zeros((P_HALO, D_POOL), F32)
            sm_ref[...] = jnp.zeros(sm_ref.shape, F32)
            s5_ref[...] = jnp.zeros(s5_ref.shape, F32)
            sp_ref[...] = jnp.zeros(sp_ref.shape, F32)

        dx1v = dx1_ref[...]
        dm = _dot_nt(dx1v.astype(BF16), wout_v[...])
        dya, dyb = dm[:, :D_CONV], dm[:, D_CONV:]
        dbis = []
        for g, w in enumerate(POOL_WINDOWS):
            cols = slice(g * POOL_GROUP, (g + 1) * POOL_GROUP)
            dgb = d_ref[:, cols]
            pwb = pw_ref[g].astype(BF16)
            dyg = dyb[:, cols]
            s5_ref[row_ps:row_ps + 1, cols] += _colsum(dyg * _dot(dgb, pwb))
            dqb = (dyg * ps_ref[:, cols]).astype(BF16)
            sp_ref[g] += _dot_tn(dgb, dqb)
            dd = _dot_nt(dqb, pwb)
            e = dd / _pool_count(tile, tr, w)
            ebuf[0:tr, cols] = e
            s = e
            for kk in range(1, w):
                s = s + ebuf[kk:kk + tr, cols]
            dbis.append(s - dd)
        ebuf[tr:tr + P_HALO, :] = ebuf[0:P_HALO, :]
        cv = c_ref[...].astype(F32)
        xc = cv - _rowmean(cv)
        rs = lax.rsqrt(_rowmean(xc * xc) + EPS)
        z = xc * rs
        ln = z * lg_ref[...] + lb_ref[...]
        sl = _sigmoid(ln)
        dl = dya * (sl * (1.0 + ln * (1.0 - sl)))
        s5_ref[row_lg:row_lg + 1, :] += _colsum(dl * z)
        s5_ref[row_lb:row_lb + 1, :] += _colsum(dl)
        dz = dl * lg_ref[...]
        dc = rs * (dz - _rowmean(dz) - z * _rowmean(dz * z))
        s5_ref[row_cb:row_cb + 1, :] += _colsum(dc)
        dcbuf[0:tr, :] = dc
        keep = (tile > 0).astype(F32)
        avh = projh_ref[:, :D_CONV].astype(F32)
        agh = projh_ref[:, D_CONV:].astype(F32)
        ubuf[0:A_HALO, :] = avh * _sigmoid(agh) * keep
        av = proj_ref[:, :D_CONV].astype(F32)
        ag = proj_ref[:, D_CONV:2 * D_CONV].astype(F32)
        sg = _sigmoid(ag)
        ubuf[A_HALO:A_HALO + tr, :] = av * sg
        off = A_HALO - (CONV_A - 1)
        du = wa_ref[CONV_A - 1:CONV_A, :] * dc
        dview = _shifted_views(dcbuf, dshift, tr)
        uview = _shifted_views(ubuf, ushift, tr)
        for j in range(CONV_A - 1):
            du = du + wa_ref[j:j + 1, :] * dview(CONV_A - 1 - j)
        for j in range(CONV_A):
            s5_ref[j:j + 1, :] += _colsum(dc * uview(off + j))
        dcbuf[tr:tr + A_HALO, :] = dcbuf[0:A_HALO, :]
        dav = du * sg
        dag = du * av * (sg * (1.0 - sg))
        dprojb = jnp.concatenate([dav, dag] + dbis, axis=1).astype(BF16)
        dproj_ref[...] = dprojb
        dh1 = _dot_nt(dprojb, win_v[...])
        xv = x_ref[...]
        r1 = lax.rsqrt(_rowmean(xv * xv) + EPS)
        n1 = xv * r1
        sm_ref[0:1, :] += _colsum(dh1 * n1)
        dn1 = dh1 * g1_ref[...]
        gx_ref[...] = dx1v + r1 * (dn1 - n1 * _rowmean(dn1 * n1))

    tile = lambda w: pl.BlockSpec((tr, w), lambda i: (n - 1 - i, 0))
    full = lambda a: pl.BlockSpec(a.shape, lambda i: (0,) * a.ndim)
    halo = pl.BlockSpec((A_HALO, 2 * D_CONV), lambda i: (jnp.maximum((n - 1 - i) * (tr // A_HALO) - 1, 0), 0))
    acc = lambda shape: pl.BlockSpec(shape, lambda i: (0,) * len(shape))
    return pl.pallas_call(
        body, name="mixer_bwd", grid=(n,),
        in_specs=[tile(D_MODEL), tile(D_MODEL), tile(D_IN), halo, tile(D_CONV), tile(D_POOL), full(g1), ANY, full(wa),
                  full(lg), full(lb), full(pw), full(ps), ANY],
        out_specs=[tile(D_IN), tile(D_MODEL), acc((8, D_MODEL)), acc((40, D_CONV)), acc(pw.shape)],
        out_shape=[
            jax.ShapeDtypeStruct((seq, D_IN), BF16), jax.ShapeDtypeStruct((seq, D_MODEL), F32),
            jax.ShapeDtypeStruct((8, D_MODEL), F32), jax.ShapeDtypeStruct((40, D_CONV), F32),
            jax.ShapeDtypeStruct(pw.shape, F32),
        ],
        scratch_shapes=[
            pltpu.VMEM(win.shape, BF16), pltpu.VMEM(wout.shape, BF16),
            pltpu.VMEM((tr + A_HALO, D_CONV), F32), pltpu.VMEM((7, tr + A_HALO - 8, D_CONV), F32),
            pltpu.VMEM((tr + A_HALO, D_CONV), F32), pltpu.VMEM((7, tr + A_HALO - 8, D_CONV), F32),
            pltpu.VMEM((tr + P_HALO, D_POOL), F32), pltpu.SemaphoreType.DMA((2,)),
        ],
        compiler_params=pltpu.CompilerParams(dimension_semantics=("arbitrary",), vmem_limit_bytes=VMEM_LIMIT),
    )(dx1, x, proj, proj, cpre, d, g1, win, wa, lg, lb, pw, ps, wout)


def _weight_grad(a, b, layout, k_rows):
    seq, m_dim = a.shape
    n_dim = b.shape[1]
    steps = seq // k_rows

    def store(o_ref, index, value):
        s = pl.program_id(1)

        @pl.when(s == 0)
        def _():
            o_ref[index] = value

        @pl.when(s > 0)
        def _():
            o_ref[index] += value

    if layout in ("rows1", "rows2"):
        groups = int(layout[-1])
        per_tile = N_CHIPS // groups
        rows = m_dim // N_CHIPS // 2
        a_w = m_dim // groups

        def body(a_ref, b_ref, o_ref):
            r = _dot_tn(a_ref[...], b_ref[...])
            for p in range(per_tile):
                for h in range(2):
                    store(o_ref, (p, h), r[(2 * p + h) * rows:(2 * p + h + 1) * rows, :])

        in_specs = [pl.BlockSpec((k_rows, a_w), lambda g, s: (s, g)), pl.BlockSpec((k_rows, n_dim), lambda g, s: (s, 0))]
        out_spec = pl.BlockSpec((per_tile, 2, rows, n_dim), lambda g, s: (g, 0, 0, 0))
        out_dims = (N_CHIPS, 2, rows, n_dim)
    elif layout == "cols_chip":
        groups = N_CHIPS
        rows, cols = m_dim // 2, n_dim // N_CHIPS

        def body(a_ref, b_ref, o_ref):
            r = _dot_tn(a_ref[...], b_ref[...])
            for h in range(2):
                store(o_ref, h, r[h * rows:(h + 1) * rows, :])

        in_specs = [pl.BlockSpec((k_rows, m_dim), lambda g, s: (s, 0)), pl.BlockSpec((k_rows, cols), lambda g, s: (s, g))]
        out_spec = pl.BlockSpec((None, 2, rows, cols), lambda g, s: (g, 0, 0, 0))
        out_dims = (N_CHIPS, 2, rows, cols)
    else:
        groups = 2
        rows, cols = m_dim // 2, n_dim // N_CHIPS

        def body(a_ref, b_ref, o_ref):
            r = _dot_tn(a_ref[...], b_ref[...])
            for k in range(N_CHIPS):
                store(o_ref, k, r[:, k * cols:(k + 1) * cols])

        in_specs = [pl.BlockSpec((k_rows, rows), lambda g, s: (s, g)), pl.BlockSpec((k_rows, n_dim), lambda g, s: (s, 0))]
        out_spec = pl.BlockSpec((N_CHIPS, None, rows, cols), lambda g, s: (0, g, 0, 0))
        out_dims = (N_CHIPS, 2, rows, cols)

    return pl.pallas_call(
        body, name=f"weight_grad_{layout}_{m_dim}x{n_dim}", grid=(groups, steps),
        in_specs=in_specs, out_specs=out_spec, out_shape=jax.ShapeDtypeStruct(out_dims, F32),
        compiler_params=pltpu.CompilerParams(dimension_semantics=("arbitrary", "arbitrary"), vmem_limit_bytes=VMEM_LIMIT),
    )(a, b)


def _sibling_exchange(bigs, smalls):
    nb, ns = len(bigs), len(smalls)

    def body(*refs):
        ins, outs = refs[:nb + ns], refs[nb + ns:2 * (nb + ns)]
        send, recv = refs[2 * (nb + ns):]
        x, y, c, _, _ = _place()
        cps = []
        for t in range(nb + ns):
            src = ins[t].at[:, 1 - c] if t < nb else ins[t]
            cps.append(pltpu.make_async_remote_copy(
                src_ref=src, dst_ref=outs[t], send_sem=send.at[t], recv_sem=recv.at[t],
                device_id=(x, y, 1 - c), device_id_type=MESH))
        for cp in cps:
            cp.start()
        for cp in cps:
            cp.wait()

    out_shape = [jax.ShapeDtypeStruct((N_CHIPS,) + b.shape[2:], F32) for b in bigs]
    out_shape += [jax.ShapeDtypeStruct(s.shape, F32) for s in smalls]
    return pl.pallas_call(
        body, name="sibling_exchange", out_shape=out_shape,
        in_specs=[ANY] * (nb + ns), out_specs=[ANY] * (nb + ns),
        scratch_shapes=[pltpu.SemaphoreType.DMA((nb + ns,)), pltpu.SemaphoreType.DMA((nb + ns,))],
    )(*bigs, *smalls)


def _pair_sum(core, mine, theirs, tag, block_rows):
    _, _, rows, cols = mine.shape
    steps = rows // block_rows

    def body(core_ref, a_ref, b_ref, o_ref):
        o_ref[...] = (a_ref[...] + b_ref[...]).astype(BF16)

    grid_spec = pltpu.PrefetchScalarGridSpec(
        num_scalar_prefetch=1, grid=(N_CHIPS, steps),
        in_specs=[pl.BlockSpec((None, None, block_rows, cols), lambda k, r, core_ref: (k, core_ref[0], r, 0)),
                  pl.BlockSpec((None, block_rows, cols), lambda k, r, core_ref: (k, r, 0))],
        out_specs=pl.BlockSpec((None, block_rows, cols), lambda k, r, core_ref: (k, r, 0)),
    )
    return pl.pallas_call(
        body, name=f"pair_sum_{tag}", grid_spec=grid_spec,
        out_shape=jax.ShapeDtypeStruct((N_CHIPS, rows, cols), BF16),
        compiler_params=pltpu.CompilerParams(dimension_semantics=("arbitrary", "arbitrary"), vmem_limit_bytes=VMEM_LIMIT),
    )(core, mine, theirs)


def _pair_sum_small(mine, theirs):
    (m_f2, m_b1, m_b2, m_sf, m_s5, m_sp) = mine

    def body(a0, a1, a2, a3, a4, a5, b0, b1, b2, b3, b4, b5, o_m, o_f, o_5, o_p):
        sm = (a0[...] + a1[...] + a2[...]) + (b0[...] + b1[...] + b2[...])
        sf = a3[...] + b3[...]
        s5 = a4[...] + b4[...]
        for h in range(2):
            o_m[h] = sm[:, h * (D_MODEL // 2):(h + 1) * (D_MODEL // 2)]
            o_f[h] = sf[:, h * (D_FF // 2):(h + 1) * (D_FF // 2)]
            o_5[h] = s5[:, h * (D_CONV // 2):(h + 1) * (D_CONV // 2)]
            for g in range(2):
                o_p[h, g] = a5[2 * h + g] + b5[2 * h + g]

    out_shape = [
        jax.ShapeDtypeStruct((2, 8, D_MODEL // 2), F32), jax.ShapeDtypeStruct((2, 8, D_FF // 2), F32),
        jax.ShapeDtypeStruct((2, 40, D_CONV // 2), F32), jax.ShapeDtypeStruct((2, 2, POOL_GROUP, POOL_GROUP), F32),
    ]
    return pl.pallas_call(body, name="pair_sum_small", out_shape=out_shape, in_specs=[VMEM] * 12, out_specs=[VMEM] * 4)(
        *mine, *theirs)


def _chip_scatter(parts, smalls):
    nt = len(parts) + len(smalls)
    nb = len(parts)

    def body(*refs):
        ins, outs = refs[:nt], refs[nt:2 * nt]
        ici_send, ici_recv, fwd_send, fwd_recv, loc_sem = refs[2 * nt:]
        x, y, c, k, chips = _place()

        def src_of(t, kk):
            return ins[t].at[kk] if t < nb else ins[t].at[c]

        def ici(t, j, kk):
            return pltpu.make_async_remote_copy(
                src_ref=src_of(t, kk), dst_ref=outs[t].at[k, c], send_sem=ici_send.at[t * 3 + j],
                recv_sem=ici_recv.at[t * 3 + j], device_id=(*chips[j], c), device_id_type=MESH)

        def fwd(t, j, q, half, src=None):
            slot = outs[t].at[q, half]
            return pltpu.make_async_remote_copy(
                src_ref=slot if src is None else src, dst_ref=slot, send_sem=fwd_send.at[t * 4 + j],
                recv_sem=fwd_recv.at[t * 4 + j], device_id=(x, y, 1 - c), device_id_type=MESH)

        local = [pltpu.make_async_copy(src_of(t, k), outs[t].at[k, c], loc_sem.at[t]) for t in range(nt)]
        for cp in local:
            cp.start()
        sends = []
        for t in range(nt):
            sends.append(fwd(t, 3, k, c, src=src_of(t, k)))
            for j, (qx, qy) in enumerate(chips):
                cp = ici(t, j, 2 * qx + qy)
                sends.append(cp)
        for cp in sends:
            cp.start()
        passed = []
        for t in range(nt):
            for j, (qx, qy) in enumerate(chips):
                kq = 2 * qx + qy
                pltpu.make_async_remote_copy(
                    src_ref=src_of(t, kq), dst_ref=outs[t].at[kq, c], send_sem=ici_send.at[t * 3 + j],
                    recv_sem=ici_recv.at[t * 3 + j], device_id=(*chips[j], c), device_id_type=MESH).wait_recv()
                cp = fwd(t, j, kq, c)
                cp.start()
                passed.append(cp)
        for t in range(nt):
            fwd(t, 3, k, 1 - c).wait_recv()
            for j, (qx, qy) in enumerate(chips):
                fwd(t, j, 2 * qx + qy, 1 - c).wait_recv()
        for cp in sends + passed:
            cp.wait_send()
        for cp in local:
            cp.wait()

    out_shape = [jax.ShapeDtypeStruct((N_CHIPS, 2) + p.shape[1:], p.dtype) for p in parts]
    out_shape += [jax.ShapeDtypeStruct((N_CHIPS, 2) + s.shape[1:], s.dtype) for s in smalls]
    return pl.pallas_call(
        body, name="chip_scatter", out_shape=out_shape, in_specs=[ANY] * nt, out_specs=[ANY] * nt,
        scratch_shapes=[
            pltpu.SemaphoreType.DMA((3 * nt,)), pltpu.SemaphoreType.DMA((3 * nt,)),
            pltpu.SemaphoreType.DMA((4 * nt,)), pltpu.SemaphoreType.DMA((4 * nt,)),
            pltpu.SemaphoreType.DMA((nt,)),
        ],
    )(*parts, *smalls)


def _adamw(w, g, m, v):
    m = ADAM_B1 * m + (1.0 - ADAM_B1) * g
    v = ADAM_B2 * v + (1.0 - ADAM_B2) * (g * g)
    m_hat = m / (1.0 - ADAM_B1 ** ADAM_STEP)
    v_hat = v / (1.0 - ADAM_B2 ** ADAM_STEP)
    delta = -ADAM_LR * (m_hat / (jnp.sqrt(v_hat) + ADAM_EPS) + ADAM_WD * w)
    return delta, m, v


def _adam_big(parts, w, m, v, tag, block_rows):
    _, _, rows, cols = parts.shape
    steps = rows // block_rows

    def body(p_ref, w_ref, m_ref, v_ref, g_out, d_out, m_out, v_out):
        g = p_ref[0].astype(F32)
        for q in range(1, N_CHIPS):
            g = g + p_ref[q].astype(F32)
        delta, m_new, v_new = _adamw(w_ref[...], g, m_ref[...], v_ref[...])
        g_out[...] = g
        d_out[...] = delta
        m_out[...] = m_new
        v_out[...] = v_new

    blk = pl.BlockSpec((block_rows, cols), lambda h, r: (h * steps + r, 0))
    return pl.pallas_call(
        body, name=f"adam_{tag}", grid=(2, steps),
        in_specs=[pl.BlockSpec((N_CHIPS, None, block_rows, cols), lambda h, r: (0, h, r, 0)), blk, blk, blk],
        out_specs=[blk] * 4, out_shape=[jax.ShapeDtypeStruct(w.shape, F32)] * 4,
        compiler_params=pltpu.CompilerParams(dimension_semantics=("arbitrary", "arbitrary"), vmem_limit_bytes=VMEM_LIMIT),
    )(parts, w, m, v)


def _reduce_small(l_m, l_f, l_5, l_p):
    def total(ref):
        t = ref[0]
        for q in range(1, N_CHIPS):
            t = t + ref[q]
        return t

    def body(m_ref, f_ref, s_ref, p_ref, g1_o, g2_o, g3_o, loss_o, wf_o, fb_o, wa_o, cb_o, lg_o, lb_o, ps_o, pw_o):
        tm, tf, t5, tp = total(m_ref), total(f_ref), total(s_ref), total(p_ref)
        sm = jnp.concatenate([tm[0], tm[1]], axis=1)
        sf = jnp.concatenate([tf[0], tf[1]], axis=1)
        s5 = jnp.concatenate([t5[0], t5[1]], axis=1)
        g1_o[...] = sm[0:1]
        g2_o[...] = sm[1:2]
        g3_o[...] = sm[2:3]
        loss_o[...] = sm[3:4, 0:128]
        wf_o[...] = sf
        fb_o[...] = sf[3:4]
        wa_o[...] = s5[0:32]
        cb_o[...] = s5[32:33]
        lg_o[...] = s5[33:34]
        lb_o[...] = s5[34:35]
        ps_o[...] = s5[35:36]
        for h in range(2):
            for g in range(2):
                pw_o[2 * h + g] = tp[h, g]

    row = lambda w: jax.ShapeDtypeStruct((1, w), F32)
    out_shape = [row(D_MODEL), row(D_MODEL), row(D_MODEL), row(128), jax.ShapeDtypeStruct((8, D_FF), F32), row(D_FF),
                 jax.ShapeDtypeStruct((32, D_CONV), F32), row(D_CONV), row(D_CONV), row(D_CONV), row(D_POOL),
                 jax.ShapeDtypeStruct((4, POOL_GROUP, POOL_GROUP), F32)]
    return pl.pallas_call(body, name="reduce_small", out_shape=out_shape, in_specs=[VMEM] * 4, out_specs=[VMEM] * 12)(
        l_m, l_f, l_5, l_p)


def _adam_small(ws, gs, ms, vs):
    count = len(ws)

    def body(*refs):
        w_r, g_r, m_r, v_r = (refs[t * count:(t + 1) * count] for t in range(4))
        d_o, m_o, v_o = (refs[(4 + t) * count:(5 + t) * count] for t in range(3))
        for t in range(count):
            delta, m_new, v_new = _adamw(w_r[t][...], g_r[t][...], m_r[t][...], v_r[t][...])
            d_o[t][...] = delta
            m_o[t][...] = m_new
            v_o[t][...] = v_new

    out_shape = [jax.ShapeDtypeStruct(w.shape, F32) for w in ws] * 3
    outs = pl.pallas_call(body, name="adam_small", out_shape=out_shape, in_specs=[VMEM] * (4 * count),
                          out_specs=[VMEM] * (3 * count))(*ws, *gs, *ms, *vs)
    return outs[:count], outs[count:2 * count], outs[2 * count:]


MIX_TILE = 512
FFN_TILE = 256
GRAD_K = 2048


def kernel(x, norm_mix_g, w_in, conv_a_w, conv_a_b, ln_a_g, ln_a_b, pool_w, pool_scale, w_out, norm_ffn_g, w_up, conv_f_w, conv_f_b, w_down, norm_final_g, loss_target, m_norm_mix_g, m_w_in, m_conv_a_w, m_conv_a_b, m_ln_a_g, m_ln_a_b, m_pool_w, m_pool_scale, m_w_out, m_norm_ffn_g, m_w_up, m_conv_f_w, m_conv_f_b, m_w_down, m_norm_final_g, v_norm_mix_g, v_w_in, v_conv_a_w, v_conv_a_b, v_ln_a_g, v_ln_a_b, v_pool_w, v_pool_scale, v_w_out, v_norm_ffn_g, v_w_up, v_conv_f_w, v_conv_f_b, v_w_down, v_norm_final_g):
    seq = x.shape[1]
    xs, ts = x[0], loss_target[0]
    mix_tile, ffn_tile, grad_k = min(MIX_TILE, seq), min(FFN_TILE, seq), min(GRAD_K, seq)
    chip = 2 * lax.axis_index("x") + lax.axis_index("y")
    core = lax.axis_index("c").astype(jnp.int32).reshape(1)

    wa_s = jnp.pad(conv_a_w[0], ((0, 32 - CONV_A), (0, 0)))
    wf_s = jnp.pad(conv_f_w[0], ((0, 8 - CONV_F), (0, 0)))
    win, wout, wup, wdown, wa_g, wf_g = _gather_weights(w_in[0], w_out[0], w_up[0], w_down[0], wa_s, wf_s)
    wa = jnp.transpose(wa_g, (1, 0, 2)).reshape(32, D_CONV)
    wf = jnp.transpose(wf_g, (1, 0, 2)).reshape(8, D_FF)
    g3 = norm_final_g.reshape(1, D_MODEL)
    pw = pool_w[0]

    h1, proj, cpre, dpool, mcat, x1 = _mixer_fwd(
        xs, norm_mix_g, win, wa, conv_a_b, ln_a_g, ln_a_b, pw, pool_scale, wout, mix_tile)
    h2, up, act, dx2, dx2b, sm_f2 = _ffn_fwd(x1, norm_ffn_g, wup, wf, conv_f_b, wdown, g3, ts, ffn_tile)
    g_wdown = _weight_grad(act, dx2b, "rows2", grad_k)
    dup, dx1, dx1b, sm_b1, sf = _ffn_bwd(dx2, up, x1, norm_ffn_g, wup, wf, conv_f_b, wdown, ffn_tile)
    g_wup = _weight_grad(h2, dup, "cols_chip", grad_k)
    g_wout = _weight_grad(mcat, dx1b, "rows1", grad_k)
    dproj, grad_x, sm_b2, s5, sp = _mixer_bwd(
        dx1, xs, proj, cpre, dpool, norm_mix_g, win, wa, ln_a_g, ln_a_b, pw, pool_scale, wout, mix_tile)
    g_win = _weight_grad(h1, dproj, "cols_half", grad_k)

    bigs = (g_win, g_wout, g_wup, g_wdown)
    smalls = (sm_f2, sm_b1, sm_b2, sf, s5, sp)
    landed = _sibling_exchange(bigs, smalls)
    tags = ("w_in", "w_out", "w_up", "w_down")
    blocks = (128, 128, 128, 176)
    parts = [_pair_sum(core, b, l, tag, br) for b, l, tag, br in zip(bigs, landed[:4], tags, blocks)]
    small_parts = _pair_sum_small(smalls, landed[4:])
    scattered = _chip_scatter(parts, small_parts)

    big_w = (w_in[0], w_out[0], w_up[0], w_down[0])
    big_m = (m_w_in[0], m_w_out[0], m_w_up[0], m_w_down[0])
    big_v = (v_w_in[0], v_w_out[0], v_w_up[0], v_w_down[0])
    big = {}
    for tag, p, w, m, v, br in zip(tags, scattered[:4], big_w, big_m, big_v, blocks):
        big[tag] = [a[None] for a in _adam_big(p, w, m, v, tag, br)]

    (g_g1, g_g2, g_g3, loss_row, g_wf_all, g_fb, g_wa_all, g_cb, g_lg, g_lb, g_ps, g_pw) = _reduce_small(*scattered[4:])
    g_wa = lax.dynamic_slice(g_wa_all, (0, chip * (D_CONV // N_CHIPS)), (32, D_CONV // N_CHIPS))[:CONV_A]
    g_wf = lax.dynamic_slice(g_wf_all, (0, chip * (D_FF // N_CHIPS)), (8, D_FF // N_CHIPS))[:CONV_F]
    small_names = ("norm_mix_g", "conv_a_w", "conv_a_b", "ln_a_g", "ln_a_b", "pool_w", "pool_scale", "norm_ffn_g",
                   "conv_f_w", "conv_f_b", "norm_final_g")
    small_w = (norm_mix_g, conv_a_w[0], conv_a_b, ln_a_g, ln_a_b, pw, pool_scale, norm_ffn_g, conv_f_w[0], conv_f_b, g3)
    small_m = (m_norm_mix_g, m_conv_a_w[0], m_conv_a_b, m_ln_a_g, m_ln_a_b, m_pool_w[0], m_pool_scale, m_norm_ffn_g,
               m_conv_f_w[0], m_conv_f_b, m_norm_final_g.reshape(1, D_MODEL))
    small_v = (v_norm_mix_g, v_conv_a_w[0], v_conv_a_b, v_ln_a_g, v_ln_a_b, v_pool_w[0], v_pool_scale, v_norm_ffn_g,
               v_conv_f_w[0], v_conv_f_b, v_norm_final_g.reshape(1, D_MODEL))
    small_g = (g_g1, g_wa, g_cb, g_lg, g_lb, g_pw, g_ps, g_g2, g_wf, g_fb, g_g3)
    s_delta, s_m, s_v = _adam_small(small_w, small_g, small_m, small_v)
    shapes = {"conv_a_w": conv_a_w.shape, "pool_w": pool_w.shape, "conv_f_w": conv_f_w.shape, "norm_final_g": norm_final_g.shape}
    small = {}
    for t, name in enumerate(small_names):
        shp = shapes.get(name)
        small[name] = [a if shp is None else a.reshape(shp) for a in (small_g[t], s_delta[t], s_m[t], s_v[t])]

    order = ("norm_mix_g", "w_in", "conv_a_w", "conv_a_b", "ln_a_g", "ln_a_b", "pool_w", "pool_scale", "w_out", "norm_ffn_g",
             "w_up", "conv_f_w", "conv_f_b", "w_down", "norm_final_g")
    table = {**big, **small}
    loss = loss_row[0, 0]
    outs = [loss, grad_x[None]]
    for t in range(4):
        outs += [table[name][t] for name in order]
    return tuple(outs)
```

```python
import functools

import jax
import jax.numpy as jnp
from jax import lax
from jax.experimental import pallas as pl
from jax.experimental.pallas import tpu as pltpu

F32 = jnp.float32
BF16 = jnp.bfloat16
EPS = 1e-6
ADAM_LR = 0.001
ADAM_B1 = 0.9
ADAM_B2 = 0.999
ADAM_EPS = 1e-08
ADAM_WD = 0.01
ADAM_STEP = 10

D_MODEL = 1024
D_CONV = 512
D_POOL = 512
D_IN = 1536
D_FF = 2816
CONV_A = 31
CONV_F = 3
POOL_WINDOWS = (2, 4, 8, 16)
POOL_GROUP = 128
N_CHIPS = 4
FF_CHUNK = 256
N_FF_CHUNKS = D_FF // FF_CHUNK
A_HALO = 32
F_HALO = 16
P_HALO = 16
VMEM_LIMIT = 56 * 1024 * 1024
MESH = pl.DeviceIdType.MESH

ANY = pl.BlockSpec(memory_space=pl.ANY)
VMEM = pl.BlockSpec(memory_space=pltpu.VMEM)


def _dot(a, b):
    return jnp.dot(a, b, preferred_element_type=F32)


def _dot_nt(a, b):
    return lax.dot_general(a, b, (((1,), (1,)), ((), ())), preferred_element_type=F32)


def _dot_tn(a, b):
    return lax.dot_general(a, b, (((0,), (0,)), ((), ())), preferred_element_type=F32)


def _sigmoid(v):
    return jax.nn.sigmoid(v)


def _colsum(v):
    return jnp.sum(v, axis=0, keepdims=True)


def _rowmean(v):
    return jnp.mean(v, axis=-1, keepdims=True)


def _place():
    x, y, c = lax.axis_index("x"), lax.axis_index("y"), lax.axis_index("c")
    chips = [(1 - x, y), (x, 1 - y), (1 - x, 1 - y)]
    return x, y, c, 2 * x + y, chips


def _gather_ops(bufs, fulls, col_sharded, sems):
    ici_send, ici_recv, fwd_send, fwd_recv, loc_sem = sems
    n_big = len(bufs)
    x, y, c, k, chips = _place()

    def block(i, kk, half=None):
        rows, cols = bufs[i].shape
        if col_sharded[i]:
            rs = slice(None) if half is None else pl.ds(pl.multiple_of(half * (rows // 2), 16), rows // 2)
            return fulls[i].at[rs, pl.ds(pl.multiple_of(kk * cols, 128), cols)]
        if half is None:
            return fulls[i].at[pl.ds(pl.multiple_of(kk * rows, 16), rows), :]
        return fulls[i].at[pl.ds(pl.multiple_of(kk * rows + half * (rows // 2), 16), rows // 2), :]

    def my_half(i):
        rows = bufs[i].shape[0]
        return bufs[i].at[pl.ds(pl.multiple_of(c * (rows // 2), 16), rows // 2), :]

    def ici(i, j, kk):
        return pltpu.make_async_remote_copy(
            src_ref=my_half(i), dst_ref=block(i, kk, c), send_sem=ici_send.at[i * 3 + j], recv_sem=ici_recv.at[i * 3 + j],
            device_id=(*chips[j], c), device_id_type=MESH)

    def fwd(i, j, kk, half):
        return pltpu.make_async_remote_copy(
            src_ref=block(i, kk, half), dst_ref=block(i, kk, half), send_sem=fwd_send.at[i * 3 + j],
            recv_sem=fwd_recv.at[i * 3 + j], device_id=(x, y, 1 - c), device_id_type=MESH)

    local = [pltpu.make_async_copy(bufs[i], block(i, k), loc_sem.at[i]) for i in range(n_big)]
    sends = [ici(i, j, k) for i in range(n_big) for j in range(3)]
    peers = [(i, j, 2 * qx + qy) for i in range(n_big) for j, (qx, qy) in enumerate(chips)]

    def start():
        for cp in local + sends:
            cp.start()

    def finish():
        passed = []
        for i, j, kq in peers:
            ici(i, j, kq).wait_recv()
            cp = fwd(i, j, kq, c)
            cp.start()
            passed.append(cp)
        for i, j, kq in peers:
            fwd(i, j, kq, 1 - c).wait_recv()
        for cp in sends + passed:
            cp.wait_send()
        for cp in local:
            cp.wait()

    return start, finish


def _gather_sems(n_big):
    return [pltpu.SemaphoreType.DMA((3 * n_big,))] * 4 + [pltpu.SemaphoreType.DMA((n_big,))]


def _gather_first(win_s, wout_s, wup_s, wdown_s, wa_s, wf_s):
    def body(win_r, wout_r, wup_r, wdown_r, wa_r, wf_r, win_f, wout_f, wa_g, wf_g, wup_b, wdown_b,
             b0, b1, ici_send, ici_recv, fwd_send, fwd_recv, loc_sem, cv_send, cv_recv, cv_loc):
        _, _, c, k, chips = _place()
        b0[...] = win_r[...].astype(BF16)
        b1[...] = wout_r[...].astype(BF16)
        start, finish = _gather_ops((b0, b1), (win_f, wout_f), (True, False), (ici_send, ici_recv, fwd_send, fwd_recv, loc_sem))
        start()
        taps = ((wa_r, wa_g), (wf_r, wf_g))

        def conv(t, j, kk):
            return pltpu.make_async_remote_copy(
                src_ref=taps[t][0], dst_ref=taps[t][1].at[kk], send_sem=cv_send.at[t * 3 + j], recv_sem=cv_recv.at[t * 3 + j],
                device_id=(*chips[j], c), device_id_type=MESH)

        local = [pltpu.make_async_copy(src, dst.at[k], cv_loc.at[t]) for t, (src, dst) in enumerate(taps)]
        sends = [conv(t, j, k) for t in range(2) for j in range(3)]
        for cp in local + sends:
            cp.start()
        wup_b[...] = wup_r[...].astype(BF16)
        wdown_b[...] = wdown_r[...].astype(BF16)
        finish()
        for t in range(2):
            for j, (qx, qy) in enumerate(chips):
                conv(t, j, 2 * qx + qy).wait_recv()
        for cp in sends:
            cp.wait_send()
        for cp in local:
            cp.wait()

    out_shape = (
        jax.ShapeDtypeStruct((D_MODEL, D_IN), BF16),
        jax.ShapeDtypeStruct((D_MODEL, D_MODEL), BF16),
        jax.ShapeDtypeStruct((N_CHIPS,) + wa_s.shape, F32),
        jax.ShapeDtypeStruct((N_CHIPS,) + wf_s.shape, F32),
        jax.ShapeDtypeStruct(wup_s.shape, BF16),
        jax.ShapeDtypeStruct(wdown_s.shape, BF16),
    )
    return pl.pallas_call(
        body, name="gather_first", out_shape=out_shape,
        in_specs=[VMEM] * 6, out_specs=[ANY] * 4 + [VMEM] * 2,
        scratch_shapes=[pltpu.VMEM(win_s.shape, BF16), pltpu.VMEM(wout_s.shape, BF16)] + _gather_sems(2) + [
            pltpu.SemaphoreType.DMA((6,)), pltpu.SemaphoreType.DMA((6,)), pltpu.SemaphoreType.DMA((2,))],
        compiler_params=pltpu.CompilerParams(vmem_limit_bytes=VMEM_LIMIT),
    )(win_s, wout_s, wup_s, wdown_s, wa_s, wf_s)


def _load_weights(pairs, sem):
    cps = [pltpu.make_async_copy(src, dst, sem.at[i]) for i, (src, dst) in enumerate(pairs)]
    for cp in cps:
        cp.start()
    for cp in cps:
        cp.wait()


def _shifted_views(buf, shifted, t_rows):
    n = t_rows + A_HALO - 8
    for b in range(1, 8):
        shifted[b - 1] = buf[b:b + n, :]

    def view(offset):
        a, b = divmod(offset, 8)
        if b == 0:
            return buf[8 * a:8 * a + t_rows, :]
        return shifted[b - 1, 8 * a:8 * a + t_rows, :]

    return view


def _pool_count(tile, t_rows, w):
    row = lax.broadcasted_iota(jnp.int32, (t_rows, POOL_GROUP), 0) + tile * t_rows
    return jnp.minimum(row + 1, w).astype(F32)


def _mixer_fwd(x, g1, win, wa, cb, lg, lb, pw, ps, wout, wup_b, wdown_b, tile_rows):
    seq = x.shape[0]
    tr = tile_rows
    n = seq // tr

    def body(x_ref, g1_ref, win_hbm, wa_ref, cb_ref, lg_ref, lb_ref, pw_ref, ps_ref, wout_hbm, wup_b_hbm, wdown_b_hbm,
             h1_ref, proj_ref, c_ref, d_ref, m_ref, x1_ref, wup_f, wdown_f, win_v, wout_v, ubuf, ushift, bbuf, sem, *gsems):
        i = pl.program_id(0)

        def gather():
            return _gather_ops((wup_b_hbm, wdown_b_hbm), (wup_f, wdown_f), (True, False), gsems)

        @pl.when(i == 0)
        def _():
            gather()[0]()
            _load_weights(((win_hbm, win_v), (wout_hbm, wout_v)), sem)
            ubuf[0:A_HALO, :] = jnp.zeros((A_HALO, D_CONV), F32)
            bbuf[0:P_HALO, :] = jnp.zeros((P_HALO, D_POOL), F32)

        xv = x_ref[...]
        r = lax.rsqrt(_rowmean(xv * xv) + EPS)
        h1 = (xv * r * g1_ref[...]).astype(BF16)
        h1_ref[...] = h1
        proj = _dot(h1, win_v[...])
        proj_ref[...] = proj.astype(BF16)
        av, ag, bi = proj[:, :D_CONV], proj[:, D_CONV:2 * D_CONV], proj[:, 2 * D_CONV:]
        ubuf[A_HALO:A_HALO + tr, :] = av * _sigmoid(ag)
        off = A_HALO - (CONV_A - 1)
        uview = _shifted_views(ubuf, ushift, tr)
        acc = wa_ref[0:1, :] * uview(off)
        for j in range(1, CONV_A):
            acc = acc + wa_ref[j:j + 1, :] * uview(off + j)
        cv = acc + cb_ref[...]
        ubuf[0:A_HALO, :] = ubuf[tr:tr + A_HALO, :]
        c_ref[...] = cv.astype(BF16)
        xc = cv - _rowmean(cv)
        z = xc * lax.rsqrt(_rowmean(xc * xc) + EPS)
        ln = z * lg_ref[...] + lb_ref[...]
        ya = ln * _sigmoid(ln)
        bbuf[P_HALO:P_HALO + tr, :] = bi
        ds, ybs = [], []
        for g, w in enumerate(POOL_WINDOWS):
            cols = slice(g * POOL_GROUP, (g + 1) * POOL_GROUP)
            s = bi[:, cols]
            for kk in range(1, w):
                s = s + bbuf[P_HALO - kk:P_HALO - kk + tr, cols]
            dg = s / _pool_count(i, tr, w) - bi[:, cols]
            ds.append(dg)
            ybs.append(_dot(dg.astype(BF16), pw_ref[g].astype(BF16)))
        bbuf[0:P_HALO, :] = bbuf[tr:tr + P_HALO, :]
        d_ref[...] = jnp.concatenate(ds, axis=1).astype(BF16)
        yb = jnp.concatenate(ybs, axis=1) * ps_ref[...]
        m = jnp.concatenate([ya, yb], axis=1).astype(BF16)
        m_ref[...] = m
        x1_ref[...] = xv + _dot(m, wout_v[...])

        @pl.when(i == n - 1)
        def _():
            gather()[1]()

    tile = lambda w: pl.BlockSpec((tr, w), lambda i: (i, 0))
    full = lambda a: pl.BlockSpec(a.shape, lambda i: (0,) * a.ndim)
    return pl.pallas_call(
        body, name="mixer_fwd", grid=(n,),
        in_specs=[tile(D_MODEL), full(g1), ANY, full(wa), full(cb), full(lg), full(lb), full(pw), full(ps), ANY, ANY, ANY],
        out_specs=[tile(D_MODEL), tile(D_IN), tile(D_CONV), tile(D_POOL), tile(D_MODEL), tile(D_MODEL), ANY, ANY],
        out_shape=[
            jax.ShapeDtypeStruct((seq, D_MODEL), BF16), jax.ShapeDtypeStruct((seq, D_IN), BF16),
            jax.ShapeDtypeStruct((seq, D_CONV), BF16), jax.ShapeDtypeStruct((seq, D_POOL), BF16),
            jax.ShapeDtypeStruct((seq, D_MODEL), BF16), jax.ShapeDtypeStruct((seq, D_MODEL), F32),
            jax.ShapeDtypeStruct((D_MODEL, 2 * D_FF), BF16), jax.ShapeDtypeStruct((D_FF, D_MODEL), BF16),
        ],
        scratch_shapes=[
            pltpu.VMEM(win.shape, BF16), pltpu.VMEM(wout.shape, BF16),
            pltpu.VMEM((tr + A_HALO, D_CONV), F32), pltpu.VMEM((7, tr + A_HALO - 8, D_CONV), F32),
            pltpu.VMEM((tr + P_HALO, D_POOL), F32), pltpu.SemaphoreType.DMA((2,)),
        ] + _gather_sems(2),
        compiler_params=pltpu.CompilerParams(dimension_semantics=("arbitrary",), vmem_limit_bytes=VMEM_LIMIT),
    )(x, g1, win, wa, cb, lg, lb, pw, ps, wout, wup_b, wdown_b)


def _ffn_fwd(x1, g2, wup, wf, fb, wdown, g3, target, tile_rows):
    seq = x1.shape[0]
    tr = tile_rows
    n = seq // tr

    def body(x1_ref, g2_ref, wup_hbm, wf_ref, fb_ref, wdown_hbm, g3_ref, t_ref,
             h2_ref, up_ref, act_ref, dx2_ref, dx2b_ref, sm_ref, wup_v, wdown_v, gbuf, sem):
        i = pl.program_id(0)

        @pl.when(i == 0)
        def _():
            _load_weights(((wup_hbm, wup_v), (wdown_hbm, wdown_v)), sem)
            gbuf[0:8, :] = jnp.zeros((8, D_FF), F32)
            sm_ref[...] = jnp.zeros(sm_ref.shape, F32)

        x1v = x1_ref[...]
        r2 = lax.rsqrt(_rowmean(x1v * x1v) + EPS)
        h2 = (x1v * r2 * g2_ref[...]).astype(BF16)
        h2_ref[...] = h2
        x2 = x1v

        def up_proj(j):
            return (_dot(h2, wup_v[:, j * FF_CHUNK:(j + 1) * FF_CHUNK]),
                    _dot(h2, wup_v[:, D_FF + j * FF_CHUNK:D_FF + (j + 1) * FF_CHUNK]))

        ahead = up_proj(0)
        for j in range(N_FF_CHUNKS):
            cs = slice(j * FF_CHUNK, (j + 1) * FF_CHUNK)
            vs = slice(D_FF + j * FF_CHUNK, D_FF + (j + 1) * FF_CHUNK)
            gate, val = ahead
            if j + 1 < N_FF_CHUNKS:
                ahead = up_proj(j + 1)
            up_ref[:, cs] = gate.astype(BF16)
            up_ref[:, vs] = val.astype(BF16)
            gbuf[8:8 + tr, cs] = gate
            gc = (wf_ref[0:1, cs] * gbuf[6:6 + tr, cs] + wf_ref[1:2, cs] * gbuf[7:7 + tr, cs]
                  + wf_ref[2:3, cs] * gate + fb_ref[:, cs])
            gbuf[0:8, cs] = gbuf[tr:tr + 8, cs]
            act = (gc * _sigmoid(gc) * val).astype(BF16)
            act_ref[:, cs] = act
            x2 = x2 + _dot(act, wdown_v[cs, :])
        r3 = lax.rsqrt(_rowmean(x2 * x2) + EPS)
        n3 = x2 * r3
        err = n3 * g3_ref[...] - t_ref[...]
        dy = err / D_MODEL
        sm_ref[2:3, :] += _colsum(dy * n3)
        loss = 0.5 * _colsum(_rowmean(err * err))
        sm_ref[3:4, :] += jnp.broadcast_to(loss, (1, D_MODEL))
        dn = dy * g3_ref[...]
        dx2v = r3 * (dn - n3 * _rowmean(dn * n3))
        dx2_ref[...] = dx2v
        dx2b_ref[...] = dx2v.astype(BF16)

    tile = lambda w: pl.BlockSpec((tr, w), lambda i: (i, 0))
    full = lambda a: pl.BlockSpec(a.shape, lambda i: (0,) * a.ndim)
    return pl.pallas_call(
        body, name="ffn_fwd", grid=(n,),
        in_specs=[tile(D_MODEL), full(g2), ANY, full(wf), full(fb), ANY, full(g3), tile(D_MODEL)],
        out_specs=[tile(D_MODEL), tile(2 * D_FF), tile(D_FF), tile(D_MODEL), tile(D_MODEL),
                   pl.BlockSpec((8, D_MODEL), lambda i: (0, 0))],
        out_shape=[
            jax.ShapeDtypeStruct((seq, D_MODEL), BF16), jax.ShapeDtypeStruct((seq, 2 * D_FF), BF16),
            jax.ShapeDtypeStruct((seq, D_FF), BF16), jax.ShapeDtypeStruct((seq, D_MODEL), F32),
            jax.ShapeDtypeStruct((seq, D_MODEL), BF16), jax.ShapeDtypeStruct((8, D_MODEL), F32),
        ],
        scratch_shapes=[
            pltpu.VMEM(wup.shape, BF16), pltpu.VMEM(wdown.shape, BF16),
            pltpu.VMEM((tr + 8, D_FF), F32), pltpu.SemaphoreType.DMA((2,)),
        ],
        compiler_params=pltpu.CompilerParams(dimension_semantics=("arbitrary",), vmem_limit_bytes=VMEM_LIMIT),
    )(x1, g2, wup, wf, fb, wdown, g3, target)


def _ffn_bwd(dx2, up, x1, g2, wup, wf, fb, wdown, tile_rows):
    seq = x1.shape[0]
    tr = tile_rows
    n = seq // tr

    def body(dx2_ref, up_ref, uph_ref, x1_ref, g2_ref, wup_hbm, wf_ref, fb_ref, wdown_hbm,
             dup_ref, dx1_ref, dx1b_ref, sm_ref, sf_ref, wup_v, wdown_v, gbuf, dbuf, dcar, sem):
        i = pl.program_id(0)
        tile = n - 1 - i

        @pl.when(i == 0)
        def _():
            _load_weights(((wup_hbm, wup_v), (wdown_hbm, wdown_v)), sem)
            dcar[...] = jnp.zeros(dcar.shape, F32)
            sm_ref[...] = jnp.zeros(sm_ref.shape, F32)
            sf_ref[...] = jnp.zeros(sf_ref.shape, F32)

        dx2v = dx2_ref[...]
        dx2b = dx2v.astype(BF16)
        keep = (tile > 0).astype(F32)
        dh2 = jnp.zeros((tr, D_MODEL), F32)

        def down_t(j):
            return _dot_nt(dx2b, wdown_v[j * FF_CHUNK:(j + 1) * FF_CHUNK, :])

        ahead = down_t(0)
        for j in range(N_FF_CHUNKS):
            cs = slice(j * FF_CHUNK, (j + 1) * FF_CHUNK)
            vs = slice(D_FF + j * FF_CHUNK, D_FF + (j + 1) * FF_CHUNK)
            dact = ahead
            if j + 1 < N_FF_CHUNKS:
                ahead = down_t(j + 1)
            gate = up_ref[:, cs].astype(F32)
            val = up_ref[:, vs].astype(F32)
            gbuf[0:F_HALO, :] = uph_ref[:, cs].astype(F32) * keep
            gbuf[F_HALO:F_HALO + tr, :] = gate
            g_m2 = gbuf[F_HALO - 2:F_HALO - 2 + tr, :]
            g_m1 = gbuf[F_HALO - 1:F_HALO - 1 + tr, :]
            gc = wf_ref[0:1, cs] * g_m2 + wf_ref[1:2, cs] * g_m1 + wf_ref[2:3, cs] * gate + fb_ref[:, cs]
            sg = _sigmoid(gc)
            dval = dact * (gc * sg)
            dgc = dact * val * (sg * (1.0 + gc * (1.0 - sg)))
            dbuf[0:tr, :] = dgc
            dbuf[tr:tr + 8, :] = dcar[:, cs]
            dgate = wf_ref[2:3, cs] * dgc + wf_ref[1:2, cs] * dbuf[1:1 + tr, :] + wf_ref[0:1, cs] * dbuf[2:2 + tr, :]
            dcar[:, cs] = dgc[0:8, :]
            sf_ref[0:1, cs] += _colsum(dgc * g_m2)
            sf_ref[1:2, cs] += _colsum(dgc * g_m1)
            sf_ref[2:3, cs] += _colsum(dgc * gate)
            sf_ref[3:4, cs] += _colsum(dgc)
            dgb, dvb = dgate.astype(BF16), dval.astype(BF16)
            dup_ref[:, cs] = dgb
            dup_ref[:, vs] = dvb
            dh2 = dh2 + _dot_nt(dgb, wup_v[:, cs]) + _dot_nt(dvb, wup_v[:, vs])
        x1v = x1_ref[...]
        r2 = lax.rsqrt(_rowmean(x1v * x1v) + EPS)
        n2 = x1v * r2
        sm_ref[1:2, :] += _colsum(dh2 * n2)
        dn2 = dh2 * g2_ref[...]
        dx1v = dx2v + r2 * (dn2 - n2 * _rowmean(dn2 * n2))
        dx1_ref[...] = dx1v
        dx1b_ref[...] = dx1v.astype(BF16)

    tile = lambda w: pl.BlockSpec((tr, w), lambda i: (n - 1 - i, 0))
    full = lambda a: pl.BlockSpec(a.shape, lambda i: (0,) * a.ndim)
    halo = pl.BlockSpec((F_HALO, D_FF), lambda i: (jnp.maximum((n - 1 - i) * (tr // F_HALO) - 1, 0), 0))
    acc = lambda rows, w: pl.BlockSpec((rows, w), lambda i: (0, 0))
    return pl.pallas_call(
        body, name="ffn_bwd", grid=(n,),
        in_specs=[tile(D_MODEL), tile(2 * D_FF), halo, tile(D_MODEL), full(g2), ANY, full(wf), full(fb), ANY],
        out_specs=[tile(2 * D_FF), tile(D_MODEL), tile(D_MODEL), acc(8, D_MODEL), acc(8, D_FF)],
        out_shape=[
            jax.ShapeDtypeStruct((seq, 2 * D_FF), BF16), jax.ShapeDtypeStruct((seq, D_MODEL), F32),
            jax.ShapeDtypeStruct((seq, D_MODEL), BF16), jax.ShapeDtypeStruct((8, D_MODEL), F32),
            jax.ShapeDtypeStruct((8, D_FF), F32),
        ],
        scratch_shapes=[
            pltpu.VMEM(wup.shape, BF16), pltpu.VMEM(wdown.shape, BF16),
            pltpu.VMEM((tr + F_HALO, FF_CHUNK), F32), pltpu.VMEM((tr + 8, FF_CHUNK), F32),
            pltpu.VMEM((8, D_FF), F32), pltpu.SemaphoreType.DMA((2,)),
        ],
        compiler_params=pltpu.CompilerParams(dimension_semantics=("arbitrary",), vmem_limit_bytes=VMEM_LIMIT),
    )(dx2, up, up, x1, g2, wup, wf, fb, wdown)


def _mixer_bwd(dx1, x, proj, cpre, d, g1, win, wa, lg, lb, pw, ps, wout, parts, tile_rows):
    seq = x.shape[0]
    n_parts = len(parts)
    tr = tile_rows
    n = seq // tr
    row_cb, row_lg, row_lb, row_ps = 32, 33, 34, 35

    def body(dx1_ref, x_ref, proj_ref, projh_ref, c_ref, d_ref, g1_ref, win_hbm, wa_ref, lg_ref, lb_ref, pw_ref, ps_ref,
             wout_hbm, *rest):
        part_refs, rest = rest[:n_parts], rest[n_parts:]
        dproj_ref, gx_ref, sm_ref, s5_ref, sp_ref = rest[:5]
        land_refs, rest = rest[5:5 + n_parts], rest[5 + n_parts:]
        win_v, wout_v, ubuf, ushift, dcbuf, dshift, ebuf, sem = rest[:8]
        ssems = rest[8:]
        i = pl.program_id(0)
        tile = n - 1 - i

        def scatter():
            return _scatter_ops(part_refs, land_refs, n_parts, ssems)

        @pl.when(i == 0)
        def _():
            scatter()[0]()
            _load_weights(((win_hbm, win_v), (wout_hbm, wout_v)), sem)
            dcbuf[tr:tr + A_HALO, :] = jnp.zeros((A_HALO, D_CONV), F32)
            ebuf[tr:tr + P_HALO, :] = jnp.zeros((P_HALO, D_POOL), F32)
            sm_ref[...] = jnp.zeros(sm_ref.shape, F32)
            s5_ref[...] = jnp.zeros(s5_ref.shape, F32)
            sp_ref[...] = jnp.zeros(sp_ref.shape, F32)

        dx1v = dx1_ref[...]
        dm = _dot_nt(dx1v.astype(BF16), wout_v[...])
        dya, dyb = dm[:, :D_CONV], dm[:, D_CONV:]
        dbis = []
        for g, w in enumerate(POOL_WINDOWS):
            cols = slice(g * POOL_GROUP, (g + 1) * POOL_GROUP)
            dgb = d_ref[:, cols]
            pwb = pw_ref[g].astype(BF16)
            dyg = dyb[:, cols]
            s5_ref[row_ps:row_ps + 1, cols] += _colsum(dyg * _dot(dgb, pwb))
            dqb = (dyg * ps_ref[:, cols]).astype(BF16)
            sp_ref[g] += _dot_tn(dgb, dqb)
            dd = _dot_nt(dqb, pwb)
            e = dd / _pool_count(tile, tr, w)
            ebuf[0:tr, cols] = e
            s = e
            for kk in range(1, w):
                s = s + ebuf[kk:kk + tr, cols]
            dbis.append(s - dd)
        ebuf[tr:tr + P_HALO, :] = ebuf[0:P_HALO, :]
        cv = c_ref[...].astype(F32)
        xc = cv - _rowmean(cv)
        rs = lax.rsqrt(_rowmean(xc * xc) + EPS)
        z = xc * rs
        ln = z * lg_ref[...] + lb_ref[...]
        sl = _sigmoid(ln)
        dl = dya * (sl * (1.0 + ln * (1.0 - sl)))
        s5_ref[row_lg:row_lg + 1, :] += _colsum(dl * z)
        s5_ref[row_lb:row_lb + 1, :] += _colsum(dl)
        dz = dl * lg_ref[...]
        dc = rs * (dz - _rowmean(dz) - z * _rowmean(dz * z))
        s5_ref[row_cb:row_cb + 1, :] += _colsum(dc)
        dcbuf[0:tr, :] = dc
        keep = (tile > 0).astype(F32)
        avh = projh_ref[:, :D_CONV].astype(F32)
        agh = projh_ref[:, D_CONV:].astype(F32)
        ubuf[0:A_HALO, :] = avh * _sigmoid(agh) * keep
        av = proj_ref[:, :D_CONV].astype(F32)
        ag = proj_ref[:, D_CONV:2 * D_CONV].astype(F32)
        sg = _sigmoid(ag)
        ubuf[A_HALO:A_HALO + tr, :] = av * sg
        off = A_HALO - (CONV_A - 1)
        du = wa_ref[CONV_A - 1:CONV_A, :] * dc
        dview = _shifted_views(dcbuf, dshift, tr)
        uview = _shifted_views(ubuf, ushift, tr)
        for j in range(CONV_A - 1):
            du = du + wa_ref[j:j + 1, :] * dview(CONV_A - 1 - j)
        for j in range(CONV_A):
            s5_ref[j:j + 1, :] += _colsum(dc * uview(off + j))
        dcbuf[tr:tr + A_HALO, :] = dcbuf[0:A_HALO, :]
        dav = du * sg
        dag = du * av * (sg * (1.0 - sg))
        dprojb = jnp.concatenate([dav, dag] + dbis, axis=1).astype(BF16)
        dproj_ref[...] = dprojb
        dh1 = _dot_nt(dprojb, win_v[...])
        xv = x_ref[...]
        r1 = lax.rsqrt(_rowmean(xv * xv) + EPS)
        n1 = xv * r1
        sm_ref[0:1, :] += _colsum(dh1 * n1)
        dn1 = dh1 * g1_ref[...]
        gx_ref[...] = dx1v + r1 * (dn1 - n1 * _rowmean(dn1 * n1))

        @pl.when(i == n - 1)
        def _():
            scatter()[1]()

    tile = lambda w: pl.BlockSpec((tr, w), lambda i: (n - 1 - i, 0))
    full = lambda a: pl.BlockSpec(a.shape, lambda i: (0,) * a.ndim)
    halo = pl.BlockSpec((A_HALO, 2 * D_CONV), lambda i: (jnp.maximum((n - 1 - i) * (tr // A_HALO) - 1, 0), 0))
    acc = lambda shape: pl.BlockSpec(shape, lambda i: (0,) * len(shape))
    return pl.pallas_call(
        body, name="mixer_bwd", grid=(n,),
        in_specs=[tile(D_MODEL), tile(D_MODEL), tile(D_IN), halo, tile(D_CONV), tile(D_POOL), full(g1), ANY, full(wa),
                  full(lg), full(lb), full(pw), full(ps), ANY] + [ANY] * n_parts,
        out_specs=[tile(D_IN), tile(D_MODEL), acc((8, D_MODEL)), acc((40, D_CONV)), acc(pw.shape)] + [ANY] * n_parts,
        out_shape=[
            jax.ShapeDtypeStruct((seq, D_IN), BF16), jax.ShapeDtypeStruct((seq, D_MODEL), F32),
            jax.ShapeDtypeStruct((8, D_MODEL), F32), jax.ShapeDtypeStruct((40, D_CONV), F32),
            jax.ShapeDtypeStruct(pw.shape, F32),
        ] + _scatter_shapes(parts, ()),
        scratch_shapes=[
            pltpu.VMEM(win.shape, BF16), pltpu.VMEM(wout.shape, BF16),
            pltpu.VMEM((tr + A_HALO, D_CONV), F32), pltpu.VMEM((7, tr + A_HALO - 8, D_CONV), F32),
            pltpu.VMEM((tr + A_HALO, D_CONV), F32), pltpu.VMEM((7, tr + A_HALO - 8, D_CONV), F32),
            pltpu.VMEM((tr + P_HALO, D_POOL), F32), pltpu.SemaphoreType.DMA((2,)),
        ] + _scatter_sems(n_parts),
        compiler_params=pltpu.CompilerParams(dimension_semantics=("arbitrary",), vmem_limit_bytes=VMEM_LIMIT),
    )(dx1, x, proj, proj, cpre, d, g1, win, wa, lg, lb, pw, ps, wout, *parts)


def _weight_grad(a, b, layout, k_rows):
    seq, m_dim = a.shape
    n_dim = b.shape[1]
    steps = seq // k_rows

    def store(o_ref, index, value):
        s = pl.program_id(1)

        @pl.when(s == 0)
        def _():
            o_ref[index] = value

        @pl.when(s > 0)
        def _():
            o_ref[index] += value

    if layout in ("rows1", "rows2"):
        groups = int(layout[-1])
        per_tile = N_CHIPS // groups
        rows = m_dim // N_CHIPS // 2
        a_w = m_dim // groups

        def body(a_ref, b_ref, o_ref):
            r = _dot_tn(a_ref[...], b_ref[...])
            for p in range(per_tile):
                for h in range(2):
                    store(o_ref, (p, h), r[(2 * p + h) * rows:(2 * p + h + 1) * rows, :])

        in_specs = [pl.BlockSpec((k_rows, a_w), lambda g, s: (s, g)), pl.BlockSpec((k_rows, n_dim), lambda g, s: (s, 0))]
        out_spec = pl.BlockSpec((per_tile, 2, rows, n_dim), lambda g, s: (g, 0, 0, 0))
        out_dims = (N_CHIPS, 2, rows, n_dim)
    elif layout == "cols_chip":
        groups = N_CHIPS
        rows, cols = m_dim // 2, n_dim // N_CHIPS

        def body(a_ref, b_ref, o_ref):
            r = _dot_tn(a_ref[...], b_ref[...])
            for h in range(2):
                store(o_ref, h, r[h * rows:(h + 1) * rows, :])

        in_specs = [pl.BlockSpec((k_rows, m_dim), lambda g, s: (s, 0)), pl.BlockSpec((k_rows, cols), lambda g, s: (s, g))]
        out_spec = pl.BlockSpec((None, 2, rows, cols), lambda g, s: (g, 0, 0, 0))
        out_dims = (N_CHIPS, 2, rows, cols)
    else:
        groups = 2
        rows, cols = m_dim // 2, n_dim // N_CHIPS

        def body(a_ref, b_ref, o_ref):
            r = _dot_tn(a_ref[...], b_ref[...])
            for k in range(N_CHIPS):
                store(o_ref, k, r[:, k * cols:(k + 1) * cols])

        in_specs = [pl.BlockSpec((k_rows, rows), lambda g, s: (s, g)), pl.BlockSpec((k_rows, n_dim), lambda g, s: (s, 0))]
        out_spec = pl.BlockSpec((N_CHIPS, None, rows, cols), lambda g, s: (0, g, 0, 0))
        out_dims = (N_CHIPS, 2, rows, cols)

    return pl.pallas_call(
        body, name=f"weight_grad_{layout}_{m_dim}x{n_dim}", grid=(groups, steps),
        in_specs=in_specs, out_specs=out_spec, out_shape=jax.ShapeDtypeStruct(out_dims, F32),
        compiler_params=pltpu.CompilerParams(dimension_semantics=("arbitrary", "arbitrary"), vmem_limit_bytes=VMEM_LIMIT),
    )(a, b)


def _sibling_exchange(bigs, smalls, tag):
    nb, ns = len(bigs), len(smalls)

    def body(*refs):
        ins, outs = refs[:nb + ns], refs[nb + ns:2 * (nb + ns)]
        send, recv = refs[2 * (nb + ns):]
        x, y, c, _, _ = _place()
        cps = []
        for t in range(nb + ns):
            src = ins[t].at[:, 1 - c] if t < nb else ins[t]
            cps.append(pltpu.make_async_remote_copy(
                src_ref=src, dst_ref=outs[t], send_sem=send.at[t], recv_sem=recv.at[t],
                device_id=(x, y, 1 - c), device_id_type=MESH))
        for cp in cps:
            cp.start()
        for cp in cps:
            cp.wait()

    out_shape = [jax.ShapeDtypeStruct((N_CHIPS,) + b.shape[2:], F32) for b in bigs]
    out_shape += [jax.ShapeDtypeStruct(s.shape, F32) for s in smalls]
    return pl.pallas_call(
        body, name=f"sibling_exchange_{tag}", out_shape=out_shape,
        in_specs=[ANY] * (nb + ns), out_specs=[ANY] * (nb + ns),
        scratch_shapes=[pltpu.SemaphoreType.DMA((nb + ns,)), pltpu.SemaphoreType.DMA((nb + ns,))],
    )(*bigs, *smalls)


def _pair_sum(core, mine, theirs, tag, block_rows):
    _, _, rows, cols = mine.shape
    steps = rows // block_rows

    def body(core_ref, a_ref, b_ref, o_ref):
        o_ref[...] = (a_ref[...] + b_ref[...]).astype(BF16)

    grid_spec = pltpu.PrefetchScalarGridSpec(
        num_scalar_prefetch=1, grid=(N_CHIPS, steps),
        in_specs=[pl.BlockSpec((None, None, block_rows, cols), lambda k, r, core_ref: (k, core_ref[0], r, 0)),
                  pl.BlockSpec((None, block_rows, cols), lambda k, r, core_ref: (k, r, 0))],
        out_specs=pl.BlockSpec((None, block_rows, cols), lambda k, r, core_ref: (k, r, 0)),
    )
    return pl.pallas_call(
        body, name=f"pair_sum_{tag}", grid_spec=grid_spec,
        out_shape=jax.ShapeDtypeStruct((N_CHIPS, rows, cols), BF16),
        compiler_params=pltpu.CompilerParams(dimension_semantics=("arbitrary", "arbitrary"), vmem_limit_bytes=VMEM_LIMIT),
    )(core, mine, theirs)


def _pair_sum_small(mine, theirs):
    (m_f2, m_b1, m_b2, m_sf, m_s5, m_sp) = mine

    def body(a0, a1, a2, a3, a4, a5, b0, b1, b2, b3, b4, b5, o_m, o_f, o_5, o_p):
        sm = (a0[...] + a1[...] + a2[...]) + (b0[...] + b1[...] + b2[...])
        sf = a3[...] + b3[...]
        s5 = a4[...] + b4[...]
        for h in range(2):
            o_m[h] = sm[:, h * (D_MODEL // 2):(h + 1) * (D_MODEL // 2)]
            o_f[h] = sf[:, h * (D_FF // 2):(h + 1) * (D_FF // 2)]
            o_5[h] = s5[:, h * (D_CONV // 2):(h + 1) * (D_CONV // 2)]
            for g in range(2):
                o_p[h, g] = a5[2 * h + g] + b5[2 * h + g]

    out_shape = [
        jax.ShapeDtypeStruct((2, 8, D_MODEL // 2), F32), jax.ShapeDtypeStruct((2, 8, D_FF // 2), F32),
        jax.ShapeDtypeStruct((2, 40, D_CONV // 2), F32), jax.ShapeDtypeStruct((2, 2, POOL_GROUP, POOL_GROUP), F32),
    ]
    return pl.pallas_call(body, name="pair_sum_small", out_shape=out_shape, in_specs=[VMEM] * 12, out_specs=[VMEM] * 4)(
        *mine, *theirs)


def _scatter_ops(ins, outs, n_parts, sems):
    ici_send, ici_recv, fwd_send, fwd_recv, loc_sem = sems
    nt = len(ins)
    x, y, c, k, chips = _place()

    def src_of(t, kk):
        return ins[t].at[kk] if t < n_parts else ins[t].at[c]

    def ici(t, j, kk, slot):
        return pltpu.make_async_remote_copy(
            src_ref=src_of(t, kk), dst_ref=outs[t].at[slot, c], send_sem=ici_send.at[t * 3 + j],
            recv_sem=ici_recv.at[t * 3 + j], device_id=(*chips[j], c), device_id_type=MESH)

    def fwd(t, j, q, half, src=None):
        slot = outs[t].at[q, half]
        return pltpu.make_async_remote_copy(
            src_ref=slot if src is None else src, dst_ref=slot, send_sem=fwd_send.at[t * 4 + j],
            recv_sem=fwd_recv.at[t * 4 + j], device_id=(x, y, 1 - c), device_id_type=MESH)

    local = [pltpu.make_async_copy(src_of(t, k), outs[t].at[k, c], loc_sem.at[t]) for t in range(nt)]
    peers = [(t, j, 2 * qx + qy) for t in range(nt) for j, (qx, qy) in enumerate(chips)]
    sends = [fwd(t, 3, k, c, src=src_of(t, k)) for t in range(nt)]
    sends += [ici(t, j, kq, k) for t, j, kq in peers]

    def start():
        for cp in local + sends:
            cp.start()

    def finish():
        passed = []
        for t, j, kq in peers:
            ici(t, j, kq, kq).wait_recv()
            cp = fwd(t, j, kq, c)
            cp.start()
            passed.append(cp)
        for t in range(nt):
            fwd(t, 3, k, 1 - c).wait_recv()
        for t, j, kq in peers:
            fwd(t, j, kq, 1 - c).wait_recv()
        for cp in sends + passed:
            cp.wait_send()
        for cp in local:
            cp.wait()

    return start, finish


def _scatter_sems(nt):
    return [pltpu.SemaphoreType.DMA((3 * nt,))] * 2 + [pltpu.SemaphoreType.DMA((4 * nt,))] * 2 + [pltpu.SemaphoreType.DMA((nt,))]


def _scatter_shapes(parts, smalls):
    return [jax.ShapeDtypeStruct((N_CHIPS, 2) + p.shape[1:], p.dtype) for p in tuple(parts) + tuple(smalls)]


def _chip_scatter(parts, smalls):
    nt = len(parts) + len(smalls)

    def body(*refs):
        start, finish = _scatter_ops(refs[:nt], refs[nt:2 * nt], len(parts), refs[2 * nt:])
        start()
        finish()

    return pl.pallas_call(
        body, name="chip_scatter", out_shape=_scatter_shapes(parts, smalls), in_specs=[ANY] * nt, out_specs=[ANY] * nt,
        scratch_shapes=_scatter_sems(nt),
    )(*parts, *smalls)


def _adamw(w, g, m, v):
    m = ADAM_B1 * m + (1.0 - ADAM_B1) * g
    v = ADAM_B2 * v + (1.0 - ADAM_B2) * (g * g)
    m_hat = m / (1.0 - ADAM_B1 ** ADAM_STEP)
    v_hat = v / (1.0 - ADAM_B2 ** ADAM_STEP)
    delta = -ADAM_LR * (m_hat / (jnp.sqrt(v_hat) + ADAM_EPS) + ADAM_WD * w)
    return delta, m, v


def _adam_big(parts, w, m, v, tag, block_rows):
    _, _, rows, cols = parts.shape
    steps = rows // block_rows

    def body(p_ref, w_ref, m_ref, v_ref, g_out, d_out, m_out, v_out):
        g = p_ref[0].astype(F32)
        for q in range(1, N_CHIPS):
            g = g + p_ref[q].astype(F32)
        delta, m_new, v_new = _adamw(w_ref[...], g, m_ref[...], v_ref[...])
        g_out[...] = g
        d_out[...] = delta
        m_out[...] = m_new
        v_out[...] = v_new

    blk = pl.BlockSpec((block_rows, cols), lambda h, r: (h * steps + r, 0))
    return pl.pallas_call(
        body, name=f"adam_{tag}", grid=(2, steps),
        in_specs=[pl.BlockSpec((N_CHIPS, None, block_rows, cols), lambda h, r: (0, h, r, 0)), blk, blk, blk],
        out_specs=[blk] * 4, out_shape=[jax.ShapeDtypeStruct(w.shape, F32)] * 4,
        compiler_params=pltpu.CompilerParams(dimension_semantics=("arbitrary", "arbitrary"), vmem_limit_bytes=VMEM_LIMIT),
    )(parts, w, m, v)


def _reduce_small(l_m, l_f, l_5, l_p):
    def total(ref):
        t = ref[0]
        for q in range(1, N_CHIPS):
            t = t + ref[q]
        return t

    def body(m_ref, f_ref, s_ref, p_ref, g1_o, g2_o, g3_o, loss_o, wf_o, fb_o, wa_o, cb_o, lg_o, lb_o, ps_o, pw_o):
        tm, tf, t5, tp = total(m_ref), total(f_ref), total(s_ref), total(p_ref)
        sm = jnp.concatenate([tm[0], tm[1]], axis=1)
        sf = jnp.concatenate([tf[0], tf[1]], axis=1)
        s5 = jnp.concatenate([t5[0], t5[1]], axis=1)
        g1_o[...] = sm[0:1]
        g2_o[...] = sm[1:2]
        g3_o[...] = sm[2:3]
        loss_o[...] = sm[3:4, 0:128]
        wf_o[...] = sf
        fb_o[...] = sf[3:4]
        wa_o[...] = s5[0:32]
        cb_o[...] = s5[32:33]
        lg_o[...] = s5[33:34]
        lb_o[...] = s5[34:35]
        ps_o[...] = s5[35:36]
        for h in range(2):
            for g in range(2):
                pw_o[2 * h + g] = tp[h, g]

    row = lambda w: jax.ShapeDtypeStruct((1, w), F32)
    out_shape = [row(D_MODEL), row(D_MODEL), row(D_MODEL), row(128), jax.ShapeDtypeStruct((8, D_FF), F32), row(D_FF),
                 jax.ShapeDtypeStruct((32, D_CONV), F32), row(D_CONV), row(D_CONV), row(D_CONV), row(D_POOL),
                 jax.ShapeDtypeStruct((4, POOL_GROUP, POOL_GROUP), F32)]
    return pl.pallas_call(body, name="reduce_small", out_shape=out_shape, in_specs=[VMEM] * 4, out_specs=[VMEM] * 12)(
        l_m, l_f, l_5, l_p)


def _adam_small(ws, gs, ms, vs):
    count = len(ws)

    def body(*refs):
        w_r, g_r, m_r, v_r = (refs[t * count:(t + 1) * count] for t in range(4))
        d_o, m_o, v_o = (refs[(4 + t) * count:(5 + t) * count] for t in range(3))
        for t in range(count):
            delta, m_new, v_new = _adamw(w_r[t][...], g_r[t][...], m_r[t][...], v_r[t][...])
            d_o[t][...] = delta
            m_o[t][...] = m_new
            v_o[t][...] = v_new

    out_shape = [jax.ShapeDtypeStruct(w.shape, F32) for w in ws] * 3
    outs = pl.pallas_call(body, name="adam_small", out_shape=out_shape, in_specs=[VMEM] * (4 * count),
                          out_specs=[VMEM] * (3 * count))(*ws, *gs, *ms, *vs)
    return outs[:count], outs[count:2 * count], outs[2 * count:]


MIX_TILE = 512
FFN_TILE = 256
GRAD_K = 2048


def kernel(x, norm_mix_g, w_in, conv_a_w, conv_a_b, ln_a_g, ln_a_b, pool_w, pool_scale, w_out, norm_ffn_g, w_up, conv_f_w, conv_f_b, w_down, norm_final_g, loss_target, m_norm_mix_g, m_w_in, m_conv_a_w, m_conv_a_b, m_ln_a_g, m_ln_a_b, m_pool_w, m_pool_scale, m_w_out, m_norm_ffn_g, m_w_up, m_conv_f_w, m_conv_f_b, m_w_down, m_norm_final_g, v_norm_mix_g, v_w_in, v_conv_a_w, v_conv_a_b, v_ln_a_g, v_ln_a_b, v_pool_w, v_pool_scale, v_w_out, v_norm_ffn_g, v_w_up, v_conv_f_w, v_conv_f_b, v_w_down, v_norm_final_g):
    seq = x.shape[1]
    xs, ts = x[0], loss_target[0]
    mix_tile, ffn_tile, grad_k = min(MIX_TILE, seq), min(FFN_TILE, seq), min(GRAD_K, seq)
    chip = 2 * lax.axis_index("x") + lax.axis_index("y")
    core = lax.axis_index("c").astype(jnp.int32).reshape(1)

    wa_s = jnp.pad(conv_a_w[0], ((0, 32 - CONV_A), (0, 0)))
    wf_s = jnp.pad(conv_f_w[0], ((0, 8 - CONV_F), (0, 0)))
    win, wout, wa_g, wf_g, wup_b, wdown_b = _gather_first(w_in[0], w_out[0], w_up[0], w_down[0], wa_s, wf_s)
    wa = jnp.transpose(wa_g, (1, 0, 2)).reshape(32, D_CONV)
    wf = jnp.transpose(wf_g, (1, 0, 2)).reshape(8, D_FF)
    g3 = norm_final_g.reshape(1, D_MODEL)
    pw = pool_w[0]

    h1, proj, cpre, dpool, mcat, x1, wup, wdown = _mixer_fwd(
        xs, norm_mix_g, win, wa, conv_a_b, ln_a_g, ln_a_b, pw, pool_scale, wout, wup_b, wdown_b, mix_tile)
    h2, up, act, dx2, dx2b, sm_f2 = _ffn_fwd(x1, norm_ffn_g, wup, wf, conv_f_b, wdown, g3, ts, ffn_tile)
    g_wdown = _weight_grad(act, dx2b, "rows2", grad_k)
    dup, dx1, dx1b, sm_b1, sf = _ffn_bwd(dx2, up, x1, norm_ffn_g, wup, wf, conv_f_b, wdown, ffn_tile)
    g_wup = _weight_grad(h2, dup, "cols_chip", grad_k)
    g_wout = _weight_grad(mcat, dx1b, "rows1", grad_k)
    tags = ("w_in", "w_out", "w_up", "w_down")
    blocks = (128, 128, 128, 176)
    early = (g_wout, g_wup, g_wdown)
    landed = _sibling_exchange(early, (), "early")
    early_parts = [_pair_sum(core, b, l, tag, br) for b, l, tag, br in zip(early, landed, tags[1:], blocks[1:])]
    dproj, grad_x, sm_b2, s5, sp, s_wout, s_wup, s_wdown = _mixer_bwd(
        dx1, xs, proj, cpre, dpool, norm_mix_g, win, wa, ln_a_g, ln_a_b, pw, pool_scale, wout, early_parts, mix_tile)
    g_win = _weight_grad(h1, dproj, "cols_half", grad_k)

    smalls = (sm_f2, sm_b1, sm_b2, sf, s5, sp)
    landed = _sibling_exchange((g_win,), smalls, "late")
    part_win = _pair_sum(core, g_win, landed[0], tags[0], blocks[0])
    small_parts = _pair_sum_small(smalls, landed[1:])
    late = _chip_scatter([part_win], small_parts)
    scattered = [late[0], s_wout, s_wup, s_wdown] + list(late[1:])

    big_w = (w_in[0], w_out[0], w_up[0], w_down[0])
    big_m = (m_w_in[0], m_w_out[0], m_w_up[0], m_w_down[0])
    big_v = (v_w_in[0], v_w_out[0], v_w_up[0], v_w_down[0])
    big = {}
    for tag, p, w, m, v, br in zip(tags, scattered[:4], big_w, big_m, big_v, blocks):
        big[tag] = [a[None] for a in _adam_big(p, w, m, v, tag, br)]

    (g_g1, g_g2, g_g3, loss_row, g_wf_all, g_fb, g_wa_all, g_cb, g_lg, g_lb, g_ps, g_pw) = _reduce_small(*scattered[4:])
    g_wa = lax.dynamic_slice(g_wa_all, (0, chip * (D_CONV // N_CHIPS)), (32, D_CONV // N_CHIPS))[:CONV_A]
    g_wf = lax.dynamic_slice(g_wf_all, (0, chip * (D_FF // N_CHIPS)), (8, D_FF // N_CHIPS))[:CONV_F]
    small_names = ("norm_mix_g", "conv_a_w", "conv_a_b", "ln_a_g", "ln_a_b", "pool_w", "pool_scale", "norm_ffn_g",
                   "conv_f_w", "conv_f_b", "norm_final_g")
    small_w = (norm_mix_g, conv_a_w[0], conv_a_b, ln_a_g, ln_a_b, pw, pool_scale, norm_ffn_g, conv_f_w[0], conv_f_b, g3)
    small_m = (m_norm_mix_g, m_conv_a_w[0], m_conv_a_b, m_ln_a_g, m_ln_a_b, m_pool_w[0], m_pool_scale, m_norm_ffn_g,
               m_conv_f_w[0], m_conv_f_b, m_norm_final_g.reshape(1, D_MODEL))
    small_v = (v_norm_mix_g, v_conv_a_w[0], v_conv_a_b, v_ln_a_g, v_ln_a_b, v_pool_w[0], v_pool_scale, v_norm_ffn_g,
               v_conv_f_w[0], v_conv_f_b, v_norm_final_g.reshape(1, D_MODEL))
    small_g = (g_g1, g_wa, g_cb, g_lg, g_lb, g_pw, g_ps, g_g2, g_wf, g_fb, g_g3)
    s_delta, s_m, s_v = _adam_small(small_w, small_g, small_m, small_v)
    shapes = {"conv_a_w": conv_a_w.shape, "pool_w": pool_w.shape, "conv_f_w": conv_f_w.shape, "norm_final_g": norm_final_g.shape}
    small = {}
    for t, name in enumerate(small_names):
        shp = shapes.get(name)
        small[name] = [a if shp is None else a.reshape(shp) for a in (small_g[t], s_delta[t], s_m[t], s_v[t])]

    order = ("norm_mix_g", "w_in", "conv_a_w", "conv_a_b", "ln_a_g", "ln_a_b", "pool_w", "pool_scale", "w_out", "norm_ffn_g",
             "w_up", "conv_f_w", "conv_f_b", "w_down", "norm_final_g")
    table = {**big, **small}
    loss = loss_row[0, 0]
    outs = [loss, grad_x[None]]
    for t in range(4):
        outs += [table[name][t] for name in order]
    return tuple(outs)
```

```python
import functools

import jax
import jax.numpy as jnp
from jax import lax
from jax.experimental import pallas as pl
from jax.experimental.pallas import tpu as pltpu

F32 = jnp.float32
BF16 = jnp.bfloat16
EPS = 1e-6
ADAM_LR = 0.001
ADAM_B1 = 0.9
ADAM_B2 = 0.999
ADAM_EPS = 1e-08
ADAM_WD = 0.01
ADAM_STEP = 10

D_MODEL = 1024
D_CONV = 512
D_POOL = 512
D_IN = 1536
D_FF = 2816
CONV_A = 31
CONV_F = 3
POOL_WINDOWS = (2, 4, 8, 16)
POOL_GROUP = 128
N_CHIPS = 4
FF_CHUNK = 256
N_FF_CHUNKS = D_FF // FF_CHUNK
A_HALO = 32
P_HALO = 16
VMEM_LIMIT = 56 * 1024 * 1024
MESH = pl.DeviceIdType.MESH

ANY = pl.BlockSpec(memory_space=pl.ANY)
VMEM = pl.BlockSpec(memory_space=pltpu.VMEM)


def _dot(a, b):
    return jnp.dot(a, b, preferred_element_type=F32)


def _dot_nt(a, b):
    return lax.dot_general(a, b, (((1,), (1,)), ((), ())), preferred_element_type=F32)


def _dot_tn(a, b):
    return lax.dot_general(a, b, (((0,), (0,)), ((), ())), preferred_element_type=F32)


def _sigmoid(v):
    return jax.nn.sigmoid(v)


def _colsum(v):
    return jnp.sum(v, axis=0, keepdims=True)


def _rowmean(v):
    return jnp.mean(v, axis=-1, keepdims=True)


def _place():
    x, y, c = lax.axis_index("x"), lax.axis_index("y"), lax.axis_index("c")
    chips = [(1 - x, y), (x, 1 - y), (1 - x, 1 - y)]
    return x, y, c, 2 * x + y, chips


def _gather_ops(bufs, fulls, col_sharded, sems):
    ici_send, ici_recv, fwd_send, fwd_recv, loc_sem = sems
    n_big = len(bufs)
    x, y, c, k, chips = _place()

    def block(i, kk, half=None):
        rows, cols = bufs[i].shape
        if col_sharded[i]:
            rs = slice(None) if half is None else pl.ds(pl.multiple_of(half * (rows // 2), 16), rows // 2)
            return fulls[i].at[rs, pl.ds(pl.multiple_of(kk * cols, 128), cols)]
        if half is None:
            return fulls[i].at[pl.ds(pl.multiple_of(kk * rows, 16), rows), :]
        return fulls[i].at[pl.ds(pl.multiple_of(kk * rows + half * (rows // 2), 16), rows // 2), :]

    def my_half(i):
        rows = bufs[i].shape[0]
        return bufs[i].at[pl.ds(pl.multiple_of(c * (rows // 2), 16), rows // 2), :]

    def ici(i, j, kk):
        return pltpu.make_async_remote_copy(
            src_ref=my_half(i), dst_ref=block(i, kk, c), send_sem=ici_send.at[i * 3 + j], recv_sem=ici_recv.at[i * 3 + j],
            device_id=(*chips[j], c), device_id_type=MESH)

    def fwd(i, j, kk, half):
        return pltpu.make_async_remote_copy(
            src_ref=block(i, kk, half), dst_ref=block(i, kk, half), send_sem=fwd_send.at[i * 3 + j],
            recv_sem=fwd_recv.at[i * 3 + j], device_id=(x, y, 1 - c), device_id_type=MESH)

    local = [pltpu.make_async_copy(bufs[i], block(i, k), loc_sem.at[i]) for i in range(n_big)]
    sends = [ici(i, j, k) for i in range(n_big) for j in range(3)]
    peers = [(i, j, 2 * qx + qy) for i in range(n_big) for j, (qx, qy) in enumerate(chips)]

    def start():
        for cp in local + sends:
            cp.start()

    def finish():
        passed = []
        for i, j, kq in peers:
            ici(i, j, kq).wait_recv()
            cp = fwd(i, j, kq, c)
            cp.start()
            passed.append(cp)
        for i, j, kq in peers:
            fwd(i, j, kq, 1 - c).wait_recv()
        for cp in sends + passed:
            cp.wait_send()
        for cp in local:
            cp.wait()

    return start, finish


def _gather_sems(n_big):
    return [pltpu.SemaphoreType.DMA((3 * n_big,))] * 4 + [pltpu.SemaphoreType.DMA((n_big,))]


def _gather_first(win_s, wout_s, wup_s, wdown_s, wa_s, wf_s):
    def body(win_r, wout_r, wup_r, wdown_r, wa_r, wf_r, win_f, wout_f, wa_g, wf_g, wup_b, wdown_b,
             b0, b1, ici_send, ici_recv, fwd_send, fwd_recv, loc_sem, cv_send, cv_recv, cv_loc):
        _, _, c, k, chips = _place()
        b0[...] = win_r[...].astype(BF16)
        b1[...] = wout_r[...].astype(BF16)
        start, finish = _gather_ops((b0, b1), (win_f, wout_f), (True, False), (ici_send, ici_recv, fwd_send, fwd_recv, loc_sem))
        start()
        taps = ((wa_r, wa_g), (wf_r, wf_g))

        def conv(t, j, kk):
            return pltpu.make_async_remote_copy(
                src_ref=taps[t][0], dst_ref=taps[t][1].at[kk], send_sem=cv_send.at[t * 3 + j], recv_sem=cv_recv.at[t * 3 + j],
                device_id=(*chips[j], c), device_id_type=MESH)

        local = [pltpu.make_async_copy(src, dst.at[k], cv_loc.at[t]) for t, (src, dst) in enumerate(taps)]
        sends = [conv(t, j, k) for t in range(2) for j in range(3)]
        for cp in local + sends:
            cp.start()
        wup_b[...] = wup_r[...].astype(BF16)
        wdown_b[...] = wdown_r[...].astype(BF16)
        finish()
        for t in range(2):
            for j, (qx, qy) in enumerate(chips):
                conv(t, j, 2 * qx + qy).wait_recv()
        for cp in sends:
            cp.wait_send()
        for cp in local:
            cp.wait()

    out_shape = (
        jax.ShapeDtypeStruct((D_MODEL, D_IN), BF16),
        jax.ShapeDtypeStruct((D_MODEL, D_MODEL), BF16),
        jax.ShapeDtypeStruct((N_CHIPS,) + wa_s.shape, F32),
        jax.ShapeDtypeStruct((N_CHIPS,) + wf_s.shape, F32),
        jax.ShapeDtypeStruct(wup_s.shape, BF16),
        jax.ShapeDtypeStruct(wdown_s.shape, BF16),
    )
    return pl.pallas_call(
        body, name="gather_first", out_shape=out_shape,
        in_specs=[VMEM] * 6, out_specs=[ANY] * 4 + [VMEM] * 2,
        scratch_shapes=[pltpu.VMEM(win_s.shape, BF16), pltpu.VMEM(wout_s.shape, BF16)] + _gather_sems(2) + [
            pltpu.SemaphoreType.DMA((6,)), pltpu.SemaphoreType.DMA((6,)), pltpu.SemaphoreType.DMA((2,))],
        compiler_params=pltpu.CompilerParams(vmem_limit_bytes=VMEM_LIMIT),
    )(win_s, wout_s, wup_s, wdown_s, wa_s, wf_s)


def _load_weights(pairs, sem):
    cps = [pltpu.make_async_copy(src, dst, sem.at[i]) for i, (src, dst) in enumerate(pairs)]
    for cp in cps:
        cp.start()
    for cp in cps:
        cp.wait()


def _shifted_views(buf, shifted, t_rows):
    n = t_rows + A_HALO - 8
    for b in range(1, 8):
        shifted[b - 1] = buf[b:b + n, :]

    def view(offset):
        a, b = divmod(offset, 8)
        if b == 0:
            return buf[8 * a:8 * a + t_rows, :]
        return shifted[b - 1, 8 * a:8 * a + t_rows, :]

    return view


def _pool_count(tile, t_rows, w):
    row = lax.broadcasted_iota(jnp.int32, (t_rows, POOL_GROUP), 0) + tile * t_rows
    return jnp.minimum(row + 1, w).astype(F32)


def _mixer_fwd(x, g1, win, wa, cb, lg, lb, pw, ps, wout, wup_b, wdown_b, tile_rows):
    seq = x.shape[0]
    tr = tile_rows
    n = seq // tr

    def body(x_ref, g1_ref, win_hbm, wa_ref, cb_ref, lg_ref, lb_ref, pw_ref, ps_ref, wout_hbm, wup_b_hbm, wdown_b_hbm,
             h1_ref, proj_ref, c_ref, d_ref, m_ref, x1_ref, wup_f, wdown_f, win_v, wout_v, ubuf, ushift, bbuf, sem, *gsems):
        i = pl.program_id(0)

        def gather():
            return _gather_ops((wup_b_hbm, wdown_b_hbm), (wup_f, wdown_f), (True, False), gsems)

        @pl.when(i == 0)
        def _():
            gather()[0]()
            _load_weights(((win_hbm, win_v), (wout_hbm, wout_v)), sem)
            ubuf[0:A_HALO, :] = jnp.zeros((A_HALO, D_CONV), F32)
            bbuf[0:P_HALO, :] = jnp.zeros((P_HALO, D_POOL), F32)

        xv = x_ref[...]
        r = lax.rsqrt(_rowmean(xv * xv) + EPS)
        h1 = (xv * r * g1_ref[...]).astype(BF16)
        h1_ref[...] = h1
        proj = _dot(h1, win_v[...])
        proj_ref[...] = proj.astype(BF16)
        av, ag, bi = proj[:, :D_CONV], proj[:, D_CONV:2 * D_CONV], proj[:, 2 * D_CONV:]
        ubuf[A_HALO:A_HALO + tr, :] = av * _sigmoid(ag)
        off = A_HALO - (CONV_A - 1)
        uview = _shifted_views(ubuf, ushift, tr)
        acc = wa_ref[0:1, :] * uview(off)
        for j in range(1, CONV_A):
            acc = acc + wa_ref[j:j + 1, :] * uview(off + j)
        cv = acc + cb_ref[...]
        ubuf[0:A_HALO, :] = ubuf[tr:tr + A_HALO, :]
        c_ref[...] = cv.astype(BF16)
        xc = cv - _rowmean(cv)
        z = xc * lax.rsqrt(_rowmean(xc * xc) + EPS)
        ln = z * lg_ref[...] + lb_ref[...]
        ya = ln * _sigmoid(ln)
        bbuf[P_HALO:P_HALO + tr, :] = bi
        ds, ybs = [], []
        for g, w in enumerate(POOL_WINDOWS):
            cols = slice(g * POOL_GROUP, (g + 1) * POOL_GROUP)
            s = bi[:, cols]
            for kk in range(1, w):
                s = s + bbuf[P_HALO - kk:P_HALO - kk + tr, cols]
            dg = s / _pool_count(i, tr, w) - bi[:, cols]
            ds.append(dg)
            ybs.append(_dot(dg.astype(BF16), pw_ref[g].astype(BF16)))
        bbuf[0:P_HALO, :] = bbuf[tr:tr + P_HALO, :]
        d_ref[...] = jnp.concatenate(ds, axis=1).astype(BF16)
        yb = jnp.concatenate(ybs, axis=1) * ps_ref[...]
        m = jnp.concatenate([ya, yb], axis=1).astype(BF16)
        m_ref[...] = m
        x1_ref[...] = xv + _dot(m, wout_v[...])

        @pl.when(i == n - 1)
        def _():
            gather()[1]()

    tile = lambda w: pl.BlockSpec((tr, w), lambda i: (i, 0))
    full = lambda a: pl.BlockSpec(a.shape, lambda i: (0,) * a.ndim)
    return pl.pallas_call(
        body, name="mixer_fwd", grid=(n,),
        in_specs=[tile(D_MODEL), full(g1), ANY, full(wa), full(cb), full(lg), full(lb), full(pw), full(ps), ANY, ANY, ANY],
        out_specs=[tile(D_MODEL), tile(D_IN), tile(D_CONV), tile(D_POOL), tile(D_MODEL), tile(D_MODEL), ANY, ANY],
        out_shape=[
            jax.ShapeDtypeStruct((seq, D_MODEL), BF16), jax.ShapeDtypeStruct((seq, D_IN), BF16),
            jax.ShapeDtypeStruct((seq, D_CONV), BF16), jax.ShapeDtypeStruct((seq, D_POOL), BF16),
            jax.ShapeDtypeStruct((seq, D_MODEL), BF16), jax.ShapeDtypeStruct((seq, D_MODEL), F32),
            jax.ShapeDtypeStruct((D_MODEL, 2 * D_FF), BF16), jax.ShapeDtypeStruct((D_FF, D_MODEL), BF16),
        ],
        scratch_shapes=[
            pltpu.VMEM(win.shape, BF16), pltpu.VMEM(wout.shape, BF16),
            pltpu.VMEM((tr + A_HALO, D_CONV), F32), pltpu.VMEM((7, tr + A_HALO - 8, D_CONV), F32),
            pltpu.VMEM((tr + P_HALO, D_POOL), F32), pltpu.SemaphoreType.DMA((2,)),
        ] + _gather_sems(2),
        compiler_params=pltpu.CompilerParams(dimension_semantics=("arbitrary",), vmem_limit_bytes=VMEM_LIMIT),
    )(x, g1, win, wa, cb, lg, lb, pw, ps, wout, wup_b, wdown_b)


def _ffn_fwd(x1, g2, wup, wf, fb, wdown, g3, target, tile_rows):
    seq = x1.shape[0]
    tr = tile_rows
    n = seq // tr

    def body(x1_ref, g2_ref, wup_hbm, wf_ref, fb_ref, wdown_hbm, g3_ref, t_ref,
             h2_ref, up_ref, gc_ref, act_ref, dx2_ref, dx2b_ref, sm_ref, wup_v, wdown_v, gbuf, sem):
        i = pl.program_id(0)

        @pl.when(i == 0)
        def _():
            _load_weights(((wup_hbm, wup_v), (wdown_hbm, wdown_v)), sem)
            gbuf[0:8, :] = jnp.zeros((8, D_FF), F32)
            sm_ref[...] = jnp.zeros(sm_ref.shape, F32)

        x1v = x1_ref[...]
        r2 = lax.rsqrt(_rowmean(x1v * x1v) + EPS)
        h2 = (x1v * r2 * g2_ref[...]).astype(BF16)
        h2_ref[...] = h2
        x2 = x1v

        def up_proj(j):
            return (_dot(h2, wup_v[:, j * FF_CHUNK:(j + 1) * FF_CHUNK]),
                    _dot(h2, wup_v[:, D_FF + j * FF_CHUNK:D_FF + (j + 1) * FF_CHUNK]))

        ahead = up_proj(0)
        for j in range(N_FF_CHUNKS):
            cs = slice(j * FF_CHUNK, (j + 1) * FF_CHUNK)
            vs = slice(D_FF + j * FF_CHUNK, D_FF + (j + 1) * FF_CHUNK)
            gate, val = ahead
            if j + 1 < N_FF_CHUNKS:
                ahead = up_proj(j + 1)
            up_ref[:, cs] = gate.astype(BF16)
            up_ref[:, vs] = val.astype(BF16)
            gbuf[8:8 + tr, cs] = gate
            gc = (wf_ref[0:1, cs] * gbuf[6:6 + tr, cs] + wf_ref[1:2, cs] * gbuf[7:7 + tr, cs]
                  + wf_ref[2:3, cs] * gate + fb_ref[:, cs])
            gbuf[0:8, cs] = gbuf[tr:tr + 8, cs]
            gc_ref[:, cs] = gc.astype(BF16)
            act = (gc * _sigmoid(gc) * val).astype(BF16)
            act_ref[:, cs] = act
            x2 = x2 + _dot(act, wdown_v[cs, :])
        r3 = lax.rsqrt(_rowmean(x2 * x2) + EPS)
        n3 = x2 * r3
        err = n3 * g3_ref[...] - t_ref[...]
        dy = err / D_MODEL
        sm_ref[2:3, :] += _colsum(dy * n3)
        loss = 0.5 * _colsum(_rowmean(err * err))
        sm_ref[3:4, :] += jnp.broadcast_to(loss, (1, D_MODEL))
        dn = dy * g3_ref[...]
        dx2v = r3 * (dn - n3 * _rowmean(dn * n3))
        dx2_ref[...] = dx2v
        dx2b_ref[...] = dx2v.astype(BF16)

    tile = lambda w: pl.BlockSpec((tr, w), lambda i: (i, 0))
    full = lambda a: pl.BlockSpec(a.shape, lambda i: (0,) * a.ndim)
    return pl.pallas_call(
        body, name="ffn_fwd", grid=(n,),
        in_specs=[tile(D_MODEL), full(g2), ANY, full(wf), full(fb), ANY, full(g3), tile(D_MODEL)],
        out_specs=[tile(D_MODEL), tile(2 * D_FF), tile(D_FF), tile(D_FF), tile(D_MODEL), tile(D_MODEL),
                   pl.BlockSpec((8, D_MODEL), lambda i: (0, 0))],
        out_shape=[
            jax.ShapeDtypeStruct((seq, D_MODEL), BF16), jax.ShapeDtypeStruct((seq, 2 * D_FF), BF16),
            jax.ShapeDtypeStruct((seq, D_FF), BF16), jax.ShapeDtypeStruct((seq, D_FF), BF16),
            jax.ShapeDtypeStruct((seq, D_MODEL), F32),
            jax.ShapeDtypeStruct((seq, D_MODEL), BF16), jax.ShapeDtypeStruct((8, D_MODEL), F32),
        ],
        scratch_shapes=[
            pltpu.VMEM(wup.shape, BF16), pltpu.VMEM(wdown.shape, BF16),
            pltpu.VMEM((tr + 8, D_FF), F32), pltpu.SemaphoreType.DMA((2,)),
        ],
        compiler_params=pltpu.CompilerParams(dimension_semantics=("arbitrary",), vmem_limit_bytes=VMEM_LIMIT),
    )(x1, g2, wup, wf, fb, wdown, g3, target)


def _ffn_bwd(dx2, up, gcs, x1, g2, wup, wf, wdown, tile_rows):
    seq = x1.shape[0]
    tr = tile_rows
    n = seq // tr

    def body(dx2_ref, up_ref, gc_ref, x1_ref, g2_ref, wup_hbm, wf_ref, wdown_hbm,
             dup_ref, dx1_ref, dx1b_ref, sm_ref, sf_ref, wup_v, wdown_v, dbuf, dcar, sem):
        i = pl.program_id(0)

        @pl.when(i == 0)
        def _():
            _load_weights(((wup_hbm, wup_v), (wdown_hbm, wdown_v)), sem)
            dcar[...] = jnp.zeros(dcar.shape, F32)
            sm_ref[...] = jnp.zeros(sm_ref.shape, F32)
            sf_ref[...] = jnp.zeros(sf_ref.shape, F32)

        dx2v = dx2_ref[...]
        dx2b = dx2v.astype(BF16)
        dh2 = jnp.zeros((tr, D_MODEL), F32)

        def down_t(j):
            return _dot_nt(dx2b, wdown_v[j * FF_CHUNK:(j + 1) * FF_CHUNK, :])

        ahead = down_t(0)
        for j in range(N_FF_CHUNKS):
            cs = slice(j * FF_CHUNK, (j + 1) * FF_CHUNK)
            vs = slice(D_FF + j * FF_CHUNK, D_FF + (j + 1) * FF_CHUNK)
            dact = ahead
            if j + 1 < N_FF_CHUNKS:
                ahead = down_t(j + 1)
            gate = up_ref[:, cs].astype(F32)
            val = up_ref[:, vs].astype(F32)
            gc = gc_ref[:, cs].astype(F32)
            sg = _sigmoid(gc)
            dval = dact * (gc * sg)
            dgc = dact * val * (sg * (1.0 + gc * (1.0 - sg)))
            dbuf[0:tr, :] = dgc
            dbuf[tr:tr + 8, :] = dcar[:, cs]
            d_p1 = dbuf[1:1 + tr, :]
            d_p2 = dbuf[2:2 + tr, :]
            dgate = wf_ref[2:3, cs] * dgc + wf_ref[1:2, cs] * d_p1 + wf_ref[0:1, cs] * d_p2
            dcar[:, cs] = dgc[0:8, :]
            sf_ref[0:1, cs] += _colsum(d_p2 * gate)
            sf_ref[1:2, cs] += _colsum(d_p1 * gate)
            sf_ref[2:3, cs] += _colsum(dgc * gate)
            sf_ref[3:4, cs] += _colsum(dgc)
            dgb, dvb = dgate.astype(BF16), dval.astype(BF16)
            dup_ref[:, cs] = dgb
            dup_ref[:, vs] = dvb
            dh2 = dh2 + _dot_nt(dgb, wup_v[:, cs]) + _dot_nt(dvb, wup_v[:, vs])
        x1v = x1_ref[...]
        r2 = lax.rsqrt(_rowmean(x1v * x1v) + EPS)
        n2 = x1v * r2
        sm_ref[1:2, :] += _colsum(dh2 * n2)
        dn2 = dh2 * g2_ref[...]
        dx1v = dx2v + r2 * (dn2 - n2 * _rowmean(dn2 * n2))
        dx1_ref[...] = dx1v
        dx1b_ref[...] = dx1v.astype(BF16)

    tile = lambda w: pl.BlockSpec((tr, w), lambda i: (n - 1 - i, 0))
    full = lambda a: pl.BlockSpec(a.shape, lambda i: (0,) * a.ndim)
    acc = lambda rows, w: pl.BlockSpec((rows, w), lambda i: (0, 0))
    return pl.pallas_call(
        body, name="ffn_bwd", grid=(n,),
        in_specs=[tile(D_MODEL), tile(2 * D_FF), tile(D_FF), tile(D_MODEL), full(g2), ANY, full(wf), ANY],
        out_specs=[tile(2 * D_FF), tile(D_MODEL), tile(D_MODEL), acc(8, D_MODEL), acc(8, D_FF)],
        out_shape=[
            jax.ShapeDtypeStruct((seq, 2 * D_FF), BF16), jax.ShapeDtypeStruct((seq, D_MODEL), F32),
            jax.ShapeDtypeStruct((seq, D_MODEL), BF16), jax.ShapeDtypeStruct((8, D_MODEL), F32),
            jax.ShapeDtypeStruct((8, D_FF), F32),
        ],
        scratch_shapes=[
            pltpu.VMEM(wup.shape, BF16), pltpu.VMEM(wdown.shape, BF16),
            pltpu.VMEM((tr + 8, FF_CHUNK), F32), pltpu.VMEM((8, D_FF), F32), pltpu.SemaphoreType.DMA((2,)),
        ],
        compiler_params=pltpu.CompilerParams(dimension_semantics=("arbitrary",), vmem_limit_bytes=VMEM_LIMIT),
    )(dx2, up, gcs, x1, g2, wup, wf, wdown)


def _mixer_bwd(dx1, x, proj, cpre, d, g1, win, wa, lg, lb, pw, ps, wout, parts, tile_rows):
    seq = x.shape[0]
    n_parts = len(parts)
    tr = tile_rows
    n = seq // tr
    row_cb, row_lg, row_lb, row_ps = 32, 33, 34, 35

    def body(dx1_ref, x_ref, proj_ref, projh_ref, c_ref, d_ref, g1_ref, win_hbm, wa_ref, lg_ref, lb_ref, pw_ref, ps_ref,
             wout_hbm, *rest):
        part_refs, rest = rest[:n_parts], rest[n_parts:]
        dproj_ref, gx_ref, sm_ref, s5_ref, sp_ref = rest[:5]
        land_refs, rest = rest[5:5 + n_parts], rest[5 + n_parts:]
        win_v, wout_v, ubuf, ushift, dcbuf, dshift, ebuf, sem = rest[:8]
        ssems = rest[8:]
        i = pl.program_id(0)
        tile = n - 1 - i

        def scatter():
            return _scatter_ops(part_refs, land_refs, n_parts, ssems)

        @pl.when(i == 0)
        def _():
            scatter()[0]()
            _load_weights(((win_hbm, win_v), (wout_hbm, wout_v)), sem)
            dcbuf[tr:tr + A_HALO, :] = jnp.zeros((A_HALO, D_CONV), F32)
            ebuf[tr:tr + P_HALO, :] = jnp.zeros((P_HALO, D_POOL), F32)
            sm_ref[...] = jnp.zeros(sm_ref.shape, F32)
            s5_ref[...] = jnp.zeros(s5_ref.shape, F32)
            sp_ref[...] = jnp.zeros(sp_ref.shape, F32)

        dx1v = dx1_ref[...]
        dm = _dot_nt(dx1v.astype(BF16), wout_v[...])
        dya, dyb = dm[:, :D_CONV], dm[:, D_CONV:]
        dbis = []
        for g, w in enumerate(POOL_WINDOWS):
            cols = slice(g * POOL_GROUP, (g + 1) * POOL_GROUP)
            dgb = d_ref[:, cols]
            pwb = pw_ref[g].astype(BF16)
            dyg = dyb[:, cols]
            s5_ref[row_ps:row_ps + 1, cols] += _colsum(dyg * _dot(dgb, pwb))
            dqb = (dyg * ps_ref[:, cols]).astype(BF16)
            sp_ref[g] += _dot_tn(dgb, dqb)
            dd = _dot_nt(dqb, pwb)
            e = dd / _pool_count(tile, tr, w)
            ebuf[0:tr, cols] = e
            s = e
            for kk in range(1, w):
                s = s + ebuf[kk:kk + tr, cols]
            dbis.append(s - dd)
        ebuf[tr:tr + P_HALO, :] = ebuf[0:P_HALO, :]
        cv = c_ref[...].astype(F32)
        xc = cv - _rowmean(cv)
        rs = lax.rsqrt(_rowmean(xc * xc) + EPS)
        z = xc * rs
        ln = z * lg_ref[...] + lb_ref[...]
        sl = _sigmoid(ln)
        dl = dya * (sl * (1.0 + ln * (1.0 - sl)))
        s5_ref[row_lg:row_lg + 1, :] += _colsum(dl * z)
        s5_ref[row_lb:row_lb + 1, :] += _colsum(dl)
        dz = dl * lg_ref[...]
        dc = rs * (dz - _rowmean(dz) - z * _rowmean(dz * z))
        s5_ref[row_cb:row_cb + 1, :] += _colsum(dc)
        dcbuf[0:tr, :] = dc
        keep = (tile > 0).astype(F32)
        avh = projh_ref[:, :D_CONV].astype(F32)
        agh = projh_ref[:, D_CONV:].astype(F32)
        ubuf[0:A_HALO, :] = avh * _sigmoid(agh) * keep
        av = proj_ref[:, :D_CONV].astype(F32)
        ag = proj_ref[:, D_CONV:2 * D_CONV].astype(F32)
        sg = _sigmoid(ag)
        ubuf[A_HALO:A_HALO + tr, :] = av * sg
        off = A_HALO - (CONV_A - 1)
        du = wa_ref[CONV_A - 1:CONV_A, :] * dc
        dview = _shifted_views(dcbuf, dshift, tr)
        uview = _shifted_views(ubuf, ushift, tr)
        for j in range(CONV_A - 1):
            du = du + wa_ref[j:j + 1, :] * dview(CONV_A - 1 - j)
        for j in range(CONV_A):
            s5_ref[j:j + 1, :] += _colsum(dc * uview(off + j))
        dcbuf[tr:tr + A_HALO, :] = dcbuf[0:A_HALO, :]
        dav = du * sg
        dag = du * av * (sg * (1.0 - sg))
        dprojb = jnp.concatenate([dav, dag] + dbis, axis=1).astype(BF16)
        dproj_ref[...] = dprojb
        dh1 = _dot_nt(dprojb, win_v[...])
        xv = x_ref[...]
        r1 = lax.rsqrt(_rowmean(xv * xv) + EPS)
        n1 = xv * r1
        sm_ref[0:1, :] += _colsum(dh1 * n1)
        dn1 = dh1 * g1_ref[...]
        gx_ref[...] = dx1v + r1 * (dn1 - n1 * _rowmean(dn1 * n1))

        @pl.when(i == n - 1)
        def _():
            scatter()[1]()

    tile = lambda w: pl.BlockSpec((tr, w), lambda i: (n - 1 - i, 0))
    full = lambda a: pl.BlockSpec(a.shape, lambda i: (0,) * a.ndim)
    halo = pl.BlockSpec((A_HALO, 2 * D_CONV), lambda i: (jnp.maximum((n - 1 - i) * (tr // A_HALO) - 1, 0), 0))
    acc = lambda shape: pl.BlockSpec(shape, lambda i: (0,) * len(shape))
    return pl.pallas_call(
        body, name="mixer_bwd", grid=(n,),
        in_specs=[tile(D_MODEL), tile(D_MODEL), tile(D_IN), halo, tile(D_CONV), tile(D_POOL), full(g1), ANY, full(wa),
                  full(lg), full(lb), full(pw), full(ps), ANY] + [ANY] * n_parts,
        out_specs=[tile(D_IN), tile(D_MODEL), acc((8, D_MODEL)), acc((40, D_CONV)), acc(pw.shape)] + [ANY] * n_parts,
        out_shape=[
            jax.ShapeDtypeStruct((seq, D_IN), BF16), jax.ShapeDtypeStruct((seq, D_MODEL), F32),
            jax.ShapeDtypeStruct((8, D_MODEL), F32), jax.ShapeDtypeStruct((40, D_CONV), F32),
            jax.ShapeDtypeStruct(pw.shape, F32),
        ] + _scatter_shapes(parts, ()),
        scratch_shapes=[
            pltpu.VMEM(win.shape, BF16), pltpu.VMEM(wout.shape, BF16),
            pltpu.VMEM((tr + A_HALO, D_CONV), F32), pltpu.VMEM((7, tr + A_HALO - 8, D_CONV), F32),
            pltpu.VMEM((tr + A_HALO, D_CONV), F32), pltpu.VMEM((7, tr + A_HALO - 8, D_CONV), F32),
            pltpu.VMEM((tr + P_HALO, D_POOL), F32), pltpu.SemaphoreType.DMA((2,)),
        ] + _scatter_sems(n_parts),
        compiler_params=pltpu.CompilerParams(dimension_semantics=("arbitrary",), vmem_limit_bytes=VMEM_LIMIT),
    )(dx1, x, proj, proj, cpre, d, g1, win, wa, lg, lb, pw, ps, wout, *parts)


def _weight_grad(a, b, layout, k_rows):
    seq, m_dim = a.shape
    n_dim = b.shape[1]
    steps = seq // k_rows

    def store(o_ref, acc, index, value):
        if steps == 1:
            o_ref[index] = value.astype(BF16)
            return
        s = pl.program_id(1)

        @pl.when(s == 0)
        def _():
            acc[index] = value

        @pl.when(jnp.logical_and(s > 0, s < steps - 1))
        def _():
            acc[index] += value

        @pl.when(s == steps - 1)
        def _():
            o_ref[index] = (acc[index] + value).astype(BF16)

    if layout in ("rows1", "rows2"):
        groups = int(layout[-1])
        per_tile = N_CHIPS // groups
        rows = m_dim // N_CHIPS // 2
        a_w = m_dim // groups

        def body(a_ref, b_ref, o_ref, acc):
            r = _dot_tn(a_ref[...], b_ref[...])
            for p in range(per_tile):
                for h in range(2):
                    store(o_ref, acc, (p, h), r[(2 * p + h) * rows:(2 * p + h + 1) * rows, :])

        in_specs = [pl.BlockSpec((k_rows, a_w), lambda g, s: (s, g)), pl.BlockSpec((k_rows, n_dim), lambda g, s: (s, 0))]
        out_spec = pl.BlockSpec((per_tile, 2, rows, n_dim), lambda g, s: (g, 0, 0, 0))
        out_dims, acc_dims = (N_CHIPS, 2, rows, n_dim), (per_tile, 2, rows, n_dim)
    elif layout == "cols_chip":
        groups = N_CHIPS
        rows, cols = m_dim // 2, n_dim // N_CHIPS

        def body(a_ref, b_ref, o_ref, acc):
            r = _dot_tn(a_ref[...], b_ref[...])
            for h in range(2):
                store(o_ref, acc, h, r[h * rows:(h + 1) * rows, :])

        in_specs = [pl.BlockSpec((k_rows, m_dim), lambda g, s: (s, 0)), pl.BlockSpec((k_rows, cols), lambda g, s: (s, g))]
        out_spec = pl.BlockSpec((None, 2, rows, cols), lambda g, s: (g, 0, 0, 0))
        out_dims, acc_dims = (N_CHIPS, 2, rows, cols), (2, rows, cols)
    else:
        groups = 2
        rows, cols = m_dim // 2, n_dim // N_CHIPS

        def body(a_ref, b_ref, o_ref, acc):
            r = _dot_tn(a_ref[...], b_ref[...])
            for k in range(N_CHIPS):
                store(o_ref, acc, k, r[:, k * cols:(k + 1) * cols])

        in_specs = [pl.BlockSpec((k_rows, rows), lambda g, s: (s, g)), pl.BlockSpec((k_rows, n_dim), lambda g, s: (s, 0))]
        out_spec = pl.BlockSpec((N_CHIPS, None, rows, cols), lambda g, s: (0, g, 0, 0))
        out_dims, acc_dims = (N_CHIPS, 2, rows, cols), (N_CHIPS, rows, cols)

    return pl.pallas_call(
        body, name=f"weight_grad_{layout}_{m_dim}x{n_dim}", grid=(groups, steps),
        in_specs=in_specs, out_specs=out_spec, out_shape=jax.ShapeDtypeStruct(out_dims, BF16),
        scratch_shapes=[pltpu.VMEM(acc_dims, F32)],
        compiler_params=pltpu.CompilerParams(dimension_semantics=("arbitrary", "arbitrary"), vmem_limit_bytes=VMEM_LIMIT),
    )(a, b)


def _sibling_exchange(bigs, smalls, tag):
    nb, ns = len(bigs), len(smalls)

    def body(*refs):
        ins, outs = refs[:nb + ns], refs[nb + ns:2 * (nb + ns)]
        send, recv = refs[2 * (nb + ns):]
        x, y, c, _, _ = _place()
        cps = []
        for t in range(nb + ns):
            src = ins[t].at[:, 1 - c] if t < nb else ins[t]
            cps.append(pltpu.make_async_remote_copy(
                src_ref=src, dst_ref=outs[t], send_sem=send.at[t], recv_sem=recv.at[t],
                device_id=(x, y, 1 - c), device_id_type=MESH))
        for cp in cps:
            cp.start()
        for cp in cps:
            cp.wait()

    out_shape = [jax.ShapeDtypeStruct((N_CHIPS,) + b.shape[2:], b.dtype) for b in bigs]
    out_shape += [jax.ShapeDtypeStruct(s.shape, F32) for s in smalls]
    return pl.pallas_call(
        body, name=f"sibling_exchange_{tag}", out_shape=out_shape,
        in_specs=[ANY] * (nb + ns), out_specs=[ANY] * (nb + ns),
        scratch_shapes=[pltpu.SemaphoreType.DMA((nb + ns,)), pltpu.SemaphoreType.DMA((nb + ns,))],
    )(*bigs, *smalls)


def _pair_sum(core, mine, theirs, tag, block_rows):
    _, _, rows, cols = mine.shape
    steps = rows // block_rows

    def body(core_ref, a_ref, b_ref, o_ref):
        o_ref[...] = (a_ref[...].astype(F32) + b_ref[...].astype(F32)).astype(BF16)

    grid_spec = pltpu.PrefetchScalarGridSpec(
        num_scalar_prefetch=1, grid=(N_CHIPS, steps),
        in_specs=[pl.BlockSpec((None, None, block_rows, cols), lambda k, r, core_ref: (k, core_ref[0], r, 0)),
                  pl.BlockSpec((None, block_rows, cols), lambda k, r, core_ref: (k, r, 0))],
        out_specs=pl.BlockSpec((None, block_rows, cols), lambda k, r, core_ref: (k, r, 0)),
    )
    return pl.pallas_call(
        body, name=f"pair_sum_{tag}", grid_spec=grid_spec,
        out_shape=jax.ShapeDtypeStruct((N_CHIPS, rows, cols), BF16),
        compiler_params=pltpu.CompilerParams(dimension_semantics=("arbitrary", "arbitrary"), vmem_limit_bytes=VMEM_LIMIT),
    )(core, mine, theirs)


def _pair_sum_small(mine, theirs):
    (m_f2, m_b1, m_b2, m_sf, m_s5, m_sp) = mine

    def body(a0, a1, a2, a3, a4, a5, b0, b1, b2, b3, b4, b5, o_m, o_f, o_5, o_p):
        sm = (a0[...] + a1[...] + a2[...]) + (b0[...] + b1[...] + b2[...])
        sf = a3[...] + b3[...]
        s5 = a4[...] + b4[...]
        for h in range(2):
            o_m[h] = sm[:, h * (D_MODEL // 2):(h + 1) * (D_MODEL // 2)]
            o_f[h] = sf[:, h * (D_FF // 2):(h + 1) * (D_FF // 2)]
            o_5[h] = s5[:, h * (D_CONV // 2):(h + 1) * (D_CONV // 2)]
            for g in range(2):
                o_p[h, g] = a5[2 * h + g] + b5[2 * h + g]

    out_shape = [
        jax.ShapeDtypeStruct((2, 8, D_MODEL // 2), F32), jax.ShapeDtypeStruct((2, 8, D_FF // 2), F32),
        jax.ShapeDtypeStruct((2, 40, D_CONV // 2), F32), jax.ShapeDtypeStruct((2, 2, POOL_GROUP, POOL_GROUP), F32),
    ]
    return pl.pallas_call(body, name="pair_sum_small", out_shape=out_shape, in_specs=[VMEM] * 12, out_specs=[VMEM] * 4)(
        *mine, *theirs)


def _scatter_ops(ins, outs, n_parts, sems):
    ici_send, ici_recv, fwd_send, fwd_recv, loc_sem = sems
    nt = len(ins)
    x, y, c, k, chips = _place()

    def src_of(t, kk):
        return ins[t].at[kk] if t < n_parts else ins[t].at[c]

    def ici(t, j, kk, slot):
        return pltpu.make_async_remote_copy(
            src_ref=src_of(t, kk), dst_ref=outs[t].at[slot, c], send_sem=ici_send.at[t * 3 + j],
            recv_sem=ici_recv.at[t * 3 + j], device_id=(*chips[j], c), device_id_type=MESH)

    def fwd(t, j, q, half, src=None):
        slot = outs[t].at[q, half]
        return pltpu.make_async_remote_copy(
            src_ref=slot if src is None else src, dst_ref=slot, send_sem=fwd_send.at[t * 4 + j],
            recv_sem=fwd_recv.at[t * 4 + j], device_id=(x, y, 1 - c), device_id_type=MESH)

    local = [pltpu.make_async_copy(src_of(t, k), outs[t].at[k, c], loc_sem.at[t]) for t in range(nt)]
    peers = [(t, j, 2 * qx + qy) for t in range(nt) for j, (qx, qy) in enumerate(chips)]
    sends = [fwd(t, 3, k, c, src=src_of(t, k)) for t in range(nt)]
    sends += [ici(t, j, kq, k) for t, j, kq in peers]

    def start():
        for cp in local + sends:
            cp.start()

    def finish():
        passed = []
        for t, j, kq in peers:
            ici(t, j, kq, kq).wait_recv()
            cp = fwd(t, j, kq, c)
            cp.start()
            passed.append(cp)
        for t in range(nt):
            fwd(t, 3, k, 1 - c).wait_recv()
        for t, j, kq in peers:
            fwd(t, j, kq, 1 - c).wait_recv()
        for cp in sends + passed:
            cp.wait_send()
        for cp in local:
            cp.wait()

    return start, finish


def _scatter_sems(nt):
    return [pltpu.SemaphoreType.DMA((3 * nt,))] * 2 + [pltpu.SemaphoreType.DMA((4 * nt,))] * 2 + [pltpu.SemaphoreType.DMA((nt,))]


def _scatter_shapes(parts, smalls):
    return [jax.ShapeDtypeStruct((N_CHIPS, 2) + p.shape[1:], p.dtype) for p in tuple(parts) + tuple(smalls)]


def _chip_scatter(parts, smalls):
    nt = len(parts) + len(smalls)

    def body(*refs):
        start, finish = _scatter_ops(refs[:nt], refs[nt:2 * nt], len(parts), refs[2 * nt:])
        start()
        finish()

    return pl.pallas_call(
        body, name="chip_scatter", out_shape=_scatter_shapes(parts, smalls), in_specs=[ANY] * nt, out_specs=[ANY] * nt,
        scratch_shapes=_scatter_sems(nt),
    )(*parts, *smalls)


def _adamw(w, g, m, v):
    m = ADAM_B1 * m + (1.0 - ADAM_B1) * g
    v = ADAM_B2 * v + (1.0 - ADAM_B2) * (g * g)
    m_hat = m / (1.0 - ADAM_B1 ** ADAM_STEP)
    v_hat = v / (1.0 - ADAM_B2 ** ADAM_STEP)
    delta = -ADAM_LR * (m_hat / (jnp.sqrt(v_hat) + ADAM_EPS) + ADAM_WD * w)
    return delta, m, v


def _adam_big(parts, w, m, v, tag, block_rows):
    _, _, rows, cols = parts.shape
    steps = rows // block_rows

    def body(p_ref, w_ref, m_ref, v_ref, g_out, d_out, m_out, v_out):
        g = p_ref[0].astype(F32)
        for q in range(1, N_CHIPS):
            g = g + p_ref[q].astype(F32)
        delta, m_new, v_new = _adamw(w_ref[...], g, m_ref[...], v_ref[...])
        g_out[...] = g
        d_out[...] = delta
        m_out[...] = m_new
        v_out[...] = v_new

    blk = pl.BlockSpec((block_rows, cols), lambda h, r: (h * steps + r, 0))
    return pl.pallas_call(
        body, name=f"adam_{tag}", grid=(2, steps),
        in_specs=[pl.BlockSpec((N_CHIPS, None, block_rows, cols), lambda h, r: (0, h, r, 0)), blk, blk, blk],
        out_specs=[blk] * 4, out_shape=[jax.ShapeDtypeStruct(w.shape, F32)] * 4,
        compiler_params=pltpu.CompilerParams(dimension_semantics=("arbitrary", "arbitrary"), vmem_limit_bytes=VMEM_LIMIT),
    )(parts, w, m, v)


def _reduce_small(l_m, l_f, l_5, l_p):
    def total(ref):
        t = ref[0]
        for q in range(1, N_CHIPS):
            t = t + ref[q]
        return t

    def body(m_ref, f_ref, s_ref, p_ref, g1_o, g2_o, g3_o, loss_o, wf_o, fb_o, wa_o, cb_o, lg_o, lb_o, ps_o, pw_o):
        tm, tf, t5, tp = total(m_ref), total(f_ref), total(s_ref), total(p_ref)
        sm = jnp.concatenate([tm[0], tm[1]], axis=1)
        sf = jnp.concatenate([tf[0], tf[1]], axis=1)
        s5 = jnp.concatenate([t5[0], t5[1]], axis=1)
        g1_o[...] = sm[0:1]
        g2_o[...] = sm[1:2]
        g3_o[...] = sm[2:3]
        loss_o[...] = sm[3:4, 0:128]
        wf_o[...] = sf
        fb_o[...] = sf[3:4]
        wa_o[...] = s5[0:32]
        cb_o[...] = s5[32:33]
        lg_o[...] = s5[33:34]
        lb_o[...] = s5[34:35]
        ps_o[...] = s5[35:36]
        for h in range(2):
            for g in range(2):
                pw_o[2 * h + g] = tp[h, g]

    row = lambda w: jax.ShapeDtypeStruct((1, w), F32)
    out_shape = [row(D_MODEL), row(D_MODEL), row(D_MODEL), row(128), jax.ShapeDtypeStruct((8, D_FF), F32), row(D_FF),
                 jax.ShapeDtypeStruct((32, D_CONV), F32), row(D_CONV), row(D_CONV), row(D_CONV), row(D_POOL),
                 jax.ShapeDtypeStruct((4, POOL_GROUP, POOL_GROUP), F32)]
    return pl.pallas_call(body, name="reduce_small", out_shape=out_shape, in_specs=[VMEM] * 4, out_specs=[VMEM] * 12)(
        l_m, l_f, l_5, l_p)


def _adam_small(ws, gs, ms, vs):
    count = len(ws)

    def body(*refs):
        w_r, g_r, m_r, v_r = (refs[t * count:(t + 1) * count] for t in range(4))
        d_o, m_o, v_o = (refs[(4 + t) * count:(5 + t) * count] for t in range(3))
        for t in range(count):
            delta, m_new, v_new = _adamw(w_r[t][...], g_r[t][...], m_r[t][...], v_r[t][...])
            d_o[t][...] = delta
            m_o[t][...] = m_new
            v_o[t][...] = v_new

    out_shape = [jax.ShapeDtypeStruct(w.shape, F32) for w in ws] * 3
    outs = pl.pallas_call(body, name="adam_small", out_shape=out_shape, in_specs=[VMEM] * (4 * count),
                          out_specs=[VMEM] * (3 * count))(*ws, *gs, *ms, *vs)
    return outs[:count], outs[count:2 * count], outs[2 * count:]


MIX_TILE = 512
FFN_TILE = 256
GRAD_K = 2048


def kernel(x, norm_mix_g, w_in, conv_a_w, conv_a_b, ln_a_g, ln_a_b, pool_w, pool_scale, w_out, norm_ffn_g, w_up, conv_f_w, conv_f_b, w_down, norm_final_g, loss_target, m_norm_mix_g, m_w_in, m_conv_a_w, m_conv_a_b, m_ln_a_g, m_ln_a_b, m_pool_w, m_pool_scale, m_w_out, m_norm_ffn_g, m_w_up, m_conv_f_w, m_conv_f_b, m_w_down, m_norm_final_g, v_norm_mix_g, v_w_in, v_conv_a_w, v_conv_a_b, v_ln_a_g, v_ln_a_b, v_pool_w, v_pool_scale, v_w_out, v_norm_ffn_g, v_w_up, v_conv_f_w, v_conv_f_b, v_w_down, v_norm_final_g):
    seq = x.shape[1]
    xs, ts = x[0], loss_target[0]
    mix_tile, ffn_tile, grad_k = min(MIX_TILE, seq), min(FFN_TILE, seq), min(GRAD_K, seq)
    chip = 2 * lax.axis_index("x") + lax.axis_index("y")
    core = lax.axis_index("c").astype(jnp.int32).reshape(1)

    wa_s = jnp.pad(conv_a_w[0], ((0, 32 - CONV_A), (0, 0)))
    wf_s = jnp.pad(conv_f_w[0], ((0, 8 - CONV_F), (0, 0)))
    win, wout, wa_g, wf_g, wup_b, wdown_b = _gather_first(w_in[0], w_out[0], w_up[0], w_down[0], wa_s, wf_s)
    wa = jnp.transpose(wa_g, (1, 0, 2)).reshape(32, D_CONV)
    wf = jnp.transpose(wf_g, (1, 0, 2)).reshape(8, D_FF)
    g3 = norm_final_g.reshape(1, D_MODEL)
    pw = pool_w[0]

    h1, proj, cpre, dpool, mcat, x1, wup, wdown = _mixer_fwd(
        xs, norm_mix_g, win, wa, conv_a_b, ln_a_g, ln_a_b, pw, pool_scale, wout, wup_b, wdown_b, mix_tile)
    h2, up, gcs, act, dx2, dx2b, sm_f2 = _ffn_fwd(x1, norm_ffn_g, wup, wf, conv_f_b, wdown, g3, ts, ffn_tile)
    g_wdown = _weight_grad(act, dx2b, "rows2", grad_k)
    dup, dx1, dx1b, sm_b1, sf = _ffn_bwd(dx2, up, gcs, x1, norm_ffn_g, wup, wf, wdown, ffn_tile)
    g_wup = _weight_grad(h2, dup, "cols_chip", grad_k)
    g_wout = _weight_grad(mcat, dx1b, "rows1", grad_k)
    tags = ("w_in", "w_out", "w_up", "w_down")
    blocks = (256, 128, 256, 176)
    early = (g_wout, g_wup, g_wdown)
    landed = _sibling_exchange(early, (), "early")
    early_parts = [_pair_sum(core, b, l, tag, br) for b, l, tag, br in zip(early, landed, tags[1:], blocks[1:])]
    dproj, grad_x, sm_b2, s5, sp, s_wout, s_wup, s_wdown = _mixer_bwd(
        dx1, xs, proj, cpre, dpool, norm_mix_g, win, wa, ln_a_g, ln_a_b, pw, pool_scale, wout, early_parts, mix_tile)
    g_win = _weight_grad(h1, dproj, "cols_half", grad_k)

    smalls = (sm_f2, sm_b1, sm_b2, sf, s5, sp)
    landed = _sibling_exchange((g_win,), smalls, "late")
    part_win = _pair_sum(core, g_win, landed[0], tags[0], blocks[0])
    small_parts = _pair_sum_small(smalls, landed[1:])
    late = _chip_scatter([part_win], small_parts)
    scattered = [late[0], s_wout, s_wup, s_wdown] + list(late[1:])

    big_w = (w_in[0], w_out[0], w_up[0], w_down[0])
    big_m = (m_w_in[0], m_w_out[0], m_w_up[0], m_w_down[0])
    big_v = (v_w_in[0], v_w_out[0], v_w_up[0], v_w_down[0])
    big = {}
    for tag, p, w, m, v, br in zip(tags, scattered[:4], big_w, big_m, big_v, blocks):
        big[tag] = [a[None] for a in _adam_big(p, w, m, v, tag, br)]

    (g_g1, g_g2, g_g3, loss_row, g_wf_all, g_fb, g_wa_all, g_cb, g_lg, g_lb, g_ps, g_pw) = _reduce_small(*scattered[4:])
    g_wa = lax.dynamic_slice(g_wa_all, (0, chip * (D_CONV // N_CHIPS)), (32, D_CONV // N_CHIPS))[:CONV_A]
    g_wf = lax.dynamic_slice(g_wf_all, (0, chip * (D_FF // N_CHIPS)), (8, D_FF // N_CHIPS))[:CONV_F]
    small_names = ("norm_mix_g", "conv_a_w", "conv_a_b", "ln_a_g", "ln_a_b", "pool_w", "pool_scale", "norm_ffn_g",
                   "conv_f_w", "conv_f_b", "norm_final_g")
    small_w = (norm_mix_g, conv_a_w[0], conv_a_b, ln_a_g, ln_a_b, pw, pool_scale, norm_ffn_g, conv_f_w[0], conv_f_b, g3)
    small_m = (m_norm_mix_g, m_conv_a_w[0], m_conv_a_b, m_ln_a_g, m_ln_a_b, m_pool_w[0], m_pool_scale, m_norm_ffn_g,
               m_conv_f_w[0], m_conv_f_b, m_norm_final_g.reshape(1, D_MODEL))
    small_v = (v_norm_mix_g, v_conv_a_w[0], v_conv_a_b, v_ln_a_g, v_ln_a_b, v_pool_w[0], v_pool_scale, v_norm_ffn_g,
               v_conv_f_w[0], v_conv_f_b, v_norm_final_g.reshape(1, D_MODEL))
    small_g = (g_g1, g_wa, g_cb, g_lg, g_lb, g_pw, g_ps, g_g2, g_wf, g_fb, g_g3)
    s_delta, s_m, s_v = _adam_small(small_w, small_g, small_m, small_v)
    shapes = {"conv_a_w": conv_a_w.shape, "pool_w": pool_w.shape, "conv_f_w": conv_f_w.shape, "norm_final_g": norm_final_g.shape}
    small = {}
    for t, name in enumerate(small_names):
        shp = shapes.get(name)
        small[name] = [a if shp is None else a.reshape(shp) for a in (small_g[t], s_delta[t], s_m[t], s_v[t])]

    order = ("norm_mix_g", "w_in", "conv_a_w", "conv_a_b", "ln_a_g", "ln_a_b", "pool_w", "pool_scale", "w_out", "norm_ffn_g",
             "w_up", "conv_f_w", "conv_f_b", "w_down", "norm_final_g")
    table = {**big, **small}
    loss = loss_row[0, 0]
    outs = [loss, grad_x[None]]
    for t in range(4):
        outs += [table[name][t] for name in order]
    return tuple(outs)
```

```python
import functools

import jax
import jax.numpy as jnp
from jax import lax
from jax.experimental import pallas as pl
from jax.experimental.pallas import tpu as pltpu

F32 = jnp.float32
BF16 = jnp.bfloat16
EPS = 1e-6
ADAM_LR = 0.001
ADAM_B1 = 0.9
ADAM_B2 = 0.999
ADAM_EPS = 1e-08
ADAM_WD = 0.01
ADAM_STEP = 10

D_MODEL = 1024
D_CONV = 512
D_POOL = 512
D_IN = 1536
D_FF = 2816
CONV_A = 31
CONV_F = 3
POOL_WINDOWS = (2, 4, 8, 16)
POOL_GROUP = 128
N_CHIPS = 4
FF_CHUNK = 256
N_FF_CHUNKS = D_FF // FF_CHUNK
A_HALO = 32
P_HALO = 16
VMEM_LIMIT = 56 * 1024 * 1024
MESH = pl.DeviceIdType.MESH

ANY = pl.BlockSpec(memory_space=pl.ANY)
VMEM = pl.BlockSpec(memory_space=pltpu.VMEM)


def _dot(a, b):
    return jnp.dot(a, b, preferred_element_type=F32)


def _dot_nt(a, b):
    return lax.dot_general(a, b, (((1,), (1,)), ((), ())), preferred_element_type=F32)


def _dot_tn(a, b):
    return lax.dot_general(a, b, (((0,), (0,)), ((), ())), preferred_element_type=F32)


def _sigmoid(v):
    return jax.nn.sigmoid(v)


def _colsum(v):
    return jnp.sum(v, axis=0, keepdims=True)


def _rowmean(v):
    return jnp.mean(v, axis=-1, keepdims=True)


def _place():
    x, y, c = lax.axis_index("x"), lax.axis_index("y"), lax.axis_index("c")
    chips = [(1 - x, y), (x, 1 - y), (1 - x, 1 - y)]
    return x, y, c, 2 * x + y, chips


def _gather_ops(bufs, fulls, col_sharded, sems):
    ici_send, ici_recv, fwd_send, fwd_recv, loc_sem = sems
    n_big = len(bufs)
    x, y, c, k, chips = _place()

    def block(i, kk, half=None):
        rows, cols = bufs[i].shape
        if col_sharded[i]:
            rs = slice(None) if half is None else pl.ds(pl.multiple_of(half * (rows // 2), 16), rows // 2)
            return fulls[i].at[rs, pl.ds(pl.multiple_of(kk * cols, 128), cols)]
        if half is None:
            return fulls[i].at[pl.ds(pl.multiple_of(kk * rows, 16), rows), :]
        return fulls[i].at[pl.ds(pl.multiple_of(kk * rows + half * (rows // 2), 16), rows // 2), :]

    def my_half(i):
        rows = bufs[i].shape[0]
        return bufs[i].at[pl.ds(pl.multiple_of(c * (rows // 2), 16), rows // 2), :]

    def ici(i, j, kk):
        return pltpu.make_async_remote_copy(
            src_ref=my_half(i), dst_ref=block(i, kk, c), send_sem=ici_send.at[i * 3 + j], recv_sem=ici_recv.at[i * 3 + j],
            device_id=(*chips[j], c), device_id_type=MESH)

    def fwd(i, j, kk, half):
        return pltpu.make_async_remote_copy(
            src_ref=block(i, kk, half), dst_ref=block(i, kk, half), send_sem=fwd_send.at[i * 3 + j],
            recv_sem=fwd_recv.at[i * 3 + j], device_id=(x, y, 1 - c), device_id_type=MESH)

    local = [pltpu.make_async_copy(bufs[i], block(i, k), loc_sem.at[i]) for i in range(n_big)]
    sends = [ici(i, j, k) for i in range(n_big) for j in range(3)]
    peers = [(i, j, 2 * qx + qy) for i in range(n_big) for j, (qx, qy) in enumerate(chips)]

    def start():
        for cp in local + sends:
            cp.start()

    def finish():
        passed = []
        for i, j, kq in peers:
            ici(i, j, kq).wait_recv()
            cp = fwd(i, j, kq, c)
            cp.start()
            passed.append(cp)
        for i, j, kq in peers:
            fwd(i, j, kq, 1 - c).wait_recv()
        for cp in sends + passed:
            cp.wait_send()
        for cp in local:
            cp.wait()

    return start, finish


def _gather_sems(n_big):
    return [pltpu.SemaphoreType.DMA((3 * n_big,))] * 4 + [pltpu.SemaphoreType.DMA((n_big,))]


def _tap_ops(srcs, dsts, sems):
    send, recv, loc = sems
    _, _, c, k, chips = _place()

    def copy(t, j, kk):
        return pltpu.make_async_remote_copy(
            src_ref=srcs[t], dst_ref=dsts[t].at[kk], send_sem=send.at[t * 3 + j], recv_sem=recv.at[t * 3 + j],
            device_id=(*chips[j], c), device_id_type=MESH)

    local = [pltpu.make_async_copy(srcs[t], dsts[t].at[k], loc.at[t]) for t in range(len(srcs))]
    sends = [[copy(t, j, k) for j in range(3)] for t in range(len(srcs))]

    def start():
        for t, cp in enumerate(local):
            cp.start()
            for sd in sends[t]:
                sd.start()

    def wait(t):
        for j, (qx, qy) in enumerate(chips):
            copy(t, j, 2 * qx + qy).wait_recv()
        for sd in sends[t]:
            sd.wait_send()
        local[t].wait()

    return start, wait


def _cast_shards(*shards):
    def body(*refs):
        for src, dst in zip(refs[:len(shards)], refs[len(shards):]):
            dst[...] = src[...].astype(BF16)

    return pl.pallas_call(
        body, name="cast_shards", out_shape=[jax.ShapeDtypeStruct(s.shape, BF16) for s in shards],
        in_specs=[VMEM] * len(shards), out_specs=[VMEM] * len(shards),
        compiler_params=pltpu.CompilerParams(vmem_limit_bytes=VMEM_LIMIT),
    )(*shards)


def _load_weights(pairs, sem):
    cps = [pltpu.make_async_copy(src, dst, sem.at[i]) for i, (src, dst) in enumerate(pairs)]
    for cp in cps:
        cp.start()
    for cp in cps:
        cp.wait()


def _shifted_views(buf, shifted, t_rows):
    n = t_rows + A_HALO - 8
    for b in range(1, 8):
        shifted[b - 1] = buf[b:b + n, :]

    def view(offset):
        a, b = divmod(offset, 8)
        if b == 0:
            return buf[8 * a:8 * a + t_rows, :]
        return shifted[b - 1, 8 * a:8 * a + t_rows, :]

    return view


def _pool_count(tile, t_rows, w):
    row = lax.broadcasted_iota(jnp.int32, (t_rows, POOL_GROUP), 0) + tile * t_rows
    return jnp.minimum(row + 1, w).astype(F32)


def _mixer_fwd(x, g1, win_b, wout_b, wup_b, wdown_b, wa_s, wf_s, cb, lg, lb, pw, ps, tile_rows):
    seq = x.shape[0]
    tr = tile_rows
    n = seq // tr

    def body(x_ref, g1_ref, win_b_hbm, wout_b_hbm, wup_b_hbm, wdown_b_hbm, wa_s_hbm, wf_s_hbm, cb_ref, lg_ref, lb_ref, pw_ref,
             ps_ref, h1_ref, proj_ref, c_ref, d_ref, m_ref, x1_ref, win_f, wout_f, wup_f, wdown_f, wa_g, wf_g,
             win_v, wout_v, wa_ref, ubuf, ushift, bbuf, sem, *csems):
        i = pl.program_id(0)
        first_sems, later_sems, tap_sems = csems[0:5], csems[5:10], csems[10:13]

        def first():
            return _gather_ops((win_b_hbm, wout_b_hbm), (win_f, wout_f), (True, False), first_sems)

        def later():
            return _gather_ops((wup_b_hbm, wdown_b_hbm), (wup_f, wdown_f), (True, False), later_sems)

        def taps():
            return _tap_ops((wa_s_hbm, wf_s_hbm), (wa_g, wf_g), tap_sems)

        @pl.when(i == 0)
        def _():
            first()[0]()
            taps()[0]()
            later()[0]()
            first()[1]()
            taps()[1](0)
            loads = [(win_f, win_v), (wout_f, wout_v)]
            loads += [(wa_g.at[kk], wa_ref.at[:, kk * (D_CONV // N_CHIPS):(kk + 1) * (D_CONV // N_CHIPS)]) for kk in range(N_CHIPS)]
            _load_weights(loads, sem)
            ubuf[0:A_HALO, :] = jnp.zeros((A_HALO, D_CONV), F32)
            bbuf[0:P_HALO, :] = jnp.zeros((P_HALO, D_POOL), F32)

        xv = x_ref[...]
        r = lax.rsqrt(_rowmean(xv * xv) + EPS)
        h1 = (xv * r * g1_ref[...]).astype(BF16)
        h1_ref[...] = h1
        proj = _dot(h1, win_v[...])
        proj_ref[...] = proj.astype(BF16)
        av, ag, bi = proj[:, :D_CONV], proj[:, D_CONV:2 * D_CONV], proj[:, 2 * D_CONV:]
        ubuf[A_HALO:A_HALO + tr, :] = av * _sigmoid(ag)
        off = A_HALO - (CONV_A - 1)
        uview = _shifted_views(ubuf, ushift, tr)
        acc = wa_ref[0:1, :] * uview(off)
        for j in range(1, CONV_A):
            acc = acc + wa_ref[j:j + 1, :] * uview(off + j)
        cv = acc + cb_ref[...]
        ubuf[0:A_HALO, :] = ubuf[tr:tr + A_HALO, :]
        c_ref[...] = cv.astype(BF16)
        xc = cv - _rowmean(cv)
        z = xc * lax.rsqrt(_rowmean(xc * xc) + EPS)
        ln = z * lg_ref[...] + lb_ref[...]
        ya = ln * _sigmoid(ln)
        bbuf[P_HALO:P_HALO + tr, :] = bi
        ds, ybs = [], []
        for g, w in enumerate(POOL_WINDOWS):
            cols = slice(g * POOL_GROUP, (g + 1) * POOL_GROUP)
            s = bi[:, cols]
            for kk in range(1, w):
                s = s + bbuf[P_HALO - kk:P_HALO - kk + tr, cols]
            dg = s / _pool_count(i, tr, w) - bi[:, cols]
            ds.append(dg)
            ybs.append(_dot(dg.astype(BF16), pw_ref[g].astype(BF16)))
        bbuf[0:P_HALO, :] = bbuf[tr:tr + P_HALO, :]
        d_ref[...] = jnp.concatenate(ds, axis=1).astype(BF16)
        yb = jnp.concatenate(ybs, axis=1) * ps_ref[...]
        m = jnp.concatenate([ya, yb], axis=1).astype(BF16)
        m_ref[...] = m
        x1_ref[...] = xv + _dot(m, wout_v[...])

        @pl.when(i == n - 1)
        def _():
            later()[1]()
            taps()[1](1)

    tile = lambda w: pl.BlockSpec((tr, w), lambda i: (i, 0))
    full = lambda a: pl.BlockSpec(a.shape, lambda i: (0,) * a.ndim)
    return pl.pallas_call(
        body, name="mixer_fwd", grid=(n,),
        in_specs=[tile(D_MODEL), full(g1)] + [ANY] * 6 + [full(cb), full(lg), full(lb), full(pw), full(ps)],
        out_specs=[tile(D_MODEL), tile(D_IN), tile(D_CONV), tile(D_POOL), tile(D_MODEL), tile(D_MODEL)] + [ANY] * 6,
        out_shape=[
            jax.ShapeDtypeStruct((seq, D_MODEL), BF16), jax.ShapeDtypeStruct((seq, D_IN), BF16),
            jax.ShapeDtypeStruct((seq, D_CONV), BF16), jax.ShapeDtypeStruct((seq, D_POOL), BF16),
            jax.ShapeDtypeStruct((seq, D_MODEL), BF16), jax.ShapeDtypeStruct((seq, D_MODEL), F32),
            jax.ShapeDtypeStruct((D_MODEL, D_IN), BF16), jax.ShapeDtypeStruct((D_MODEL, D_MODEL), BF16),
            jax.ShapeDtypeStruct((D_MODEL, 2 * D_FF), BF16), jax.ShapeDtypeStruct((D_FF, D_MODEL), BF16),
            jax.ShapeDtypeStruct((N_CHIPS,) + wa_s.shape, F32), jax.ShapeDtypeStruct((N_CHIPS,) + wf_s.shape, F32),
        ],
        scratch_shapes=[
            pltpu.VMEM((D_MODEL, D_IN), BF16), pltpu.VMEM((D_MODEL, D_MODEL), BF16), pltpu.VMEM((32, D_CONV), F32),
            pltpu.VMEM((tr + A_HALO, D_CONV), F32), pltpu.VMEM((7, tr + A_HALO - 8, D_CONV), F32),
            pltpu.VMEM((tr + P_HALO, D_POOL), F32), pltpu.SemaphoreType.DMA((2 + N_CHIPS,)),
        ] + _gather_sems(2) + _gather_sems(2) + [
            pltpu.SemaphoreType.DMA((6,)), pltpu.SemaphoreType.DMA((6,)), pltpu.SemaphoreType.DMA((2,))],
        compiler_params=pltpu.CompilerParams(dimension_semantics=("arbitrary",), vmem_limit_bytes=VMEM_LIMIT),
    )(x, g1, win_b, wout_b, wup_b, wdown_b, wa_s, wf_s, cb, lg, lb, pw, ps)


def _ffn_fwd(x1, g2, wup, wf, fb, wdown, g3, target, tile_rows):
    seq = x1.shape[0]
    tr = tile_rows
    n = seq // tr

    def body(x1_ref, g2_ref, wup_hbm, wf_ref, fb_ref, wdown_hbm, g3_ref, t_ref,
             h2_ref, up_ref, gc_ref, act_ref, dx2_ref, dx2b_ref, sm_ref, wup_v, wdown_v, gbuf, sem):
        i = pl.program_id(0)

        @pl.when(i == 0)
        def _():
            _load_weights(((wup_hbm, wup_v), (wdown_hbm, wdown_v)), sem)
            gbuf[0:8, :] = jnp.zeros((8, D_FF), F32)
            sm_ref[...] = jnp.zeros(sm_ref.shape, F32)

        x1v = x1_ref[...]
        r2 = lax.rsqrt(_rowmean(x1v * x1v) + EPS)
        h2 = (x1v * r2 * g2_ref[...]).astype(BF16)
        h2_ref[...] = h2
        x2 = x1v

        def up_proj(j):
            return (_dot(h2, wup_v[:, j * FF_CHUNK:(j + 1) * FF_CHUNK]),
                    _dot(h2, wup_v[:, D_FF + j * FF_CHUNK:D_FF + (j + 1) * FF_CHUNK]))

        ahead = up_proj(0)
        for j in range(N_FF_CHUNKS):
            cs = slice(j * FF_CHUNK, (j + 1) * FF_CHUNK)
            vs = slice(D_FF + j * FF_CHUNK, D_FF + (j + 1) * FF_CHUNK)
            gate, val = ahead
            if j + 1 < N_FF_CHUNKS:
                ahead = up_proj(j + 1)
            up_ref[:, cs] = gate.astype(BF16)
            up_ref[:, vs] = val.astype(BF16)
            gbuf[8:8 + tr, cs] = gate
            gc = (wf_ref[0:1, cs] * gbuf[6:6 + tr, cs] + wf_ref[1:2, cs] * gbuf[7:7 + tr, cs]
                  + wf_ref[2:3, cs] * gate + fb_ref[:, cs])
            gbuf[0:8, cs] = gbuf[tr:tr + 8, cs]
            gc_ref[:, cs] = gc.astype(BF16)
            act = (gc * _sigmoid(gc) * val).astype(BF16)
            act_ref[:, cs] = act
            x2 = x2 + _dot(act, wdown_v[cs, :])
        r3 = lax.rsqrt(_rowmean(x2 * x2) + EPS)
        n3 = x2 * r3
        err = n3 * g3_ref[...] - t_ref[...]
        dy = err / D_MODEL
        sm_ref[2:3, :] += _colsum(dy * n3)
        loss = 0.5 * _colsum(_rowmean(err * err))
        sm_ref[3:4, :] += jnp.broadcast_to(loss, (1, D_MODEL))
        dn = dy * g3_ref[...]
        dx2v = r3 * (dn - n3 * _rowmean(dn * n3))
        dx2_ref[...] = dx2v
        dx2b_ref[...] = dx2v.astype(BF16)

    tile = lambda w: pl.BlockSpec((tr, w), lambda i: (i, 0))
    full = lambda a: pl.BlockSpec(a.shape, lambda i: (0,) * a.ndim)
    return pl.pallas_call(
        body, name="ffn_fwd", grid=(n,),
        in_specs=[tile(D_MODEL), full(g2), ANY, full(wf), full(fb), ANY, full(g3), tile(D_MODEL)],
        out_specs=[tile(D_MODEL), tile(2 * D_FF), tile(D_FF), tile(D_FF), tile(D_MODEL), tile(D_MODEL),
                   pl.BlockSpec((8, D_MODEL), lambda i: (0, 0))],
        out_shape=[
            jax.ShapeDtypeStruct((seq, D_MODEL), BF16), jax.ShapeDtypeStruct((seq, 2 * D_FF), BF16),
            jax.ShapeDtypeStruct((seq, D_FF), BF16), jax.ShapeDtypeStruct((seq, D_FF), BF16),
            jax.ShapeDtypeStruct((seq, D_MODEL), F32),
            jax.ShapeDtypeStruct((seq, D_MODEL), BF16), jax.ShapeDtypeStruct((8, D_MODEL), F32),
        ],
        scratch_shapes=[
            pltpu.VMEM(wup.shape, BF16), pltpu.VMEM(wdown.shape, BF16),
            pltpu.VMEM((tr + 8, D_FF), F32), pltpu.SemaphoreType.DMA((2,)),
        ],
        compiler_params=pltpu.CompilerParams(dimension_semantics=("arbitrary",), vmem_limit_bytes=VMEM_LIMIT),
    )(x1, g2, wup, wf, fb, wdown, g3, target)


def _ffn_bwd(dx2, up, gcs, x1, g2, wup, wf, wdown, tile_rows):
    seq = x1.shape[0]
    tr = tile_rows
    n = seq // tr

    def body(dx2_ref, up_ref, gc_ref, x1_ref, g2_ref, wup_hbm, wf_ref, wdown_hbm,
             dup_ref, dx1_ref, dx1b_ref, sm_ref, sf_ref, wup_v, wdown_v, dbuf, dcar, sem):
        i = pl.program_id(0)

        @pl.when(i == 0)
        def _():
            _load_weights(((wup_hbm, wup_v), (wdown_hbm, wdown_v)), sem)
            dcar[...] = jnp.zeros(dcar.shape, F32)
            sm_ref[...] = jnp.zeros(sm_ref.shape, F32)
            sf_ref[...] = jnp.zeros(sf_ref.shape, F32)

        dx2v = dx2_ref[...]
        dx2b = dx2v.astype(BF16)
        dh2 = jnp.zeros((tr, D_MODEL), F32)

        def down_t(j):
            return _dot_nt(dx2b, wdown_v[j * FF_CHUNK:(j + 1) * FF_CHUNK, :])

        ahead = down_t(0)
        for j in range(N_FF_CHUNKS):
            cs = slice(j * FF_CHUNK, (j + 1) * FF_CHUNK)
            vs = slice(D_FF + j * FF_CHUNK, D_FF + (j + 1) * FF_CHUNK)
            dact = ahead
            if j + 1 < N_FF_CHUNKS:
                ahead = down_t(j + 1)
            gate = up_ref[:, cs].astype(F32)
            val = up_ref[:, vs].astype(F32)
            gc = gc_ref[:, cs].astype(F32)
            sg = _sigmoid(gc)
            dval = dact * (gc * sg)
            dgc = dact * val * (sg * (1.0 + gc * (1.0 - sg)))
            dbuf[0:tr, :] = dgc
            dbuf[tr:tr + 8, :] = dcar[:, cs]
            d_p1 = dbuf[1:1 + tr, :]
            d_p2 = dbuf[2:2 + tr, :]
            dgate = wf_ref[2:3, cs] * dgc + wf_ref[1:2, cs] * d_p1 + wf_ref[0:1, cs] * d_p2
            dcar[:, cs] = dgc[0:8, :]
            sf_ref[0:1, cs] += _colsum(d_p2 * gate)
            sf_ref[1:2, cs] += _colsum(d_p1 * gate)
            sf_ref[2:3, cs] += _colsum(dgc * gate)
            sf_ref[3:4, cs] += _colsum(dgc)
            dgb, dvb = dgate.astype(BF16), dval.astype(BF16)
            dup_ref[:, cs] = dgb
            dup_ref[:, vs] = dvb
            dh2 = dh2 + _dot_nt(dgb, wup_v[:, cs]) + _dot_nt(dvb, wup_v[:, vs])
        x1v = x1_ref[...]
        r2 = lax.rsqrt(_rowmean(x1v * x1v) + EPS)
        n2 = x1v * r2
        sm_ref[1:2, :] += _colsum(dh2 * n2)
        dn2 = dh2 * g2_ref[...]
        dx1v = dx2v + r2 * (dn2 - n2 * _rowmean(dn2 * n2))
        dx1_ref[...] = dx1v
        dx1b_ref[...] = dx1v.astype(BF16)

    tile = lambda w: pl.BlockSpec((tr, w), lambda i: (n - 1 - i, 0))
    full = lambda a: pl.BlockSpec(a.shape, lambda i: (0,) * a.ndim)
    acc = lambda rows, w: pl.BlockSpec((rows, w), lambda i: (0, 0))
    return pl.pallas_call(
        body, name="ffn_bwd", grid=(n,),
        in_specs=[tile(D_MODEL), tile(2 * D_FF), tile(D_FF), tile(D_MODEL), full(g2), ANY, full(wf), ANY],
        out_specs=[tile(2 * D_FF), tile(D_MODEL), tile(D_MODEL), acc(8, D_MODEL), acc(8, D_FF)],
        out_shape=[
            jax.ShapeDtypeStruct((seq, 2 * D_FF), BF16), jax.ShapeDtypeStruct((seq, D_MODEL), F32),
            jax.ShapeDtypeStruct((seq, D_MODEL), BF16), jax.ShapeDtypeStruct((8, D_MODEL), F32),
            jax.ShapeDtypeStruct((8, D_FF), F32),
        ],
        scratch_shapes=[
            pltpu.VMEM(wup.shape, BF16), pltpu.VMEM(wdown.shape, BF16),
            pltpu.VMEM((tr + 8, FF_CHUNK), F32), pltpu.VMEM((8, D_FF), F32), pltpu.SemaphoreType.DMA((2,)),
        ],
        compiler_params=pltpu.CompilerParams(dimension_semantics=("arbitrary",), vmem_limit_bytes=VMEM_LIMIT),
    )(dx2, up, gcs, x1, g2, wup, wf, wdown)


def _mixer_bwd(dx1, x, proj, cpre, d, g1, win, wa, lg, lb, pw, ps, wout, parts, tile_rows):
    seq = x.shape[0]
    n_parts = len(parts)
    tr = tile_rows
    n = seq // tr
    row_cb, row_lg, row_lb, row_ps = 32, 33, 34, 35

    def body(dx1_ref, x_ref, proj_ref, projh_ref, c_ref, d_ref, g1_ref, win_hbm, wa_ref, lg_ref, lb_ref, pw_ref, ps_ref,
             wout_hbm, *rest):
        part_refs, rest = rest[:n_parts], rest[n_parts:]
        dproj_ref, gx_ref, sm_ref, s5_ref, sp_ref = rest[:5]
        land_refs, rest = rest[5:5 + n_parts], rest[5 + n_parts:]
        win_v, wout_v, ubuf, ushift, dcbuf, dshift, ebuf, sem = rest[:8]
        ssems = rest[8:]
        i = pl.program_id(0)
        tile = n - 1 - i

        def scatter():
            return _scatter_ops(part_refs, land_refs, n_parts, ssems)

        @pl.when(i == 0)
        def _():
            scatter()[0]()
            _load_weights(((win_hbm, win_v), (wout_hbm, wout_v)), sem)
            dcbuf[tr:tr + A_HALO, :] = jnp.zeros((A_HALO, D_CONV), F32)
            ebuf[tr:tr + P_HALO, :] = jnp.zeros((P_HALO, D_POOL), F32)
            sm_ref[...] = jnp.zeros(sm_ref.shape, F32)
            s5_ref[...] = jnp.zeros(s5_ref.shape, F32)
            sp_ref[...] = jnp.zeros(sp_ref.shape, F32)

        dx1v = dx1_ref[...]
        dm = _dot_nt(dx1v.astype(BF16), wout_v[...])
        dya, dyb = dm[:, :D_CONV], dm[:, D_CONV:]
        dbis = []
        for g, w in enumerate(POOL_WINDOWS):
            cols = slice(g * POOL_GROUP, (g + 1) * POOL_GROUP)
            dgb = d_ref[:, cols]
            pwb = pw_ref[g].astype(BF16)
            dyg = dyb[:, cols]
            s5_ref[row_ps:row_ps + 1, cols] += _colsum(dyg * _dot(dgb, pwb))
            dqb = (dyg * ps_ref[:, cols]).astype(BF16)
            sp_ref[g] += _dot_tn(dgb, dqb)
            dd = _dot_nt(dqb, pwb)
            e = dd / _pool_count(tile, tr, w)
            ebuf[0:tr, cols] = e
            s = e
            for kk in range(1, w):
                s = s + ebuf[kk:kk + tr, cols]
            dbis.append(s - dd)
        ebuf[tr:tr + P_HALO, :] = ebuf[0:P_HALO, :]
        cv = c_ref[...].astype(F32)
        xc = cv - _rowmean(cv)
        rs = lax.rsqrt(_rowmean(xc * xc) + EPS)
        z = xc * rs
        ln = z * lg_ref[...] + lb_ref[...]
        sl = _sigmoid(ln)
        dl = dya * (sl * (1.0 + ln * (1.0 - sl)))
        s5_ref[row_lg:row_lg + 1, :] += _colsum(dl * z)
        s5_ref[row_lb:row_lb + 1, :] += _colsum(dl)
        dz = dl * lg_ref[...]
        dc = rs * (dz - _rowmean(dz) - z * _rowmean(dz * z))
        s5_ref[row_cb:row_cb + 1, :] += _colsum(dc)
        dcbuf[0:tr, :] = dc
        keep = (tile > 0).astype(F32)
        avh = projh_ref[:, :D_CONV].astype(F32)
        agh = projh_ref[:, D_CONV:].astype(F32)
        ubuf[0:A_HALO, :] = avh * _sigmoid(agh) * keep
        av = proj_ref[:, :D_CONV].astype(F32)
        ag = proj_ref[:, D_CONV:2 * D_CONV].astype(F32)
        sg = _sigmoid(ag)
        ubuf[A_HALO:A_HALO + tr, :] = av * sg
        off = A_HALO - (CONV_A - 1)
        du = wa_ref[CONV_A - 1:CONV_A, :] * dc
        dview = _shifted_views(dcbuf, dshift, tr)
        uview = _shifted_views(ubuf, ushift, tr)
        for j in range(CONV_A - 1):
            du = du + wa_ref[j:j + 1, :] * dview(CONV_A - 1 - j)
        for j in range(CONV_A):
            s5_ref[j:j + 1, :] += _colsum(dc * uview(off + j))
        dcbuf[tr:tr + A_HALO, :] = dcbuf[0:A_HALO, :]
        dav = du * sg
        dag = du * av * (sg * (1.0 - sg))
        dprojb = jnp.concatenate([dav, dag] + dbis, axis=1).astype(BF16)
        dproj_ref[...] = dprojb
        dh1 = _dot_nt(dprojb, win_v[...])
        xv = x_ref[...]
        r1 = lax.rsqrt(_rowmean(xv * xv) + EPS)
        n1 = xv * r1
        sm_ref[0:1, :] += _colsum(dh1 * n1)
        dn1 = dh1 * g1_ref[...]
        gx_ref[...] = dx1v + r1 * (dn1 - n1 * _rowmean(dn1 * n1))

        @pl.when(i == n - 1)
        def _():
            scatter()[1]()

    tile = lambda w: pl.BlockSpec((tr, w), lambda i: (n - 1 - i, 0))
    full = lambda a: pl.BlockSpec(a.shape, lambda i: (0,) * a.ndim)
    halo = pl.BlockSpec((A_HALO, 2 * D_CONV), lambda i: (jnp.maximum((n - 1 - i) * (tr // A_HALO) - 1, 0), 0))
    acc = lambda shape: pl.BlockSpec(shape, lambda i: (0,) * len(shape))
    return pl.pallas_call(
        body, name="mixer_bwd", grid=(n,),
        in_specs=[tile(D_MODEL), tile(D_MODEL), tile(D_IN), halo, tile(D_CONV), tile(D_POOL), full(g1), ANY, full(wa),
                  full(lg), full(lb), full(pw), full(ps), ANY] + [ANY] * n_parts,
        out_specs=[tile(D_IN), tile(D_MODEL), acc((8, D_MODEL)), acc((40, D_CONV)), acc(pw.shape)] + [ANY] * n_parts,
        out_shape=[
            jax.ShapeDtypeStruct((seq, D_IN), BF16), jax.ShapeDtypeStruct((seq, D_MODEL), F32),
            jax.ShapeDtypeStruct((8, D_MODEL), F32), jax.ShapeDtypeStruct((40, D_CONV), F32),
            jax.ShapeDtypeStruct(pw.shape, F32),
        ] + _scatter_shapes(parts, ()),
        scratch_shapes=[
            pltpu.VMEM(win.shape, BF16), pltpu.VMEM(wout.shape, BF16),
            pltpu.VMEM((tr + A_HALO, D_CONV), F32), pltpu.VMEM((7, tr + A_HALO - 8, D_CONV), F32),
            pltpu.VMEM((tr + A_HALO, D_CONV), F32), pltpu.VMEM((7, tr + A_HALO - 8, D_CONV), F32),
            pltpu.VMEM((tr + P_HALO, D_POOL), F32), pltpu.SemaphoreType.DMA((2,)),
        ] + _scatter_sems(n_parts),
        compiler_params=pltpu.CompilerParams(dimension_semantics=("arbitrary",), vmem_limit_bytes=VMEM_LIMIT),
    )(dx1, x, proj, proj, cpre, d, g1, win, wa, lg, lb, pw, ps, wout, *parts)


def _weight_grad(a, b, layout, k_rows):
    seq, m_dim = a.shape
    n_dim = b.shape[1]
    steps = seq // k_rows

    def store(o_ref, acc, index, value):
        if steps == 1:
            o_ref[index] = value.astype(BF16)
            return
        s = pl.program_id(1)

        @pl.when(s == 0)
        def _():
            acc[index] = value

        @pl.when(jnp.logical_and(s > 0, s < steps - 1))
        def _():
            acc[index] += value

        @pl.when(s == steps - 1)
        def _():
            o_ref[index] = (acc[index] + value).astype(BF16)

    if layout in ("rows1", "rows2"):
        groups = int(layout[-1])
        per_tile = N_CHIPS // groups
        rows = m_dim // N_CHIPS // 2
        a_w = m_dim // groups

        def body(a_ref, b_ref, o_ref, acc):
            r = _dot_tn(a_ref[...], b_ref[...])
            for p in range(per_tile):
                for h in range(2):
                    store(o_ref, acc, (p, h), r[(2 * p + h) * rows:(2 * p + h + 1) * rows, :])

        in_specs = [pl.BlockSpec((k_rows, a_w), lambda g, s: (s, g)), pl.BlockSpec((k_rows, n_dim), lambda g, s: (s, 0))]
        out_spec = pl.BlockSpec((per_tile, 2, rows, n_dim), lambda g, s: (g, 0, 0, 0))
        out_dims, acc_dims = (N_CHIPS, 2, rows, n_dim), (per_tile, 2, rows, n_dim)
    elif layout == "cols_chip":
        groups = N_CHIPS
        rows, cols = m_dim // 2, n_dim // N_CHIPS

        def body(a_ref, b_ref, o_ref, acc):
            r = _dot_tn(a_ref[...], b_ref[...])
            for h in range(2):
                store(o_ref, acc, h, r[h * rows:(h + 1) * rows, :])

        in_specs = [pl.BlockSpec((k_rows, m_dim), lambda g, s: (s, 0)), pl.BlockSpec((k_rows, cols), lambda g, s: (s, g))]
        out_spec = pl.BlockSpec((None, 2, rows, cols), lambda g, s: (g, 0, 0, 0))
        out_dims, acc_dims = (N_CHIPS, 2, rows, cols), (2, rows, cols)
    else:
        groups = 2
        rows, cols = m_dim // 2, n_dim // N_CHIPS

        def body(a_ref, b_ref, o_ref, acc):
            r = _dot_tn(a_ref[...], b_ref[...])
            for k in range(N_CHIPS):
                store(o_ref, acc, k, r[:, k * cols:(k + 1) * cols])

        in_specs = [pl.BlockSpec((k_rows, rows), lambda g, s: (s, g)), pl.BlockSpec((k_rows, n_dim), lambda g, s: (s, 0))]
        out_spec = pl.BlockSpec((N_CHIPS, None, rows, cols), lambda g, s: (0, g, 0, 0))
        out_dims, acc_dims = (N_CHIPS, 2, rows, cols), (N_CHIPS, rows, cols)

    return pl.pallas_call(
        body, name=f"weight_grad_{layout}_{m_dim}x{n_dim}", grid=(groups, steps),
        in_specs=in_specs, out_specs=out_spec, out_shape=jax.ShapeDtypeStruct(out_dims, BF16),
        scratch_shapes=[pltpu.VMEM(acc_dims, F32)],
        compiler_params=pltpu.CompilerParams(dimension_semantics=("arbitrary", "arbitrary"), vmem_limit_bytes=VMEM_LIMIT),
    )(a, b)


def _sibling_exchange(bigs, smalls, tag):
    nb, ns = len(bigs), len(smalls)

    def body(*refs):
        ins, outs = refs[:nb + ns], refs[nb + ns:2 * (nb + ns)]
        send, recv = refs[2 * (nb + ns):]
        x, y, c, _, _ = _place()
        cps = []
        for t in range(nb + ns):
            src = ins[t].at[:, 1 - c] if t < nb else ins[t]
            cps.append(pltpu.make_async_remote_copy(
                src_ref=src, dst_ref=outs[t], send_sem=send.at[t], recv_sem=recv.at[t],
                device_id=(x, y, 1 - c), device_id_type=MESH))
        for cp in cps:
            cp.start()
        for cp in cps:
            cp.wait()

    out_shape = [jax.ShapeDtypeStruct((N_CHIPS,) + b.shape[2:], b.dtype) for b in bigs]
    out_shape += [jax.ShapeDtypeStruct(s.shape, F32) for s in smalls]
    return pl.pallas_call(
        body, name=f"sibling_exchange_{tag}", out_shape=out_shape,
        in_specs=[ANY] * (nb + ns), out_specs=[ANY] * (nb + ns),
        scratch_shapes=[pltpu.SemaphoreType.DMA((nb + ns,)), pltpu.SemaphoreType.DMA((nb + ns,))],
    )(*bigs, *smalls)


def _pair_sum(core, mine, theirs, tag, block_rows):
    _, _, rows, cols = mine.shape
    steps = rows // block_rows

    def body(core_ref, a_ref, b_ref, o_ref):
        o_ref[...] = (a_ref[...].astype(F32) + b_ref[...].astype(F32)).astype(BF16)

    grid_spec = pltpu.PrefetchScalarGridSpec(
        num_scalar_prefetch=1, grid=(N_CHIPS, steps),
        in_specs=[pl.BlockSpec((None, None, block_rows, cols), lambda k, r, core_ref: (k, core_ref[0], r, 0)),
                  pl.BlockSpec((None, block_rows, cols), lambda k, r, core_ref: (k, r, 0))],
        out_specs=pl.BlockSpec((None, block_rows, cols), lambda k, r, core_ref: (k, r, 0)),
    )
    return pl.pallas_call(
        body, name=f"pair_sum_{tag}", grid_spec=grid_spec,
        out_shape=jax.ShapeDtypeStruct((N_CHIPS, rows, cols), BF16),
        compiler_params=pltpu.CompilerParams(dimension_semantics=("arbitrary", "arbitrary"), vmem_limit_bytes=VMEM_LIMIT),
    )(core, mine, theirs)


def _pair_sum_small(mine, theirs):
    (m_f2, m_b1, m_b2, m_sf, m_s5, m_sp) = mine

    def body(a0, a1, a2, a3, a4, a5, b0, b1, b2, b3, b4, b5, o_m, o_f, o_5, o_p):
        sm = (a0[...] + a1[...] + a2[...]) + (b0[...] + b1[...] + b2[...])
        sf = a3[...] + b3[...]
        s5 = a4[...] + b4[...]
        for h in range(2):
            o_m[h] = sm[:, h * (D_MODEL // 2):(h + 1) * (D_MODEL // 2)]
            o_f[h] = sf[:, h * (D_FF // 2):(h + 1) * (D_FF // 2)]
            o_5[h] = s5[:, h * (D_CONV // 2):(h + 1) * (D_CONV // 2)]
            for g in range(2):
                o_p[h, g] = a5[2 * h + g] + b5[2 * h + g]

    out_shape = [
        jax.ShapeDtypeStruct((2, 8, D_MODEL // 2), F32), jax.ShapeDtypeStruct((2, 8, D_FF // 2), F32),
        jax.ShapeDtypeStruct((2, 40, D_CONV // 2), F32), jax.ShapeDtypeStruct((2, 2, POOL_GROUP, POOL_GROUP), F32),
    ]
    return pl.pallas_call(body, name="pair_sum_small", out_shape=out_shape, in_specs=[VMEM] * 12, out_specs=[VMEM] * 4)(
        *mine, *theirs)


def _scatter_ops(ins, outs, n_parts, sems):
    ici_send, ici_recv, fwd_send, fwd_recv, loc_sem = sems
    nt = len(ins)
    x, y, c, k, chips = _place()

    def src_of(t, kk):
        return ins[t].at[kk] if t < n_parts else ins[t].at[c]

    def ici(t, j, kk, slot):
        return pltpu.make_async_remote_copy(
            src_ref=src_of(t, kk), dst_ref=outs[t].at[slot, c], send_sem=ici_send.at[t * 3 + j],
            recv_sem=ici_recv.at[t * 3 + j], device_id=(*chips[j], c), device_id_type=MESH)

    def fwd(t, j, q, half, src=None):
        slot = outs[t].at[q, half]
        return pltpu.make_async_remote_copy(
            src_ref=slot if src is None else src, dst_ref=slot, send_sem=fwd_send.at[t * 4 + j],
            recv_sem=fwd_recv.at[t * 4 + j], device_id=(x, y, 1 - c), device_id_type=MESH)

    local = [pltpu.make_async_copy(src_of(t, k), outs[t].at[k, c], loc_sem.at[t]) for t in range(nt)]
    peers = [(t, j, 2 * qx + qy) for t in range(nt) for j, (qx, qy) in enumerate(chips)]
    sends = [fwd(t, 3, k, c, src=src_of(t, k)) for t in range(nt)]
    sends += [ici(t, j, kq, k) for t, j, kq in peers]

    def start():
        for cp in local + sends:
            cp.start()

    def finish():
        passed = []
        for t, j, kq in peers:
            ici(t, j, kq, kq).wait_recv()
            cp = fwd(t, j, kq, c)
            cp.start()
            passed.append(cp)
        for t in range(nt):
            fwd(t, 3, k, 1 - c).wait_recv()
        for t, j, kq in peers:
            fwd(t, j, kq, 1 - c).wait_recv()
        for cp in sends + passed:
            cp.wait_send()
        for cp in local:
            cp.wait()

    return start, finish


def _scatter_sems(nt):
    return [pltpu.SemaphoreType.DMA((3 * nt,))] * 2 + [pltpu.SemaphoreType.DMA((4 * nt,))] * 2 + [pltpu.SemaphoreType.DMA((nt,))]


def _scatter_shapes(parts, smalls):
    return [jax.ShapeDtypeStruct((N_CHIPS, 2) + p.shape[1:], p.dtype) for p in tuple(parts) + tuple(smalls)]


def _chip_scatter(parts, smalls):
    nt = len(parts) + len(smalls)

    def body(*refs):
        start, finish = _scatter_ops(refs[:nt], refs[nt:2 * nt], len(parts), refs[2 * nt:])
        start()
        finish()

    return pl.pallas_call(
        body, name="chip_scatter", out_shape=_scatter_shapes(parts, smalls), in_specs=[ANY] * nt, out_specs=[ANY] * nt,
        scratch_shapes=_scatter_sems(nt),
    )(*parts, *smalls)


def _adamw(w, g, m, v):
    m = ADAM_B1 * m + (1.0 - ADAM_B1) * g
    v = ADAM_B2 * v + (1.0 - ADAM_B2) * (g * g)
    m_hat = m / (1.0 - ADAM_B1 ** ADAM_STEP)
    v_hat = v / (1.0 - ADAM_B2 ** ADAM_STEP)
    delta = -ADAM_LR * (m_hat / (jnp.sqrt(v_hat) + ADAM_EPS) + ADAM_WD * w)
    return delta, m, v


def _adam_big(parts, w, m, v, tag, block_rows):
    _, _, rows, cols = parts.shape
    steps = rows // block_rows

    def body(p_ref, w_ref, m_ref, v_ref, g_out, d_out, m_out, v_out):
        g = p_ref[0].astype(F32)
        for q in range(1, N_CHIPS):
            g = g + p_ref[q].astype(F32)
        delta, m_new, v_new = _adamw(w_ref[...], g, m_ref[...], v_ref[...])
        g_out[...] = g
        d_out[...] = delta
        m_out[...] = m_new
        v_out[...] = v_new

    blk = pl.BlockSpec((block_rows, cols), lambda h, r: (h * steps + r, 0))
    return pl.pallas_call(
        body, name=f"adam_{tag}", grid=(2, steps),
        in_specs=[pl.BlockSpec((N_CHIPS, None, block_rows, cols), lambda h, r: (0, h, r, 0)), blk, blk, blk],
        out_specs=[blk] * 4, out_shape=[jax.ShapeDtypeStruct(w.shape, F32)] * 4,
        compiler_params=pltpu.CompilerParams(dimension_semantics=("arbitrary", "arbitrary"), vmem_limit_bytes=VMEM_LIMIT),
    )(parts, w, m, v)


def _reduce_small(l_m, l_f, l_5, l_p):
    def total(ref):
        t = ref[0]
        for q in range(1, N_CHIPS):
            t = t + ref[q]
        return t

    def body(m_ref, f_ref, s_ref, p_ref, g1_o, g2_o, g3_o, loss_o, wf_o, fb_o, wa_o, cb_o, lg_o, lb_o, ps_o, pw_o):
        tm, tf, t5, tp = total(m_ref), total(f_ref), total(s_ref), total(p_ref)
        sm = jnp.concatenate([tm[0], tm[1]], axis=1)
        sf = jnp.concatenate([tf[0], tf[1]], axis=1)
        s5 = jnp.concatenate([t5[0], t5[1]], axis=1)
        g1_o[...] = sm[0:1]
        g2_o[...] = sm[1:2]
        g3_o[...] = sm[2:3]
        loss_o[...] = sm[3:4, 0:128]
        wf_o[...] = sf
        fb_o[...] = sf[3:4]
        wa_o[...] = s5[0:32]
        cb_o[...] = s5[32:33]
        lg_o[...] = s5[33:34]
        lb_o[...] = s5[34:35]
        ps_o[...] = s5[35:36]
        for h in range(2):
            for g in range(2):
                pw_o[2 * h + g] = tp[h, g]

    row = lambda w: jax.ShapeDtypeStruct((1, w), F32)
    out_shape = [row(D_MODEL), row(D_MODEL), row(D_MODEL), row(128), jax.ShapeDtypeStruct((8, D_FF), F32), row(D_FF),
                 jax.ShapeDtypeStruct((32, D_CONV), F32), row(D_CONV), row(D_CONV), row(D_CONV), row(D_POOL),
                 jax.ShapeDtypeStruct((4, POOL_GROUP, POOL_GROUP), F32)]
    return pl.pallas_call(body, name="reduce_small", out_shape=out_shape, in_specs=[VMEM] * 4, out_specs=[VMEM] * 12)(
        l_m, l_f, l_5, l_p)


def _adam_small(ws, gs, ms, vs):
    count = len(ws)

    def body(*refs):
        w_r, g_r, m_r, v_r = (refs[t * count:(t + 1) * count] for t in range(4))
        d_o, m_o, v_o = (refs[(4 + t) * count:(5 + t) * count] for t in range(3))
        for t in range(count):
            delta, m_new, v_new = _adamw(w_r[t][...], g_r[t][...], m_r[t][...], v_r[t][...])
            d_o[t][...] = delta
            m_o[t][...] = m_new
            v_o[t][...] = v_new

    out_shape = [jax.ShapeDtypeStruct(w.shape, F32) for w in ws] * 3
    outs = pl.pallas_call(body, name="adam_small", out_shape=out_shape, in_specs=[VMEM] * (4 * count),
                          out_specs=[VMEM] * (3 * count))(*ws, *gs, *ms, *vs)
    return outs[:count], outs[count:2 * count], outs[2 * count:]


MIX_TILE = 512
FFN_TILE = 256
GRAD_K = 2048


def kernel(x, norm_mix_g, w_in, conv_a_w, conv_a_b, ln_a_g, ln_a_b, pool_w, pool_scale, w_out, norm_ffn_g, w_up, conv_f_w, conv_f_b, w_down, norm_final_g, loss_target, m_norm_mix_g, m_w_in, m_conv_a_w, m_conv_a_b, m_ln_a_g, m_ln_a_b, m_pool_w, m_pool_scale, m_w_out, m_norm_ffn_g, m_w_up, m_conv_f_w, m_conv_f_b, m_w_down, m_norm_final_g, v_norm_mix_g, v_w_in, v_conv_a_w, v_conv_a_b, v_ln_a_g, v_ln_a_b, v_pool_w, v_pool_scale, v_w_out, v_norm_ffn_g, v_w_up, v_conv_f_w, v_conv_f_b, v_w_down, v_norm_final_g):
    seq = x.shape[1]
    xs, ts = x[0], loss_target[0]
    mix_tile, ffn_tile, grad_k = min(MIX_TILE, seq), min(FFN_TILE, seq), min(GRAD_K, seq)
    chip = 2 * lax.axis_index("x") + lax.axis_index("y")
    core = lax.axis_index("c").astype(jnp.int32).reshape(1)

    wa_s = jnp.pad(conv_a_w[0], ((0, 32 - CONV_A), (0, 0)))
    wf_s = jnp.pad(conv_f_w[0], ((0, 8 - CONV_F), (0, 0)))
    win_b, wout_b, wup_b, wdown_b = _cast_shards(w_in[0], w_out[0], w_up[0], w_down[0])
    g3 = norm_final_g.reshape(1, D_MODEL)
    pw = pool_w[0]

    h1, proj, cpre, dpool, mcat, x1, win, wout, wup, wdown, wa_g, wf_g = _mixer_fwd(
        xs, norm_mix_g, win_b, wout_b, wup_b, wdown_b, wa_s, wf_s, conv_a_b, ln_a_g, ln_a_b, pw, pool_scale, mix_tile)
    wa = jnp.transpose(wa_g, (1, 0, 2)).reshape(32, D_CONV)
    wf = jnp.transpose(wf_g, (1, 0, 2)).reshape(8, D_FF)
    h2, up, gcs, act, dx2, dx2b, sm_f2 = _ffn_fwd(x1, norm_ffn_g, wup, wf, conv_f_b, wdown, g3, ts, ffn_tile)
    g_wdown = _weight_grad(act, dx2b, "rows2", grad_k)
    dup, dx1, dx1b, sm_b1, sf = _ffn_bwd(dx2, up, gcs, x1, norm_ffn_g, wup, wf, wdown, ffn_tile)
    g_wup = _weight_grad(h2, dup, "cols_chip", grad_k)
    g_wout = _weight_grad(mcat, dx1b, "rows1", grad_k)
    tags = ("w_in", "w_out", "w_up", "w_down")
    blocks = (256, 128, 256, 176)
    early = (g_wout, g_wup, g_wdown)
    landed = _sibling_exchange(early, (), "early")
    early_parts = [_pair_sum(core, b, l, tag, br) for b, l, tag, br in zip(early, landed, tags[1:], blocks[1:])]
    dproj, grad_x, sm_b2, s5, sp, s_wout, s_wup, s_wdown = _mixer_bwd(
        dx1, xs, proj, cpre, dpool, norm_mix_g, win, wa, ln_a_g, ln_a_b, pw, pool_scale, wout, early_parts, mix_tile)
    g_win = _weight_grad(h1, dproj, "cols_half", grad_k)

    smalls = (sm_f2, sm_b1, sm_b2, sf, s5, sp)
    landed = _sibling_exchange((g_win,), smalls, "late")
    part_win = _pair_sum(core, g_win, landed[0], tags[0], blocks[0])
    small_parts = _pair_sum_small(smalls, landed[1:])
    late = _chip_scatter([part_win], small_parts)
    scattered = [late[0], s_wout, s_wup, s_wdown] + list(late[1:])

    big_w = (w_in[0], w_out[0], w_up[0], w_down[0])
    big_m = (m_w_in[0], m_w_out[0], m_w_up[0], m_w_down[0])
    big_v = (v_w_in[0], v_w_out[0], v_w_up[0], v_w_down[0])
    big = {}
    for tag, p, w, m, v, br in zip(tags, scattered[:4], big_w, big_m, big_v, blocks):
        big[tag] = [a[None] for a in _adam_big(p, w, m, v, tag, br)]

    (g_g1, g_g2, g_g3, loss_row, g_wf_all, g_fb, g_wa_all, g_cb, g_lg, g_lb, g_ps, g_pw) = _reduce_small(*scattered[4:])
    g_wa = lax.dynamic_slice(g_wa_all, (0, chip * (D_CONV // N_CHIPS)), (32, D_CONV // N_CHIPS))[:CONV_A]
    g_wf = lax.dynamic_slice(g_wf_all, (0, chip * (D_FF // N_CHIPS)), (8, D_FF // N_CHIPS))[:CONV_F]
    small_names = ("norm_mix_g", "conv_a_w", "conv_a_b", "ln_a_g", "ln_a_b", "pool_w", "pool_scale", "norm_ffn_g",
                   "conv_f_w", "conv_f_b", "norm_final_g")
    small_w = (norm_mix_g, conv_a_w[0], conv_a_b, ln_a_g, ln_a_b, pw, pool_scale, norm_ffn_g, conv_f_w[0], conv_f_b, g3)
    small_m = (m_norm_mix_g, m_conv_a_w[0], m_conv_a_b, m_ln_a_g, m_ln_a_b, m_pool_w[0], m_pool_scale, m_norm_ffn_g,
               m_conv_f_w[0], m_conv_f_b, m_norm_final_g.reshape(1, D_MODEL))
    small_v = (v_norm_mix_g, v_conv_a_w[0], v_conv_a_b, v_ln_a_g, v_ln_a_b, v_pool_w[0], v_pool_scale, v_norm_ffn_g,
               v_conv_f_w[0], v_conv_f_b, v_norm_final_g.reshape(1, D_MODEL))
    small_g = (g_g1, g_wa, g_cb, g_lg, g_lb, g_pw, g_ps, g_g2, g_wf, g_fb, g_g3)
    s_delta, s_m, s_v = _adam_small(small_w, small_g, small_m, small_v)
    shapes = {"conv_a_w": conv_a_w.shape, "pool_w": pool_w.shape, "conv_f_w": conv_f_w.shape, "norm_final_g": norm_final_g.shape}
    small = {}
    for t, name in enumerate(small_names):
        shp = shapes.get(name)
        small[name] = [a if shp is None else a.reshape(shp) for a in (small_g[t], s_delta[t], s_m[t], s_v[t])]

    order = ("norm_mix_g", "w_in", "conv_a_w", "conv_a_b", "ln_a_g", "ln_a_b", "pool_w", "pool_scale", "w_out", "norm_ffn_g",
             "w_up", "conv_f_w", "conv_f_b", "w_down", "norm_final_g")
    table = {**big, **small}
    loss = loss_row[0, 0]
    outs = [loss, grad_x[None]]
    for t in range(4):
        outs += [table[name][t] for name in order]
    return tuple(outs)
```

```python
import functools

import jax
import jax.numpy as jnp
from jax import lax
from jax.experimental import pallas as pl
from jax.experimental.pallas import tpu as pltpu

F32 = jnp.float32
BF16 = jnp.bfloat16
EPS = 1e-6
ADAM_LR = 0.001
ADAM_B1 = 0.9
ADAM_B2 = 0.999
ADAM_EPS = 1e-08
ADAM_WD = 0.01
ADAM_STEP = 10

D_MODEL = 1024
D_CONV = 512
D_POOL = 512
D_IN = 1536
D_FF = 2816
CONV_A = 31
CONV_F = 3
POOL_WINDOWS = (2, 4, 8, 16)
POOL_GROUP = 128
N_CHIPS = 4
FF_CHUNK = 256
N_FF_CHUNKS = D_FF // FF_CHUNK
A_HALO = 32
P_HALO = 16
VMEM_LIMIT = 56 * 1024 * 1024
MESH = pl.DeviceIdType.MESH

ANY = pl.BlockSpec(memory_space=pl.ANY)
VMEM = pl.BlockSpec(memory_space=pltpu.VMEM)


def _dot(a, b):
    return jnp.dot(a, b, preferred_element_type=F32)


def _dot_nt(a, b):
    return lax.dot_general(a, b, (((1,), (1,)), ((), ())), preferred_element_type=F32)


def _dot_tn(a, b):
    return lax.dot_general(a, b, (((0,), (0,)), ((), ())), preferred_element_type=F32)


def _sigmoid(v):
    return jax.nn.sigmoid(v)


def _colsum(v):
    return jnp.sum(v, axis=0, keepdims=True)


def _rowmean(v):
    return jnp.mean(v, axis=-1, keepdims=True)


def _place():
    x, y, c = lax.axis_index("x"), lax.axis_index("y"), lax.axis_index("c")
    chips = [(1 - x, y), (x, 1 - y), (1 - x, 1 - y)]
    return x, y, c, 2 * x + y, chips


def _gather_ops(bufs, fulls, col_sharded, sems):
    ici_send, ici_recv, fwd_send, fwd_recv, loc_sem = sems
    n_big = len(bufs)
    x, y, c, k, chips = _place()

    def block(i, kk, half=None):
        rows, cols = bufs[i].shape
        if col_sharded[i]:
            rs = slice(None) if half is None else pl.ds(pl.multiple_of(half * (rows // 2), 16), rows // 2)
            return fulls[i].at[rs, pl.ds(pl.multiple_of(kk * cols, 128), cols)]
        if half is None:
            return fulls[i].at[pl.ds(pl.multiple_of(kk * rows, 16), rows), :]
        return fulls[i].at[pl.ds(pl.multiple_of(kk * rows + half * (rows // 2), 16), rows // 2), :]

    def my_half(i):
        rows = bufs[i].shape[0]
        return bufs[i].at[pl.ds(pl.multiple_of(c * (rows // 2), 16), rows // 2), :]

    def ici(i, j, kk):
        return pltpu.make_async_remote_copy(
            src_ref=my_half(i), dst_ref=block(i, kk, c), send_sem=ici_send.at[i * 3 + j], recv_sem=ici_recv.at[i * 3 + j],
            device_id=(*chips[j], c), device_id_type=MESH)

    def fwd(i, j, kk, half):
        return pltpu.make_async_remote_copy(
            src_ref=block(i, kk, half), dst_ref=block(i, kk, half), send_sem=fwd_send.at[i * 3 + j],
            recv_sem=fwd_recv.at[i * 3 + j], device_id=(x, y, 1 - c), device_id_type=MESH)

    local = [pltpu.make_async_copy(bufs[i], block(i, k), loc_sem.at[i]) for i in range(n_big)]
    sends = [ici(i, j, k) for i in range(n_big) for j in range(3)]
    peers = [(i, j, 2 * qx + qy) for i in range(n_big) for j, (qx, qy) in enumerate(chips)]

    def start():
        for cp in local + sends:
            cp.start()

    def finish():
        passed = []
        for i, j, kq in peers:
            ici(i, j, kq).wait_recv()
            cp = fwd(i, j, kq, c)
            cp.start()
            passed.append(cp)
        for i, j, kq in peers:
            fwd(i, j, kq, 1 - c).wait_recv()
        for cp in sends + passed:
            cp.wait_send()
        for cp in local:
            cp.wait()

    return start, finish


def _gather_sems(n_big):
    return [pltpu.SemaphoreType.DMA((3 * n_big,))] * 4 + [pltpu.SemaphoreType.DMA((n_big,))]


def _tap_ops(srcs, dsts, sems):
    send, recv, loc = sems
    _, _, c, k, chips = _place()

    def copy(t, j, kk):
        return pltpu.make_async_remote_copy(
            src_ref=srcs[t], dst_ref=dsts[t].at[kk], send_sem=send.at[t * 3 + j], recv_sem=recv.at[t * 3 + j],
            device_id=(*chips[j], c), device_id_type=MESH)

    local = [pltpu.make_async_copy(srcs[t], dsts[t].at[k], loc.at[t]) for t in range(len(srcs))]
    sends = [[copy(t, j, k) for j in range(3)] for t in range(len(srcs))]

    def start():
        for t, cp in enumerate(local):
            cp.start()
            for sd in sends[t]:
                sd.start()

    def wait(t):
        for j, (qx, qy) in enumerate(chips):
            copy(t, j, 2 * qx + qy).wait_recv()
        for sd in sends[t]:
            sd.wait_send()
        local[t].wait()

    return start, wait


def _cast_shards(*shards):
    def body(*refs):
        for src, dst in zip(refs[:len(shards)], refs[len(shards):]):
            dst[...] = src[...].astype(BF16)

    return pl.pallas_call(
        body, name="cast_shards", out_shape=[jax.ShapeDtypeStruct(s.shape, BF16) for s in shards],
        in_specs=[VMEM] * len(shards), out_specs=[VMEM] * len(shards),
        compiler_params=pltpu.CompilerParams(vmem_limit_bytes=VMEM_LIMIT),
    )(*shards)


def _load_weights(pairs, sem):
    cps = [pltpu.make_async_copy(src, dst, sem.at[i]) for i, (src, dst) in enumerate(pairs)]
    for cp in cps:
        cp.start()
    for cp in cps:
        cp.wait()


def _shifted_views(buf, shifted, t_rows):
    n = t_rows + A_HALO - 8
    for b in range(1, 8):
        shifted[b - 1] = buf[b:b + n, :]

    def view(offset):
        a, b = divmod(offset, 8)
        if b == 0:
            return buf[8 * a:8 * a + t_rows, :]
        return shifted[b - 1, 8 * a:8 * a + t_rows, :]

    return view


def _pool_count(tile, t_rows, w):
    row = lax.broadcasted_iota(jnp.int32, (t_rows, POOL_GROUP), 0) + tile * t_rows
    return jnp.minimum(row + 1, w).astype(F32)


def _mixer_fwd(x, g1, win_b, wout_b, wup_b, wa_s, wf_s, cb, lg, lb, pw, ps, tile_rows):
    seq = x.shape[0]
    tr = tile_rows
    n = seq // tr

    def body(x_ref, g1_ref, win_b_hbm, wout_b_hbm, wup_b_hbm, wa_s_hbm, wf_s_hbm, cb_ref, lg_ref, lb_ref, pw_ref,
             ps_ref, h1_ref, proj_ref, c_ref, d_ref, m_ref, x1_ref, win_f, wout_f, wup_f, wa_g, wf_g,
             win_v, wout_v, wa_ref, ubuf, ushift, bbuf, sem, *csems):
        i = pl.program_id(0)
        first_sems, later_sems, tap_sems = csems[0:5], csems[5:10], csems[10:13]

        def first():
            return _gather_ops((win_b_hbm, wout_b_hbm), (win_f, wout_f), (True, False), first_sems)

        def later():
            return _gather_ops((wup_b_hbm,), (wup_f,), (True,), later_sems)

        def taps():
            return _tap_ops((wa_s_hbm, wf_s_hbm), (wa_g, wf_g), tap_sems)

        @pl.when(i == 0)
        def _():
            first()[0]()
            taps()[0]()
            later()[0]()
            first()[1]()
            taps()[1](0)
            loads = [(win_f, win_v), (wout_f, wout_v)]
            loads += [(wa_g.at[kk], wa_ref.at[:, kk * (D_CONV // N_CHIPS):(kk + 1) * (D_CONV // N_CHIPS)]) for kk in range(N_CHIPS)]
            _load_weights(loads, sem)
            ubuf[0:A_HALO, :] = jnp.zeros((A_HALO, D_CONV), F32)
            bbuf[0:P_HALO, :] = jnp.zeros((P_HALO, D_POOL), F32)

        xv = x_ref[...]
        r = lax.rsqrt(_rowmean(xv * xv) + EPS)
        h1 = (xv * r * g1_ref[...]).astype(BF16)
        h1_ref[...] = h1
        proj = _dot(h1, win_v[...])
        proj_ref[...] = proj.astype(BF16)
        av, ag, bi = proj[:, :D_CONV], proj[:, D_CONV:2 * D_CONV], proj[:, 2 * D_CONV:]
        ubuf[A_HALO:A_HALO + tr, :] = av * _sigmoid(ag)
        off = A_HALO - (CONV_A - 1)
        uview = _shifted_views(ubuf, ushift, tr)
        acc = wa_ref[0:1, :] * uview(off)
        for j in range(1, CONV_A):
            acc = acc + wa_ref[j:j + 1, :] * uview(off + j)
        cv = acc + cb_ref[...]
        ubuf[0:A_HALO, :] = ubuf[tr:tr + A_HALO, :]
        c_ref[...] = cv.astype(BF16)
        xc = cv - _rowmean(cv)
        z = xc * lax.rsqrt(_rowmean(xc * xc) + EPS)
        ln = z * lg_ref[...] + lb_ref[...]
        ya = ln * _sigmoid(ln)
        bbuf[P_HALO:P_HALO + tr, :] = bi
        ds, ybs = [], []
        for g, w in enumerate(POOL_WINDOWS):
            cols = slice(g * POOL_GROUP, (g + 1) * POOL_GROUP)
            s = bi[:, cols]
            for kk in range(1, w):
                s = s + bbuf[P_HALO - kk:P_HALO - kk + tr, cols]
            dg = s / _pool_count(i, tr, w) - bi[:, cols]
            ds.append(dg)
            ybs.append(_dot(dg.astype(BF16), pw_ref[g].astype(BF16)))
        bbuf[0:P_HALO, :] = bbuf[tr:tr + P_HALO, :]
        d_ref[...] = jnp.concatenate(ds, axis=1).astype(BF16)
        yb = jnp.concatenate(ybs, axis=1) * ps_ref[...]
        m = jnp.concatenate([ya, yb], axis=1).astype(BF16)
        m_ref[...] = m
        x1_ref[...] = xv + _dot(m, wout_v[...])

        @pl.when(i == n - 1)
        def _():
            later()[1]()
            taps()[1](1)

    tile = lambda w: pl.BlockSpec((tr, w), lambda i: (i, 0))
    full = lambda a: pl.BlockSpec(a.shape, lambda i: (0,) * a.ndim)
    return pl.pallas_call(
        body, name="mixer_fwd", grid=(n,),
        in_specs=[tile(D_MODEL), full(g1)] + [ANY] * 5 + [full(cb), full(lg), full(lb), full(pw), full(ps)],
        out_specs=[tile(D_MODEL), tile(D_IN), tile(D_CONV), tile(D_POOL), tile(D_MODEL), tile(D_MODEL)] + [ANY] * 5,
        out_shape=[
            jax.ShapeDtypeStruct((seq, D_MODEL), BF16), jax.ShapeDtypeStruct((seq, D_IN), BF16),
            jax.ShapeDtypeStruct((seq, D_CONV), BF16), jax.ShapeDtypeStruct((seq, D_POOL), BF16),
            jax.ShapeDtypeStruct((seq, D_MODEL), BF16), jax.ShapeDtypeStruct((seq, D_MODEL), F32),
            jax.ShapeDtypeStruct((D_MODEL, D_IN), BF16), jax.ShapeDtypeStruct((D_MODEL, D_MODEL), BF16),
            jax.ShapeDtypeStruct((D_MODEL, 2 * D_FF), BF16),
            jax.ShapeDtypeStruct((N_CHIPS,) + wa_s.shape, F32), jax.ShapeDtypeStruct((N_CHIPS,) + wf_s.shape, F32),
        ],
        scratch_shapes=[
            pltpu.VMEM((D_MODEL, D_IN), BF16), pltpu.VMEM((D_MODEL, D_MODEL), BF16), pltpu.VMEM((32, D_CONV), F32),
            pltpu.VMEM((tr + A_HALO, D_CONV), F32), pltpu.VMEM((7, tr + A_HALO - 8, D_CONV), F32),
            pltpu.VMEM((tr + P_HALO, D_POOL), F32), pltpu.SemaphoreType.DMA((2 + N_CHIPS,)),
        ] + _gather_sems(2) + _gather_sems(1) + [
            pltpu.SemaphoreType.DMA((6,)), pltpu.SemaphoreType.DMA((6,)), pltpu.SemaphoreType.DMA((2,))],
        compiler_params=pltpu.CompilerParams(dimension_semantics=("arbitrary",), vmem_limit_bytes=VMEM_LIMIT),
    )(x, g1, win_b, wout_b, wup_b, wa_s, wf_s, cb, lg, lb, pw, ps)


def _ffn_up(x1, g2, wup, wf, fb, wdown_b, tile_rows):
    seq = x1.shape[0]
    tr = tile_rows
    n = seq // tr

    def body(x1_ref, g2_ref, wup_hbm, wf_ref, fb_ref, wdown_b_hbm,
             h2_ref, up_ref, gc_ref, act_ref, wdown_f, wup_v, gbuf, sem, *gsems):
        i = pl.program_id(0)

        def gather():
            return _gather_ops((wdown_b_hbm,), (wdown_f,), (False,), gsems)

        @pl.when(i == 0)
        def _():
            gather()[0]()
            _load_weights(((wup_hbm, wup_v),), sem)
            gbuf[0:8, :] = jnp.zeros((8, D_FF), F32)

        x1v = x1_ref[...]
        r2 = lax.rsqrt(_rowmean(x1v * x1v) + EPS)
        h2 = (x1v * r2 * g2_ref[...]).astype(BF16)
        h2_ref[...] = h2

        def up_proj(j):
            return (_dot(h2, wup_v[:, j * FF_CHUNK:(j + 1) * FF_CHUNK]),
                    _dot(h2, wup_v[:, D_FF + j * FF_CHUNK:D_FF + (j + 1) * FF_CHUNK]))

        ahead = up_proj(0)
        for j in range(N_FF_CHUNKS):
            cs = slice(j * FF_CHUNK, (j + 1) * FF_CHUNK)
            vs = slice(D_FF + j * FF_CHUNK, D_FF + (j + 1) * FF_CHUNK)
            gate, val = ahead
            if j + 1 < N_FF_CHUNKS:
                ahead = up_proj(j + 1)
            up_ref[:, cs] = gate.astype(BF16)
            up_ref[:, vs] = val.astype(BF16)
            gbuf[8:8 + tr, cs] = gate
            gc = (wf_ref[0:1, cs] * gbuf[6:6 + tr, cs] + wf_ref[1:2, cs] * gbuf[7:7 + tr, cs]
                  + wf_ref[2:3, cs] * gate + fb_ref[:, cs])
            gbuf[0:8, cs] = gbuf[tr:tr + 8, cs]
            gc_ref[:, cs] = gc.astype(BF16)
            act_ref[:, cs] = (gc * _sigmoid(gc) * val).astype(BF16)

        @pl.when(i == n - 1)
        def _():
            gather()[1]()

    tile = lambda w: pl.BlockSpec((tr, w), lambda i: (i, 0))
    full = lambda a: pl.BlockSpec(a.shape, lambda i: (0,) * a.ndim)
    return pl.pallas_call(
        body, name="ffn_up", grid=(n,),
        in_specs=[tile(D_MODEL), full(g2), ANY, full(wf), full(fb), ANY],
        out_specs=[tile(D_MODEL), tile(2 * D_FF), tile(D_FF), tile(D_FF), ANY],
        out_shape=[
            jax.ShapeDtypeStruct((seq, D_MODEL), BF16), jax.ShapeDtypeStruct((seq, 2 * D_FF), BF16),
            jax.ShapeDtypeStruct((seq, D_FF), BF16), jax.ShapeDtypeStruct((seq, D_FF), BF16),
            jax.ShapeDtypeStruct((D_FF, D_MODEL), BF16),
        ],
        scratch_shapes=[pltpu.VMEM(wup.shape, BF16), pltpu.VMEM((tr + 8, D_FF), F32), pltpu.SemaphoreType.DMA((1,))]
        + _gather_sems(1),
        compiler_params=pltpu.CompilerParams(dimension_semantics=("arbitrary",), vmem_limit_bytes=VMEM_LIMIT),
    )(x1, g2, wup, wf, fb, wdown_b)


def _ffn_down(x1, act, wdown, g3, target, tile_rows):
    seq = x1.shape[0]
    tr = tile_rows
    n = seq // tr

    def body(x1_ref, act_ref, wdown_hbm, g3_ref, t_ref, dx2_ref, dx2b_ref, sm_ref, wdown_v, sem):
        i = pl.program_id(0)

        @pl.when(i == 0)
        def _():
            _load_weights(((wdown_hbm, wdown_v),), sem)
            sm_ref[...] = jnp.zeros(sm_ref.shape, F32)

        x2 = x1_ref[...] + _dot(act_ref[...], wdown_v[...])
        r3 = lax.rsqrt(_rowmean(x2 * x2) + EPS)
        n3 = x2 * r3
        err = n3 * g3_ref[...] - t_ref[...]
        dy = err / D_MODEL
        sm_ref[2:3, :] += _colsum(dy * n3)
        loss = 0.5 * _colsum(_rowmean(err * err))
        sm_ref[3:4, :] += jnp.broadcast_to(loss, (1, D_MODEL))
        dn = dy * g3_ref[...]
        dx2v = r3 * (dn - n3 * _rowmean(dn * n3))
        dx2_ref[...] = dx2v
        dx2b_ref[...] = dx2v.astype(BF16)

    tile = lambda w: pl.BlockSpec((tr, w), lambda i: (i, 0))
    full = lambda a: pl.BlockSpec(a.shape, lambda i: (0,) * a.ndim)
    return pl.pallas_call(
        body, name="ffn_down", grid=(n,),
        in_specs=[tile(D_MODEL), tile(D_FF), ANY, full(g3), tile(D_MODEL)],
        out_specs=[tile(D_MODEL), tile(D_MODEL), pl.BlockSpec((8, D_MODEL), lambda i: (0, 0))],
        out_shape=[
            jax.ShapeDtypeStruct((seq, D_MODEL), F32), jax.ShapeDtypeStruct((seq, D_MODEL), BF16),
            jax.ShapeDtypeStruct((8, D_MODEL), F32),
        ],
        scratch_shapes=[pltpu.VMEM(wdown.shape, BF16), pltpu.SemaphoreType.DMA((1,))],
        compiler_params=pltpu.CompilerParams(dimension_semantics=("arbitrary",), vmem_limit_bytes=VMEM_LIMIT),
    )(x1, act, wdown, g3, target)


def _ffn_bwd(dx2, up, gcs, x1, g2, wup, wf, wdown, tile_rows):
    seq = x1.shape[0]
    tr = tile_rows
    n = seq // tr

    def body(dx2_ref, up_ref, gc_ref, x1_ref, g2_ref, wup_hbm, wf_ref, wdown_hbm,
             dup_ref, dx1_ref, dx1b_ref, sm_ref, sf_ref, wup_v, wdown_v, dbuf, dcar, sem):
        i = pl.program_id(0)

        @pl.when(i == 0)
        def _():
            _load_weights(((wup_hbm, wup_v), (wdown_hbm, wdown_v)), sem)
            dcar[...] = jnp.zeros(dcar.shape, F32)
            sm_ref[...] = jnp.zeros(sm_ref.shape, F32)
            sf_ref[...] = jnp.zeros(sf_ref.shape, F32)

        dx2v = dx2_ref[...]
        dx2b = dx2v.astype(BF16)
        dh2 = jnp.zeros((tr, D_MODEL), F32)

        def down_t(j):
            return _dot_nt(dx2b, wdown_v[j * FF_CHUNK:(j + 1) * FF_CHUNK, :])

        ahead = down_t(0)
        for j in range(N_FF_CHUNKS):
            cs = slice(j * FF_CHUNK, (j + 1) * FF_CHUNK)
            vs = slice(D_FF + j * FF_CHUNK, D_FF + (j + 1) * FF_CHUNK)
            dact = ahead
            if j + 1 < N_FF_CHUNKS:
                ahead = down_t(j + 1)
            gate = up_ref[:, cs].astype(F32)
            val = up_ref[:, vs].astype(F32)
            gc = gc_ref[:, cs].astype(F32)
            sg = _sigmoid(gc)
            dval = dact * (gc * sg)
            dgc = dact * val * (sg * (1.0 + gc * (1.0 - sg)))
            dbuf[0:tr, :] = dgc
            dbuf[tr:tr + 8, :] = dcar[:, cs]
            d_p1 = dbuf[1:1 + tr, :]
            d_p2 = dbuf[2:2 + tr, :]
            dgate = wf_ref[2:3, cs] * dgc + wf_ref[1:2, cs] * d_p1 + wf_ref[0:1, cs] * d_p2
            dcar[:, cs] = dgc[0:8, :]
            sf_ref[0:1, cs] += _colsum(d_p2 * gate)
            sf_ref[1:2, cs] += _colsum(d_p1 * gate)
            sf_ref[2:3, cs] += _colsum(dgc * gate)
            sf_ref[3:4, cs] += _colsum(dgc)
            dgb, dvb = dgate.astype(BF16), dval.astype(BF16)
            dup_ref[:, cs] = dgb
            dup_ref[:, vs] = dvb
            dh2 = dh2 + _dot_nt(dgb, wup_v[:, cs]) + _dot_nt(dvb, wup_v[:, vs])
        x1v = x1_ref[...]
        r2 = lax.rsqrt(_rowmean(x1v * x1v) + EPS)
        n2 = x1v * r2
        sm_ref[1:2, :] += _colsum(dh2 * n2)
        dn2 = dh2 * g2_ref[...]
        dx1v = dx2v + r2 * (dn2 - n2 * _rowmean(dn2 * n2))
        dx1_ref[...] = dx1v
        dx1b_ref[...] = dx1v.astype(BF16)

    tile = lambda w: pl.BlockSpec((tr, w), lambda i: (n - 1 - i, 0))
    full = lambda a: pl.BlockSpec(a.shape, lambda i: (0,) * a.ndim)
    acc = lambda rows, w: pl.BlockSpec((rows, w), lambda i: (0, 0))
    return pl.pallas_call(
        body, name="ffn_bwd", grid=(n,),
        in_specs=[tile(D_MODEL), tile(2 * D_FF), tile(D_FF), tile(D_MODEL), full(g2), ANY, full(wf), ANY],
        out_specs=[tile(2 * D_FF), tile(D_MODEL), tile(D_MODEL), acc(8, D_MODEL), acc(8, D_FF)],
        out_shape=[
            jax.ShapeDtypeStruct((seq, 2 * D_FF), BF16), jax.ShapeDtypeStruct((seq, D_MODEL), F32),
            jax.ShapeDtypeStruct((seq, D_MODEL), BF16), jax.ShapeDtypeStruct((8, D_MODEL), F32),
            jax.ShapeDtypeStruct((8, D_FF), F32),
        ],
        scratch_shapes=[
            pltpu.VMEM(wup.shape, BF16), pltpu.VMEM(wdown.shape, BF16),
            pltpu.VMEM((tr + 8, FF_CHUNK), F32), pltpu.VMEM((8, D_FF), F32), pltpu.SemaphoreType.DMA((2,)),
        ],
        compiler_params=pltpu.CompilerParams(dimension_semantics=("arbitrary",), vmem_limit_bytes=VMEM_LIMIT),
    )(dx2, up, gcs, x1, g2, wup, wf, wdown)


def _mixer_bwd(dx1, x, proj, cpre, d, g1, win, wa, lg, lb, pw, ps, wout, parts, tile_rows):
    seq = x.shape[0]
    n_parts = len(parts)
    tr = tile_rows
    n = seq // tr
    row_cb, row_lg, row_lb, row_ps = 32, 33, 34, 35

    def body(dx1_ref, x_ref, proj_ref, projh_ref, c_ref, d_ref, g1_ref, win_hbm, wa_ref, lg_ref, lb_ref, pw_ref, ps_ref,
             wout_hbm, *rest):
        part_refs, rest = rest[:n_parts], rest[n_parts:]
        dproj_ref, gx_ref, sm_ref, s5_ref, sp_ref = rest[:5]
        land_refs, rest = rest[5:5 + n_parts], rest[5 + n_parts:]
        win_v, wout_v, ubuf, ushift, dcbuf, dshift, ebuf, sem = rest[:8]
        ssems = rest[8:]
        i = pl.program_id(0)
        tile = n - 1 - i

        def scatter():
            return _scatter_ops(part_refs, land_refs, n_parts, ssems)

        @pl.when(i == 0)
        def _():
            scatter()[0]()
            _load_weights(((win_hbm, win_v), (wout_hbm, wout_v)), sem)
            dcbuf[tr:tr + A_HALO, :] = jnp.zeros((A_HALO, D_CONV), F32)
            ebuf[tr:tr + P_HALO, :] = jnp.zeros((P_HALO, D_POOL), F32)
            sm_ref[...] = jnp.zeros(sm_ref.shape, F32)
            s5_ref[...] = jnp.zeros(s5_ref.shape, F32)
            sp_ref[...] = jnp.zeros(sp_ref.shape, F32)

        dx1v = dx1_ref[...]
        dm = _dot_nt(dx1v.astype(BF16), wout_v[...])
        dya, dyb = dm[:, :D_CONV], dm[:, D_CONV:]
        dbis = []
        for g, w in enumerate(POOL_WINDOWS):
            cols = slice(g * POOL_GROUP, (g + 1) * POOL_GROUP)
            dgb = d_ref[:, cols]
            pwb = pw_ref[g].astype(BF16)
            dyg = dyb[:, cols]
            s5_ref[row_ps:row_ps + 1, cols] += _colsum(dyg * _dot(dgb, pwb))
            dqb = (dyg * ps_ref[:, cols]).astype(BF16)
            sp_ref[g] += _dot_tn(dgb, dqb)
            dd = _dot_nt(dqb, pwb)
            e = dd / _pool_count(tile, tr, w)
            ebuf[0:tr, cols] = e
            s = e
            for kk in range(1, w):
                s = s + ebuf[kk:kk + tr, cols]
            dbis.append(s - dd)
        ebuf[tr:tr + P_HALO, :] = ebuf[0:P_HALO, :]
        cv = c_ref[...].astype(F32)
        xc = cv - _rowmean(cv)
        rs = lax.rsqrt(_rowmean(xc * xc) + EPS)
        z = xc * rs
        ln = z * lg_ref[...] + lb_ref[...]
        sl = _sigmoid(ln)
        dl = dya * (sl * (1.0 + ln * (1.0 - sl)))
        s5_ref[row_lg:row_lg + 1, :] += _colsum(dl * z)
        s5_ref[row_lb:row_lb + 1, :] += _colsum(dl)
        dz = dl * lg_ref[...]
        dc = rs * (dz - _rowmean(dz) - z * _rowmean(dz * z))
        s5_ref[row_cb:row_cb + 1, :] += _colsum(dc)
        dcbuf[0:tr, :] = dc
        keep = (tile > 0).astype(F32)
        avh = projh_ref[:, :D_CONV].astype(F32)
        agh = projh_ref[:, D_CONV:].astype(F32)
        ubuf[0:A_HALO, :] = avh * _sigmoid(agh) * keep
        av = proj_ref[:, :D_CONV].astype(F32)
        ag = proj_ref[:, D_CONV:2 * D_CONV].astype(F32)
        sg = _sigmoid(ag)
        ubuf[A_HALO:A_HALO + tr, :] = av * sg
        off = A_HALO - (CONV_A - 1)
        du = wa_ref[CONV_A - 1:CONV_A, :] * dc
        dview = _shifted_views(dcbuf, dshift, tr)
        uview = _shifted_views(ubuf, ushift, tr)
        for j in range(CONV_A - 1):
            du = du + wa_ref[j:j + 1, :] * dview(CONV_A - 1 - j)
        for j in range(CONV_A):
            s5_ref[j:j + 1, :] += _colsum(dc * uview(off + j))
        dcbuf[tr:tr + A_HALO, :] = dcbuf[0:A_HALO, :]
        dav = du * sg
        dag = du * av * (sg * (1.0 - sg))
        dprojb = jnp.concatenate([dav, dag] + dbis, axis=1).astype(BF16)
        dproj_ref[...] = dprojb
        dh1 = _dot_nt(dprojb, win_v[...])
        xv = x_ref[...]
        r1 = lax.rsqrt(_rowmean(xv * xv) + EPS)
        n1 = xv * r1
        sm_ref[0:1, :] += _colsum(dh1 * n1)
        dn1 = dh1 * g1_ref[...]
        gx_ref[...] = dx1v + r1 * (dn1 - n1 * _rowmean(dn1 * n1))

        @pl.when(i == n - 1)
        def _():
            scatter()[1]()

    tile = lambda w: pl.BlockSpec((tr, w), lambda i: (n - 1 - i, 0))
    full = lambda a: pl.BlockSpec(a.shape, lambda i: (0,) * a.ndim)
    halo = pl.BlockSpec((A_HALO, 2 * D_CONV), lambda i: (jnp.maximum((n - 1 - i) * (tr // A_HALO) - 1, 0), 0))
    acc = lambda shape: pl.BlockSpec(shape, lambda i: (0,) * len(shape))
    return pl.pallas_call(
        body, name="mixer_bwd", grid=(n,),
        in_specs=[tile(D_MODEL), tile(D_MODEL), tile(D_IN), halo, tile(D_CONV), tile(D_POOL), full(g1), ANY, full(wa),
                  full(lg), full(lb), full(pw), full(ps), ANY] + [ANY] * n_parts,
        out_specs=[tile(D_IN), tile(D_MODEL), acc((8, D_MODEL)), acc((40, D_CONV)), acc(pw.shape)] + [ANY] * n_parts,
        out_shape=[
            jax.ShapeDtypeStruct((seq, D_IN), BF16), jax.ShapeDtypeStruct((seq, D_MODEL), F32),
            jax.ShapeDtypeStruct((8, D_MODEL), F32), jax.ShapeDtypeStruct((40, D_CONV), F32),
            jax.ShapeDtypeStruct(pw.shape, F32),
        ] + _scatter_shapes(parts, ()),
        scratch_shapes=[
            pltpu.VMEM(win.shape, BF16), pltpu.VMEM(wout.shape, BF16),
            pltpu.VMEM((tr + A_HALO, D_CONV), F32), pltpu.VMEM((7, tr + A_HALO - 8, D_CONV), F32),
            pltpu.VMEM((tr + A_HALO, D_CONV), F32), pltpu.VMEM((7, tr + A_HALO - 8, D_CONV), F32),
            pltpu.VMEM((tr + P_HALO, D_POOL), F32), pltpu.SemaphoreType.DMA((2,)),
        ] + _scatter_sems(n_parts),
        compiler_params=pltpu.CompilerParams(dimension_semantics=("arbitrary",), vmem_limit_bytes=VMEM_LIMIT),
    )(dx1, x, proj, proj, cpre, d, g1, win, wa, lg, lb, pw, ps, wout, *parts)


def _weight_grad(a, b, layout, k_rows):
    seq, m_dim = a.shape
    n_dim = b.shape[1]
    steps = seq // k_rows

    def store(o_ref, acc, index, value):
        if steps == 1:
            o_ref[index] = value.astype(BF16)
            return
        s = pl.program_id(1)

        @pl.when(s == 0)
        def _():
            acc[index] = value

        @pl.when(jnp.logical_and(s > 0, s < steps - 1))
        def _():
            acc[index] += value

        @pl.when(s == steps - 1)
        def _():
            o_ref[index] = (acc[index] + value).astype(BF16)

    if layout in ("rows1", "rows2"):
        groups = int(layout[-1])
        per_tile = N_CHIPS // groups
        rows = m_dim // N_CHIPS // 2
        a_w = m_dim // groups

        def body(a_ref, b_ref, o_ref, acc):
            r = _dot_tn(a_ref[...], b_ref[...])
            for p in range(per_tile):
                for h in range(2):
                    store(o_ref, acc, (p, h), r[(2 * p + h) * rows:(2 * p + h + 1) * rows, :])

        in_specs = [pl.BlockSpec((k_rows, a_w), lambda g, s: (s, g)), pl.BlockSpec((k_rows, n_dim), lambda g, s: (s, 0))]
        out_spec = pl.BlockSpec((per_tile, 2, rows, n_dim), lambda g, s: (g, 0, 0, 0))
        out_dims, acc_dims = (N_CHIPS, 2, rows, n_dim), (per_tile, 2, rows, n_dim)
    elif layout == "cols_chip":
        groups = N_CHIPS
        rows, cols = m_dim // 2, n_dim // N_CHIPS

        def body(a_ref, b_ref, o_ref, acc):
            r = _dot_tn(a_ref[...], b_ref[...])
            for h in range(2):
                store(o_ref, acc, h, r[h * rows:(h + 1) * rows, :])

        in_specs = [pl.BlockSpec((k_rows, m_dim), lambda g, s: (s, 0)), pl.BlockSpec((k_rows, cols), lambda g, s: (s, g))]
        out_spec = pl.BlockSpec((None, 2, rows, cols), lambda g, s: (g, 0, 0, 0))
        out_dims, acc_dims = (N_CHIPS, 2, rows, cols), (2, rows, cols)
    else:
        groups = 2
        rows, cols = m_dim // 2, n_dim // N_CHIPS

        def body(a_ref, b_ref, o_ref, acc):
            r = _dot_tn(a_ref[...], b_ref[...])
            for k in range(N_CHIPS):
                store(o_ref, acc, k, r[:, k * cols:(k + 1) * cols])

        in_specs = [pl.BlockSpec((k_rows, rows), lambda g, s: (s, g)), pl.BlockSpec((k_rows, n_dim), lambda g, s: (s, 0))]
        out_spec = pl.BlockSpec((N_CHIPS, None, rows, cols), lambda g, s: (0, g, 0, 0))
        out_dims, acc_dims = (N_CHIPS, 2, rows, cols), (N_CHIPS, rows, cols)

    return pl.pallas_call(
        body, name=f"weight_grad_{layout}_{m_dim}x{n_dim}", grid=(groups, steps),
        in_specs=in_specs, out_specs=out_spec, out_shape=jax.ShapeDtypeStruct(out_dims, BF16),
        scratch_shapes=[pltpu.VMEM(acc_dims, F32)],
        compiler_params=pltpu.CompilerParams(dimension_semantics=("arbitrary", "arbitrary"), vmem_limit_bytes=VMEM_LIMIT),
    )(a, b)


def _sibling_exchange(bigs, smalls, tag):
    nb, ns = len(bigs), len(smalls)

    def body(*refs):
        ins, outs = refs[:nb + ns], refs[nb + ns:2 * (nb + ns)]
        send, recv = refs[2 * (nb + ns):]
        x, y, c, _, _ = _place()
        cps = []
        for t in range(nb + ns):
            src = ins[t].at[:, 1 - c] if t < nb else ins[t]
            cps.append(pltpu.make_async_remote_copy(
                src_ref=src, dst_ref=outs[t], send_sem=send.at[t], recv_sem=recv.at[t],
                device_id=(x, y, 1 - c), device_id_type=MESH))
        for cp in cps:
            cp.start()
        for cp in cps:
            cp.wait()

    out_shape = [jax.ShapeDtypeStruct((N_CHIPS,) + b.shape[2:], b.dtype) for b in bigs]
    out_shape += [jax.ShapeDtypeStruct(s.shape, F32) for s in smalls]
    return pl.pallas_call(
        body, name=f"sibling_exchange_{tag}", out_shape=out_shape,
        in_specs=[ANY] * (nb + ns), out_specs=[ANY] * (nb + ns),
        scratch_shapes=[pltpu.SemaphoreType.DMA((nb + ns,)), pltpu.SemaphoreType.DMA((nb + ns,))],
    )(*bigs, *smalls)


def _pair_sum(core, mine, theirs, tag, block_rows):
    _, _, rows, cols = mine.shape
    steps = rows // block_rows

    def body(core_ref, a_ref, b_ref, o_ref):
        o_ref[...] = (a_ref[...].astype(F32) + b_ref[...].astype(F32)).astype(BF16)

    grid_spec = pltpu.PrefetchScalarGridSpec(
        num_scalar_prefetch=1, grid=(N_CHIPS, steps),
        in_specs=[pl.BlockSpec((None, None, block_rows, cols), lambda k, r, core_ref: (k, core_ref[0], r, 0)),
                  pl.BlockSpec((None, block_rows, cols), lambda k, r, core_ref: (k, r, 0))],
        out_specs=pl.BlockSpec((None, block_rows, cols), lambda k, r, core_ref: (k, r, 0)),
    )
    return pl.pallas_call(
        body, name=f"pair_sum_{tag}", grid_spec=grid_spec,
        out_shape=jax.ShapeDtypeStruct((N_CHIPS, rows, cols), BF16),
        compiler_params=pltpu.CompilerParams(dimension_semantics=("arbitrary", "arbitrary"), vmem_limit_bytes=VMEM_LIMIT),
    )(core, mine, theirs)


def _pair_sum_small(mine, theirs):
    (m_f2, m_b1, m_b2, m_sf, m_s5, m_sp) = mine

    def body(a0, a1, a2, a3, a4, a5, b0, b1, b2, b3, b4, b5, o_m, o_f, o_5, o_p):
        sm = (a0[...] + a1[...] + a2[...]) + (b0[...] + b1[...] + b2[...])
        sf = a3[...] + b3[...]
        s5 = a4[...] + b4[...]
        for h in range(2):
            o_m[h] = sm[:, h * (D_MODEL // 2):(h + 1) * (D_MODEL // 2)]
            o_f[h] = sf[:, h * (D_FF // 2):(h + 1) * (D_FF // 2)]
            o_5[h] = s5[:, h * (D_CONV // 2):(h + 1) * (D_CONV // 2)]
            for g in range(2):
                o_p[h, g] = a5[2 * h + g] + b5[2 * h + g]

    out_shape = [
        jax.ShapeDtypeStruct((2, 8, D_MODEL // 2), F32), jax.ShapeDtypeStruct((2, 8, D_FF // 2), F32),
        jax.ShapeDtypeStruct((2, 40, D_CONV // 2), F32), jax.ShapeDtypeStruct((2, 2, POOL_GROUP, POOL_GROUP), F32),
    ]
    return pl.pallas_call(body, name="pair_sum_small", out_shape=out_shape, in_specs=[VMEM] * 12, out_specs=[VMEM] * 4)(
        *mine, *theirs)


def _scatter_ops(ins, outs, n_parts, sems):
    ici_send, ici_recv, fwd_send, fwd_recv, loc_sem = sems
    nt = len(ins)
    x, y, c, k, chips = _place()

    def src_of(t, kk):
        return ins[t].at[kk] if t < n_parts else ins[t].at[c]

    def ici(t, j, kk, slot):
        return pltpu.make_async_remote_copy(
            src_ref=src_of(t, kk), dst_ref=outs[t].at[slot, c], send_sem=ici_send.at[t * 3 + j],
            recv_sem=ici_recv.at[t * 3 + j], device_id=(*chips[j], c), device_id_type=MESH)

    def fwd(t, j, q, half, src=None):
        slot = outs[t].at[q, half]
        return pltpu.make_async_remote_copy(
            src_ref=slot if src is None else src, dst_ref=slot, send_sem=fwd_send.at[t * 4 + j],
            recv_sem=fwd_recv.at[t * 4 + j], device_id=(x, y, 1 - c), device_id_type=MESH)

    local = [pltpu.make_async_copy(src_of(t, k), outs[t].at[k, c], loc_sem.at[t]) for t in range(nt)]
    peers = [(t, j, 2 * qx + qy) for t in range(nt) for j, (qx, qy) in enumerate(chips)]
    sends = [fwd(t, 3, k, c, src=src_of(t, k)) for t in range(nt)]
    sends += [ici(t, j, kq, k) for t, j, kq in peers]

    def start():
        for cp in local + sends:
            cp.start()

    def finish():
        passed = []
        for t, j, kq in peers:
            ici(t, j, kq, kq).wait_recv()
            cp = fwd(t, j, kq, c)
            cp.start()
            passed.append(cp)
        for t in range(nt):
            fwd(t, 3, k, 1 - c).wait_recv()
        for t, j, kq in peers:
            fwd(t, j, kq, 1 - c).wait_recv()
        for cp in sends + passed:
            cp.wait_send()
        for cp in local:
            cp.wait()

    return start, finish


def _scatter_sems(nt):
    return [pltpu.SemaphoreType.DMA((3 * nt,))] * 2 + [pltpu.SemaphoreType.DMA((4 * nt,))] * 2 + [pltpu.SemaphoreType.DMA((nt,))]


def _scatter_shapes(parts, smalls):
    return [jax.ShapeDtypeStruct((N_CHIPS, 2) + p.shape[1:], p.dtype) for p in tuple(parts) + tuple(smalls)]


def _chip_scatter(parts, smalls):
    nt = len(parts) + len(smalls)

    def body(*refs):
        start, finish = _scatter_ops(refs[:nt], refs[nt:2 * nt], len(parts), refs[2 * nt:])
        start()
        finish()

    return pl.pallas_call(
        body, name="chip_scatter", out_shape=_scatter_shapes(parts, smalls), in_specs=[ANY] * nt, out_specs=[ANY] * nt,
        scratch_shapes=_scatter_sems(nt),
    )(*parts, *smalls)


def _adamw(w, g, m, v):
    m = ADAM_B1 * m + (1.0 - ADAM_B1) * g
    v = ADAM_B2 * v + (1.0 - ADAM_B2) * (g * g)
    m_hat = m / (1.0 - ADAM_B1 ** ADAM_STEP)
    v_hat = v / (1.0 - ADAM_B2 ** ADAM_STEP)
    delta = -ADAM_LR * (m_hat / (jnp.sqrt(v_hat) + ADAM_EPS) + ADAM_WD * w)
    return delta, m, v


def _adam_big(parts, w, m, v, tag, block_rows):
    _, _, rows, cols = parts.shape
    steps = rows // block_rows

    def body(p_ref, w_ref, m_ref, v_ref, g_out, d_out, m_out, v_out):
        g = p_ref[0].astype(F32)
        for q in range(1, N_CHIPS):
            g = g + p_ref[q].astype(F32)
        delta, m_new, v_new = _adamw(w_ref[...], g, m_ref[...], v_ref[...])
        g_out[...] = g
        d_out[...] = delta
        m_out[...] = m_new
        v_out[...] = v_new

    blk = pl.BlockSpec((block_rows, cols), lambda h, r: (h * steps + r, 0))
    return pl.pallas_call(
        body, name=f"adam_{tag}", grid=(2, steps),
        in_specs=[pl.BlockSpec((N_CHIPS, None, block_rows, cols), lambda h, r: (0, h, r, 0)), blk, blk, blk],
        out_specs=[blk] * 4, out_shape=[jax.ShapeDtypeStruct(w.shape, F32)] * 4,
        compiler_params=pltpu.CompilerParams(dimension_semantics=("arbitrary", "arbitrary"), vmem_limit_bytes=VMEM_LIMIT),
    )(parts, w, m, v)


def _reduce_small(l_m, l_f, l_5, l_p):
    def total(ref):
        t = ref[0]
        for q in range(1, N_CHIPS):
            t = t + ref[q]
        return t

    def body(m_ref, f_ref, s_ref, p_ref, g1_o, g2_o, g3_o, loss_o, wf_o, fb_o, wa_o, cb_o, lg_o, lb_o, ps_o, pw_o):
        tm, tf, t5, tp = total(m_ref), total(f_ref), total(s_ref), total(p_ref)
        sm = jnp.concatenate([tm[0], tm[1]], axis=1)
        sf = jnp.concatenate([tf[0], tf[1]], axis=1)
        s5 = jnp.concatenate([t5[0], t5[1]], axis=1)
        g1_o[...] = sm[0:1]
        g2_o[...] = sm[1:2]
        g3_o[...] = sm[2:3]
        loss_o[...] = sm[3:4, 0:128]
        wf_o[...] = sf
        fb_o[...] = sf[3:4]
        wa_o[...] = s5[0:32]
        cb_o[...] = s5[32:33]
        lg_o[...] = s5[33:34]
        lb_o[...] = s5[34:35]
        ps_o[...] = s5[35:36]
        for h in range(2):
            for g in range(2):
                pw_o[2 * h + g] = tp[h, g]

    row = lambda w: jax.ShapeDtypeStruct((1, w), F32)
    out_shape = [row(D_MODEL), row(D_MODEL), row(D_MODEL), row(128), jax.ShapeDtypeStruct((8, D_FF), F32), row(D_FF),
                 jax.ShapeDtypeStruct((32, D_CONV), F32), row(D_CONV), row(D_CONV), row(D_CONV), row(D_POOL),
                 jax.ShapeDtypeStruct((4, POOL_GROUP, POOL_GROUP), F32)]
    return pl.pallas_call(body, name="reduce_small", out_shape=out_shape, in_specs=[VMEM] * 4, out_specs=[VMEM] * 12)(
        l_m, l_f, l_5, l_p)


def _adam_small(ws, gs, ms, vs):
    count = len(ws)

    def body(*refs):
        w_r, g_r, m_r, v_r = (refs[t * count:(t + 1) * count] for t in range(4))
        d_o, m_o, v_o = (refs[(4 + t) * count:(5 + t) * count] for t in range(3))
        for t in range(count):
            delta, m_new, v_new = _adamw(w_r[t][...], g_r[t][...], m_r[t][...], v_r[t][...])
            d_o[t][...] = delta
            m_o[t][...] = m_new
            v_o[t][...] = v_new

    out_shape = [jax.ShapeDtypeStruct(w.shape, F32) for w in ws] * 3
    outs = pl.pallas_call(body, name="adam_small", out_shape=out_shape, in_specs=[VMEM] * (4 * count),
                          out_specs=[VMEM] * (3 * count))(*ws, *gs, *ms, *vs)
    return outs[:count], outs[count:2 * count], outs[2 * count:]


MIX_TILE = 512
FFN_TILE = 256
GRAD_K = 2048


def kernel(x, norm_mix_g, w_in, conv_a_w, conv_a_b, ln_a_g, ln_a_b, pool_w, pool_scale, w_out, norm_ffn_g, w_up, conv_f_w, conv_f_b, w_down, norm_final_g, loss_target, m_norm_mix_g, m_w_in, m_conv_a_w, m_conv_a_b, m_ln_a_g, m_ln_a_b, m_pool_w, m_pool_scale, m_w_out, m_norm_ffn_g, m_w_up, m_conv_f_w, m_conv_f_b, m_w_down, m_norm_final_g, v_norm_mix_g, v_w_in, v_conv_a_w, v_conv_a_b, v_ln_a_g, v_ln_a_b, v_pool_w, v_pool_scale, v_w_out, v_norm_ffn_g, v_w_up, v_conv_f_w, v_conv_f_b, v_w_down, v_norm_final_g):
    seq = x.shape[1]
    xs, ts = x[0], loss_target[0]
    mix_tile, ffn_tile, grad_k = min(MIX_TILE, seq), min(FFN_TILE, seq), min(GRAD_K, seq)
    chip = 2 * lax.axis_index("x") + lax.axis_index("y")
    core = lax.axis_index("c").astype(jnp.int32).reshape(1)

    wa_s = jnp.pad(conv_a_w[0], ((0, 32 - CONV_A), (0, 0)))
    wf_s = jnp.pad(conv_f_w[0], ((0, 8 - CONV_F), (0, 0)))
    win_b, wout_b, wup_b, wdown_b = _cast_shards(w_in[0], w_out[0], w_up[0], w_down[0])
    g3 = norm_final_g.reshape(1, D_MODEL)
    pw = pool_w[0]

    h1, proj, cpre, dpool, mcat, x1, win, wout, wup, wa_g, wf_g = _mixer_fwd(
        xs, norm_mix_g, win_b, wout_b, wup_b, wa_s, wf_s, conv_a_b, ln_a_g, ln_a_b, pw, pool_scale, mix_tile)
    wa = jnp.transpose(wa_g, (1, 0, 2)).reshape(32, D_CONV)
    wf = jnp.transpose(wf_g, (1, 0, 2)).reshape(8, D_FF)
    h2, up, gcs, act, wdown = _ffn_up(x1, norm_ffn_g, wup, wf, conv_f_b, wdown_b, ffn_tile)
    dx2, dx2b, sm_f2 = _ffn_down(x1, act, wdown, g3, ts, mix_tile)
    g_wdown = _weight_grad(act, dx2b, "rows2", grad_k)
    dup, dx1, dx1b, sm_b1, sf = _ffn_bwd(dx2, up, gcs, x1, norm_ffn_g, wup, wf, wdown, ffn_tile)
    g_wup = _weight_grad(h2, dup, "cols_chip", grad_k)
    g_wout = _weight_grad(mcat, dx1b, "rows1", grad_k)
    tags = ("w_in", "w_out", "w_up", "w_down")
    blocks = (256, 128, 256, 176)
    early = (g_wout, g_wup, g_wdown)
    landed = _sibling_exchange(early, (), "early")
    early_parts = [_pair_sum(core, b, l, tag, br) for b, l, tag, br in zip(early, landed, tags[1:], blocks[1:])]
    dproj, grad_x, sm_b2, s5, sp, s_wout, s_wup, s_wdown = _mixer_bwd(
        dx1, xs, proj, cpre, dpool, norm_mix_g, win, wa, ln_a_g, ln_a_b, pw, pool_scale, wout, early_parts, mix_tile)
    g_win = _weight_grad(h1, dproj, "cols_half", grad_k)

    smalls = (sm_f2, sm_b1, sm_b2, sf, s5, sp)
    landed = _sibling_exchange((g_win,), smalls, "late")
    part_win = _pair_sum(core, g_win, landed[0], tags[0], blocks[0])
    small_parts = _pair_sum_small(smalls, landed[1:])
    late = _chip_scatter([part_win], small_parts)
    scattered = [late[0], s_wout, s_wup, s_wdown] + list(late[1:])

    big_w = (w_in[0], w_out[0], w_up[0], w_down[0])
    big_m = (m_w_in[0], m_w_out[0], m_w_up[0], m_w_down[0])
    big_v = (v_w_in[0], v_w_out[0], v_w_up[0], v_w_down[0])
    big = {}
    for tag, p, w, m, v, br in zip(tags, scattered[:4], big_w, big_m, big_v, blocks):
        big[tag] = [a[None] for a in _adam_big(p, w, m, v, tag, br)]

    (g_g1, g_g2, g_g3, loss_row, g_wf_all, g_fb, g_wa_all, g_cb, g_lg, g_lb, g_ps, g_pw) = _reduce_small(*scattered[4:])
    g_wa = lax.dynamic_slice(g_wa_all, (0, chip * (D_CONV // N_CHIPS)), (32, D_CONV // N_CHIPS))[:CONV_A]
    g_wf = lax.dynamic_slice(g_wf_all, (0, chip * (D_FF // N_CHIPS)), (8, D_FF // N_CHIPS))[:CONV_F]
    small_names = ("norm_mix_g", "conv_a_w", "conv_a_b", "ln_a_g", "ln_a_b", "pool_w", "pool_scale", "norm_ffn_g",
                   "conv_f_w", "conv_f_b", "norm_final_g")
    small_w = (norm_mix_g, conv_a_w[0], conv_a_b, ln_a_g, ln_a_b, pw, pool_scale, norm_ffn_g, conv_f_w[0], conv_f_b, g3)
    small_m = (m_norm_mix_g, m_conv_a_w[0], m_conv_a_b, m_ln_a_g, m_ln_a_b, m_pool_w[0], m_pool_scale, m_norm_ffn_g,
               m_conv_f_w[0], m_conv_f_b, m_norm_final_g.reshape(1, D_MODEL))
    small_v = (v_norm_mix_g, v_conv_a_w[0], v_conv_a_b, v_ln_a_g, v_ln_a_b, v_pool_w[0], v_pool_scale, v_norm_ffn_g,
               v_conv_f_w[0], v_conv_f_b, v_norm_final_g.reshape(1, D_MODEL))
    small_g = (g_g1, g_wa, g_cb, g_lg, g_lb, g_pw, g_ps, g_g2, g_wf, g_fb, g_g3)
    s_delta, s_m, s_v = _adam_small(small_w, small_g, small_m, small_v)
    shapes = {"conv_a_w": conv_a_w.shape, "pool_w": pool_w.shape, "conv_f_w": conv_f_w.shape, "norm_final_g": norm_final_g.shape}
    small = {}
    for t, name in enumerate(small_names):
        shp = shapes.get(name)
        small[name] = [a if shp is None else a.reshape(shp) for a in (small_g[t], s_delta[t], s_m[t], s_v[t])]

    order = ("norm_mix_g", "w_in", "conv_a_w", "conv_a_b", "ln_a_g", "ln_a_b", "pool_w", "pool_scale", "w_out", "norm_ffn_g",
             "w_up", "conv_f_w", "conv_f_b", "w_down", "norm_final_g")
    table = {**big, **small}
    loss = loss_row[0, 0]
    outs = [loss, grad_x[None]]
    for t in range(4):
        outs += [table[name][t] for name in order]
    return tuple(outs)
```

```python
import functools

import jax
import jax.numpy as jnp
from jax import lax
from jax.experimental import pallas as pl
from jax.experimental.pallas import tpu as pltpu

F32 = jnp.float32
BF16 = jnp.bfloat16
EPS = 1e-6
ADAM_LR = 0.001
ADAM_B1 = 0.9
ADAM_B2 = 0.999
ADAM_EPS = 1e-08
ADAM_WD = 0.01
ADAM_STEP = 10

D_MODEL = 1024
D_CONV = 512
D_POOL = 512
D_IN = 1536
D_FF = 2816
CONV_A = 31
CONV_F = 3
POOL_WINDOWS = (2, 4, 8, 16)
POOL_GROUP = 128
N_CHIPS = 4
FF_CHUNK = 256
N_FF_CHUNKS = D_FF // FF_CHUNK
A_HALO = 32
P_HALO = 16
VMEM_LIMIT = 56 * 1024 * 1024
MESH = pl.DeviceIdType.MESH

ANY = pl.BlockSpec(memory_space=pl.ANY)
VMEM = pl.BlockSpec(memory_space=pltpu.VMEM)


def _dot(a, b):
    return jnp.dot(a, b, preferred_element_type=F32)


def _dot_nt(a, b):
    return lax.dot_general(a, b, (((1,), (1,)), ((), ())), preferred_element_type=F32)


def _dot_tn(a, b):
    return lax.dot_general(a, b, (((0,), (0,)), ((), ())), preferred_element_type=F32)


def _sigmoid(v):
    return jax.nn.sigmoid(v)


def _colsum(v):
    return jnp.sum(v, axis=0, keepdims=True)


def _rowmean(v):
    return jnp.mean(v, axis=-1, keepdims=True)


def _place():
    x, y, c = lax.axis_index("x"), lax.axis_index("y"), lax.axis_index("c")
    chips = [(1 - x, y), (x, 1 - y), (1 - x, 1 - y)]
    return x, y, c, 2 * x + y, chips


def _gather_ops(bufs, fulls, col_sharded, sems):
    ici_send, ici_recv, fwd_send, fwd_recv, loc_sem = sems
    n_big = len(bufs)
    x, y, c, k, chips = _place()

    def block(i, kk, half=None):
        rows, cols = bufs[i].shape
        if col_sharded[i]:
            rs = slice(None) if half is None else pl.ds(pl.multiple_of(half * (rows // 2), 16), rows // 2)
            return fulls[i].at[rs, pl.ds(pl.multiple_of(kk * cols, 128), cols)]
        if half is None:
            return fulls[i].at[pl.ds(pl.multiple_of(kk * rows, 16), rows), :]
        return fulls[i].at[pl.ds(pl.multiple_of(kk * rows + half * (rows // 2), 16), rows // 2), :]

    def my_half(i):
        rows = bufs[i].shape[0]
        return bufs[i].at[pl.ds(pl.multiple_of(c * (rows // 2), 16), rows // 2), :]

    def ici(i, j, kk):
        return pltpu.make_async_remote_copy(
            src_ref=my_half(i), dst_ref=block(i, kk, c), send_sem=ici_send.at[i * 3 + j], recv_sem=ici_recv.at[i * 3 + j],
            device_id=(*chips[j], c), device_id_type=MESH)

    def fwd(i, j, kk, half):
        return pltpu.make_async_remote_copy(
            src_ref=block(i, kk, half), dst_ref=block(i, kk, half), send_sem=fwd_send.at[i * 3 + j],
            recv_sem=fwd_recv.at[i * 3 + j], device_id=(x, y, 1 - c), device_id_type=MESH)

    local = [pltpu.make_async_copy(bufs[i], block(i, k), loc_sem.at[i]) for i in range(n_big)]
    sends = [ici(i, j, k) for i in range(n_big) for j in range(3)]
    peers = [(i, j, 2 * qx + qy) for i in range(n_big) for j, (qx, qy) in enumerate(chips)]

    def start():
        for cp in local + sends:
            cp.start()

    def finish():
        passed = []
        for i, j, kq in peers:
            ici(i, j, kq).wait_recv()
            cp = fwd(i, j, kq, c)
            cp.start()
            passed.append(cp)
        for i, j, kq in peers:
            fwd(i, j, kq, 1 - c).wait_recv()
        for cp in sends + passed:
            cp.wait_send()
        for cp in local:
            cp.wait()

    return start, finish


def _gather_sems(n_big):
    return [pltpu.SemaphoreType.DMA((3 * n_big,))] * 4 + [pltpu.SemaphoreType.DMA((n_big,))]


def _tap_ops(srcs, dsts, sems):
    send, recv, loc = sems
    _, _, c, k, chips = _place()

    def copy(t, j, kk):
        return pltpu.make_async_remote_copy(
            src_ref=srcs[t], dst_ref=dsts[t].at[kk], send_sem=send.at[t * 3 + j], recv_sem=recv.at[t * 3 + j],
            device_id=(*chips[j], c), device_id_type=MESH)

    local = [pltpu.make_async_copy(srcs[t], dsts[t].at[k], loc.at[t]) for t in range(len(srcs))]
    sends = [[copy(t, j, k) for j in range(3)] for t in range(len(srcs))]

    def start():
        for t, cp in enumerate(local):
            cp.start()
            for sd in sends[t]:
                sd.start()

    def wait(t):
        for j, (qx, qy) in enumerate(chips):
            copy(t, j, 2 * qx + qy).wait_recv()
        for sd in sends[t]:
            sd.wait_send()
        local[t].wait()

    return start, wait


def _cast_shards(*shards):
    def body(*refs):
        for src, dst in zip(refs[:len(shards)], refs[len(shards):]):
            dst[...] = src[...].astype(BF16)

    return pl.pallas_call(
        body, name="cast_shards", out_shape=[jax.ShapeDtypeStruct(s.shape, BF16) for s in shards],
        in_specs=[VMEM] * len(shards), out_specs=[VMEM] * len(shards),
        compiler_params=pltpu.CompilerParams(vmem_limit_bytes=VMEM_LIMIT),
    )(*shards)


def _load_weights(pairs, sem):
    cps = [pltpu.make_async_copy(src, dst, sem.at[i]) for i, (src, dst) in enumerate(pairs)]
    for cp in cps:
        cp.start()
    for cp in cps:
        cp.wait()


def _shifted_views(buf, shifted, t_rows):
    n = t_rows + A_HALO - 8
    for b in range(1, 8):
        shifted[b - 1] = buf[b:b + n, :]

    def view(offset):
        a, b = divmod(offset, 8)
        if b == 0:
            return buf[8 * a:8 * a + t_rows, :]
        return shifted[b - 1, 8 * a:8 * a + t_rows, :]

    return view


def _pool_count(tile, t_rows, w):
    row = lax.broadcasted_iota(jnp.int32, (t_rows, POOL_GROUP), 0) + tile * t_rows
    return jnp.minimum(row + 1, w).astype(F32)


def _mixer_fwd(x, g1, win_b, wout_b, wup_b, wa_s, wf_s, cb, lg, lb, pw, ps, tile_rows):
    seq = x.shape[0]
    tr = tile_rows
    n = seq // tr

    def body(x_ref, g1_ref, win_b_hbm, wout_b_hbm, wup_b_hbm, wa_s_hbm, wf_s_hbm, cb_ref, lg_ref, lb_ref, pw_ref,
             ps_ref, h1_ref, proj_ref, c_ref, d_ref, m_ref, x1_ref, win_f, wout_f, wup_f, wa_g, wf_g,
             win_v, wout_v, wa_ref, ubuf, ushift, bbuf, sem, *csems):
        i = pl.program_id(0)
        first_sems, later_sems, tap_sems = csems[0:5], csems[5:10], csems[10:13]

        def first():
            return _gather_ops((win_b_hbm, wout_b_hbm), (win_f, wout_f), (True, False), first_sems)

        def later():
            return _gather_ops((wup_b_hbm,), (wup_f,), (True,), later_sems)

        def taps():
            return _tap_ops((wa_s_hbm, wf_s_hbm), (wa_g, wf_g), tap_sems)

        @pl.when(i == 0)
        def _():
            first()[0]()
            taps()[0]()
            later()[0]()
            first()[1]()
            taps()[1](0)
            loads = [(win_f, win_v), (wout_f, wout_v)]
            loads += [(wa_g.at[kk], wa_ref.at[:, kk * (D_CONV // N_CHIPS):(kk + 1) * (D_CONV // N_CHIPS)]) for kk in range(N_CHIPS)]
            _load_weights(loads, sem)
            ubuf[0:A_HALO, :] = jnp.zeros((A_HALO, D_CONV), F32)
            bbuf[0:P_HALO, :] = jnp.zeros((P_HALO, D_POOL), F32)

        xv = x_ref[...]
        r = lax.rsqrt(_rowmean(xv * xv) + EPS)
        h1 = (xv * r * g1_ref[...]).astype(BF16)
        h1_ref[...] = h1
        proj = _dot(h1, win_v[...])
        proj_ref[...] = proj.astype(BF16)
        av, ag, bi = proj[:, :D_CONV], proj[:, D_CONV:2 * D_CONV], proj[:, 2 * D_CONV:]
        ubuf[A_HALO:A_HALO + tr, :] = av * _sigmoid(ag)
        off = A_HALO - (CONV_A - 1)
        uview = _shifted_views(ubuf, ushift, tr)
        acc = wa_ref[0:1, :] * uview(off)
        for j in range(1, CONV_A):
            acc = acc + wa_ref[j:j + 1, :] * uview(off + j)
        cv = acc + cb_ref[...]
        ubuf[0:A_HALO, :] = ubuf[tr:tr + A_HALO, :]
        c_ref[...] = cv.astype(BF16)
        xc = cv - _rowmean(cv)
        z = xc * lax.rsqrt(_rowmean(xc * xc) + EPS)
        ln = z * lg_ref[...] + lb_ref[...]
        ya = ln * _sigmoid(ln)
        bbuf[P_HALO:P_HALO + tr, :] = bi
        ds, ybs = [], []
        for g, w in enumerate(POOL_WINDOWS):
            cols = slice(g * POOL_GROUP, (g + 1) * POOL_GROUP)
            s = bi[:, cols]
            for kk in range(1, w):
                s = s + bbuf[P_HALO - kk:P_HALO - kk + tr, cols]
            dg = s / _pool_count(i, tr, w) - bi[:, cols]
            ds.append(dg)
            ybs.append(_dot(dg.astype(BF16), pw_ref[g].astype(BF16)))
        bbuf[0:P_HALO, :] = bbuf[tr:tr + P_HALO, :]
        d_ref[...] = jnp.concatenate(ds, axis=1).astype(BF16)
        yb = jnp.concatenate(ybs, axis=1) * ps_ref[...]
        m = jnp.concatenate([ya, yb], axis=1).astype(BF16)
        m_ref[...] = m
        x1_ref[...] = xv + _dot(m, wout_v[...])

        @pl.when(i == n - 1)
        def _():
            later()[1]()
            taps()[1](1)

    tile = lambda w: pl.BlockSpec((tr, w), lambda i: (i, 0))
    full = lambda a: pl.BlockSpec(a.shape, lambda i: (0,) * a.ndim)
    return pl.pallas_call(
        body, name="mixer_fwd", grid=(n,),
        in_specs=[tile(D_MODEL), full(g1)] + [ANY] * 5 + [full(cb), full(lg), full(lb), full(pw), full(ps)],
        out_specs=[tile(D_MODEL), tile(D_IN), tile(D_CONV), tile(D_POOL), tile(D_MODEL), tile(D_MODEL)] + [ANY] * 5,
        out_shape=[
            jax.ShapeDtypeStruct((seq, D_MODEL), BF16), jax.ShapeDtypeStruct((seq, D_IN), BF16),
            jax.ShapeDtypeStruct((seq, D_CONV), BF16), jax.ShapeDtypeStruct((seq, D_POOL), BF16),
            jax.ShapeDtypeStruct((seq, D_MODEL), BF16), jax.ShapeDtypeStruct((seq, D_MODEL), F32),
            jax.ShapeDtypeStruct((D_MODEL, D_IN), BF16), jax.ShapeDtypeStruct((D_MODEL, D_MODEL), BF16),
            jax.ShapeDtypeStruct((D_MODEL, 2 * D_FF), BF16),
            jax.ShapeDtypeStruct((N_CHIPS,) + wa_s.shape, F32), jax.ShapeDtypeStruct((N_CHIPS,) + wf_s.shape, F32),
        ],
        scratch_shapes=[
            pltpu.VMEM((D_MODEL, D_IN), BF16), pltpu.VMEM((D_MODEL, D_MODEL), BF16), pltpu.VMEM((32, D_CONV), F32),
            pltpu.VMEM((tr + A_HALO, D_CONV), F32), pltpu.VMEM((7, tr + A_HALO - 8, D_CONV), F32),
            pltpu.VMEM((tr + P_HALO, D_POOL), F32), pltpu.SemaphoreType.DMA((2 + N_CHIPS,)),
        ] + _gather_sems(2) + _gather_sems(1) + [
            pltpu.SemaphoreType.DMA((6,)), pltpu.SemaphoreType.DMA((6,)), pltpu.SemaphoreType.DMA((2,))],
        compiler_params=pltpu.CompilerParams(dimension_semantics=("arbitrary",), vmem_limit_bytes=VMEM_LIMIT),
    )(x, g1, win_b, wout_b, wup_b, wa_s, wf_s, cb, lg, lb, pw, ps)


def _ffn_up(x1, g2, wup, wf, fb, wdown_b, tile_rows):
    seq = x1.shape[0]
    tr = tile_rows
    n = seq // tr

    def body(x1_ref, g2_ref, wup_hbm, wf_ref, fb_ref, wdown_b_hbm,
             h2_ref, up_ref, gc_ref, act_ref, wdown_f, wup_v, gbuf, sem, *gsems):
        i = pl.program_id(0)

        def gather():
            return _gather_ops((wdown_b_hbm,), (wdown_f,), (False,), gsems)

        @pl.when(i == 0)
        def _():
            gather()[0]()
            _load_weights(((wup_hbm, wup_v),), sem)
            gbuf[0:8, :] = jnp.zeros((8, D_FF), F32)

        x1v = x1_ref[...]
        r2 = lax.rsqrt(_rowmean(x1v * x1v) + EPS)
        h2 = (x1v * r2 * g2_ref[...]).astype(BF16)
        h2_ref[...] = h2

        def up_proj(j):
            return (_dot(h2, wup_v[:, j * FF_CHUNK:(j + 1) * FF_CHUNK]),
                    _dot(h2, wup_v[:, D_FF + j * FF_CHUNK:D_FF + (j + 1) * FF_CHUNK]))

        ahead = up_proj(0)
        for j in range(N_FF_CHUNKS):
            cs = slice(j * FF_CHUNK, (j + 1) * FF_CHUNK)
            vs = slice(D_FF + j * FF_CHUNK, D_FF + (j + 1) * FF_CHUNK)
            gate, val = ahead
            if j + 1 < N_FF_CHUNKS:
                ahead = up_proj(j + 1)
            up_ref[:, cs] = gate.astype(BF16)
            up_ref[:, vs] = val.astype(BF16)
            gbuf[8:8 + tr, cs] = gate
            gc = (wf_ref[0:1, cs] * gbuf[6:6 + tr, cs] + wf_ref[1:2, cs] * gbuf[7:7 + tr, cs]
                  + wf_ref[2:3, cs] * gate + fb_ref[:, cs])
            gbuf[0:8, cs] = gbuf[tr:tr + 8, cs]
            gc_ref[:, cs] = gc.astype(BF16)
            act_ref[:, cs] = (gc * _sigmoid(gc) * val).astype(BF16)

        @pl.when(i == n - 1)
        def _():
            gather()[1]()

    tile = lambda w: pl.BlockSpec((tr, w), lambda i: (i, 0))
    full = lambda a: pl.BlockSpec(a.shape, lambda i: (0,) * a.ndim)
    return pl.pallas_call(
        body, name="ffn_up", grid=(n,),
        in_specs=[tile(D_MODEL), full(g2), ANY, full(wf), full(fb), ANY],
        out_specs=[tile(D_MODEL), tile(2 * D_FF), tile(D_FF), tile(D_FF), ANY],
        out_shape=[
            jax.ShapeDtypeStruct((seq, D_MODEL), BF16), jax.ShapeDtypeStruct((seq, 2 * D_FF), BF16),
            jax.ShapeDtypeStruct((seq, D_FF), BF16), jax.ShapeDtypeStruct((seq, D_FF), BF16),
            jax.ShapeDtypeStruct((D_FF, D_MODEL), BF16),
        ],
        scratch_shapes=[pltpu.VMEM(wup.shape, BF16), pltpu.VMEM((tr + 8, D_FF), F32), pltpu.SemaphoreType.DMA((1,))]
        + _gather_sems(1),
        compiler_params=pltpu.CompilerParams(dimension_semantics=("arbitrary",), vmem_limit_bytes=VMEM_LIMIT),
    )(x1, g2, wup, wf, fb, wdown_b)


def _ffn_down(x1, act, wdown, g3, target, tile_rows):
    seq = x1.shape[0]
    tr = tile_rows
    n = seq // tr

    def body(x1_ref, act_ref, wdown_hbm, g3_ref, t_ref, dx2_ref, dx2b_ref, sm_ref, wdown_v, sem):
        i = pl.program_id(0)

        @pl.when(i == 0)
        def _():
            _load_weights(((wdown_hbm, wdown_v),), sem)
            sm_ref[...] = jnp.zeros(sm_ref.shape, F32)

        x2 = x1_ref[...] + _dot(act_ref[...], wdown_v[...])
        r3 = lax.rsqrt(_rowmean(x2 * x2) + EPS)
        n3 = x2 * r3
        err = n3 * g3_ref[...] - t_ref[...]
        dy = err / D_MODEL
        sm_ref[2:3, :] += _colsum(dy * n3)
        loss = 0.5 * _colsum(_rowmean(err * err))
        sm_ref[3:4, :] += jnp.broadcast_to(loss, (1, D_MODEL))
        dn = dy * g3_ref[...]
        dx2v = r3 * (dn - n3 * _rowmean(dn * n3))
        dx2_ref[...] = dx2v
        dx2b_ref[...] = dx2v.astype(BF16)

    tile = lambda w: pl.BlockSpec((tr, w), lambda i: (i, 0))
    full = lambda a: pl.BlockSpec(a.shape, lambda i: (0,) * a.ndim)
    return pl.pallas_call(
        body, name="ffn_down", grid=(n,),
        in_specs=[tile(D_MODEL), tile(D_FF), ANY, full(g3), tile(D_MODEL)],
        out_specs=[tile(D_MODEL), tile(D_MODEL), pl.BlockSpec((8, D_MODEL), lambda i: (0, 0))],
        out_shape=[
            jax.ShapeDtypeStruct((seq, D_MODEL), F32), jax.ShapeDtypeStruct((seq, D_MODEL), BF16),
            jax.ShapeDtypeStruct((8, D_MODEL), F32),
        ],
        scratch_shapes=[pltpu.VMEM(wdown.shape, BF16), pltpu.SemaphoreType.DMA((1,))],
        compiler_params=pltpu.CompilerParams(dimension_semantics=("arbitrary",), vmem_limit_bytes=VMEM_LIMIT),
    )(x1, act, wdown, g3, target)


def _ffn_bwd(dx2, up, gcs, x1, g2, wup, wf, wdown, comm, tile_rows):
    seq = x1.shape[0]
    c_ins, c_shapes, c_sems, c_ops = _comm_plan(comm)
    nc = len(c_ins)
    tr = tile_rows
    n = seq // tr

    def body(dx2_ref, up_ref, gc_ref, x1_ref, g2_ref, wup_hbm, wf_ref, wdown_hbm, *rest):
        c_in, rest = rest[:nc], rest[nc:]
        dup_ref, dx1_ref, dx1b_ref, sm_ref, sf_ref = rest[:5]
        c_out, rest = rest[5:5 + nc], rest[5 + nc:]
        wup_v, wdown_v, dbuf, dcar, sem = rest[:5]
        c_sem_refs = rest[5:]
        i = pl.program_id(0)

        @pl.when(i == 0)
        def _():
            c_ops(c_in, c_out, c_sem_refs)[0]()
            _load_weights(((wup_hbm, wup_v), (wdown_hbm, wdown_v)), sem)
            dcar[...] = jnp.zeros(dcar.shape, F32)
            sm_ref[...] = jnp.zeros(sm_ref.shape, F32)
            sf_ref[...] = jnp.zeros(sf_ref.shape, F32)

        dx2v = dx2_ref[...]
        dx2b = dx2v.astype(BF16)
        dh2 = jnp.zeros((tr, D_MODEL), F32)

        def down_t(j):
            return _dot_nt(dx2b, wdown_v[j * FF_CHUNK:(j + 1) * FF_CHUNK, :])

        ahead = down_t(0)
        for j in range(N_FF_CHUNKS):
            cs = slice(j * FF_CHUNK, (j + 1) * FF_CHUNK)
            vs = slice(D_FF + j * FF_CHUNK, D_FF + (j + 1) * FF_CHUNK)
            dact = ahead
            if j + 1 < N_FF_CHUNKS:
                ahead = down_t(j + 1)
            gate = up_ref[:, cs].astype(F32)
            val = up_ref[:, vs].astype(F32)
            gc = gc_ref[:, cs].astype(F32)
            sg = _sigmoid(gc)
            dval = dact * (gc * sg)
            dgc = dact * val * (sg * (1.0 + gc * (1.0 - sg)))
            dbuf[0:tr, :] = dgc
            dbuf[tr:tr + 8, :] = dcar[:, cs]
            d_p1 = dbuf[1:1 + tr, :]
            d_p2 = dbuf[2:2 + tr, :]
            dgate = wf_ref[2:3, cs] * dgc + wf_ref[1:2, cs] * d_p1 + wf_ref[0:1, cs] * d_p2
            dcar[:, cs] = dgc[0:8, :]
            sf_ref[0:1, cs] += _colsum(d_p2 * gate)
            sf_ref[1:2, cs] += _colsum(d_p1 * gate)
            sf_ref[2:3, cs] += _colsum(dgc * gate)
            sf_ref[3:4, cs] += _colsum(dgc)
            dgb, dvb = dgate.astype(BF16), dval.astype(BF16)
            dup_ref[:, cs] = dgb
            dup_ref[:, vs] = dvb
            dh2 = dh2 + _dot_nt(dgb, wup_v[:, cs]) + _dot_nt(dvb, wup_v[:, vs])
        x1v = x1_ref[...]
        r2 = lax.rsqrt(_rowmean(x1v * x1v) + EPS)
        n2 = x1v * r2
        sm_ref[1:2, :] += _colsum(dh2 * n2)
        dn2 = dh2 * g2_ref[...]
        dx1v = dx2v + r2 * (dn2 - n2 * _rowmean(dn2 * n2))
        dx1_ref[...] = dx1v
        dx1b_ref[...] = dx1v.astype(BF16)

        @pl.when(i == n - 1)
        def _():
            c_ops(c_in, c_out, c_sem_refs)[1]()

    tile = lambda w: pl.BlockSpec((tr, w), lambda i: (n - 1 - i, 0))
    full = lambda a: pl.BlockSpec(a.shape, lambda i: (0,) * a.ndim)
    acc = lambda rows, w: pl.BlockSpec((rows, w), lambda i: (0, 0))
    return pl.pallas_call(
        body, name="ffn_bwd", grid=(n,),
        in_specs=[tile(D_MODEL), tile(2 * D_FF), tile(D_FF), tile(D_MODEL), full(g2), ANY, full(wf), ANY] + [ANY] * nc,
        out_specs=[tile(2 * D_FF), tile(D_MODEL), tile(D_MODEL), acc(8, D_MODEL), acc(8, D_FF)] + [ANY] * nc,
        out_shape=[
            jax.ShapeDtypeStruct((seq, 2 * D_FF), BF16), jax.ShapeDtypeStruct((seq, D_MODEL), F32),
            jax.ShapeDtypeStruct((seq, D_MODEL), BF16), jax.ShapeDtypeStruct((8, D_MODEL), F32),
            jax.ShapeDtypeStruct((8, D_FF), F32),
        ] + c_shapes,
        scratch_shapes=[
            pltpu.VMEM(wup.shape, BF16), pltpu.VMEM(wdown.shape, BF16),
            pltpu.VMEM((tr + 8, FF_CHUNK), F32), pltpu.VMEM((8, D_FF), F32), pltpu.SemaphoreType.DMA((2,)),
        ] + c_sems,
        compiler_params=pltpu.CompilerParams(dimension_semantics=("arbitrary",), vmem_limit_bytes=VMEM_LIMIT),
    )(dx2, up, gcs, x1, g2, wup, wf, wdown, *c_ins)


def _mixer_bwd(dx1, x, proj, cpre, d, g1, win, wa, lg, lb, pw, ps, wout, parts, tile_rows):
    seq = x.shape[0]
    n_parts = len(parts)
    tr = tile_rows
    n = seq // tr
    row_cb, row_lg, row_lb, row_ps = 32, 33, 34, 35

    def body(dx1_ref, x_ref, proj_ref, projh_ref, c_ref, d_ref, g1_ref, win_hbm, wa_ref, lg_ref, lb_ref, pw_ref, ps_ref,
             wout_hbm, *rest):
        part_refs, rest = rest[:n_parts], rest[n_parts:]
        dproj_ref, gx_ref, sm_ref, s5_ref, sp_ref = rest[:5]
        land_refs, rest = rest[5:5 + n_parts], rest[5 + n_parts:]
        win_v, wout_v, ubuf, ushift, dcbuf, dshift, ebuf, sem = rest[:8]
        ssems = rest[8:]
        i = pl.program_id(0)
        tile = n - 1 - i

        def scatter():
            return _scatter_ops(part_refs, land_refs, n_parts, ssems)

        @pl.when(i == 0)
        def _():
            scatter()[0]()
            _load_weights(((win_hbm, win_v), (wout_hbm, wout_v)), sem)
            dcbuf[tr:tr + A_HALO, :] = jnp.zeros((A_HALO, D_CONV), F32)
            ebuf[tr:tr + P_HALO, :] = jnp.zeros((P_HALO, D_POOL), F32)
            sm_ref[...] = jnp.zeros(sm_ref.shape, F32)
            s5_ref[...] = jnp.zeros(s5_ref.shape, F32)
            sp_ref[...] = jnp.zeros(sp_ref.shape, F32)

        dx1v = dx1_ref[...]
        dm = _dot_nt(dx1v.astype(BF16), wout_v[...])
        dya, dyb = dm[:, :D_CONV], dm[:, D_CONV:]
        dbis = []
        for g, w in enumerate(POOL_WINDOWS):
            cols = slice(g * POOL_GROUP, (g + 1) * POOL_GROUP)
            dgb = d_ref[:, cols]
            pwb = pw_ref[g].astype(BF16)
            dyg = dyb[:, cols]
            s5_ref[row_ps:row_ps + 1, cols] += _colsum(dyg * _dot(dgb, pwb))
            dqb = (dyg * ps_ref[:, cols]).astype(BF16)
            sp_ref[g] += _dot_tn(dgb, dqb)
            dd = _dot_nt(dqb, pwb)
            e = dd / _pool_count(tile, tr, w)
            ebuf[0:tr, cols] = e
            s = e
            for kk in range(1, w):
                s = s + ebuf[kk:kk + tr, cols]
            dbis.append(s - dd)
        ebuf[tr:tr + P_HALO, :] = ebuf[0:P_HALO, :]
        cv = c_ref[...].astype(F32)
        xc = cv - _rowmean(cv)
        rs = lax.rsqrt(_rowmean(xc * xc) + EPS)
        z = xc * rs
        ln = z * lg_ref[...] + lb_ref[...]
        sl = _sigmoid(ln)
        dl = dya * (sl * (1.0 + ln * (1.0 - sl)))
        s5_ref[row_lg:row_lg + 1, :] += _colsum(dl * z)
        s5_ref[row_lb:row_lb + 1, :] += _colsum(dl)
        dz = dl * lg_ref[...]
        dc = rs * (dz - _rowmean(dz) - z * _rowmean(dz * z))
        s5_ref[row_cb:row_cb + 1, :] += _colsum(dc)
        dcbuf[0:tr, :] = dc
        keep = (tile > 0).astype(F32)
        avh = projh_ref[:, :D_CONV].astype(F32)
        agh = projh_ref[:, D_CONV:].astype(F32)
        ubuf[0:A_HALO, :] = avh * _sigmoid(agh) * keep
        av = proj_ref[:, :D_CONV].astype(F32)
        ag = proj_ref[:, D_CONV:2 * D_CONV].astype(F32)
        sg = _sigmoid(ag)
        ubuf[A_HALO:A_HALO + tr, :] = av * sg
        off = A_HALO - (CONV_A - 1)
        du = wa_ref[CONV_A - 1:CONV_A, :] * dc
        dview = _shifted_views(dcbuf, dshift, tr)
        uview = _shifted_views(ubuf, ushift, tr)
        for j in range(CONV_A - 1):
            du = du + wa_ref[j:j + 1, :] * dview(CONV_A - 1 - j)
        for j in range(CONV_A):
            s5_ref[j:j + 1, :] += _colsum(dc * uview(off + j))
        dcbuf[tr:tr + A_HALO, :] = dcbuf[0:A_HALO, :]
        dav = du * sg
        dag = du * av * (sg * (1.0 - sg))
        dprojb = jnp.concatenate([dav, dag] + dbis, axis=1).astype(BF16)
        dproj_ref[...] = dprojb
        dh1 = _dot_nt(dprojb, win_v[...])
        xv = x_ref[...]
        r1 = lax.rsqrt(_rowmean(xv * xv) + EPS)
        n1 = xv * r1
        sm_ref[0:1, :] += _colsum(dh1 * n1)
        dn1 = dh1 * g1_ref[...]
        gx_ref[...] = dx1v + r1 * (dn1 - n1 * _rowmean(dn1 * n1))

        @pl.when(i == n - 1)
        def _():
            scatter()[1]()

    tile = lambda w: pl.BlockSpec((tr, w), lambda i: (n - 1 - i, 0))
    full = lambda a: pl.BlockSpec(a.shape, lambda i: (0,) * a.ndim)
    halo = pl.BlockSpec((A_HALO, 2 * D_CONV), lambda i: (jnp.maximum((n - 1 - i) * (tr // A_HALO) - 1, 0), 0))
    acc = lambda shape: pl.BlockSpec(shape, lambda i: (0,) * len(shape))
    return pl.pallas_call(
        body, name="mixer_bwd", grid=(n,),
        in_specs=[tile(D_MODEL), tile(D_MODEL), tile(D_IN), halo, tile(D_CONV), tile(D_POOL), full(g1), ANY, full(wa),
                  full(lg), full(lb), full(pw), full(ps), ANY] + [ANY] * n_parts,
        out_specs=[tile(D_IN), tile(D_MODEL), acc((8, D_MODEL)), acc((40, D_CONV)), acc(pw.shape)] + [ANY] * n_parts,
        out_shape=[
            jax.ShapeDtypeStruct((seq, D_IN), BF16), jax.ShapeDtypeStruct((seq, D_MODEL), F32),
            jax.ShapeDtypeStruct((8, D_MODEL), F32), jax.ShapeDtypeStruct((40, D_CONV), F32),
            jax.ShapeDtypeStruct(pw.shape, F32),
        ] + _scatter_shapes(parts, ()),
        scratch_shapes=[
            pltpu.VMEM(win.shape, BF16), pltpu.VMEM(wout.shape, BF16),
            pltpu.VMEM((tr + A_HALO, D_CONV), F32), pltpu.VMEM((7, tr + A_HALO - 8, D_CONV), F32),
            pltpu.VMEM((tr + A_HALO, D_CONV), F32), pltpu.VMEM((7, tr + A_HALO - 8, D_CONV), F32),
            pltpu.VMEM((tr + P_HALO, D_POOL), F32), pltpu.SemaphoreType.DMA((2,)),
        ] + _scatter_sems(n_parts),
        compiler_params=pltpu.CompilerParams(dimension_semantics=("arbitrary",), vmem_limit_bytes=VMEM_LIMIT),
    )(dx1, x, proj, proj, cpre, d, g1, win, wa, lg, lb, pw, ps, wout, *parts)


def _weight_grad(a, b, layout, k_rows, comm=None):
    seq, m_dim = a.shape
    n_dim = b.shape[1]
    steps = seq // k_rows

    def store(o_ref, acc, index, value):
        if steps == 1:
            o_ref[index] = value.astype(BF16)
            return
        s = pl.program_id(1)

        @pl.when(s == 0)
        def _():
            acc[index] = value

        @pl.when(jnp.logical_and(s > 0, s < steps - 1))
        def _():
            acc[index] += value

        @pl.when(s == steps - 1)
        def _():
            o_ref[index] = (acc[index] + value).astype(BF16)

    if layout in ("rows1", "rows2"):
        groups = int(layout[-1])
        per_tile = N_CHIPS // groups
        rows = m_dim // N_CHIPS // 2
        a_w = m_dim // groups

        def body(a_ref, b_ref, o_ref, acc):
            r = _dot_tn(a_ref[...], b_ref[...])
            for p in range(per_tile):
                for h in range(2):
                    store(o_ref, acc, (p, h), r[(2 * p + h) * rows:(2 * p + h + 1) * rows, :])

        in_specs = [pl.BlockSpec((k_rows, a_w), lambda g, s: (s, g)), pl.BlockSpec((k_rows, n_dim), lambda g, s: (s, 0))]
        out_spec = pl.BlockSpec((per_tile, 2, rows, n_dim), lambda g, s: (g, 0, 0, 0))
        out_dims, acc_dims = (N_CHIPS, 2, rows, n_dim), (per_tile, 2, rows, n_dim)
    elif layout == "cols_chip":
        groups = N_CHIPS
        rows, cols = m_dim // 2, n_dim // N_CHIPS

        def body(a_ref, b_ref, o_ref, acc):
            r = _dot_tn(a_ref[...], b_ref[...])
            for h in range(2):
                store(o_ref, acc, h, r[h * rows:(h + 1) * rows, :])

        in_specs = [pl.BlockSpec((k_rows, m_dim), lambda g, s: (s, 0)), pl.BlockSpec((k_rows, cols), lambda g, s: (s, g))]
        out_spec = pl.BlockSpec((None, 2, rows, cols), lambda g, s: (g, 0, 0, 0))
        out_dims, acc_dims = (N_CHIPS, 2, rows, cols), (2, rows, cols)
    else:
        groups = 2
        rows, cols = m_dim // 2, n_dim // N_CHIPS

        def body(a_ref, b_ref, o_ref, acc):
            r = _dot_tn(a_ref[...], b_ref[...])
            for k in range(N_CHIPS):
                store(o_ref, acc, k, r[:, k * cols:(k + 1) * cols])

        in_specs = [pl.BlockSpec((k_rows, rows), lambda g, s: (s, g)), pl.BlockSpec((k_rows, n_dim), lambda g, s: (s, 0))]
        out_spec = pl.BlockSpec((N_CHIPS, None, rows, cols), lambda g, s: (0, g, 0, 0))
        out_dims, acc_dims = (N_CHIPS, 2, rows, cols), (N_CHIPS, rows, cols)

    c_ins, c_shapes, c_sems, c_ops = _comm_plan(comm)
    nc = len(c_ins)

    def hosted(a_ref, b_ref, *rest):
        c_in, o_ref, c_out, acc, sems = rest[:nc], rest[nc], rest[nc + 1:2 * nc + 1], rest[2 * nc + 1], rest[2 * nc + 2:]
        g, s = pl.program_id(0), pl.program_id(1)
        if nc:
            @pl.when(jnp.logical_and(g == 0, s == 0))
            def _():
                c_ops(c_in, c_out, sems)[0]()

        body(a_ref, b_ref, o_ref, acc)
        if nc:
            @pl.when(jnp.logical_and(g == groups - 1, s == steps - 1))
            def _():
                c_ops(c_in, c_out, sems)[1]()

    outs = pl.pallas_call(
        hosted, name=f"weight_grad_{layout}_{m_dim}x{n_dim}", grid=(groups, steps),
        in_specs=in_specs + [ANY] * nc, out_specs=[out_spec] + [ANY] * nc,
        out_shape=[jax.ShapeDtypeStruct(out_dims, BF16)] + c_shapes,
        scratch_shapes=[pltpu.VMEM(acc_dims, F32)] + c_sems,
        compiler_params=pltpu.CompilerParams(dimension_semantics=("arbitrary", "arbitrary"), vmem_limit_bytes=VMEM_LIMIT),
    )(a, b, *c_ins)
    return outs if nc else outs[0]


def _exchange_ops(ins, outs, n_big, sems):
    send, recv = sems
    x, y, c, _, _ = _place()
    cps = [pltpu.make_async_remote_copy(
        src_ref=ins[t].at[:, 1 - c] if t < n_big else ins[t], dst_ref=outs[t], send_sem=send.at[t], recv_sem=recv.at[t],
        device_id=(x, y, 1 - c), device_id_type=MESH) for t in range(len(ins))]

    def start():
        for cp in cps:
            cp.start()

    def finish():
        for cp in cps:
            cp.wait()

    return start, finish


def _exchange_shapes(bigs, smalls):
    return [jax.ShapeDtypeStruct((N_CHIPS,) + b.shape[2:], b.dtype) for b in bigs] + [
        jax.ShapeDtypeStruct(s.shape, s.dtype) for s in smalls]


def _comm_plan(comm):
    if comm is None:
        return (), [], [], None
    kind, arrays = comm
    n = len(arrays)
    if kind == "scatter":
        return tuple(arrays), _scatter_shapes(arrays, ()), _scatter_sems(n), lambda i, o, sm: _scatter_ops(i, o, n, sm)
    return (tuple(arrays), _exchange_shapes(arrays, ()), [pltpu.SemaphoreType.DMA((n,))] * 2,
            lambda i, o, sm: _exchange_ops(i, o, n, sm))


def _sibling_exchange(bigs, smalls, tag):
    nb, nt = len(bigs), len(bigs) + len(smalls)

    def body(*refs):
        start, finish = _exchange_ops(refs[:nt], refs[nt:2 * nt], nb, refs[2 * nt:])
        start()
        finish()

    return pl.pallas_call(
        body, name=f"sibling_exchange_{tag}", out_shape=_exchange_shapes(bigs, smalls),
        in_specs=[ANY] * nt, out_specs=[ANY] * nt,
        scratch_shapes=[pltpu.SemaphoreType.DMA((nt,)), pltpu.SemaphoreType.DMA((nt,))],
    )(*bigs, *smalls)


def _pair_sum(core, mine, theirs, tag, block_rows):
    _, _, rows, cols = mine.shape
    steps = rows // block_rows

    def body(core_ref, a_ref, b_ref, o_ref):
        o_ref[...] = (a_ref[...].astype(F32) + b_ref[...].astype(F32)).astype(BF16)

    grid_spec = pltpu.PrefetchScalarGridSpec(
        num_scalar_prefetch=1, grid=(N_CHIPS, steps),
        in_specs=[pl.BlockSpec((None, None, block_rows, cols), lambda k, r, core_ref: (k, core_ref[0], r, 0)),
                  pl.BlockSpec((None, block_rows, cols), lambda k, r, core_ref: (k, r, 0))],
        out_specs=pl.BlockSpec((None, block_rows, cols), lambda k, r, core_ref: (k, r, 0)),
    )
    return pl.pallas_call(
        body, name=f"pair_sum_{tag}", grid_spec=grid_spec,
        out_shape=jax.ShapeDtypeStruct((N_CHIPS, rows, cols), BF16),
        compiler_params=pltpu.CompilerParams(dimension_semantics=("arbitrary", "arbitrary"), vmem_limit_bytes=VMEM_LIMIT),
    )(core, mine, theirs)


def _pair_sum_small(mine, theirs):
    (m_f2, m_b1, m_b2, m_sf, m_s5, m_sp) = mine

    def body(a0, a1, a2, a3, a4, a5, b0, b1, b2, b3, b4, b5, o_m, o_f, o_5, o_p):
        sm = (a0[...] + a1[...] + a2[...]) + (b0[...] + b1[...] + b2[...])
        sf = a3[...] + b3[...]
        s5 = a4[...] + b4[...]
        for h in range(2):
            o_m[h] = sm[:, h * (D_MODEL // 2):(h + 1) * (D_MODEL // 2)]
            o_f[h] = sf[:, h * (D_FF // 2):(h + 1) * (D_FF // 2)]
            o_5[h] = s5[:, h * (D_CONV // 2):(h + 1) * (D_CONV // 2)]
            for g in range(2):
                o_p[h, g] = a5[2 * h + g] + b5[2 * h + g]

    out_shape = [
        jax.ShapeDtypeStruct((2, 8, D_MODEL // 2), F32), jax.ShapeDtypeStruct((2, 8, D_FF // 2), F32),
        jax.ShapeDtypeStruct((2, 40, D_CONV // 2), F32), jax.ShapeDtypeStruct((2, 2, POOL_GROUP, POOL_GROUP), F32),
    ]
    return pl.pallas_call(body, name="pair_sum_small", out_shape=out_shape, in_specs=[VMEM] * 12, out_specs=[VMEM] * 4)(
        *mine, *theirs)


def _scatter_ops(ins, outs, n_parts, sems):
    ici_send, ici_recv, fwd_send, fwd_recv, loc_sem = sems
    nt = len(ins)
    x, y, c, k, chips = _place()

    def src_of(t, kk):
        return ins[t].at[kk] if t < n_parts else ins[t].at[c]

    def ici(t, j, kk, slot):
        return pltpu.make_async_remote_copy(
            src_ref=src_of(t, kk), dst_ref=outs[t].at[slot, c], send_sem=ici_send.at[t * 3 + j],
            recv_sem=ici_recv.at[t * 3 + j], device_id=(*chips[j], c), device_id_type=MESH)

    def fwd(t, j, q, half, src=None):
        slot = outs[t].at[q, half]
        return pltpu.make_async_remote_copy(
            src_ref=slot if src is None else src, dst_ref=slot, send_sem=fwd_send.at[t * 4 + j],
            recv_sem=fwd_recv.at[t * 4 + j], device_id=(x, y, 1 - c), device_id_type=MESH)

    local = [pltpu.make_async_copy(src_of(t, k), outs[t].at[k, c], loc_sem.at[t]) for t in range(nt)]
    peers = [(t, j, 2 * qx + qy) for t in range(nt) for j, (qx, qy) in enumerate(chips)]
    sends = [fwd(t, 3, k, c, src=src_of(t, k)) for t in range(nt)]
    sends += [ici(t, j, kq, k) for t, j, kq in peers]

    def start():
        for cp in local + sends:
            cp.start()

    def finish():
        passed = []
        for t, j, kq in peers:
            ici(t, j, kq, kq).wait_recv()
            cp = fwd(t, j, kq, c)
            cp.start()
            passed.append(cp)
        for t in range(nt):
            fwd(t, 3, k, 1 - c).wait_recv()
        for t, j, kq in peers:
            fwd(t, j, kq, 1 - c).wait_recv()
        for cp in sends + passed:
            cp.wait_send()
        for cp in local:
            cp.wait()

    return start, finish


def _scatter_sems(nt):
    return [pltpu.SemaphoreType.DMA((3 * nt,))] * 2 + [pltpu.SemaphoreType.DMA((4 * nt,))] * 2 + [pltpu.SemaphoreType.DMA((nt,))]


def _scatter_shapes(parts, smalls):
    return [jax.ShapeDtypeStruct((N_CHIPS, 2) + p.shape[1:], p.dtype) for p in tuple(parts) + tuple(smalls)]


def _chip_scatter(parts, smalls):
    nt = len(parts) + len(smalls)

    def body(*refs):
        start, finish = _scatter_ops(refs[:nt], refs[nt:2 * nt], len(parts), refs[2 * nt:])
        start()
        finish()

    return pl.pallas_call(
        body, name="chip_scatter", out_shape=_scatter_shapes(parts, smalls), in_specs=[ANY] * nt, out_specs=[ANY] * nt,
        scratch_shapes=_scatter_sems(nt),
    )(*parts, *smalls)


def _adamw(w, g, m, v):
    m = ADAM_B1 * m + (1.0 - ADAM_B1) * g
    v = ADAM_B2 * v + (1.0 - ADAM_B2) * (g * g)
    m_hat = m / (1.0 - ADAM_B1 ** ADAM_STEP)
    v_hat = v / (1.0 - ADAM_B2 ** ADAM_STEP)
    delta = -ADAM_LR * (m_hat / (jnp.sqrt(v_hat) + ADAM_EPS) + ADAM_WD * w)
    return delta, m, v


def _adam_big(parts, w, m, v, tag, block_rows):
    _, _, rows, cols = parts.shape
    steps = rows // block_rows

    def body(p_ref, w_ref, m_ref, v_ref, g_out, d_out, m_out, v_out):
        g = p_ref[0].astype(F32)
        for q in range(1, N_CHIPS):
            g = g + p_ref[q].astype(F32)
        delta, m_new, v_new = _adamw(w_ref[...], g, m_ref[...], v_ref[...])
        g_out[...] = g
        d_out[...] = delta
        m_out[...] = m_new
        v_out[...] = v_new

    blk = pl.BlockSpec((block_rows, cols), lambda h, r: (h * steps + r, 0))
    return pl.pallas_call(
        body, name=f"adam_{tag}", grid=(2, steps),
        in_specs=[pl.BlockSpec((N_CHIPS, None, block_rows, cols), lambda h, r: (0, h, r, 0)), blk, blk, blk],
        out_specs=[blk] * 4, out_shape=[jax.ShapeDtypeStruct(w.shape, F32)] * 4,
        compiler_params=pltpu.CompilerParams(dimension_semantics=("arbitrary", "arbitrary"), vmem_limit_bytes=VMEM_LIMIT),
    )(parts, w, m, v)


def _reduce_small(l_m, l_f, l_5, l_p):
    def total(ref):
        t = ref[0]
        for q in range(1, N_CHIPS):
            t = t + ref[q]
        return t

    def body(m_ref, f_ref, s_ref, p_ref, g1_o, g2_o, g3_o, loss_o, wf_o, fb_o, wa_o, cb_o, lg_o, lb_o, ps_o, pw_o):
        tm, tf, t5, tp = total(m_ref), total(f_ref), total(s_ref), total(p_ref)
        sm = jnp.concatenate([tm[0], tm[1]], axis=1)
        sf = jnp.concatenate([tf[0], tf[1]], axis=1)
        s5 = jnp.concatenate([t5[0], t5[1]], axis=1)
        g1_o[...] = sm[0:1]
        g2_o[...] = sm[1:2]
        g3_o[...] = sm[2:3]
        loss_o[...] = sm[3:4, 0:128]
        wf_o[...] = sf
        fb_o[...] = sf[3:4]
        wa_o[...] = s5[0:32]
        cb_o[...] = s5[32:33]
        lg_o[...] = s5[33:34]
        lb_o[...] = s5[34:35]
        ps_o[...] = s5[35:36]
        for h in range(2):
            for g in range(2):
                pw_o[2 * h + g] = tp[h, g]

    row = lambda w: jax.ShapeDtypeStruct((1, w), F32)
    out_shape = [row(D_MODEL), row(D_MODEL), row(D_MODEL), row(128), jax.ShapeDtypeStruct((8, D_FF), F32), row(D_FF),
                 jax.ShapeDtypeStruct((32, D_CONV), F32), row(D_CONV), row(D_CONV), row(D_CONV), row(D_POOL),
                 jax.ShapeDtypeStruct((4, POOL_GROUP, POOL_GROUP), F32)]
    return pl.pallas_call(body, name="reduce_small", out_shape=out_shape, in_specs=[VMEM] * 4, out_specs=[VMEM] * 12)(
        l_m, l_f, l_5, l_p)


def _adam_small(ws, gs, ms, vs):
    count = len(ws)

    def body(*refs):
        w_r, g_r, m_r, v_r = (refs[t * count:(t + 1) * count] for t in range(4))
        d_o, m_o, v_o = (refs[(4 + t) * count:(5 + t) * count] for t in range(3))
        for t in range(count):
            delta, m_new, v_new = _adamw(w_r[t][...], g_r[t][...], m_r[t][...], v_r[t][...])
            d_o[t][...] = delta
            m_o[t][...] = m_new
            v_o[t][...] = v_new

    out_shape = [jax.ShapeDtypeStruct(w.shape, F32) for w in ws] * 3
    outs = pl.pallas_call(body, name="adam_small", out_shape=out_shape, in_specs=[VMEM] * (4 * count),
                          out_specs=[VMEM] * (3 * count))(*ws, *gs, *ms, *vs)
    return outs[:count], outs[count:2 * count], outs[2 * count:]


MIX_TILE = 512
FFN_TILE = 256
GRAD_K = 2048


def kernel(x, norm_mix_g, w_in, conv_a_w, conv_a_b, ln_a_g, ln_a_b, pool_w, pool_scale, w_out, norm_ffn_g, w_up, conv_f_w, conv_f_b, w_down, norm_final_g, loss_target, m_norm_mix_g, m_w_in, m_conv_a_w, m_conv_a_b, m_ln_a_g, m_ln_a_b, m_pool_w, m_pool_scale, m_w_out, m_norm_ffn_g, m_w_up, m_conv_f_w, m_conv_f_b, m_w_down, m_norm_final_g, v_norm_mix_g, v_w_in, v_conv_a_w, v_conv_a_b, v_ln_a_g, v_ln_a_b, v_pool_w, v_pool_scale, v_w_out, v_norm_ffn_g, v_w_up, v_conv_f_w, v_conv_f_b, v_w_down, v_norm_final_g):
    seq = x.shape[1]
    xs, ts = x[0], loss_target[0]
    mix_tile, ffn_tile, grad_k = min(MIX_TILE, seq), min(FFN_TILE, seq), min(GRAD_K, seq)
    chip = 2 * lax.axis_index("x") + lax.axis_index("y")
    core = lax.axis_index("c").astype(jnp.int32).reshape(1)

    wa_s = jnp.pad(conv_a_w[0], ((0, 32 - CONV_A), (0, 0)))
    wf_s = jnp.pad(conv_f_w[0], ((0, 8 - CONV_F), (0, 0)))
    win_b, wout_b, wup_b, wdown_b = _cast_shards(w_in[0], w_out[0], w_up[0], w_down[0])
    g3 = norm_final_g.reshape(1, D_MODEL)
    pw = pool_w[0]

    h1, proj, cpre, dpool, mcat, x1, win, wout, wup, wa_g, wf_g = _mixer_fwd(
        xs, norm_mix_g, win_b, wout_b, wup_b, wa_s, wf_s, conv_a_b, ln_a_g, ln_a_b, pw, pool_scale, mix_tile)
    wa = jnp.transpose(wa_g, (1, 0, 2)).reshape(32, D_CONV)
    wf = jnp.transpose(wf_g, (1, 0, 2)).reshape(8, D_FF)
    h2, up, gcs, act, wdown = _ffn_up(x1, norm_ffn_g, wup, wf, conv_f_b, wdown_b, ffn_tile)
    dx2, dx2b, sm_f2 = _ffn_down(x1, act, wdown, g3, ts, mix_tile)
    tags = ("w_in", "w_out", "w_up", "w_down")
    blocks = (256, 128, 256, 176)
    g_wdown = _weight_grad(act, dx2b, "rows2", grad_k)
    dup, dx1, dx1b, sm_b1, sf, l_wdown = _ffn_bwd(
        dx2, up, gcs, x1, norm_ffn_g, wup, wf, wdown, ("exchange", [g_wdown]), ffn_tile)
    p_wdown = _pair_sum(core, g_wdown, l_wdown, tags[3], blocks[3])
    g_wup, s_wdown = _weight_grad(h2, dup, "cols_chip", grad_k, ("scatter", [p_wdown]))
    g_wout, l_wup = _weight_grad(mcat, dx1b, "rows1", grad_k, ("exchange", [g_wup]))
    p_wup = _pair_sum(core, g_wup, l_wup, tags[2], blocks[2])
    l_wout, = _sibling_exchange((g_wout,), (), "early")
    p_wout = _pair_sum(core, g_wout, l_wout, tags[1], blocks[1])
    dproj, grad_x, sm_b2, s5, sp, s_wout, s_wup = _mixer_bwd(
        dx1, xs, proj, cpre, dpool, norm_mix_g, win, wa, ln_a_g, ln_a_b, pw, pool_scale, wout, [p_wout, p_wup], mix_tile)
    g_win = _weight_grad(h1, dproj, "cols_half", grad_k)

    smalls = (sm_f2, sm_b1, sm_b2, sf, s5, sp)
    landed = _sibling_exchange((g_win,), smalls, "late")
    part_win = _pair_sum(core, g_win, landed[0], tags[0], blocks[0])
    small_parts = _pair_sum_small(smalls, landed[1:])
    late = _chip_scatter([part_win], small_parts)
    scattered = [late[0], s_wout, s_wup, s_wdown] + list(late[1:])

    big_w = (w_in[0], w_out[0], w_up[0], w_down[0])
    big_m = (m_w_in[0], m_w_out[0], m_w_up[0], m_w_down[0])
    big_v = (v_w_in[0], v_w_out[0], v_w_up[0], v_w_down[0])
    big = {}
    for tag, p, w, m, v, br in zip(tags, scattered[:4], big_w, big_m, big_v, blocks):
        big[tag] = [a[None] for a in _adam_big(p, w, m, v, tag, br)]

    (g_g1, g_g2, g_g3, loss_row, g_wf_all, g_fb, g_wa_all, g_cb, g_lg, g_lb, g_ps, g_pw) = _reduce_small(*scattered[4:])
    g_wa = lax.dynamic_slice(g_wa_all, (0, chip * (D_CONV // N_CHIPS)), (32, D_CONV // N_CHIPS))[:CONV_A]
    g_wf = lax.dynamic_slice(g_wf_all, (0, chip * (D_FF // N_CHIPS)), (8, D_FF // N_CHIPS))[:CONV_F]
    small_names = ("norm_mix_g", "conv_a_w", "conv_a_b", "ln_a_g", "ln_a_b", "pool_w", "pool_scale", "norm_ffn_g",
                   "conv_f_w", "conv_f_b", "norm_final_g")
    small_w = (norm_mix_g, conv_a_w[0], conv_a_b, ln_a_g, ln_a_b, pw, pool_scale, norm_ffn_g, conv_f_w[0], conv_f_b, g3)
    small_m = (m_norm_mix_g, m_conv_a_w[0], m_conv_a_b, m_ln_a_g, m_ln_a_b, m_pool_w[0], m_pool_scale, m_norm_ffn_g,
               m_conv_f_w[0], m_conv_f_b, m_norm_final_g.reshape(1, D_MODEL))
    small_v = (v_norm_mix_g, v_conv_a_w[0], v_conv_a_b, v_ln_a_g, v_ln_a_b, v_pool_w[0], v_pool_scale, v_norm_ffn_g,
               v_conv_f_w[0], v_conv_f_b, v_norm_final_g.reshape(1, D_MODEL))
    small_g = (g_g1, g_wa, g_cb, g_lg, g_lb, g_pw, g_ps, g_g2, g_wf, g_fb, g_g3)
    s_delta, s_m, s_v = _adam_small(small_w, small_g, small_m, small_v)
    shapes = {"conv_a_w": conv_a_w.shape, "pool_w": pool_w.shape, "conv_f_w": conv_f_w.shape, "norm_final_g": norm_final_g.shape}
    small = {}
    for t, name in enumerate(small_names):
        shp = shapes.get(name)
        small[name] = [a if shp is None else a.reshape(shp) for a in (small_g[t], s_delta[t], s_m[t], s_v[t])]

    order = ("norm_mix_g", "w_in", "conv_a_w", "conv_a_b", "ln_a_g", "ln_a_b", "pool_w", "pool_scale", "w_out", "norm_ffn_g",
             "w_up", "conv_f_w", "conv_f_b", "w_down", "norm_final_g")
    table = {**big, **small}
    loss = loss_row[0, 0]
    outs = [loss, grad_x[None]]
    for t in range(4):
        outs += [table[name][t] for name in order]
    return tuple(outs)
```

```python
import functools

import jax
import jax.numpy as jnp
from jax import lax
from jax.experimental import pallas as pl
from jax.experimental.pallas import tpu as pltpu

F32 = jnp.float32
BF16 = jnp.bfloat16
EPS = 1e-6
ADAM_LR = 0.001
ADAM_B1 = 0.9
ADAM_B2 = 0.999
ADAM_EPS = 1e-08
ADAM_WD = 0.01
ADAM_STEP = 10

D_MODEL = 1024
D_CONV = 512
D_POOL = 512
D_IN = 1536
D_FF = 2816
CONV_A = 31
CONV_F = 3
POOL_WINDOWS = (2, 4, 8, 16)
POOL_GROUP = 128
N_CHIPS = 4
FF_CHUNK = 256
N_FF_CHUNKS = D_FF // FF_CHUNK
A_HALO = 32
P_HALO = 16
VMEM_LIMIT = 56 * 1024 * 1024
MESH = pl.DeviceIdType.MESH

ANY = pl.BlockSpec(memory_space=pl.ANY)
VMEM = pl.BlockSpec(memory_space=pltpu.VMEM)


def _dot(a, b):
    return jnp.dot(a, b, preferred_element_type=F32)


def _dot_nt(a, b):
    return lax.dot_general(a, b, (((1,), (1,)), ((), ())), preferred_element_type=F32)


def _dot_tn(a, b):
    return lax.dot_general(a, b, (((0,), (0,)), ((), ())), preferred_element_type=F32)


def _sigmoid(v):
    return jax.nn.sigmoid(v)


def _colsum(v):
    return jnp.sum(v, axis=0, keepdims=True)


def _rowmean(v):
    return jnp.mean(v, axis=-1, keepdims=True)


def _place():
    x, y, c = lax.axis_index("x"), lax.axis_index("y"), lax.axis_index("c")
    chips = [(1 - x, y), (x, 1 - y), (1 - x, 1 - y)]
    return x, y, c, 2 * x + y, chips


def _gather_ops(bufs, fulls, col_sharded, sems):
    ici_send, ici_recv, fwd_send, fwd_recv, loc_sem = sems
    n_big = len(bufs)
    x, y, c, k, chips = _place()

    def block(i, kk, half=None):
        rows, cols = bufs[i].shape
        if col_sharded[i]:
            rs = slice(None) if half is None else pl.ds(pl.multiple_of(half * (rows // 2), 16), rows // 2)
            return fulls[i].at[rs, pl.ds(pl.multiple_of(kk * cols, 128), cols)]
        if half is None:
            return fulls[i].at[pl.ds(pl.multiple_of(kk * rows, 16), rows), :]
        return fulls[i].at[pl.ds(pl.multiple_of(kk * rows + half * (rows // 2), 16), rows // 2), :]

    def my_half(i):
        rows = bufs[i].shape[0]
        return bufs[i].at[pl.ds(pl.multiple_of(c * (rows // 2), 16), rows // 2), :]

    def ici(i, j, kk):
        return pltpu.make_async_remote_copy(
            src_ref=my_half(i), dst_ref=block(i, kk, c), send_sem=ici_send.at[i * 3 + j], recv_sem=ici_recv.at[i * 3 + j],
            device_id=(*chips[j], c), device_id_type=MESH)

    def fwd(i, j, kk, half):
        return pltpu.make_async_remote_copy(
            src_ref=block(i, kk, half), dst_ref=block(i, kk, half), send_sem=fwd_send.at[i * 3 + j],
            recv_sem=fwd_recv.at[i * 3 + j], device_id=(x, y, 1 - c), device_id_type=MESH)

    local = [pltpu.make_async_copy(bufs[i], block(i, k), loc_sem.at[i]) for i in range(n_big)]
    sends = [ici(i, j, k) for i in range(n_big) for j in range(3)]
    peers = [(i, j, 2 * qx + qy) for i in range(n_big) for j, (qx, qy) in enumerate(chips)]

    def start():
        for cp in local + sends:
            cp.start()

    def finish():
        passed = []
        for i, j, kq in peers:
            ici(i, j, kq).wait_recv()
            cp = fwd(i, j, kq, c)
            cp.start()
            passed.append(cp)
        for i, j, kq in peers:
            fwd(i, j, kq, 1 - c).wait_recv()
        for cp in sends + passed:
            cp.wait_send()
        for cp in local:
            cp.wait()

    return start, finish


def _gather_sems(n_big):
    return [pltpu.SemaphoreType.DMA((3 * n_big,))] * 4 + [pltpu.SemaphoreType.DMA((n_big,))]


def _tap_ops(srcs, dsts, sems):
    send, recv, loc = sems
    _, _, c, k, chips = _place()

    def copy(t, j, kk):
        return pltpu.make_async_remote_copy(
            src_ref=srcs[t], dst_ref=dsts[t].at[kk], send_sem=send.at[t * 3 + j], recv_sem=recv.at[t * 3 + j],
            device_id=(*chips[j], c), device_id_type=MESH)

    local = [pltpu.make_async_copy(srcs[t], dsts[t].at[k], loc.at[t]) for t in range(len(srcs))]
    sends = [[copy(t, j, k) for j in range(3)] for t in range(len(srcs))]

    def start():
        for t, cp in enumerate(local):
            cp.start()
            for sd in sends[t]:
                sd.start()

    def wait(t):
        for j, (qx, qy) in enumerate(chips):
            copy(t, j, 2 * qx + qy).wait_recv()
        for sd in sends[t]:
            sd.wait_send()
        local[t].wait()

    return start, wait


def _cast_shards(*shards):
    def body(*refs):
        for src, dst in zip(refs[:len(shards)], refs[len(shards):]):
            dst[...] = src[...].astype(BF16)

    return pl.pallas_call(
        body, name="cast_shards", out_shape=[jax.ShapeDtypeStruct(s.shape, BF16) for s in shards],
        in_specs=[VMEM] * len(shards), out_specs=[VMEM] * len(shards),
        compiler_params=pltpu.CompilerParams(vmem_limit_bytes=VMEM_LIMIT),
    )(*shards)


def _load_weights(pairs, sem):
    cps = [pltpu.make_async_copy(src, dst, sem.at[i]) for i, (src, dst) in enumerate(pairs)]
    for cp in cps:
        cp.start()
    for cp in cps:
        cp.wait()


def _shifted_views(buf, shifted, t_rows):
    n = t_rows + A_HALO - 8
    for b in range(1, 8):
        shifted[b - 1] = buf[b:b + n, :]

    def view(offset):
        a, b = divmod(offset, 8)
        if b == 0:
            return buf[8 * a:8 * a + t_rows, :]
        return shifted[b - 1, 8 * a:8 * a + t_rows, :]

    return view


def _pool_count(tile, t_rows, w):
    row = lax.broadcasted_iota(jnp.int32, (t_rows, POOL_GROUP), 0) + tile * t_rows
    return jnp.minimum(row + 1, w).astype(F32)


def _mixer_fwd(x, g1, win_b, wout_b, wup_b, wa_s, wf_s, cb, lg, lb, pw, ps, tile_rows):
    seq = x.shape[0]
    tr = tile_rows
    n = seq // tr

    def body(x_ref, g1_ref, win_b_hbm, wout_b_hbm, wup_b_hbm, wa_s_hbm, wf_s_hbm, cb_ref, lg_ref, lb_ref, pw_ref,
             ps_ref, h1_ref, proj_ref, c_ref, d_ref, m_ref, x1_ref, win_f, wout_f, wup_f, wa_g, wf_g,
             win_v, wout_v, wa_ref, ubuf, ushift, bbuf, sem, *csems):
        i = pl.program_id(0)
        first_sems, later_sems, tap_sems = csems[0:5], csems[5:10], csems[10:13]

        def first():
            return _gather_ops((win_b_hbm, wout_b_hbm), (win_f, wout_f), (True, False), first_sems)

        def later():
            return _gather_ops((wup_b_hbm,), (wup_f,), (True,), later_sems)

        def taps():
            return _tap_ops((wa_s_hbm, wf_s_hbm), (wa_g, wf_g), tap_sems)

        @pl.when(i == 0)
        def _():
            first()[0]()
            taps()[0]()
            later()[0]()
            first()[1]()
            taps()[1](0)
            loads = [(win_f, win_v), (wout_f, wout_v)]
            loads += [(wa_g.at[kk], wa_ref.at[:, kk * (D_CONV // N_CHIPS):(kk + 1) * (D_CONV // N_CHIPS)]) for kk in range(N_CHIPS)]
            _load_weights(loads, sem)
            ubuf[0:A_HALO, :] = jnp.zeros((A_HALO, D_CONV), F32)
            bbuf[0:P_HALO, :] = jnp.zeros((P_HALO, D_POOL), F32)

        xv = x_ref[...]
        r = lax.rsqrt(_rowmean(xv * xv) + EPS)
        h1 = (xv * r * g1_ref[...]).astype(BF16)
        h1_ref[...] = h1
        proj = _dot(h1, win_v[...])
        proj_ref[...] = proj.astype(BF16)
        av, ag, bi = proj[:, :D_CONV], proj[:, D_CONV:2 * D_CONV], proj[:, 2 * D_CONV:]
        ubuf[A_HALO:A_HALO + tr, :] = av * _sigmoid(ag)
        off = A_HALO - (CONV_A - 1)
        uview = _shifted_views(ubuf, ushift, tr)
        acc = wa_ref[0:1, :] * uview(off)
        for j in range(1, CONV_A):
            acc = acc + wa_ref[j:j + 1, :] * uview(off + j)
        cv = acc + cb_ref[...]
        ubuf[0:A_HALO, :] = ubuf[tr:tr + A_HALO, :]
        c_ref[...] = cv.astype(BF16)
        xc = cv - _rowmean(cv)
        z = xc * lax.rsqrt(_rowmean(xc * xc) + EPS)
        ln = z * lg_ref[...] + lb_ref[...]
        ya = ln * _sigmoid(ln)
        bbuf[P_HALO:P_HALO + tr, :] = bi
        ds, ybs = [], []
        for g, w in enumerate(POOL_WINDOWS):
            cols = slice(g * POOL_GROUP, (g + 1) * POOL_GROUP)
            s = bi[:, cols]
            for kk in range(1, w):
                s = s + bbuf[P_HALO - kk:P_HALO - kk + tr, cols]
            dg = s / _pool_count(i, tr, w) - bi[:, cols]
            ds.append(dg)
            ybs.append(_dot(dg.astype(BF16), pw_ref[g].astype(BF16)))
        bbuf[0:P_HALO, :] = bbuf[tr:tr + P_HALO, :]
        d_ref[...] = jnp.concatenate(ds, axis=1).astype(BF16)
        yb = jnp.concatenate(ybs, axis=1) * ps_ref[...]
        m = jnp.concatenate([ya, yb], axis=1).astype(BF16)
        m_ref[...] = m
        x1_ref[...] = xv + _dot(m, wout_v[...])

        @pl.when(i == n - 1)
        def _():
            later()[1]()
            taps()[1](1)

    tile = lambda w: pl.BlockSpec((tr, w), lambda i: (i, 0))
    full = lambda a: pl.BlockSpec(a.shape, lambda i: (0,) * a.ndim)
    return pl.pallas_call(
        body, name="mixer_fwd", grid=(n,),
        in_specs=[tile(D_MODEL), full(g1)] + [ANY] * 5 + [full(cb), full(lg), full(lb), full(pw), full(ps)],
        out_specs=[tile(D_MODEL), tile(D_IN), tile(D_CONV), tile(D_POOL), tile(D_MODEL), tile(D_MODEL)] + [ANY] * 5,
        out_shape=[
            jax.ShapeDtypeStruct((seq, D_MODEL), BF16), jax.ShapeDtypeStruct((seq, D_IN), BF16),
            jax.ShapeDtypeStruct((seq, D_CONV), BF16), jax.ShapeDtypeStruct((seq, D_POOL), BF16),
            jax.ShapeDtypeStruct((seq, D_MODEL), BF16), jax.ShapeDtypeStruct((seq, D_MODEL), F32),
            jax.ShapeDtypeStruct((D_MODEL, D_IN), BF16), jax.ShapeDtypeStruct((D_MODEL, D_MODEL), BF16),
            jax.ShapeDtypeStruct((D_MODEL, 2 * D_FF), BF16),
            jax.ShapeDtypeStruct((N_CHIPS,) + wa_s.shape, F32), jax.ShapeDtypeStruct((N_CHIPS,) + wf_s.shape, F32),
        ],
        scratch_shapes=[
            pltpu.VMEM((D_MODEL, D_IN), BF16), pltpu.VMEM((D_MODEL, D_MODEL), BF16), pltpu.VMEM((32, D_CONV), F32),
            pltpu.VMEM((tr + A_HALO, D_CONV), F32), pltpu.VMEM((7, tr + A_HALO - 8, D_CONV), F32),
            pltpu.VMEM((tr + P_HALO, D_POOL), F32), pltpu.SemaphoreType.DMA((2 + N_CHIPS,)),
        ] + _gather_sems(2) + _gather_sems(1) + [
            pltpu.SemaphoreType.DMA((6,)), pltpu.SemaphoreType.DMA((6,)), pltpu.SemaphoreType.DMA((2,))],
        compiler_params=pltpu.CompilerParams(dimension_semantics=("arbitrary",), vmem_limit_bytes=VMEM_LIMIT),
    )(x, g1, win_b, wout_b, wup_b, wa_s, wf_s, cb, lg, lb, pw, ps)


def _ffn_up(x1, g2, wup, wf, fb, wdown_b, tile_rows):
    seq = x1.shape[0]
    tr = tile_rows
    n = seq // tr

    def body(x1_ref, g2_ref, wup_hbm, wf_ref, fb_ref, wdown_b_hbm,
             h2_ref, up_ref, gc_ref, act_ref, wdown_f, wup_v, gbuf, sem, *gsems):
        i = pl.program_id(0)

        def gather():
            return _gather_ops((wdown_b_hbm,), (wdown_f,), (False,), gsems)

        @pl.when(i == 0)
        def _():
            gather()[0]()
            _load_weights(((wup_hbm, wup_v),), sem)
            gbuf[0:8, :] = jnp.zeros((8, D_FF), F32)

        x1v = x1_ref[...]
        r2 = lax.rsqrt(_rowmean(x1v * x1v) + EPS)
        h2 = (x1v * r2 * g2_ref[...]).astype(BF16)
        h2_ref[...] = h2

        def up_proj(j):
            return (_dot(h2, wup_v[:, j * FF_CHUNK:(j + 1) * FF_CHUNK]),
                    _dot(h2, wup_v[:, D_FF + j * FF_CHUNK:D_FF + (j + 1) * FF_CHUNK]))

        ahead = up_proj(0)
        for j in range(N_FF_CHUNKS):
            cs = slice(j * FF_CHUNK, (j + 1) * FF_CHUNK)
            vs = slice(D_FF + j * FF_CHUNK, D_FF + (j + 1) * FF_CHUNK)
            gate, val = ahead
            if j + 1 < N_FF_CHUNKS:
                ahead = up_proj(j + 1)
            up_ref[:, cs] = gate.astype(BF16)
            up_ref[:, vs] = val.astype(BF16)
            gbuf[8:8 + tr, cs] = gate
            gc = (wf_ref[0:1, cs] * gbuf[6:6 + tr, cs] + wf_ref[1:2, cs] * gbuf[7:7 + tr, cs]
                  + wf_ref[2:3, cs] * gate + fb_ref[:, cs])
            gbuf[0:8, cs] = gbuf[tr:tr + 8, cs]
            gc_ref[:, cs] = gc.astype(BF16)
            act_ref[:, cs] = (gc * _sigmoid(gc) * val).astype(BF16)

        @pl.when(i == n - 1)
        def _():
            gather()[1]()

    tile = lambda w: pl.BlockSpec((tr, w), lambda i: (i, 0))
    full = lambda a: pl.BlockSpec(a.shape, lambda i: (0,) * a.ndim)
    return pl.pallas_call(
        body, name="ffn_up", grid=(n,),
        in_specs=[tile(D_MODEL), full(g2), ANY, full(wf), full(fb), ANY],
        out_specs=[tile(D_MODEL), tile(2 * D_FF), tile(D_FF), tile(D_FF), ANY],
        out_shape=[
            jax.ShapeDtypeStruct((seq, D_MODEL), BF16), jax.ShapeDtypeStruct((seq, 2 * D_FF), BF16),
            jax.ShapeDtypeStruct((seq, D_FF), BF16), jax.ShapeDtypeStruct((seq, D_FF), BF16),
            jax.ShapeDtypeStruct((D_FF, D_MODEL), BF16),
        ],
        scratch_shapes=[pltpu.VMEM(wup.shape, BF16), pltpu.VMEM((tr + 8, D_FF), F32), pltpu.SemaphoreType.DMA((1,))]
        + _gather_sems(1),
        compiler_params=pltpu.CompilerParams(dimension_semantics=("arbitrary",), vmem_limit_bytes=VMEM_LIMIT),
    )(x1, g2, wup, wf, fb, wdown_b)


def _ffn_down(x1, act, wdown, g3, target, tile_rows):
    seq = x1.shape[0]
    tr = tile_rows
    n = seq // tr

    def body(x1_ref, act_ref, wdown_hbm, g3_ref, t_ref, dx2_ref, dx2b_ref, sm_ref, wdown_v, sem):
        i = pl.program_id(0)

        @pl.when(i == 0)
        def _():
            _load_weights(((wdown_hbm, wdown_v),), sem)
            sm_ref[...] = jnp.zeros(sm_ref.shape, F32)

        x2 = x1_ref[...] + _dot(act_ref[...], wdown_v[...])
        r3 = lax.rsqrt(_rowmean(x2 * x2) + EPS)
        n3 = x2 * r3
        err = n3 * g3_ref[...] - t_ref[...]
        dy = err / D_MODEL
        sm_ref[2:3, :] += _colsum(dy * n3)
        loss = 0.5 * _colsum(_rowmean(err * err))
        sm_ref[3:4, :] += jnp.broadcast_to(loss, (1, D_MODEL))
        dn = dy * g3_ref[...]
        dx2v = r3 * (dn - n3 * _rowmean(dn * n3))
        dx2_ref[...] = dx2v
        dx2b_ref[...] = dx2v.astype(BF16)

    tile = lambda w: pl.BlockSpec((tr, w), lambda i: (i, 0))
    full = lambda a: pl.BlockSpec(a.shape, lambda i: (0,) * a.ndim)
    return pl.pallas_call(
        body, name="ffn_down", grid=(n,),
        in_specs=[tile(D_MODEL), tile(D_FF), ANY, full(g3), tile(D_MODEL)],
        out_specs=[tile(D_MODEL), tile(D_MODEL), pl.BlockSpec((8, D_MODEL), lambda i: (0, 0))],
        out_shape=[
            jax.ShapeDtypeStruct((seq, D_MODEL), F32), jax.ShapeDtypeStruct((seq, D_MODEL), BF16),
            jax.ShapeDtypeStruct((8, D_MODEL), F32),
        ],
        scratch_shapes=[pltpu.VMEM(wdown.shape, BF16), pltpu.SemaphoreType.DMA((1,))],
        compiler_params=pltpu.CompilerParams(dimension_semantics=("arbitrary",), vmem_limit_bytes=VMEM_LIMIT),
    )(x1, act, wdown, g3, target)


def _ffn_bwd(dx2, up, gcs, x1, g2, wup, wf, wdown, comm, tile_rows):
    seq = x1.shape[0]
    c_ins, c_shapes, c_sems, c_ops = _comm_plan(comm)
    nc = len(c_ins)
    tr = tile_rows
    n = seq // tr

    def body(dx2_ref, up_ref, gc_ref, x1_ref, g2_ref, wup_hbm, wf_ref, wdown_hbm, *rest):
        c_in, rest = rest[:nc], rest[nc:]
        dup_ref, dx1_ref, dx1b_ref, sm_ref, sf_ref = rest[:5]
        c_out, rest = rest[5:5 + nc], rest[5 + nc:]
        wup_v, wdown_v, dbuf, dcar, sem = rest[:5]
        c_sem_refs = rest[5:]
        i = pl.program_id(0)

        @pl.when(i == 0)
        def _():
            c_ops(c_in, c_out, c_sem_refs)[0]()
            _load_weights(((wup_hbm, wup_v), (wdown_hbm, wdown_v)), sem)
            dcar[...] = jnp.zeros(dcar.shape, F32)
            sm_ref[...] = jnp.zeros(sm_ref.shape, F32)
            sf_ref[...] = jnp.zeros(sf_ref.shape, F32)

        dx2v = dx2_ref[...]
        dx2b = dx2v.astype(BF16)
        dh2 = jnp.zeros((tr, D_MODEL), F32)

        def down_t(j):
            return _dot_nt(dx2b, wdown_v[j * FF_CHUNK:(j + 1) * FF_CHUNK, :])

        ahead = down_t(0)
        for j in range(N_FF_CHUNKS):
            cs = slice(j * FF_CHUNK, (j + 1) * FF_CHUNK)
            vs = slice(D_FF + j * FF_CHUNK, D_FF + (j + 1) * FF_CHUNK)
            dact = ahead
            if j + 1 < N_FF_CHUNKS:
                ahead = down_t(j + 1)
            gate = up_ref[:, cs].astype(F32)
            val = up_ref[:, vs].astype(F32)
            gc = gc_ref[:, cs].astype(F32)
            sg = _sigmoid(gc)
            dval = dact * (gc * sg)
            dgc = dact * val * (sg * (1.0 + gc * (1.0 - sg)))
            dbuf[0:tr, :] = dgc
            dbuf[tr:tr + 8, :] = dcar[:, cs]
            d_p1 = dbuf[1:1 + tr, :]
            d_p2 = dbuf[2:2 + tr, :]
            dgate = wf_ref[2:3, cs] * dgc + wf_ref[1:2, cs] * d_p1 + wf_ref[0:1, cs] * d_p2
            dcar[:, cs] = dgc[0:8, :]
            sf_ref[0:1, cs] += _colsum(d_p2 * gate)
            sf_ref[1:2, cs] += _colsum(d_p1 * gate)
            sf_ref[2:3, cs] += _colsum(dgc * gate)
            sf_ref[3:4, cs] += _colsum(dgc)
            dgb, dvb = dgate.astype(BF16), dval.astype(BF16)
            dup_ref[:, cs] = dgb
            dup_ref[:, vs] = dvb
            dh2 = dh2 + _dot_nt(dgb, wup_v[:, cs]) + _dot_nt(dvb, wup_v[:, vs])
        x1v = x1_ref[...]
        r2 = lax.rsqrt(_rowmean(x1v * x1v) + EPS)
        n2 = x1v * r2
        sm_ref[1:2, :] += _colsum(dh2 * n2)
        dn2 = dh2 * g2_ref[...]
        dx1v = dx2v + r2 * (dn2 - n2 * _rowmean(dn2 * n2))
        dx1_ref[...] = dx1v
        dx1b_ref[...] = dx1v.astype(BF16)

        @pl.when(i == n - 1)
        def _():
            c_ops(c_in, c_out, c_sem_refs)[1]()

    tile = lambda w: pl.BlockSpec((tr, w), lambda i: (n - 1 - i, 0))
    full = lambda a: pl.BlockSpec(a.shape, lambda i: (0,) * a.ndim)
    acc = lambda rows, w: pl.BlockSpec((rows, w), lambda i: (0, 0))
    return pl.pallas_call(
        body, name="ffn_bwd", grid=(n,),
        in_specs=[tile(D_MODEL), tile(2 * D_FF), tile(D_FF), tile(D_MODEL), full(g2), ANY, full(wf), ANY] + [ANY] * nc,
        out_specs=[tile(2 * D_FF), tile(D_MODEL), tile(D_MODEL), acc(8, D_MODEL), acc(8, D_FF)] + [ANY] * nc,
        out_shape=[
            jax.ShapeDtypeStruct((seq, 2 * D_FF), BF16), jax.ShapeDtypeStruct((seq, D_MODEL), F32),
            jax.ShapeDtypeStruct((seq, D_MODEL), BF16), jax.ShapeDtypeStruct((8, D_MODEL), F32),
            jax.ShapeDtypeStruct((8, D_FF), F32),
        ] + c_shapes,
        scratch_shapes=[
            pltpu.VMEM(wup.shape, BF16), pltpu.VMEM(wdown.shape, BF16),
            pltpu.VMEM((tr + 8, FF_CHUNK), F32), pltpu.VMEM((8, D_FF), F32), pltpu.SemaphoreType.DMA((2,)),
        ] + c_sems,
        compiler_params=pltpu.CompilerParams(dimension_semantics=("arbitrary",), vmem_limit_bytes=VMEM_LIMIT),
    )(dx2, up, gcs, x1, g2, wup, wf, wdown, *c_ins)


def _mixer_bwd(dx1, x, proj, cpre, d, g1, win, wa, lg, lb, pw, ps, wout, parts, tile_rows):
    seq = x.shape[0]
    n_parts = len(parts)
    tr = tile_rows
    n = seq // tr
    row_cb, row_lg, row_lb, row_ps = 32, 33, 34, 35

    def body(dx1_ref, x_ref, proj_ref, projh_ref, c_ref, d_ref, g1_ref, win_hbm, wa_ref, lg_ref, lb_ref, pw_ref, ps_ref,
             wout_hbm, *rest):
        part_refs, rest = rest[:n_parts], rest[n_parts:]
        dproj_ref, gx_ref, sm_ref, s5_ref, sp_ref = rest[:5]
        land_refs, rest = rest[5:5 + n_parts], rest[5 + n_parts:]
        win_v, wout_v, ubuf, ushift, dcbuf, dshift, ebuf, sem = rest[:8]
        ssems = rest[8:]
        i = pl.program_id(0)
        tile = n - 1 - i

        def scatter():
            return _scatter_ops(part_refs, land_refs, n_parts, ssems)

        @pl.when(i == 0)
        def _():
            scatter()[0]()
            _load_weights(((win_hbm, win_v), (wout_hbm, wout_v)), sem)
            dcbuf[tr:tr + A_HALO, :] = jnp.zeros((A_HALO, D_CONV), F32)
            ebuf[tr:tr + P_HALO, :] = jnp.zeros((P_HALO, D_POOL), F32)
            sm_ref[...] = jnp.zeros(sm_ref.shape, F32)
            s5_ref[...] = jnp.zeros(s5_ref.shape, F32)
            sp_ref[...] = jnp.zeros(sp_ref.shape, F32)

        dx1v = dx1_ref[...]
        dm = _dot_nt(dx1v.astype(BF16), wout_v[...])
        dya, dyb = dm[:, :D_CONV], dm[:, D_CONV:]
        dbis = []
        for g, w in enumerate(POOL_WINDOWS):
            cols = slice(g * POOL_GROUP, (g + 1) * POOL_GROUP)
            dgb = d_ref[:, cols]
            pwb = pw_ref[g].astype(BF16)
            dyg = dyb[:, cols]
            s5_ref[row_ps:row_ps + 1, cols] += _colsum(dyg * _dot(dgb, pwb))
            dqb = (dyg * ps_ref[:, cols]).astype(BF16)
            sp_ref[g] += _dot_tn(dgb, dqb)
            dd = _dot_nt(dqb, pwb)
            e = dd / _pool_count(tile, tr, w)
            ebuf[0:tr, cols] = e
            s = e
            for kk in range(1, w):
                s = s + ebuf[kk:kk + tr, cols]
            dbis.append(s - dd)
        ebuf[tr:tr + P_HALO, :] = ebuf[0:P_HALO, :]
        cv = c_ref[...].astype(F32)
        xc = cv - _rowmean(cv)
        rs = lax.rsqrt(_rowmean(xc * xc) + EPS)
        z = xc * rs
        ln = z * lg_ref[...] + lb_ref[...]
        sl = _sigmoid(ln)
        dl = dya * (sl * (1.0 + ln * (1.0 - sl)))
        s5_ref[row_lg:row_lg + 1, :] += _colsum(dl * z)
        s5_ref[row_lb:row_lb + 1, :] += _colsum(dl)
        dz = dl * lg_ref[...]
        dc = rs * (dz - _rowmean(dz) - z * _rowmean(dz * z))
        s5_ref[row_cb:row_cb + 1, :] += _colsum(dc)
        dcbuf[0:tr, :] = dc
        keep = (tile > 0).astype(F32)
        avh = projh_ref[:, :D_CONV].astype(F32)
        agh = projh_ref[:, D_CONV:].astype(F32)
        ubuf[0:A_HALO, :] = avh * _sigmoid(agh) * keep
        av = proj_ref[:, :D_CONV].astype(F32)
        ag = proj_ref[:, D_CONV:2 * D_CONV].astype(F32)
        sg = _sigmoid(ag)
        ubuf[A_HALO:A_HALO + tr, :] = av * sg
        off = A_HALO - (CONV_A - 1)
        du = wa_ref[CONV_A - 1:CONV_A, :] * dc
        dview = _shifted_views(dcbuf, dshift, tr)
        uview = _shifted_views(ubuf, ushift, tr)
        for j in range(CONV_A - 1):
            du = du + wa_ref[j:j + 1, :] * dview(CONV_A - 1 - j)
        for j in range(CONV_A):
            s5_ref[j:j + 1, :] += _colsum(dc * uview(off + j))
        dcbuf[tr:tr + A_HALO, :] = dcbuf[0:A_HALO, :]
        dav = du * sg
        dag = du * av * (sg * (1.0 - sg))
        dprojb = jnp.concatenate([dav, dag] + dbis, axis=1).astype(BF16)
        dproj_ref[...] = dprojb
        dh1 = _dot_nt(dprojb, win_v[...])
        xv = x_ref[...]
        r1 = lax.rsqrt(_rowmean(xv * xv) + EPS)
        n1 = xv * r1
        sm_ref[0:1, :] += _colsum(dh1 * n1)
        dn1 = dh1 * g1_ref[...]
        gx_ref[...] = dx1v + r1 * (dn1 - n1 * _rowmean(dn1 * n1))

        @pl.when(i == n - 1)
        def _():
            scatter()[1]()

    tile = lambda w: pl.BlockSpec((tr, w), lambda i: (n - 1 - i, 0))
    full = lambda a: pl.BlockSpec(a.shape, lambda i: (0,) * a.ndim)
    halo = pl.BlockSpec((A_HALO, 2 * D_CONV), lambda i: (jnp.maximum((n - 1 - i) * (tr // A_HALO) - 1, 0), 0))
    acc = lambda shape: pl.BlockSpec(shape, lambda i: (0,) * len(shape))
    return pl.pallas_call(
        body, name="mixer_bwd", grid=(n,),
        in_specs=[tile(D_MODEL), tile(D_MODEL), tile(D_IN), halo, tile(D_CONV), tile(D_POOL), full(g1), ANY, full(wa),
                  full(lg), full(lb), full(pw), full(ps), ANY] + [ANY] * n_parts,
        out_specs=[tile(D_IN), tile(D_MODEL), acc((8, D_MODEL)), acc((40, D_CONV)), acc(pw.shape)] + [ANY] * n_parts,
        out_shape=[
            jax.ShapeDtypeStruct((seq, D_IN), BF16), jax.ShapeDtypeStruct((seq, D_MODEL), F32),
            jax.ShapeDtypeStruct((8, D_MODEL), F32), jax.ShapeDtypeStruct((40, D_CONV), F32),
            jax.ShapeDtypeStruct(pw.shape, F32),
        ] + _scatter_shapes(parts, ()),
        scratch_shapes=[
            pltpu.VMEM(win.shape, BF16), pltpu.VMEM(wout.shape, BF16),
            pltpu.VMEM((tr + A_HALO, D_CONV), F32), pltpu.VMEM((7, tr + A_HALO - 8, D_CONV), F32),
            pltpu.VMEM((tr + A_HALO, D_CONV), F32), pltpu.VMEM((7, tr + A_HALO - 8, D_CONV), F32),
            pltpu.VMEM((tr + P_HALO, D_POOL), F32), pltpu.SemaphoreType.DMA((2,)),
        ] + _scatter_sems(n_parts),
        compiler_params=pltpu.CompilerParams(dimension_semantics=("arbitrary",), vmem_limit_bytes=VMEM_LIMIT),
    )(dx1, x, proj, proj, cpre, d, g1, win, wa, lg, lb, pw, ps, wout, *parts)


def _weight_grad(a, b, layout, k_rows, comm=None):
    seq, m_dim = a.shape
    n_dim = b.shape[1]
    steps = seq // k_rows

    def store(o_ref, acc, index, value):
        if steps == 1:
            o_ref[index] = value.astype(BF16)
            return
        s = pl.program_id(1)

        @pl.when(s == 0)
        def _():
            acc[index] = value

        @pl.when(jnp.logical_and(s > 0, s < steps - 1))
        def _():
            acc[index] += value

        @pl.when(s == steps - 1)
        def _():
            o_ref[index] = (acc[index] + value).astype(BF16)

    if layout in ("rows1", "rows2"):
        groups = int(layout[-1])
        per_tile = N_CHIPS // groups
        rows = m_dim // N_CHIPS // 2
        a_w = m_dim // groups

        def body(a_ref, b_ref, o_ref, acc):
            r = _dot_tn(a_ref[...], b_ref[...])
            for p in range(per_tile):
                for h in range(2):
                    store(o_ref, acc, (p, h), r[(2 * p + h) * rows:(2 * p + h + 1) * rows, :])

        in_specs = [pl.BlockSpec((k_rows, a_w), lambda g, s: (s, g)), pl.BlockSpec((k_rows, n_dim), lambda g, s: (s, 0))]
        out_spec = pl.BlockSpec((per_tile, 2, rows, n_dim), lambda g, s: (g, 0, 0, 0))
        out_dims, acc_dims = (N_CHIPS, 2, rows, n_dim), (per_tile, 2, rows, n_dim)
    elif layout == "cols_chip":
        groups = N_CHIPS
        rows, cols = m_dim // 2, n_dim // N_CHIPS

        def body(a_ref, b_ref, o_ref, acc):
            r = _dot_tn(a_ref[...], b_ref[...])
            for h in range(2):
                store(o_ref, acc, h, r[h * rows:(h + 1) * rows, :])

        in_specs = [pl.BlockSpec((k_rows, m_dim), lambda g, s: (s, 0)), pl.BlockSpec((k_rows, cols), lambda g, s: (s, g))]
        out_spec = pl.BlockSpec((None, 2, rows, cols), lambda g, s: (g, 0, 0, 0))
        out_dims, acc_dims = (N_CHIPS, 2, rows, cols), (2, rows, cols)
    else:
        groups = 2
        rows, cols = m_dim // 2, n_dim // N_CHIPS

        def body(a_ref, b_ref, o_ref, acc):
            r = _dot_tn(a_ref[...], b_ref[...])
            for k in range(N_CHIPS):
                store(o_ref, acc, k, r[:, k * cols:(k + 1) * cols])

        in_specs = [pl.BlockSpec((k_rows, rows), lambda g, s: (s, g)), pl.BlockSpec((k_rows, n_dim), lambda g, s: (s, 0))]
        out_spec = pl.BlockSpec((N_CHIPS, None, rows, cols), lambda g, s: (0, g, 0, 0))
        out_dims, acc_dims = (N_CHIPS, 2, rows, cols), (N_CHIPS, rows, cols)

    c_ins, c_shapes, c_sems, c_ops = _comm_plan(comm)
    nc = len(c_ins)

    def hosted(a_ref, b_ref, *rest):
        c_in, o_ref, c_out, acc, sems = rest[:nc], rest[nc], rest[nc + 1:2 * nc + 1], rest[2 * nc + 1], rest[2 * nc + 2:]
        g, s = pl.program_id(0), pl.program_id(1)
        if nc:
            @pl.when(jnp.logical_and(g == 0, s == 0))
            def _():
                c_ops(c_in, c_out, sems)[0]()

        body(a_ref, b_ref, o_ref, acc)
        if nc:
            @pl.when(jnp.logical_and(g == groups - 1, s == steps - 1))
            def _():
                c_ops(c_in, c_out, sems)[1]()

    outs = pl.pallas_call(
        hosted, name=f"weight_grad_{layout}_{m_dim}x{n_dim}", grid=(groups, steps),
        in_specs=in_specs + [ANY] * nc, out_specs=[out_spec] + [ANY] * nc,
        out_shape=[jax.ShapeDtypeStruct(out_dims, BF16)] + c_shapes,
        scratch_shapes=[pltpu.VMEM(acc_dims, F32)] + c_sems,
        compiler_params=pltpu.CompilerParams(dimension_semantics=("arbitrary", "arbitrary"), vmem_limit_bytes=VMEM_LIMIT),
    )(a, b, *c_ins)
    return outs if nc else outs[0]


def _exchange_ops(ins, outs, n_big, sems):
    send, recv = sems
    x, y, c, _, _ = _place()
    cps = [pltpu.make_async_remote_copy(
        src_ref=ins[t].at[:, 1 - c] if t < n_big else ins[t], dst_ref=outs[t], send_sem=send.at[t], recv_sem=recv.at[t],
        device_id=(x, y, 1 - c), device_id_type=MESH) for t in range(len(ins))]

    def start():
        for cp in cps:
            cp.start()

    def finish():
        for cp in cps:
            cp.wait()

    return start, finish


def _exchange_shapes(bigs, smalls):
    return [jax.ShapeDtypeStruct((N_CHIPS,) + b.shape[2:], b.dtype) for b in bigs] + [
        jax.ShapeDtypeStruct(s.shape, s.dtype) for s in smalls]


def _comm_plan(comm):
    if comm is None:
        return (), [], [], None
    kind, arrays = comm
    n = len(arrays)
    if kind == "scatter":
        return tuple(arrays), _scatter_shapes(arrays, ()), _scatter_sems(n), lambda i, o, sm: _scatter_ops(i, o, n, sm)
    return (tuple(arrays), _exchange_shapes(arrays, ()), [pltpu.SemaphoreType.DMA((n,))] * 2,
            lambda i, o, sm: _exchange_ops(i, o, n, sm))


def _sibling_exchange(bigs, smalls, tag):
    nb, nt = len(bigs), len(bigs) + len(smalls)

    def body(*refs):
        start, finish = _exchange_ops(refs[:nt], refs[nt:2 * nt], nb, refs[2 * nt:])
        start()
        finish()

    return pl.pallas_call(
        body, name=f"sibling_exchange_{tag}", out_shape=_exchange_shapes(bigs, smalls),
        in_specs=[ANY] * nt, out_specs=[ANY] * nt,
        scratch_shapes=[pltpu.SemaphoreType.DMA((nt,)), pltpu.SemaphoreType.DMA((nt,))],
    )(*bigs, *smalls)


def _pair_sum(core, mine, theirs, tag, block_rows):
    _, _, rows, cols = mine.shape
    steps = rows // block_rows

    def body(core_ref, a_ref, b_ref, o_ref):
        o_ref[...] = (a_ref[...].astype(F32) + b_ref[...].astype(F32)).astype(BF16)

    grid_spec = pltpu.PrefetchScalarGridSpec(
        num_scalar_prefetch=1, grid=(N_CHIPS, steps),
        in_specs=[pl.BlockSpec((None, None, block_rows, cols), lambda k, r, core_ref: (k, core_ref[0], r, 0)),
                  pl.BlockSpec((None, block_rows, cols), lambda k, r, core_ref: (k, r, 0))],
        out_specs=pl.BlockSpec((None, block_rows, cols), lambda k, r, core_ref: (k, r, 0)),
    )
    return pl.pallas_call(
        body, name=f"pair_sum_{tag}", grid_spec=grid_spec,
        out_shape=jax.ShapeDtypeStruct((N_CHIPS, rows, cols), BF16),
        compiler_params=pltpu.CompilerParams(dimension_semantics=("arbitrary", "arbitrary"), vmem_limit_bytes=VMEM_LIMIT),
    )(core, mine, theirs)


def _pair_sum_small(mine, theirs):
    (m_f2, m_b1, m_b2, m_sf, m_s5, m_sp) = mine

    def body(a0, a1, a2, a3, a4, a5, b0, b1, b2, b3, b4, b5, o_m, o_f, o_5, o_p):
        sm = (a0[...] + a1[...] + a2[...]) + (b0[...] + b1[...] + b2[...])
        sf = a3[...] + b3[...]
        s5 = a4[...] + b4[...]
        for h in range(2):
            o_m[h] = sm[:, h * (D_MODEL // 2):(h + 1) * (D_MODEL // 2)]
            o_f[h] = sf[:, h * (D_FF // 2):(h + 1) * (D_FF // 2)]
            o_5[h] = s5[:, h * (D_CONV // 2):(h + 1) * (D_CONV // 2)]
            for g in range(2):
                o_p[h, g] = a5[2 * h + g] + b5[2 * h + g]

    out_shape = [
        jax.ShapeDtypeStruct((2, 8, D_MODEL // 2), F32), jax.ShapeDtypeStruct((2, 8, D_FF // 2), F32),
        jax.ShapeDtypeStruct((2, 40, D_CONV // 2), F32), jax.ShapeDtypeStruct((2, 2, POOL_GROUP, POOL_GROUP), F32),
    ]
    return pl.pallas_call(body, name="pair_sum_small", out_shape=out_shape, in_specs=[VMEM] * 12, out_specs=[VMEM] * 4)(
        *mine, *theirs)


def _scatter_ops(ins, outs, n_parts, sems, landed=False):
    ici_send, ici_recv, fwd_send, fwd_recv, loc_sem = sems
    nt = len(ins)
    x, y, c, k, chips = _place()

    def src_of(t, kk):
        return ins[t].at[kk] if t < n_parts else ins[t].at[c]

    def ici(t, j, kk, slot):
        return pltpu.make_async_remote_copy(
            src_ref=src_of(t, kk), dst_ref=outs[t].at[slot, c], send_sem=ici_send.at[t * 3 + j],
            recv_sem=ici_recv.at[t * 3 + j], device_id=(*chips[j], c), device_id_type=MESH)

    def fwd(t, j, q, half, src=None):
        slot = outs[t].at[q, half]
        return pltpu.make_async_remote_copy(
            src_ref=slot if src is None else src, dst_ref=slot, send_sem=fwd_send.at[t * 4 + j],
            recv_sem=fwd_recv.at[t * 4 + j], device_id=(x, y, 1 - c), device_id_type=MESH)

    local = [pltpu.make_async_copy(src_of(t, k), outs[t].at[k, c], loc_sem.at[t]) for t in range(nt)]
    peers = [(t, j, 2 * qx + qy) for t in range(nt) for j, (qx, qy) in enumerate(chips)]
    sends = [fwd(t, 3, k, c, src=src_of(t, k)) for t in range(nt)]
    if not landed:
        sends += [ici(t, j, kq, k) for t, j, kq in peers]

    def start():
        for cp in local + sends:
            cp.start()

    def finish():
        passed = []
        for t, j, kq in peers:
            if not landed:
                ici(t, j, kq, kq).wait_recv()
            cp = fwd(t, j, kq, c)
            cp.start()
            passed.append(cp)
        for t in range(nt):
            fwd(t, 3, k, 1 - c).wait_recv()
        for t, j, kq in peers:
            fwd(t, j, kq, 1 - c).wait_recv()
        for cp in sends + passed:
            cp.wait_send()
        for cp in local:
            cp.wait()

    return start, finish


def _scatter_sems(nt):
    return [pltpu.SemaphoreType.DMA((3 * nt,))] * 2 + [pltpu.SemaphoreType.DMA((4 * nt,))] * 2 + [pltpu.SemaphoreType.DMA((nt,))]


def _scatter_shapes(parts, smalls):
    return [jax.ShapeDtypeStruct((N_CHIPS, 2) + p.shape[1:], p.dtype) for p in tuple(parts) + tuple(smalls)]


HBM_SPEC = pl.BlockSpec(memory_space=pltpu.HBM)
SEM_SPEC = pl.BlockSpec(memory_space=pltpu.SEMAPHORE)
EFFECT = pltpu.SideEffectType.DATAFLOW_SIDE_EFFECTING


def _ici_copy(ins, lands, n_parts, send, recv, t, j):
    _, _, c, k, chips = _place()
    qx, qy = chips[j]
    src = ins[t].at[2 * qx + qy] if t < n_parts else ins[t].at[c]
    return pltpu.make_async_remote_copy(
        src_ref=src, dst_ref=lands[t].at[k, c], send_sem=send.at[t * 3 + j], recv_sem=recv.at[t * 3 + j],
        device_id=(qx, qy, c), device_id_type=MESH)


def _scatter_start(parts, smalls):
    arrays = tuple(parts) + tuple(smalls)
    nt = len(arrays)

    def body(*refs):
        ins, lands = refs[:nt], refs[nt:2 * nt]
        send, recv = refs[2 * nt], refs[2 * nt + 1]
        token = refs[-1]
        for t in range(nt):
            for j in range(3):
                _ici_copy(ins, lands, len(parts), send, recv, t, j).start()
        token[...] = jnp.zeros(token.shape, F32)

    land_shapes = _scatter_shapes(parts, smalls)
    out_shape = ([pltpu.SemaphoreType.DMA((3 * nt,))] * 2 + [pltpu.HBM(a.shape, a.dtype) for a in arrays]
                 + [pltpu.HBM(a.shape, a.dtype) for a in land_shapes] + [jax.ShapeDtypeStruct((8, 128), F32)])
    operands = [pltpu.with_memory_space_constraint(a, pltpu.HBM) for a in arrays]
    operands += [pltpu.with_memory_space_constraint(lax.empty(a.shape, a.dtype), pltpu.HBM) for a in land_shapes]
    outs = pl.pallas_call(
        body, name="scatter_start", out_shape=out_shape, in_specs=[HBM_SPEC] * (2 * nt),
        out_specs=[SEM_SPEC] * 2 + [HBM_SPEC] * (2 * nt) + [VMEM],
        input_output_aliases={i: 2 + i for i in range(2 * nt)},
        compiler_params=pltpu.CompilerParams(has_side_effects=EFFECT),
    )(*operands)
    return outs[0], outs[1], outs[2:2 + nt], outs[2 + nt:2 + 2 * nt], outs[-1]


def _scatter_wait(send, recv, ins, lands, n_parts, after):
    nt = len(ins)

    def body(*refs):
        in_refs, land_refs = refs[:nt], refs[nt:2 * nt]
        send_ref, recv_ref = refs[2 * nt], refs[2 * nt + 1]
        for t in range(nt):
            for j in range(3):
                cp = _ici_copy(in_refs, land_refs, n_parts, send_ref, recv_ref, t, j)
                cp.wait_send()
                cp.wait_recv()

    outs = pl.pallas_call(
        body, name="scatter_wait", out_shape=[pltpu.HBM(a.shape, a.dtype) for a in tuple(ins) + tuple(lands)],
        in_specs=[HBM_SPEC] * (2 * nt) + [SEM_SPEC] * 2 + [ANY], out_specs=[HBM_SPEC] * (2 * nt),
        input_output_aliases={i: i for i in range(2 * nt)},
        compiler_params=pltpu.CompilerParams(has_side_effects=EFFECT),
    )(*ins, *lands, send, recv, after)
    return outs[:nt], outs[nt:]


def _scatter_forward(ins, lands, n_parts):
    nt = len(ins)

    def body(*refs):
        start, finish = _scatter_ops(refs[:nt], refs[2 * nt:3 * nt], n_parts, refs[3 * nt:], landed=True)
        start()
        finish()

    return pl.pallas_call(
        body, name="scatter_forward", out_shape=[jax.ShapeDtypeStruct(a.shape, a.dtype) for a in lands],
        in_specs=[ANY] * (2 * nt), out_specs=[ANY] * nt, input_output_aliases={nt + i: i for i in range(nt)},
        scratch_shapes=_scatter_sems(nt),
    )(*ins, *lands)


def _chip_scatter(parts, smalls):
    nt = len(parts) + len(smalls)

    def body(*refs):
        start, finish = _scatter_ops(refs[:nt], refs[nt:2 * nt], len(parts), refs[2 * nt:])
        start()
        finish()

    return pl.pallas_call(
        body, name="chip_scatter", out_shape=_scatter_shapes(parts, smalls), in_specs=[ANY] * nt, out_specs=[ANY] * nt,
        scratch_shapes=_scatter_sems(nt),
    )(*parts, *smalls)


def _adamw(w, g, m, v):
    m = ADAM_B1 * m + (1.0 - ADAM_B1) * g
    v = ADAM_B2 * v + (1.0 - ADAM_B2) * (g * g)
    m_hat = m / (1.0 - ADAM_B1 ** ADAM_STEP)
    v_hat = v / (1.0 - ADAM_B2 ** ADAM_STEP)
    delta = -ADAM_LR * (m_hat / (jnp.sqrt(v_hat) + ADAM_EPS) + ADAM_WD * w)
    return delta, m, v


def _adam_big(parts, w, m, v, tag, block_rows, token):
    _, _, rows, cols = parts.shape
    steps = rows // block_rows

    def body(p_ref, w_ref, m_ref, v_ref, token_ref, g_out, d_out, m_out, v_out):
        g = p_ref[0].astype(F32)
        for q in range(1, N_CHIPS):
            g = g + p_ref[q].astype(F32)
        delta, m_new, v_new = _adamw(w_ref[...], g, m_ref[...], v_ref[...])
        g_out[...] = g
        d_out[...] = delta
        m_out[...] = m_new
        v_out[...] = v_new

    blk = pl.BlockSpec((block_rows, cols), lambda h, r: (h * steps + r, 0))
    return pl.pallas_call(
        body, name=f"adam_{tag}", grid=(2, steps),
        in_specs=[pl.BlockSpec((N_CHIPS, None, block_rows, cols), lambda h, r: (0, h, r, 0)), blk, blk, blk, ANY],
        out_specs=[blk] * 4, out_shape=[jax.ShapeDtypeStruct(w.shape, F32)] * 4,
        compiler_params=pltpu.CompilerParams(dimension_semantics=("arbitrary", "arbitrary"), vmem_limit_bytes=VMEM_LIMIT),
    )(parts, w, m, v, token)


def _reduce_small(l_m, l_f, l_5, l_p):
    def total(ref):
        t = ref[0]
        for q in range(1, N_CHIPS):
            t = t + ref[q]
        return t

    def body(m_ref, f_ref, s_ref, p_ref, g1_o, g2_o, g3_o, loss_o, wf_o, fb_o, wa_o, cb_o, lg_o, lb_o, ps_o, pw_o):
        tm, tf, t5, tp = total(m_ref), total(f_ref), total(s_ref), total(p_ref)
        sm = jnp.concatenate([tm[0], tm[1]], axis=1)
        sf = jnp.concatenate([tf[0], tf[1]], axis=1)
        s5 = jnp.concatenate([t5[0], t5[1]], axis=1)
        g1_o[...] = sm[0:1]
        g2_o[...] = sm[1:2]
        g3_o[...] = sm[2:3]
        loss_o[...] = sm[3:4, 0:128]
        wf_o[...] = sf
        fb_o[...] = sf[3:4]
        wa_o[...] = s5[0:32]
        cb_o[...] = s5[32:33]
        lg_o[...] = s5[33:34]
        lb_o[...] = s5[34:35]
        ps_o[...] = s5[35:36]
        for h in range(2):
            for g in range(2):
                pw_o[2 * h + g] = tp[h, g]

    row = lambda w: jax.ShapeDtypeStruct((1, w), F32)
    out_shape = [row(D_MODEL), row(D_MODEL), row(D_MODEL), row(128), jax.ShapeDtypeStruct((8, D_FF), F32), row(D_FF),
                 jax.ShapeDtypeStruct((32, D_CONV), F32), row(D_CONV), row(D_CONV), row(D_CONV), row(D_POOL),
                 jax.ShapeDtypeStruct((4, POOL_GROUP, POOL_GROUP), F32)]
    return pl.pallas_call(body, name="reduce_small", out_shape=out_shape, in_specs=[VMEM] * 4, out_specs=[VMEM] * 12)(
        l_m, l_f, l_5, l_p)


def _adam_small(ws, gs, ms, vs):
    count = len(ws)

    def body(*refs):
        w_r, g_r, m_r, v_r = (refs[t * count:(t + 1) * count] for t in range(4))
        d_o, m_o, v_o = (refs[(4 + t) * count:(5 + t) * count] for t in range(3))
        for t in range(count):
            delta, m_new, v_new = _adamw(w_r[t][...], g_r[t][...], m_r[t][...], v_r[t][...])
            d_o[t][...] = delta
            m_o[t][...] = m_new
            v_o[t][...] = v_new

    out_shape = [jax.ShapeDtypeStruct(w.shape, F32) for w in ws] * 3
    outs = pl.pallas_call(body, name="adam_small", out_shape=out_shape, in_specs=[VMEM] * (4 * count),
                          out_specs=[VMEM] * (3 * count))(*ws, *gs, *ms, *vs)
    return outs[:count], outs[count:2 * count], outs[2 * count:]


MIX_TILE = 512
FFN_TILE = 256
GRAD_K = 2048


def kernel(x, norm_mix_g, w_in, conv_a_w, conv_a_b, ln_a_g, ln_a_b, pool_w, pool_scale, w_out, norm_ffn_g, w_up, conv_f_w, conv_f_b, w_down, norm_final_g, loss_target, m_norm_mix_g, m_w_in, m_conv_a_w, m_conv_a_b, m_ln_a_g, m_ln_a_b, m_pool_w, m_pool_scale, m_w_out, m_norm_ffn_g, m_w_up, m_conv_f_w, m_conv_f_b, m_w_down, m_norm_final_g, v_norm_mix_g, v_w_in, v_conv_a_w, v_conv_a_b, v_ln_a_g, v_ln_a_b, v_pool_w, v_pool_scale, v_w_out, v_norm_ffn_g, v_w_up, v_conv_f_w, v_conv_f_b, v_w_down, v_norm_final_g):
    seq = x.shape[1]
    xs, ts = x[0], loss_target[0]
    mix_tile, ffn_tile, grad_k = min(MIX_TILE, seq), min(FFN_TILE, seq), min(GRAD_K, seq)
    chip = 2 * lax.axis_index("x") + lax.axis_index("y")
    core = lax.axis_index("c").astype(jnp.int32).reshape(1)

    wa_s = jnp.pad(conv_a_w[0], ((0, 32 - CONV_A), (0, 0)))
    wf_s = jnp.pad(conv_f_w[0], ((0, 8 - CONV_F), (0, 0)))
    win_b, wout_b, wup_b, wdown_b = _cast_shards(w_in[0], w_out[0], w_up[0], w_down[0])
    g3 = norm_final_g.reshape(1, D_MODEL)
    pw = pool_w[0]

    h1, proj, cpre, dpool, mcat, x1, win, wout, wup, wa_g, wf_g = _mixer_fwd(
        xs, norm_mix_g, win_b, wout_b, wup_b, wa_s, wf_s, conv_a_b, ln_a_g, ln_a_b, pw, pool_scale, mix_tile)
    wa = jnp.transpose(wa_g, (1, 0, 2)).reshape(32, D_CONV)
    wf = jnp.transpose(wf_g, (1, 0, 2)).reshape(8, D_FF)
    h2, up, gcs, act, wdown = _ffn_up(x1, norm_ffn_g, wup, wf, conv_f_b, wdown_b, ffn_tile)
    dx2, dx2b, sm_f2 = _ffn_down(x1, act, wdown, g3, ts, mix_tile)
    tags = ("w_in", "w_out", "w_up", "w_down")
    blocks = (256, 128, 256, 176)
    g_wdown = _weight_grad(act, dx2b, "rows2", grad_k)
    dup, dx1, dx1b, sm_b1, sf, l_wdown = _ffn_bwd(
        dx2, up, gcs, x1, norm_ffn_g, wup, wf, wdown, ("exchange", [g_wdown]), ffn_tile)
    p_wdown = _pair_sum(core, g_wdown, l_wdown, tags[3], blocks[3])
    g_wup, s_wdown = _weight_grad(h2, dup, "cols_chip", grad_k, ("scatter", [p_wdown]))
    g_wout, l_wup = _weight_grad(mcat, dx1b, "rows1", grad_k, ("exchange", [g_wup]))
    p_wup = _pair_sum(core, g_wup, l_wup, tags[2], blocks[2])
    l_wout, = _sibling_exchange((g_wout,), (), "early")
    p_wout = _pair_sum(core, g_wout, l_wout, tags[1], blocks[1])
    dproj, grad_x, sm_b2, s5, sp, s_wout, s_wup = _mixer_bwd(
        dx1, xs, proj, cpre, dpool, norm_mix_g, win, wa, ln_a_g, ln_a_b, pw, pool_scale, wout, [p_wout, p_wup], mix_tile)
    g_win = _weight_grad(h1, dproj, "cols_half", grad_k)

    smalls = (sm_f2, sm_b1, sm_b2, sf, s5, sp)
    landed = _sibling_exchange((g_win,), smalls, "late")
    part_win = _pair_sum(core, g_win, landed[0], tags[0], blocks[0])
    small_parts = _pair_sum_small(smalls, landed[1:])
    send, recv, late_src, late_land, token = _scatter_start([part_win], small_parts)
    big_w = (w_in[0], w_out[0], w_up[0], w_down[0])
    big_m = (m_w_in[0], m_w_out[0], m_w_up[0], m_w_down[0])
    big_v = (v_w_in[0], v_w_out[0], v_w_up[0], v_w_down[0])
    big = {}
    for t, p in ((1, s_wout), (2, s_wup), (3, s_wdown)):
        big[tags[t]] = _adam_big(p, big_w[t], big_m[t], big_v[t], tags[t], blocks[t], token)
    late_src, late_land = _scatter_wait(send, recv, late_src, late_land, 1, big[tags[3]][3])
    late = _scatter_forward(late_src, late_land, 1)
    big[tags[0]] = _adam_big(late[0], big_w[0], big_m[0], big_v[0], tags[0], blocks[0], token)
    big = {tag: [a[None] for a in outs] for tag, outs in big.items()}
    scattered = [None] * 4 + list(late[1:])

    (g_g1, g_g2, g_g3, loss_row, g_wf_all, g_fb, g_wa_all, g_cb, g_lg, g_lb, g_ps, g_pw) = _reduce_small(*scattered[4:])
    g_wa = lax.dynamic_slice(g_wa_all, (0, chip * (D_CONV // N_CHIPS)), (32, D_CONV // N_CHIPS))[:CONV_A]
    g_wf = lax.dynamic_slice(g_wf_all, (0, chip * (D_FF // N_CHIPS)), (8, D_FF // N_CHIPS))[:CONV_F]
    small_names = ("norm_mix_g", "conv_a_w", "conv_a_b", "ln_a_g", "ln_a_b", "pool_w", "pool_scale", "norm_ffn_g",
                   "conv_f_w", "conv_f_b", "norm_final_g")
    small_w = (norm_mix_g, conv_a_w[0], conv_a_b, ln_a_g, ln_a_b, pw, pool_scale, norm_ffn_g, conv_f_w[0], conv_f_b, g3)
    small_m = (m_norm_mix_g, m_conv_a_w[0], m_conv_a_b, m_ln_a_g, m_ln_a_b, m_pool_w[0], m_pool_scale, m_norm_ffn_g,
               m_conv_f_w[0], m_conv_f_b, m_norm_final_g.reshape(1, D_MODEL))
    small_v = (v_norm_mix_g, v_conv_a_w[0], v_conv_a_b, v_ln_a_g, v_ln_a_b, v_pool_w[0], v_pool_scale, v_norm_ffn_g,
               v_conv_f_w[0], v_conv_f_b, v_norm_final_g.reshape(1, D_MODEL))
    small_g = (g_g1, g_wa, g_cb, g_lg, g_lb, g_pw, g_ps, g_g2, g_wf, g_fb, g_g3)
    s_delta, s_m, s_v = _adam_small(small_w, small_g, small_m, small_v)
    shapes = {"conv_a_w": conv_a_w.shape, "pool_w": pool_w.shape, "conv_f_w": conv_f_w.shape, "norm_final_g": norm_final_g.shape}
    small = {}
    for t, name in enumerate(small_names):
        shp = shapes.get(name)
        small[name] = [a if shp is None else a.reshape(shp) for a in (small_g[t], s_delta[t], s_m[t], s_v[t])]

    order = ("norm_mix_g", "w_in", "conv_a_w", "conv_a_b", "ln_a_g", "ln_a_b", "pool_w", "pool_scale", "w_out", "norm_ffn_g",
             "w_up", "conv_f_w", "conv_f_b", "w_down", "norm_final_g")
    table = {**big, **small}
    loss = loss_row[0, 0]
    outs = [loss, grad_x[None]]
    for t in range(4):
        outs += [table[name][t] for name in order]
    return tuple(outs)
```

```python
import functools

import jax
import jax.numpy as jnp
from jax import lax
from jax.experimental import pallas as pl
from jax.experimental.pallas import tpu as pltpu

F32 = jnp.float32
BF16 = jnp.bfloat16
EPS = 1e-6
ADAM_LR = 0.001
ADAM_B1 = 0.9
ADAM_B2 = 0.999
ADAM_EPS = 1e-08
ADAM_WD = 0.01
ADAM_STEP = 10

D_MODEL = 1024
D_CONV = 512
D_POOL = 512
D_IN = 1536
D_FF = 2816
CONV_A = 31
CONV_F = 3
POOL_WINDOWS = (2, 4, 8, 16)
POOL_GROUP = 128
N_CHIPS = 4
FF_CHUNK = 256
N_FF_CHUNKS = D_FF // FF_CHUNK
A_HALO = 32
P_HALO = 16
VMEM_LIMIT = 56 * 1024 * 1024
MESH = pl.DeviceIdType.MESH

ANY = pl.BlockSpec(memory_space=pl.ANY)
VMEM = pl.BlockSpec(memory_space=pltpu.VMEM)


def _dot(a, b):
    return jnp.dot(a, b, preferred_element_type=F32)


def _dot_nt(a, b):
    return lax.dot_general(a, b, (((1,), (1,)), ((), ())), preferred_element_type=F32)


def _dot_tn(a, b):
    return lax.dot_general(a, b, (((0,), (0,)), ((), ())), preferred_element_type=F32)


def _sigmoid(v):
    return jax.nn.sigmoid(v)


def _colsum(v):
    return jnp.sum(v, axis=0, keepdims=True)


def _rowmean(v):
    return jnp.mean(v, axis=-1, keepdims=True)


def _place():
    x, y, c = lax.axis_index("x"), lax.axis_index("y"), lax.axis_index("c")
    chips = [(1 - x, y), (x, 1 - y), (1 - x, 1 - y)]
    return x, y, c, 2 * x + y, chips


def _gather_ops(bufs, fulls, col_sharded, sems):
    ici_send, ici_recv, fwd_send, fwd_recv, loc_sem = sems
    n_big = len(bufs)
    x, y, c, k, chips = _place()

    def block(i, kk, half=None):
        rows, cols = bufs[i].shape
        if col_sharded[i]:
            rs = slice(None) if half is None else pl.ds(pl.multiple_of(half * (rows // 2), 16), rows // 2)
            return fulls[i].at[rs, pl.ds(pl.multiple_of(kk * cols, 128), cols)]
        if half is None:
            return fulls[i].at[pl.ds(pl.multiple_of(kk * rows, 16), rows), :]
        return fulls[i].at[pl.ds(pl.multiple_of(kk * rows + half * (rows // 2), 16), rows // 2), :]

    def my_half(i):
        rows = bufs[i].shape[0]
        return bufs[i].at[pl.ds(pl.multiple_of(c * (rows // 2), 16), rows // 2), :]

    def ici(i, j, kk):
        return pltpu.make_async_remote_copy(
            src_ref=my_half(i), dst_ref=block(i, kk, c), send_sem=ici_send.at[i * 3 + j], recv_sem=ici_recv.at[i * 3 + j],
            device_id=(*chips[j], c), device_id_type=MESH)

    def fwd(i, j, kk, half):
        return pltpu.make_async_remote_copy(
            src_ref=block(i, kk, half), dst_ref=block(i, kk, half), send_sem=fwd_send.at[i * 3 + j],
            recv_sem=fwd_recv.at[i * 3 + j], device_id=(x, y, 1 - c), device_id_type=MESH)

    local = [pltpu.make_async_copy(bufs[i], block(i, k), loc_sem.at[i]) for i in range(n_big)]
    sends = [ici(i, j, k) for i in range(n_big) for j in range(3)]
    peers = [(i, j, 2 * qx + qy) for i in range(n_big) for j, (qx, qy) in enumerate(chips)]

    def start():
        for cp in local + sends:
            cp.start()

    def finish():
        passed = []
        for i, j, kq in peers:
            ici(i, j, kq).wait_recv()
            cp = fwd(i, j, kq, c)
            cp.start()
            passed.append(cp)
        for i, j, kq in peers:
            fwd(i, j, kq, 1 - c).wait_recv()
        for cp in sends + passed:
            cp.wait_send()
        for cp in local:
            cp.wait()

    return start, finish


def _gather_sems(n_big):
    return [pltpu.SemaphoreType.DMA((3 * n_big,))] * 4 + [pltpu.SemaphoreType.DMA((n_big,))]


def _tap_ops(srcs, dsts, sems):
    send, recv, loc = sems
    _, _, c, k, chips = _place()

    def copy(t, j, kk):
        return pltpu.make_async_remote_copy(
            src_ref=srcs[t], dst_ref=dsts[t].at[kk], send_sem=send.at[t * 3 + j], recv_sem=recv.at[t * 3 + j],
            device_id=(*chips[j], c), device_id_type=MESH)

    local = [pltpu.make_async_copy(srcs[t], dsts[t].at[k], loc.at[t]) for t in range(len(srcs))]
    sends = [[copy(t, j, k) for j in range(3)] for t in range(len(srcs))]

    def start():
        for t, cp in enumerate(local):
            cp.start()
            for sd in sends[t]:
                sd.start()

    def wait(t):
        for j, (qx, qy) in enumerate(chips):
            copy(t, j, 2 * qx + qy).wait_recv()
        for sd in sends[t]:
            sd.wait_send()
        local[t].wait()

    return start, wait


def _cast_shards(*shards):
    def body(*refs):
        for src, dst in zip(refs[:len(shards)], refs[len(shards):]):
            dst[...] = src[...].astype(BF16)

    return pl.pallas_call(
        body, name="cast_shards", out_shape=[jax.ShapeDtypeStruct(s.shape, BF16) for s in shards],
        in_specs=[VMEM] * len(shards), out_specs=[VMEM] * len(shards),
        compiler_params=pltpu.CompilerParams(vmem_limit_bytes=VMEM_LIMIT),
    )(*shards)


def _load_weights(pairs, sem):
    cps = [pltpu.make_async_copy(src, dst, sem.at[i]) for i, (src, dst) in enumerate(pairs)]
    for cp in cps:
        cp.start()
    for cp in cps:
        cp.wait()


def _shifted_views(buf, shifted, t_rows):
    n = t_rows + A_HALO - 8
    for b in range(1, 8):
        shifted[b - 1] = buf[b:b + n, :]

    def view(offset):
        a, b = divmod(offset, 8)
        if b == 0:
            return buf[8 * a:8 * a + t_rows, :]
        return shifted[b - 1, 8 * a:8 * a + t_rows, :]

    return view


def _pool_count(tile, t_rows, w):
    row = lax.broadcasted_iota(jnp.int32, (t_rows, POOL_GROUP), 0) + tile * t_rows
    return jnp.minimum(row + 1, w).astype(F32)


def _mixer_fwd(x, g1, win_b, wout_b, wup_b, wa_s, wf_s, cb, lg, lb, pw, ps, tile_rows):
    seq = x.shape[0]
    tr = tile_rows
    n = seq // tr

    def body(x_ref, g1_ref, win_b_hbm, wout_b_hbm, wup_b_hbm, wa_s_hbm, wf_s_hbm, cb_ref, lg_ref, lb_ref, pw_ref,
             ps_ref, h1_ref, proj_ref, c_ref, d_ref, m_ref, x1_ref, win_f, wout_f, wup_f, wa_g, wf_g,
             win_v, wout_v, wa_ref, ubuf, ushift, bbuf, sem, *csems):
        i = pl.program_id(0)
        first_sems, later_sems, tap_sems = csems[0:5], csems[5:10], csems[10:13]

        def first():
            return _gather_ops((win_b_hbm, wout_b_hbm), (win_f, wout_f), (True, False), first_sems)

        def later():
            return _gather_ops((wup_b_hbm,), (wup_f,), (True,), later_sems)

        def taps():
            return _tap_ops((wa_s_hbm, wf_s_hbm), (wa_g, wf_g), tap_sems)

        @pl.when(i == 0)
        def _():
            first()[0]()
            taps()[0]()
            later()[0]()
            first()[1]()
            taps()[1](0)
            loads = [(win_f, win_v), (wout_f, wout_v)]
            loads += [(wa_g.at[kk], wa_ref.at[:, kk * (D_CONV // N_CHIPS):(kk + 1) * (D_CONV // N_CHIPS)]) for kk in range(N_CHIPS)]
            _load_weights(loads, sem)
            ubuf[0:A_HALO, :] = jnp.zeros((A_HALO, D_CONV), F32)
            bbuf[0:P_HALO, :] = jnp.zeros((P_HALO, D_POOL), F32)

        xv = x_ref[...]
        r = lax.rsqrt(_rowmean(xv * xv) + EPS)
        h1 = (xv * r * g1_ref[...]).astype(BF16)
        h1_ref[...] = h1
        proj = _dot(h1, win_v[...])
        proj_ref[...] = proj.astype(BF16)
        av, ag, bi = proj[:, :D_CONV], proj[:, D_CONV:2 * D_CONV], proj[:, 2 * D_CONV:]
        ubuf[A_HALO:A_HALO + tr, :] = av * _sigmoid(ag)
        off = A_HALO - (CONV_A - 1)
        uview = _shifted_views(ubuf, ushift, tr)
        acc = wa_ref[0:1, :] * uview(off)
        for j in range(1, CONV_A):
            acc = acc + wa_ref[j:j + 1, :] * uview(off + j)
        cv = acc + cb_ref[...]
        ubuf[0:A_HALO, :] = ubuf[tr:tr + A_HALO, :]
        c_ref[...] = cv.astype(BF16)
        xc = cv - _rowmean(cv)
        z = xc * lax.rsqrt(_rowmean(xc * xc) + EPS)
        ln = z * lg_ref[...] + lb_ref[...]
        ya = ln * _sigmoid(ln)
        bbuf[P_HALO:P_HALO + tr, :] = bi
        ds, ybs = [], []
        for g, w in enumerate(POOL_WINDOWS):
            cols = slice(g * POOL_GROUP, (g + 1) * POOL_GROUP)
            s = bi[:, cols]
            for kk in range(1, w):
                s = s + bbuf[P_HALO - kk:P_HALO - kk + tr, cols]
            dg = s / _pool_count(i, tr, w) - bi[:, cols]
            ds.append(dg)
            ybs.append(_dot(dg.astype(BF16), pw_ref[g].astype(BF16)))
        bbuf[0:P_HALO, :] = bbuf[tr:tr + P_HALO, :]
        d_ref[...] = jnp.concatenate(ds, axis=1).astype(BF16)
        yb = jnp.concatenate(ybs, axis=1) * ps_ref[...]
        m = jnp.concatenate([ya, yb], axis=1).astype(BF16)
        m_ref[...] = m
        x1_ref[...] = xv + _dot(m, wout_v[...])

        @pl.when(i == n - 1)
        def _():
            later()[1]()
            taps()[1](1)

    tile = lambda w: pl.BlockSpec((tr, w), lambda i: (i, 0))
    full = lambda a: pl.BlockSpec(a.shape, lambda i: (0,) * a.ndim)
    return pl.pallas_call(
        body, name="mixer_fwd", grid=(n,),
        in_specs=[tile(D_MODEL), full(g1)] + [ANY] * 5 + [full(cb), full(lg), full(lb), full(pw), full(ps)],
        out_specs=[tile(D_MODEL), tile(D_IN), tile(D_CONV), tile(D_POOL), tile(D_MODEL), tile(D_MODEL)] + [ANY] * 5,
        out_shape=[
            jax.ShapeDtypeStruct((seq, D_MODEL), BF16), jax.ShapeDtypeStruct((seq, D_IN), BF16),
            jax.ShapeDtypeStruct((seq, D_CONV), BF16), jax.ShapeDtypeStruct((seq, D_POOL), BF16),
            jax.ShapeDtypeStruct((seq, D_MODEL), BF16), jax.ShapeDtypeStruct((seq, D_MODEL), F32),
            jax.ShapeDtypeStruct((D_MODEL, D_IN), BF16), jax.ShapeDtypeStruct((D_MODEL, D_MODEL), BF16),
            jax.ShapeDtypeStruct((D_MODEL, 2 * D_FF), BF16),
            jax.ShapeDtypeStruct((N_CHIPS,) + wa_s.shape, F32), jax.ShapeDtypeStruct((N_CHIPS,) + wf_s.shape, F32),
        ],
        scratch_shapes=[
            pltpu.VMEM((D_MODEL, D_IN), BF16), pltpu.VMEM((D_MODEL, D_MODEL), BF16), pltpu.VMEM((32, D_CONV), F32),
            pltpu.VMEM((tr + A_HALO, D_CONV), F32), pltpu.VMEM((7, tr + A_HALO - 8, D_CONV), F32),
            pltpu.VMEM((tr + P_HALO, D_POOL), F32), pltpu.SemaphoreType.DMA((2 + N_CHIPS,)),
        ] + _gather_sems(2) + _gather_sems(1) + [
            pltpu.SemaphoreType.DMA((6,)), pltpu.SemaphoreType.DMA((6,)), pltpu.SemaphoreType.DMA((2,))],
        compiler_params=pltpu.CompilerParams(dimension_semantics=("arbitrary",), vmem_limit_bytes=VMEM_LIMIT),
    )(x, g1, win_b, wout_b, wup_b, wa_s, wf_s, cb, lg, lb, pw, ps)


def _ffn_up(x1, g2, wup, wf, fb, wdown_b, tile_rows):
    seq = x1.shape[0]
    tr = tile_rows
    n = seq // tr

    def body(x1_ref, g2_ref, wup_hbm, wf_ref, fb_ref, wdown_b_hbm,
             h2_ref, up_ref, gc_ref, act_ref, wdown_f, wup_v, gbuf, sem, *gsems):
        i = pl.program_id(0)

        def gather():
            return _gather_ops((wdown_b_hbm,), (wdown_f,), (False,), gsems)

        @pl.when(i == 0)
        def _():
            gather()[0]()
            _load_weights(((wup_hbm, wup_v),), sem)
            gbuf[0:8, :] = jnp.zeros((8, D_FF), F32)

        x1v = x1_ref[...]
        r2 = lax.rsqrt(_rowmean(x1v * x1v) + EPS)
        h2 = (x1v * r2 * g2_ref[...]).astype(BF16)
        h2_ref[...] = h2

        def up_proj(j):
            return (_dot(h2, wup_v[:, j * FF_CHUNK:(j + 1) * FF_CHUNK]),
                    _dot(h2, wup_v[:, D_FF + j * FF_CHUNK:D_FF + (j + 1) * FF_CHUNK]))

        ahead = up_proj(0)
        for j in range(N_FF_CHUNKS):
            cs = slice(j * FF_CHUNK, (j + 1) * FF_CHUNK)
            vs = slice(D_FF + j * FF_CHUNK, D_FF + (j + 1) * FF_CHUNK)
            gate, val = ahead
            if j + 1 < N_FF_CHUNKS:
                ahead = up_proj(j + 1)
            up_ref[:, cs] = gate.astype(BF16)
            up_ref[:, vs] = val.astype(BF16)
            gbuf[8:8 + tr, cs] = gate
            gc = (wf_ref[0:1, cs] * gbuf[6:6 + tr, cs] + wf_ref[1:2, cs] * gbuf[7:7 + tr, cs]
                  + wf_ref[2:3, cs] * gate + fb_ref[:, cs])
            gbuf[0:8, cs] = gbuf[tr:tr + 8, cs]
            gc_ref[:, cs] = gc.astype(BF16)
            act_ref[:, cs] = (gc * _sigmoid(gc) * val).astype(BF16)

        @pl.when(i == n - 1)
        def _():
            gather()[1]()

    tile = lambda w: pl.BlockSpec((tr, w), lambda i: (i, 0))
    full = lambda a: pl.BlockSpec(a.shape, lambda i: (0,) * a.ndim)
    return pl.pallas_call(
        body, name="ffn_up", grid=(n,),
        in_specs=[tile(D_MODEL), full(g2), ANY, full(wf), full(fb), ANY],
        out_specs=[tile(D_MODEL), tile(2 * D_FF), tile(D_FF), tile(D_FF), ANY],
        out_shape=[
            jax.ShapeDtypeStruct((seq, D_MODEL), BF16), jax.ShapeDtypeStruct((seq, 2 * D_FF), BF16),
            jax.ShapeDtypeStruct((seq, D_FF), BF16), jax.ShapeDtypeStruct((seq, D_FF), BF16),
            jax.ShapeDtypeStruct((D_FF, D_MODEL), BF16),
        ],
        scratch_shapes=[pltpu.VMEM(wup.shape, BF16), pltpu.VMEM((tr + 8, D_FF), F32), pltpu.SemaphoreType.DMA((1,))]
        + _gather_sems(1),
        compiler_params=pltpu.CompilerParams(dimension_semantics=("arbitrary",), vmem_limit_bytes=VMEM_LIMIT),
    )(x1, g2, wup, wf, fb, wdown_b)


def _ffn_down(x1, act, wdown, g3, target, tile_rows):
    seq = x1.shape[0]
    tr = tile_rows
    n = seq // tr

    def body(x1_ref, act_ref, wdown_hbm, g3_ref, t_ref, dx2_ref, dx2b_ref, sm_ref, wdown_v, sem):
        i = pl.program_id(0)

        @pl.when(i == 0)
        def _():
            _load_weights(((wdown_hbm, wdown_v),), sem)
            sm_ref[...] = jnp.zeros(sm_ref.shape, F32)

        x2 = x1_ref[...] + _dot(act_ref[...], wdown_v[...])
        r3 = lax.rsqrt(_rowmean(x2 * x2) + EPS)
        n3 = x2 * r3
        err = n3 * g3_ref[...] - t_ref[...]
        dy = err / D_MODEL
        sm_ref[2:3, :] += _colsum(dy * n3)
        loss = 0.5 * _colsum(_rowmean(err * err))
        sm_ref[3:4, :] += jnp.broadcast_to(loss, (1, D_MODEL))
        dn = dy * g3_ref[...]
        dx2v = r3 * (dn - n3 * _rowmean(dn * n3))
        dx2_ref[...] = dx2v
        dx2b_ref[...] = dx2v.astype(BF16)

    tile = lambda w: pl.BlockSpec((tr, w), lambda i: (i, 0))
    full = lambda a: pl.BlockSpec(a.shape, lambda i: (0,) * a.ndim)
    return pl.pallas_call(
        body, name="ffn_down", grid=(n,),
        in_specs=[tile(D_MODEL), tile(D_FF), ANY, full(g3), tile(D_MODEL)],
        out_specs=[tile(D_MODEL), tile(D_MODEL), pl.BlockSpec((8, D_MODEL), lambda i: (0, 0))],
        out_shape=[
            jax.ShapeDtypeStruct((seq, D_MODEL), F32), jax.ShapeDtypeStruct((seq, D_MODEL), BF16),
            jax.ShapeDtypeStruct((8, D_MODEL), F32),
        ],
        scratch_shapes=[pltpu.VMEM(wdown.shape, BF16), pltpu.SemaphoreType.DMA((1,))],
        compiler_params=pltpu.CompilerParams(dimension_semantics=("arbitrary",), vmem_limit_bytes=VMEM_LIMIT),
    )(x1, act, wdown, g3, target)


def _ffn_bwd(dx2, up, gcs, x1, g2, wup, wf, wdown, comm, tile_rows):
    seq = x1.shape[0]
    c_ins, c_shapes, c_sems, c_ops = _comm_plan(comm)
    nc = len(c_ins)
    tr = tile_rows
    n = seq // tr

    def body(dx2_ref, up_ref, gc_ref, x1_ref, g2_ref, wup_hbm, wf_ref, wdown_hbm, *rest):
        c_in, rest = rest[:nc], rest[nc:]
        dup_ref, dx1_ref, dx1b_ref, sm_ref, sf_ref = rest[:5]
        c_out, rest = rest[5:5 + nc], rest[5 + nc:]
        wup_v, wdown_v, dbuf, dcar, sem = rest[:5]
        c_sem_refs = rest[5:]
        i = pl.program_id(0)

        @pl.when(i == 0)
        def _():
            c_ops(c_in, c_out, c_sem_refs)[0]()
            _load_weights(((wup_hbm, wup_v), (wdown_hbm, wdown_v)), sem)
            dcar[...] = jnp.zeros(dcar.shape, F32)
            sm_ref[...] = jnp.zeros(sm_ref.shape, F32)
            sf_ref[...] = jnp.zeros(sf_ref.shape, F32)

        dx2v = dx2_ref[...]
        dx2b = dx2v.astype(BF16)
        dh2 = jnp.zeros((tr, D_MODEL), F32)

        def down_t(j):
            return _dot_nt(dx2b, wdown_v[j * FF_CHUNK:(j + 1) * FF_CHUNK, :])

        ahead = down_t(0)
        for j in range(N_FF_CHUNKS):
            cs = slice(j * FF_CHUNK, (j + 1) * FF_CHUNK)
            vs = slice(D_FF + j * FF_CHUNK, D_FF + (j + 1) * FF_CHUNK)
            dact = ahead
            if j + 1 < N_FF_CHUNKS:
                ahead = down_t(j + 1)
            gate = up_ref[:, cs].astype(F32)
            val = up_ref[:, vs].astype(F32)
            gc = gc_ref[:, cs].astype(F32)
            sg = _sigmoid(gc)
            dval = dact * (gc * sg)
            dgc = dact * val * (sg * (1.0 + gc * (1.0 - sg)))
            dbuf[0:tr, :] = dgc
            dbuf[tr:tr + 8, :] = dcar[:, cs]
            d_p1 = dbuf[1:1 + tr, :]
            d_p2 = dbuf[2:2 + tr, :]
            dgate = wf_ref[2:3, cs] * dgc + wf_ref[1:2, cs] * d_p1 + wf_ref[0:1, cs] * d_p2
            dcar[:, cs] = dgc[0:8, :]
            sf_ref[0:1, cs] += _colsum(d_p2 * gate)
            sf_ref[1:2, cs] += _colsum(d_p1 * gate)
            sf_ref[2:3, cs] += _colsum(dgc * gate)
            sf_ref[3:4, cs] += _colsum(dgc)
            dgb, dvb = dgate.astype(BF16), dval.astype(BF16)
            dup_ref[:, cs] = dgb
            dup_ref[:, vs] = dvb
            dh2 = dh2 + _dot_nt(dgb, wup_v[:, cs]) + _dot_nt(dvb, wup_v[:, vs])
        x1v = x1_ref[...]
        r2 = lax.rsqrt(_rowmean(x1v * x1v) + EPS)
        n2 = x1v * r2
        sm_ref[1:2, :] += _colsum(dh2 * n2)
        dn2 = dh2 * g2_ref[...]
        dx1v = dx2v + r2 * (dn2 - n2 * _rowmean(dn2 * n2))
        dx1_ref[...] = dx1v
        dx1b_ref[...] = dx1v.astype(BF16)

        @pl.when(i == n - 1)
        def _():
            c_ops(c_in, c_out, c_sem_refs)[1]()

    tile = lambda w: pl.BlockSpec((tr, w), lambda i: (n - 1 - i, 0))
    full = lambda a: pl.BlockSpec(a.shape, lambda i: (0,) * a.ndim)
    acc = lambda rows, w: pl.BlockSpec((rows, w), lambda i: (0, 0))
    return pl.pallas_call(
        body, name="ffn_bwd", grid=(n,),
        in_specs=[tile(D_MODEL), tile(2 * D_FF), tile(D_FF), tile(D_MODEL), full(g2), ANY, full(wf), ANY] + [ANY] * nc,
        out_specs=[tile(2 * D_FF), tile(D_MODEL), tile(D_MODEL), acc(8, D_MODEL), acc(8, D_FF)] + [ANY] * nc,
        out_shape=[
            jax.ShapeDtypeStruct((seq, 2 * D_FF), BF16), jax.ShapeDtypeStruct((seq, D_MODEL), F32),
            jax.ShapeDtypeStruct((seq, D_MODEL), BF16), jax.ShapeDtypeStruct((8, D_MODEL), F32),
            jax.ShapeDtypeStruct((8, D_FF), F32),
        ] + c_shapes,
        scratch_shapes=[
            pltpu.VMEM(wup.shape, BF16), pltpu.VMEM(wdown.shape, BF16),
            pltpu.VMEM((tr + 8, FF_CHUNK), F32), pltpu.VMEM((8, D_FF), F32), pltpu.SemaphoreType.DMA((2,)),
        ] + c_sems,
        compiler_params=pltpu.CompilerParams(dimension_semantics=("arbitrary",), vmem_limit_bytes=VMEM_LIMIT),
    )(dx2, up, gcs, x1, g2, wup, wf, wdown, *c_ins)


def _mixer_bwd(dx1, x, proj, cpre, d, g1, win, wa, lg, lb, pw, ps, wout, parts, tile_rows):
    seq = x.shape[0]
    n_parts = len(parts)
    tr = tile_rows
    n = seq // tr
    row_cb, row_lg, row_lb, row_ps = 32, 33, 34, 35

    def body(dx1_ref, x_ref, proj_ref, projh_ref, c_ref, d_ref, g1_ref, win_hbm, wa_ref, lg_ref, lb_ref, pw_ref, ps_ref,
             wout_hbm, *rest):
        part_refs, rest = rest[:n_parts], rest[n_parts:]
        dproj_ref, gx_ref, sm_ref, s5_ref, sp_ref = rest[:5]
        land_refs, rest = rest[5:5 + n_parts], rest[5 + n_parts:]
        win_v, wout_v, ubuf, ushift, dcbuf, dshift, ebuf, sem = rest[:8]
        ssems = rest[8:]
        i = pl.program_id(0)
        tile = n - 1 - i

        def scatter():
            return _scatter_ops(part_refs, land_refs, n_parts, ssems)

        @pl.when(i == 0)
        def _():
            scatter()[0]()
            _load_weights(((win_hbm, win_v), (wout_hbm, wout_v)), sem)
            dcbuf[tr:tr + A_HALO, :] = jnp.zeros((A_HALO, D_CONV), F32)
            ebuf[tr:tr + P_HALO, :] = jnp.zeros((P_HALO, D_POOL), F32)
            sm_ref[...] = jnp.zeros(sm_ref.shape, F32)
            s5_ref[...] = jnp.zeros(s5_ref.shape, F32)
            sp_ref[...] = jnp.zeros(sp_ref.shape, F32)

        dx1v = dx1_ref[...]
        dm = _dot_nt(dx1v.astype(BF16), wout_v[...])
        dya, dyb = dm[:, :D_CONV], dm[:, D_CONV:]
        dbis = []
        for g, w in enumerate(POOL_WINDOWS):
            cols = slice(g * POOL_GROUP, (g + 1) * POOL_GROUP)
            dgb = d_ref[:, cols]
            pwb = pw_ref[g].astype(BF16)
            dyg = dyb[:, cols]
            s5_ref[row_ps:row_ps + 1, cols] += _colsum(dyg * _dot(dgb, pwb))
            dqb = (dyg * ps_ref[:, cols]).astype(BF16)
            sp_ref[g] += _dot_tn(dgb, dqb)
            dd = _dot_nt(dqb, pwb)
            e = dd / _pool_count(tile, tr, w)
            ebuf[0:tr, cols] = e
            s = e
            for kk in range(1, w):
                s = s + ebuf[kk:kk + tr, cols]
            dbis.append(s - dd)
        ebuf[tr:tr + P_HALO, :] = ebuf[0:P_HALO, :]
        cv = c_ref[...].astype(F32)
        xc = cv - _rowmean(cv)
        rs = lax.rsqrt(_rowmean(xc * xc) + EPS)
        z = xc * rs
        ln = z * lg_ref[...] + lb_ref[...]
        sl = _sigmoid(ln)
        dl = dya * (sl * (1.0 + ln * (1.0 - sl)))
        s5_ref[row_lg:row_lg + 1, :] += _colsum(dl * z)
        s5_ref[row_lb:row_lb + 1, :] += _colsum(dl)
        dz = dl * lg_ref[...]
        dc = rs * (dz - _rowmean(dz) - z * _rowmean(dz * z))
        s5_ref[row_cb:row_cb + 1, :] += _colsum(dc)
        dcbuf[0:tr, :] = dc
        keep = (tile > 0).astype(F32)
        avh = projh_ref[:, :D_CONV].astype(F32)
        agh = projh_ref[:, D_CONV:].astype(F32)
        ubuf[0:A_HALO, :] = avh * _sigmoid(agh) * keep
        av = proj_ref[:, :D_CONV].astype(F32)
        ag = proj_ref[:, D_CONV:2 * D_CONV].astype(F32)
        sg = _sigmoid(ag)
        ubuf[A_HALO:A_HALO + tr, :] = av * sg
        off = A_HALO - (CONV_A - 1)
        du = wa_ref[CONV_A - 1:CONV_A, :] * dc
        dview = _shifted_views(dcbuf, dshift, tr)
        uview = _shifted_views(ubuf, ushift, tr)
        for j in range(CONV_A - 1):
            du = du + wa_ref[j:j + 1, :] * dview(CONV_A - 1 - j)
        for j in range(CONV_A):
            s5_ref[j:j + 1, :] += _colsum(dc * uview(off + j))
        dcbuf[tr:tr + A_HALO, :] = dcbuf[0:A_HALO, :]
        dav = du * sg
        dag = du * av * (sg * (1.0 - sg))
        dprojb = jnp.concatenate([dav, dag] + dbis, axis=1).astype(BF16)
        dproj_ref[...] = dprojb
        dh1 = _dot_nt(dprojb, win_v[...])
        xv = x_ref[...]
        r1 = lax.rsqrt(_rowmean(xv * xv) + EPS)
        n1 = xv * r1
        sm_ref[0:1, :] += _colsum(dh1 * n1)
        dn1 = dh1 * g1_ref[...]
        gx_ref[...] = dx1v + r1 * (dn1 - n1 * _rowmean(dn1 * n1))

        @pl.when(i == n - 1)
        def _():
            scatter()[1]()

    tile = lambda w: pl.BlockSpec((tr, w), lambda i: (n - 1 - i, 0))
    full = lambda a: pl.BlockSpec(a.shape, lambda i: (0,) * a.ndim)
    halo = pl.BlockSpec((A_HALO, 2 * D_CONV), lambda i: (jnp.maximum((n - 1 - i) * (tr // A_HALO) - 1, 0), 0))
    acc = lambda shape: pl.BlockSpec(shape, lambda i: (0,) * len(shape))
    return pl.pallas_call(
        body, name="mixer_bwd", grid=(n,),
        in_specs=[tile(D_MODEL), tile(D_MODEL), tile(D_IN), halo, tile(D_CONV), tile(D_POOL), full(g1), ANY, full(wa),
                  full(lg), full(lb), full(pw), full(ps), ANY] + [ANY] * n_parts,
        out_specs=[tile(D_IN), tile(D_MODEL), acc((8, D_MODEL)), acc((40, D_CONV)), acc(pw.shape)] + [ANY] * n_parts,
        out_shape=[
            jax.ShapeDtypeStruct((seq, D_IN), BF16), jax.ShapeDtypeStruct((seq, D_MODEL), F32),
            jax.ShapeDtypeStruct((8, D_MODEL), F32), jax.ShapeDtypeStruct((40, D_CONV), F32),
            jax.ShapeDtypeStruct(pw.shape, F32),
        ] + _scatter_shapes(parts, ()),
        scratch_shapes=[
            pltpu.VMEM(win.shape, BF16), pltpu.VMEM(wout.shape, BF16),
            pltpu.VMEM((tr + A_HALO, D_CONV), F32), pltpu.VMEM((7, tr + A_HALO - 8, D_CONV), F32),
            pltpu.VMEM((tr + A_HALO, D_CONV), F32), pltpu.VMEM((7, tr + A_HALO - 8, D_CONV), F32),
            pltpu.VMEM((tr + P_HALO, D_POOL), F32), pltpu.SemaphoreType.DMA((2,)),
        ] + _scatter_sems(n_parts),
        compiler_params=pltpu.CompilerParams(dimension_semantics=("arbitrary",), vmem_limit_bytes=VMEM_LIMIT),
    )(dx1, x, proj, proj, cpre, d, g1, win, wa, lg, lb, pw, ps, wout, *parts)


def _weight_grad(a, b, layout, k_rows, comm=None):
    seq, m_dim = a.shape
    n_dim = b.shape[1]
    steps = seq // k_rows

    def store(o_ref, acc, index, value):
        if steps == 1:
            o_ref[index] = value.astype(BF16)
            return
        s = pl.program_id(1)

        @pl.when(s == 0)
        def _():
            acc[index] = value

        @pl.when(jnp.logical_and(s > 0, s < steps - 1))
        def _():
            acc[index] += value

        @pl.when(s == steps - 1)
        def _():
            o_ref[index] = (acc[index] + value).astype(BF16)

    if layout in ("rows1", "rows2"):
        groups = int(layout[-1])
        per_tile = N_CHIPS // groups
        rows = m_dim // N_CHIPS // 2
        a_w = m_dim // groups

        def body(a_ref, b_ref, o_ref, acc):
            r = _dot_tn(a_ref[...], b_ref[...])
            for p in range(per_tile):
                for h in range(2):
                    store(o_ref, acc, (p, h), r[(2 * p + h) * rows:(2 * p + h + 1) * rows, :])

        in_specs = [pl.BlockSpec((k_rows, a_w), lambda g, s: (s, g)), pl.BlockSpec((k_rows, n_dim), lambda g, s: (s, 0))]
        out_spec = pl.BlockSpec((per_tile, 2, rows, n_dim), lambda g, s: (g, 0, 0, 0))
        out_dims, acc_dims = (N_CHIPS, 2, rows, n_dim), (per_tile, 2, rows, n_dim)
    elif layout == "cols_chip":
        groups = N_CHIPS
        rows, cols = m_dim // 2, n_dim // N_CHIPS

        def body(a_ref, b_ref, o_ref, acc):
            r = _dot_tn(a_ref[...], b_ref[...])
            for h in range(2):
                store(o_ref, acc, h, r[h * rows:(h + 1) * rows, :])

        in_specs = [pl.BlockSpec((k_rows, m_dim), lambda g, s: (s, 0)), pl.BlockSpec((k_rows, cols), lambda g, s: (s, g))]
        out_spec = pl.BlockSpec((None, 2, rows, cols), lambda g, s: (g, 0, 0, 0))
        out_dims, acc_dims = (N_CHIPS, 2, rows, cols), (2, rows, cols)
    else:
        groups = 2
        rows, cols = m_dim // 2, n_dim // N_CHIPS

        def body(a_ref, b_ref, o_ref, acc):
            r = _dot_tn(a_ref[...], b_ref[...])
            for k in range(N_CHIPS):
                store(o_ref, acc, k, r[:, k * cols:(k + 1) * cols])

        in_specs = [pl.BlockSpec((k_rows, rows), lambda g, s: (s, g)), pl.BlockSpec((k_rows, n_dim), lambda g, s: (s, 0))]
        out_spec = pl.BlockSpec((N_CHIPS, None, rows, cols), lambda g, s: (0, g, 0, 0))
        out_dims, acc_dims = (N_CHIPS, 2, rows, cols), (N_CHIPS, rows, cols)

    c_ins, c_shapes, c_sems, c_ops = _comm_plan(comm)
    nc = len(c_ins)

    def hosted(a_ref, b_ref, *rest):
        c_in, o_ref, c_out, acc, sems = rest[:nc], rest[nc], rest[nc + 1:2 * nc + 1], rest[2 * nc + 1], rest[2 * nc + 2:]
        g, s = pl.program_id(0), pl.program_id(1)
        if nc:
            @pl.when(jnp.logical_and(g == 0, s == 0))
            def _():
                c_ops(c_in, c_out, sems)[0]()

        body(a_ref, b_ref, o_ref, acc)
        if nc:
            @pl.when(jnp.logical_and(g == groups - 1, s == steps - 1))
            def _():
                c_ops(c_in, c_out, sems)[1]()

    outs = pl.pallas_call(
        hosted, name=f"weight_grad_{layout}_{m_dim}x{n_dim}", grid=(groups, steps),
        in_specs=in_specs + [ANY] * nc, out_specs=[out_spec] + [ANY] * nc,
        out_shape=[jax.ShapeDtypeStruct(out_dims, BF16)] + c_shapes,
        scratch_shapes=[pltpu.VMEM(acc_dims, F32)] + c_sems,
        compiler_params=pltpu.CompilerParams(dimension_semantics=("arbitrary", "arbitrary"), vmem_limit_bytes=VMEM_LIMIT),
    )(a, b, *c_ins)
    return outs if nc else outs[0]


def _exchange_ops(ins, outs, n_big, sems):
    send, recv = sems
    x, y, c, _, _ = _place()
    cps = [pltpu.make_async_remote_copy(
        src_ref=ins[t].at[:, 1 - c] if t < n_big else ins[t], dst_ref=outs[t], send_sem=send.at[t], recv_sem=recv.at[t],
        device_id=(x, y, 1 - c), device_id_type=MESH) for t in range(len(ins))]

    def start():
        for cp in cps:
            cp.start()

    def finish():
        for cp in cps:
            cp.wait()

    return start, finish


def _exchange_shapes(bigs, smalls):
    return [jax.ShapeDtypeStruct((N_CHIPS,) + b.shape[2:], b.dtype) for b in bigs] + [
        jax.ShapeDtypeStruct(s.shape, s.dtype) for s in smalls]


def _comm_plan(comm):
    if comm is None:
        return (), [], [], None
    kind, arrays = comm
    n = len(arrays)
    if kind == "scatter":
        return tuple(arrays), _scatter_shapes(arrays, ()), _scatter_sems(n), lambda i, o, sm: _scatter_ops(i, o, n, sm)
    return (tuple(arrays), _exchange_shapes(arrays, ()), [pltpu.SemaphoreType.DMA((n,))] * 2,
            lambda i, o, sm: _exchange_ops(i, o, n, sm))


def _sibling_exchange(bigs, smalls, tag):
    nb, nt = len(bigs), len(bigs) + len(smalls)

    def body(*refs):
        start, finish = _exchange_ops(refs[:nt], refs[nt:2 * nt], nb, refs[2 * nt:])
        start()
        finish()

    return pl.pallas_call(
        body, name=f"sibling_exchange_{tag}", out_shape=_exchange_shapes(bigs, smalls),
        in_specs=[ANY] * nt, out_specs=[ANY] * nt,
        scratch_shapes=[pltpu.SemaphoreType.DMA((nt,)), pltpu.SemaphoreType.DMA((nt,))],
    )(*bigs, *smalls)


def _pair_sum(core, mine, theirs, tag, block_rows):
    _, _, rows, cols = mine.shape
    steps = rows // block_rows

    def body(core_ref, a_ref, b_ref, o_ref):
        o_ref[...] = (a_ref[...].astype(F32) + b_ref[...].astype(F32)).astype(BF16)

    grid_spec = pltpu.PrefetchScalarGridSpec(
        num_scalar_prefetch=1, grid=(N_CHIPS, steps),
        in_specs=[pl.BlockSpec((None, None, block_rows, cols), lambda k, r, core_ref: (k, core_ref[0], r, 0)),
                  pl.BlockSpec((None, block_rows, cols), lambda k, r, core_ref: (k, r, 0))],
        out_specs=pl.BlockSpec((None, block_rows, cols), lambda k, r, core_ref: (k, r, 0)),
    )
    return pl.pallas_call(
        body, name=f"pair_sum_{tag}", grid_spec=grid_spec,
        out_shape=jax.ShapeDtypeStruct((N_CHIPS, rows, cols), BF16),
        compiler_params=pltpu.CompilerParams(dimension_semantics=("arbitrary", "arbitrary"), vmem_limit_bytes=VMEM_LIMIT),
    )(core, mine, theirs)


def _pair_sum_small(mine, theirs):
    (m_f2, m_b1, m_b2, m_sf, m_s5, m_sp) = mine

    def body(a0, a1, a2, a3, a4, a5, b0, b1, b2, b3, b4, b5, o_m, o_f, o_5, o_p):
        sm = (a0[...] + a1[...] + a2[...]) + (b0[...] + b1[...] + b2[...])
        sf = a3[...] + b3[...]
        s5 = a4[...] + b4[...]
        for h in range(2):
            o_m[h] = sm[:, h * (D_MODEL // 2):(h + 1) * (D_MODEL // 2)]
            o_f[h] = sf[:, h * (D_FF // 2):(h + 1) * (D_FF // 2)]
            o_5[h] = s5[:, h * (D_CONV // 2):(h + 1) * (D_CONV // 2)]
            for g in range(2):
                o_p[h, g] = a5[2 * h + g] + b5[2 * h + g]

    out_shape = [
        jax.ShapeDtypeStruct((2, 8, D_MODEL // 2), F32), jax.ShapeDtypeStruct((2, 8, D_FF // 2), F32),
        jax.ShapeDtypeStruct((2, 40, D_CONV // 2), F32), jax.ShapeDtypeStruct((2, 2, POOL_GROUP, POOL_GROUP), F32),
    ]
    return pl.pallas_call(body, name="pair_sum_small", out_shape=out_shape, in_specs=[VMEM] * 12, out_specs=[VMEM] * 4)(
        *mine, *theirs)


def _scatter_ops(ins, outs, n_parts, sems, landed=False):
    ici_send, ici_recv, fwd_send, fwd_recv, loc_sem = sems
    nt = len(ins)
    x, y, c, k, chips = _place()

    def src_of(t, kk):
        return ins[t].at[kk] if t < n_parts else ins[t].at[c]

    def ici(t, j, kk, slot):
        return pltpu.make_async_remote_copy(
            src_ref=src_of(t, kk), dst_ref=outs[t].at[slot, c], send_sem=ici_send.at[t * 3 + j],
            recv_sem=ici_recv.at[t * 3 + j], device_id=(*chips[j], c), device_id_type=MESH)

    def fwd(t, half):
        slots = outs[t].at[:, half]
        return pltpu.make_async_remote_copy(
            src_ref=slots, dst_ref=slots, send_sem=fwd_send.at[t], recv_sem=fwd_recv.at[t],
            device_id=(x, y, 1 - c), device_id_type=MESH)

    local = [pltpu.make_async_copy(src_of(t, k), outs[t].at[k, c], loc_sem.at[t]) for t in range(nt)]
    peers = [(t, j, 2 * qx + qy) for t in range(nt) for j, (qx, qy) in enumerate(chips)]
    sends = [] if landed else [ici(t, j, kq, k) for t, j, kq in peers]

    def start():
        for cp in local + sends:
            cp.start()

    def finish():
        if not landed:
            for t, j, kq in peers:
                ici(t, j, kq, kq).wait_recv()
        for cp in local:
            cp.wait()
        passed = [fwd(t, c) for t in range(nt)]
        for cp in passed:
            cp.start()
        for t in range(nt):
            fwd(t, 1 - c).wait_recv()
        for cp in sends + passed:
            cp.wait_send()

    return start, finish


def _scatter_sems(nt):
    return [pltpu.SemaphoreType.DMA((3 * nt,))] * 2 + [pltpu.SemaphoreType.DMA((nt,))] * 3


def _scatter_shapes(parts, smalls):
    return [jax.ShapeDtypeStruct((N_CHIPS, 2) + p.shape[1:], p.dtype) for p in tuple(parts) + tuple(smalls)]


HBM_SPEC = pl.BlockSpec(memory_space=pltpu.HBM)
SEM_SPEC = pl.BlockSpec(memory_space=pltpu.SEMAPHORE)
EFFECT = pltpu.SideEffectType.DATAFLOW_SIDE_EFFECTING


def _ici_copy(ins, lands, n_parts, send, recv, t, j):
    _, _, c, k, chips = _place()
    qx, qy = chips[j]
    src = ins[t].at[2 * qx + qy] if t < n_parts else ins[t].at[c]
    return pltpu.make_async_remote_copy(
        src_ref=src, dst_ref=lands[t].at[k, c], send_sem=send.at[t * 3 + j], recv_sem=recv.at[t * 3 + j],
        device_id=(qx, qy, c), device_id_type=MESH)


def _scatter_start(parts, smalls):
    arrays = tuple(parts) + tuple(smalls)
    nt = len(arrays)

    def body(*refs):
        ins, lands = refs[:nt], refs[nt:2 * nt]
        send, recv = refs[2 * nt], refs[2 * nt + 1]
        token = refs[-1]
        for t in range(nt):
            for j in range(3):
                _ici_copy(ins, lands, len(parts), send, recv, t, j).start()
        token[...] = jnp.zeros(token.shape, F32)

    land_shapes = _scatter_shapes(parts, smalls)
    out_shape = ([pltpu.SemaphoreType.DMA((3 * nt,))] * 2 + [pltpu.HBM(a.shape, a.dtype) for a in arrays]
                 + [pltpu.HBM(a.shape, a.dtype) for a in land_shapes] + [jax.ShapeDtypeStruct((8, 128), F32)])
    operands = [pltpu.with_memory_space_constraint(a, pltpu.HBM) for a in arrays]
    operands += [pltpu.with_memory_space_constraint(lax.empty(a.shape, a.dtype), pltpu.HBM) for a in land_shapes]
    outs = pl.pallas_call(
        body, name="scatter_start", out_shape=out_shape, in_specs=[HBM_SPEC] * (2 * nt),
        out_specs=[SEM_SPEC] * 2 + [HBM_SPEC] * (2 * nt) + [VMEM],
        input_output_aliases={i: 2 + i for i in range(2 * nt)},
        compiler_params=pltpu.CompilerParams(has_side_effects=EFFECT),
    )(*operands)
    return outs[0], outs[1], outs[2:2 + nt], outs[2 + nt:2 + 2 * nt], outs[-1]


def _scatter_wait(send, recv, ins, lands, n_parts, after):
    nt = len(ins)

    def body(*refs):
        in_refs, land_refs = refs[:nt], refs[nt:2 * nt]
        send_ref, recv_ref = refs[2 * nt], refs[2 * nt + 1]
        for t in range(nt):
            for j in range(3):
                cp = _ici_copy(in_refs, land_refs, n_parts, send_ref, recv_ref, t, j)
                cp.wait_send()
                cp.wait_recv()

    outs = pl.pallas_call(
        body, name="scatter_wait", out_shape=[pltpu.HBM(a.shape, a.dtype) for a in tuple(ins) + tuple(lands)],
        in_specs=[HBM_SPEC] * (2 * nt) + [SEM_SPEC] * 2 + [ANY] * len(after), out_specs=[HBM_SPEC] * (2 * nt),
        input_output_aliases={i: i for i in range(2 * nt)},
        compiler_params=pltpu.CompilerParams(has_side_effects=EFFECT),
    )(*ins, *lands, send, recv, *after)
    return outs[:nt], outs[nt:]


def _scatter_forward(ins, lands, n_parts):
    nt = len(ins)

    def body(*refs):
        start, finish = _scatter_ops(refs[:nt], refs[2 * nt:3 * nt], n_parts, refs[3 * nt:], landed=True)
        start()
        finish()

    return pl.pallas_call(
        body, name="scatter_forward", out_shape=[jax.ShapeDtypeStruct(a.shape, a.dtype) for a in lands],
        in_specs=[ANY] * (2 * nt), out_specs=[ANY] * nt, input_output_aliases={nt + i: i for i in range(nt)},
        scratch_shapes=_scatter_sems(nt),
    )(*ins, *lands)


def _chip_scatter(parts, smalls):
    nt = len(parts) + len(smalls)

    def body(*refs):
        start, finish = _scatter_ops(refs[:nt], refs[nt:2 * nt], len(parts), refs[2 * nt:])
        start()
        finish()

    return pl.pallas_call(
        body, name="chip_scatter", out_shape=_scatter_shapes(parts, smalls), in_specs=[ANY] * nt, out_specs=[ANY] * nt,
        scratch_shapes=_scatter_sems(nt),
    )(*parts, *smalls)


def _adamw(w, g, m, v):
    m = ADAM_B1 * m + (1.0 - ADAM_B1) * g
    v = ADAM_B2 * v + (1.0 - ADAM_B2) * (g * g)
    m_hat = m / (1.0 - ADAM_B1 ** ADAM_STEP)
    v_hat = v / (1.0 - ADAM_B2 ** ADAM_STEP)
    delta = -ADAM_LR * (m_hat / (jnp.sqrt(v_hat) + ADAM_EPS) + ADAM_WD * w)
    return delta, m, v


def _adam_big(parts, w, m, v, tag, block_rows, token):
    _, _, rows, cols = parts.shape
    steps = rows // block_rows

    def body(p_ref, w_ref, m_ref, v_ref, token_ref, g_out, d_out, m_out, v_out):
        g = p_ref[0].astype(F32)
        for q in range(1, N_CHIPS):
            g = g + p_ref[q].astype(F32)
        delta, m_new, v_new = _adamw(w_ref[...], g, m_ref[...], v_ref[...])
        g_out[...] = g
        d_out[...] = delta
        m_out[...] = m_new
        v_out[...] = v_new

    blk = pl.BlockSpec((block_rows, cols), lambda h, r: (h * steps + r, 0))
    return pl.pallas_call(
        body, name=f"adam_{tag}", grid=(2, steps),
        in_specs=[pl.BlockSpec((N_CHIPS, None, block_rows, cols), lambda h, r: (0, h, r, 0)), blk, blk, blk, ANY],
        out_specs=[blk] * 4, out_shape=[jax.ShapeDtypeStruct(w.shape, F32)] * 4,
        compiler_params=pltpu.CompilerParams(dimension_semantics=("arbitrary", "arbitrary"), vmem_limit_bytes=VMEM_LIMIT),
    )(parts, w, m, v, token)


def _reduce_small(l_m, l_f, l_5, l_p):
    def total(ref):
        t = ref[0]
        for q in range(1, N_CHIPS):
            t = t + ref[q]
        return t

    def body(m_ref, f_ref, s_ref, p_ref, g1_o, g2_o, g3_o, loss_o, wf_o, fb_o, wa_o, cb_o, lg_o, lb_o, ps_o, pw_o):
        tm, tf, t5, tp = total(m_ref), total(f_ref), total(s_ref), total(p_ref)
        sm = jnp.concatenate([tm[0], tm[1]], axis=1)
        sf = jnp.concatenate([tf[0], tf[1]], axis=1)
        s5 = jnp.concatenate([t5[0], t5[1]], axis=1)
        g1_o[...] = sm[0:1]
        g2_o[...] = sm[1:2]
        g3_o[...] = sm[2:3]
        loss_o[...] = sm[3:4, 0:128]
        wf_o[...] = sf
        fb_o[...] = sf[3:4]
        wa_o[...] = s5[0:32]
        cb_o[...] = s5[32:33]
        lg_o[...] = s5[33:34]
        lb_o[...] = s5[34:35]
        ps_o[...] = s5[35:36]
        for h in range(2):
            for g in range(2):
                pw_o[2 * h + g] = tp[h, g]

    row = lambda w: jax.ShapeDtypeStruct((1, w), F32)
    out_shape = [row(D_MODEL), row(D_MODEL), row(D_MODEL), row(128), jax.ShapeDtypeStruct((8, D_FF), F32), row(D_FF),
                 jax.ShapeDtypeStruct((32, D_CONV), F32), row(D_CONV), row(D_CONV), row(D_CONV), row(D_POOL),
                 jax.ShapeDtypeStruct((4, POOL_GROUP, POOL_GROUP), F32)]
    return pl.pallas_call(body, name="reduce_small", out_shape=out_shape, in_specs=[VMEM] * 4, out_specs=[VMEM] * 12)(
        l_m, l_f, l_5, l_p)


def _adam_small(ws, gs, ms, vs):
    count = len(ws)

    def body(*refs):
        w_r, g_r, m_r, v_r = (refs[t * count:(t + 1) * count] for t in range(4))
        d_o, m_o, v_o = (refs[(4 + t) * count:(5 + t) * count] for t in range(3))
        for t in range(count):
            delta, m_new, v_new = _adamw(w_r[t][...], g_r[t][...], m_r[t][...], v_r[t][...])
            d_o[t][...] = delta
            m_o[t][...] = m_new
            v_o[t][...] = v_new

    out_shape = [jax.ShapeDtypeStruct(w.shape, F32) for w in ws] * 3
    outs = pl.pallas_call(body, name="adam_small", out_shape=out_shape, in_specs=[VMEM] * (4 * count),
                          out_specs=[VMEM] * (3 * count))(*ws, *gs, *ms, *vs)
    return outs[:count], outs[count:2 * count], outs[2 * count:]


MIX_TILE = 512
FFN_TILE = 256
GRAD_K = 2048


def kernel(x, norm_mix_g, w_in, conv_a_w, conv_a_b, ln_a_g, ln_a_b, pool_w, pool_scale, w_out, norm_ffn_g, w_up, conv_f_w, conv_f_b, w_down, norm_final_g, loss_target, m_norm_mix_g, m_w_in, m_conv_a_w, m_conv_a_b, m_ln_a_g, m_ln_a_b, m_pool_w, m_pool_scale, m_w_out, m_norm_ffn_g, m_w_up, m_conv_f_w, m_conv_f_b, m_w_down, m_norm_final_g, v_norm_mix_g, v_w_in, v_conv_a_w, v_conv_a_b, v_ln_a_g, v_ln_a_b, v_pool_w, v_pool_scale, v_w_out, v_norm_ffn_g, v_w_up, v_conv_f_w, v_conv_f_b, v_w_down, v_norm_final_g):
    seq = x.shape[1]
    xs, ts = x[0], loss_target[0]
    mix_tile, ffn_tile, grad_k = min(MIX_TILE, seq), min(FFN_TILE, seq), min(GRAD_K, seq)
    chip = 2 * lax.axis_index("x") + lax.axis_index("y")
    core = lax.axis_index("c").astype(jnp.int32).reshape(1)

    wa_s = jnp.pad(conv_a_w[0], ((0, 32 - CONV_A), (0, 0)))
    wf_s = jnp.pad(conv_f_w[0], ((0, 8 - CONV_F), (0, 0)))
    win_b, wout_b, wup_b, wdown_b = _cast_shards(w_in[0], w_out[0], w_up[0], w_down[0])
    g3 = norm_final_g.reshape(1, D_MODEL)
    pw = pool_w[0]

    h1, proj, cpre, dpool, mcat, x1, win, wout, wup, wa_g, wf_g = _mixer_fwd(
        xs, norm_mix_g, win_b, wout_b, wup_b, wa_s, wf_s, conv_a_b, ln_a_g, ln_a_b, pw, pool_scale, mix_tile)
    wa = jnp.transpose(wa_g, (1, 0, 2)).reshape(32, D_CONV)
    wf = jnp.transpose(wf_g, (1, 0, 2)).reshape(8, D_FF)
    h2, up, gcs, act, wdown = _ffn_up(x1, norm_ffn_g, wup, wf, conv_f_b, wdown_b, ffn_tile)
    dx2, dx2b, sm_f2 = _ffn_down(x1, act, wdown, g3, ts, mix_tile)
    tags = ("w_in", "w_out", "w_up", "w_down")
    blocks = (256, 128, 256, 176)
    g_wdown = _weight_grad(act, dx2b, "rows2", grad_k)
    dup, dx1, dx1b, sm_b1, sf, l_wdown = _ffn_bwd(
        dx2, up, gcs, x1, norm_ffn_g, wup, wf, wdown, ("exchange", [g_wdown]), ffn_tile)
    p_wdown = _pair_sum(core, g_wdown, l_wdown, tags[3], blocks[3])
    g_wup, s_wdown = _weight_grad(h2, dup, "cols_chip", grad_k, ("scatter", [p_wdown]))
    g_wout, l_wup = _weight_grad(mcat, dx1b, "rows1", grad_k, ("exchange", [g_wup]))
    p_wup = _pair_sum(core, g_wup, l_wup, tags[2], blocks[2])
    l_wout, = _sibling_exchange((g_wout,), (), "early")
    p_wout = _pair_sum(core, g_wout, l_wout, tags[1], blocks[1])
    dproj, grad_x, sm_b2, s5, sp, s_wout, s_wup = _mixer_bwd(
        dx1, xs, proj, cpre, dpool, norm_mix_g, win, wa, ln_a_g, ln_a_b, pw, pool_scale, wout, [p_wout, p_wup], mix_tile)
    g_win = _weight_grad(h1, dproj, "cols_half", grad_k)

    smalls = (sm_f2, sm_b1, sm_b2, sf, s5, sp)
    landed = _sibling_exchange((g_win,), smalls, "late")
    part_win = _pair_sum(core, g_win, landed[0], tags[0], blocks[0])
    small_parts = _pair_sum_small(smalls, landed[1:])
    send, recv, late_src, late_land, token = _scatter_start([part_win], small_parts)
    big_w = (w_in[0], w_out[0], w_up[0], w_down[0])
    big_m = (m_w_in[0], m_w_out[0], m_w_up[0], m_w_down[0])
    big_v = (v_w_in[0], v_w_out[0], v_w_up[0], v_w_down[0])
    big = {}
    for t, p in ((1, s_wout), (2, s_wup), (3, s_wdown)):
        big[tags[t]] = _adam_big(p, big_w[t], big_m[t], big_v[t], tags[t], blocks[t], token)
    late_src, late_land = _scatter_wait(send, recv, late_src, late_land, 1, [big[tags[t]][3] for t in (1, 2, 3)])
    late = _scatter_forward(late_src, late_land, 1)
    big[tags[0]] = _adam_big(late[0], big_w[0], big_m[0], big_v[0], tags[0], blocks[0], token)
    big = {tag: [a[None] for a in outs] for tag, outs in big.items()}
    scattered = [None] * 4 + list(late[1:])

    (g_g1, g_g2, g_g3, loss_row, g_wf_all, g_fb, g_wa_all, g_cb, g_lg, g_lb, g_ps, g_pw) = _reduce_small(*scattered[4:])
    g_wa = lax.dynamic_slice(g_wa_all, (0, chip * (D_CONV // N_CHIPS)), (32, D_CONV // N_CHIPS))[:CONV_A]
    g_wf = lax.dynamic_slice(g_wf_all, (0, chip * (D_FF // N_CHIPS)), (8, D_FF // N_CHIPS))[:CONV_F]
    small_names = ("norm_mix_g", "conv_a_w", "conv_a_b", "ln_a_g", "ln_a_b", "pool_w", "pool_scale", "norm_ffn_g",
                   "conv_f_w", "conv_f_b", "norm_final_g")
    small_w = (norm_mix_g, conv_a_w[0], conv_a_b, ln_a_g, ln_a_b, pw, pool_scale, norm_ffn_g, conv_f_w[0], conv_f_b, g3)
    small_m = (m_norm_mix_g, m_conv_a_w[0], m_conv_a_b, m_ln_a_g, m_ln_a_b, m_pool_w[0], m_pool_scale, m_norm_ffn_g,
               m_conv_f_w[0], m_conv_f_b, m_norm_final_g.reshape(1, D_MODEL))
    small_v = (v_norm_mix_g, v_conv_a_w[0], v_conv_a_b, v_ln_a_g, v_ln_a_b, v_pool_w[0], v_pool_scale, v_norm_ffn_g,
               v_conv_f_w[0], v_conv_f_b, v_norm_final_g.reshape(1, D_MODEL))
    small_g = (g_g1, g_wa, g_cb, g_lg, g_lb, g_pw, g_ps, g_g2, g_wf, g_fb, g_g3)
    s_delta, s_m, s_v = _adam_small(small_w, small_g, small_m, small_v)
    shapes = {"conv_a_w": conv_a_w.shape, "pool_w": pool_w.shape, "conv_f_w": conv_f_w.shape, "norm_final_g": norm_final_g.shape}
    small = {}
    for t, name in enumerate(small_names):
        shp = shapes.get(name)
        small[name] = [a if shp is None else a.reshape(shp) for a in (small_g[t], s_delta[t], s_m[t], s_v[t])]

    order = ("norm_mix_g", "w_in", "conv_a_w", "conv_a_b", "ln_a_g", "ln_a_b", "pool_w", "pool_scale", "w_out", "norm_ffn_g",
             "w_up", "conv_f_w", "conv_f_b", "w_down", "norm_final_g")
    table = {**big, **small}
    loss = loss_row[0, 0]
    outs = [loss, grad_x[None]]
    for t in range(4):
        outs += [table[name][t] for name in order]
    return tuple(outs)
```

```python
import functools

import jax
import jax.numpy as jnp
from jax import lax
from jax.experimental import pallas as pl
from jax.experimental.pallas import tpu as pltpu

F32 = jnp.float32
BF16 = jnp.bfloat16
EPS = 1e-6
ADAM_LR = 0.001
ADAM_B1 = 0.9
ADAM_B2 = 0.999
ADAM_EPS = 1e-08
ADAM_WD = 0.01
ADAM_STEP = 10

D_MODEL = 1024
D_CONV = 512
D_POOL = 512
D_IN = 1536
D_FF = 2816
CONV_A = 31
CONV_F = 3
POOL_WINDOWS = (2, 4, 8, 16)
POOL_GROUP = 128
N_CHIPS = 4
FF_CHUNK = 256
N_FF_CHUNKS = D_FF // FF_CHUNK
A_HALO = 32
P_HALO = 16
VMEM_LIMIT = 56 * 1024 * 1024
MESH = pl.DeviceIdType.MESH

ANY = pl.BlockSpec(memory_space=pl.ANY)
VMEM = pl.BlockSpec(memory_space=pltpu.VMEM)


def _dot(a, b):
    return jnp.dot(a, b, preferred_element_type=F32)


def _dot_nt(a, b):
    return lax.dot_general(a, b, (((1,), (1,)), ((), ())), preferred_element_type=F32)


def _dot_tn(a, b):
    return lax.dot_general(a, b, (((0,), (0,)), ((), ())), preferred_element_type=F32)


def _sigmoid(v):
    return jax.nn.sigmoid(v)


def _colsum(v):
    return jnp.sum(v, axis=0, keepdims=True)


def _rowmean(v):
    return jnp.mean(v, axis=-1, keepdims=True)


def _place():
    x, y, c = lax.axis_index("x"), lax.axis_index("y"), lax.axis_index("c")
    chips = [(1 - x, y), (x, 1 - y), (1 - x, 1 - y)]
    return x, y, c, 2 * x + y, chips


def _gather_ops(bufs, fulls, col_sharded, sems):
    ici_send, ici_recv, fwd_send, fwd_recv, loc_sem = sems
    n_big = len(bufs)
    x, y, c, k, chips = _place()

    def block(i, kk, half=None):
        rows, cols = bufs[i].shape
        if col_sharded[i]:
            rs = slice(None) if half is None else pl.ds(pl.multiple_of(half * (rows // 2), 16), rows // 2)
            return fulls[i].at[rs, pl.ds(pl.multiple_of(kk * cols, 128), cols)]
        if half is None:
            return fulls[i].at[pl.ds(pl.multiple_of(kk * rows, 16), rows), :]
        return fulls[i].at[pl.ds(pl.multiple_of(kk * rows + half * (rows // 2), 16), rows // 2), :]

    def my_half(i):
        rows = bufs[i].shape[0]
        return bufs[i].at[pl.ds(pl.multiple_of(c * (rows // 2), 16), rows // 2), :]

    def ici(i, j, kk):
        return pltpu.make_async_remote_copy(
            src_ref=my_half(i), dst_ref=block(i, kk, c), send_sem=ici_send.at[i * 3 + j], recv_sem=ici_recv.at[i * 3 + j],
            device_id=(*chips[j], c), device_id_type=MESH)

    def fwd(i, j, kk, half):
        return pltpu.make_async_remote_copy(
            src_ref=block(i, kk, half), dst_ref=block(i, kk, half), send_sem=fwd_send.at[i * 3 + j],
            recv_sem=fwd_recv.at[i * 3 + j], device_id=(x, y, 1 - c), device_id_type=MESH)

    local = [pltpu.make_async_copy(bufs[i], block(i, k), loc_sem.at[i]) for i in range(n_big)]
    sends = [ici(i, j, k) for i in range(n_big) for j in range(3)]
    peers = [(i, j, 2 * qx + qy) for i in range(n_big) for j, (qx, qy) in enumerate(chips)]

    def start():
        for cp in local + sends:
            cp.start()

    def finish():
        passed = []
        for i, j, kq in peers:
            ici(i, j, kq).wait_recv()
            cp = fwd(i, j, kq, c)
            cp.start()
            passed.append(cp)
        for i, j, kq in peers:
            fwd(i, j, kq, 1 - c).wait_recv()
        for cp in sends + passed:
            cp.wait_send()
        for cp in local:
            cp.wait()

    return start, finish


def _gather_sems(n_big):
    return [pltpu.SemaphoreType.DMA((3 * n_big,))] * 4 + [pltpu.SemaphoreType.DMA((n_big,))]


def _tap_ops(srcs, dsts, sems):
    send, recv, loc = sems
    _, _, c, k, chips = _place()

    def copy(t, j, kk):
        return pltpu.make_async_remote_copy(
            src_ref=srcs[t], dst_ref=dsts[t].at[kk], send_sem=send.at[t * 3 + j], recv_sem=recv.at[t * 3 + j],
            device_id=(*chips[j], c), device_id_type=MESH)

    local = [pltpu.make_async_copy(srcs[t], dsts[t].at[k], loc.at[t]) for t in range(len(srcs))]
    sends = [[copy(t, j, k) for j in range(3)] for t in range(len(srcs))]

    def start():
        for t, cp in enumerate(local):
            cp.start()
            for sd in sends[t]:
                sd.start()

    def wait(t):
        for j, (qx, qy) in enumerate(chips):
            copy(t, j, 2 * qx + qy).wait_recv()
        for sd in sends[t]:
            sd.wait_send()
        local[t].wait()

    return start, wait


def _cast_shards(*shards):
    def body(*refs):
        for src, dst in zip(refs[:len(shards)], refs[len(shards):]):
            dst[...] = src[...].astype(BF16)

    return pl.pallas_call(
        body, name="cast_shards", out_shape=[jax.ShapeDtypeStruct(s.shape, BF16) for s in shards],
        in_specs=[VMEM] * len(shards), out_specs=[VMEM] * len(shards),
        compiler_params=pltpu.CompilerParams(vmem_limit_bytes=VMEM_LIMIT),
    )(*shards)


def _load_weights(pairs, sem):
    cps = [pltpu.make_async_copy(src, dst, sem.at[i]) for i, (src, dst) in enumerate(pairs)]
    for cp in cps:
        cp.start()
    for cp in cps:
        cp.wait()


def _shifted_views(buf, shifted, t_rows):
    n = t_rows + A_HALO - 8
    for b in range(1, 8):
        shifted[b - 1] = buf[b:b + n, :]

    def view(offset):
        a, b = divmod(offset, 8)
        if b == 0:
            return buf[8 * a:8 * a + t_rows, :]
        return shifted[b - 1, 8 * a:8 * a + t_rows, :]

    return view


def _pool_count(tile, t_rows, w):
    row = lax.broadcasted_iota(jnp.int32, (t_rows, POOL_GROUP), 0) + tile * t_rows
    return jnp.minimum(row + 1, w).astype(F32)


def _mixer_fwd(x, g1, win_b, wout_b, wup_b, wa_s, wf_s, cb, lg, lb, pw, ps, tile_rows):
    seq = x.shape[0]
    tr = tile_rows
    n = seq // tr

    def body(x_ref, g1_ref, win_b_hbm, wout_b_hbm, wup_b_hbm, wa_s_hbm, wf_s_hbm, cb_ref, lg_ref, lb_ref, pw_ref,
             ps_ref, h1_ref, proj_ref, c_ref, d_ref, m_ref, x1_ref, win_f, wout_f, wup_f, wa_g, wf_g,
             win_v, wout_v, wa_ref, ubuf, ushift, bbuf, sem, *csems):
        i = pl.program_id(0)
        first_sems, later_sems, tap_sems = csems[0:5], csems[5:10], csems[10:13]

        def first():
            return _gather_ops((win_b_hbm, wout_b_hbm), (win_f, wout_f), (True, False), first_sems)

        def later():
            return _gather_ops((wup_b_hbm,), (wup_f,), (True,), later_sems)

        def taps():
            return _tap_ops((wa_s_hbm, wf_s_hbm), (wa_g, wf_g), tap_sems)

        @pl.when(i == 0)
        def _():
            first()[0]()
            taps()[0]()
            later()[0]()
            first()[1]()
            taps()[1](0)
            loads = [(win_f, win_v), (wout_f, wout_v)]
            loads += [(wa_g.at[kk], wa_ref.at[:, kk * (D_CONV // N_CHIPS):(kk + 1) * (D_CONV // N_CHIPS)]) for kk in range(N_CHIPS)]
            _load_weights(loads, sem)
            ubuf[0:A_HALO, :] = jnp.zeros((A_HALO, D_CONV), F32)
            bbuf[0:P_HALO, :] = jnp.zeros((P_HALO, D_POOL), F32)

        xv = x_ref[...]
        r = lax.rsqrt(_rowmean(xv * xv) + EPS)
        h1 = (xv * r * g1_ref[...]).astype(BF16)
        h1_ref[...] = h1
        proj = _dot(h1, win_v[...])
        proj_ref[...] = proj.astype(BF16)
        av, ag, bi = proj[:, :D_CONV], proj[:, D_CONV:2 * D_CONV], proj[:, 2 * D_CONV:]
        ubuf[A_HALO:A_HALO + tr, :] = av * _sigmoid(ag)
        off = A_HALO - (CONV_A - 1)
        uview = _shifted_views(ubuf, ushift, tr)
        acc = wa_ref[0:1, :] * uview(off)
        for j in range(1, CONV_A):
            acc = acc + wa_ref[j:j + 1, :] * uview(off + j)
        cv = acc + cb_ref[...]
        ubuf[0:A_HALO, :] = ubuf[tr:tr + A_HALO, :]
        c_ref[...] = cv.astype(BF16)
        xc = cv - _rowmean(cv)
        z = xc * lax.rsqrt(_rowmean(xc * xc) + EPS)
        ln = z * lg_ref[...] + lb_ref[...]
        ya = ln * _sigmoid(ln)
        bbuf[P_HALO:P_HALO + tr, :] = bi
        ds, ybs = [], []
        for g, w in enumerate(POOL_WINDOWS):
            cols = slice(g * POOL_GROUP, (g + 1) * POOL_GROUP)
            s = bi[:, cols]
            for kk in range(1, w):
                s = s + bbuf[P_HALO - kk:P_HALO - kk + tr, cols]
            dg = s / _pool_count(i, tr, w) - bi[:, cols]
            ds.append(dg)
            ybs.append(_dot(dg.astype(BF16), pw_ref[g].astype(BF16)))
        bbuf[0:P_HALO, :] = bbuf[tr:tr + P_HALO, :]
        d_ref[...] = jnp.concatenate(ds, axis=1).astype(BF16)
        yb = jnp.concatenate(ybs, axis=1) * ps_ref[...]
        m = jnp.concatenate([ya, yb], axis=1).astype(BF16)
        m_ref[...] = m
        x1_ref[...] = xv + _dot(m, wout_v[...])

        @pl.when(i == n - 1)
        def _():
            later()[1]()
            taps()[1](1)

    tile = lambda w: pl.BlockSpec((tr, w), lambda i: (i, 0))
    full = lambda a: pl.BlockSpec(a.shape, lambda i: (0,) * a.ndim)
    return pl.pallas_call(
        body, name="mixer_fwd", grid=(n,),
        in_specs=[tile(D_MODEL), full(g1)] + [ANY] * 5 + [full(cb), full(lg), full(lb), full(pw), full(ps)],
        out_specs=[tile(D_MODEL), tile(D_IN), tile(D_CONV), tile(D_POOL), tile(D_MODEL), tile(D_MODEL)] + [ANY] * 5,
        out_shape=[
            jax.ShapeDtypeStruct((seq, D_MODEL), BF16), jax.ShapeDtypeStruct((seq, D_IN), BF16),
            jax.ShapeDtypeStruct((seq, D_CONV), BF16), jax.ShapeDtypeStruct((seq, D_POOL), BF16),
            jax.ShapeDtypeStruct((seq, D_MODEL), BF16), jax.ShapeDtypeStruct((seq, D_MODEL), F32),
            jax.ShapeDtypeStruct((D_MODEL, D_IN), BF16), jax.ShapeDtypeStruct((D_MODEL, D_MODEL), BF16),
            jax.ShapeDtypeStruct((D_MODEL, 2 * D_FF), BF16),
            jax.ShapeDtypeStruct((N_CHIPS,) + wa_s.shape, F32), jax.ShapeDtypeStruct((N_CHIPS,) + wf_s.shape, F32),
        ],
        scratch_shapes=[
            pltpu.VMEM((D_MODEL, D_IN), BF16), pltpu.VMEM((D_MODEL, D_MODEL), BF16), pltpu.VMEM((32, D_CONV), F32),
            pltpu.VMEM((tr + A_HALO, D_CONV), F32), pltpu.VMEM((7, tr + A_HALO - 8, D_CONV), F32),
            pltpu.VMEM((tr + P_HALO, D_POOL), F32), pltpu.SemaphoreType.DMA((2 + N_CHIPS,)),
        ] + _gather_sems(2) + _gather_sems(1) + [
            pltpu.SemaphoreType.DMA((6,)), pltpu.SemaphoreType.DMA((6,)), pltpu.SemaphoreType.DMA((2,))],
        compiler_params=pltpu.CompilerParams(dimension_semantics=("arbitrary",), vmem_limit_bytes=VMEM_LIMIT),
    )(x, g1, win_b, wout_b, wup_b, wa_s, wf_s, cb, lg, lb, pw, ps)


def _ffn_up(x1, g2, wup, wf, fb, wdown_b, tile_rows):
    seq = x1.shape[0]
    tr = tile_rows
    n = seq // tr

    def body(x1_ref, g2_ref, wup_hbm, wf_ref, fb_ref, wdown_b_hbm,
             h2_ref, up_ref, gc_ref, act_ref, wdown_f, wup_v, gbuf, sem, *gsems):
        i = pl.program_id(0)

        def gather():
            return _gather_ops((wdown_b_hbm,), (wdown_f,), (False,), gsems)

        @pl.when(i == 0)
        def _():
            gather()[0]()
            _load_weights(((wup_hbm, wup_v),), sem)
            gbuf[0:8, :] = jnp.zeros((8, D_FF), F32)

        x1v = x1_ref[...]
        r2 = lax.rsqrt(_rowmean(x1v * x1v) + EPS)
        h2 = (x1v * r2 * g2_ref[...]).astype(BF16)
        h2_ref[...] = h2

        def up_proj(j):
            return (_dot(h2, wup_v[:, j * FF_CHUNK:(j + 1) * FF_CHUNK]),
                    _dot(h2, wup_v[:, D_FF + j * FF_CHUNK:D_FF + (j + 1) * FF_CHUNK]))

        ahead = up_proj(0)
        for j in range(N_FF_CHUNKS):
            cs = slice(j * FF_CHUNK, (j + 1) * FF_CHUNK)
            vs = slice(D_FF + j * FF_CHUNK, D_FF + (j + 1) * FF_CHUNK)
            gate, val = ahead
            if j + 1 < N_FF_CHUNKS:
                ahead = up_proj(j + 1)
            up_ref[:, cs] = gate.astype(BF16)
            up_ref[:, vs] = val.astype(BF16)
            gbuf[8:8 + tr, cs] = gate
            gc = (wf_ref[0:1, cs] * gbuf[6:6 + tr, cs] + wf_ref[1:2, cs] * gbuf[7:7 + tr, cs]
                  + wf_ref[2:3, cs] * gate + fb_ref[:, cs])
            gbuf[0:8, cs] = gbuf[tr:tr + 8, cs]
            gc_ref[:, cs] = gc.astype(BF16)
            act_ref[:, cs] = (gc * _sigmoid(gc) * val).astype(BF16)

        @pl.when(i == n - 1)
        def _():
            gather()[1]()

    tile = lambda w: pl.BlockSpec((tr, w), lambda i: (i, 0))
    full = lambda a: pl.BlockSpec(a.shape, lambda i: (0,) * a.ndim)
    return pl.pallas_call(
        body, name="ffn_up", grid=(n,),
        in_specs=[tile(D_MODEL), full(g2), ANY, full(wf), full(fb), ANY],
        out_specs=[tile(D_MODEL), tile(2 * D_FF), tile(D_FF), tile(D_FF), ANY],
        out_shape=[
            jax.ShapeDtypeStruct((seq, D_MODEL), BF16), jax.ShapeDtypeStruct((seq, 2 * D_FF), BF16),
            jax.ShapeDtypeStruct((seq, D_FF), BF16), jax.ShapeDtypeStruct((seq, D_FF), BF16),
            jax.ShapeDtypeStruct((D_FF, D_MODEL), BF16),
        ],
        scratch_shapes=[pltpu.VMEM(wup.shape, BF16), pltpu.VMEM((tr + 8, D_FF), F32), pltpu.SemaphoreType.DMA((1,))]
        + _gather_sems(1),
        compiler_params=pltpu.CompilerParams(dimension_semantics=("arbitrary",), vmem_limit_bytes=VMEM_LIMIT),
    )(x1, g2, wup, wf, fb, wdown_b)


def _ffn_down(x1, act, wdown, g3, target, tile_rows):
    seq = x1.shape[0]
    tr = tile_rows
    n = seq // tr

    def body(x1_ref, act_ref, wdown_hbm, g3_ref, t_ref, dx2_ref, dx2b_ref, sm_ref, wdown_v, sem):
        i = pl.program_id(0)

        @pl.when(i == 0)
        def _():
            _load_weights(((wdown_hbm, wdown_v),), sem)
            sm_ref[...] = jnp.zeros(sm_ref.shape, F32)

        x2 = x1_ref[...] + _dot(act_ref[...], wdown_v[...])
        r3 = lax.rsqrt(_rowmean(x2 * x2) + EPS)
        n3 = x2 * r3
        err = n3 * g3_ref[...] - t_ref[...]
        dy = err / D_MODEL
        sm_ref[2:3, :] += _colsum(dy * n3)
        loss = 0.5 * _colsum(_rowmean(err * err))
        sm_ref[3:4, :] += jnp.broadcast_to(loss, (1, D_MODEL))
        dn = dy * g3_ref[...]
        dx2v = r3 * (dn - n3 * _rowmean(dn * n3))
        dx2_ref[...] = dx2v
        dx2b_ref[...] = dx2v.astype(BF16)

    tile = lambda w: pl.BlockSpec((tr, w), lambda i: (i, 0))
    full = lambda a: pl.BlockSpec(a.shape, lambda i: (0,) * a.ndim)
    return pl.pallas_call(
        body, name="ffn_down", grid=(n,),
        in_specs=[tile(D_MODEL), tile(D_FF), ANY, full(g3), tile(D_MODEL)],
        out_specs=[tile(D_MODEL), tile(D_MODEL), pl.BlockSpec((8, D_MODEL), lambda i: (0, 0))],
        out_shape=[
            jax.ShapeDtypeStruct((seq, D_MODEL), F32), jax.ShapeDtypeStruct((seq, D_MODEL), BF16),
            jax.ShapeDtypeStruct((8, D_MODEL), F32),
        ],
        scratch_shapes=[pltpu.VMEM(wdown.shape, BF16), pltpu.SemaphoreType.DMA((1,))],
        compiler_params=pltpu.CompilerParams(dimension_semantics=("arbitrary",), vmem_limit_bytes=VMEM_LIMIT),
    )(x1, act, wdown, g3, target)


def _ffn_bwd(dx2, up, gcs, x1, g2, wup, wf, wdown, comm, tile_rows):
    seq = x1.shape[0]
    c_ins, c_shapes, c_sems, c_ops = _comm_plan(comm)
    nc = len(c_ins)
    tr = tile_rows
    n = seq // tr

    def body(dx2_ref, up_ref, gc_ref, x1_ref, g2_ref, wup_hbm, wf_ref, wdown_hbm, *rest):
        c_in, rest = rest[:nc], rest[nc:]
        dup_ref, dx1_ref, dx1b_ref, sm_ref, sf_ref = rest[:5]
        c_out, rest = rest[5:5 + nc], rest[5 + nc:]
        wup_v, wdown_v, dbuf, dcar, sem = rest[:5]
        c_sem_refs = rest[5:]
        i = pl.program_id(0)

        @pl.when(i == 0)
        def _():
            c_ops(c_in, c_out, c_sem_refs)[0]()
            _load_weights(((wup_hbm, wup_v), (wdown_hbm, wdown_v)), sem)
            dcar[...] = jnp.zeros(dcar.shape, F32)
            sm_ref[...] = jnp.zeros(sm_ref.shape, F32)
            sf_ref[...] = jnp.zeros(sf_ref.shape, F32)

        dx2v = dx2_ref[...]
        dx2b = dx2v.astype(BF16)
        dh2 = jnp.zeros((tr, D_MODEL), F32)

        def down_t(j):
            return _dot_nt(dx2b, wdown_v[j * FF_CHUNK:(j + 1) * FF_CHUNK, :])

        ahead = down_t(0)
        for j in range(N_FF_CHUNKS):
            cs = slice(j * FF_CHUNK, (j + 1) * FF_CHUNK)
            vs = slice(D_FF + j * FF_CHUNK, D_FF + (j + 1) * FF_CHUNK)
            dact = ahead
            if j + 1 < N_FF_CHUNKS:
                ahead = down_t(j + 1)
            gate = up_ref[:, cs].astype(F32)
            val = up_ref[:, vs].astype(F32)
            gc = gc_ref[:, cs].astype(F32)
            sg = _sigmoid(gc)
            dval = dact * (gc * sg)
            dgc = dact * val * (sg * (1.0 + gc * (1.0 - sg)))
            dbuf[0:tr, :] = dgc
            dbuf[tr:tr + 8, :] = dcar[:, cs]
            d_p1 = dbuf[1:1 + tr, :]
            d_p2 = dbuf[2:2 + tr, :]
            dgate = wf_ref[2:3, cs] * dgc + wf_ref[1:2, cs] * d_p1 + wf_ref[0:1, cs] * d_p2
            dcar[:, cs] = dgc[0:8, :]
            sf_ref[0:1, cs] += _colsum(d_p2 * gate)
            sf_ref[1:2, cs] += _colsum(d_p1 * gate)
            sf_ref[2:3, cs] += _colsum(dgc * gate)
            sf_ref[3:4, cs] += _colsum(dgc)
            dgb, dvb = dgate.astype(BF16), dval.astype(BF16)
            dup_ref[:, cs] = dgb
            dup_ref[:, vs] = dvb
            dh2 = dh2 + _dot_nt(dgb, wup_v[:, cs]) + _dot_nt(dvb, wup_v[:, vs])
        x1v = x1_ref[...]
        r2 = lax.rsqrt(_rowmean(x1v * x1v) + EPS)
        n2 = x1v * r2
        sm_ref[1:2, :] += _colsum(dh2 * n2)
        dn2 = dh2 * g2_ref[...]
        dx1v = dx2v + r2 * (dn2 - n2 * _rowmean(dn2 * n2))
        dx1_ref[...] = dx1v
        dx1b_ref[...] = dx1v.astype(BF16)

        @pl.when(i == n - 1)
        def _():
            c_ops(c_in, c_out, c_sem_refs)[1]()

    tile = lambda w: pl.BlockSpec((tr, w), lambda i: (n - 1 - i, 0))
    full = lambda a: pl.BlockSpec(a.shape, lambda i: (0,) * a.ndim)
    acc = lambda rows, w: pl.BlockSpec((rows, w), lambda i: (0, 0))
    return pl.pallas_call(
        body, name="ffn_bwd", grid=(n,),
        in_specs=[tile(D_MODEL), tile(2 * D_FF), tile(D_FF), tile(D_MODEL), full(g2), ANY, full(wf), ANY] + [ANY] * nc,
        out_specs=[tile(2 * D_FF), tile(D_MODEL), tile(D_MODEL), acc(8, D_MODEL), acc(8, D_FF)] + [ANY] * nc,
        out_shape=[
            jax.ShapeDtypeStruct((seq, 2 * D_FF), BF16), jax.ShapeDtypeStruct((seq, D_MODEL), F32),
            jax.ShapeDtypeStruct((seq, D_MODEL), BF16), jax.ShapeDtypeStruct((8, D_MODEL), F32),
            jax.ShapeDtypeStruct((8, D_FF), F32),
        ] + c_shapes,
        scratch_shapes=[
            pltpu.VMEM(wup.shape, BF16), pltpu.VMEM(wdown.shape, BF16),
            pltpu.VMEM((tr + 8, FF_CHUNK), F32), pltpu.VMEM((8, D_FF), F32), pltpu.SemaphoreType.DMA((2,)),
        ] + c_sems,
        compiler_params=pltpu.CompilerParams(dimension_semantics=("arbitrary",), vmem_limit_bytes=VMEM_LIMIT),
    )(dx2, up, gcs, x1, g2, wup, wf, wdown, *c_ins)


def _mixer_bwd(dx1, x, proj, cpre, d, g1, win, wa, lg, lb, pw, ps, wout, parts, tile_rows):
    seq = x.shape[0]
    n_parts = len(parts)
    tr = tile_rows
    n = seq // tr
    row_cb, row_lg, row_lb, row_ps = 32, 33, 34, 35

    def body(dx1_ref, x_ref, proj_ref, projh_ref, c_ref, d_ref, g1_ref, win_hbm, wa_ref, lg_ref, lb_ref, pw_ref, ps_ref,
             wout_hbm, *rest):
        part_refs, rest = rest[:n_parts], rest[n_parts:]
        dproj_ref, gx_ref, sm_ref, s5_ref, sp_ref = rest[:5]
        land_refs, rest = rest[5:5 + n_parts], rest[5 + n_parts:]
        win_v, wout_v, ubuf, ushift, dcbuf, dshift, ebuf, sem = rest[:8]
        ssems = rest[8:]
        i = pl.program_id(0)
        tile = n - 1 - i

        def scatter():
            return _scatter_ops(part_refs, land_refs, n_parts, ssems)

        @pl.when(i == 0)
        def _():
            scatter()[0]()
            _load_weights(((win_hbm, win_v), (wout_hbm, wout_v)), sem)
            dcbuf[tr:tr + A_HALO, :] = jnp.zeros((A_HALO, D_CONV), F32)
            ebuf[tr:tr + P_HALO, :] = jnp.zeros((P_HALO, D_POOL), F32)
            sm_ref[...] = jnp.zeros(sm_ref.shape, F32)
            s5_ref[...] = jnp.zeros(s5_ref.shape, F32)
            sp_ref[...] = jnp.zeros(sp_ref.shape, F32)

        dx1v = dx1_ref[...]
        dm = _dot_nt(dx1v.astype(BF16), wout_v[...])
        dya, dyb = dm[:, :D_CONV], dm[:, D_CONV:]
        dbis = []
        for g, w in enumerate(POOL_WINDOWS):
            cols = slice(g * POOL_GROUP, (g + 1) * POOL_GROUP)
            dgb = d_ref[:, cols]
            pwb = pw_ref[g].astype(BF16)
            dyg = dyb[:, cols]
            s5_ref[row_ps:row_ps + 1, cols] += _colsum(dyg * _dot(dgb, pwb))
            dqb = (dyg * ps_ref[:, cols]).astype(BF16)
            sp_ref[g] += _dot_tn(dgb, dqb)
            dd = _dot_nt(dqb, pwb)
            e = dd / _pool_count(tile, tr, w)
            ebuf[0:tr, cols] = e
            s = e
            for kk in range(1, w):
                s = s + ebuf[kk:kk + tr, cols]
            dbis.append(s - dd)
        ebuf[tr:tr + P_HALO, :] = ebuf[0:P_HALO, :]
        cv = c_ref[...].astype(F32)
        xc = cv - _rowmean(cv)
        rs = lax.rsqrt(_rowmean(xc * xc) + EPS)
        z = xc * rs
        ln = z * lg_ref[...] + lb_ref[...]
        sl = _sigmoid(ln)
        dl = dya * (sl * (1.0 + ln * (1.0 - sl)))
        s5_ref[row_lg:row_lg + 1, :] += _colsum(dl * z)
        s5_ref[row_lb:row_lb + 1, :] += _colsum(dl)
        dz = dl * lg_ref[...]
        dc = rs * (dz - _rowmean(dz) - z * _rowmean(dz * z))
        s5_ref[row_cb:row_cb + 1, :] += _colsum(dc)
        dcbuf[0:tr, :] = dc
        keep = (tile > 0).astype(F32)
        avh = projh_ref[:, :D_CONV].astype(F32)
        agh = projh_ref[:, D_CONV:].astype(F32)
        ubuf[0:A_HALO, :] = avh * _sigmoid(agh) * keep
        av = proj_ref[:, :D_CONV].astype(F32)
        ag = proj_ref[:, D_CONV:2 * D_CONV].astype(F32)
        sg = _sigmoid(ag)
        ubuf[A_HALO:A_HALO + tr, :] = av * sg
        off = A_HALO - (CONV_A - 1)
        du = wa_ref[CONV_A - 1:CONV_A, :] * dc
        dview = _shifted_views(dcbuf, dshift, tr)
        uview = _shifted_views(ubuf, ushift, tr)
        for j in range(CONV_A - 1):
            du = du + wa_ref[j:j + 1, :] * dview(CONV_A - 1 - j)
        for j in range(CONV_A):
            s5_ref[j:j + 1, :] += _colsum(dc * uview(off + j))
        dcbuf[tr:tr + A_HALO, :] = dcbuf[0:A_HALO, :]
        dav = du * sg
        dag = du * av * (sg * (1.0 - sg))
        dprojb = jnp.concatenate([dav, dag] + dbis, axis=1).astype(BF16)
        dproj_ref[...] = dprojb
        dh1 = _dot_nt(dprojb, win_v[...])
        xv = x_ref[...]
        r1 = lax.rsqrt(_rowmean(xv * xv) + EPS)
        n1 = xv * r1
        sm_ref[0:1, :] += _colsum(dh1 * n1)
        dn1 = dh1 * g1_ref[...]
        gx_ref[...] = dx1v + r1 * (dn1 - n1 * _rowmean(dn1 * n1))

        @pl.when(i == n - 1)
        def _():
            scatter()[1]()

    tile = lambda w: pl.BlockSpec((tr, w), lambda i: (n - 1 - i, 0))
    full = lambda a: pl.BlockSpec(a.shape, lambda i: (0,) * a.ndim)
    halo = pl.BlockSpec((A_HALO, 2 * D_CONV), lambda i: (jnp.maximum((n - 1 - i) * (tr // A_HALO) - 1, 0), 0))
    acc = lambda shape: pl.BlockSpec(shape, lambda i: (0,) * len(shape))
    return pl.pallas_call(
        body, name="mixer_bwd", grid=(n,),
        in_specs=[tile(D_MODEL), tile(D_MODEL), tile(D_IN), halo, tile(D_CONV), tile(D_POOL), full(g1), ANY, full(wa),
                  full(lg), full(lb), full(pw), full(ps), ANY] + [ANY] * n_parts,
        out_specs=[tile(D_IN), tile(D_MODEL), acc((8, D_MODEL)), acc((40, D_CONV)), acc(pw.shape)] + [ANY] * n_parts,
        out_shape=[
            jax.ShapeDtypeStruct((seq, D_IN), BF16), jax.ShapeDtypeStruct((seq, D_MODEL), F32),
            jax.ShapeDtypeStruct((8, D_MODEL), F32), jax.ShapeDtypeStruct((40, D_CONV), F32),
            jax.ShapeDtypeStruct(pw.shape, F32),
        ] + _scatter_shapes(parts, ()),
        scratch_shapes=[
            pltpu.VMEM(win.shape, BF16), pltpu.VMEM(wout.shape, BF16),
            pltpu.VMEM((tr + A_HALO, D_CONV), F32), pltpu.VMEM((7, tr + A_HALO - 8, D_CONV), F32),
            pltpu.VMEM((tr + A_HALO, D_CONV), F32), pltpu.VMEM((7, tr + A_HALO - 8, D_CONV), F32),
            pltpu.VMEM((tr + P_HALO, D_POOL), F32), pltpu.SemaphoreType.DMA((2,)),
        ] + _scatter_sems(n_parts),
        compiler_params=pltpu.CompilerParams(dimension_semantics=("arbitrary",), vmem_limit_bytes=VMEM_LIMIT),
    )(dx1, x, proj, proj, cpre, d, g1, win, wa, lg, lb, pw, ps, wout, *parts)


def _weight_grad(a, b, layout, k_rows, comm=None):
    seq, m_dim = a.shape
    n_dim = b.shape[1]
    steps = seq // k_rows

    def store(o_ref, acc, index, value):
        if steps == 1:
            o_ref[index] = value.astype(BF16)
            return
        s = pl.program_id(1)

        @pl.when(s == 0)
        def _():
            acc[index] = value

        @pl.when(jnp.logical_and(s > 0, s < steps - 1))
        def _():
            acc[index] += value

        @pl.when(s == steps - 1)
        def _():
            o_ref[index] = (acc[index] + value).astype(BF16)

    if layout in ("rows1", "rows2"):
        groups = int(layout[-1])
        per_tile = N_CHIPS // groups
        rows = m_dim // N_CHIPS // 2
        a_w = m_dim // groups

        def body(a_ref, b_ref, o_ref, acc):
            r = _dot_tn(a_ref[...], b_ref[...])
            for p in range(per_tile):
                for h in range(2):
                    store(o_ref, acc, (h, p), r[(2 * p + h) * rows:(2 * p + h + 1) * rows, :])

        in_specs = [pl.BlockSpec((k_rows, a_w), lambda g, s: (s, g)), pl.BlockSpec((k_rows, n_dim), lambda g, s: (s, 0))]
        out_spec = pl.BlockSpec((2, per_tile, rows, n_dim), lambda g, s: (0, g, 0, 0))
        out_dims, acc_dims = (2, N_CHIPS, rows, n_dim), (2, per_tile, rows, n_dim)
    elif layout == "cols_chip":
        groups = N_CHIPS
        rows, cols = m_dim // 2, n_dim // N_CHIPS

        def body(a_ref, b_ref, o_ref, acc):
            r = _dot_tn(a_ref[...], b_ref[...])
            for h in range(2):
                store(o_ref, acc, h, r[h * rows:(h + 1) * rows, :])

        in_specs = [pl.BlockSpec((k_rows, m_dim), lambda g, s: (s, 0)), pl.BlockSpec((k_rows, cols), lambda g, s: (s, g))]
        out_spec = pl.BlockSpec((2, None, rows, cols), lambda g, s: (0, g, 0, 0))
        out_dims, acc_dims = (2, N_CHIPS, rows, cols), (2, rows, cols)
    else:
        groups = 2
        rows, cols = m_dim // 2, n_dim // N_CHIPS

        def body(a_ref, b_ref, o_ref, acc):
            r = _dot_tn(a_ref[...], b_ref[...])
            for k in range(N_CHIPS):
                store(o_ref, acc, k, r[:, k * cols:(k + 1) * cols])

        in_specs = [pl.BlockSpec((k_rows, rows), lambda g, s: (s, g)), pl.BlockSpec((k_rows, n_dim), lambda g, s: (s, 0))]
        out_spec = pl.BlockSpec((None, N_CHIPS, rows, cols), lambda g, s: (g, 0, 0, 0))
        out_dims, acc_dims = (2, N_CHIPS, rows, cols), (N_CHIPS, rows, cols)

    c_ins, c_shapes, c_sems, c_ops = _comm_plan(comm)
    nc = len(c_ins)

    def hosted(a_ref, b_ref, *rest):
        c_in, o_ref, c_out, acc, sems = rest[:nc], rest[nc], rest[nc + 1:2 * nc + 1], rest[2 * nc + 1], rest[2 * nc + 2:]
        g, s = pl.program_id(0), pl.program_id(1)
        if nc:
            @pl.when(jnp.logical_and(g == 0, s == 0))
            def _():
                c_ops(c_in, c_out, sems)[0]()

        body(a_ref, b_ref, o_ref, acc)
        if nc:
            @pl.when(jnp.logical_and(g == groups - 1, s == steps - 1))
            def _():
                c_ops(c_in, c_out, sems)[1]()

    outs = pl.pallas_call(
        hosted, name=f"weight_grad_{layout}_{m_dim}x{n_dim}", grid=(groups, steps),
        in_specs=in_specs + [ANY] * nc, out_specs=[out_spec] + [ANY] * nc,
        out_shape=[jax.ShapeDtypeStruct(out_dims, BF16)] + c_shapes,
        scratch_shapes=[pltpu.VMEM(acc_dims, F32)] + c_sems,
        compiler_params=pltpu.CompilerParams(dimension_semantics=("arbitrary", "arbitrary"), vmem_limit_bytes=VMEM_LIMIT),
    )(a, b, *c_ins)
    return outs if nc else outs[0]


def _exchange_ops(ins, outs, n_big, sems):
    send, recv = sems
    x, y, c, _, _ = _place()
    cps = [pltpu.make_async_remote_copy(
        src_ref=ins[t].at[1 - c] if t < n_big else ins[t], dst_ref=outs[t], send_sem=send.at[t], recv_sem=recv.at[t],
        device_id=(x, y, 1 - c), device_id_type=MESH) for t in range(len(ins))]

    def start():
        for cp in cps:
            cp.start()

    def finish():
        for cp in cps:
            cp.wait()

    return start, finish


def _exchange_shapes(bigs, smalls):
    return [jax.ShapeDtypeStruct((N_CHIPS,) + b.shape[2:], b.dtype) for b in bigs] + [
        jax.ShapeDtypeStruct(s.shape, s.dtype) for s in smalls]


def _comm_plan(comm):
    if comm is None:
        return (), [], [], None
    kind, arrays = comm
    n = len(arrays)
    if kind == "scatter":
        return tuple(arrays), _scatter_shapes(arrays, ()), _scatter_sems(n), lambda i, o, sm: _scatter_ops(i, o, n, sm)
    return (tuple(arrays), _exchange_shapes(arrays, ()), [pltpu.SemaphoreType.DMA((n,))] * 2,
            lambda i, o, sm: _exchange_ops(i, o, n, sm))


def _sibling_exchange(bigs, smalls, tag):
    nb, nt = len(bigs), len(bigs) + len(smalls)

    def body(*refs):
        start, finish = _exchange_ops(refs[:nt], refs[nt:2 * nt], nb, refs[2 * nt:])
        start()
        finish()

    return pl.pallas_call(
        body, name=f"sibling_exchange_{tag}", out_shape=_exchange_shapes(bigs, smalls),
        in_specs=[ANY] * nt, out_specs=[ANY] * nt,
        scratch_shapes=[pltpu.SemaphoreType.DMA((nt,)), pltpu.SemaphoreType.DMA((nt,))],
    )(*bigs, *smalls)


def _pair_sum(core, mine, theirs, tag, block_rows):
    _, _, rows, cols = mine.shape
    steps = rows // block_rows

    def body(core_ref, a_ref, b_ref, o_ref):
        o_ref[...] = (a_ref[...].astype(F32) + b_ref[...].astype(F32)).astype(BF16)

    grid_spec = pltpu.PrefetchScalarGridSpec(
        num_scalar_prefetch=1, grid=(N_CHIPS, steps),
        in_specs=[pl.BlockSpec((None, None, block_rows, cols), lambda k, r, core_ref: (core_ref[0], k, r, 0)),
                  pl.BlockSpec((None, block_rows, cols), lambda k, r, core_ref: (k, r, 0))],
        out_specs=pl.BlockSpec((None, block_rows, cols), lambda k, r, core_ref: (k, r, 0)),
    )
    return pl.pallas_call(
        body, name=f"pair_sum_{tag}", grid_spec=grid_spec,
        out_shape=jax.ShapeDtypeStruct((N_CHIPS, rows, cols), BF16),
        compiler_params=pltpu.CompilerParams(dimension_semantics=("arbitrary", "arbitrary"), vmem_limit_bytes=VMEM_LIMIT),
    )(core, mine, theirs)


def _pair_sum_small(mine, theirs):
    (m_f2, m_b1, m_b2, m_sf, m_s5, m_sp) = mine

    def body(a0, a1, a2, a3, a4, a5, b0, b1, b2, b3, b4, b5, o_m, o_f, o_5, o_p):
        sm = (a0[...] + a1[...] + a2[...]) + (b0[...] + b1[...] + b2[...])
        sf = a3[...] + b3[...]
        s5 = a4[...] + b4[...]
        for h in range(2):
            o_m[h] = sm[:, h * (D_MODEL // 2):(h + 1) * (D_MODEL // 2)]
            o_f[h] = sf[:, h * (D_FF // 2):(h + 1) * (D_FF // 2)]
            o_5[h] = s5[:, h * (D_CONV // 2):(h + 1) * (D_CONV // 2)]
            for g in range(2):
                o_p[h, g] = a5[2 * h + g] + b5[2 * h + g]

    out_shape = [
        jax.ShapeDtypeStruct((2, 8, D_MODEL // 2), F32), jax.ShapeDtypeStruct((2, 8, D_FF // 2), F32),
        jax.ShapeDtypeStruct((2, 40, D_CONV // 2), F32), jax.ShapeDtypeStruct((2, 2, POOL_GROUP, POOL_GROUP), F32),
    ]
    return pl.pallas_call(body, name="pair_sum_small", out_shape=out_shape, in_specs=[VMEM] * 12, out_specs=[VMEM] * 4)(
        *mine, *theirs)


def _scatter_ops(ins, outs, n_parts, sems, landed=False):
    ici_send, ici_recv, fwd_send, fwd_recv, loc_sem = sems
    nt = len(ins)
    x, y, c, k, chips = _place()

    def src_of(t, kk):
        return ins[t].at[kk] if t < n_parts else ins[t].at[c]

    def ici(t, j, kk, slot):
        return pltpu.make_async_remote_copy(
            src_ref=src_of(t, kk), dst_ref=outs[t].at[c, slot], send_sem=ici_send.at[t * 3 + j],
            recv_sem=ici_recv.at[t * 3 + j], device_id=(*chips[j], c), device_id_type=MESH)

    def fwd(t, half):
        slots = outs[t].at[half]
        return pltpu.make_async_remote_copy(
            src_ref=slots, dst_ref=slots, send_sem=fwd_send.at[t], recv_sem=fwd_recv.at[t],
            device_id=(x, y, 1 - c), device_id_type=MESH)

    local = [pltpu.make_async_copy(src_of(t, k), outs[t].at[c, k], loc_sem.at[t]) for t in range(nt)]
    peers = [(t, j, 2 * qx + qy) for t in range(nt) for j, (qx, qy) in enumerate(chips)]
    sends = [] if landed else [ici(t, j, kq, k) for t, j, kq in peers]

    def start():
        for cp in local + sends:
            cp.start()

    def finish():
        if not landed:
            for t, j, kq in peers:
                ici(t, j, kq, kq).wait_recv()
        for cp in local:
            cp.wait()
        passed = [fwd(t, c) for t in range(nt)]
        for cp in passed:
            cp.start()
        for t in range(nt):
            fwd(t, 1 - c).wait_recv()
        for cp in sends + passed:
            cp.wait_send()

    return start, finish


def _scatter_sems(nt):
    return [pltpu.SemaphoreType.DMA((3 * nt,))] * 2 + [pltpu.SemaphoreType.DMA((nt,))] * 3


def _scatter_shapes(parts, smalls):
    return [jax.ShapeDtypeStruct((2, N_CHIPS) + p.shape[1:], p.dtype) for p in tuple(parts) + tuple(smalls)]


HBM_SPEC = pl.BlockSpec(memory_space=pltpu.HBM)
SEM_SPEC = pl.BlockSpec(memory_space=pltpu.SEMAPHORE)
EFFECT = pltpu.SideEffectType.DATAFLOW_SIDE_EFFECTING


def _ici_copy(ins, lands, n_parts, send, recv, t, j):
    _, _, c, k, chips = _place()
    qx, qy = chips[j]
    src = ins[t].at[2 * qx + qy] if t < n_parts else ins[t].at[c]
    return pltpu.make_async_remote_copy(
        src_ref=src, dst_ref=lands[t].at[c, k], send_sem=send.at[t * 3 + j], recv_sem=recv.at[t * 3 + j],
        device_id=(qx, qy, c), device_id_type=MESH)


def _scatter_start(parts, smalls):
    arrays = tuple(parts) + tuple(smalls)
    nt = len(arrays)

    def body(*refs):
        ins, lands = refs[:nt], refs[nt:2 * nt]
        send, recv = refs[2 * nt], refs[2 * nt + 1]
        token = refs[-1]
        for t in range(nt):
            for j in range(3):
                _ici_copy(ins, lands, len(parts), send, recv, t, j).start()
        token[...] = jnp.zeros(token.shape, F32)

    land_shapes = _scatter_shapes(parts, smalls)
    out_shape = ([pltpu.SemaphoreType.DMA((3 * nt,))] * 2 + [pltpu.HBM(a.shape, a.dtype) for a in arrays]
                 + [pltpu.HBM(a.shape, a.dtype) for a in land_shapes] + [jax.ShapeDtypeStruct((8, 128), F32)])
    operands = [pltpu.with_memory_space_constraint(a, pltpu.HBM) for a in arrays]
    operands += [pltpu.with_memory_space_constraint(lax.empty(a.shape, a.dtype), pltpu.HBM) for a in land_shapes]
    outs = pl.pallas_call(
        body, name="scatter_start", out_shape=out_shape, in_specs=[HBM_SPEC] * (2 * nt),
        out_specs=[SEM_SPEC] * 2 + [HBM_SPEC] * (2 * nt) + [VMEM],
        input_output_aliases={i: 2 + i for i in range(2 * nt)},
        compiler_params=pltpu.CompilerParams(has_side_effects=EFFECT),
    )(*operands)
    return outs[0], outs[1], outs[2:2 + nt], outs[2 + nt:2 + 2 * nt], outs[-1]


def _scatter_wait(send, recv, ins, lands, n_parts, after):
    nt = len(ins)

    def body(*refs):
        in_refs, land_refs = refs[:nt], refs[nt:2 * nt]
        send_ref, recv_ref = refs[2 * nt], refs[2 * nt + 1]
        for t in range(nt):
            for j in range(3):
                cp = _ici_copy(in_refs, land_refs, n_parts, send_ref, recv_ref, t, j)
                cp.wait_send()
                cp.wait_recv()

    outs = pl.pallas_call(
        body, name="scatter_wait", out_shape=[pltpu.HBM(a.shape, a.dtype) for a in tuple(ins) + tuple(lands)],
        in_specs=[HBM_SPEC] * (2 * nt) + [SEM_SPEC] * 2 + [ANY] * len(after), out_specs=[HBM_SPEC] * (2 * nt),
        input_output_aliases={i: i for i in range(2 * nt)},
        compiler_params=pltpu.CompilerParams(has_side_effects=EFFECT),
    )(*ins, *lands, send, recv, *after)
    return outs[:nt], outs[nt:]


def _scatter_forward(ins, lands, n_parts):
    nt = len(ins)

    def body(*refs):
        start, finish = _scatter_ops(refs[:nt], refs[2 * nt:3 * nt], n_parts, refs[3 * nt:], landed=True)
        start()
        finish()

    return pl.pallas_call(
        body, name="scatter_forward", out_shape=[jax.ShapeDtypeStruct(a.shape, a.dtype) for a in lands],
        in_specs=[ANY] * (2 * nt), out_specs=[ANY] * nt, input_output_aliases={nt + i: i for i in range(nt)},
        scratch_shapes=_scatter_sems(nt),
    )(*ins, *lands)


def _chip_scatter(parts, smalls):
    nt = len(parts) + len(smalls)

    def body(*refs):
        start, finish = _scatter_ops(refs[:nt], refs[nt:2 * nt], len(parts), refs[2 * nt:])
        start()
        finish()

    return pl.pallas_call(
        body, name="chip_scatter", out_shape=_scatter_shapes(parts, smalls), in_specs=[ANY] * nt, out_specs=[ANY] * nt,
        scratch_shapes=_scatter_sems(nt),
    )(*parts, *smalls)


def _adamw(w, g, m, v):
    m = ADAM_B1 * m + (1.0 - ADAM_B1) * g
    v = ADAM_B2 * v + (1.0 - ADAM_B2) * (g * g)
    m_hat = m / (1.0 - ADAM_B1 ** ADAM_STEP)
    v_hat = v / (1.0 - ADAM_B2 ** ADAM_STEP)
    delta = -ADAM_LR * (m_hat / (jnp.sqrt(v_hat) + ADAM_EPS) + ADAM_WD * w)
    return delta, m, v


def _adam_big(parts, w, m, v, tag, block_rows, token):
    _, _, rows, cols = parts.shape
    steps = rows // block_rows

    def body(p_ref, w_ref, m_ref, v_ref, token_ref, g_out, d_out, m_out, v_out):
        g = p_ref[0].astype(F32)
        for q in range(1, N_CHIPS):
            g = g + p_ref[q].astype(F32)
        delta, m_new, v_new = _adamw(w_ref[...], g, m_ref[...], v_ref[...])
        g_out[...] = g
        d_out[...] = delta
        m_out[...] = m_new
        v_out[...] = v_new

    blk = pl.BlockSpec((block_rows, cols), lambda h, r: (h * steps + r, 0))
    return pl.pallas_call(
        body, name=f"adam_{tag}", grid=(2, steps),
        in_specs=[pl.BlockSpec((None, N_CHIPS, block_rows, cols), lambda h, r: (h, 0, r, 0)), blk, blk, blk, ANY],
        out_specs=[blk] * 4, out_shape=[jax.ShapeDtypeStruct(w.shape, F32)] * 4,
        compiler_params=pltpu.CompilerParams(dimension_semantics=("arbitrary", "arbitrary"), vmem_limit_bytes=VMEM_LIMIT),
    )(parts, w, m, v, token)


def _reduce_small(l_m, l_f, l_5, l_p):
    def total(ref):
        t = ref[:, 0]
        for q in range(1, N_CHIPS):
            t = t + ref[:, q]
        return t

    def body(m_ref, f_ref, s_ref, p_ref, g1_o, g2_o, g3_o, loss_o, wf_o, fb_o, wa_o, cb_o, lg_o, lb_o, ps_o, pw_o):
        tm, tf, t5, tp = total(m_ref), total(f_ref), total(s_ref), total(p_ref)
        sm = jnp.concatenate([tm[0], tm[1]], axis=1)
        sf = jnp.concatenate([tf[0], tf[1]], axis=1)
        s5 = jnp.concatenate([t5[0], t5[1]], axis=1)
        g1_o[...] = sm[0:1]
        g2_o[...] = sm[1:2]
        g3_o[...] = sm[2:3]
        loss_o[...] = sm[3:4, 0:128]
        wf_o[...] = sf
        fb_o[...] = sf[3:4]
        wa_o[...] = s5[0:32]
        cb_o[...] = s5[32:33]
        lg_o[...] = s5[33:34]
        lb_o[...] = s5[34:35]
        ps_o[...] = s5[35:36]
        for h in range(2):
            for g in range(2):
                pw_o[2 * h + g] = tp[h, g]

    row = lambda w: jax.ShapeDtypeStruct((1, w), F32)
    out_shape = [row(D_MODEL), row(D_MODEL), row(D_MODEL), row(128), jax.ShapeDtypeStruct((8, D_FF), F32), row(D_FF),
                 jax.ShapeDtypeStruct((32, D_CONV), F32), row(D_CONV), row(D_CONV), row(D_CONV), row(D_POOL),
                 jax.ShapeDtypeStruct((4, POOL_GROUP, POOL_GROUP), F32)]
    return pl.pallas_call(body, name="reduce_small", out_shape=out_shape, in_specs=[VMEM] * 4, out_specs=[VMEM] * 12)(
        l_m, l_f, l_5, l_p)


def _adam_small(ws, gs, ms, vs):
    count = len(ws)

    def body(*refs):
        w_r, g_r, m_r, v_r = (refs[t * count:(t + 1) * count] for t in range(4))
        d_o, m_o, v_o = (refs[(4 + t) * count:(5 + t) * count] for t in range(3))
        for t in range(count):
            delta, m_new, v_new = _adamw(w_r[t][...], g_r[t][...], m_r[t][...], v_r[t][...])
            d_o[t][...] = delta
            m_o[t][...] = m_new
            v_o[t][...] = v_new

    out_shape = [jax.ShapeDtypeStruct(w.shape, F32) for w in ws] * 3
    outs = pl.pallas_call(body, name="adam_small", out_shape=out_shape, in_specs=[VMEM] * (4 * count),
                          out_specs=[VMEM] * (3 * count))(*ws, *gs, *ms, *vs)
    return outs[:count], outs[count:2 * count], outs[2 * count:]


MIX_TILE = 512
FFN_TILE = 256
GRAD_K = 2048


def kernel(x, norm_mix_g, w_in, conv_a_w, conv_a_b, ln_a_g, ln_a_b, pool_w, pool_scale, w_out, norm_ffn_g, w_up, conv_f_w, conv_f_b, w_down, norm_final_g, loss_target, m_norm_mix_g, m_w_in, m_conv_a_w, m_conv_a_b, m_ln_a_g, m_ln_a_b, m_pool_w, m_pool_scale, m_w_out, m_norm_ffn_g, m_w_up, m_conv_f_w, m_conv_f_b, m_w_down, m_norm_final_g, v_norm_mix_g, v_w_in, v_conv_a_w, v_conv_a_b, v_ln_a_g, v_ln_a_b, v_pool_w, v_pool_scale, v_w_out, v_norm_ffn_g, v_w_up, v_conv_f_w, v_conv_f_b, v_w_down, v_norm_final_g):
    seq = x.shape[1]
    xs, ts = x[0], loss_target[0]
    mix_tile, ffn_tile, grad_k = min(MIX_TILE, seq), min(FFN_TILE, seq), min(GRAD_K, seq)
    chip = 2 * lax.axis_index("x") + lax.axis_index("y")
    core = lax.axis_index("c").astype(jnp.int32).reshape(1)

    wa_s = jnp.pad(conv_a_w[0], ((0, 32 - CONV_A), (0, 0)))
    wf_s = jnp.pad(conv_f_w[0], ((0, 8 - CONV_F), (0, 0)))
    win_b, wout_b, wup_b, wdown_b = _cast_shards(w_in[0], w_out[0], w_up[0], w_down[0])
    g3 = norm_final_g.reshape(1, D_MODEL)
    pw = pool_w[0]

    h1, proj, cpre, dpool, mcat, x1, win, wout, wup, wa_g, wf_g = _mixer_fwd(
        xs, norm_mix_g, win_b, wout_b, wup_b, wa_s, wf_s, conv_a_b, ln_a_g, ln_a_b, pw, pool_scale, mix_tile)
    wa = jnp.transpose(wa_g, (1, 0, 2)).reshape(32, D_CONV)
    wf = jnp.transpose(wf_g, (1, 0, 2)).reshape(8, D_FF)
    h2, up, gcs, act, wdown = _ffn_up(x1, norm_ffn_g, wup, wf, conv_f_b, wdown_b, ffn_tile)
    dx2, dx2b, sm_f2 = _ffn_down(x1, act, wdown, g3, ts, mix_tile)
    tags = ("w_in", "w_out", "w_up", "w_down")
    blocks = (256, 128, 256, 176)
    g_wdown = _weight_grad(act, dx2b, "rows2", grad_k)
    dup, dx1, dx1b, sm_b1, sf, l_wdown = _ffn_bwd(
        dx2, up, gcs, x1, norm_ffn_g, wup, wf, wdown, ("exchange", [g_wdown]), ffn_tile)
    p_wdown = _pair_sum(core, g_wdown, l_wdown, tags[3], blocks[3])
    g_wup, s_wdown = _weight_grad(h2, dup, "cols_chip", grad_k, ("scatter", [p_wdown]))
    g_wout, l_wup = _weight_grad(mcat, dx1b, "rows1", grad_k, ("exchange", [g_wup]))
    p_wup = _pair_sum(core, g_wup, l_wup, tags[2], blocks[2])
    l_wout, = _sibling_exchange((g_wout,), (), "early")
    p_wout = _pair_sum(core, g_wout, l_wout, tags[1], blocks[1])
    dproj, grad_x, sm_b2, s5, sp, s_wout, s_wup = _mixer_bwd(
        dx1, xs, proj, cpre, dpool, norm_mix_g, win, wa, ln_a_g, ln_a_b, pw, pool_scale, wout, [p_wout, p_wup], mix_tile)
    g_win = _weight_grad(h1, dproj, "cols_half", grad_k)

    smalls = (sm_f2, sm_b1, sm_b2, sf, s5, sp)
    landed = _sibling_exchange((g_win,), smalls, "late")
    part_win = _pair_sum(core, g_win, landed[0], tags[0], blocks[0])
    small_parts = _pair_sum_small(smalls, landed[1:])
    send, recv, late_src, late_land, token = _scatter_start([part_win], small_parts)
    big_w = (w_in[0], w_out[0], w_up[0], w_down[0])
    big_m = (m_w_in[0], m_w_out[0], m_w_up[0], m_w_down[0])
    big_v = (v_w_in[0], v_w_out[0], v_w_up[0], v_w_down[0])
    big = {}
    for t, p in ((1, s_wout), (2, s_wup), (3, s_wdown)):
        big[tags[t]] = _adam_big(p, big_w[t], big_m[t], big_v[t], tags[t], blocks[t], token)
    late_src, late_land = _scatter_wait(send, recv, late_src, late_land, 1, [big[tags[t]][3] for t in (1, 2, 3)])
    late = _scatter_forward(late_src, late_land, 1)
    big[tags[0]] = _adam_big(late[0], big_w[0], big_m[0], big_v[0], tags[0], blocks[0], token)
    big = {tag: [a[None] for a in outs] for tag, outs in big.items()}
    scattered = [None] * 4 + list(late[1:])

    (g_g1, g_g2, g_g3, loss_row, g_wf_all, g_fb, g_wa_all, g_cb, g_lg, g_lb, g_ps, g_pw) = _reduce_small(*scattered[4:])
    g_wa = lax.dynamic_slice(g_wa_all, (0, chip * (D_CONV // N_CHIPS)), (32, D_CONV // N_CHIPS))[:CONV_A]
    g_wf = lax.dynamic_slice(g_wf_all, (0, chip * (D_FF // N_CHIPS)), (8, D_FF // N_CHIPS))[:CONV_F]
    small_names = ("norm_mix_g", "conv_a_w", "conv_a_b", "ln_a_g", "ln_a_b", "pool_w", "pool_scale", "norm_ffn_g",
                   "conv_f_w", "conv_f_b", "norm_final_g")
    small_w = (norm_mix_g, conv_a_w[0], conv_a_b, ln_a_g, ln_a_b, pw, pool_scale, norm_ffn_g, conv_f_w[0], conv_f_b, g3)
    small_m = (m_norm_mix_g, m_conv_a_w[0], m_conv_a_b, m_ln_a_g, m_ln_a_b, m_pool_w[0], m_pool_scale, m_norm_ffn_g,
               m_conv_f_w[0], m_conv_f_b, m_norm_final_g.reshape(1, D_MODEL))
    small_v = (v_norm_mix_g, v_conv_a_w[0], v_conv_a_b, v_ln_a_g, v_ln_a_b, v_pool_w[0], v_pool_scale, v_norm_ffn_g,
               v_conv_f_w[0], v_conv_f_b, v_norm_final_g.reshape(1, D_MODEL))
    small_g = (g_g1, g_wa, g_cb, g_lg, g_lb, g_pw, g_ps, g_g2, g_wf, g_fb, g_g3)
    s_delta, s_m, s_v = _adam_small(small_w, small_g, small_m, small_v)
    shapes = {"conv_a_w": conv_a_w.shape, "pool_w": pool_w.shape, "conv_f_w": conv_f_w.shape, "norm_final_g": norm_final_g.shape}
    small = {}
    for t, name in enumerate(small_names):
        shp = shapes.get(name)
        small[name] = [a if shp is None else a.reshape(shp) for a in (small_g[t], s_delta[t], s_m[t], s_v[t])]

    order = ("norm_mix_g", "w_in", "conv_a_w", "conv_a_b", "ln_a_g", "ln_a_b", "pool_w", "pool_scale", "w_out", "norm_ffn_g",
             "w_up", "conv_f_w", "conv_f_b", "w_down", "norm_final_g")
    table = {**big, **small}
    loss = loss_row[0, 0]
    outs = [loss, grad_x[None]]
    for t in range(4):
        outs += [table[name][t] for name in order]
    return tuple(outs)
```

```python
import functools

import jax
import jax.numpy as jnp
from jax import lax
from jax.experimental import pallas as pl
from jax.experimental.pallas import tpu as pltpu

F32 = jnp.float32
BF16 = jnp.bfloat16
EPS = 1e-6
ADAM_LR = 0.001
ADAM_B1 = 0.9
ADAM_B2 = 0.999
ADAM_EPS = 1e-08
ADAM_WD = 0.01
ADAM_STEP = 10

D_MODEL = 1024
D_CONV = 512
D_POOL = 512
D_IN = 1536
D_FF = 2816
CONV_A = 31
CONV_F = 3
POOL_WINDOWS = (2, 4, 8, 16)
POOL_GROUP = 128
N_CHIPS = 4
FF_CHUNK = 256
N_FF_CHUNKS = D_FF // FF_CHUNK
A_HALO = 32
P_HALO = 16
VMEM_LIMIT = 56 * 1024 * 1024
MESH = pl.DeviceIdType.MESH

ANY = pl.BlockSpec(memory_space=pl.ANY)
VMEM = pl.BlockSpec(memory_space=pltpu.VMEM)


def _dot(a, b):
    return jnp.dot(a, b, preferred_element_type=F32)


def _dot_nt(a, b):
    return lax.dot_general(a, b, (((1,), (1,)), ((), ())), preferred_element_type=F32)


def _dot_tn(a, b):
    return lax.dot_general(a, b, (((0,), (0,)), ((), ())), preferred_element_type=F32)


def _sigmoid(v):
    return jax.nn.sigmoid(v)


def _colsum(v):
    return jnp.sum(v, axis=0, keepdims=True)


def _rowmean(v):
    return jnp.mean(v, axis=-1, keepdims=True)


def _place():
    x, y, c = lax.axis_index("x"), lax.axis_index("y"), lax.axis_index("c")
    chips = [(1 - x, y), (x, 1 - y), (1 - x, 1 - y)]
    return x, y, c, 2 * x + y, chips


def _gather_ops(bufs, fulls, col_sharded, sems):
    ici_send, ici_recv, fwd_send, fwd_recv, loc_sem = sems
    n_big = len(bufs)
    x, y, c, k, chips = _place()

    def block(i, kk, half=None):
        rows, cols = bufs[i].shape
        if col_sharded[i]:
            rs = slice(None) if half is None else pl.ds(pl.multiple_of(half * (rows // 2), 16), rows // 2)
            return fulls[i].at[rs, pl.ds(pl.multiple_of(kk * cols, 128), cols)]
        if half is None:
            return fulls[i].at[pl.ds(pl.multiple_of(kk * rows, 16), rows), :]
        return fulls[i].at[pl.ds(pl.multiple_of(kk * rows + half * (rows // 2), 16), rows // 2), :]

    def my_half(i):
        rows = bufs[i].shape[0]
        return bufs[i].at[pl.ds(pl.multiple_of(c * (rows // 2), 16), rows // 2), :]

    def ici(i, j, kk):
        return pltpu.make_async_remote_copy(
            src_ref=my_half(i), dst_ref=block(i, kk, c), send_sem=ici_send.at[i * 3 + j], recv_sem=ici_recv.at[i * 3 + j],
            device_id=(*chips[j], c), device_id_type=MESH)

    def fwd(i, j, kk, half):
        return pltpu.make_async_remote_copy(
            src_ref=block(i, kk, half), dst_ref=block(i, kk, half), send_sem=fwd_send.at[i * 3 + j],
            recv_sem=fwd_recv.at[i * 3 + j], device_id=(x, y, 1 - c), device_id_type=MESH)

    local = [pltpu.make_async_copy(bufs[i], block(i, k), loc_sem.at[i]) for i in range(n_big)]
    sends = [ici(i, j, k) for i in range(n_big) for j in range(3)]
    peers = [(i, j, 2 * qx + qy) for i in range(n_big) for j, (qx, qy) in enumerate(chips)]

    def start():
        for cp in local + sends:
            cp.start()

    def land():
        for i, j, kq in peers:
            ici(i, j, kq).wait_recv()
            fwd(i, j, kq, c).start()

    def finish():
        for i, j, kq in peers:
            fwd(i, j, kq, 1 - c).wait_recv()
            fwd(i, j, kq, c).wait_send()
        for cp in sends:
            cp.wait_send()
        for cp in local:
            cp.wait()

    return start, land, finish


def _gather_sems(n_big):
    return [pltpu.SemaphoreType.DMA((3 * n_big,))] * 4 + [pltpu.SemaphoreType.DMA((n_big,))]


def _tap_ops(srcs, dsts, sems):
    send, recv, loc = sems
    _, _, c, k, chips = _place()

    def copy(t, j, kk):
        return pltpu.make_async_remote_copy(
            src_ref=srcs[t], dst_ref=dsts[t].at[kk], send_sem=send.at[t * 3 + j], recv_sem=recv.at[t * 3 + j],
            device_id=(*chips[j], c), device_id_type=MESH)

    local = [pltpu.make_async_copy(srcs[t], dsts[t].at[k], loc.at[t]) for t in range(len(srcs))]
    sends = [[copy(t, j, k) for j in range(3)] for t in range(len(srcs))]

    def start():
        for t, cp in enumerate(local):
            cp.start()
            for sd in sends[t]:
                sd.start()

    def wait(t):
        for j, (qx, qy) in enumerate(chips):
            copy(t, j, 2 * qx + qy).wait_recv()
        for sd in sends[t]:
            sd.wait_send()
        local[t].wait()

    return start, wait


def _cast_shards(*shards):
    def body(*refs):
        for src, dst in zip(refs[:len(shards)], refs[len(shards):]):
            dst[...] = src[...].astype(BF16)

    return pl.pallas_call(
        body, name="cast_shards", out_shape=[jax.ShapeDtypeStruct(s.shape, BF16) for s in shards],
        in_specs=[VMEM] * len(shards), out_specs=[VMEM] * len(shards),
        compiler_params=pltpu.CompilerParams(vmem_limit_bytes=VMEM_LIMIT),
    )(*shards)


def _load_weights(pairs, sem):
    cps = [pltpu.make_async_copy(src, dst, sem.at[i]) for i, (src, dst) in enumerate(pairs)]
    for cp in cps:
        cp.start()
    for cp in cps:
        cp.wait()


def _shifted_views(buf, shifted, t_rows):
    n = t_rows + A_HALO - 8
    for b in range(1, 8):
        shifted[b - 1] = buf[b:b + n, :]

    def view(offset):
        a, b = divmod(offset, 8)
        if b == 0:
            return buf[8 * a:8 * a + t_rows, :]
        return shifted[b - 1, 8 * a:8 * a + t_rows, :]

    return view


def _pool_count(tile, t_rows, w):
    row = lax.broadcasted_iota(jnp.int32, (t_rows, POOL_GROUP), 0) + tile * t_rows
    return jnp.minimum(row + 1, w).astype(F32)


def _mixer_fwd(x, g1, win_b, wout_b, wup_b, wa_s, wf_s, cb, lg, lb, pw, ps, tile_rows):
    seq = x.shape[0]
    tr = tile_rows
    n = seq // tr

    def body(x_ref, g1_ref, win_b_hbm, wout_b_hbm, wup_b_hbm, wa_s_hbm, wf_s_hbm, cb_ref, lg_ref, lb_ref, pw_ref,
             ps_ref, h1_ref, proj_ref, c_ref, d_ref, m_ref, x1_ref, win_f, wout_f, wup_f, wa_g, wf_g,
             win_v, wout_v, wa_ref, ubuf, ushift, bbuf, sem, *csems):
        i = pl.program_id(0)
        first_sems, later_sems, tap_sems = csems[0:5], csems[5:10], csems[10:13]

        def first():
            return _gather_ops((win_b_hbm, wout_b_hbm), (win_f, wout_f), (True, False), first_sems)

        def later():
            return _gather_ops((wup_b_hbm,), (wup_f,), (True,), later_sems)

        def taps():
            return _tap_ops((wa_s_hbm, wf_s_hbm), (wa_g, wf_g), tap_sems)

        @pl.when(i == 0)
        def _():
            first()[0]()
            taps()[0]()
            later()[0]()
            first()[1]()
            first()[2]()
            taps()[1](0)
            loads = [(win_f, win_v), (wout_f, wout_v)]
            loads += [(wa_g.at[kk], wa_ref.at[:, kk * (D_CONV // N_CHIPS):(kk + 1) * (D_CONV // N_CHIPS)]) for kk in range(N_CHIPS)]
            _load_weights(loads, sem)
            ubuf[0:A_HALO, :] = jnp.zeros((A_HALO, D_CONV), F32)
            bbuf[0:P_HALO, :] = jnp.zeros((P_HALO, D_POOL), F32)

        xv = x_ref[...]
        r = lax.rsqrt(_rowmean(xv * xv) + EPS)
        h1 = (xv * r * g1_ref[...]).astype(BF16)
        h1_ref[...] = h1
        proj = _dot(h1, win_v[...])
        proj_ref[...] = proj.astype(BF16)
        av, ag, bi = proj[:, :D_CONV], proj[:, D_CONV:2 * D_CONV], proj[:, 2 * D_CONV:]
        ubuf[A_HALO:A_HALO + tr, :] = av * _sigmoid(ag)
        off = A_HALO - (CONV_A - 1)
        uview = _shifted_views(ubuf, ushift, tr)
        acc = wa_ref[0:1, :] * uview(off)
        for j in range(1, CONV_A):
            acc = acc + wa_ref[j:j + 1, :] * uview(off + j)
        cv = acc + cb_ref[...]
        ubuf[0:A_HALO, :] = ubuf[tr:tr + A_HALO, :]
        c_ref[...] = cv.astype(BF16)
        xc = cv - _rowmean(cv)
        z = xc * lax.rsqrt(_rowmean(xc * xc) + EPS)
        ln = z * lg_ref[...] + lb_ref[...]
        ya = ln * _sigmoid(ln)
        bbuf[P_HALO:P_HALO + tr, :] = bi
        ds, ybs = [], []
        for g, w in enumerate(POOL_WINDOWS):
            cols = slice(g * POOL_GROUP, (g + 1) * POOL_GROUP)
            s = bi[:, cols]
            for kk in range(1, w):
                s = s + bbuf[P_HALO - kk:P_HALO - kk + tr, cols]
            dg = s / _pool_count(i, tr, w) - bi[:, cols]
            ds.append(dg)
            ybs.append(_dot(dg.astype(BF16), pw_ref[g].astype(BF16)))
        bbuf[0:P_HALO, :] = bbuf[tr:tr + P_HALO, :]
        d_ref[...] = jnp.concatenate(ds, axis=1).astype(BF16)
        yb = jnp.concatenate(ybs, axis=1) * ps_ref[...]
        m = jnp.concatenate([ya, yb], axis=1).astype(BF16)
        m_ref[...] = m
        x1_ref[...] = xv + _dot(m, wout_v[...])

        @pl.when(i == n - 1)
        def _():
            later()[1]()
            later()[2]()
            taps()[1](1)

    tile = lambda w: pl.BlockSpec((tr, w), lambda i: (i, 0))
    full = lambda a: pl.BlockSpec(a.shape, lambda i: (0,) * a.ndim)
    return pl.pallas_call(
        body, name="mixer_fwd", grid=(n,),
        in_specs=[tile(D_MODEL), full(g1)] + [ANY] * 5 + [full(cb), full(lg), full(lb), full(pw), full(ps)],
        out_specs=[tile(D_MODEL), tile(D_IN), tile(D_CONV), tile(D_POOL), tile(D_MODEL), tile(D_MODEL)] + [ANY] * 5,
        out_shape=[
            jax.ShapeDtypeStruct((seq, D_MODEL), BF16), jax.ShapeDtypeStruct((seq, D_IN), BF16),
            jax.ShapeDtypeStruct((seq, D_CONV), BF16), jax.ShapeDtypeStruct((seq, D_POOL), BF16),
            jax.ShapeDtypeStruct((seq, D_MODEL), BF16), jax.ShapeDtypeStruct((seq, D_MODEL), F32),
            jax.ShapeDtypeStruct((D_MODEL, D_IN), BF16), jax.ShapeDtypeStruct((D_MODEL, D_MODEL), BF16),
            jax.ShapeDtypeStruct((D_MODEL, 2 * D_FF), BF16),
            jax.ShapeDtypeStruct((N_CHIPS,) + wa_s.shape, F32), jax.ShapeDtypeStruct((N_CHIPS,) + wf_s.shape, F32),
        ],
        scratch_shapes=[
            pltpu.VMEM((D_MODEL, D_IN), BF16), pltpu.VMEM((D_MODEL, D_MODEL), BF16), pltpu.VMEM((32, D_CONV), F32),
            pltpu.VMEM((tr + A_HALO, D_CONV), F32), pltpu.VMEM((7, tr + A_HALO - 8, D_CONV), F32),
            pltpu.VMEM((tr + P_HALO, D_POOL), F32), pltpu.SemaphoreType.DMA((2 + N_CHIPS,)),
        ] + _gather_sems(2) + _gather_sems(1) + [
            pltpu.SemaphoreType.DMA((6,)), pltpu.SemaphoreType.DMA((6,)), pltpu.SemaphoreType.DMA((2,))],
        compiler_params=pltpu.CompilerParams(dimension_semantics=("arbitrary",), vmem_limit_bytes=VMEM_LIMIT),
    )(x, g1, win_b, wout_b, wup_b, wa_s, wf_s, cb, lg, lb, pw, ps)


def _ffn_up(x1, g2, wup, wf, fb, wdown_b, tile_rows):
    seq = x1.shape[0]
    tr = tile_rows
    n = seq // tr

    def body(x1_ref, g2_ref, wup_hbm, wf_ref, fb_ref, wdown_b_hbm,
             h2_ref, up_ref, gc_ref, act_ref, wdown_f, wup_v, gbuf, sem, *gsems):
        i = pl.program_id(0)

        def gather():
            return _gather_ops((wdown_b_hbm,), (wdown_f,), (False,), gsems)

        @pl.when(i == 0)
        def _():
            gather()[0]()
            _load_weights(((wup_hbm, wup_v),), sem)
            gbuf[0:8, :] = jnp.zeros((8, D_FF), F32)

        x1v = x1_ref[...]
        r2 = lax.rsqrt(_rowmean(x1v * x1v) + EPS)
        h2 = (x1v * r2 * g2_ref[...]).astype(BF16)
        h2_ref[...] = h2

        def up_proj(j):
            return (_dot(h2, wup_v[:, j * FF_CHUNK:(j + 1) * FF_CHUNK]),
                    _dot(h2, wup_v[:, D_FF + j * FF_CHUNK:D_FF + (j + 1) * FF_CHUNK]))

        ahead = up_proj(0)
        for j in range(N_FF_CHUNKS):
            cs = slice(j * FF_CHUNK, (j + 1) * FF_CHUNK)
            vs = slice(D_FF + j * FF_CHUNK, D_FF + (j + 1) * FF_CHUNK)
            gate, val = ahead
            if j + 1 < N_FF_CHUNKS:
                ahead = up_proj(j + 1)
            up_ref[:, cs] = gate.astype(BF16)
            up_ref[:, vs] = val.astype(BF16)
            gbuf[8:8 + tr, cs] = gate
            gc = (wf_ref[0:1, cs] * gbuf[6:6 + tr, cs] + wf_ref[1:2, cs] * gbuf[7:7 + tr, cs]
                  + wf_ref[2:3, cs] * gate + fb_ref[:, cs])
            gbuf[0:8, cs] = gbuf[tr:tr + 8, cs]
            gc_ref[:, cs] = gc.astype(BF16)
            act_ref[:, cs] = (gc * _sigmoid(gc) * val).astype(BF16)

        @pl.when(i == max(n - 2, 0))
        def _():
            gather()[1]()

        @pl.when(i == n - 1)
        def _():
            gather()[2]()

    tile = lambda w: pl.BlockSpec((tr, w), lambda i: (i, 0))
    full = lambda a: pl.BlockSpec(a.shape, lambda i: (0,) * a.ndim)
    return pl.pallas_call(
        body, name="ffn_up", grid=(n,),
        in_specs=[tile(D_MODEL), full(g2), ANY, full(wf), full(fb), ANY],
        out_specs=[tile(D_MODEL), tile(2 * D_FF), tile(D_FF), tile(D_FF), ANY],
        out_shape=[
            jax.ShapeDtypeStruct((seq, D_MODEL), BF16), jax.ShapeDtypeStruct((seq, 2 * D_FF), BF16),
            jax.ShapeDtypeStruct((seq, D_FF), BF16), jax.ShapeDtypeStruct((seq, D_FF), BF16),
            jax.ShapeDtypeStruct((D_FF, D_MODEL), BF16),
        ],
        scratch_shapes=[pltpu.VMEM(wup.shape, BF16), pltpu.VMEM((tr + 8, D_FF), F32), pltpu.SemaphoreType.DMA((1,))]
        + _gather_sems(1),
        compiler_params=pltpu.CompilerParams(dimension_semantics=("arbitrary",), vmem_limit_bytes=VMEM_LIMIT),
    )(x1, g2, wup, wf, fb, wdown_b)


def _ffn_down(x1, act, wdown, g3, target, tile_rows):
    seq = x1.shape[0]
    tr = tile_rows
    n = seq // tr

    def body(x1_ref, act_ref, wdown_hbm, g3_ref, t_ref, dx2_ref, dx2b_ref, sm_ref, wdown_v, sem):
        i = pl.program_id(0)

        @pl.when(i == 0)
        def _():
            _load_weights(((wdown_hbm, wdown_v),), sem)
            sm_ref[...] = jnp.zeros(sm_ref.shape, F32)

        x2 = x1_ref[...] + _dot(act_ref[...], wdown_v[...])
        r3 = lax.rsqrt(_rowmean(x2 * x2) + EPS)
        n3 = x2 * r3
        err = n3 * g3_ref[...] - t_ref[...]
        dy = err / D_MODEL
        sm_ref[2:3, :] += _colsum(dy * n3)
        loss = 0.5 * _colsum(_rowmean(err * err))
        sm_ref[3:4, :] += jnp.broadcast_to(loss, (1, D_MODEL))
        dn = dy * g3_ref[...]
        dx2v = r3 * (dn - n3 * _rowmean(dn * n3))
        dx2_ref[...] = dx2v
        dx2b_ref[...] = dx2v.astype(BF16)

    tile = lambda w: pl.BlockSpec((tr, w), lambda i: (i, 0))
    full = lambda a: pl.BlockSpec(a.shape, lambda i: (0,) * a.ndim)
    return pl.pallas_call(
        body, name="ffn_down", grid=(n,),
        in_specs=[tile(D_MODEL), tile(D_FF), ANY, full(g3), tile(D_MODEL)],
        out_specs=[tile(D_MODEL), tile(D_MODEL), pl.BlockSpec((8, D_MODEL), lambda i: (0, 0))],
        out_shape=[
            jax.ShapeDtypeStruct((seq, D_MODEL), F32), jax.ShapeDtypeStruct((seq, D_MODEL), BF16),
            jax.ShapeDtypeStruct((8, D_MODEL), F32),
        ],
        scratch_shapes=[pltpu.VMEM(wdown.shape, BF16), pltpu.SemaphoreType.DMA((1,))],
        compiler_params=pltpu.CompilerParams(dimension_semantics=("arbitrary",), vmem_limit_bytes=VMEM_LIMIT),
    )(x1, act, wdown, g3, target)


def _ffn_bwd(dx2, up, gcs, x1, g2, wup, wf, wdown, comm, tile_rows):
    seq = x1.shape[0]
    c_ins, c_shapes, c_sems, c_ops = _comm_plan(comm)
    nc = len(c_ins)
    tr = tile_rows
    n = seq // tr

    def body(dx2_ref, up_ref, gc_ref, x1_ref, g2_ref, wup_hbm, wf_ref, wdown_hbm, *rest):
        c_in, rest = rest[:nc], rest[nc:]
        dup_ref, dx1_ref, dx1b_ref, sm_ref, sf_ref = rest[:5]
        c_out, rest = rest[5:5 + nc], rest[5 + nc:]
        wup_v, wdown_v, dbuf, dcar, sem = rest[:5]
        c_sem_refs = rest[5:]
        i = pl.program_id(0)

        @pl.when(i == 0)
        def _():
            c_ops(c_in, c_out, c_sem_refs)[0]()
            _load_weights(((wup_hbm, wup_v), (wdown_hbm, wdown_v)), sem)
            dcar[...] = jnp.zeros(dcar.shape, F32)
            sm_ref[...] = jnp.zeros(sm_ref.shape, F32)
            sf_ref[...] = jnp.zeros(sf_ref.shape, F32)

        dx2v = dx2_ref[...]
        dx2b = dx2v.astype(BF16)
        dh2 = jnp.zeros((tr, D_MODEL), F32)

        def down_t(j):
            return _dot_nt(dx2b, wdown_v[j * FF_CHUNK:(j + 1) * FF_CHUNK, :])

        ahead = down_t(0)
        for j in range(N_FF_CHUNKS):
            cs = slice(j * FF_CHUNK, (j + 1) * FF_CHUNK)
            vs = slice(D_FF + j * FF_CHUNK, D_FF + (j + 1) * FF_CHUNK)
            dact = ahead
            if j + 1 < N_FF_CHUNKS:
                ahead = down_t(j + 1)
            gate = up_ref[:, cs].astype(F32)
            val = up_ref[:, vs].astype(F32)
            gc = gc_ref[:, cs].astype(F32)
            sg = _sigmoid(gc)
            dval = dact * (gc * sg)
            dgc = dact * val * (sg * (1.0 + gc * (1.0 - sg)))
            dbuf[0:tr, :] = dgc
            dbuf[tr:tr + 8, :] = dcar[:, cs]
            d_p1 = dbuf[1:1 + tr, :]
            d_p2 = dbuf[2:2 + tr, :]
            dgate = wf_ref[2:3, cs] * dgc + wf_ref[1:2, cs] * d_p1 + wf_ref[0:1, cs] * d_p2
            dcar[:, cs] = dgc[0:8, :]
            sf_ref[0:1, cs] += _colsum(d_p2 * gate)
            sf_ref[1:2, cs] += _colsum(d_p1 * gate)
            sf_ref[2:3, cs] += _colsum(dgc * gate)
            sf_ref[3:4, cs] += _colsum(dgc)
            dgb, dvb = dgate.astype(BF16), dval.astype(BF16)
            dup_ref[:, cs] = dgb
            dup_ref[:, vs] = dvb
            dh2 = dh2 + _dot_nt(dgb, wup_v[:, cs]) + _dot_nt(dvb, wup_v[:, vs])
        x1v = x1_ref[...]
        r2 = lax.rsqrt(_rowmean(x1v * x1v) + EPS)
        n2 = x1v * r2
        sm_ref[1:2, :] += _colsum(dh2 * n2)
        dn2 = dh2 * g2_ref[...]
        dx1v = dx2v + r2 * (dn2 - n2 * _rowmean(dn2 * n2))
        dx1_ref[...] = dx1v
        dx1b_ref[...] = dx1v.astype(BF16)

        @pl.when(i == n - 1)
        def _():
            c_ops(c_in, c_out, c_sem_refs)[2]()

    tile = lambda w: pl.BlockSpec((tr, w), lambda i: (n - 1 - i, 0))
    full = lambda a: pl.BlockSpec(a.shape, lambda i: (0,) * a.ndim)
    acc = lambda rows, w: pl.BlockSpec((rows, w), lambda i: (0, 0))
    return pl.pallas_call(
        body, name="ffn_bwd", grid=(n,),
        in_specs=[tile(D_MODEL), tile(2 * D_FF), tile(D_FF), tile(D_MODEL), full(g2), ANY, full(wf), ANY] + [ANY] * nc,
        out_specs=[tile(2 * D_FF), tile(D_MODEL), tile(D_MODEL), acc(8, D_MODEL), acc(8, D_FF)] + [ANY] * nc,
        out_shape=[
            jax.ShapeDtypeStruct((seq, 2 * D_FF), BF16), jax.ShapeDtypeStruct((seq, D_MODEL), F32),
            jax.ShapeDtypeStruct((seq, D_MODEL), BF16), jax.ShapeDtypeStruct((8, D_MODEL), F32),
            jax.ShapeDtypeStruct((8, D_FF), F32),
        ] + c_shapes,
        scratch_shapes=[
            pltpu.VMEM(wup.shape, BF16), pltpu.VMEM(wdown.shape, BF16),
            pltpu.VMEM((tr + 8, FF_CHUNK), F32), pltpu.VMEM((8, D_FF), F32), pltpu.SemaphoreType.DMA((2,)),
        ] + c_sems,
        compiler_params=pltpu.CompilerParams(dimension_semantics=("arbitrary",), vmem_limit_bytes=VMEM_LIMIT),
    )(dx2, up, gcs, x1, g2, wup, wf, wdown, *c_ins)


def _mixer_bwd(dx1, x, proj, cpre, d, g1, win, wa, lg, lb, pw, ps, wout, parts, tile_rows):
    seq = x.shape[0]
    n_parts = len(parts)
    tr = tile_rows
    n = seq // tr
    row_cb, row_lg, row_lb, row_ps = 32, 33, 34, 35

    def body(dx1_ref, x_ref, proj_ref, projh_ref, c_ref, d_ref, g1_ref, win_hbm, wa_ref, lg_ref, lb_ref, pw_ref, ps_ref,
             wout_hbm, *rest):
        part_refs, rest = rest[:n_parts], rest[n_parts:]
        dproj_ref, gx_ref, sm_ref, s5_ref, sp_ref = rest[:5]
        land_refs, rest = rest[5:5 + n_parts], rest[5 + n_parts:]
        win_v, wout_v, ubuf, ushift, dcbuf, dshift, ebuf, sem = rest[:8]
        ssems = rest[8:]
        i = pl.program_id(0)
        tile = n - 1 - i

        def scatter():
            return _scatter_ops(part_refs, land_refs, n_parts, ssems)

        @pl.when(i == 0)
        def _():
            scatter()[0]()
            _load_weights(((win_hbm, win_v), (wout_hbm, wout_v)), sem)
            dcbuf[tr:tr + A_HALO, :] = jnp.zeros((A_HALO, D_CONV), F32)
            ebuf[tr:tr + P_HALO, :] = jnp.zeros((P_HALO, D_POOL), F32)
            sm_ref[...] = jnp.zeros(sm_ref.shape, F32)
            s5_ref[...] = jnp.zeros(s5_ref.shape, F32)
            sp_ref[...] = jnp.zeros(sp_ref.shape, F32)

        dx1v = dx1_ref[...]
        dm = _dot_nt(dx1v.astype(BF16), wout_v[...])
        dya, dyb = dm[:, :D_CONV], dm[:, D_CONV:]
        dbis = []
        for g, w in enumerate(POOL_WINDOWS):
            cols = slice(g * POOL_GROUP, (g + 1) * POOL_GROUP)
            dgb = d_ref[:, cols]
            pwb = pw_ref[g].astype(BF16)
            dyg = dyb[:, cols]
            s5_ref[row_ps:row_ps + 1, cols] += _colsum(dyg * _dot(dgb, pwb))
            dqb = (dyg * ps_ref[:, cols]).astype(BF16)
            sp_ref[g] += _dot_tn(dgb, dqb)
            dd = _dot_nt(dqb, pwb)
            e = dd / _pool_count(tile, tr, w)
            ebuf[0:tr, cols] = e
            s = e
            for kk in range(1, w):
                s = s + ebuf[kk:kk + tr, cols]
            dbis.append(s - dd)
        ebuf[tr:tr + P_HALO, :] = ebuf[0:P_HALO, :]
        cv = c_ref[...].astype(F32)
        xc = cv - _rowmean(cv)
        rs = lax.rsqrt(_rowmean(xc * xc) + EPS)
        z = xc * rs
        ln = z * lg_ref[...] + lb_ref[...]
        sl = _sigmoid(ln)
        dl = dya * (sl * (1.0 + ln * (1.0 - sl)))
        s5_ref[row_lg:row_lg + 1, :] += _colsum(dl * z)
        s5_ref[row_lb:row_lb + 1, :] += _colsum(dl)
        dz = dl * lg_ref[...]
        dc = rs * (dz - _rowmean(dz) - z * _rowmean(dz * z))
        s5_ref[row_cb:row_cb + 1, :] += _colsum(dc)
        dcbuf[0:tr, :] = dc
        keep = (tile > 0).astype(F32)
        avh = projh_ref[:, :D_CONV].astype(F32)
        agh = projh_ref[:, D_CONV:].astype(F32)
        ubuf[0:A_HALO, :] = avh * _sigmoid(agh) * keep
        av = proj_ref[:, :D_CONV].astype(F32)
        ag = proj_ref[:, D_CONV:2 * D_CONV].astype(F32)
        sg = _sigmoid(ag)
        ubuf[A_HALO:A_HALO + tr, :] = av * sg
        off = A_HALO - (CONV_A - 1)
        du = wa_ref[CONV_A - 1:CONV_A, :] * dc
        dview = _shifted_views(dcbuf, dshift, tr)
        uview = _shifted_views(ubuf, ushift, tr)
        for j in range(CONV_A - 1):
            du = du + wa_ref[j:j + 1, :] * dview(CONV_A - 1 - j)
        for j in range(CONV_A):
            s5_ref[j:j + 1, :] += _colsum(dc * uview(off + j))
        dcbuf[tr:tr + A_HALO, :] = dcbuf[0:A_HALO, :]
        dav = du * sg
        dag = du * av * (sg * (1.0 - sg))
        dprojb = jnp.concatenate([dav, dag] + dbis, axis=1).astype(BF16)
        dproj_ref[...] = dprojb
        dh1 = _dot_nt(dprojb, win_v[...])
        xv = x_ref[...]
        r1 = lax.rsqrt(_rowmean(xv * xv) + EPS)
        n1 = xv * r1
        sm_ref[0:1, :] += _colsum(dh1 * n1)
        dn1 = dh1 * g1_ref[...]
        gx_ref[...] = dx1v + r1 * (dn1 - n1 * _rowmean(dn1 * n1))

        @pl.when(i == max(n - 2, 0))
        def _():
            scatter()[1]()

        @pl.when(i == n - 1)
        def _():
            scatter()[2]()

    tile = lambda w: pl.BlockSpec((tr, w), lambda i: (n - 1 - i, 0))
    full = lambda a: pl.BlockSpec(a.shape, lambda i: (0,) * a.ndim)
    halo = pl.BlockSpec((A_HALO, 2 * D_CONV), lambda i: (jnp.maximum((n - 1 - i) * (tr // A_HALO) - 1, 0), 0))
    acc = lambda shape: pl.BlockSpec(shape, lambda i: (0,) * len(shape))
    return pl.pallas_call(
        body, name="mixer_bwd", grid=(n,),
        in_specs=[tile(D_MODEL), tile(D_MODEL), tile(D_IN), halo, tile(D_CONV), tile(D_POOL), full(g1), ANY, full(wa),
                  full(lg), full(lb), full(pw), full(ps), ANY] + [ANY] * n_parts,
        out_specs=[tile(D_IN), tile(D_MODEL), acc((8, D_MODEL)), acc((40, D_CONV)), acc(pw.shape)] + [ANY] * n_parts,
        out_shape=[
            jax.ShapeDtypeStruct((seq, D_IN), BF16), jax.ShapeDtypeStruct((seq, D_MODEL), F32),
            jax.ShapeDtypeStruct((8, D_MODEL), F32), jax.ShapeDtypeStruct((40, D_CONV), F32),
            jax.ShapeDtypeStruct(pw.shape, F32),
        ] + _scatter_shapes(parts, ()),
        scratch_shapes=[
            pltpu.VMEM(win.shape, BF16), pltpu.VMEM(wout.shape, BF16),
            pltpu.VMEM((tr + A_HALO, D_CONV), F32), pltpu.VMEM((7, tr + A_HALO - 8, D_CONV), F32),
            pltpu.VMEM((tr + A_HALO, D_CONV), F32), pltpu.VMEM((7, tr + A_HALO - 8, D_CONV), F32),
            pltpu.VMEM((tr + P_HALO, D_POOL), F32), pltpu.SemaphoreType.DMA((2,)),
        ] + _scatter_sems(n_parts),
        compiler_params=pltpu.CompilerParams(dimension_semantics=("arbitrary",), vmem_limit_bytes=VMEM_LIMIT),
    )(dx1, x, proj, proj, cpre, d, g1, win, wa, lg, lb, pw, ps, wout, *parts)


def _weight_grad(a, b, layout, k_rows, comm=None, carry=None):
    seq, m_dim = a.shape
    n_dim = b.shape[1]
    steps = seq // k_rows

    def store(o_ref, acc, index, value):
        if steps == 1:
            o_ref[index] = value.astype(BF16)
            return
        s = pl.program_id(1)

        @pl.when(s == 0)
        def _():
            acc[index] = value

        @pl.when(jnp.logical_and(s > 0, s < steps - 1))
        def _():
            acc[index] += value

        @pl.when(s == steps - 1)
        def _():
            o_ref[index] = (acc[index] + value).astype(BF16)

    if layout in ("rows1", "rows2"):
        groups = int(layout[-1])
        per_tile = N_CHIPS // groups
        rows = m_dim // N_CHIPS // 2
        a_w = m_dim // groups

        def body(a_ref, b_ref, o_ref, acc):
            r = _dot_tn(a_ref[...], b_ref[...])
            for p in range(per_tile):
                for h in range(2):
                    store(o_ref, acc, (h, p), r[(2 * p + h) * rows:(2 * p + h + 1) * rows, :])

        in_specs = [pl.BlockSpec((k_rows, a_w), lambda g, s: (s, g)), pl.BlockSpec((k_rows, n_dim), lambda g, s: (s, 0))]
        out_spec = pl.BlockSpec((2, per_tile, rows, n_dim), lambda g, s: (0, g, 0, 0))
        out_dims, acc_dims = (2, N_CHIPS, rows, n_dim), (2, per_tile, rows, n_dim)
    elif layout == "cols_chip":
        groups = N_CHIPS
        rows, cols = m_dim // 2, n_dim // N_CHIPS

        def body(a_ref, b_ref, o_ref, acc):
            r = _dot_tn(a_ref[...], b_ref[...])
            for h in range(2):
                store(o_ref, acc, h, r[h * rows:(h + 1) * rows, :])

        in_specs = [pl.BlockSpec((k_rows, m_dim), lambda g, s: (s, 0)), pl.BlockSpec((k_rows, cols), lambda g, s: (s, g))]
        out_spec = pl.BlockSpec((2, None, rows, cols), lambda g, s: (0, g, 0, 0))
        out_dims, acc_dims = (2, N_CHIPS, rows, cols), (2, rows, cols)
    else:
        groups = 2
        rows, cols = m_dim // 2, n_dim // N_CHIPS

        def body(a_ref, b_ref, o_ref, acc):
            r = _dot_tn(a_ref[...], b_ref[...])
            for k in range(N_CHIPS):
                store(o_ref, acc, k, r[:, k * cols:(k + 1) * cols])

        in_specs = [pl.BlockSpec((k_rows, rows), lambda g, s: (s, g)), pl.BlockSpec((k_rows, n_dim), lambda g, s: (s, 0))]
        out_spec = pl.BlockSpec((None, N_CHIPS, rows, cols), lambda g, s: (g, 0, 0, 0))
        out_dims, acc_dims = (2, N_CHIPS, rows, cols), (N_CHIPS, rows, cols)

    c_ins, c_shapes, c_sems, c_ops = _comm_plan(comm)
    if carry is not None:
        assert comm is None
        c_ins, c_shapes, c_sems = (carry,), [jax.ShapeDtypeStruct(carry.shape, carry.dtype)], [pltpu.SemaphoreType.DMA((1,))]

        def c_ops(i, o, sm):
            cp = pltpu.make_async_copy(i[0], o[0], sm[0].at[0])
            return cp.start, lambda: None, cp.wait
    nc = len(c_ins)

    def hosted(a_ref, b_ref, *rest):
        c_in, o_ref, c_out, acc, sems = rest[:nc], rest[nc], rest[nc + 1:2 * nc + 1], rest[2 * nc + 1], rest[2 * nc + 2:]
        g, s = pl.program_id(0), pl.program_id(1)
        if nc:
            @pl.when(jnp.logical_and(g == 0, s == 0))
            def _():
                c_ops(c_in, c_out, sems)[0]()

        body(a_ref, b_ref, o_ref, acc)
        if nc:
            step = g * steps + s

            @pl.when(step == max(groups * steps - 2, 0))
            def _():
                c_ops(c_in, c_out, sems)[1]()

            @pl.when(step == groups * steps - 1)
            def _():
                c_ops(c_in, c_out, sems)[2]()

    outs = pl.pallas_call(
        hosted, name=f"weight_grad_{layout}_{m_dim}x{n_dim}", grid=(groups, steps),
        in_specs=in_specs + [ANY] * nc, out_specs=[out_spec] + [ANY] * nc,
        out_shape=[jax.ShapeDtypeStruct(out_dims, BF16)] + c_shapes,
        scratch_shapes=[pltpu.VMEM(acc_dims, F32)] + c_sems,
        compiler_params=pltpu.CompilerParams(dimension_semantics=("arbitrary", "arbitrary"), vmem_limit_bytes=VMEM_LIMIT),
    )(a, b, *c_ins)
    return outs if nc else outs[0]


def _exchange_ops(ins, outs, n_big, sems):
    send, recv = sems
    x, y, c, _, _ = _place()
    cps = [pltpu.make_async_remote_copy(
        src_ref=ins[t].at[1 - c] if t < n_big else ins[t], dst_ref=outs[t], send_sem=send.at[t], recv_sem=recv.at[t],
        device_id=(x, y, 1 - c), device_id_type=MESH) for t in range(len(ins))]

    def start():
        for cp in cps:
            cp.start()

    def finish():
        for cp in cps:
            cp.wait()

    return start, finish


def _exchange_shapes(bigs, smalls):
    return [jax.ShapeDtypeStruct((N_CHIPS,) + b.shape[2:], b.dtype) for b in bigs] + [
        jax.ShapeDtypeStruct(s.shape, s.dtype) for s in smalls]


def _comm_plan(comm):
    if comm is None:
        return (), [], [], None
    kind, arrays = comm
    n = len(arrays)
    if kind == "scatter":
        return tuple(arrays), _scatter_shapes(arrays, ()), _scatter_sems(n), lambda i, o, sm: _scatter_ops(i, o, n, sm)
    def exchange(i, o, sm):
        start, finish = _exchange_ops(i, o, n, sm)
        return start, lambda: None, finish

    return tuple(arrays), _exchange_shapes(arrays, ()), [pltpu.SemaphoreType.DMA((n,))] * 2, exchange


def _sibling_exchange(bigs, smalls, tag):
    nb, nt = len(bigs), len(bigs) + len(smalls)

    def body(*refs):
        start, finish = _exchange_ops(refs[:nt], refs[nt:2 * nt], nb, refs[2 * nt:])
        start()
        finish()

    return pl.pallas_call(
        body, name=f"sibling_exchange_{tag}", out_shape=_exchange_shapes(bigs, smalls),
        in_specs=[ANY] * nt, out_specs=[ANY] * nt,
        scratch_shapes=[pltpu.SemaphoreType.DMA((nt,)), pltpu.SemaphoreType.DMA((nt,))],
    )(*bigs, *smalls)


def _pair_sum(core, mine, theirs, tag, block_rows):
    _, _, rows, cols = mine.shape
    steps = rows // block_rows

    def body(core_ref, a_ref, b_ref, o_ref):
        o_ref[...] = (a_ref[...].astype(F32) + b_ref[...].astype(F32)).astype(BF16)

    grid_spec = pltpu.PrefetchScalarGridSpec(
        num_scalar_prefetch=1, grid=(N_CHIPS, steps),
        in_specs=[pl.BlockSpec((None, None, block_rows, cols), lambda k, r, core_ref: (core_ref[0], k, r, 0)),
                  pl.BlockSpec((None, block_rows, cols), lambda k, r, core_ref: (k, r, 0))],
        out_specs=pl.BlockSpec((None, block_rows, cols), lambda k, r, core_ref: (k, r, 0)),
    )
    return pl.pallas_call(
        body, name=f"pair_sum_{tag}", grid_spec=grid_spec,
        out_shape=jax.ShapeDtypeStruct((N_CHIPS, rows, cols), BF16),
        compiler_params=pltpu.CompilerParams(dimension_semantics=("arbitrary", "arbitrary"), vmem_limit_bytes=VMEM_LIMIT),
    )(core, mine, theirs)


def _pair_sum_small(mine, theirs):
    (m_f2, m_b1, m_b2, m_sf, m_s5, m_sp) = mine

    def body(a0, a1, a2, a3, a4, a5, b0, b1, b2, b3, b4, b5, o_m, o_f, o_5, o_p):
        sm = (a0[...] + a1[...] + a2[...]) + (b0[...] + b1[...] + b2[...])
        sf = a3[...] + b3[...]
        s5 = a4[...] + b4[...]
        for h in range(2):
            o_m[h] = sm[:, h * (D_MODEL // 2):(h + 1) * (D_MODEL // 2)]
            o_f[h] = sf[:, h * (D_FF // 2):(h + 1) * (D_FF // 2)]
            o_5[h] = s5[:, h * (D_CONV // 2):(h + 1) * (D_CONV // 2)]
            for g in range(2):
                o_p[h, g] = a5[2 * h + g] + b5[2 * h + g]

    out_shape = [
        jax.ShapeDtypeStruct((2, 8, D_MODEL // 2), F32), jax.ShapeDtypeStruct((2, 8, D_FF // 2), F32),
        jax.ShapeDtypeStruct((2, 40, D_CONV // 2), F32), jax.ShapeDtypeStruct((2, 2, POOL_GROUP, POOL_GROUP), F32),
    ]
    return pl.pallas_call(body, name="pair_sum_small", out_shape=out_shape, in_specs=[VMEM] * 12, out_specs=[VMEM] * 4)(
        *mine, *theirs)


def _scatter_ops(ins, outs, n_parts, sems):
    ici_send, ici_recv, fwd_send, fwd_recv, loc_sem = sems
    nt = len(ins)
    x, y, c, k, chips = _place()

    def src_of(t, kk):
        return ins[t].at[kk] if t < n_parts else ins[t].at[c]

    def ici(t, j, kk, slot):
        return pltpu.make_async_remote_copy(
            src_ref=src_of(t, kk), dst_ref=outs[t].at[c, slot], send_sem=ici_send.at[t * 3 + j],
            recv_sem=ici_recv.at[t * 3 + j], device_id=(*chips[j], c), device_id_type=MESH)

    def fwd(t, half):
        slots = outs[t].at[half]
        return pltpu.make_async_remote_copy(
            src_ref=slots, dst_ref=slots, send_sem=fwd_send.at[t], recv_sem=fwd_recv.at[t],
            device_id=(x, y, 1 - c), device_id_type=MESH)

    local = [pltpu.make_async_copy(src_of(t, k), outs[t].at[c, k], loc_sem.at[t]) for t in range(nt)]
    peers = [(t, j, 2 * qx + qy) for t in range(nt) for j, (qx, qy) in enumerate(chips)]
    sends = [ici(t, j, kq, k) for t, j, kq in peers]

    def start():
        for cp in local + sends:
            cp.start()

    def land():
        for t, j, kq in peers:
            ici(t, j, kq, kq).wait_recv()
        for cp in local:
            cp.wait()
        for t in range(nt):
            fwd(t, c).start()

    def finish():
        for t in range(nt):
            fwd(t, 1 - c).wait_recv()
            fwd(t, c).wait_send()
        for cp in sends:
            cp.wait_send()

    return start, land, finish


def _scatter_sems(nt):
    return [pltpu.SemaphoreType.DMA((3 * nt,))] * 2 + [pltpu.SemaphoreType.DMA((nt,))] * 3


def _scatter_shapes(parts, smalls):
    return [jax.ShapeDtypeStruct((2, N_CHIPS) + p.shape[1:], p.dtype) for p in tuple(parts) + tuple(smalls)]


def _chip_scatter(parts, smalls):
    nt = len(parts) + len(smalls)

    def body(*refs):
        start, land, finish = _scatter_ops(refs[:nt], refs[nt:2 * nt], len(parts), refs[2 * nt:])
        start()
        land()
        finish()

    return pl.pallas_call(
        body, name="chip_scatter", out_shape=_scatter_shapes(parts, smalls), in_specs=[ANY] * nt, out_specs=[ANY] * nt,
        scratch_shapes=_scatter_sems(nt),
    )(*parts, *smalls)


def _adamw(w, g, m, v):
    m = ADAM_B1 * m + (1.0 - ADAM_B1) * g
    v = ADAM_B2 * v + (1.0 - ADAM_B2) * (g * g)
    m_hat = m / (1.0 - ADAM_B1 ** ADAM_STEP)
    v_hat = v / (1.0 - ADAM_B2 ** ADAM_STEP)
    delta = -ADAM_LR * (m_hat / (jnp.sqrt(v_hat) + ADAM_EPS) + ADAM_WD * w)
    return delta, m, v


def _adam_big(parts, w, m, v, tag, block_rows):
    _, _, rows, cols = parts.shape
    steps = rows // block_rows

    def body(p_ref, w_ref, m_ref, v_ref, g_out, d_out, m_out, v_out):
        g = p_ref[0].astype(F32)
        for q in range(1, N_CHIPS):
            g = g + p_ref[q].astype(F32)
        delta, m_new, v_new = _adamw(w_ref[...], g, m_ref[...], v_ref[...])
        g_out[...] = g
        d_out[...] = delta
        m_out[...] = m_new
        v_out[...] = v_new

    blk = pl.BlockSpec((block_rows, cols), lambda h, r: (h * steps + r, 0))
    return pl.pallas_call(
        body, name=f"adam_{tag}", grid=(2, steps),
        in_specs=[pl.BlockSpec((None, N_CHIPS, block_rows, cols), lambda h, r: (h, 0, r, 0)), blk, blk, blk],
        out_specs=[blk] * 4, out_shape=[jax.ShapeDtypeStruct(w.shape, F32)] * 4,
        compiler_params=pltpu.CompilerParams(dimension_semantics=("arbitrary", "arbitrary"), vmem_limit_bytes=VMEM_LIMIT),
    )(parts, w, m, v)


def _reduce_small(l_m, l_f, l_5, l_p):
    def total(ref):
        t = ref[:, 0]
        for q in range(1, N_CHIPS):
            t = t + ref[:, q]
        return t

    def body(m_ref, f_ref, s_ref, p_ref, g1_o, g2_o, g3_o, loss_o, wf_o, fb_o, wa_o, cb_o, lg_o, lb_o, ps_o, pw_o):
        tm, tf, t5, tp = total(m_ref), total(f_ref), total(s_ref), total(p_ref)
        sm = jnp.concatenate([tm[0], tm[1]], axis=1)
        sf = jnp.concatenate([tf[0], tf[1]], axis=1)
        s5 = jnp.concatenate([t5[0], t5[1]], axis=1)
        g1_o[...] = sm[0:1]
        g2_o[...] = sm[1:2]
        g3_o[...] = sm[2:3]
        loss_o[...] = sm[3:4, 0:128]
        wf_o[...] = sf
        fb_o[...] = sf[3:4]
        wa_o[...] = s5[0:32]
        cb_o[...] = s5[32:33]
        lg_o[...] = s5[33:34]
        lb_o[...] = s5[34:35]
        ps_o[...] = s5[35:36]
        for h in range(2):
            for g in range(2):
                pw_o[2 * h + g] = tp[h, g]

    row = lambda w: jax.ShapeDtypeStruct((1, w), F32)
    out_shape = [row(D_MODEL), row(D_MODEL), row(D_MODEL), row(128), jax.ShapeDtypeStruct((8, D_FF), F32), row(D_FF),
                 jax.ShapeDtypeStruct((32, D_CONV), F32), row(D_CONV), row(D_CONV), row(D_CONV), row(D_POOL),
                 jax.ShapeDtypeStruct((4, POOL_GROUP, POOL_GROUP), F32)]
    return pl.pallas_call(body, name="reduce_small", out_shape=out_shape, in_specs=[VMEM] * 4, out_specs=[VMEM] * 12)(
        l_m, l_f, l_5, l_p)


def _adam_small(ws, gs, ms, vs):
    count = len(ws)

    def body(*refs):
        w_r, g_r, m_r, v_r = (refs[t * count:(t + 1) * count] for t in range(4))
        d_o, m_o, v_o = (refs[(4 + t) * count:(5 + t) * count] for t in range(3))
        for t in range(count):
            delta, m_new, v_new = _adamw(w_r[t][...], g_r[t][...], m_r[t][...], v_r[t][...])
            d_o[t][...] = delta
            m_o[t][...] = m_new
            v_o[t][...] = v_new

    out_shape = [jax.ShapeDtypeStruct(w.shape, F32) for w in ws] * 3
    outs = pl.pallas_call(body, name="adam_small", out_shape=out_shape, in_specs=[VMEM] * (4 * count),
                          out_specs=[VMEM] * (3 * count))(*ws, *gs, *ms, *vs)
    return outs[:count], outs[count:2 * count], outs[2 * count:]


MIX_TILE = 512
FFN_TILE = 256
GRAD_K = 2048


def kernel(x, norm_mix_g, w_in, conv_a_w, conv_a_b, ln_a_g, ln_a_b, pool_w, pool_scale, w_out, norm_ffn_g, w_up, conv_f_w, conv_f_b, w_down, norm_final_g, loss_target, m_norm_mix_g, m_w_in, m_conv_a_w, m_conv_a_b, m_ln_a_g, m_ln_a_b, m_pool_w, m_pool_scale, m_w_out, m_norm_ffn_g, m_w_up, m_conv_f_w, m_conv_f_b, m_w_down, m_norm_final_g, v_norm_mix_g, v_w_in, v_conv_a_w, v_conv_a_b, v_ln_a_g, v_ln_a_b, v_pool_w, v_pool_scale, v_w_out, v_norm_ffn_g, v_w_up, v_conv_f_w, v_conv_f_b, v_w_down, v_norm_final_g):
    seq = x.shape[1]
    xs, ts = x[0], loss_target[0]
    mix_tile, ffn_tile, grad_k = min(MIX_TILE, seq), min(FFN_TILE, seq), min(GRAD_K, seq)
    chip = 2 * lax.axis_index("x") + lax.axis_index("y")
    core = lax.axis_index("c").astype(jnp.int32).reshape(1)

    wa_s = jnp.pad(conv_a_w[0], ((0, 32 - CONV_A), (0, 0)))
    wf_s = jnp.pad(conv_f_w[0], ((0, 8 - CONV_F), (0, 0)))
    win_b, wout_b, wup_b, wdown_b = _cast_shards(w_in[0], w_out[0], w_up[0], w_down[0])
    g3 = norm_final_g.reshape(1, D_MODEL)
    pw = pool_w[0]

    h1, proj, cpre, dpool, mcat, x1, win, wout, wup, wa_g, wf_g = _mixer_fwd(
        xs, norm_mix_g, win_b, wout_b, wup_b, wa_s, wf_s, conv_a_b, ln_a_g, ln_a_b, pw, pool_scale, mix_tile)
    wa = jnp.transpose(wa_g, (1, 0, 2)).reshape(32, D_CONV)
    wf = jnp.transpose(wf_g, (1, 0, 2)).reshape(8, D_FF)
    h2, up, gcs, act, wdown = _ffn_up(x1, norm_ffn_g, wup, wf, conv_f_b, wdown_b, ffn_tile)
    dx2, dx2b, sm_f2 = _ffn_down(x1, act, wdown, g3, ts, mix_tile)
    tags = ("w_in", "w_out", "w_up", "w_down")
    blocks = (256, 128, 256, 176)
    g_wdown = _weight_grad(act, dx2b, "rows2", grad_k)
    dup, dx1, dx1b, sm_b1, sf, l_wdown = _ffn_bwd(
        dx2, up, gcs, x1, norm_ffn_g, wup, wf, wdown, ("exchange", [g_wdown]), ffn_tile)
    p_wdown = _pair_sum(core, g_wdown, l_wdown, tags[3], blocks[3])
    g_wup, s_wdown = _weight_grad(h2, dup, "cols_chip", grad_k, ("scatter", [p_wdown]))
    g_wout, l_wup = _weight_grad(mcat, dx1b, "rows1", grad_k, ("exchange", [g_wup]))
    p_wup = _pair_sum(core, g_wup, l_wup, tags[2], blocks[2])
    l_wout, = _sibling_exchange((g_wout,), (), "early")
    p_wout = _pair_sum(core, g_wout, l_wout, tags[1], blocks[1])
    dproj, gx, sm_b2, s5, sp, s_wout, s_wup = _mixer_bwd(
        dx1, xs, proj, cpre, dpool, norm_mix_g, win, wa, ln_a_g, ln_a_b, pw, pool_scale, wout, [p_wout, p_wup], mix_tile)
    g_win, grad_x = _weight_grad(h1, dproj, "cols_half", grad_k, carry=gx)

    smalls = (sm_f2, sm_b1, sm_b2, sf, s5, sp)
    landed = _sibling_exchange((g_win,), smalls, "late")
    part_win = _pair_sum(core, g_win, landed[0], tags[0], blocks[0])
    small_parts = _pair_sum_small(smalls, landed[1:])
    late = _chip_scatter([part_win], small_parts)
    scattered = [late[0], s_wout, s_wup, s_wdown] + list(late[1:])

    big_w = (w_in[0], w_out[0], w_up[0], w_down[0])
    big_m = (m_w_in[0], m_w_out[0], m_w_up[0], m_w_down[0])
    big_v = (v_w_in[0], v_w_out[0], v_w_up[0], v_w_down[0])
    big = {}
    for tag, p, w, m, v, br in zip(tags, scattered[:4], big_w, big_m, big_v, blocks):
        big[tag] = [a[None] for a in _adam_big(p, w, m, v, tag, br)]

    (g_g1, g_g2, g_g3, loss_row, g_wf_all, g_fb, g_wa_all, g_cb, g_lg, g_lb, g_ps, g_pw) = _reduce_small(*scattered[4:])
    g_wa = lax.dynamic_slice(g_wa_all, (0, chip * (D_CONV // N_CHIPS)), (32, D_CONV // N_CHIPS))[:CONV_A]
    g_wf = lax.dynamic_slice(g_wf_all, (0, chip * (D_FF // N_CHIPS)), (8, D_FF // N_CHIPS))[:CONV_F]
    small_names = ("norm_mix_g", "conv_a_w", "conv_a_b", "ln_a_g", "ln_a_b", "pool_w", "pool_scale", "norm_ffn_g",
                   "conv_f_w", "conv_f_b", "norm_final_g")
    small_w = (norm_mix_g, conv_a_w[0], conv_a_b, ln_a_g, ln_a_b, pw, pool_scale, norm_ffn_g, conv_f_w[0], conv_f_b, g3)
    small_m = (m_norm_mix_g, m_conv_a_w[0], m_conv_a_b, m_ln_a_g, m_ln_a_b, m_pool_w[0], m_pool_scale, m_norm_ffn_g,
               m_conv_f_w[0], m_conv_f_b, m_norm_final_g.reshape(1, D_MODEL))
    small_v = (v_norm_mix_g, v_conv_a_w[0], v_conv_a_b, v_ln_a_g, v_ln_a_b, v_pool_w[0], v_pool_scale, v_norm_ffn_g,
               v_conv_f_w[0], v_conv_f_b, v_norm_final_g.reshape(1, D_MODEL))
    small_g = (g_g1, g_wa, g_cb, g_lg, g_lb, g_pw, g_ps, g_g2, g_wf, g_fb, g_g3)
    s_delta, s_m, s_v = _adam_small(small_w, small_g, small_m, small_v)
    shapes = {"conv_a_w": conv_a_w.shape, "pool_w": pool_w.shape, "conv_f_w": conv_f_w.shape, "norm_final_g": norm_final_g.shape}
    small = {}
    for t, name in enumerate(small_names):
        shp = shapes.get(name)
        small[name] = [a if shp is None else a.reshape(shp) for a in (small_g[t], s_delta[t], s_m[t], s_v[t])]

    order = ("norm_mix_g", "w_in", "conv_a_w", "conv_a_b", "ln_a_g", "ln_a_b", "pool_w", "pool_scale", "w_out", "norm_ffn_g",
             "w_up", "conv_f_w", "conv_f_b", "w_down", "norm_final_g")
    table = {**big, **small}
    loss = loss_row[0, 0]
    outs = [loss, grad_x[None]]
    for t in range(4):
        outs += [table[name][t] for name in order]
    return tuple(outs)
```

```python
import functools

import jax
import jax.numpy as jnp
from jax import lax
from jax.experimental import pallas as pl
from jax.experimental.pallas import tpu as pltpu

F32 = jnp.float32
BF16 = jnp.bfloat16
EPS = 1e-6
ADAM_LR = 0.001
ADAM_B1 = 0.9
ADAM_B2 = 0.999
ADAM_EPS = 1e-08
ADAM_WD = 0.01
ADAM_STEP = 10

D_MODEL = 1024
D_CONV = 512
D_POOL = 512
D_IN = 1536
D_FF = 2816
CONV_A = 31
CONV_F = 3
POOL_WINDOWS = (2, 4, 8, 16)
POOL_GROUP = 128
N_CHIPS = 4
FF_CHUNK = 256
N_FF_CHUNKS = D_FF // FF_CHUNK
A_HALO = 32
P_HALO = 16
VMEM_LIMIT = 56 * 1024 * 1024
MESH = pl.DeviceIdType.MESH

ANY = pl.BlockSpec(memory_space=pl.ANY)
VMEM = pl.BlockSpec(memory_space=pltpu.VMEM)


def _dot(a, b):
    return jnp.dot(a, b, preferred_element_type=F32)


def _dot_nt(a, b):
    return lax.dot_general(a, b, (((1,), (1,)), ((), ())), preferred_element_type=F32)


def _dot_tn(a, b):
    return lax.dot_general(a, b, (((0,), (0,)), ((), ())), preferred_element_type=F32)


def _sigmoid(v):
    return jax.nn.sigmoid(v)


def _colsum(v):
    return jnp.sum(v, axis=0, keepdims=True)


def _rowmean(v):
    return jnp.mean(v, axis=-1, keepdims=True)


def _place():
    x, y, c = lax.axis_index("x"), lax.axis_index("y"), lax.axis_index("c")
    chips = [(1 - x, y), (x, 1 - y), (1 - x, 1 - y)]
    return x, y, c, 2 * x + y, chips


def _staged(src, dst, stage, sem_in, sem_out):
    hop_in = pltpu.make_async_copy(src, stage, sem_in)
    hop_out = pltpu.make_async_copy(stage, dst, sem_out)

    def relay():
        hop_in.wait()
        hop_out.start()

    return hop_in.start, relay, hop_out.wait


def _gather_ops(bufs, fulls, col_sharded, sems, stages):
    ici_send, ici_recv, fwd_send, fwd_recv, loc_in, loc_out = sems
    n_big = len(bufs)
    x, y, c, k, chips = _place()

    def block(i, kk, half=None):
        rows, cols = bufs[i].shape
        if col_sharded[i]:
            rs = slice(None) if half is None else pl.ds(pl.multiple_of(half * (rows // 2), 16), rows // 2)
            return fulls[i].at[rs, pl.ds(pl.multiple_of(kk * cols, 128), cols)]
        if half is None:
            return fulls[i].at[pl.ds(pl.multiple_of(kk * rows, 16), rows), :]
        return fulls[i].at[pl.ds(pl.multiple_of(kk * rows + half * (rows // 2), 16), rows // 2), :]

    def my_half(i):
        rows = bufs[i].shape[0]
        return bufs[i].at[pl.ds(pl.multiple_of(c * (rows // 2), 16), rows // 2), :]

    def ici(i, j, kk):
        return pltpu.make_async_remote_copy(
            src_ref=my_half(i), dst_ref=block(i, kk, c), send_sem=ici_send.at[i * 3 + j], recv_sem=ici_recv.at[i * 3 + j],
            device_id=(*chips[j], c), device_id_type=MESH)

    def fwd(i, j, kk, half):
        return pltpu.make_async_remote_copy(
            src_ref=block(i, kk, half), dst_ref=block(i, kk, half), send_sem=fwd_send.at[i * 3 + j],
            recv_sem=fwd_recv.at[i * 3 + j], device_id=(x, y, 1 - c), device_id_type=MESH)

    local = [_staged(bufs[i], block(i, k), stages[i], loc_in.at[i], loc_out.at[i]) for i in range(n_big)]
    sends = [ici(i, j, k) for i in range(n_big) for j in range(3)]
    peers = [(i, j, 2 * qx + qy) for i in range(n_big) for j, (qx, qy) in enumerate(chips)]

    def start():
        for cp in local:
            cp[0]()
        for cp in sends:
            cp.start()

    def land():
        for cp in local:
            cp[1]()
        for i, j, kq in peers:
            ici(i, j, kq).wait_recv()
            fwd(i, j, kq, c).start()

    def finish():
        for i, j, kq in peers:
            fwd(i, j, kq, 1 - c).wait_recv()
            fwd(i, j, kq, c).wait_send()
        for cp in sends:
            cp.wait_send()
        for cp in local:
            cp[2]()

    return start, land, finish


def _gather_scratch(shards):
    n_big = len(shards)
    return ([pltpu.SemaphoreType.DMA((3 * n_big,))] * 4 + [pltpu.SemaphoreType.DMA((n_big,))] * 2
            + [pltpu.VMEM(b.shape, b.dtype) for b in shards])


def _tap_ops(srcs, dsts, sems):
    send, recv, loc = sems
    _, _, c, k, chips = _place()

    def copy(t, j, kk):
        return pltpu.make_async_remote_copy(
            src_ref=srcs[t], dst_ref=dsts[t].at[kk], send_sem=send.at[t * 3 + j], recv_sem=recv.at[t * 3 + j],
            device_id=(*chips[j], c), device_id_type=MESH)

    local = [pltpu.make_async_copy(srcs[t], dsts[t].at[k], loc.at[t]) for t in range(len(srcs))]
    sends = [[copy(t, j, k) for j in range(3)] for t in range(len(srcs))]

    def start():
        for t, cp in enumerate(local):
            cp.start()
            for sd in sends[t]:
                sd.start()

    def wait(t):
        for j, (qx, qy) in enumerate(chips):
            copy(t, j, 2 * qx + qy).wait_recv()
        for sd in sends[t]:
            sd.wait_send()
        local[t].wait()

    return start, wait


def _cast_shards(*shards):
    def body(*refs):
        for src, dst in zip(refs[:len(shards)], refs[len(shards):]):
            dst[...] = src[...].astype(BF16)

    return pl.pallas_call(
        body, name="cast_shards", out_shape=[jax.ShapeDtypeStruct(s.shape, BF16) for s in shards],
        in_specs=[VMEM] * len(shards), out_specs=[VMEM] * len(shards),
        compiler_params=pltpu.CompilerParams(vmem_limit_bytes=VMEM_LIMIT),
    )(*shards)


def _load_weights(pairs, sem):
    cps = [pltpu.make_async_copy(src, dst, sem.at[i]) for i, (src, dst) in enumerate(pairs)]
    for cp in cps:
        cp.start()
    for cp in cps:
        cp.wait()


def _shifted_views(buf, shifted, t_rows):
    n = t_rows + A_HALO - 8
    for b in range(1, 8):
        shifted[b - 1] = buf[b:b + n, :]

    def view(offset):
        a, b = divmod(offset, 8)
        if b == 0:
            return buf[8 * a:8 * a + t_rows, :]
        return shifted[b - 1, 8 * a:8 * a + t_rows, :]

    return view


def _pool_count(tile, t_rows, w):
    row = lax.broadcasted_iota(jnp.int32, (t_rows, POOL_GROUP), 0) + tile * t_rows
    return jnp.minimum(row + 1, w).astype(F32)


def _mixer_fwd(x, g1, win_b, wout_b, wup_b, wa_s, wf_s, cb, lg, lb, pw, ps, tile_rows):
    seq = x.shape[0]
    tr = tile_rows
    n = seq // tr

    def body(x_ref, g1_ref, win_b_hbm, wout_b_hbm, wup_b_hbm, wa_s_hbm, wf_s_hbm, cb_ref, lg_ref, lb_ref, pw_ref,
             ps_ref, h1_ref, proj_ref, c_ref, d_ref, m_ref, x1_ref, win_f, wout_f, wup_f, wa_g, wf_g,
             win_v, wout_v, wa_ref, ubuf, ushift, bbuf, sem, *csems):
        i = pl.program_id(0)
        first_sems, first_stages, later_sems, later_stages, tap_sems = (
            csems[0:6], csems[6:8], csems[8:14], csems[14:15], csems[15:18])

        def first():
            return _gather_ops((win_b_hbm, wout_b_hbm), (win_f, wout_f), (True, False), first_sems, first_stages)

        def later():
            return _gather_ops((wup_b_hbm,), (wup_f,), (True,), later_sems, later_stages)

        def taps():
            return _tap_ops((wa_s_hbm, wf_s_hbm), (wa_g, wf_g), tap_sems)

        @pl.when(i == 0)
        def _():
            first()[0]()
            taps()[0]()
            later()[0]()
            first()[1]()
            first()[2]()
            taps()[1](0)
            loads = [(win_f, win_v), (wout_f, wout_v)]
            loads += [(wa_g.at[kk], wa_ref.at[:, kk * (D_CONV // N_CHIPS):(kk + 1) * (D_CONV // N_CHIPS)]) for kk in range(N_CHIPS)]
            _load_weights(loads, sem)
            ubuf[0:A_HALO, :] = jnp.zeros((A_HALO, D_CONV), F32)
            bbuf[0:P_HALO, :] = jnp.zeros((P_HALO, D_POOL), F32)

        xv = x_ref[...]
        r = lax.rsqrt(_rowmean(xv * xv) + EPS)
        h1 = (xv * r * g1_ref[...]).astype(BF16)
        h1_ref[...] = h1
        proj = _dot(h1, win_v[...])
        proj_ref[...] = proj.astype(BF16)
        av, ag, bi = proj[:, :D_CONV], proj[:, D_CONV:2 * D_CONV], proj[:, 2 * D_CONV:]
        ubuf[A_HALO:A_HALO + tr, :] = av * _sigmoid(ag)
        off = A_HALO - (CONV_A - 1)
        uview = _shifted_views(ubuf, ushift, tr)
        acc = wa_ref[0:1, :] * uview(off)
        for j in range(1, CONV_A):
            acc = acc + wa_ref[j:j + 1, :] * uview(off + j)
        cv = acc + cb_ref[...]
        ubuf[0:A_HALO, :] = ubuf[tr:tr + A_HALO, :]
        c_ref[...] = cv.astype(BF16)
        xc = cv - _rowmean(cv)
        z = xc * lax.rsqrt(_rowmean(xc * xc) + EPS)
        ln = z * lg_ref[...] + lb_ref[...]
        ya = ln * _sigmoid(ln)
        bbuf[P_HALO:P_HALO + tr, :] = bi
        ds, ybs = [], []
        for g, w in enumerate(POOL_WINDOWS):
            cols = slice(g * POOL_GROUP, (g + 1) * POOL_GROUP)
            s = bi[:, cols]
            for kk in range(1, w):
                s = s + bbuf[P_HALO - kk:P_HALO - kk + tr, cols]
            dg = s / _pool_count(i, tr, w) - bi[:, cols]
            ds.append(dg)
            ybs.append(_dot(dg.astype(BF16), pw_ref[g].astype(BF16)))
        bbuf[0:P_HALO, :] = bbuf[tr:tr + P_HALO, :]
        d_ref[...] = jnp.concatenate(ds, axis=1).astype(BF16)
        yb = jnp.concatenate(ybs, axis=1) * ps_ref[...]
        m = jnp.concatenate([ya, yb], axis=1).astype(BF16)
        m_ref[...] = m
        x1_ref[...] = xv + _dot(m, wout_v[...])

        @pl.when(i == n - 1)
        def _():
            later()[1]()
            later()[2]()
            taps()[1](1)

    tile = lambda w: pl.BlockSpec((tr, w), lambda i: (i, 0))
    full = lambda a: pl.BlockSpec(a.shape, lambda i: (0,) * a.ndim)
    return pl.pallas_call(
        body, name="mixer_fwd", grid=(n,),
        in_specs=[tile(D_MODEL), full(g1)] + [ANY] * 5 + [full(cb), full(lg), full(lb), full(pw), full(ps)],
        out_specs=[tile(D_MODEL), tile(D_IN), tile(D_CONV), tile(D_POOL), tile(D_MODEL), tile(D_MODEL)] + [ANY] * 5,
        out_shape=[
            jax.ShapeDtypeStruct((seq, D_MODEL), BF16), jax.ShapeDtypeStruct((seq, D_IN), BF16),
            jax.ShapeDtypeStruct((seq, D_CONV), BF16), jax.ShapeDtypeStruct((seq, D_POOL), BF16),
            jax.ShapeDtypeStruct((seq, D_MODEL), BF16), jax.ShapeDtypeStruct((seq, D_MODEL), F32),
            jax.ShapeDtypeStruct((D_MODEL, D_IN), BF16), jax.ShapeDtypeStruct((D_MODEL, D_MODEL), BF16),
            jax.ShapeDtypeStruct((D_MODEL, 2 * D_FF), BF16),
            jax.ShapeDtypeStruct((N_CHIPS,) + wa_s.shape, F32), jax.ShapeDtypeStruct((N_CHIPS,) + wf_s.shape, F32),
        ],
        scratch_shapes=[
            pltpu.VMEM((D_MODEL, D_IN), BF16), pltpu.VMEM((D_MODEL, D_MODEL), BF16), pltpu.VMEM((32, D_CONV), F32),
            pltpu.VMEM((tr + A_HALO, D_CONV), F32), pltpu.VMEM((7, tr + A_HALO - 8, D_CONV), F32),
            pltpu.VMEM((tr + P_HALO, D_POOL), F32), pltpu.SemaphoreType.DMA((2 + N_CHIPS,)),
        ] + _gather_scratch((win_b, wout_b)) + _gather_scratch((wup_b,)) + [
            pltpu.SemaphoreType.DMA((6,)), pltpu.SemaphoreType.DMA((6,)), pltpu.SemaphoreType.DMA((2,))],
        compiler_params=pltpu.CompilerParams(dimension_semantics=("arbitrary",), vmem_limit_bytes=VMEM_LIMIT),
    )(x, g1, win_b, wout_b, wup_b, wa_s, wf_s, cb, lg, lb, pw, ps)


def _ffn_up(x1, g2, wup, wf, fb, wdown_b, tile_rows):
    seq = x1.shape[0]
    tr = tile_rows
    n = seq // tr

    def body(x1_ref, g2_ref, wup_hbm, wf_ref, fb_ref, wdown_b_hbm,
             h2_ref, up_ref, gc_ref, act_ref, wdown_f, wup_v, gbuf, sem, *gsems):
        i = pl.program_id(0)

        def gather():
            return _gather_ops((wdown_b_hbm,), (wdown_f,), (False,), gsems[:6], gsems[6:])

        @pl.when(i == 0)
        def _():
            gather()[0]()
            _load_weights(((wup_hbm, wup_v),), sem)
            gbuf[0:8, :] = jnp.zeros((8, D_FF), F32)

        x1v = x1_ref[...]
        r2 = lax.rsqrt(_rowmean(x1v * x1v) + EPS)
        h2 = (x1v * r2 * g2_ref[...]).astype(BF16)
        h2_ref[...] = h2

        def up_proj(j):
            return (_dot(h2, wup_v[:, j * FF_CHUNK:(j + 1) * FF_CHUNK]),
                    _dot(h2, wup_v[:, D_FF + j * FF_CHUNK:D_FF + (j + 1) * FF_CHUNK]))

        ahead = up_proj(0)
        for j in range(N_FF_CHUNKS):
            cs = slice(j * FF_CHUNK, (j + 1) * FF_CHUNK)
            vs = slice(D_FF + j * FF_CHUNK, D_FF + (j + 1) * FF_CHUNK)
            gate, val = ahead
            if j + 1 < N_FF_CHUNKS:
                ahead = up_proj(j + 1)
            up_ref[:, cs] = gate.astype(BF16)
            up_ref[:, vs] = val.astype(BF16)
            gbuf[8:8 + tr, cs] = gate
            gc = (wf_ref[0:1, cs] * gbuf[6:6 + tr, cs] + wf_ref[1:2, cs] * gbuf[7:7 + tr, cs]
                  + wf_ref[2:3, cs] * gate + fb_ref[:, cs])
            gbuf[0:8, cs] = gbuf[tr:tr + 8, cs]
            gc_ref[:, cs] = gc.astype(BF16)
            act_ref[:, cs] = (gc * _sigmoid(gc) * val).astype(BF16)

        @pl.when(i == max(n - 2, 0))
        def _():
            gather()[1]()

        @pl.when(i == n - 1)
        def _():
            gather()[2]()

    tile = lambda w: pl.BlockSpec((tr, w), lambda i: (i, 0))
    full = lambda a: pl.BlockSpec(a.shape, lambda i: (0,) * a.ndim)
    return pl.pallas_call(
        body, name="ffn_up", grid=(n,),
        in_specs=[tile(D_MODEL), full(g2), ANY, full(wf), full(fb), ANY],
        out_specs=[tile(D_MODEL), tile(2 * D_FF), tile(D_FF), tile(D_FF), ANY],
        out_shape=[
            jax.ShapeDtypeStruct((seq, D_MODEL), BF16), jax.ShapeDtypeStruct((seq, 2 * D_FF), BF16),
            jax.ShapeDtypeStruct((seq, D_FF), BF16), jax.ShapeDtypeStruct((seq, D_FF), BF16),
            jax.ShapeDtypeStruct((D_FF, D_MODEL), BF16),
        ],
        scratch_shapes=[pltpu.VMEM(wup.shape, BF16), pltpu.VMEM((tr + 8, D_FF), F32), pltpu.SemaphoreType.DMA((1,))]
        + _gather_scratch((wdown_b,)),
        compiler_params=pltpu.CompilerParams(dimension_semantics=("arbitrary",), vmem_limit_bytes=VMEM_LIMIT),
    )(x1, g2, wup, wf, fb, wdown_b)


def _ffn_down(x1, act, wdown, g3, target, tile_rows):
    seq = x1.shape[0]
    tr = tile_rows
    n = seq // tr

    def body(x1_ref, act_ref, wdown_hbm, g3_ref, t_ref, dx2_ref, dx2b_ref, sm_ref, wdown_v, sem):
        i = pl.program_id(0)

        @pl.when(i == 0)
        def _():
            _load_weights(((wdown_hbm, wdown_v),), sem)
            sm_ref[...] = jnp.zeros(sm_ref.shape, F32)

        x2 = x1_ref[...] + _dot(act_ref[...], wdown_v[...])
        r3 = lax.rsqrt(_rowmean(x2 * x2) + EPS)
        n3 = x2 * r3
        err = n3 * g3_ref[...] - t_ref[...]
        dy = err / D_MODEL
        sm_ref[2:3, :] += _colsum(dy * n3)
        loss = 0.5 * _colsum(_rowmean(err * err))
        sm_ref[3:4, :] += jnp.broadcast_to(loss, (1, D_MODEL))
        dn = dy * g3_ref[...]
        dx2v = r3 * (dn - n3 * _rowmean(dn * n3))
        dx2_ref[...] = dx2v
        dx2b_ref[...] = dx2v.astype(BF16)

    tile = lambda w: pl.BlockSpec((tr, w), lambda i: (i, 0))
    full = lambda a: pl.BlockSpec(a.shape, lambda i: (0,) * a.ndim)
    return pl.pallas_call(
        body, name="ffn_down", grid=(n,),
        in_specs=[tile(D_MODEL), tile(D_FF), ANY, full(g3), tile(D_MODEL)],
        out_specs=[tile(D_MODEL), tile(D_MODEL), pl.BlockSpec((8, D_MODEL), lambda i: (0, 0))],
        out_shape=[
            jax.ShapeDtypeStruct((seq, D_MODEL), F32), jax.ShapeDtypeStruct((seq, D_MODEL), BF16),
            jax.ShapeDtypeStruct((8, D_MODEL), F32),
        ],
        scratch_shapes=[pltpu.VMEM(wdown.shape, BF16), pltpu.SemaphoreType.DMA((1,))],
        compiler_params=pltpu.CompilerParams(dimension_semantics=("arbitrary",), vmem_limit_bytes=VMEM_LIMIT),
    )(x1, act, wdown, g3, target)


def _ffn_bwd(dx2, up, gcs, x1, g2, wup, wf, wdown, comm, tile_rows):
    seq = x1.shape[0]
    c_ins, c_shapes, c_sems, c_ops = _comm_plan(comm)
    nc = len(c_ins)
    tr = tile_rows
    n = seq // tr

    def body(dx2_ref, up_ref, gc_ref, x1_ref, g2_ref, wup_hbm, wf_ref, wdown_hbm, *rest):
        c_in, rest = rest[:nc], rest[nc:]
        dup_ref, dx1_ref, dx1b_ref, sm_ref, sf_ref = rest[:5]
        c_out, rest = rest[5:5 + nc], rest[5 + nc:]
        wup_v, wdown_v, dbuf, dcar, sem = rest[:5]
        c_sem_refs = rest[5:]
        i = pl.program_id(0)

        @pl.when(i == 0)
        def _():
            c_ops(c_in, c_out, c_sem_refs)[0]()
            _load_weights(((wup_hbm, wup_v), (wdown_hbm, wdown_v)), sem)
            dcar[...] = jnp.zeros(dcar.shape, F32)
            sm_ref[...] = jnp.zeros(sm_ref.shape, F32)
            sf_ref[...] = jnp.zeros(sf_ref.shape, F32)

        dx2v = dx2_ref[...]
        dx2b = dx2v.astype(BF16)
        dh2 = jnp.zeros((tr, D_MODEL), F32)

        def down_t(j):
            return _dot_nt(dx2b, wdown_v[j * FF_CHUNK:(j + 1) * FF_CHUNK, :])

        ahead = down_t(0)
        for j in range(N_FF_CHUNKS):
            cs = slice(j * FF_CHUNK, (j + 1) * FF_CHUNK)
            vs = slice(D_FF + j * FF_CHUNK, D_FF + (j + 1) * FF_CHUNK)
            dact = ahead
            if j + 1 < N_FF_CHUNKS:
                ahead = down_t(j + 1)
            gate = up_ref[:, cs].astype(F32)
            val = up_ref[:, vs].astype(F32)
            gc = gc_ref[:, cs].astype(F32)
            sg = _sigmoid(gc)
            dval = dact * (gc * sg)
            dgc = dact * val * (sg * (1.0 + gc * (1.0 - sg)))
            dbuf[0:tr, :] = dgc
            dbuf[tr:tr + 8, :] = dcar[:, cs]
            d_p1 = dbuf[1:1 + tr, :]
            d_p2 = dbuf[2:2 + tr, :]
            dgate = wf_ref[2:3, cs] * dgc + wf_ref[1:2, cs] * d_p1 + wf_ref[0:1, cs] * d_p2
            dcar[:, cs] = dgc[0:8, :]
            sf_ref[0:1, cs] += _colsum(d_p2 * gate)
            sf_ref[1:2, cs] += _colsum(d_p1 * gate)
            sf_ref[2:3, cs] += _colsum(dgc * gate)
            sf_ref[3:4, cs] += _colsum(dgc)
            dgb, dvb = dgate.astype(BF16), dval.astype(BF16)
            dup_ref[:, cs] = dgb
            dup_ref[:, vs] = dvb
            dh2 = dh2 + _dot_nt(dgb, wup_v[:, cs]) + _dot_nt(dvb, wup_v[:, vs])
        x1v = x1_ref[...]
        r2 = lax.rsqrt(_rowmean(x1v * x1v) + EPS)
        n2 = x1v * r2
        sm_ref[1:2, :] += _colsum(dh2 * n2)
        dn2 = dh2 * g2_ref[...]
        dx1v = dx2v + r2 * (dn2 - n2 * _rowmean(dn2 * n2))
        dx1_ref[...] = dx1v
        dx1b_ref[...] = dx1v.astype(BF16)

        @pl.when(i == n - 1)
        def _():
            c_ops(c_in, c_out, c_sem_refs)[2]()

    tile = lambda w: pl.BlockSpec((tr, w), lambda i: (n - 1 - i, 0))
    full = lambda a: pl.BlockSpec(a.shape, lambda i: (0,) * a.ndim)
    acc = lambda rows, w: pl.BlockSpec((rows, w), lambda i: (0, 0))
    return pl.pallas_call(
        body, name="ffn_bwd", grid=(n,),
        in_specs=[tile(D_MODEL), tile(2 * D_FF), tile(D_FF), tile(D_MODEL), full(g2), ANY, full(wf), ANY] + [ANY] * nc,
        out_specs=[tile(2 * D_FF), tile(D_MODEL), tile(D_MODEL), acc(8, D_MODEL), acc(8, D_FF)] + [ANY] * nc,
        out_shape=[
            jax.ShapeDtypeStruct((seq, 2 * D_FF), BF16), jax.ShapeDtypeStruct((seq, D_MODEL), F32),
            jax.ShapeDtypeStruct((seq, D_MODEL), BF16), jax.ShapeDtypeStruct((8, D_MODEL), F32),
            jax.ShapeDtypeStruct((8, D_FF), F32),
        ] + c_shapes,
        scratch_shapes=[
            pltpu.VMEM(wup.shape, BF16), pltpu.VMEM(wdown.shape, BF16),
            pltpu.VMEM((tr + 8, FF_CHUNK), F32), pltpu.VMEM((8, D_FF), F32), pltpu.SemaphoreType.DMA((2,)),
        ] + c_sems,
        compiler_params=pltpu.CompilerParams(dimension_semantics=("arbitrary",), vmem_limit_bytes=VMEM_LIMIT),
    )(dx2, up, gcs, x1, g2, wup, wf, wdown, *c_ins)


def _mixer_bwd(dx1, x, proj, cpre, d, g1, win, wa, lg, lb, pw, ps, wout, parts, tile_rows):
    seq = x.shape[0]
    n_parts = len(parts)
    tr = tile_rows
    n = seq // tr
    row_cb, row_lg, row_lb, row_ps = 32, 33, 34, 35

    def body(dx1_ref, x_ref, proj_ref, projh_ref, c_ref, d_ref, g1_ref, win_hbm, wa_ref, lg_ref, lb_ref, pw_ref, ps_ref,
             wout_hbm, *rest):
        part_refs, rest = rest[:n_parts], rest[n_parts:]
        dproj_ref, gx_ref, sm_ref, s5_ref, sp_ref = rest[:5]
        land_refs, rest = rest[5:5 + n_parts], rest[5 + n_parts:]
        win_v, wout_v, ubuf, ushift, dcbuf, dshift, ebuf, sem = rest[:8]
        ssems = rest[8:]
        i = pl.program_id(0)
        tile = n - 1 - i

        def scatter():
            return _scatter_ops(part_refs, land_refs, n_parts, ssems[:6], ssems[6:])

        @pl.when(i == 0)
        def _():
            scatter()[0]()
            _load_weights(((win_hbm, win_v), (wout_hbm, wout_v)), sem)
            dcbuf[tr:tr + A_HALO, :] = jnp.zeros((A_HALO, D_CONV), F32)
            ebuf[tr:tr + P_HALO, :] = jnp.zeros((P_HALO, D_POOL), F32)
            sm_ref[...] = jnp.zeros(sm_ref.shape, F32)
            s5_ref[...] = jnp.zeros(s5_ref.shape, F32)
            sp_ref[...] = jnp.zeros(sp_ref.shape, F32)

        dx1v = dx1_ref[...]
        dm = _dot_nt(dx1v.astype(BF16), wout_v[...])
        dya, dyb = dm[:, :D_CONV], dm[:, D_CONV:]
        dbis = []
        for g, w in enumerate(POOL_WINDOWS):
            cols = slice(g * POOL_GROUP, (g + 1) * POOL_GROUP)
            dgb = d_ref[:, cols]
            pwb = pw_ref[g].astype(BF16)
            dyg = dyb[:, cols]
            s5_ref[row_ps:row_ps + 1, cols] += _colsum(dyg * _dot(dgb, pwb))
            dqb = (dyg * ps_ref[:, cols]).astype(BF16)
            sp_ref[g] += _dot_tn(dgb, dqb)
            dd = _dot_nt(dqb, pwb)
            e = dd / _pool_count(tile, tr, w)
            ebuf[0:tr, cols] = e
            s = e
            for kk in range(1, w):
                s = s + ebuf[kk:kk + tr, cols]
            dbis.append(s - dd)
        ebuf[tr:tr + P_HALO, :] = ebuf[0:P_HALO, :]
        cv = c_ref[...].astype(F32)
        xc = cv - _rowmean(cv)
        rs = lax.rsqrt(_rowmean(xc * xc) + EPS)
        z = xc * rs
        ln = z * lg_ref[...] + lb_ref[...]
        sl = _sigmoid(ln)
        dl = dya * (sl * (1.0 + ln * (1.0 - sl)))
        s5_ref[row_lg:row_lg + 1, :] += _colsum(dl * z)
        s5_ref[row_lb:row_lb + 1, :] += _colsum(dl)
        dz = dl * lg_ref[...]
        dc = rs * (dz - _rowmean(dz) - z * _rowmean(dz * z))
        s5_ref[row_cb:row_cb + 1, :] += _colsum(dc)
        dcbuf[0:tr, :] = dc
        keep = (tile > 0).astype(F32)
        avh = projh_ref[:, :D_CONV].astype(F32)
        agh = projh_ref[:, D_CONV:].astype(F32)
        ubuf[0:A_HALO, :] = avh * _sigmoid(agh) * keep
        av = proj_ref[:, :D_CONV].astype(F32)
        ag = proj_ref[:, D_CONV:2 * D_CONV].astype(F32)
        sg = _sigmoid(ag)
        ubuf[A_HALO:A_HALO + tr, :] = av * sg
        off = A_HALO - (CONV_A - 1)
        du = wa_ref[CONV_A - 1:CONV_A, :] * dc
        dview = _shifted_views(dcbuf, dshift, tr)
        uview = _shifted_views(ubuf, ushift, tr)
        for j in range(CONV_A - 1):
            du = du + wa_ref[j:j + 1, :] * dview(CONV_A - 1 - j)
        for j in range(CONV_A):
            s5_ref[j:j + 1, :] += _colsum(dc * uview(off + j))
        dcbuf[tr:tr + A_HALO, :] = dcbuf[0:A_HALO, :]
        dav = du * sg
        dag = du * av * (sg * (1.0 - sg))
        dprojb = jnp.concatenate([dav, dag] + dbis, axis=1).astype(BF16)
        dproj_ref[...] = dprojb
        dh1 = _dot_nt(dprojb, win_v[...])
        xv = x_ref[...]
        r1 = lax.rsqrt(_rowmean(xv * xv) + EPS)
        n1 = xv * r1
        sm_ref[0:1, :] += _colsum(dh1 * n1)
        dn1 = dh1 * g1_ref[...]
        gx_ref[...] = dx1v + r1 * (dn1 - n1 * _rowmean(dn1 * n1))

        @pl.when(i == max(n - 2, 0))
        def _():
            scatter()[1]()

        @pl.when(i == n - 1)
        def _():
            scatter()[2]()

    tile = lambda w: pl.BlockSpec((tr, w), lambda i: (n - 1 - i, 0))
    full = lambda a: pl.BlockSpec(a.shape, lambda i: (0,) * a.ndim)
    halo = pl.BlockSpec((A_HALO, 2 * D_CONV), lambda i: (jnp.maximum((n - 1 - i) * (tr // A_HALO) - 1, 0), 0))
    acc = lambda shape: pl.BlockSpec(shape, lambda i: (0,) * len(shape))
    return pl.pallas_call(
        body, name="mixer_bwd", grid=(n,),
        in_specs=[tile(D_MODEL), tile(D_MODEL), tile(D_IN), halo, tile(D_CONV), tile(D_POOL), full(g1), ANY, full(wa),
                  full(lg), full(lb), full(pw), full(ps), ANY] + [ANY] * n_parts,
        out_specs=[tile(D_IN), tile(D_MODEL), acc((8, D_MODEL)), acc((40, D_CONV)), acc(pw.shape)] + [ANY] * n_parts,
        out_shape=[
            jax.ShapeDtypeStruct((seq, D_IN), BF16), jax.ShapeDtypeStruct((seq, D_MODEL), F32),
            jax.ShapeDtypeStruct((8, D_MODEL), F32), jax.ShapeDtypeStruct((40, D_CONV), F32),
            jax.ShapeDtypeStruct(pw.shape, F32),
        ] + _scatter_shapes(parts, ()),
        scratch_shapes=[
            pltpu.VMEM(win.shape, BF16), pltpu.VMEM(wout.shape, BF16),
            pltpu.VMEM((tr + A_HALO, D_CONV), F32), pltpu.VMEM((7, tr + A_HALO - 8, D_CONV), F32),
            pltpu.VMEM((tr + A_HALO, D_CONV), F32), pltpu.VMEM((7, tr + A_HALO - 8, D_CONV), F32),
            pltpu.VMEM((tr + P_HALO, D_POOL), F32), pltpu.SemaphoreType.DMA((2,)),
        ] + _scatter_scratch(parts, ()),
        compiler_params=pltpu.CompilerParams(dimension_semantics=("arbitrary",), vmem_limit_bytes=VMEM_LIMIT),
    )(dx1, x, proj, proj, cpre, d, g1, win, wa, lg, lb, pw, ps, wout, *parts)


def _weight_grad(a, b, layout, k_rows, comm=None, carry=None):
    seq, m_dim = a.shape
    n_dim = b.shape[1]
    steps = seq // k_rows

    def store(o_ref, acc, index, value):
        if steps == 1:
            o_ref[index] = value.astype(BF16)
            return
        s = pl.program_id(1)

        @pl.when(s == 0)
        def _():
            acc[index] = value

        @pl.when(jnp.logical_and(s > 0, s < steps - 1))
        def _():
            acc[index] += value

        @pl.when(s == steps - 1)
        def _():
            o_ref[index] = (acc[index] + value).astype(BF16)

    if layout in ("rows1", "rows2"):
        groups = int(layout[-1])
        per_tile = N_CHIPS // groups
        rows = m_dim // N_CHIPS // 2
        a_w = m_dim // groups

        def body(a_ref, b_ref, o_ref, acc):
            r = _dot_tn(a_ref[...], b_ref[...])
            for p in range(per_tile):
                for h in range(2):
                    store(o_ref, acc, (h, p), r[(2 * p + h) * rows:(2 * p + h + 1) * rows, :])

        in_specs = [pl.BlockSpec((k_rows, a_w), lambda g, s: (s, g)), pl.BlockSpec((k_rows, n_dim), lambda g, s: (s, 0))]
        out_spec = pl.BlockSpec((2, per_tile, rows, n_dim), lambda g, s: (0, g, 0, 0))
        out_dims, acc_dims = (2, N_CHIPS, rows, n_dim), (2, per_tile, rows, n_dim)
    elif layout == "cols_chip":
        groups = N_CHIPS
        rows, cols = m_dim // 2, n_dim // N_CHIPS

        def body(a_ref, b_ref, o_ref, acc):
            r = _dot_tn(a_ref[...], b_ref[...])
            for h in range(2):
                store(o_ref, acc, h, r[h * rows:(h + 1) * rows, :])

        in_specs = [pl.BlockSpec((k_rows, m_dim), lambda g, s: (s, 0)), pl.BlockSpec((k_rows, cols), lambda g, s: (s, g))]
        out_spec = pl.BlockSpec((2, None, rows, cols), lambda g, s: (0, g, 0, 0))
        out_dims, acc_dims = (2, N_CHIPS, rows, cols), (2, rows, cols)
    else:
        groups = 2
        rows, cols = m_dim // 2, n_dim // N_CHIPS

        def body(a_ref, b_ref, o_ref, acc):
            r = _dot_tn(a_ref[...], b_ref[...])
            for k in range(N_CHIPS):
                store(o_ref, acc, k, r[:, k * cols:(k + 1) * cols])

        in_specs = [pl.BlockSpec((k_rows, rows), lambda g, s: (s, g)), pl.BlockSpec((k_rows, n_dim), lambda g, s: (s, 0))]
        out_spec = pl.BlockSpec((None, N_CHIPS, rows, cols), lambda g, s: (g, 0, 0, 0))
        out_dims, acc_dims = (2, N_CHIPS, rows, cols), (N_CHIPS, rows, cols)

    c_ins, c_shapes, c_sems, c_ops = _comm_plan(comm)
    nc = len(c_ins)
    c_specs = [ANY] * nc
    if carry is not None:
        assert comm is None and carry.shape[0] % (groups * steps) == 0
        carry_spec = pl.BlockSpec((carry.shape[0] // (groups * steps), carry.shape[1]), lambda g, s: (g * steps + s, 0))
        c_ins, c_shapes, c_specs, nc = (carry,), [jax.ShapeDtypeStruct(carry.shape, carry.dtype)], [carry_spec], 1

    def hosted(a_ref, b_ref, *rest):
        c_in, o_ref, c_out, acc, sems = rest[:nc], rest[nc], rest[nc + 1:2 * nc + 1], rest[2 * nc + 1], rest[2 * nc + 2:]
        g, s = pl.program_id(0), pl.program_id(1)
        if carry is not None:
            c_out[0][...] = c_in[0][...]
            body(a_ref, b_ref, o_ref, acc)
            return
        if nc:
            @pl.when(jnp.logical_and(g == 0, s == 0))
            def _():
                c_ops(c_in, c_out, sems)[0]()

        body(a_ref, b_ref, o_ref, acc)
        if nc:
            step = g * steps + s

            @pl.when(step == max(groups * steps - 2, 0))
            def _():
                c_ops(c_in, c_out, sems)[1]()

            @pl.when(step == groups * steps - 1)
            def _():
                c_ops(c_in, c_out, sems)[2]()

    outs = pl.pallas_call(
        hosted, name=f"weight_grad_{layout}_{m_dim}x{n_dim}", grid=(groups, steps),
        in_specs=in_specs + c_specs, out_specs=[out_spec] + c_specs,
        out_shape=[jax.ShapeDtypeStruct(out_dims, BF16)] + c_shapes,
        scratch_shapes=[pltpu.VMEM(acc_dims, F32)] + c_sems,
        compiler_params=pltpu.CompilerParams(dimension_semantics=("arbitrary", "arbitrary"), vmem_limit_bytes=VMEM_LIMIT),
    )(a, b, *c_ins)
    return outs if nc else outs[0]


def _exchange_ops(ins, outs, n_big, sems):
    send, recv = sems
    x, y, c, _, _ = _place()
    cps = [pltpu.make_async_remote_copy(
        src_ref=ins[t].at[1 - c] if t < n_big else ins[t], dst_ref=outs[t], send_sem=send.at[t], recv_sem=recv.at[t],
        device_id=(x, y, 1 - c), device_id_type=MESH) for t in range(len(ins))]

    def start():
        for cp in cps:
            cp.start()

    def finish():
        for cp in cps:
            cp.wait()

    return start, finish


def _exchange_shapes(bigs, smalls):
    return [jax.ShapeDtypeStruct((N_CHIPS,) + b.shape[2:], b.dtype) for b in bigs] + [
        jax.ShapeDtypeStruct(s.shape, s.dtype) for s in smalls]


def _comm_plan(comm):
    if comm is None:
        return (), [], [], None
    kind, arrays = comm
    n = len(arrays)
    if kind == "scatter":
        return (tuple(arrays), _scatter_shapes(arrays, ()), _scatter_scratch(arrays, ()),
                lambda i, o, sm: _scatter_ops(i, o, n, sm[:6], sm[6:]))
    def exchange(i, o, sm):
        start, finish = _exchange_ops(i, o, n, sm)
        return start, lambda: None, finish

    return tuple(arrays), _exchange_shapes(arrays, ()), [pltpu.SemaphoreType.DMA((n,))] * 2, exchange


def _sibling_exchange(bigs, smalls, tag):
    nb, nt = len(bigs), len(bigs) + len(smalls)

    def body(*refs):
        start, finish = _exchange_ops(refs[:nt], refs[nt:2 * nt], nb, refs[2 * nt:])
        start()
        finish()

    return pl.pallas_call(
        body, name=f"sibling_exchange_{tag}", out_shape=_exchange_shapes(bigs, smalls),
        in_specs=[ANY] * nt, out_specs=[ANY] * nt,
        scratch_shapes=[pltpu.SemaphoreType.DMA((nt,)), pltpu.SemaphoreType.DMA((nt,))],
    )(*bigs, *smalls)


def _pair_sum(core, mine, theirs, tag, block_rows):
    _, _, rows, cols = mine.shape
    steps = rows // block_rows

    def body(core_ref, a_ref, b_ref, o_ref):
        o_ref[...] = (a_ref[...].astype(F32) + b_ref[...].astype(F32)).astype(BF16)

    grid_spec = pltpu.PrefetchScalarGridSpec(
        num_scalar_prefetch=1, grid=(N_CHIPS, steps),
        in_specs=[pl.BlockSpec((None, None, block_rows, cols), lambda k, r, core_ref: (core_ref[0], k, r, 0)),
                  pl.BlockSpec((None, block_rows, cols), lambda k, r, core_ref: (k, r, 0))],
        out_specs=pl.BlockSpec((None, block_rows, cols), lambda k, r, core_ref: (k, r, 0)),
    )
    return pl.pallas_call(
        body, name=f"pair_sum_{tag}", grid_spec=grid_spec,
        out_shape=jax.ShapeDtypeStruct((N_CHIPS, rows, cols), BF16),
        compiler_params=pltpu.CompilerParams(dimension_semantics=("arbitrary", "arbitrary"), vmem_limit_bytes=VMEM_LIMIT),
    )(core, mine, theirs)


def _pair_sum_small(mine, theirs):
    (m_f2, m_b1, m_b2, m_sf, m_s5, m_sp) = mine

    def body(a0, a1, a2, a3, a4, a5, b0, b1, b2, b3, b4, b5, o_m, o_f, o_5, o_p):
        sm = (a0[...] + a1[...] + a2[...]) + (b0[...] + b1[...] + b2[...])
        sf = a3[...] + b3[...]
        s5 = a4[...] + b4[...]
        for h in range(2):
            o_m[h] = sm[:, h * (D_MODEL // 2):(h + 1) * (D_MODEL // 2)]
            o_f[h] = sf[:, h * (D_FF // 2):(h + 1) * (D_FF // 2)]
            o_5[h] = s5[:, h * (D_CONV // 2):(h + 1) * (D_CONV // 2)]
            for g in range(2):
                o_p[h, g] = a5[2 * h + g] + b5[2 * h + g]

    out_shape = [
        jax.ShapeDtypeStruct((2, 8, D_MODEL // 2), F32), jax.ShapeDtypeStruct((2, 8, D_FF // 2), F32),
        jax.ShapeDtypeStruct((2, 40, D_CONV // 2), F32), jax.ShapeDtypeStruct((2, 2, POOL_GROUP, POOL_GROUP), F32),
    ]
    return pl.pallas_call(body, name="pair_sum_small", out_shape=out_shape, in_specs=[VMEM] * 12, out_specs=[VMEM] * 4)(
        *mine, *theirs)


def _scatter_ops(ins, outs, n_parts, sems, stages):
    ici_send, ici_recv, fwd_send, fwd_recv, loc_in, loc_out = sems
    nt = len(ins)
    x, y, c, k, chips = _place()

    def src_of(t, kk):
        return ins[t].at[kk] if t < n_parts else ins[t].at[c]

    def ici(t, j, kk, slot):
        return pltpu.make_async_remote_copy(
            src_ref=src_of(t, kk), dst_ref=outs[t].at[c, slot], send_sem=ici_send.at[t * 3 + j],
            recv_sem=ici_recv.at[t * 3 + j], device_id=(*chips[j], c), device_id_type=MESH)

    def fwd(t, half):
        slots = outs[t].at[half]
        return pltpu.make_async_remote_copy(
            src_ref=slots, dst_ref=slots, send_sem=fwd_send.at[t], recv_sem=fwd_recv.at[t],
            device_id=(x, y, 1 - c), device_id_type=MESH)

    local = [_staged(src_of(t, k), outs[t].at[c, k], stages[t], loc_in.at[t], loc_out.at[t]) for t in range(nt)]
    peers = [(t, j, 2 * qx + qy) for t in range(nt) for j, (qx, qy) in enumerate(chips)]
    sends = [ici(t, j, kq, k) for t, j, kq in peers]

    def start():
        for cp in local:
            cp[0]()
        for cp in sends:
            cp.start()

    def land():
        for cp in local:
            cp[1]()
        for t, j, kq in peers:
            ici(t, j, kq, kq).wait_recv()
        for cp in local:
            cp[2]()
        for t in range(nt):
            fwd(t, c).start()

    def finish():
        for t in range(nt):
            fwd(t, 1 - c).wait_recv()
            fwd(t, c).wait_send()
        for cp in sends:
            cp.wait_send()

    return start, land, finish


def _scatter_scratch(parts, smalls):
    arrays = tuple(parts) + tuple(smalls)
    nt = len(arrays)
    return ([pltpu.SemaphoreType.DMA((3 * nt,))] * 2 + [pltpu.SemaphoreType.DMA((nt,))] * 4
            + [pltpu.VMEM(a.shape[1:], a.dtype) for a in arrays])


def _scatter_shapes(parts, smalls):
    return [jax.ShapeDtypeStruct((2, N_CHIPS) + p.shape[1:], p.dtype) for p in tuple(parts) + tuple(smalls)]


def _chip_scatter(parts, smalls):
    nt = len(parts) + len(smalls)

    def body(*refs):
        start, land, finish = _scatter_ops(refs[:nt], refs[nt:2 * nt], len(parts), refs[2 * nt:2 * nt + 6], refs[2 * nt + 6:])
        start()
        land()
        finish()

    return pl.pallas_call(
        body, name="chip_scatter", out_shape=_scatter_shapes(parts, smalls), in_specs=[ANY] * nt, out_specs=[ANY] * nt,
        scratch_shapes=_scatter_scratch(parts, smalls),
    )(*parts, *smalls)


def _adamw(w, g, m, v):
    m = ADAM_B1 * m + (1.0 - ADAM_B1) * g
    v = ADAM_B2 * v + (1.0 - ADAM_B2) * (g * g)
    m_hat = m / (1.0 - ADAM_B1 ** ADAM_STEP)
    v_hat = v / (1.0 - ADAM_B2 ** ADAM_STEP)
    delta = -ADAM_LR * (m_hat / (jnp.sqrt(v_hat) + ADAM_EPS) + ADAM_WD * w)
    return delta, m, v


def _adam_big(parts, w, m, v, tag, block_rows):
    _, _, rows, cols = parts.shape
    steps = rows // block_rows

    def body(p_ref, w_ref, m_ref, v_ref, g_out, d_out, m_out, v_out):
        g = p_ref[0].astype(F32)
        for q in range(1, N_CHIPS):
            g = g + p_ref[q].astype(F32)
        delta, m_new, v_new = _adamw(w_ref[...], g, m_ref[...], v_ref[...])
        g_out[...] = g
        d_out[...] = delta
        m_out[...] = m_new
        v_out[...] = v_new

    blk = pl.BlockSpec((block_rows, cols), lambda h, r: (h * steps + r, 0))
    return pl.pallas_call(
        body, name=f"adam_{tag}", grid=(2, steps),
        in_specs=[pl.BlockSpec((None, N_CHIPS, block_rows, cols), lambda h, r: (h, 0, r, 0)), blk, blk, blk],
        out_specs=[blk] * 4, out_shape=[jax.ShapeDtypeStruct(w.shape, F32)] * 4,
        compiler_params=pltpu.CompilerParams(dimension_semantics=("arbitrary", "arbitrary"), vmem_limit_bytes=VMEM_LIMIT),
    )(parts, w, m, v)


def _reduce_small(l_m, l_f, l_5, l_p):
    def total(ref):
        t = ref[:, 0]
        for q in range(1, N_CHIPS):
            t = t + ref[:, q]
        return t

    def body(m_ref, f_ref, s_ref, p_ref, g1_o, g2_o, g3_o, loss_o, wf_o, fb_o, wa_o, cb_o, lg_o, lb_o, ps_o, pw_o):
        tm, tf, t5, tp = total(m_ref), total(f_ref), total(s_ref), total(p_ref)
        sm = jnp.concatenate([tm[0], tm[1]], axis=1)
        sf = jnp.concatenate([tf[0], tf[1]], axis=1)
        s5 = jnp.concatenate([t5[0], t5[1]], axis=1)
        g1_o[...] = sm[0:1]
        g2_o[...] = sm[1:2]
        g3_o[...] = sm[2:3]
        loss_o[...] = sm[3:4, 0:128]
        wf_o[...] = sf
        fb_o[...] = sf[3:4]
        wa_o[...] = s5[0:32]
        cb_o[...] = s5[32:33]
        lg_o[...] = s5[33:34]
        lb_o[...] = s5[34:35]
        ps_o[...] = s5[35:36]
        for h in range(2):
            for g in range(2):
                pw_o[2 * h + g] = tp[h, g]

    row = lambda w: jax.ShapeDtypeStruct((1, w), F32)
    out_shape = [row(D_MODEL), row(D_MODEL), row(D_MODEL), row(128), jax.ShapeDtypeStruct((8, D_FF), F32), row(D_FF),
                 jax.ShapeDtypeStruct((32, D_CONV), F32), row(D_CONV), row(D_CONV), row(D_CONV), row(D_POOL),
                 jax.ShapeDtypeStruct((4, POOL_GROUP, POOL_GROUP), F32)]
    return pl.pallas_call(body, name="reduce_small", out_shape=out_shape, in_specs=[VMEM] * 4, out_specs=[VMEM] * 12)(
        l_m, l_f, l_5, l_p)


def _adam_small(ws, gs, ms, vs):
    count = len(ws)

    def body(*refs):
        w_r, g_r, m_r, v_r = (refs[t * count:(t + 1) * count] for t in range(4))
        d_o, m_o, v_o = (refs[(4 + t) * count:(5 + t) * count] for t in range(3))
        for t in range(count):
            delta, m_new, v_new = _adamw(w_r[t][...], g_r[t][...], m_r[t][...], v_r[t][...])
            d_o[t][...] = delta
            m_o[t][...] = m_new
            v_o[t][...] = v_new

    out_shape = [jax.ShapeDtypeStruct(w.shape, F32) for w in ws] * 3
    outs = pl.pallas_call(body, name="adam_small", out_shape=out_shape, in_specs=[VMEM] * (4 * count),
                          out_specs=[VMEM] * (3 * count))(*ws, *gs, *ms, *vs)
    return outs[:count], outs[count:2 * count], outs[2 * count:]


MIX_TILE = 512
FFN_TILE = 256
GRAD_K = 2048


def kernel(x, norm_mix_g, w_in, conv_a_w, conv_a_b, ln_a_g, ln_a_b, pool_w, pool_scale, w_out, norm_ffn_g, w_up, conv_f_w, conv_f_b, w_down, norm_final_g, loss_target, m_norm_mix_g, m_w_in, m_conv_a_w, m_conv_a_b, m_ln_a_g, m_ln_a_b, m_pool_w, m_pool_scale, m_w_out, m_norm_ffn_g, m_w_up, m_conv_f_w, m_conv_f_b, m_w_down, m_norm_final_g, v_norm_mix_g, v_w_in, v_conv_a_w, v_conv_a_b, v_ln_a_g, v_ln_a_b, v_pool_w, v_pool_scale, v_w_out, v_norm_ffn_g, v_w_up, v_conv_f_w, v_conv_f_b, v_w_down, v_norm_final_g):
    seq = x.shape[1]
    xs, ts = x[0], loss_target[0]
    mix_tile, ffn_tile, grad_k = min(MIX_TILE, seq), min(FFN_TILE, seq), min(GRAD_K, seq)
    chip = 2 * lax.axis_index("x") + lax.axis_index("y")
    core = lax.axis_index("c").astype(jnp.int32).reshape(1)

    wa_s = jnp.pad(conv_a_w[0], ((0, 32 - CONV_A), (0, 0)))
    wf_s = jnp.pad(conv_f_w[0], ((0, 8 - CONV_F), (0, 0)))
    win_b, wout_b, wup_b, wdown_b = _cast_shards(w_in[0], w_out[0], w_up[0], w_down[0])
    g3 = norm_final_g.reshape(1, D_MODEL)
    pw = pool_w[0]

    h1, proj, cpre, dpool, mcat, x1, win, wout, wup, wa_g, wf_g = _mixer_fwd(
        xs, norm_mix_g, win_b, wout_b, wup_b, wa_s, wf_s, conv_a_b, ln_a_g, ln_a_b, pw, pool_scale, mix_tile)
    wa = jnp.transpose(wa_g, (1, 0, 2)).reshape(32, D_CONV)
    wf = jnp.transpose(wf_g, (1, 0, 2)).reshape(8, D_FF)
    h2, up, gcs, act, wdown = _ffn_up(x1, norm_ffn_g, wup, wf, conv_f_b, wdown_b, ffn_tile)
    dx2, dx2b, sm_f2 = _ffn_down(x1, act, wdown, g3, ts, mix_tile)
    tags = ("w_in", "w_out", "w_up", "w_down")
    blocks = (256, 128, 256, 176)
    g_wdown = _weight_grad(act, dx2b, "rows2", grad_k)
    dup, dx1, dx1b, sm_b1, sf, l_wdown = _ffn_bwd(
        dx2, up, gcs, x1, norm_ffn_g, wup, wf, wdown, ("exchange", [g_wdown]), ffn_tile)
    p_wdown = _pair_sum(core, g_wdown, l_wdown, tags[3], blocks[3])
    g_wup, s_wdown = _weight_grad(h2, dup, "cols_chip", grad_k, ("scatter", [p_wdown]))
    g_wout, l_wup = _weight_grad(mcat, dx1b, "rows1", grad_k, ("exchange", [g_wup]))
    p_wup = _pair_sum(core, g_wup, l_wup, tags[2], blocks[2])
    l_wout, = _sibling_exchange((g_wout,), (), "early")
    p_wout = _pair_sum(core, g_wout, l_wout, tags[1], blocks[1])
    dproj, gx, sm_b2, s5, sp, s_wout, s_wup = _mixer_bwd(
        dx1, xs, proj, cpre, dpool, norm_mix_g, win, wa, ln_a_g, ln_a_b, pw, pool_scale, wout, [p_wout, p_wup], mix_tile)
    g_win, grad_x = _weight_grad(h1, dproj, "cols_half", grad_k, carry=gx)

    smalls = (sm_f2, sm_b1, sm_b2, sf, s5, sp)
    landed = _sibling_exchange((g_win,), smalls, "late")
    part_win = _pair_sum(core, g_win, landed[0], tags[0], blocks[0])
    small_parts = _pair_sum_small(smalls, landed[1:])
    late = _chip_scatter([part_win], small_parts)
    scattered = [late[0], s_wout, s_wup, s_wdown] + list(late[1:])

    big_w = (w_in[0], w_out[0], w_up[0], w_down[0])
    big_m = (m_w_in[0], m_w_out[0], m_w_up[0], m_w_down[0])
    big_v = (v_w_in[0], v_w_out[0], v_w_up[0], v_w_down[0])
    big = {}
    for tag, p, w, m, v, br in zip(tags, scattered[:4], big_w, big_m, big_v, blocks):
        big[tag] = [a[None] for a in _adam_big(p, w, m, v, tag, br)]

    (g_g1, g_g2, g_g3, loss_row, g_wf_all, g_fb, g_wa_all, g_cb, g_lg, g_lb, g_ps, g_pw) = _reduce_small(*scattered[4:])
    g_wa = lax.dynamic_slice(g_wa_all, (0, chip * (D_CONV // N_CHIPS)), (32, D_CONV // N_CHIPS))[:CONV_A]
    g_wf = lax.dynamic_slice(g_wf_all, (0, chip * (D_FF // N_CHIPS)), (8, D_FF // N_CHIPS))[:CONV_F]
    small_names = ("norm_mix_g", "conv_a_w", "conv_a_b", "ln_a_g", "ln_a_b", "pool_w", "pool_scale", "norm_ffn_g",
                   "conv_f_w", "conv_f_b", "norm_final_g")
    small_w = (norm_mix_g, conv_a_w[0], conv_a_b, ln_a_g, ln_a_b, pw, pool_scale, norm_ffn_g, conv_f_w[0], conv_f_b, g3)
    small_m = (m_norm_mix_g, m_conv_a_w[0], m_conv_a_b, m_ln_a_g, m_ln_a_b, m_pool_w[0], m_pool_scale, m_norm_ffn_g,
               m_conv_f_w[0], m_conv_f_b, m_norm_final_g.reshape(1, D_MODEL))
    small_v = (v_norm_mix_g, v_conv_a_w[0], v_conv_a_b, v_ln_a_g, v_ln_a_b, v_pool_w[0], v_pool_scale, v_norm_ffn_g,
               v_conv_f_w[0], v_conv_f_b, v_norm_final_g.reshape(1, D_MODEL))
    small_g = (g_g1, g_wa, g_cb, g_lg, g_lb, g_pw, g_ps, g_g2, g_wf, g_fb, g_g3)
    s_delta, s_m, s_v = _adam_small(small_w, small_g, small_m, small_v)
    shapes = {"conv_a_w": conv_a_w.shape, "pool_w": pool_w.shape, "conv_f_w": conv_f_w.shape, "norm_final_g": norm_final_g.shape}
    small = {}
    for t, name in enumerate(small_names):
        shp = shapes.get(name)
        small[name] = [a if shp is None else a.reshape(shp) for a in (small_g[t], s_delta[t], s_m[t], s_v[t])]

    order = ("norm_mix_g", "w_in", "conv_a_w", "conv_a_b", "ln_a_g", "ln_a_b", "pool_w", "pool_scale", "w_out", "norm_ffn_g",
             "w_up", "conv_f_w", "conv_f_b", "w_down", "norm_final_g")
    table = {**big, **small}
    loss = loss_row[0, 0]
    outs = [loss, grad_x[None]]
    for t in range(4):
        outs += [table[name][t] for name in order]
    return tuple(outs)
```

```python
import functools

import jax
import jax.numpy as jnp
from jax import lax
from jax.experimental import pallas as pl
from jax.experimental.pallas import tpu as pltpu

F32 = jnp.float32
BF16 = jnp.bfloat16
EPS = 1e-6
ADAM_LR = 0.001
ADAM_B1 = 0.9
ADAM_B2 = 0.999
ADAM_EPS = 1e-08
ADAM_WD = 0.01
ADAM_STEP = 10

D_MODEL = 1024
D_CONV = 512
D_POOL = 512
D_IN = 1536
D_FF = 2816
CONV_A = 31
CONV_F = 3
POOL_WINDOWS = (2, 4, 8, 16)
POOL_GROUP = 128
N_CHIPS = 4
FF_CHUNK = 256
N_FF_CHUNKS = D_FF // FF_CHUNK
A_HALO = 32
P_HALO = 16
VMEM_LIMIT = 56 * 1024 * 1024
MESH = pl.DeviceIdType.MESH

ANY = pl.BlockSpec(memory_space=pl.ANY)
VMEM = pl.BlockSpec(memory_space=pltpu.VMEM)


def _dot(a, b):
    return jnp.dot(a, b, preferred_element_type=F32)


def _dot_nt(a, b):
    return lax.dot_general(a, b, (((1,), (1,)), ((), ())), preferred_element_type=F32)


def _dot_tn(a, b):
    return lax.dot_general(a, b, (((0,), (0,)), ((), ())), preferred_element_type=F32)


def _sigmoid(v):
    return jax.nn.sigmoid(v)


def _colsum(v):
    return jnp.sum(v, axis=0, keepdims=True)


def _rowmean(v):
    return jnp.mean(v, axis=-1, keepdims=True)


def _place():
    x, y, c = lax.axis_index("x"), lax.axis_index("y"), lax.axis_index("c")
    chips = [(1 - x, y), (x, 1 - y), (1 - x, 1 - y)]
    return x, y, c, 2 * x + y, chips


def _staged(src, dst, stage, sem_in, sem_out):
    hop_in = pltpu.make_async_copy(src, stage, sem_in)
    hop_out = pltpu.make_async_copy(stage, dst, sem_out)

    def relay():
        hop_in.wait()
        hop_out.start()

    return hop_in.start, relay, hop_out.wait


def _gather_ops(bufs, fulls, col_sharded, sems, stages):
    ici_send, ici_recv, fwd_send, fwd_recv, loc_in, loc_out = sems
    n_big = len(bufs)
    x, y, c, k, chips = _place()

    def block(i, kk, half=None):
        rows, cols = bufs[i].shape
        if col_sharded[i]:
            rs = slice(None) if half is None else pl.ds(pl.multiple_of(half * (rows // 2), 16), rows // 2)
            return fulls[i].at[rs, pl.ds(pl.multiple_of(kk * cols, 128), cols)]
        if half is None:
            return fulls[i].at[pl.ds(pl.multiple_of(kk * rows, 16), rows), :]
        return fulls[i].at[pl.ds(pl.multiple_of(kk * rows + half * (rows // 2), 16), rows // 2), :]

    def my_half(i):
        rows = bufs[i].shape[0]
        return bufs[i].at[pl.ds(pl.multiple_of(c * (rows // 2), 16), rows // 2), :]

    def ici(i, j, kk):
        return pltpu.make_async_remote_copy(
            src_ref=my_half(i), dst_ref=block(i, kk, c), send_sem=ici_send.at[i * 3 + j], recv_sem=ici_recv.at[i * 3 + j],
            device_id=(*chips[j], c), device_id_type=MESH)

    def fwd(i, j, kk, half):
        return pltpu.make_async_remote_copy(
            src_ref=block(i, kk, half), dst_ref=block(i, kk, half), send_sem=fwd_send.at[i * 3 + j],
            recv_sem=fwd_recv.at[i * 3 + j], device_id=(x, y, 1 - c), device_id_type=MESH)

    local = [_staged(bufs[i], block(i, k), stages[i], loc_in.at[i], loc_out.at[i]) for i in range(n_big)]
    sends = [ici(i, j, k) for i in range(n_big) for j in range(3)]
    peers = [(i, j, 2 * qx + qy) for i in range(n_big) for j, (qx, qy) in enumerate(chips)]

    def start():
        for cp in local:
            cp[0]()
        for cp in sends:
            cp.start()

    def land():
        for cp in local:
            cp[1]()
        for i, j, kq in peers:
            ici(i, j, kq).wait_recv()
            fwd(i, j, kq, c).start()

    def finish():
        for i, j, kq in peers:
            fwd(i, j, kq, 1 - c).wait_recv()
            fwd(i, j, kq, c).wait_send()
        for cp in sends:
            cp.wait_send()
        for cp in local:
            cp[2]()

    return start, land, finish


def _gather_scratch(shards):
    n_big = len(shards)
    return ([pltpu.SemaphoreType.DMA((3 * n_big,))] * 4 + [pltpu.SemaphoreType.DMA((n_big,))] * 2
            + [pltpu.VMEM(b.shape, b.dtype) for b in shards])


def _tap_ops(srcs, dsts, sems):
    send, recv, loc = sems
    _, _, c, k, chips = _place()

    def copy(t, j, kk):
        return pltpu.make_async_remote_copy(
            src_ref=srcs[t], dst_ref=dsts[t].at[kk], send_sem=send.at[t * 3 + j], recv_sem=recv.at[t * 3 + j],
            device_id=(*chips[j], c), device_id_type=MESH)

    local = [pltpu.make_async_copy(srcs[t], dsts[t].at[k], loc.at[t]) for t in range(len(srcs))]
    sends = [[copy(t, j, k) for j in range(3)] for t in range(len(srcs))]

    def start():
        for t, cp in enumerate(local):
            cp.start()
            for sd in sends[t]:
                sd.start()

    def wait(t):
        for j, (qx, qy) in enumerate(chips):
            copy(t, j, 2 * qx + qy).wait_recv()
        for sd in sends[t]:
            sd.wait_send()
        local[t].wait()

    return start, wait


def _cast_shards(*shards):
    def body(*refs):
        for src, dst in zip(refs[:len(shards)], refs[len(shards):]):
            dst[...] = src[...].astype(BF16)

    return pl.pallas_call(
        body, name="cast_shards", out_shape=[jax.ShapeDtypeStruct(s.shape, BF16) for s in shards],
        in_specs=[VMEM] * len(shards), out_specs=[VMEM] * len(shards),
        compiler_params=pltpu.CompilerParams(vmem_limit_bytes=VMEM_LIMIT),
    )(*shards)


def _load_weights(pairs, sem):
    cps = [pltpu.make_async_copy(src, dst, sem.at[i]) for i, (src, dst) in enumerate(pairs)]
    for cp in cps:
        cp.start()
    for cp in cps:
        cp.wait()


def _shifted_views(buf, shifted, t_rows):
    n = t_rows + A_HALO - 8
    for b in range(1, 8):
        shifted[b - 1] = buf[b:b + n, :]

    def view(offset):
        a, b = divmod(offset, 8)
        if b == 0:
            return buf[8 * a:8 * a + t_rows, :]
        return shifted[b - 1, 8 * a:8 * a + t_rows, :]

    return view


def _pool_count(tile, t_rows, w):
    row = lax.broadcasted_iota(jnp.int32, (t_rows, POOL_GROUP), 0) + tile * t_rows
    return jnp.minimum(row + 1, w).astype(F32)


def _mixer_fwd(x, g1, win_b, wout_b, wup_b, wa_s, wf_s, cb, lg, lb, pw, ps, tile_rows):
    seq = x.shape[0]
    tr = tile_rows
    n = seq // tr

    def body(x_ref, g1_ref, win_b_hbm, wout_b_hbm, wup_b_hbm, wa_s_hbm, wf_s_hbm, cb_ref, lg_ref, lb_ref, pw_ref,
             ps_ref, h1_ref, proj_ref, c_ref, d_ref, m_ref, x1_ref, win_f, wout_f, wup_f, wa_g, wf_g,
             win_v, wout_v, wa_ref, ubuf, ushift, bbuf, sem, *csems):
        i = pl.program_id(0)
        first_sems, first_stages, later_sems, later_stages, tap_sems = (
            csems[0:6], csems[6:8], csems[8:14], csems[14:15], csems[15:18])

        def first():
            return _gather_ops((win_b_hbm, wout_b_hbm), (win_f, wout_f), (True, False), first_sems, first_stages)

        def later():
            return _gather_ops((wup_b_hbm,), (wup_f,), (True,), later_sems, later_stages)

        def taps():
            return _tap_ops((wa_s_hbm, wf_s_hbm), (wa_g, wf_g), tap_sems)

        @pl.when(i == 0)
        def _():
            first()[0]()
            taps()[0]()
            later()[0]()
            first()[1]()
            first()[2]()
            taps()[1](0)
            loads = [(win_f, win_v), (wout_f, wout_v)]
            loads += [(wa_g.at[kk], wa_ref.at[:, kk * (D_CONV // N_CHIPS):(kk + 1) * (D_CONV // N_CHIPS)]) for kk in range(N_CHIPS)]
            _load_weights(loads, sem)
            ubuf[0:A_HALO, :] = jnp.zeros((A_HALO, D_CONV), F32)
            bbuf[0:P_HALO, :] = jnp.zeros((P_HALO, D_POOL), F32)

        xv = x_ref[...]
        r = lax.rsqrt(_rowmean(xv * xv) + EPS)
        h1 = (xv * r * g1_ref[...]).astype(BF16)
        h1_ref[...] = h1
        proj = _dot(h1, win_v[...])
        proj_ref[...] = proj.astype(BF16)
        av, ag, bi = proj[:, :D_CONV], proj[:, D_CONV:2 * D_CONV], proj[:, 2 * D_CONV:]
        ubuf[A_HALO:A_HALO + tr, :] = av * _sigmoid(ag)
        off = A_HALO - (CONV_A - 1)
        uview = _shifted_views(ubuf, ushift, tr)
        acc = wa_ref[0:1, :] * uview(off)
        for j in range(1, CONV_A):
            acc = acc + wa_ref[j:j + 1, :] * uview(off + j)
        cv = acc + cb_ref[...]
        ubuf[0:A_HALO, :] = ubuf[tr:tr + A_HALO, :]
        c_ref[...] = cv.astype(BF16)
        xc = cv - _rowmean(cv)
        z = xc * lax.rsqrt(_rowmean(xc * xc) + EPS)
        ln = z * lg_ref[...] + lb_ref[...]
        ya = ln * _sigmoid(ln)
        bbuf[P_HALO:P_HALO + tr, :] = bi
        ds, ybs = [], []
        for g, w in enumerate(POOL_WINDOWS):
            cols = slice(g * POOL_GROUP, (g + 1) * POOL_GROUP)
            s = bi[:, cols]
            for kk in range(1, w):
                s = s + bbuf[P_HALO - kk:P_HALO - kk + tr, cols]
            dg = s / _pool_count(i, tr, w) - bi[:, cols]
            ds.append(dg)
            ybs.append(_dot(dg.astype(BF16), pw_ref[g].astype(BF16)))
        bbuf[0:P_HALO, :] = bbuf[tr:tr + P_HALO, :]
        d_ref[...] = jnp.concatenate(ds, axis=1).astype(BF16)
        yb = jnp.concatenate(ybs, axis=1) * ps_ref[...]
        m = jnp.concatenate([ya, yb], axis=1).astype(BF16)
        m_ref[...] = m
        x1_ref[...] = xv + _dot(m, wout_v[...])

        @pl.when(i == n - 1)
        def _():
            later()[1]()
            later()[2]()
            taps()[1](1)

    tile = lambda w: pl.BlockSpec((tr, w), lambda i: (i, 0))
    full = lambda a: pl.BlockSpec(a.shape, lambda i: (0,) * a.ndim)
    return pl.pallas_call(
        body, name="mixer_fwd", grid=(n,),
        in_specs=[tile(D_MODEL), full(g1)] + [ANY] * 5 + [full(cb), full(lg), full(lb), full(pw), full(ps)],
        out_specs=[tile(D_MODEL), tile(D_IN), tile(D_CONV), tile(D_POOL), tile(D_MODEL), tile(D_MODEL)] + [ANY] * 5,
        out_shape=[
            jax.ShapeDtypeStruct((seq, D_MODEL), BF16), jax.ShapeDtypeStruct((seq, D_IN), BF16),
            jax.ShapeDtypeStruct((seq, D_CONV), BF16), jax.ShapeDtypeStruct((seq, D_POOL), BF16),
            jax.ShapeDtypeStruct((seq, D_MODEL), BF16), jax.ShapeDtypeStruct((seq, D_MODEL), F32),
            jax.ShapeDtypeStruct((D_MODEL, D_IN), BF16), jax.ShapeDtypeStruct((D_MODEL, D_MODEL), BF16),
            jax.ShapeDtypeStruct((D_MODEL, 2 * D_FF), BF16),
            jax.ShapeDtypeStruct((N_CHIPS,) + wa_s.shape, F32), jax.ShapeDtypeStruct((N_CHIPS,) + wf_s.shape, F32),
        ],
        scratch_shapes=[
            pltpu.VMEM((D_MODEL, D_IN), BF16), pltpu.VMEM((D_MODEL, D_MODEL), BF16), pltpu.VMEM((32, D_CONV), F32),
            pltpu.VMEM((tr + A_HALO, D_CONV), F32), pltpu.VMEM((7, tr + A_HALO - 8, D_CONV), F32),
            pltpu.VMEM((tr + P_HALO, D_POOL), F32), pltpu.SemaphoreType.DMA((2 + N_CHIPS,)),
        ] + _gather_scratch((win_b, wout_b)) + _gather_scratch((wup_b,)) + [
            pltpu.SemaphoreType.DMA((6,)), pltpu.SemaphoreType.DMA((6,)), pltpu.SemaphoreType.DMA((2,))],
        compiler_params=pltpu.CompilerParams(dimension_semantics=("arbitrary",), vmem_limit_bytes=VMEM_LIMIT),
    )(x, g1, win_b, wout_b, wup_b, wa_s, wf_s, cb, lg, lb, pw, ps)


def _ffn_up(x1, g2, wup, wf, fb, wdown_b, tile_rows):
    seq = x1.shape[0]
    tr = tile_rows
    n = seq // tr

    def body(x1_ref, g2_ref, wup_hbm, wf_ref, fb_ref, wdown_b_hbm,
             h2_ref, up_ref, gc_ref, act_ref, wdown_f, wup_v, gbuf, sem, *gsems):
        i = pl.program_id(0)

        def gather():
            return _gather_ops((wdown_b_hbm,), (wdown_f,), (False,), gsems[:6], gsems[6:])

        @pl.when(i == 0)
        def _():
            gather()[0]()
            _load_weights(((wup_hbm, wup_v),), sem)
            gbuf[0:8, :] = jnp.zeros((8, D_FF), F32)

        x1v = x1_ref[...]
        r2 = lax.rsqrt(_rowmean(x1v * x1v) + EPS)
        h2 = (x1v * r2 * g2_ref[...]).astype(BF16)
        h2_ref[...] = h2

        def up_proj(j):
            return (_dot(h2, wup_v[:, j * FF_CHUNK:(j + 1) * FF_CHUNK]),
                    _dot(h2, wup_v[:, D_FF + j * FF_CHUNK:D_FF + (j + 1) * FF_CHUNK]))

        ahead = up_proj(0)
        for j in range(N_FF_CHUNKS):
            cs = slice(j * FF_CHUNK, (j + 1) * FF_CHUNK)
            vs = slice(D_FF + j * FF_CHUNK, D_FF + (j + 1) * FF_CHUNK)
            gate, val = ahead
            if j + 1 < N_FF_CHUNKS:
                ahead = up_proj(j + 1)
            up_ref[:, cs] = gate.astype(BF16)
            up_ref[:, vs] = val.astype(BF16)
            gbuf[8:8 + tr, cs] = gate
            gc = (wf_ref[0:1, cs] * gbuf[6:6 + tr, cs] + wf_ref[1:2, cs] * gbuf[7:7 + tr, cs]
                  + wf_ref[2:3, cs] * gate + fb_ref[:, cs])
            gbuf[0:8, cs] = gbuf[tr:tr + 8, cs]
            gc_ref[:, cs] = gc.astype(BF16)
            act_ref[:, cs] = (gc * _sigmoid(gc) * val).astype(BF16)

        @pl.when(i == max(n - 2, 0))
        def _():
            gather()[1]()

        @pl.when(i == n - 1)
        def _():
            gather()[2]()

    tile = lambda w: pl.BlockSpec((tr, w), lambda i: (i, 0))
    full = lambda a: pl.BlockSpec(a.shape, lambda i: (0,) * a.ndim)
    return pl.pallas_call(
        body, name="ffn_up", grid=(n,),
        in_specs=[tile(D_MODEL), full(g2), ANY, full(wf), full(fb), ANY],
        out_specs=[tile(D_MODEL), tile(2 * D_FF), tile(D_FF), tile(D_FF), ANY],
        out_shape=[
            jax.ShapeDtypeStruct((seq, D_MODEL), BF16), jax.ShapeDtypeStruct((seq, 2 * D_FF), BF16),
            jax.ShapeDtypeStruct((seq, D_FF), BF16), jax.ShapeDtypeStruct((seq, D_FF), BF16),
            jax.ShapeDtypeStruct((D_FF, D_MODEL), BF16),
        ],
        scratch_shapes=[pltpu.VMEM(wup.shape, BF16), pltpu.VMEM((tr + 8, D_FF), F32), pltpu.SemaphoreType.DMA((1,))]
        + _gather_scratch((wdown_b,)),
        compiler_params=pltpu.CompilerParams(dimension_semantics=("arbitrary",), vmem_limit_bytes=VMEM_LIMIT),
    )(x1, g2, wup, wf, fb, wdown_b)


def _ffn_down(x1, act, wdown, g3, target, tile_rows):
    seq = x1.shape[0]
    tr = tile_rows
    n = seq // tr

    def body(x1_ref, act_ref, wdown_hbm, g3_ref, t_ref, dx2_ref, dx2b_ref, sm_ref, wdown_v, sem):
        i = pl.program_id(0)

        @pl.when(i == 0)
        def _():
            _load_weights(((wdown_hbm, wdown_v),), sem)
            sm_ref[...] = jnp.zeros(sm_ref.shape, F32)

        x2 = x1_ref[...] + _dot(act_ref[...], wdown_v[...])
        r3 = lax.rsqrt(_rowmean(x2 * x2) + EPS)
        n3 = x2 * r3
        err = n3 * g3_ref[...] - t_ref[...]
        dy = err / D_MODEL
        sm_ref[2:3, :] += _colsum(dy * n3)
        loss = 0.5 * _colsum(_rowmean(err * err))
        sm_ref[3:4, :] += jnp.broadcast_to(loss, (1, D_MODEL))
        dn = dy * g3_ref[...]
        dx2v = r3 * (dn - n3 * _rowmean(dn * n3))
        dx2_ref[...] = dx2v
        dx2b_ref[...] = dx2v.astype(BF16)

    tile = lambda w: pl.BlockSpec((tr, w), lambda i: (i, 0))
    full = lambda a: pl.BlockSpec(a.shape, lambda i: (0,) * a.ndim)
    return pl.pallas_call(
        body, name="ffn_down", grid=(n,),
        in_specs=[tile(D_MODEL), tile(D_FF), ANY, full(g3), tile(D_MODEL)],
        out_specs=[tile(D_MODEL), tile(D_MODEL), pl.BlockSpec((8, D_MODEL), lambda i: (0, 0))],
        out_shape=[
            jax.ShapeDtypeStruct((seq, D_MODEL), F32), jax.ShapeDtypeStruct((seq, D_MODEL), BF16),
            jax.ShapeDtypeStruct((8, D_MODEL), F32),
        ],
        scratch_shapes=[pltpu.VMEM(wdown.shape, BF16), pltpu.SemaphoreType.DMA((1,))],
        compiler_params=pltpu.CompilerParams(dimension_semantics=("arbitrary",), vmem_limit_bytes=VMEM_LIMIT),
    )(x1, act, wdown, g3, target)


def _ffn_bwd(dx2, up, gcs, x1, g2, wup, wf, wdown, comm, tile_rows):
    seq = x1.shape[0]
    c_ins, c_shapes, c_sems, c_ops = _comm_plan(comm)
    nc = len(c_ins)
    tr = tile_rows
    n = seq // tr

    def body(dx2_ref, up_ref, gc_ref, x1_ref, g2_ref, wup_hbm, wf_ref, wdown_hbm, *rest):
        c_in, rest = rest[:nc], rest[nc:]
        dup_ref, dx1_ref, dx1b_ref, sm_ref, sf_ref = rest[:5]
        c_out, rest = rest[5:5 + nc], rest[5 + nc:]
        wup_v, wdown_v, dbuf, dcar, sem = rest[:5]
        c_sem_refs = rest[5:]
        i = pl.program_id(0)

        @pl.when(i == 0)
        def _():
            c_ops(c_in, c_out, c_sem_refs)[0]()
            _load_weights(((wup_hbm, wup_v), (wdown_hbm, wdown_v)), sem)
            dcar[...] = jnp.zeros(dcar.shape, F32)
            sm_ref[...] = jnp.zeros(sm_ref.shape, F32)
            sf_ref[...] = jnp.zeros(sf_ref.shape, F32)

        dx2v = dx2_ref[...]
        dx2b = dx2v.astype(BF16)
        dh2 = jnp.zeros((tr, D_MODEL), F32)

        def down_t(j):
            return _dot_nt(dx2b, wdown_v[j * FF_CHUNK:(j + 1) * FF_CHUNK, :])

        ahead = down_t(0)
        for j in range(N_FF_CHUNKS):
            cs = slice(j * FF_CHUNK, (j + 1) * FF_CHUNK)
            vs = slice(D_FF + j * FF_CHUNK, D_FF + (j + 1) * FF_CHUNK)
            dact = ahead
            if j + 1 < N_FF_CHUNKS:
                ahead = down_t(j + 1)
            gate = up_ref[:, cs].astype(F32)
            val = up_ref[:, vs].astype(F32)
            gc = gc_ref[:, cs].astype(F32)
            sg = _sigmoid(gc)
            dval = dact * (gc * sg)
            dgc = dact * val * (sg * (1.0 + gc * (1.0 - sg)))
            dbuf[0:tr, :] = dgc
            dbuf[tr:tr + 8, :] = dcar[:, cs]
            d_p1 = dbuf[1:1 + tr, :]
            d_p2 = dbuf[2:2 + tr, :]
            dgate = wf_ref[2:3, cs] * dgc + wf_ref[1:2, cs] * d_p1 + wf_ref[0:1, cs] * d_p2
            dcar[:, cs] = dgc[0:8, :]
            sf_ref[0:1, cs] += _colsum(d_p2 * gate)
            sf_ref[1:2, cs] += _colsum(d_p1 * gate)
            sf_ref[2:3, cs] += _colsum(dgc * gate)
            sf_ref[3:4, cs] += _colsum(dgc)
            dgb, dvb = dgate.astype(BF16), dval.astype(BF16)
            dup_ref[:, cs] = dgb
            dup_ref[:, vs] = dvb
            dh2 = dh2 + _dot_nt(dgb, wup_v[:, cs]) + _dot_nt(dvb, wup_v[:, vs])
        x1v = x1_ref[...]
        r2 = lax.rsqrt(_rowmean(x1v * x1v) + EPS)
        n2 = x1v * r2
        sm_ref[1:2, :] += _colsum(dh2 * n2)
        dn2 = dh2 * g2_ref[...]
        dx1v = dx2v + r2 * (dn2 - n2 * _rowmean(dn2 * n2))
        dx1_ref[...] = dx1v
        dx1b_ref[...] = dx1v.astype(BF16)

        @pl.when(i == n - 1)
        def _():
            c_ops(c_in, c_out, c_sem_refs)[2]()

    tile = lambda w: pl.BlockSpec((tr, w), lambda i: (n - 1 - i, 0))
    full = lambda a: pl.BlockSpec(a.shape, lambda i: (0,) * a.ndim)
    acc = lambda rows, w: pl.BlockSpec((rows, w), lambda i: (0, 0))
    return pl.pallas_call(
        body, name="ffn_bwd", grid=(n,),
        in_specs=[tile(D_MODEL), tile(2 * D_FF), tile(D_FF), tile(D_MODEL), full(g2), ANY, full(wf), ANY] + [ANY] * nc,
        out_specs=[tile(2 * D_FF), tile(D_MODEL), tile(D_MODEL), acc(8, D_MODEL), acc(8, D_FF)] + [ANY] * nc,
        out_shape=[
            jax.ShapeDtypeStruct((seq, 2 * D_FF), BF16), jax.ShapeDtypeStruct((seq, D_MODEL), F32),
            jax.ShapeDtypeStruct((seq, D_MODEL), BF16), jax.ShapeDtypeStruct((8, D_MODEL), F32),
            jax.ShapeDtypeStruct((8, D_FF), F32),
        ] + c_shapes,
        scratch_shapes=[
            pltpu.VMEM(wup.shape, BF16), pltpu.VMEM(wdown.shape, BF16),
            pltpu.VMEM((tr + 8, FF_CHUNK), F32), pltpu.VMEM((8, D_FF), F32), pltpu.SemaphoreType.DMA((2,)),
        ] + c_sems,
        compiler_params=pltpu.CompilerParams(dimension_semantics=("arbitrary",), vmem_limit_bytes=VMEM_LIMIT),
    )(dx2, up, gcs, x1, g2, wup, wf, wdown, *c_ins)


def _mixer_bwd(dx1, x, proj, cpre, d, g1, win, wa, lg, lb, pw, ps, wout, parts, tile_rows):
    seq = x.shape[0]
    n_parts = len(parts)
    tr = tile_rows
    n = seq // tr
    row_cb, row_lg, row_lb, row_ps = 32, 33, 34, 35

    def body(dx1_ref, x_ref, proj_ref, projh_ref, c_ref, d_ref, g1_ref, win_hbm, wa_ref, lg_ref, lb_ref, pw_ref, ps_ref,
             wout_hbm, *rest):
        part_refs, rest = rest[:n_parts], rest[n_parts:]
        dproj_ref, gx_ref, sm_ref, s5_ref, sp_ref = rest[:5]
        land_refs, rest = rest[5:5 + n_parts], rest[5 + n_parts:]
        win_v, wout_v, ubuf, ushift, dcbuf, dshift, ebuf, sem = rest[:8]
        ssems = rest[8:]
        i = pl.program_id(0)
        tile = n - 1 - i

        def scatter():
            return _scatter_ops(part_refs, land_refs, n_parts, ssems[:6], ssems[6:])

        @pl.when(i == 0)
        def _():
            scatter()[0]()
            _load_weights(((win_hbm, win_v), (wout_hbm, wout_v)), sem)
            dcbuf[tr:tr + A_HALO, :] = jnp.zeros((A_HALO, D_CONV), F32)
            ebuf[tr:tr + P_HALO, :] = jnp.zeros((P_HALO, D_POOL), F32)
            sm_ref[...] = jnp.zeros(sm_ref.shape, F32)
            s5_ref[...] = jnp.zeros(s5_ref.shape, F32)
            sp_ref[...] = jnp.zeros(sp_ref.shape, F32)

        dx1v = dx1_ref[...]
        dm = _dot_nt(dx1v.astype(BF16), wout_v[...])
        dya, dyb = dm[:, :D_CONV], dm[:, D_CONV:]
        dbis = []
        for g, w in enumerate(POOL_WINDOWS):
            cols = slice(g * POOL_GROUP, (g + 1) * POOL_GROUP)
            dgb = d_ref[:, cols]
            pwb = pw_ref[g].astype(BF16)
            dyg = dyb[:, cols]
            s5_ref[row_ps:row_ps + 1, cols] += _colsum(dyg * _dot(dgb, pwb))
            dqb = (dyg * ps_ref[:, cols]).astype(BF16)
            sp_ref[g] += _dot_tn(dgb, dqb)
            dd = _dot_nt(dqb, pwb)
            e = dd / _pool_count(tile, tr, w)
            ebuf[0:tr, cols] = e
            s = e
            for kk in range(1, w):
                s = s + ebuf[kk:kk + tr, cols]
            dbis.append(s - dd)
        ebuf[tr:tr + P_HALO, :] = ebuf[0:P_HALO, :]
        cv = c_ref[...].astype(F32)
        xc = cv - _rowmean(cv)
        rs = lax.rsqrt(_rowmean(xc * xc) + EPS)
        z = xc * rs
        ln = z * lg_ref[...] + lb_ref[...]
        sl = _sigmoid(ln)
        dl = dya * (sl * (1.0 + ln * (1.0 - sl)))
        s5_ref[row_lg:row_lg + 1, :] += _colsum(dl * z)
        s5_ref[row_lb:row_lb + 1, :] += _colsum(dl)
        dz = dl * lg_ref[...]
        dc = rs * (dz - _rowmean(dz) - z * _rowmean(dz * z))
        s5_ref[row_cb:row_cb + 1, :] += _colsum(dc)
        dcbuf[0:tr, :] = dc
        keep = (tile > 0).astype(F32)
        avh = projh_ref[:, :D_CONV].astype(F32)
        agh = projh_ref[:, D_CONV:].astype(F32)
        ubuf[0:A_HALO, :] = avh * _sigmoid(agh) * keep
        av = proj_ref[:, :D_CONV].astype(F32)
        ag = proj_ref[:, D_CONV:2 * D_CONV].astype(F32)
        sg = _sigmoid(ag)
        ubuf[A_HALO:A_HALO + tr, :] = av * sg
        off = A_HALO - (CONV_A - 1)
        du = wa_ref[CONV_A - 1:CONV_A, :] * dc
        dview = _shifted_views(dcbuf, dshift, tr)
        uview = _shifted_views(ubuf, ushift, tr)
        for j in range(CONV_A - 1):
            du = du + wa_ref[j:j + 1, :] * dview(CONV_A - 1 - j)
        for j in range(CONV_A):
            s5_ref[j:j + 1, :] += _colsum(dc * uview(off + j))
        dcbuf[tr:tr + A_HALO, :] = dcbuf[0:A_HALO, :]
        dav = du * sg
        dag = du * av * (sg * (1.0 - sg))
        dprojb = jnp.concatenate([dav, dag] + dbis, axis=1).astype(BF16)
        dproj_ref[...] = dprojb
        dh1 = _dot_nt(dprojb, win_v[...])
        xv = x_ref[...]
        r1 = lax.rsqrt(_rowmean(xv * xv) + EPS)
        n1 = xv * r1
        sm_ref[0:1, :] += _colsum(dh1 * n1)
        dn1 = dh1 * g1_ref[...]
        gx_ref[...] = dx1v + r1 * (dn1 - n1 * _rowmean(dn1 * n1))

        @pl.when(i == max(n - 2, 0))
        def _():
            scatter()[1]()

        @pl.when(i == n - 1)
        def _():
            scatter()[2]()

    tile = lambda w: pl.BlockSpec((tr, w), lambda i: (n - 1 - i, 0))
    full = lambda a: pl.BlockSpec(a.shape, lambda i: (0,) * a.ndim)
    halo = pl.BlockSpec((A_HALO, 2 * D_CONV), lambda i: (jnp.maximum((n - 1 - i) * (tr // A_HALO) - 1, 0), 0))
    acc = lambda shape: pl.BlockSpec(shape, lambda i: (0,) * len(shape))
    return pl.pallas_call(
        body, name="mixer_bwd", grid=(n,),
        in_specs=[tile(D_MODEL), tile(D_MODEL), tile(D_IN), halo, tile(D_CONV), tile(D_POOL), full(g1), ANY, full(wa),
                  full(lg), full(lb), full(pw), full(ps), ANY] + [ANY] * n_parts,
        out_specs=[tile(D_IN), tile(D_MODEL), acc((8, D_MODEL)), acc((40, D_CONV)), acc(pw.shape)] + [ANY] * n_parts,
        out_shape=[
            jax.ShapeDtypeStruct((seq, D_IN), BF16), jax.ShapeDtypeStruct((seq, D_MODEL), F32),
            jax.ShapeDtypeStruct((8, D_MODEL), F32), jax.ShapeDtypeStruct((40, D_CONV), F32),
            jax.ShapeDtypeStruct(pw.shape, F32),
        ] + _scatter_shapes(parts, ()),
        scratch_shapes=[
            pltpu.VMEM(win.shape, BF16), pltpu.VMEM(wout.shape, BF16),
            pltpu.VMEM((tr + A_HALO, D_CONV), F32), pltpu.VMEM((7, tr + A_HALO - 8, D_CONV), F32),
            pltpu.VMEM((tr + A_HALO, D_CONV), F32), pltpu.VMEM((7, tr + A_HALO - 8, D_CONV), F32),
            pltpu.VMEM((tr + P_HALO, D_POOL), F32), pltpu.SemaphoreType.DMA((2,)),
        ] + _scatter_scratch(parts, ()),
        compiler_params=pltpu.CompilerParams(dimension_semantics=("arbitrary",), vmem_limit_bytes=VMEM_LIMIT),
    )(dx1, x, proj, proj, cpre, d, g1, win, wa, lg, lb, pw, ps, wout, *parts)


def _weight_grad(a, b, layout, k_rows, comm=None, carry=None):
    seq, m_dim = a.shape
    n_dim = b.shape[1]
    steps = seq // k_rows

    def store(o_ref, acc, index, value):
        if steps == 1:
            o_ref[index] = value.astype(BF16)
            return
        s = pl.program_id(1)

        @pl.when(s == 0)
        def _():
            acc[index] = value

        @pl.when(jnp.logical_and(s > 0, s < steps - 1))
        def _():
            acc[index] += value

        @pl.when(s == steps - 1)
        def _():
            o_ref[index] = (acc[index] + value).astype(BF16)

    if layout in ("rows1", "rows2"):
        groups = int(layout[-1])
        per_tile = N_CHIPS // groups
        rows = m_dim // N_CHIPS // 2
        a_w = m_dim // groups

        def body(a_ref, b_ref, o_ref, acc):
            r = _dot_tn(a_ref[...], b_ref[...])
            for p in range(per_tile):
                for h in range(2):
                    store(o_ref, acc, (h, p), r[(2 * p + h) * rows:(2 * p + h + 1) * rows, :])

        in_specs = [pl.BlockSpec((k_rows, a_w), lambda g, s: (s, g)), pl.BlockSpec((k_rows, n_dim), lambda g, s: (s, 0))]
        out_spec = pl.BlockSpec((2, per_tile, rows, n_dim), lambda g, s: (0, g, 0, 0))
        out_dims, acc_dims = (2, N_CHIPS, rows, n_dim), (2, per_tile, rows, n_dim)
    elif layout == "cols_chip":
        groups = N_CHIPS
        rows, cols = m_dim // 2, n_dim // N_CHIPS

        def body(a_ref, b_ref, o_ref, acc):
            r = _dot_tn(a_ref[...], b_ref[...])
            for h in range(2):
                store(o_ref, acc, h, r[h * rows:(h + 1) * rows, :])

        in_specs = [pl.BlockSpec((k_rows, m_dim), lambda g, s: (s, 0)), pl.BlockSpec((k_rows, cols), lambda g, s: (s, g))]
        out_spec = pl.BlockSpec((2, None, rows, cols), lambda g, s: (0, g, 0, 0))
        out_dims, acc_dims = (2, N_CHIPS, rows, cols), (2, rows, cols)
    else:
        groups = 2
        rows, cols = m_dim // 2, n_dim // N_CHIPS

        def body(a_ref, b_ref, o_ref, acc):
            r = _dot_tn(a_ref[...], b_ref[...])
            for k in range(N_CHIPS):
                store(o_ref, acc, k, r[:, k * cols:(k + 1) * cols])

        in_specs = [pl.BlockSpec((k_rows, rows), lambda g, s: (s, g)), pl.BlockSpec((k_rows, n_dim), lambda g, s: (s, 0))]
        out_spec = pl.BlockSpec((None, N_CHIPS, rows, cols), lambda g, s: (g, 0, 0, 0))
        out_dims, acc_dims = (2, N_CHIPS, rows, cols), (N_CHIPS, rows, cols)

    c_ins, c_shapes, c_sems, c_ops = _comm_plan(comm)
    nc = len(c_ins)
    c_specs = [ANY] * nc
    if carry is not None:
        assert comm is None and carry.shape[0] % (groups * steps) == 0
        carry_spec = pl.BlockSpec((carry.shape[0] // (groups * steps), carry.shape[1]), lambda g, s: (g * steps + s, 0))
        c_ins, c_shapes, c_specs, nc = (carry,), [jax.ShapeDtypeStruct(carry.shape, carry.dtype)], [carry_spec], 1

    def hosted(a_ref, b_ref, *rest):
        c_in, o_ref, c_out, acc, sems = rest[:nc], rest[nc], rest[nc + 1:2 * nc + 1], rest[2 * nc + 1], rest[2 * nc + 2:]
        g, s = pl.program_id(0), pl.program_id(1)
        if carry is not None:
            c_out[0][...] = c_in[0][...]
            body(a_ref, b_ref, o_ref, acc)
            return
        if nc:
            @pl.when(jnp.logical_and(g == 0, s == 0))
            def _():
                c_ops(c_in, c_out, sems)[0]()

        body(a_ref, b_ref, o_ref, acc)
        if nc:
            step = g * steps + s

            @pl.when(step == max(groups * steps - 2, 0))
            def _():
                c_ops(c_in, c_out, sems)[1]()

            @pl.when(step == groups * steps - 1)
            def _():
                c_ops(c_in, c_out, sems)[2]()

    outs = pl.pallas_call(
        hosted, name=f"weight_grad_{layout}_{m_dim}x{n_dim}", grid=(groups, steps),
        in_specs=in_specs + c_specs, out_specs=[out_spec] + c_specs,
        out_shape=[jax.ShapeDtypeStruct(out_dims, BF16)] + c_shapes,
        scratch_shapes=[pltpu.VMEM(acc_dims, F32)] + c_sems,
        compiler_params=pltpu.CompilerParams(dimension_semantics=("arbitrary", "arbitrary"), vmem_limit_bytes=VMEM_LIMIT),
    )(a, b, *c_ins)
    return outs if nc else outs[0]


def _exchange_ops(ins, outs, n_big, sems):
    send, recv = sems
    x, y, c, _, _ = _place()
    cps = [pltpu.make_async_remote_copy(
        src_ref=ins[t].at[1 - c] if t < n_big else ins[t], dst_ref=outs[t], send_sem=send.at[t], recv_sem=recv.at[t],
        device_id=(x, y, 1 - c), device_id_type=MESH) for t in range(len(ins))]

    def start():
        for cp in cps:
            cp.start()

    def finish():
        for cp in cps:
            cp.wait()

    return start, finish


def _exchange_shapes(bigs, smalls):
    return [jax.ShapeDtypeStruct((N_CHIPS,) + b.shape[2:], b.dtype) for b in bigs] + [
        jax.ShapeDtypeStruct(s.shape, s.dtype) for s in smalls]


def _comm_plan(comm):
    if comm is None:
        return (), [], [], None
    kind, arrays = comm
    n = len(arrays)
    if kind == "scatter":
        return (tuple(arrays), _scatter_shapes(arrays, ()), _scatter_scratch(arrays, ()),
                lambda i, o, sm: _scatter_ops(i, o, n, sm[:6], sm[6:]))
    def exchange(i, o, sm):
        start, finish = _exchange_ops(i, o, n, sm)
        return start, lambda: None, finish

    return tuple(arrays), _exchange_shapes(arrays, ()), [pltpu.SemaphoreType.DMA((n,))] * 2, exchange


def _sibling_exchange(bigs, smalls, tag):
    nb, nt = len(bigs), len(bigs) + len(smalls)

    def body(*refs):
        start, finish = _exchange_ops(refs[:nt], refs[nt:2 * nt], nb, refs[2 * nt:])
        start()
        finish()

    return pl.pallas_call(
        body, name=f"sibling_exchange_{tag}", out_shape=_exchange_shapes(bigs, smalls),
        in_specs=[ANY] * nt, out_specs=[ANY] * nt,
        scratch_shapes=[pltpu.SemaphoreType.DMA((nt,)), pltpu.SemaphoreType.DMA((nt,))],
    )(*bigs, *smalls)


def _pair_sum(core, mine, theirs, tag, block_rows):
    _, _, rows, cols = mine.shape
    steps = rows // block_rows

    def body(core_ref, a_ref, b_ref, o_ref):
        o_ref[...] = (a_ref[...].astype(F32) + b_ref[...].astype(F32)).astype(BF16)

    grid_spec = pltpu.PrefetchScalarGridSpec(
        num_scalar_prefetch=1, grid=(N_CHIPS, steps),
        in_specs=[pl.BlockSpec((None, None, block_rows, cols), lambda k, r, core_ref: (core_ref[0], k, r, 0)),
                  pl.BlockSpec((None, block_rows, cols), lambda k, r, core_ref: (k, r, 0))],
        out_specs=pl.BlockSpec((None, block_rows, cols), lambda k, r, core_ref: (k, r, 0)),
    )
    return pl.pallas_call(
        body, name=f"pair_sum_{tag}", grid_spec=grid_spec,
        out_shape=jax.ShapeDtypeStruct((N_CHIPS, rows, cols), BF16),
        compiler_params=pltpu.CompilerParams(dimension_semantics=("arbitrary", "arbitrary"), vmem_limit_bytes=VMEM_LIMIT),
    )(core, mine, theirs)


def _pair_sum_small(mine, theirs):
    (m_f2, m_b1, m_b2, m_sf, m_s5, m_sp) = mine

    def body(a0, a1, a2, a3, a4, a5, b0, b1, b2, b3, b4, b5, o_m, o_f, o_5, o_p):
        sm = (a0[...] + a1[...] + a2[...]) + (b0[...] + b1[...] + b2[...])
        sf = a3[...] + b3[...]
        s5 = a4[...] + b4[...]
        for h in range(2):
            o_m[h] = sm[:, h * (D_MODEL // 2):(h + 1) * (D_MODEL // 2)]
            o_f[h] = sf[:, h * (D_FF // 2):(h + 1) * (D_FF // 2)]
            o_5[h] = s5[:, h * (D_CONV // 2):(h + 1) * (D_CONV // 2)]
            for g in range(2):
                o_p[h, g] = a5[2 * h + g] + b5[2 * h + g]

    out_shape = [
        jax.ShapeDtypeStruct((2, 8, D_MODEL // 2), F32), jax.ShapeDtypeStruct((2, 8, D_FF // 2), F32),
        jax.ShapeDtypeStruct((2, 40, D_CONV // 2), F32), jax.ShapeDtypeStruct((2, 2, POOL_GROUP, POOL_GROUP), F32),
    ]
    return pl.pallas_call(body, name="pair_sum_small", out_shape=out_shape, in_specs=[VMEM] * 12, out_specs=[VMEM] * 4)(
        *mine, *theirs)


def _scatter_ops(ins, outs, n_parts, sems, stages, landed=False):
    ici_send, ici_recv, fwd_send, fwd_recv, loc_in, loc_out = sems
    nt = len(ins)
    x, y, c, k, chips = _place()

    def src_of(t, kk):
        return ins[t].at[kk] if t < n_parts else ins[t].at[c]

    def ici(t, j, kk, slot):
        return pltpu.make_async_remote_copy(
            src_ref=src_of(t, kk), dst_ref=outs[t].at[c, slot], send_sem=ici_send.at[t * 3 + j],
            recv_sem=ici_recv.at[t * 3 + j], device_id=(*chips[j], c), device_id_type=MESH)

    def fwd(t, half):
        slots = outs[t].at[half]
        return pltpu.make_async_remote_copy(
            src_ref=slots, dst_ref=slots, send_sem=fwd_send.at[t], recv_sem=fwd_recv.at[t],
            device_id=(x, y, 1 - c), device_id_type=MESH)

    local = [_staged(src_of(t, k), outs[t].at[c, k], stages[t], loc_in.at[t], loc_out.at[t]) for t in range(nt)]
    peers = [(t, j, 2 * qx + qy) for t in range(nt) for j, (qx, qy) in enumerate(chips)]
    sends = [] if landed else [ici(t, j, kq, k) for t, j, kq in peers]

    def start():
        for cp in local:
            cp[0]()
        for cp in sends:
            cp.start()

    def land():
        for cp in local:
            cp[1]()
        if not landed:
            for t, j, kq in peers:
                ici(t, j, kq, kq).wait_recv()
        for cp in local:
            cp[2]()
        for t in range(nt):
            fwd(t, c).start()

    def finish():
        for t in range(nt):
            fwd(t, 1 - c).wait_recv()
            fwd(t, c).wait_send()
        for cp in sends:
            cp.wait_send()

    return start, land, finish


def _scatter_scratch(parts, smalls):
    arrays = tuple(parts) + tuple(smalls)
    nt = len(arrays)
    return ([pltpu.SemaphoreType.DMA((3 * nt,))] * 2 + [pltpu.SemaphoreType.DMA((nt,))] * 4
            + [pltpu.VMEM(a.shape[1:], a.dtype) for a in arrays])


def _scatter_shapes(parts, smalls):
    return [jax.ShapeDtypeStruct((2, N_CHIPS) + p.shape[1:], p.dtype) for p in tuple(parts) + tuple(smalls)]


HBM_SPEC = pl.BlockSpec(memory_space=pltpu.HBM)
SEM_SPEC = pl.BlockSpec(memory_space=pltpu.SEMAPHORE)
EFFECT = pltpu.SideEffectType.DATAFLOW_SIDE_EFFECTING


def _ici_copy(ins, lands, n_parts, send, recv, t, j):
    _, _, c, k, chips = _place()
    qx, qy = chips[j]
    src = ins[t].at[2 * qx + qy] if t < n_parts else ins[t].at[c]
    return pltpu.make_async_remote_copy(
        src_ref=src, dst_ref=lands[t].at[c, k], send_sem=send.at[t * 3 + j], recv_sem=recv.at[t * 3 + j],
        device_id=(qx, qy, c), device_id_type=MESH)


def _scatter_start(parts, smalls):
    arrays = tuple(parts) + tuple(smalls)
    nt = len(arrays)

    def body(*refs):
        ins, lands = refs[:nt], refs[nt:2 * nt]
        send, recv = refs[2 * nt], refs[2 * nt + 1]
        token = refs[-1]
        for t in range(nt):
            for j in range(3):
                _ici_copy(ins, lands, len(parts), send, recv, t, j).start()
        token[...] = jnp.zeros(token.shape, F32)

    land_shapes = _scatter_shapes(parts, smalls)
    out_shape = ([pltpu.SemaphoreType.DMA((3 * nt,))] * 2 + [pltpu.HBM(a.shape, a.dtype) for a in arrays]
                 + [pltpu.HBM(a.shape, a.dtype) for a in land_shapes] + [jax.ShapeDtypeStruct((8, 128), F32)])
    operands = [pltpu.with_memory_space_constraint(a, pltpu.HBM) for a in arrays]
    operands += [pltpu.with_memory_space_constraint(lax.empty(a.shape, a.dtype), pltpu.HBM) for a in land_shapes]
    outs = pl.pallas_call(
        body, name="scatter_start", out_shape=out_shape, in_specs=[HBM_SPEC] * (2 * nt),
        out_specs=[SEM_SPEC] * 2 + [HBM_SPEC] * (2 * nt) + [VMEM],
        input_output_aliases={i: 2 + i for i in range(2 * nt)},
        compiler_params=pltpu.CompilerParams(has_side_effects=EFFECT),
    )(*operands)
    return outs[0], outs[1], outs[2:2 + nt], outs[2 + nt:2 + 2 * nt], outs[-1]


def _scatter_wait(send, recv, ins, lands, n_parts, after):
    nt = len(ins)

    def body(*refs):
        in_refs, land_refs = refs[:nt], refs[nt:2 * nt]
        send_ref, recv_ref = refs[2 * nt], refs[2 * nt + 1]
        for t in range(nt):
            for j in range(3):
                cp = _ici_copy(in_refs, land_refs, n_parts, send_ref, recv_ref, t, j)
                cp.wait_send()
                cp.wait_recv()

    outs = pl.pallas_call(
        body, name="scatter_wait", out_shape=[pltpu.HBM(a.shape, a.dtype) for a in tuple(ins) + tuple(lands)],
        in_specs=[HBM_SPEC] * (2 * nt) + [SEM_SPEC] * 2 + [ANY] * len(after), out_specs=[HBM_SPEC] * (2 * nt),
        input_output_aliases={i: i for i in range(2 * nt)},
        compiler_params=pltpu.CompilerParams(has_side_effects=EFFECT),
    )(*ins, *lands, send, recv, *after)
    return outs[:nt], outs[nt:]


def _scatter_forward(ins, lands, n_parts):
    nt = len(ins)

    def body(*refs):
        start, land, finish = _scatter_ops(
            refs[:nt], refs[2 * nt:3 * nt], n_parts, refs[3 * nt:3 * nt + 6], refs[3 * nt + 6:], landed=True)
        start()
        land()
        finish()

    return pl.pallas_call(
        body, name="scatter_forward", out_shape=[jax.ShapeDtypeStruct(a.shape, a.dtype) for a in lands],
        in_specs=[ANY] * (2 * nt), out_specs=[ANY] * nt, input_output_aliases={nt + i: i for i in range(nt)},
        scratch_shapes=_scatter_scratch(ins[:n_parts], ins[n_parts:]),
    )(*ins, *lands)


def _chip_scatter(parts, smalls):
    nt = len(parts) + len(smalls)

    def body(*refs):
        start, land, finish = _scatter_ops(refs[:nt], refs[nt:2 * nt], len(parts), refs[2 * nt:2 * nt + 6], refs[2 * nt + 6:])
        start()
        land()
        finish()

    return pl.pallas_call(
        body, name="chip_scatter", out_shape=_scatter_shapes(parts, smalls), in_specs=[ANY] * nt, out_specs=[ANY] * nt,
        scratch_shapes=_scatter_scratch(parts, smalls),
    )(*parts, *smalls)


def _adamw(w, g, m, v):
    m = ADAM_B1 * m + (1.0 - ADAM_B1) * g
    v = ADAM_B2 * v + (1.0 - ADAM_B2) * (g * g)
    m_hat = m / (1.0 - ADAM_B1 ** ADAM_STEP)
    v_hat = v / (1.0 - ADAM_B2 ** ADAM_STEP)
    delta = -ADAM_LR * (m_hat / (jnp.sqrt(v_hat) + ADAM_EPS) + ADAM_WD * w)
    return delta, m, v


def _adam_big(parts, w, m, v, tag, block_rows, token):
    _, _, rows, cols = parts.shape
    steps = rows // block_rows

    def body(p_ref, w_ref, m_ref, v_ref, token_ref, g_out, d_out, m_out, v_out):
        g = p_ref[0].astype(F32)
        for q in range(1, N_CHIPS):
            g = g + p_ref[q].astype(F32)
        delta, m_new, v_new = _adamw(w_ref[...], g, m_ref[...], v_ref[...])
        g_out[...] = g
        d_out[...] = delta
        m_out[...] = m_new
        v_out[...] = v_new

    blk = pl.BlockSpec((block_rows, cols), lambda h, r: (h * steps + r, 0))
    return pl.pallas_call(
        body, name=f"adam_{tag}", grid=(2, steps),
        in_specs=[pl.BlockSpec((None, N_CHIPS, block_rows, cols), lambda h, r: (h, 0, r, 0)), blk, blk, blk, ANY],
        out_specs=[blk] * 4, out_shape=[jax.ShapeDtypeStruct(w.shape, F32)] * 4,
        compiler_params=pltpu.CompilerParams(dimension_semantics=("arbitrary", "arbitrary"), vmem_limit_bytes=VMEM_LIMIT),
    )(parts, w, m, v, token)


def _reduce_small(l_m, l_f, l_5, l_p):
    def total(ref):
        t = ref[:, 0]
        for q in range(1, N_CHIPS):
            t = t + ref[:, q]
        return t

    def body(m_ref, f_ref, s_ref, p_ref, g1_o, g2_o, g3_o, loss_o, wf_o, fb_o, wa_o, cb_o, lg_o, lb_o, ps_o, pw_o):
        tm, tf, t5, tp = total(m_ref), total(f_ref), total(s_ref), total(p_ref)
        sm = jnp.concatenate([tm[0], tm[1]], axis=1)
        sf = jnp.concatenate([tf[0], tf[1]], axis=1)
        s5 = jnp.concatenate([t5[0], t5[1]], axis=1)
        g1_o[...] = sm[0:1]
        g2_o[...] = sm[1:2]
        g3_o[...] = sm[2:3]
        loss_o[...] = sm[3:4, 0:128]
        wf_o[...] = sf
        fb_o[...] = sf[3:4]
        wa_o[...] = s5[0:32]
        cb_o[...] = s5[32:33]
        lg_o[...] = s5[33:34]
        lb_o[...] = s5[34:35]
        ps_o[...] = s5[35:36]
        for h in range(2):
            for g in range(2):
                pw_o[2 * h + g] = tp[h, g]

    row = lambda w: jax.ShapeDtypeStruct((1, w), F32)
    out_shape = [row(D_MODEL), row(D_MODEL), row(D_MODEL), row(128), jax.ShapeDtypeStruct((8, D_FF), F32), row(D_FF),
                 jax.ShapeDtypeStruct((32, D_CONV), F32), row(D_CONV), row(D_CONV), row(D_CONV), row(D_POOL),
                 jax.ShapeDtypeStruct((4, POOL_GROUP, POOL_GROUP), F32)]
    return pl.pallas_call(body, name="reduce_small", out_shape=out_shape, in_specs=[VMEM] * 4, out_specs=[VMEM] * 12)(
        l_m, l_f, l_5, l_p)


def _adam_small(ws, gs, ms, vs):
    count = len(ws)

    def body(*refs):
        w_r, g_r, m_r, v_r = (refs[t * count:(t + 1) * count] for t in range(4))
        d_o, m_o, v_o = (refs[(4 + t) * count:(5 + t) * count] for t in range(3))
        for t in range(count):
            delta, m_new, v_new = _adamw(w_r[t][...], g_r[t][...], m_r[t][...], v_r[t][...])
            d_o[t][...] = delta
            m_o[t][...] = m_new
            v_o[t][...] = v_new

    out_shape = [jax.ShapeDtypeStruct(w.shape, F32) for w in ws] * 3
    outs = pl.pallas_call(body, name="adam_small", out_shape=out_shape, in_specs=[VMEM] * (4 * count),
                          out_specs=[VMEM] * (3 * count))(*ws, *gs, *ms, *vs)
    return outs[:count], outs[count:2 * count], outs[2 * count:]


MIX_TILE = 512
FFN_TILE = 256
GRAD_K = 2048


def kernel(x, norm_mix_g, w_in, conv_a_w, conv_a_b, ln_a_g, ln_a_b, pool_w, pool_scale, w_out, norm_ffn_g, w_up, conv_f_w, conv_f_b, w_down, norm_final_g, loss_target, m_norm_mix_g, m_w_in, m_conv_a_w, m_conv_a_b, m_ln_a_g, m_ln_a_b, m_pool_w, m_pool_scale, m_w_out, m_norm_ffn_g, m_w_up, m_conv_f_w, m_conv_f_b, m_w_down, m_norm_final_g, v_norm_mix_g, v_w_in, v_conv_a_w, v_conv_a_b, v_ln_a_g, v_ln_a_b, v_pool_w, v_pool_scale, v_w_out, v_norm_ffn_g, v_w_up, v_conv_f_w, v_conv_f_b, v_w_down, v_norm_final_g):
    seq = x.shape[1]
    xs, ts = x[0], loss_target[0]
    mix_tile, ffn_tile, grad_k = min(MIX_TILE, seq), min(FFN_TILE, seq), min(GRAD_K, seq)
    chip = 2 * lax.axis_index("x") + lax.axis_index("y")
    core = lax.axis_index("c").astype(jnp.int32).reshape(1)

    wa_s = jnp.pad(conv_a_w[0], ((0, 32 - CONV_A), (0, 0)))
    wf_s = jnp.pad(conv_f_w[0], ((0, 8 - CONV_F), (0, 0)))
    win_b, wout_b, wup_b, wdown_b = _cast_shards(w_in[0], w_out[0], w_up[0], w_down[0])
    g3 = norm_final_g.reshape(1, D_MODEL)
    pw = pool_w[0]

    h1, proj, cpre, dpool, mcat, x1, win, wout, wup, wa_g, wf_g = _mixer_fwd(
        xs, norm_mix_g, win_b, wout_b, wup_b, wa_s, wf_s, conv_a_b, ln_a_g, ln_a_b, pw, pool_scale, mix_tile)
    wa = jnp.transpose(wa_g, (1, 0, 2)).reshape(32, D_CONV)
    wf = jnp.transpose(wf_g, (1, 0, 2)).reshape(8, D_FF)
    h2, up, gcs, act, wdown = _ffn_up(x1, norm_ffn_g, wup, wf, conv_f_b, wdown_b, ffn_tile)
    dx2, dx2b, sm_f2 = _ffn_down(x1, act, wdown, g3, ts, mix_tile)
    tags = ("w_in", "w_out", "w_up", "w_down")
    blocks = (256, 128, 256, 176)
    g_wdown = _weight_grad(act, dx2b, "rows2", grad_k)
    dup, dx1, dx1b, sm_b1, sf, l_wdown = _ffn_bwd(
        dx2, up, gcs, x1, norm_ffn_g, wup, wf, wdown, ("exchange", [g_wdown]), ffn_tile)
    p_wdown = _pair_sum(core, g_wdown, l_wdown, tags[3], blocks[3])
    g_wup, s_wdown = _weight_grad(h2, dup, "cols_chip", grad_k, ("scatter", [p_wdown]))
    g_wout, l_wup = _weight_grad(mcat, dx1b, "rows1", grad_k, ("exchange", [g_wup]))
    p_wup = _pair_sum(core, g_wup, l_wup, tags[2], blocks[2])
    l_wout, = _sibling_exchange((g_wout,), (), "early")
    p_wout = _pair_sum(core, g_wout, l_wout, tags[1], blocks[1])
    dproj, gx, sm_b2, s5, sp, s_wout, s_wup = _mixer_bwd(
        dx1, xs, proj, cpre, dpool, norm_mix_g, win, wa, ln_a_g, ln_a_b, pw, pool_scale, wout, [p_wout, p_wup], mix_tile)
    g_win, grad_x = _weight_grad(h1, dproj, "cols_half", grad_k, carry=gx)

    smalls = (sm_f2, sm_b1, sm_b2, sf, s5, sp)
    landed = _sibling_exchange((g_win,), smalls, "late")
    part_win = _pair_sum(core, g_win, landed[0], tags[0], blocks[0])
    small_parts = _pair_sum_small(smalls, landed[1:])
    send, recv, late_src, late_land, token = _scatter_start([part_win], small_parts)
    big_w = (w_in[0], w_out[0], w_up[0], w_down[0])
    big_m = (m_w_in[0], m_w_out[0], m_w_up[0], m_w_down[0])
    big_v = (v_w_in[0], v_w_out[0], v_w_up[0], v_w_down[0])
    big = {}
    for t, p in ((1, s_wout), (2, s_wup), (3, s_wdown)):
        big[tags[t]] = _adam_big(p, big_w[t], big_m[t], big_v[t], tags[t], blocks[t], token)
    late_src, late_land = _scatter_wait(send, recv, late_src, late_land, 1, [big[tags[t]][3] for t in (1, 2, 3)])
    late = _scatter_forward(late_src, late_land, 1)
    big[tags[0]] = _adam_big(late[0], big_w[0], big_m[0], big_v[0], tags[0], blocks[0], token)
    big = {tag: [a[None] for a in outs] for tag, outs in big.items()}
    scattered = [None] * 4 + list(late[1:])

    (g_g1, g_g2, g_g3, loss_row, g_wf_all, g_fb, g_wa_all, g_cb, g_lg, g_lb, g_ps, g_pw) = _reduce_small(*scattered[4:])
    g_wa = lax.dynamic_slice(g_wa_all, (0, chip * (D_CONV // N_CHIPS)), (32, D_CONV // N_CHIPS))[:CONV_A]
    g_wf = lax.dynamic_slice(g_wf_all, (0, chip * (D_FF // N_CHIPS)), (8, D_FF // N_CHIPS))[:CONV_F]
    small_names = ("norm_mix_g", "conv_a_w", "conv_a_b", "ln_a_g", "ln_a_b", "pool_w", "pool_scale", "norm_ffn_g",
                   "conv_f_w", "conv_f_b", "norm_final_g")
    small_w = (norm_mix_g, conv_a_w[0], conv_a_b, ln_a_g, ln_a_b, pw, pool_scale, norm_ffn_g, conv_f_w[0], conv_f_b, g3)
    small_m = (m_norm_mix_g, m_conv_a_w[0], m_conv_a_b, m_ln_a_g, m_ln_a_b, m_pool_w[0], m_pool_scale, m_norm_ffn_g,
               m_conv_f_w[0], m_conv_f_b, m_norm_final_g.reshape(1, D_MODEL))
    small_v = (v_norm_mix_g, v_conv_a_w[0], v_conv_a_b, v_ln_a_g, v_ln_a_b, v_pool_w[0], v_pool_scale, v_norm_ffn_g,
               v_conv_f_w[0], v_conv_f_b, v_norm_final_g.reshape(1, D_MODEL))
    small_g = (g_g1, g_wa, g_cb, g_lg, g_lb, g_pw, g_ps, g_g2, g_wf, g_fb, g_g3)
    s_delta, s_m, s_v = _adam_small(small_w, small_g, small_m, small_v)
    shapes = {"conv_a_w": conv_a_w.shape, "pool_w": pool_w.shape, "conv_f_w": conv_f_w.shape, "norm_final_g": norm_final_g.shape}
    small = {}
    for t, name in enumerate(small_names):
        shp = shapes.get(name)
        small[name] = [a if shp is None else a.reshape(shp) for a in (small_g[t], s_delta[t], s_m[t], s_v[t])]

    order = ("norm_mix_g", "w_in", "conv_a_w", "conv_a_b", "ln_a_g", "ln_a_b", "pool_w", "pool_scale", "w_out", "norm_ffn_g",
             "w_up", "conv_f_w", "conv_f_b", "w_down", "norm_final_g")
    table = {**big, **small}
    loss = loss_row[0, 0]
    outs = [loss, grad_x[None]]
    for t in range(4):
        outs += [table[name][t] for name in order]
    return tuple(outs)
```

```python
import functools

import jax
import jax.numpy as jnp
from jax import lax
from jax.experimental import pallas as pl
from jax.experimental.pallas import tpu as pltpu

F32 = jnp.float32
BF16 = jnp.bfloat16
EPS = 1e-6
ADAM_LR = 0.001
ADAM_B1 = 0.9
ADAM_B2 = 0.999
ADAM_EPS = 1e-08
ADAM_WD = 0.01
ADAM_STEP = 10

D_MODEL = 1024
D_CONV = 512
D_POOL = 512
D_IN = 1536
D_FF = 2816
CONV_A = 31
CONV_F = 3
POOL_WINDOWS = (2, 4, 8, 16)
POOL_GROUP = 128
N_CHIPS = 4
FF_CHUNK = 256
N_FF_CHUNKS = D_FF // FF_CHUNK
A_HALO = 32
P_HALO = 16
VMEM_LIMIT = 56 * 1024 * 1024
MESH = pl.DeviceIdType.MESH

ANY = pl.BlockSpec(memory_space=pl.ANY)
VMEM = pl.BlockSpec(memory_space=pltpu.VMEM)


def _dot(a, b):
    return jnp.dot(a, b, preferred_element_type=F32)


def _dot_nt(a, b):
    return lax.dot_general(a, b, (((1,), (1,)), ((), ())), preferred_element_type=F32)


def _dot_tn(a, b):
    return lax.dot_general(a, b, (((0,), (0,)), ((), ())), preferred_element_type=F32)


def _sigmoid(v):
    return jax.nn.sigmoid(v)


def _colsum(v):
    return jnp.sum(v, axis=0, keepdims=True)


def _rowmean(v):
    return jnp.mean(v, axis=-1, keepdims=True)


def _place():
    x, y, c = lax.axis_index("x"), lax.axis_index("y"), lax.axis_index("c")
    chips = [(1 - x, y), (x, 1 - y), (1 - x, 1 - y)]
    return x, y, c, 2 * x + y, chips


def _staged(src, dst, stage, sem_in, sem_out):
    hop_in = pltpu.make_async_copy(src, stage, sem_in)
    hop_out = pltpu.make_async_copy(stage, dst, sem_out)

    def relay():
        hop_in.wait()
        hop_out.start()

    return hop_in.start, relay, hop_out.wait


def _gather_ops(bufs, fulls, col_sharded, sems, stages):
    ici_send, ici_recv, fwd_send, fwd_recv, loc_in, loc_out = sems
    n_big = len(bufs)
    x, y, c, k, chips = _place()

    def block(i, kk, half=None):
        rows, cols = bufs[i].shape
        if col_sharded[i]:
            rs = slice(None) if half is None else pl.ds(pl.multiple_of(half * (rows // 2), 16), rows // 2)
            return fulls[i].at[rs, pl.ds(pl.multiple_of(kk * cols, 128), cols)]
        if half is None:
            return fulls[i].at[pl.ds(pl.multiple_of(kk * rows, 16), rows), :]
        return fulls[i].at[pl.ds(pl.multiple_of(kk * rows + half * (rows // 2), 16), rows // 2), :]

    def my_half(i):
        rows = bufs[i].shape[0]
        return bufs[i].at[pl.ds(pl.multiple_of(c * (rows // 2), 16), rows // 2), :]

    def ici(i, j, kk):
        return pltpu.make_async_remote_copy(
            src_ref=my_half(i), dst_ref=block(i, kk, c), send_sem=ici_send.at[i * 3 + j], recv_sem=ici_recv.at[i * 3 + j],
            device_id=(*chips[j], c), device_id_type=MESH)

    def fwd(i, j, kk, half):
        return pltpu.make_async_remote_copy(
            src_ref=block(i, kk, half), dst_ref=block(i, kk, half), send_sem=fwd_send.at[i * 3 + j],
            recv_sem=fwd_recv.at[i * 3 + j], device_id=(x, y, 1 - c), device_id_type=MESH)

    local = [_staged(bufs[i], block(i, k), stages[i], loc_in.at[i], loc_out.at[i]) for i in range(n_big)]
    sends = [ici(i, j, k) for i in range(n_big) for j in range(3)]
    peers = [(i, j, 2 * qx + qy) for i in range(n_big) for j, (qx, qy) in enumerate(chips)]

    def start():
        for cp in local:
            cp[0]()
        for cp in sends:
            cp.start()

    def land():
        for cp in local:
            cp[1]()
        for i, j, kq in peers:
            ici(i, j, kq).wait_recv()
            fwd(i, j, kq, c).start()

    def finish():
        for i, j, kq in peers:
            fwd(i, j, kq, 1 - c).wait_recv()
            fwd(i, j, kq, c).wait_send()
        for cp in sends:
            cp.wait_send()
        for cp in local:
            cp[2]()

    return start, land, finish


def _gather_scratch(shards):
    n_big = len(shards)
    return ([pltpu.SemaphoreType.DMA((3 * n_big,))] * 4 + [pltpu.SemaphoreType.DMA((n_big,))] * 2
            + [pltpu.VMEM(b.shape, b.dtype) for b in shards])


def _tap_ops(srcs, dsts, sems):
    send, recv, loc = sems
    _, _, c, k, chips = _place()

    def copy(t, j, kk):
        return pltpu.make_async_remote_copy(
            src_ref=srcs[t], dst_ref=dsts[t].at[kk], send_sem=send.at[t * 3 + j], recv_sem=recv.at[t * 3 + j],
            device_id=(*chips[j], c), device_id_type=MESH)

    local = [pltpu.make_async_copy(srcs[t], dsts[t].at[k], loc.at[t]) for t in range(len(srcs))]
    sends = [[copy(t, j, k) for j in range(3)] for t in range(len(srcs))]

    def start():
        for t, cp in enumerate(local):
            cp.start()
            for sd in sends[t]:
                sd.start()

    def wait(t):
        for j, (qx, qy) in enumerate(chips):
            copy(t, j, 2 * qx + qy).wait_recv()
        for sd in sends[t]:
            sd.wait_send()
        local[t].wait()

    return start, wait


def _cast_shards(*shards):
    def body(*refs):
        for src, dst in zip(refs[:len(shards)], refs[len(shards):]):
            dst[...] = src[...].astype(BF16)

    return pl.pallas_call(
        body, name="cast_shards", out_shape=[jax.ShapeDtypeStruct(s.shape, BF16) for s in shards],
        in_specs=[VMEM] * len(shards), out_specs=[VMEM] * len(shards),
        compiler_params=pltpu.CompilerParams(vmem_limit_bytes=VMEM_LIMIT),
    )(*shards)


def _load_weights(pairs, sem, first=0):
    cps = [pltpu.make_async_copy(src, dst, sem.at[first + i]) for i, (src, dst) in enumerate(pairs)]
    for cp in cps:
        cp.start()
    for cp in cps:
        cp.wait()


def _shifted_views(buf, shifted, t_rows):
    n = t_rows + A_HALO - 8
    for b in range(1, 8):
        shifted[b - 1] = buf[b:b + n, :]

    def view(offset):
        a, b = divmod(offset, 8)
        if b == 0:
            return buf[8 * a:8 * a + t_rows, :]
        return shifted[b - 1, 8 * a:8 * a + t_rows, :]

    return view


def _pool_count(tile, t_rows, w):
    row = lax.broadcasted_iota(jnp.int32, (t_rows, POOL_GROUP), 0) + tile * t_rows
    return jnp.minimum(row + 1, w).astype(F32)


def _mixer_fwd(x, g1, win_b, wout_b, wup_b, wa_s, wf_s, cb, lg, lb, pw, ps, tile_rows):
    seq = x.shape[0]
    tr = tile_rows
    n = seq // tr

    def body(x_ref, g1_ref, win_b_hbm, wout_b_hbm, wup_b_hbm, wa_s_hbm, wf_s_hbm, cb_ref, lg_ref, lb_ref, pw_ref,
             ps_ref, h1_ref, proj_ref, c_ref, d_ref, m_ref, x1_ref, win_f, wout_f, wup_f, wa_g, wf_g,
             win_v, wout_v, wa_ref, ubuf, ushift, bbuf, sem, *csems):
        i = pl.program_id(0)
        first_sems, first_stages, second_sems, second_stages, later_sems, later_stages, tap_sems = (
            csems[0:6], csems[6:7], csems[7:13], csems[13:14], csems[14:20], csems[20:21], csems[21:24])

        def first():
            return _gather_ops((win_b_hbm,), (win_f,), (True,), first_sems, first_stages)

        def second():
            return _gather_ops((wout_b_hbm,), (wout_f,), (False,), second_sems, second_stages)

        def later():
            return _gather_ops((wup_b_hbm,), (wup_f,), (True,), later_sems, later_stages)

        def taps():
            return _tap_ops((wa_s_hbm, wf_s_hbm), (wa_g, wf_g), tap_sems)

        @pl.when(i == 0)
        def _():
            first()[0]()
            taps()[0]()
            second()[0]()
            later()[0]()
            first()[1]()
            first()[2]()
            _load_weights([(win_f, win_v)], sem)
            ubuf[0:A_HALO, :] = jnp.zeros((A_HALO, D_CONV), F32)
            bbuf[0:P_HALO, :] = jnp.zeros((P_HALO, D_POOL), F32)

        xv = x_ref[...]
        r = lax.rsqrt(_rowmean(xv * xv) + EPS)
        h1 = (xv * r * g1_ref[...]).astype(BF16)
        h1_ref[...] = h1
        proj = _dot(h1, win_v[...])
        proj_ref[...] = proj.astype(BF16)

        @pl.when(i == 0)
        def _():
            taps()[1](0)
            _load_weights([(wa_g.at[kk], wa_ref.at[:, kk * (D_CONV // N_CHIPS):(kk + 1) * (D_CONV // N_CHIPS)])
                           for kk in range(N_CHIPS)], sem, 2)

        av, ag, bi = proj[:, :D_CONV], proj[:, D_CONV:2 * D_CONV], proj[:, 2 * D_CONV:]
        ubuf[A_HALO:A_HALO + tr, :] = av * _sigmoid(ag)
        off = A_HALO - (CONV_A - 1)
        uview = _shifted_views(ubuf, ushift, tr)
        acc = wa_ref[0:1, :] * uview(off)
        for j in range(1, CONV_A):
            acc = acc + wa_ref[j:j + 1, :] * uview(off + j)
        cv = acc + cb_ref[...]
        ubuf[0:A_HALO, :] = ubuf[tr:tr + A_HALO, :]
        c_ref[...] = cv.astype(BF16)
        xc = cv - _rowmean(cv)
        z = xc * lax.rsqrt(_rowmean(xc * xc) + EPS)
        ln = z * lg_ref[...] + lb_ref[...]
        ya = ln * _sigmoid(ln)
        bbuf[P_HALO:P_HALO + tr, :] = bi
        ds, ybs = [], []
        for g, w in enumerate(POOL_WINDOWS):
            cols = slice(g * POOL_GROUP, (g + 1) * POOL_GROUP)
            s = bi[:, cols]
            for kk in range(1, w):
                s = s + bbuf[P_HALO - kk:P_HALO - kk + tr, cols]
            dg = s / _pool_count(i, tr, w) - bi[:, cols]
            ds.append(dg)
            ybs.append(_dot(dg.astype(BF16), pw_ref[g].astype(BF16)))
        bbuf[0:P_HALO, :] = bbuf[tr:tr + P_HALO, :]
        d_ref[...] = jnp.concatenate(ds, axis=1).astype(BF16)
        yb = jnp.concatenate(ybs, axis=1) * ps_ref[...]
        m = jnp.concatenate([ya, yb], axis=1).astype(BF16)
        m_ref[...] = m

        @pl.when(i == 0)
        def _():
            second()[1]()
            second()[2]()
            _load_weights([(wout_f, wout_v)], sem, 1)

        x1_ref[...] = xv + _dot(m, wout_v[...])

        @pl.when(i == n - 1)
        def _():
            later()[1]()
            later()[2]()
            taps()[1](1)

    tile = lambda w: pl.BlockSpec((tr, w), lambda i: (i, 0))
    full = lambda a: pl.BlockSpec(a.shape, lambda i: (0,) * a.ndim)
    return pl.pallas_call(
        body, name="mixer_fwd", grid=(n,),
        in_specs=[tile(D_MODEL), full(g1)] + [ANY] * 5 + [full(cb), full(lg), full(lb), full(pw), full(ps)],
        out_specs=[tile(D_MODEL), tile(D_IN), tile(D_CONV), tile(D_POOL), tile(D_MODEL), tile(D_MODEL)] + [ANY] * 5,
        out_shape=[
            jax.ShapeDtypeStruct((seq, D_MODEL), BF16), jax.ShapeDtypeStruct((seq, D_IN), BF16),
            jax.ShapeDtypeStruct((seq, D_CONV), BF16), jax.ShapeDtypeStruct((seq, D_POOL), BF16),
            jax.ShapeDtypeStruct((seq, D_MODEL), BF16), jax.ShapeDtypeStruct((seq, D_MODEL), F32),
            jax.ShapeDtypeStruct((D_MODEL, D_IN), BF16), jax.ShapeDtypeStruct((D_MODEL, D_MODEL), BF16),
            jax.ShapeDtypeStruct((D_MODEL, 2 * D_FF), BF16),
            jax.ShapeDtypeStruct((N_CHIPS,) + wa_s.shape, F32), jax.ShapeDtypeStruct((N_CHIPS,) + wf_s.shape, F32),
        ],
        scratch_shapes=[
            pltpu.VMEM((D_MODEL, D_IN), BF16), pltpu.VMEM((D_MODEL, D_MODEL), BF16), pltpu.VMEM((32, D_CONV), F32),
            pltpu.VMEM((tr + A_HALO, D_CONV), F32), pltpu.VMEM((7, tr + A_HALO - 8, D_CONV), F32),
            pltpu.VMEM((tr + P_HALO, D_POOL), F32), pltpu.SemaphoreType.DMA((2 + N_CHIPS,)),
        ] + _gather_scratch((win_b,)) + _gather_scratch((wout_b,)) + _gather_scratch((wup_b,)) + [
            pltpu.SemaphoreType.DMA((6,)), pltpu.SemaphoreType.DMA((6,)), pltpu.SemaphoreType.DMA((2,))],
        compiler_params=pltpu.CompilerParams(dimension_semantics=("arbitrary",), vmem_limit_bytes=VMEM_LIMIT),
    )(x, g1, win_b, wout_b, wup_b, wa_s, wf_s, cb, lg, lb, pw, ps)


def _ffn_up(x1, g2, wup, wf, fb, wdown_b, tile_rows):
    seq = x1.shape[0]
    tr = tile_rows
    n = seq // tr

    def body(x1_ref, g2_ref, wup_hbm, wf_ref, fb_ref, wdown_b_hbm,
             h2_ref, up_ref, gc_ref, act_ref, wdown_f, wup_v, gbuf, sem, *gsems):
        i = pl.program_id(0)

        def gather():
            return _gather_ops((wdown_b_hbm,), (wdown_f,), (False,), gsems[:6], gsems[6:])

        @pl.when(i == 0)
        def _():
            gather()[0]()
            _load_weights(((wup_hbm, wup_v),), sem)
            gbuf[0:8, :] = jnp.zeros((8, D_FF), F32)

        x1v = x1_ref[...]
        r2 = lax.rsqrt(_rowmean(x1v * x1v) + EPS)
        h2 = (x1v * r2 * g2_ref[...]).astype(BF16)
        h2_ref[...] = h2

        def up_proj(j):
            return (_dot(h2, wup_v[:, j * FF_CHUNK:(j + 1) * FF_CHUNK]),
                    _dot(h2, wup_v[:, D_FF + j * FF_CHUNK:D_FF + (j + 1) * FF_CHUNK]))

        ahead = up_proj(0)
        for j in range(N_FF_CHUNKS):
            cs = slice(j * FF_CHUNK, (j + 1) * FF_CHUNK)
            vs = slice(D_FF + j * FF_CHUNK, D_FF + (j + 1) * FF_CHUNK)
            gate, val = ahead
            if j + 1 < N_FF_CHUNKS:
                ahead = up_proj(j + 1)
            up_ref[:, cs] = gate.astype(BF16)
            up_ref[:, vs] = val.astype(BF16)
            gbuf[8:8 + tr, cs] = gate
            gc = (wf_ref[0:1, cs] * gbuf[6:6 + tr, cs] + wf_ref[1:2, cs] * gbuf[7:7 + tr, cs]
                  + wf_ref[2:3, cs] * gate + fb_ref[:, cs])
            gbuf[0:8, cs] = gbuf[tr:tr + 8, cs]
            gc_ref[:, cs] = gc.astype(BF16)
            act_ref[:, cs] = (gc * _sigmoid(gc) * val).astype(BF16)

        @pl.when(i == max(n - 2, 0))
        def _():
            gather()[1]()

        @pl.when(i == n - 1)
        def _():
            gather()[2]()

    tile = lambda w: pl.BlockSpec((tr, w), lambda i: (i, 0))
    full = lambda a: pl.BlockSpec(a.shape, lambda i: (0,) * a.ndim)
    return pl.pallas_call(
        body, name="ffn_up", grid=(n,),
        in_specs=[tile(D_MODEL), full(g2), ANY, full(wf), full(fb), ANY],
        out_specs=[tile(D_MODEL), tile(2 * D_FF), tile(D_FF), tile(D_FF), ANY],
        out_shape=[
            jax.ShapeDtypeStruct((seq, D_MODEL), BF16), jax.ShapeDtypeStruct((seq, 2 * D_FF), BF16),
            jax.ShapeDtypeStruct((seq, D_FF), BF16), jax.ShapeDtypeStruct((seq, D_FF), BF16),
            jax.ShapeDtypeStruct((D_FF, D_MODEL), BF16),
        ],
        scratch_shapes=[pltpu.VMEM(wup.shape, BF16), pltpu.VMEM((tr + 8, D_FF), F32), pltpu.SemaphoreType.DMA((1,))]
        + _gather_scratch((wdown_b,)),
        compiler_params=pltpu.CompilerParams(dimension_semantics=("arbitrary",), vmem_limit_bytes=VMEM_LIMIT),
    )(x1, g2, wup, wf, fb, wdown_b)


def _ffn_down(x1, act, wdown, g3, target, tile_rows):
    seq = x1.shape[0]
    tr = tile_rows
    n = seq // tr

    def body(x1_ref, act_ref, wdown_hbm, g3_ref, t_ref, dx2_ref, dx2b_ref, sm_ref, wdown_v, sem):
        i = pl.program_id(0)

        @pl.when(i == 0)
        def _():
            _load_weights(((wdown_hbm, wdown_v),), sem)
            sm_ref[...] = jnp.zeros(sm_ref.shape, F32)

        x2 = x1_ref[...] + _dot(act_ref[...], wdown_v[...])
        r3 = lax.rsqrt(_rowmean(x2 * x2) + EPS)
        n3 = x2 * r3
        err = n3 * g3_ref[...] - t_ref[...]
        dy = err / D_MODEL
        sm_ref[2:3, :] += _colsum(dy * n3)
        loss = 0.5 * _colsum(_rowmean(err * err))
        sm_ref[3:4, :] += jnp.broadcast_to(loss, (1, D_MODEL))
        dn = dy * g3_ref[...]
        dx2v = r3 * (dn - n3 * _rowmean(dn * n3))
        dx2_ref[...] = dx2v
        dx2b_ref[...] = dx2v.astype(BF16)

    tile = lambda w: pl.BlockSpec((tr, w), lambda i: (i, 0))
    full = lambda a: pl.BlockSpec(a.shape, lambda i: (0,) * a.ndim)
    return pl.pallas_call(
        body, name="ffn_down", grid=(n,),
        in_specs=[tile(D_MODEL), tile(D_FF), ANY, full(g3), tile(D_MODEL)],
        out_specs=[tile(D_MODEL), tile(D_MODEL), pl.BlockSpec((8, D_MODEL), lambda i: (0, 0))],
        out_shape=[
            jax.ShapeDtypeStruct((seq, D_MODEL), F32), jax.ShapeDtypeStruct((seq, D_MODEL), BF16),
            jax.ShapeDtypeStruct((8, D_MODEL), F32),
        ],
        scratch_shapes=[pltpu.VMEM(wdown.shape, BF16), pltpu.SemaphoreType.DMA((1,))],
        compiler_params=pltpu.CompilerParams(dimension_semantics=("arbitrary",), vmem_limit_bytes=VMEM_LIMIT),
    )(x1, act, wdown, g3, target)


def _ffn_bwd(dx2, up, gcs, x1, g2, wup, wf, wdown, comm, tile_rows):
    seq = x1.shape[0]
    c_ins, c_shapes, c_sems, c_ops = _comm_plan(comm)
    nc = len(c_ins)
    tr = tile_rows
    n = seq // tr

    def body(dx2_ref, up_ref, gc_ref, x1_ref, g2_ref, wup_hbm, wf_ref, wdown_hbm, *rest):
        c_in, rest = rest[:nc], rest[nc:]
        dup_ref, dx1_ref, dx1b_ref, sm_ref, sf_ref = rest[:5]
        c_out, rest = rest[5:5 + nc], rest[5 + nc:]
        wup_v, wdown_v, dbuf, dcar, sem = rest[:5]
        c_sem_refs = rest[5:]
        i = pl.program_id(0)

        @pl.when(i == 0)
        def _():
            c_ops(c_in, c_out, c_sem_refs)[0]()
            _load_weights(((wup_hbm, wup_v), (wdown_hbm, wdown_v)), sem)
            dcar[...] = jnp.zeros(dcar.shape, F32)
            sm_ref[...] = jnp.zeros(sm_ref.shape, F32)
            sf_ref[...] = jnp.zeros(sf_ref.shape, F32)

        dx2v = dx2_ref[...]
        dx2b = dx2v.astype(BF16)
        dh2 = jnp.zeros((tr, D_MODEL), F32)

        def down_t(j):
            return _dot_nt(dx2b, wdown_v[j * FF_CHUNK:(j + 1) * FF_CHUNK, :])

        ahead = down_t(0)
        for j in range(N_FF_CHUNKS):
            cs = slice(j * FF_CHUNK, (j + 1) * FF_CHUNK)
            vs = slice(D_FF + j * FF_CHUNK, D_FF + (j + 1) * FF_CHUNK)
            dact = ahead
            if j + 1 < N_FF_CHUNKS:
                ahead = down_t(j + 1)
            gate = up_ref[:, cs].astype(F32)
            val = up_ref[:, vs].astype(F32)
            gc = gc_ref[:, cs].astype(F32)
            sg = _sigmoid(gc)
            dval = dact * (gc * sg)
            dgc = dact * val * (sg * (1.0 + gc * (1.0 - sg)))
            dbuf[0:tr, :] = dgc
            dbuf[tr:tr + 8, :] = dcar[:, cs]
            d_p1 = dbuf[1:1 + tr, :]
            d_p2 = dbuf[2:2 + tr, :]
            dgate = wf_ref[2:3, cs] * dgc + wf_ref[1:2, cs] * d_p1 + wf_ref[0:1, cs] * d_p2
            dcar[:, cs] = dgc[0:8, :]
            sf_ref[0:1, cs] += _colsum(d_p2 * gate)
            sf_ref[1:2, cs] += _colsum(d_p1 * gate)
            sf_ref[2:3, cs] += _colsum(dgc * gate)
            sf_ref[3:4, cs] += _colsum(dgc)
            dgb, dvb = dgate.astype(BF16), dval.astype(BF16)
            dup_ref[:, cs] = dgb
            dup_ref[:, vs] = dvb
            dh2 = dh2 + _dot_nt(dgb, wup_v[:, cs]) + _dot_nt(dvb, wup_v[:, vs])
        x1v = x1_ref[...]
        r2 = lax.rsqrt(_rowmean(x1v * x1v) + EPS)
        n2 = x1v * r2
        sm_ref[1:2, :] += _colsum(dh2 * n2)
        dn2 = dh2 * g2_ref[...]
        dx1v = dx2v + r2 * (dn2 - n2 * _rowmean(dn2 * n2))
        dx1_ref[...] = dx1v
        dx1b_ref[...] = dx1v.astype(BF16)

        @pl.when(i == n - 1)
        def _():
            c_ops(c_in, c_out, c_sem_refs)[2]()

    tile = lambda w: pl.BlockSpec((tr, w), lambda i: (n - 1 - i, 0))
    full = lambda a: pl.BlockSpec(a.shape, lambda i: (0,) * a.ndim)
    acc = lambda rows, w: pl.BlockSpec((rows, w), lambda i: (0, 0))
    return pl.pallas_call(
        body, name="ffn_bwd", grid=(n,),
        in_specs=[tile(D_MODEL), tile(2 * D_FF), tile(D_FF), tile(D_MODEL), full(g2), ANY, full(wf), ANY] + [ANY] * nc,
        out_specs=[tile(2 * D_FF), tile(D_MODEL), tile(D_MODEL), acc(8, D_MODEL), acc(8, D_FF)] + [ANY] * nc,
        out_shape=[
            jax.ShapeDtypeStruct((seq, 2 * D_FF), BF16), jax.ShapeDtypeStruct((seq, D_MODEL), F32),
            jax.ShapeDtypeStruct((seq, D_MODEL), BF16), jax.ShapeDtypeStruct((8, D_MODEL), F32),
            jax.ShapeDtypeStruct((8, D_FF), F32),
        ] + c_shapes,
        scratch_shapes=[
            pltpu.VMEM(wup.shape, BF16), pltpu.VMEM(wdown.shape, BF16),
            pltpu.VMEM((tr + 8, FF_CHUNK), F32), pltpu.VMEM((8, D_FF), F32), pltpu.SemaphoreType.DMA((2,)),
        ] + c_sems,
        compiler_params=pltpu.CompilerParams(dimension_semantics=("arbitrary",), vmem_limit_bytes=VMEM_LIMIT),
    )(dx2, up, gcs, x1, g2, wup, wf, wdown, *c_ins)


def _mixer_bwd(dx1, x, proj, cpre, d, g1, win, wa, lg, lb, pw, ps, wout, parts, tile_rows):
    seq = x.shape[0]
    n_parts = len(parts)
    tr = tile_rows
    n = seq // tr
    row_cb, row_lg, row_lb, row_ps = 32, 33, 34, 35

    def body(dx1_ref, x_ref, proj_ref, projh_ref, c_ref, d_ref, g1_ref, win_hbm, wa_ref, lg_ref, lb_ref, pw_ref, ps_ref,
             wout_hbm, *rest):
        part_refs, rest = rest[:n_parts], rest[n_parts:]
        dproj_ref, gx_ref, sm_ref, s5_ref, sp_ref = rest[:5]
        land_refs, rest = rest[5:5 + n_parts], rest[5 + n_parts:]
        win_v, wout_v, ubuf, ushift, dcbuf, dshift, ebuf, sem = rest[:8]
        ssems = rest[8:]
        i = pl.program_id(0)
        tile = n - 1 - i

        def scatter():
            return _scatter_ops(part_refs, land_refs, n_parts, ssems[:6], ssems[6:])

        @pl.when(i == 0)
        def _():
            scatter()[0]()
            _load_weights(((win_hbm, win_v), (wout_hbm, wout_v)), sem)
            dcbuf[tr:tr + A_HALO, :] = jnp.zeros((A_HALO, D_CONV), F32)
            ebuf[tr:tr + P_HALO, :] = jnp.zeros((P_HALO, D_POOL), F32)
            sm_ref[...] = jnp.zeros(sm_ref.shape, F32)
            s5_ref[...] = jnp.zeros(s5_ref.shape, F32)
            sp_ref[...] = jnp.zeros(sp_ref.shape, F32)

        dx1v = dx1_ref[...]
        dm = _dot_nt(dx1v.astype(BF16), wout_v[...])
        dya, dyb = dm[:, :D_CONV], dm[:, D_CONV:]
        dbis = []
        for g, w in enumerate(POOL_WINDOWS):
            cols = slice(g * POOL_GROUP, (g + 1) * POOL_GROUP)
            dgb = d_ref[:, cols]
            pwb = pw_ref[g].astype(BF16)
            dyg = dyb[:, cols]
            s5_ref[row_ps:row_ps + 1, cols] += _colsum(dyg * _dot(dgb, pwb))
            dqb = (dyg * ps_ref[:, cols]).astype(BF16)
            sp_ref[g] += _dot_tn(dgb, dqb)
            dd = _dot_nt(dqb, pwb)
            e = dd / _pool_count(tile, tr, w)
            ebuf[0:tr, cols] = e
            s = e
            for kk in range(1, w):
                s = s + ebuf[kk:kk + tr, cols]
            dbis.append(s - dd)
        ebuf[tr:tr + P_HALO, :] = ebuf[0:P_HALO, :]
        cv = c_ref[...].astype(F32)
        xc = cv - _rowmean(cv)
        rs = lax.rsqrt(_rowmean(xc * xc) + EPS)
        z = xc * rs
        ln = z * lg_ref[...] + lb_ref[...]
        sl = _sigmoid(ln)
        dl = dya * (sl * (1.0 + ln * (1.0 - sl)))
        s5_ref[row_lg:row_lg + 1, :] += _colsum(dl * z)
        s5_ref[row_lb:row_lb + 1, :] += _colsum(dl)
        dz = dl * lg_ref[...]
        dc = rs * (dz - _rowmean(dz) - z * _rowmean(dz * z))
        s5_ref[row_cb:row_cb + 1, :] += _colsum(dc)
        dcbuf[0:tr, :] = dc
        keep = (tile > 0).astype(F32)
        avh = projh_ref[:, :D_CONV].astype(F32)
        agh = projh_ref[:, D_CONV:].astype(F32)
        ubuf[0:A_HALO, :] = avh * _sigmoid(agh) * keep
        av = proj_ref[:, :D_CONV].astype(F32)
        ag = proj_ref[:, D_CONV:2 * D_CONV].astype(F32)
        sg = _sigmoid(ag)
        ubuf[A_HALO:A_HALO + tr, :] = av * sg
        off = A_HALO - (CONV_A - 1)
        du = wa_ref[CONV_A - 1:CONV_A, :] * dc
        dview = _shifted_views(dcbuf, dshift, tr)
        uview = _shifted_views(ubuf, ushift, tr)
        for j in range(CONV_A - 1):
            du = du + wa_ref[j:j + 1, :] * dview(CONV_A - 1 - j)
        for j in range(CONV_A):
            s5_ref[j:j + 1, :] += _colsum(dc * uview(off + j))
        dcbuf[tr:tr + A_HALO, :] = dcbuf[0:A_HALO, :]
        dav = du * sg
        dag = du * av * (sg * (1.0 - sg))
        dprojb = jnp.concatenate([dav, dag] + dbis, axis=1).astype(BF16)
        dproj_ref[...] = dprojb
        dh1 = _dot_nt(dprojb, win_v[...])
        xv = x_ref[...]
        r1 = lax.rsqrt(_rowmean(xv * xv) + EPS)
        n1 = xv * r1
        sm_ref[0:1, :] += _colsum(dh1 * n1)
        dn1 = dh1 * g1_ref[...]
        gx_ref[...] = dx1v + r1 * (dn1 - n1 * _rowmean(dn1 * n1))

        @pl.when(i == max(n - 2, 0))
        def _():
            scatter()[1]()

        @pl.when(i == n - 1)
        def _():
            scatter()[2]()

    tile = lambda w: pl.BlockSpec((tr, w), lambda i: (n - 1 - i, 0))
    full = lambda a: pl.BlockSpec(a.shape, lambda i: (0,) * a.ndim)
    halo = pl.BlockSpec((A_HALO, 2 * D_CONV), lambda i: (jnp.maximum((n - 1 - i) * (tr // A_HALO) - 1, 0), 0))
    acc = lambda shape: pl.BlockSpec(shape, lambda i: (0,) * len(shape))
    return pl.pallas_call(
        body, name="mixer_bwd", grid=(n,),
        in_specs=[tile(D_MODEL), tile(D_MODEL), tile(D_IN), halo, tile(D_CONV), tile(D_POOL), full(g1), ANY, full(wa),
                  full(lg), full(lb), full(pw), full(ps), ANY] + [ANY] * n_parts,
        out_specs=[tile(D_IN), tile(D_MODEL), acc((8, D_MODEL)), acc((40, D_CONV)), acc(pw.shape)] + [ANY] * n_parts,
        out_shape=[
            jax.ShapeDtypeStruct((seq, D_IN), BF16), jax.ShapeDtypeStruct((seq, D_MODEL), F32),
            jax.ShapeDtypeStruct((8, D_MODEL), F32), jax.ShapeDtypeStruct((40, D_CONV), F32),
            jax.ShapeDtypeStruct(pw.shape, F32),
        ] + _scatter_shapes(parts, ()),
        scratch_shapes=[
            pltpu.VMEM(win.shape, BF16), pltpu.VMEM(wout.shape, BF16),
            pltpu.VMEM((tr + A_HALO, D_CONV), F32), pltpu.VMEM((7, tr + A_HALO - 8, D_CONV), F32),
            pltpu.VMEM((tr + A_HALO, D_CONV), F32), pltpu.VMEM((7, tr + A_HALO - 8, D_CONV), F32),
            pltpu.VMEM((tr + P_HALO, D_POOL), F32), pltpu.SemaphoreType.DMA((2,)),
        ] + _scatter_scratch(parts, ()),
        compiler_params=pltpu.CompilerParams(dimension_semantics=("arbitrary",), vmem_limit_bytes=VMEM_LIMIT),
    )(dx1, x, proj, proj, cpre, d, g1, win, wa, lg, lb, pw, ps, wout, *parts)


def _weight_grad(a, b, layout, k_rows, comm=None, carry=None):
    seq, m_dim = a.shape
    n_dim = b.shape[1]
    steps = seq // k_rows

    def store(o_ref, acc, index, value):
        if steps == 1:
            o_ref[index] = value.astype(BF16)
            return
        s = pl.program_id(1)

        @pl.when(s == 0)
        def _():
            acc[index] = value

        @pl.when(jnp.logical_and(s > 0, s < steps - 1))
        def _():
            acc[index] += value

        @pl.when(s == steps - 1)
        def _():
            o_ref[index] = (acc[index] + value).astype(BF16)

    if layout in ("rows1", "rows2"):
        groups = int(layout[-1])
        per_tile = N_CHIPS // groups
        rows = m_dim // N_CHIPS // 2
        a_w = m_dim // groups

        def body(a_ref, b_ref, o_ref, acc):
            r = _dot_tn(a_ref[...], b_ref[...])
            for p in range(per_tile):
                for h in range(2):
                    store(o_ref, acc, (h, p), r[(2 * p + h) * rows:(2 * p + h + 1) * rows, :])

        in_specs = [pl.BlockSpec((k_rows, a_w), lambda g, s: (s, g)), pl.BlockSpec((k_rows, n_dim), lambda g, s: (s, 0))]
        out_spec = pl.BlockSpec((2, per_tile, rows, n_dim), lambda g, s: (0, g, 0, 0))
        out_dims, acc_dims = (2, N_CHIPS, rows, n_dim), (2, per_tile, rows, n_dim)
    elif layout == "cols_chip":
        groups = N_CHIPS
        rows, cols = m_dim // 2, n_dim // N_CHIPS

        def body(a_ref, b_ref, o_ref, acc):
            r = _dot_tn(a_ref[...], b_ref[...])
            for h in range(2):
                store(o_ref, acc, h, r[h * rows:(h + 1) * rows, :])

        in_specs = [pl.BlockSpec((k_rows, m_dim), lambda g, s: (s, 0)), pl.BlockSpec((k_rows, cols), lambda g, s: (s, g))]
        out_spec = pl.BlockSpec((2, None, rows, cols), lambda g, s: (0, g, 0, 0))
        out_dims, acc_dims = (2, N_CHIPS, rows, cols), (2, rows, cols)
    else:
        groups = 2
        rows, cols = m_dim // 2, n_dim // N_CHIPS

        def body(a_ref, b_ref, o_ref, acc):
            r = _dot_tn(a_ref[...], b_ref[...])
            for k in range(N_CHIPS):
                store(o_ref, acc, k, r[:, k * cols:(k + 1) * cols])

        in_specs = [pl.BlockSpec((k_rows, rows), lambda g, s: (s, g)), pl.BlockSpec((k_rows, n_dim), lambda g, s: (s, 0))]
        out_spec = pl.BlockSpec((None, N_CHIPS, rows, cols), lambda g, s: (g, 0, 0, 0))
        out_dims, acc_dims = (2, N_CHIPS, rows, cols), (N_CHIPS, rows, cols)

    c_ins, c_shapes, c_sems, c_ops = _comm_plan(comm)
    nc = len(c_ins)
    c_specs = [ANY] * nc
    if carry is not None:
        assert comm is None and carry.shape[0] % (groups * steps) == 0
        carry_spec = pl.BlockSpec((carry.shape[0] // (groups * steps), carry.shape[1]), lambda g, s: (g * steps + s, 0))
        c_ins, c_shapes, c_specs, nc = (carry,), [jax.ShapeDtypeStruct(carry.shape, carry.dtype)], [carry_spec], 1

    def hosted(a_ref, b_ref, *rest):
        c_in, o_ref, c_out, acc, sems = rest[:nc], rest[nc], rest[nc + 1:2 * nc + 1], rest[2 * nc + 1], rest[2 * nc + 2:]
        g, s = pl.program_id(0), pl.program_id(1)
        if carry is not None:
            c_out[0][...] = c_in[0][...]
            body(a_ref, b_ref, o_ref, acc)
            return
        if nc:
            @pl.when(jnp.logical_and(g == 0, s == 0))
            def _():
                c_ops(c_in, c_out, sems)[0]()

        body(a_ref, b_ref, o_ref, acc)
        if nc:
            step = g * steps + s

            @pl.when(step == max(groups * steps - 2, 0))
            def _():
                c_ops(c_in, c_out, sems)[1]()

            @pl.when(step == groups * steps - 1)
            def _():
                c_ops(c_in, c_out, sems)[2]()

    outs = pl.pallas_call(
        hosted, name=f"weight_grad_{layout}_{m_dim}x{n_dim}", grid=(groups, steps),
        in_specs=in_specs + c_specs, out_specs=[out_spec] + c_specs,
        out_shape=[jax.ShapeDtypeStruct(out_dims, BF16)] + c_shapes,
        scratch_shapes=[pltpu.VMEM(acc_dims, F32)] + c_sems,
        compiler_params=pltpu.CompilerParams(dimension_semantics=("arbitrary", "arbitrary"), vmem_limit_bytes=VMEM_LIMIT),
    )(a, b, *c_ins)
    return outs if nc else outs[0]


def _exchange_ops(ins, outs, n_big, sems):
    send, recv = sems
    x, y, c, _, _ = _place()
    cps = [pltpu.make_async_remote_copy(
        src_ref=ins[t].at[1 - c] if t < n_big else ins[t], dst_ref=outs[t], send_sem=send.at[t], recv_sem=recv.at[t],
        device_id=(x, y, 1 - c), device_id_type=MESH) for t in range(len(ins))]

    def start():
        for cp in cps:
            cp.start()

    def finish():
        for cp in cps:
            cp.wait()

    return start, finish


def _exchange_shapes(bigs, smalls):
    return [jax.ShapeDtypeStruct((N_CHIPS,) + b.shape[2:], b.dtype) for b in bigs] + [
        jax.ShapeDtypeStruct(s.shape, s.dtype) for s in smalls]


def _comm_plan(comm):
    if comm is None:
        return (), [], [], None
    kind, arrays = comm
    n = len(arrays)
    if kind == "scatter":
        return (tuple(arrays), _scatter_shapes(arrays, ()), _scatter_scratch(arrays, ()),
                lambda i, o, sm: _scatter_ops(i, o, n, sm[:6], sm[6:]))
    def exchange(i, o, sm):
        start, finish = _exchange_ops(i, o, n, sm)
        return start, lambda: None, finish

    return tuple(arrays), _exchange_shapes(arrays, ()), [pltpu.SemaphoreType.DMA((n,))] * 2, exchange


def _sibling_exchange(bigs, smalls, tag):
    nb, nt = len(bigs), len(bigs) + len(smalls)

    def body(*refs):
        start, finish = _exchange_ops(refs[:nt], refs[nt:2 * nt], nb, refs[2 * nt:])
        start()
        finish()

    return pl.pallas_call(
        body, name=f"sibling_exchange_{tag}", out_shape=_exchange_shapes(bigs, smalls),
        in_specs=[ANY] * nt, out_specs=[ANY] * nt,
        scratch_shapes=[pltpu.SemaphoreType.DMA((nt,)), pltpu.SemaphoreType.DMA((nt,))],
    )(*bigs, *smalls)


def _pair_sum(core, mine, theirs, tag, block_rows):
    _, _, rows, cols = mine.shape
    steps = rows // block_rows

    def body(core_ref, a_ref, b_ref, o_ref):
        o_ref[...] = (a_ref[...].astype(F32) + b_ref[...].astype(F32)).astype(BF16)

    grid_spec = pltpu.PrefetchScalarGridSpec(
        num_scalar_prefetch=1, grid=(N_CHIPS, steps),
        in_specs=[pl.BlockSpec((None, None, block_rows, cols), lambda k, r, core_ref: (core_ref[0], k, r, 0)),
                  pl.BlockSpec((None, block_rows, cols), lambda k, r, core_ref: (k, r, 0))],
        out_specs=pl.BlockSpec((None, block_rows, cols), lambda k, r, core_ref: (k, r, 0)),
    )
    return pl.pallas_call(
        body, name=f"pair_sum_{tag}", grid_spec=grid_spec,
        out_shape=jax.ShapeDtypeStruct((N_CHIPS, rows, cols), BF16),
        compiler_params=pltpu.CompilerParams(dimension_semantics=("arbitrary", "arbitrary"), vmem_limit_bytes=VMEM_LIMIT),
    )(core, mine, theirs)


def _pair_sum_small(mine, theirs):
    (m_f2, m_b1, m_b2, m_sf, m_s5, m_sp) = mine

    def body(a0, a1, a2, a3, a4, a5, b0, b1, b2, b3, b4, b5, o_m, o_f, o_5, o_p):
        sm = (a0[...] + a1[...] + a2[...]) + (b0[...] + b1[...] + b2[...])
        sf = a3[...] + b3[...]
        s5 = a4[...] + b4[...]
        for h in range(2):
            o_m[h] = sm[:, h * (D_MODEL // 2):(h + 1) * (D_MODEL // 2)]
            o_f[h] = sf[:, h * (D_FF // 2):(h + 1) * (D_FF // 2)]
            o_5[h] = s5[:, h * (D_CONV // 2):(h + 1) * (D_CONV // 2)]
            for g in range(2):
                o_p[h, g] = a5[2 * h + g] + b5[2 * h + g]

    out_shape = [
        jax.ShapeDtypeStruct((2, 8, D_MODEL // 2), F32), jax.ShapeDtypeStruct((2, 8, D_FF // 2), F32),
        jax.ShapeDtypeStruct((2, 40, D_CONV // 2), F32), jax.ShapeDtypeStruct((2, 2, POOL_GROUP, POOL_GROUP), F32),
    ]
    return pl.pallas_call(body, name="pair_sum_small", out_shape=out_shape, in_specs=[VMEM] * 12, out_specs=[VMEM] * 4)(
        *mine, *theirs)


def _scatter_ops(ins, outs, n_parts, sems, stages, landed=False):
    ici_send, ici_recv, fwd_send, fwd_recv, loc_in, loc_out = sems
    nt = len(ins)
    x, y, c, k, chips = _place()

    def src_of(t, kk):
        return ins[t].at[kk] if t < n_parts else ins[t].at[c]

    def ici(t, j, kk, slot):
        return pltpu.make_async_remote_copy(
            src_ref=src_of(t, kk), dst_ref=outs[t].at[c, slot], send_sem=ici_send.at[t * 3 + j],
            recv_sem=ici_recv.at[t * 3 + j], device_id=(*chips[j], c), device_id_type=MESH)

    def fwd(t, half):
        slots = outs[t].at[half]
        return pltpu.make_async_remote_copy(
            src_ref=slots, dst_ref=slots, send_sem=fwd_send.at[t], recv_sem=fwd_recv.at[t],
            device_id=(x, y, 1 - c), device_id_type=MESH)

    local = [_staged(src_of(t, k), outs[t].at[c, k], stages[t], loc_in.at[t], loc_out.at[t]) for t in range(nt)]
    peers = [(t, j, 2 * qx + qy) for t in range(nt) for j, (qx, qy) in enumerate(chips)]
    sends = [] if landed else [ici(t, j, kq, k) for t, j, kq in peers]

    def start():
        for cp in local:
            cp[0]()
        for cp in sends:
            cp.start()

    def land():
        for cp in local:
            cp[1]()
        if not landed:
            for t, j, kq in peers:
                ici(t, j, kq, kq).wait_recv()
        for cp in local:
            cp[2]()
        for t in range(nt):
            fwd(t, c).start()

    def finish():
        for t in range(nt):
            fwd(t, 1 - c).wait_recv()
            fwd(t, c).wait_send()
        for cp in sends:
            cp.wait_send()

    return start, land, finish


def _scatter_scratch(parts, smalls):
    arrays = tuple(parts) + tuple(smalls)
    nt = len(arrays)
    return ([pltpu.SemaphoreType.DMA((3 * nt,))] * 2 + [pltpu.SemaphoreType.DMA((nt,))] * 4
            + [pltpu.VMEM(a.shape[1:], a.dtype) for a in arrays])


def _scatter_shapes(parts, smalls):
    return [jax.ShapeDtypeStruct((2, N_CHIPS) + p.shape[1:], p.dtype) for p in tuple(parts) + tuple(smalls)]


HBM_SPEC = pl.BlockSpec(memory_space=pltpu.HBM)
SEM_SPEC = pl.BlockSpec(memory_space=pltpu.SEMAPHORE)
EFFECT = pltpu.SideEffectType.DATAFLOW_SIDE_EFFECTING


def _ici_copy(ins, lands, n_parts, send, recv, t, j):
    _, _, c, k, chips = _place()
    qx, qy = chips[j]
    src = ins[t].at[2 * qx + qy] if t < n_parts else ins[t].at[c]
    return pltpu.make_async_remote_copy(
        src_ref=src, dst_ref=lands[t].at[c, k], send_sem=send.at[t * 3 + j], recv_sem=recv.at[t * 3 + j],
        device_id=(qx, qy, c), device_id_type=MESH)


def _scatter_start(parts, smalls):
    arrays = tuple(parts) + tuple(smalls)
    nt = len(arrays)

    def body(*refs):
        ins, lands = refs[:nt], refs[nt:2 * nt]
        send, recv = refs[2 * nt], refs[2 * nt + 1]
        token = refs[-1]
        for t in range(nt):
            for j in range(3):
                _ici_copy(ins, lands, len(parts), send, recv, t, j).start()
        token[...] = jnp.zeros(token.shape, F32)

    land_shapes = _scatter_shapes(parts, smalls)
    out_shape = ([pltpu.SemaphoreType.DMA((3 * nt,))] * 2 + [pltpu.HBM(a.shape, a.dtype) for a in arrays]
                 + [pltpu.HBM(a.shape, a.dtype) for a in land_shapes] + [jax.ShapeDtypeStruct((8, 128), F32)])
    operands = [pltpu.with_memory_space_constraint(a, pltpu.HBM) for a in arrays]
    operands += [pltpu.with_memory_space_constraint(lax.empty(a.shape, a.dtype), pltpu.HBM) for a in land_shapes]
    outs = pl.pallas_call(
        body, name="scatter_start", out_shape=out_shape, in_specs=[HBM_SPEC] * (2 * nt),
        out_specs=[SEM_SPEC] * 2 + [HBM_SPEC] * (2 * nt) + [VMEM],
        input_output_aliases={i: 2 + i for i in range(2 * nt)},
        compiler_params=pltpu.CompilerParams(has_side_effects=EFFECT),
    )(*operands)
    return outs[0], outs[1], outs[2:2 + nt], outs[2 + nt:2 + 2 * nt], outs[-1]


def _scatter_wait(send, recv, ins, lands, n_parts, after):
    nt = len(ins)

    def body(*refs):
        in_refs, land_refs = refs[:nt], refs[nt:2 * nt]
        send_ref, recv_ref = refs[2 * nt], refs[2 * nt + 1]
        for t in range(nt):
            for j in range(3):
                cp = _ici_copy(in_refs, land_refs, n_parts, send_ref, recv_ref, t, j)
                cp.wait_send()
                cp.wait_recv()

    outs = pl.pallas_call(
        body, name="scatter_wait", out_shape=[pltpu.HBM(a.shape, a.dtype) for a in tuple(ins) + tuple(lands)],
        in_specs=[HBM_SPEC] * (2 * nt) + [SEM_SPEC] * 2 + [ANY] * len(after), out_specs=[HBM_SPEC] * (2 * nt),
        input_output_aliases={i: i for i in range(2 * nt)},
        compiler_params=pltpu.CompilerParams(has_side_effects=EFFECT),
    )(*ins, *lands, send, recv, *after)
    return outs[:nt], outs[nt:]


def _scatter_forward(ins, lands, n_parts):
    nt = len(ins)

    def body(*refs):
        start, land, finish = _scatter_ops(
            refs[:nt], refs[2 * nt:3 * nt], n_parts, refs[3 * nt:3 * nt + 6], refs[3 * nt + 6:], landed=True)
        start()
        land()
        finish()

    return pl.pallas_call(
        body, name="scatter_forward", out_shape=[jax.ShapeDtypeStruct(a.shape, a.dtype) for a in lands],
        in_specs=[ANY] * (2 * nt), out_specs=[ANY] * nt, input_output_aliases={nt + i: i for i in range(nt)},
        scratch_shapes=_scatter_scratch(ins[:n_parts], ins[n_parts:]),
    )(*ins, *lands)


def _chip_scatter(parts, smalls):
    nt = len(parts) + len(smalls)

    def body(*refs):
        start, land, finish = _scatter_ops(refs[:nt], refs[nt:2 * nt], len(parts), refs[2 * nt:2 * nt + 6], refs[2 * nt + 6:])
        start()
        land()
        finish()

    return pl.pallas_call(
        body, name="chip_scatter", out_shape=_scatter_shapes(parts, smalls), in_specs=[ANY] * nt, out_specs=[ANY] * nt,
        scratch_shapes=_scatter_scratch(parts, smalls),
    )(*parts, *smalls)


def _adamw(w, g, m, v):
    m = ADAM_B1 * m + (1.0 - ADAM_B1) * g
    v = ADAM_B2 * v + (1.0 - ADAM_B2) * (g * g)
    m_hat = m / (1.0 - ADAM_B1 ** ADAM_STEP)
    v_hat = v / (1.0 - ADAM_B2 ** ADAM_STEP)
    delta = -ADAM_LR * (m_hat / (jnp.sqrt(v_hat) + ADAM_EPS) + ADAM_WD * w)
    return delta, m, v


def _adam_big(parts, w, m, v, tag, block_rows, token):
    _, _, rows, cols = parts.shape
    steps = rows // block_rows

    def body(p_ref, w_ref, m_ref, v_ref, token_ref, g_out, d_out, m_out, v_out):
        g = p_ref[0].astype(F32)
        for q in range(1, N_CHIPS):
            g = g + p_ref[q].astype(F32)
        delta, m_new, v_new = _adamw(w_ref[...], g, m_ref[...], v_ref[...])
        g_out[...] = g
        d_out[...] = delta
        m_out[...] = m_new
        v_out[...] = v_new

    blk = pl.BlockSpec((block_rows, cols), lambda h, r: (h * steps + r, 0))
    return pl.pallas_call(
        body, name=f"adam_{tag}", grid=(2, steps),
        in_specs=[pl.BlockSpec((None, N_CHIPS, block_rows, cols), lambda h, r: (h, 0, r, 0)), blk, blk, blk, ANY],
        out_specs=[blk] * 4, out_shape=[jax.ShapeDtypeStruct(w.shape, F32)] * 4,
        compiler_params=pltpu.CompilerParams(dimension_semantics=("arbitrary", "arbitrary"), vmem_limit_bytes=VMEM_LIMIT),
    )(parts, w, m, v, token)


def _reduce_small(l_m, l_f, l_5, l_p):
    def total(ref):
        t = ref[:, 0]
        for q in range(1, N_CHIPS):
            t = t + ref[:, q]
        return t

    def body(m_ref, f_ref, s_ref, p_ref, g1_o, g2_o, g3_o, loss_o, wf_o, fb_o, wa_o, cb_o, lg_o, lb_o, ps_o, pw_o):
        tm, tf, t5, tp = total(m_ref), total(f_ref), total(s_ref), total(p_ref)
        sm = jnp.concatenate([tm[0], tm[1]], axis=1)
        sf = jnp.concatenate([tf[0], tf[1]], axis=1)
        s5 = jnp.concatenate([t5[0], t5[1]], axis=1)
        g1_o[...] = sm[0:1]
        g2_o[...] = sm[1:2]
        g3_o[...] = sm[2:3]
        loss_o[...] = sm[3:4, 0:128]
        wf_o[...] = sf
        fb_o[...] = sf[3:4]
        wa_o[...] = s5[0:32]
        cb_o[...] = s5[32:33]
        lg_o[...] = s5[33:34]
        lb_o[...] = s5[34:35]
        ps_o[...] = s5[35:36]
        for h in range(2):
            for g in range(2):
                pw_o[2 * h + g] = tp[h, g]

    row = lambda w: jax.ShapeDtypeStruct((1, w), F32)
    out_shape = [row(D_MODEL), row(D_MODEL), row(D_MODEL), row(128), jax.ShapeDtypeStruct((8, D_FF), F32), row(D_FF),
                 jax.ShapeDtypeStruct((32, D_CONV), F32), row(D_CONV), row(D_CONV), row(D_CONV), row(D_POOL),
                 jax.ShapeDtypeStruct((4, POOL_GROUP, POOL_GROUP), F32)]
    return pl.pallas_call(body, name="reduce_small", out_shape=out_shape, in_specs=[VMEM] * 4, out_specs=[VMEM] * 12)(
        l_m, l_f, l_5, l_p)


def _adam_small(ws, gs, ms, vs):
    count = len(ws)

    def body(*refs):
        w_r, g_r, m_r, v_r = (refs[t * count:(t + 1) * count] for t in range(4))
        d_o, m_o, v_o = (refs[(4 + t) * count:(5 + t) * count] for t in range(3))
        for t in range(count):
            delta, m_new, v_new = _adamw(w_r[t][...], g_r[t][...], m_r[t][...], v_r[t][...])
            d_o[t][...] = delta
            m_o[t][...] = m_new
            v_o[t][...] = v_new

    out_shape = [jax.ShapeDtypeStruct(w.shape, F32) for w in ws] * 3
    outs = pl.pallas_call(body, name="adam_small", out_shape=out_shape, in_specs=[VMEM] * (4 * count),
                          out_specs=[VMEM] * (3 * count))(*ws, *gs, *ms, *vs)
    return outs[:count], outs[count:2 * count], outs[2 * count:]


MIX_TILE = 512
FFN_TILE = 256
GRAD_K = 2048


def kernel(x, norm_mix_g, w_in, conv_a_w, conv_a_b, ln_a_g, ln_a_b, pool_w, pool_scale, w_out, norm_ffn_g, w_up, conv_f_w, conv_f_b, w_down, norm_final_g, loss_target, m_norm_mix_g, m_w_in, m_conv_a_w, m_conv_a_b, m_ln_a_g, m_ln_a_b, m_pool_w, m_pool_scale, m_w_out, m_norm_ffn_g, m_w_up, m_conv_f_w, m_conv_f_b, m_w_down, m_norm_final_g, v_norm_mix_g, v_w_in, v_conv_a_w, v_conv_a_b, v_ln_a_g, v_ln_a_b, v_pool_w, v_pool_scale, v_w_out, v_norm_ffn_g, v_w_up, v_conv_f_w, v_conv_f_b, v_w_down, v_norm_final_g):
    seq = x.shape[1]
    xs, ts = x[0], loss_target[0]
    mix_tile, ffn_tile, grad_k = min(MIX_TILE, seq), min(FFN_TILE, seq), min(GRAD_K, seq)
    chip = 2 * lax.axis_index("x") + lax.axis_index("y")
    core = lax.axis_index("c").astype(jnp.int32).reshape(1)

    wa_s = jnp.pad(conv_a_w[0], ((0, 32 - CONV_A), (0, 0)))
    wf_s = jnp.pad(conv_f_w[0], ((0, 8 - CONV_F), (0, 0)))
    win_b, wout_b, wup_b, wdown_b = _cast_shards(w_in[0], w_out[0], w_up[0], w_down[0])
    g3 = norm_final_g.reshape(1, D_MODEL)
    pw = pool_w[0]

    h1, proj, cpre, dpool, mcat, x1, win, wout, wup, wa_g, wf_g = _mixer_fwd(
        xs, norm_mix_g, win_b, wout_b, wup_b, wa_s, wf_s, conv_a_b, ln_a_g, ln_a_b, pw, pool_scale, mix_tile)
    wa = jnp.transpose(wa_g, (1, 0, 2)).reshape(32, D_CONV)
    wf = jnp.transpose(wf_g, (1, 0, 2)).reshape(8, D_FF)
    h2, up, gcs, act, wdown = _ffn_up(x1, norm_ffn_g, wup, wf, conv_f_b, wdown_b, ffn_tile)
    dx2, dx2b, sm_f2 = _ffn_down(x1, act, wdown, g3, ts, mix_tile)
    tags = ("w_in", "w_out", "w_up", "w_down")
    blocks = (256, 128, 256, 176)
    g_wdown = _weight_grad(act, dx2b, "rows2", grad_k)
    dup, dx1, dx1b, sm_b1, sf, l_wdown = _ffn_bwd(
        dx2, up, gcs, x1, norm_ffn_g, wup, wf, wdown, ("exchange", [g_wdown]), ffn_tile)
    p_wdown = _pair_sum(core, g_wdown, l_wdown, tags[3], blocks[3])
    g_wup, s_wdown = _weight_grad(h2, dup, "cols_chip", grad_k, ("scatter", [p_wdown]))
    g_wout, l_wup = _weight_grad(mcat, dx1b, "rows1", grad_k, ("exchange", [g_wup]))
    p_wup = _pair_sum(core, g_wup, l_wup, tags[2], blocks[2])
    l_wout, = _sibling_exchange((g_wout,), (), "early")
    p_wout = _pair_sum(core, g_wout, l_wout, tags[1], blocks[1])
    dproj, gx, sm_b2, s5, sp, s_wout, s_wup = _mixer_bwd(
        dx1, xs, proj, cpre, dpool, norm_mix_g, win, wa, ln_a_g, ln_a_b, pw, pool_scale, wout, [p_wout, p_wup], mix_tile)
    g_win, grad_x = _weight_grad(h1, dproj, "cols_half", grad_k, carry=gx)

    smalls = (sm_f2, sm_b1, sm_b2, sf, s5, sp)
    landed = _sibling_exchange((g_win,), smalls, "late")
    part_win = _pair_sum(core, g_win, landed[0], tags[0], blocks[0])
    small_parts = _pair_sum_small(smalls, landed[1:])
    send, recv, late_src, late_land, token = _scatter_start([part_win], small_parts)
    big_w = (w_in[0], w_out[0], w_up[0], w_down[0])
    big_m = (m_w_in[0], m_w_out[0], m_w_up[0], m_w_down[0])
    big_v = (v_w_in[0], v_w_out[0], v_w_up[0], v_w_down[0])
    big = {}
    for t, p in ((1, s_wout), (2, s_wup), (3, s_wdown)):
        big[tags[t]] = _adam_big(p, big_w[t], big_m[t], big_v[t], tags[t], blocks[t], token)
    late_src, late_land = _scatter_wait(send, recv, late_src, late_land, 1, [big[tags[t]][3] for t in (1, 2, 3)])
    late = _scatter_forward(late_src, late_land, 1)
    big[tags[0]] = _adam_big(late[0], big_w[0], big_m[0], big_v[0], tags[0], blocks[0], token)
    big = {tag: [a[None] for a in outs] for tag, outs in big.items()}
    scattered = [None] * 4 + list(late[1:])

    (g_g1, g_g2, g_g3, loss_row, g_wf_all, g_fb, g_wa_all, g_cb, g_lg, g_lb, g_ps, g_pw) = _reduce_small(*scattered[4:])
    g_wa = lax.dynamic_slice(g_wa_all, (0, chip * (D_CONV // N_CHIPS)), (32, D_CONV // N_CHIPS))[:CONV_A]
    g_wf = lax.dynamic_slice(g_wf_all, (0, chip * (D_FF // N_CHIPS)), (8, D_FF // N_CHIPS))[:CONV_F]
    small_names = ("norm_mix_g", "conv_a_w", "conv_a_b", "ln_a_g", "ln_a_b", "pool_w", "pool_scale", "norm_ffn_g",
                   "conv_f_w", "conv_f_b", "norm_final_g")
    small_w = (norm_mix_g, conv_a_w[0], conv_a_b, ln_a_g, ln_a_b, pw, pool_scale, norm_ffn_g, conv_f_w[0], conv_f_b, g3)
    small_m = (m_norm_mix_g, m_conv_a_w[0], m_conv_a_b, m_ln_a_g, m_ln_a_b, m_pool_w[0], m_pool_scale, m_norm_ffn_g,
               m_conv_f_w[0], m_conv_f_b, m_norm_final_g.reshape(1, D_MODEL))
    small_v = (v_norm_mix_g, v_conv_a_w[0], v_conv_a_b, v_ln_a_g, v_ln_a_b, v_pool_w[0], v_pool_scale, v_norm_ffn_g,
               v_conv_f_w[0], v_conv_f_b, v_norm_final_g.reshape(1, D_MODEL))
    small_g = (g_g1, g_wa, g_cb, g_lg, g_lb, g_pw, g_ps, g_g2, g_wf, g_fb, g_g3)
    s_delta, s_m, s_v = _adam_small(small_w, small_g, small_m, small_v)
    shapes = {"conv_a_w": conv_a_w.shape, "pool_w": pool_w.shape, "conv_f_w": conv_f_w.shape, "norm_final_g": norm_final_g.shape}
    small = {}
    for t, name in enumerate(small_names):
        shp = shapes.get(name)
        small[name] = [a if shp is None else a.reshape(shp) for a in (small_g[t], s_delta[t], s_m[t], s_v[t])]

    order = ("norm_mix_g", "w_in", "conv_a_w", "conv_a_b", "ln_a_g", "ln_a_b", "pool_w", "pool_scale", "w_out", "norm_ffn_g",
             "w_up", "conv_f_w", "conv_f_b", "w_down", "norm_final_g")
    table = {**big, **small}
    loss = loss_row[0, 0]
    outs = [loss, grad_x[None]]
    for t in range(4):
        outs += [table[name][t] for name in order]
    return tuple(outs)
```

```python
import functools

import jax
import jax.numpy as jnp
from jax import lax
from jax.experimental import pallas as pl
from jax.experimental.pallas import tpu as pltpu

F32 = jnp.float32
BF16 = jnp.bfloat16
EPS = 1e-6
ADAM_LR = 0.001
ADAM_B1 = 0.9
ADAM_B2 = 0.999
ADAM_EPS = 1e-08
ADAM_WD = 0.01
ADAM_STEP = 10

D_MODEL = 1024
D_CONV = 512
D_POOL = 512
D_IN = 1536
D_FF = 2816
CONV_A = 31
CONV_F = 3
POOL_WINDOWS = (2, 4, 8, 16)
POOL_GROUP = 128
N_CHIPS = 4
FF_CHUNK = 256
N_FF_CHUNKS = D_FF // FF_CHUNK
A_HALO = 32
P_HALO = 16
VMEM_LIMIT = 56 * 1024 * 1024
MESH = pl.DeviceIdType.MESH

ANY = pl.BlockSpec(memory_space=pl.ANY)
VMEM = pl.BlockSpec(memory_space=pltpu.VMEM)


def _dot(a, b):
    return jnp.dot(a, b, preferred_element_type=F32)


def _dot_nt(a, b):
    return lax.dot_general(a, b, (((1,), (1,)), ((), ())), preferred_element_type=F32)


def _dot_tn(a, b):
    return lax.dot_general(a, b, (((0,), (0,)), ((), ())), preferred_element_type=F32)


def _sigmoid(v):
    return jax.nn.sigmoid(v)


def _colsum(v):
    return jnp.sum(v, axis=0, keepdims=True)


def _rowmean(v):
    return jnp.mean(v, axis=-1, keepdims=True)


def _place():
    x, y, c = lax.axis_index("x"), lax.axis_index("y"), lax.axis_index("c")
    chips = [(1 - x, y), (x, 1 - y), (1 - x, 1 - y)]
    return x, y, c, 2 * x + y, chips


SIBLING_ONLY, SIBLING_AND_CHIPS = 0, 1


def _handshake(collective):
    x, y, c, _, chips = _place()
    peers = [(x, y, 1 - c)] + ([(*chip, c) for chip in chips] if collective == SIBLING_AND_CHIPS else [])
    barrier = pltpu.get_barrier_semaphore()
    for peer in peers:
        pl.semaphore_signal(barrier, inc=1, device_id=peer, device_id_type=MESH)
    pl.semaphore_wait(barrier, len(peers))


def _staged(src, dst, stage, sem_in, sem_out):
    hop_in = pltpu.make_async_copy(src, stage, sem_in)
    hop_out = pltpu.make_async_copy(stage, dst, sem_out)

    def relay():
        hop_in.wait()
        hop_out.start()

    return hop_in.start, relay, hop_out.wait


def _gather_ops(bufs, fulls, col_sharded, sems, stages):
    ici_send, ici_recv, fwd_send, fwd_recv, loc_in, loc_out = sems
    n_big = len(bufs)
    x, y, c, k, chips = _place()

    def block(i, kk, half=None):
        rows, cols = bufs[i].shape
        if col_sharded[i]:
            rs = slice(None) if half is None else pl.ds(pl.multiple_of(half * (rows // 2), 16), rows // 2)
            return fulls[i].at[rs, pl.ds(pl.multiple_of(kk * cols, 128), cols)]
        if half is None:
            return fulls[i].at[pl.ds(pl.multiple_of(kk * rows, 16), rows), :]
        return fulls[i].at[pl.ds(pl.multiple_of(kk * rows + half * (rows // 2), 16), rows // 2), :]

    def my_half(i):
        rows = bufs[i].shape[0]
        return bufs[i].at[pl.ds(pl.multiple_of(c * (rows // 2), 16), rows // 2), :]

    def ici(i, j, kk):
        return pltpu.make_async_remote_copy(
            src_ref=my_half(i), dst_ref=block(i, kk, c), send_sem=ici_send.at[i * 3 + j], recv_sem=ici_recv.at[i * 3 + j],
            device_id=(*chips[j], c), device_id_type=MESH)

    def fwd(i, j, kk, half):
        return pltpu.make_async_remote_copy(
            src_ref=block(i, kk, half), dst_ref=block(i, kk, half), send_sem=fwd_send.at[i * 3 + j],
            recv_sem=fwd_recv.at[i * 3 + j], device_id=(x, y, 1 - c), device_id_type=MESH)

    local = [_staged(bufs[i], block(i, k), stages[i], loc_in.at[i], loc_out.at[i]) for i in range(n_big)]
    sends = [ici(i, j, k) for i in range(n_big) for j in range(3)]
    peers = [(i, j, 2 * qx + qy) for i in range(n_big) for j, (qx, qy) in enumerate(chips)]

    def start():
        for cp in local:
            cp[0]()
        for cp in sends:
            cp.start()

    def land():
        for cp in local:
            cp[1]()
        for i, j, kq in peers:
            ici(i, j, kq).wait_recv()
            fwd(i, j, kq, c).start()

    def finish():
        for i, j, kq in peers:
            fwd(i, j, kq, 1 - c).wait_recv()
            fwd(i, j, kq, c).wait_send()
        for cp in sends:
            cp.wait_send()
        for cp in local:
            cp[2]()

    return start, land, finish


def _gather_scratch(shards):
    n_big = len(shards)
    return ([pltpu.SemaphoreType.DMA((3 * n_big,))] * 4 + [pltpu.SemaphoreType.DMA((n_big,))] * 2
            + [pltpu.VMEM(b.shape, b.dtype) for b in shards])


def _tap_ops(srcs, dsts, sems):
    send, recv, loc = sems
    _, _, c, k, chips = _place()

    def copy(t, j, kk):
        return pltpu.make_async_remote_copy(
            src_ref=srcs[t], dst_ref=dsts[t].at[kk], send_sem=send.at[t * 3 + j], recv_sem=recv.at[t * 3 + j],
            device_id=(*chips[j], c), device_id_type=MESH)

    local = [pltpu.make_async_copy(srcs[t], dsts[t].at[k], loc.at[t]) for t in range(len(srcs))]
    sends = [[copy(t, j, k) for j in range(3)] for t in range(len(srcs))]

    def start():
        for t, cp in enumerate(local):
            cp.start()
            for sd in sends[t]:
                sd.start()

    def wait(t):
        for j, (qx, qy) in enumerate(chips):
            copy(t, j, 2 * qx + qy).wait_recv()
        for sd in sends[t]:
            sd.wait_send()
        local[t].wait()

    return start, wait


def _cast_shards(*shards):
    def body(*refs):
        for src, dst in zip(refs[:len(shards)], refs[len(shards):]):
            dst[...] = src[...].astype(BF16)

    return pl.pallas_call(
        body, name="cast_shards", out_shape=[jax.ShapeDtypeStruct(s.shape, BF16) for s in shards],
        in_specs=[VMEM] * len(shards), out_specs=[VMEM] * len(shards),
        compiler_params=pltpu.CompilerParams(vmem_limit_bytes=VMEM_LIMIT),
    )(*shards)


def _load_weights(pairs, sem, first=0):
    cps = [pltpu.make_async_copy(src, dst, sem.at[first + i]) for i, (src, dst) in enumerate(pairs)]
    for cp in cps:
        cp.start()
    for cp in cps:
        cp.wait()


def _shifted_views(buf, shifted, t_rows):
    n = t_rows + A_HALO - 8
    for b in range(1, 8):
        shifted[b - 1] = buf[b:b + n, :]

    def view(offset):
        a, b = divmod(offset, 8)
        if b == 0:
            return buf[8 * a:8 * a + t_rows, :]
        return shifted[b - 1, 8 * a:8 * a + t_rows, :]

    return view


def _pool_count(tile, t_rows, w):
    row = lax.broadcasted_iota(jnp.int32, (t_rows, POOL_GROUP), 0) + tile * t_rows
    return jnp.minimum(row + 1, w).astype(F32)


def _mixer_fwd(x, g1, win_b, wout_b, wup_b, wa_s, wf_s, cb, lg, lb, pw, ps, tile_rows):
    seq = x.shape[0]
    tr = tile_rows
    n = seq // tr

    def body(x_ref, g1_ref, win_b_hbm, wout_b_hbm, wup_b_hbm, wa_s_hbm, wf_s_hbm, cb_ref, lg_ref, lb_ref, pw_ref,
             ps_ref, h1_ref, proj_ref, c_ref, d_ref, m_ref, x1_ref, win_f, wout_f, wup_f, wa_g, wf_g,
             win_v, wout_v, wa_ref, ubuf, ushift, bbuf, sem, *csems):
        i = pl.program_id(0)
        first_sems, first_stages, second_sems, second_stages, later_sems, later_stages, tap_sems = (
            csems[0:6], csems[6:7], csems[7:13], csems[13:14], csems[14:20], csems[20:21], csems[21:24])

        def first():
            return _gather_ops((win_b_hbm,), (win_f,), (True,), first_sems, first_stages)

        def second():
            return _gather_ops((wout_b_hbm,), (wout_f,), (False,), second_sems, second_stages)

        def later():
            return _gather_ops((wup_b_hbm,), (wup_f,), (True,), later_sems, later_stages)

        def taps():
            return _tap_ops((wa_s_hbm, wf_s_hbm), (wa_g, wf_g), tap_sems)

        @pl.when(i == 0)
        def _():
            _handshake(SIBLING_AND_CHIPS)
            first()[0]()
            taps()[0]()
            second()[0]()
            later()[0]()
            first()[1]()
            first()[2]()
            _load_weights([(win_f, win_v)], sem)
            ubuf[0:A_HALO, :] = jnp.zeros((A_HALO, D_CONV), F32)
            bbuf[0:P_HALO, :] = jnp.zeros((P_HALO, D_POOL), F32)

        xv = x_ref[...]
        r = lax.rsqrt(_rowmean(xv * xv) + EPS)
        h1 = (xv * r * g1_ref[...]).astype(BF16)
        h1_ref[...] = h1
        proj = _dot(h1, win_v[...])
        proj_ref[...] = proj.astype(BF16)

        @pl.when(i == 0)
        def _():
            taps()[1](0)
            _load_weights([(wa_g.at[kk], wa_ref.at[:, kk * (D_CONV // N_CHIPS):(kk + 1) * (D_CONV // N_CHIPS)])
                           for kk in range(N_CHIPS)], sem, 2)

        av, ag, bi = proj[:, :D_CONV], proj[:, D_CONV:2 * D_CONV], proj[:, 2 * D_CONV:]
        ubuf[A_HALO:A_HALO + tr, :] = av * _sigmoid(ag)
        off = A_HALO - (CONV_A - 1)
        uview = _shifted_views(ubuf, ushift, tr)
        acc = wa_ref[0:1, :] * uview(off)
        for j in range(1, CONV_A):
            acc = acc + wa_ref[j:j + 1, :] * uview(off + j)
        cv = acc + cb_ref[...]
        ubuf[0:A_HALO, :] = ubuf[tr:tr + A_HALO, :]
        c_ref[...] = cv.astype(BF16)
        xc = cv - _rowmean(cv)
        z = xc * lax.rsqrt(_rowmean(xc * xc) + EPS)
        ln = z * lg_ref[...] + lb_ref[...]
        ya = ln * _sigmoid(ln)
        bbuf[P_HALO:P_HALO + tr, :] = bi
        ds, ybs = [], []
        for g, w in enumerate(POOL_WINDOWS):
            cols = slice(g * POOL_GROUP, (g + 1) * POOL_GROUP)
            s = bi[:, cols]
            for kk in range(1, w):
                s = s + bbuf[P_HALO - kk:P_HALO - kk + tr, cols]
            dg = s / _pool_count(i, tr, w) - bi[:, cols]
            ds.append(dg)
            ybs.append(_dot(dg.astype(BF16), pw_ref[g].astype(BF16)))
        bbuf[0:P_HALO, :] = bbuf[tr:tr + P_HALO, :]
        d_ref[...] = jnp.concatenate(ds, axis=1).astype(BF16)
        yb = jnp.concatenate(ybs, axis=1) * ps_ref[...]
        m = jnp.concatenate([ya, yb], axis=1).astype(BF16)
        m_ref[...] = m

        @pl.when(i == 0)
        def _():
            second()[1]()
            second()[2]()
            _load_weights([(wout_f, wout_v)], sem, 1)

        x1_ref[...] = xv + _dot(m, wout_v[...])

        @pl.when(i == n - 1)
        def _():
            later()[1]()
            later()[2]()
            taps()[1](1)

    tile = lambda w: pl.BlockSpec((tr, w), lambda i: (i, 0))
    full = lambda a: pl.BlockSpec(a.shape, lambda i: (0,) * a.ndim)
    return pl.pallas_call(
        body, name="mixer_fwd", grid=(n,),
        in_specs=[tile(D_MODEL), full(g1)] + [ANY] * 5 + [full(cb), full(lg), full(lb), full(pw), full(ps)],
        out_specs=[tile(D_MODEL), tile(D_IN), tile(D_CONV), tile(D_POOL), tile(D_MODEL), tile(D_MODEL)] + [ANY] * 5,
        out_shape=[
            jax.ShapeDtypeStruct((seq, D_MODEL), BF16), jax.ShapeDtypeStruct((seq, D_IN), BF16),
            jax.ShapeDtypeStruct((seq, D_CONV), BF16), jax.ShapeDtypeStruct((seq, D_POOL), BF16),
            jax.ShapeDtypeStruct((seq, D_MODEL), BF16), jax.ShapeDtypeStruct((seq, D_MODEL), F32),
            jax.ShapeDtypeStruct((D_MODEL, D_IN), BF16), jax.ShapeDtypeStruct((D_MODEL, D_MODEL), BF16),
            jax.ShapeDtypeStruct((D_MODEL, 2 * D_FF), BF16),
            jax.ShapeDtypeStruct((N_CHIPS,) + wa_s.shape, F32), jax.ShapeDtypeStruct((N_CHIPS,) + wf_s.shape, F32),
        ],
        scratch_shapes=[
            pltpu.VMEM((D_MODEL, D_IN), BF16), pltpu.VMEM((D_MODEL, D_MODEL), BF16), pltpu.VMEM((32, D_CONV), F32),
            pltpu.VMEM((tr + A_HALO, D_CONV), F32), pltpu.VMEM((7, tr + A_HALO - 8, D_CONV), F32),
            pltpu.VMEM((tr + P_HALO, D_POOL), F32), pltpu.SemaphoreType.DMA((2 + N_CHIPS,)),
        ] + _gather_scratch((win_b,)) + _gather_scratch((wout_b,)) + _gather_scratch((wup_b,)) + [
            pltpu.SemaphoreType.DMA((6,)), pltpu.SemaphoreType.DMA((6,)), pltpu.SemaphoreType.DMA((2,))],
        compiler_params=pltpu.CompilerParams(dimension_semantics=("arbitrary",), vmem_limit_bytes=VMEM_LIMIT,
                                             collective_id=SIBLING_AND_CHIPS),
    )(x, g1, win_b, wout_b, wup_b, wa_s, wf_s, cb, lg, lb, pw, ps)


def _ffn_up(x1, g2, wup, wf, fb, wdown_b, tile_rows):
    seq = x1.shape[0]
    tr = tile_rows
    n = seq // tr

    def body(x1_ref, g2_ref, wup_hbm, wf_ref, fb_ref, wdown_b_hbm,
             h2_ref, up_ref, gc_ref, act_ref, wdown_f, wup_v, gbuf, sem, *gsems):
        i = pl.program_id(0)

        def gather():
            return _gather_ops((wdown_b_hbm,), (wdown_f,), (False,), gsems[:6], gsems[6:])

        @pl.when(i == 0)
        def _():
            _handshake(SIBLING_AND_CHIPS)
            gather()[0]()
            _load_weights(((wup_hbm, wup_v),), sem)
            gbuf[0:8, :] = jnp.zeros((8, D_FF), F32)

        x1v = x1_ref[...]
        r2 = lax.rsqrt(_rowmean(x1v * x1v) + EPS)
        h2 = (x1v * r2 * g2_ref[...]).astype(BF16)
        h2_ref[...] = h2

        def up_proj(j):
            return (_dot(h2, wup_v[:, j * FF_CHUNK:(j + 1) * FF_CHUNK]),
                    _dot(h2, wup_v[:, D_FF + j * FF_CHUNK:D_FF + (j + 1) * FF_CHUNK]))

        ahead = up_proj(0)
        for j in range(N_FF_CHUNKS):
            cs = slice(j * FF_CHUNK, (j + 1) * FF_CHUNK)
            vs = slice(D_FF + j * FF_CHUNK, D_FF + (j + 1) * FF_CHUNK)
            gate, val = ahead
            if j + 1 < N_FF_CHUNKS:
                ahead = up_proj(j + 1)
            up_ref[:, cs] = gate.astype(BF16)
            up_ref[:, vs] = val.astype(BF16)
            gbuf[8:8 + tr, cs] = gate
            gc = (wf_ref[0:1, cs] * gbuf[6:6 + tr, cs] + wf_ref[1:2, cs] * gbuf[7:7 + tr, cs]
                  + wf_ref[2:3, cs] * gate + fb_ref[:, cs])
            gbuf[0:8, cs] = gbuf[tr:tr + 8, cs]
            gc_ref[:, cs] = gc.astype(BF16)
            act_ref[:, cs] = (gc * _sigmoid(gc) * val).astype(BF16)

        @pl.when(i == max(n - 2, 0))
        def _():
            gather()[1]()

        @pl.when(i == n - 1)
        def _():
            gather()[2]()

    tile = lambda w: pl.BlockSpec((tr, w), lambda i: (i, 0))
    full = lambda a: pl.BlockSpec(a.shape, lambda i: (0,) * a.ndim)
    return pl.pallas_call(
        body, name="ffn_up", grid=(n,),
        in_specs=[tile(D_MODEL), full(g2), ANY, full(wf), full(fb), ANY],
        out_specs=[tile(D_MODEL), tile(2 * D_FF), tile(D_FF), tile(D_FF), ANY],
        out_shape=[
            jax.ShapeDtypeStruct((seq, D_MODEL), BF16), jax.ShapeDtypeStruct((seq, 2 * D_FF), BF16),
            jax.ShapeDtypeStruct((seq, D_FF), BF16), jax.ShapeDtypeStruct((seq, D_FF), BF16),
            jax.ShapeDtypeStruct((D_FF, D_MODEL), BF16),
        ],
        scratch_shapes=[pltpu.VMEM(wup.shape, BF16), pltpu.VMEM((tr + 8, D_FF), F32), pltpu.SemaphoreType.DMA((1,))]
        + _gather_scratch((wdown_b,)),
        compiler_params=pltpu.CompilerParams(dimension_semantics=("arbitrary",), vmem_limit_bytes=VMEM_LIMIT,
                                             collective_id=SIBLING_AND_CHIPS),
    )(x1, g2, wup, wf, fb, wdown_b)


def _ffn_down(x1, act, wdown, g3, target, tile_rows):
    seq = x1.shape[0]
    tr = tile_rows
    n = seq // tr

    def body(x1_ref, act_ref, wdown_hbm, g3_ref, t_ref, dx2_ref, dx2b_ref, sm_ref, wdown_v, sem):
        i = pl.program_id(0)

        @pl.when(i == 0)
        def _():
            _load_weights(((wdown_hbm, wdown_v),), sem)
            sm_ref[...] = jnp.zeros(sm_ref.shape, F32)

        x2 = x1_ref[...] + _dot(act_ref[...], wdown_v[...])
        r3 = lax.rsqrt(_rowmean(x2 * x2) + EPS)
        n3 = x2 * r3
        err = n3 * g3_ref[...] - t_ref[...]
        dy = err / D_MODEL
        sm_ref[2:3, :] += _colsum(dy * n3)
        loss = 0.5 * _colsum(_rowmean(err * err))
        sm_ref[3:4, :] += jnp.broadcast_to(loss, (1, D_MODEL))
        dn = dy * g3_ref[...]
        dx2v = r3 * (dn - n3 * _rowmean(dn * n3))
        dx2_ref[...] = dx2v
        dx2b_ref[...] = dx2v.astype(BF16)

    tile = lambda w: pl.BlockSpec((tr, w), lambda i: (i, 0))
    full = lambda a: pl.BlockSpec(a.shape, lambda i: (0,) * a.ndim)
    return pl.pallas_call(
        body, name="ffn_down", grid=(n,),
        in_specs=[tile(D_MODEL), tile(D_FF), ANY, full(g3), tile(D_MODEL)],
        out_specs=[tile(D_MODEL), tile(D_MODEL), pl.BlockSpec((8, D_MODEL), lambda i: (0, 0))],
        out_shape=[
            jax.ShapeDtypeStruct((seq, D_MODEL), F32), jax.ShapeDtypeStruct((seq, D_MODEL), BF16),
            jax.ShapeDtypeStruct((8, D_MODEL), F32),
        ],
        scratch_shapes=[pltpu.VMEM(wdown.shape, BF16), pltpu.SemaphoreType.DMA((1,))],
        compiler_params=pltpu.CompilerParams(dimension_semantics=("arbitrary",), vmem_limit_bytes=VMEM_LIMIT),
    )(x1, act, wdown, g3, target)


def _ffn_bwd(dx2, up, gcs, x1, g2, wup, wf, wdown, comm, tile_rows):
    seq = x1.shape[0]
    c_ins, c_shapes, c_sems, c_ops, c_id = _comm_plan(comm)
    nc = len(c_ins)
    tr = tile_rows
    n = seq // tr

    def body(dx2_ref, up_ref, gc_ref, x1_ref, g2_ref, wup_hbm, wf_ref, wdown_hbm, *rest):
        c_in, rest = rest[:nc], rest[nc:]
        dup_ref, dx1_ref, dx1b_ref, sm_ref, sf_ref = rest[:5]
        c_out, rest = rest[5:5 + nc], rest[5 + nc:]
        wup_v, wdown_v, dbuf, dcar, sem = rest[:5]
        c_sem_refs = rest[5:]
        i = pl.program_id(0)

        @pl.when(i == 0)
        def _():
            c_ops(c_in, c_out, c_sem_refs)[0]()
            _load_weights(((wup_hbm, wup_v), (wdown_hbm, wdown_v)), sem)
            dcar[...] = jnp.zeros(dcar.shape, F32)
            sm_ref[...] = jnp.zeros(sm_ref.shape, F32)
            sf_ref[...] = jnp.zeros(sf_ref.shape, F32)

        dx2v = dx2_ref[...]
        dx2b = dx2v.astype(BF16)
        dh2 = jnp.zeros((tr, D_MODEL), F32)

        def down_t(j):
            return _dot_nt(dx2b, wdown_v[j * FF_CHUNK:(j + 1) * FF_CHUNK, :])

        ahead = down_t(0)
        for j in range(N_FF_CHUNKS):
            cs = slice(j * FF_CHUNK, (j + 1) * FF_CHUNK)
            vs = slice(D_FF + j * FF_CHUNK, D_FF + (j + 1) * FF_CHUNK)
            dact = ahead
            if j + 1 < N_FF_CHUNKS:
                ahead = down_t(j + 1)
            gate = up_ref[:, cs].astype(F32)
            val = up_ref[:, vs].astype(F32)
            gc = gc_ref[:, cs].astype(F32)
            sg = _sigmoid(gc)
            dval = dact * (gc * sg)
            dgc = dact * val * (sg * (1.0 + gc * (1.0 - sg)))
            dbuf[0:tr, :] = dgc
            dbuf[tr:tr + 8, :] = dcar[:, cs]
            d_p1 = dbuf[1:1 + tr, :]
            d_p2 = dbuf[2:2 + tr, :]
            dgate = wf_ref[2:3, cs] * dgc + wf_ref[1:2, cs] * d_p1 + wf_ref[0:1, cs] * d_p2
            dcar[:, cs] = dgc[0:8, :]
            sf_ref[0:1, cs] += _colsum(d_p2 * gate)
            sf_ref[1:2, cs] += _colsum(d_p1 * gate)
            sf_ref[2:3, cs] += _colsum(dgc * gate)
            sf_ref[3:4, cs] += _colsum(dgc)
            dgb, dvb = dgate.astype(BF16), dval.astype(BF16)
            dup_ref[:, cs] = dgb
            dup_ref[:, vs] = dvb
            dh2 = dh2 + _dot_nt(dgb, wup_v[:, cs]) + _dot_nt(dvb, wup_v[:, vs])
        x1v = x1_ref[...]
        r2 = lax.rsqrt(_rowmean(x1v * x1v) + EPS)
        n2 = x1v * r2
        sm_ref[1:2, :] += _colsum(dh2 * n2)
        dn2 = dh2 * g2_ref[...]
        dx1v = dx2v + r2 * (dn2 - n2 * _rowmean(dn2 * n2))
        dx1_ref[...] = dx1v
        dx1b_ref[...] = dx1v.astype(BF16)

        @pl.when(i == n - 1)
        def _():
            c_ops(c_in, c_out, c_sem_refs)[2]()

    tile = lambda w: pl.BlockSpec((tr, w), lambda i: (n - 1 - i, 0))
    full = lambda a: pl.BlockSpec(a.shape, lambda i: (0,) * a.ndim)
    acc = lambda rows, w: pl.BlockSpec((rows, w), lambda i: (0, 0))
    return pl.pallas_call(
        body, name="ffn_bwd", grid=(n,),
        in_specs=[tile(D_MODEL), tile(2 * D_FF), tile(D_FF), tile(D_MODEL), full(g2), ANY, full(wf), ANY] + [ANY] * nc,
        out_specs=[tile(2 * D_FF), tile(D_MODEL), tile(D_MODEL), acc(8, D_MODEL), acc(8, D_FF)] + [ANY] * nc,
        out_shape=[
            jax.ShapeDtypeStruct((seq, 2 * D_FF), BF16), jax.ShapeDtypeStruct((seq, D_MODEL), F32),
            jax.ShapeDtypeStruct((seq, D_MODEL), BF16), jax.ShapeDtypeStruct((8, D_MODEL), F32),
            jax.ShapeDtypeStruct((8, D_FF), F32),
        ] + c_shapes,
        scratch_shapes=[
            pltpu.VMEM(wup.shape, BF16), pltpu.VMEM(wdown.shape, BF16),
            pltpu.VMEM((tr + 8, FF_CHUNK), F32), pltpu.VMEM((8, D_FF), F32), pltpu.SemaphoreType.DMA((2,)),
        ] + c_sems,
        compiler_params=pltpu.CompilerParams(dimension_semantics=("arbitrary",), vmem_limit_bytes=VMEM_LIMIT,
                                             collective_id=c_id),
    )(dx2, up, gcs, x1, g2, wup, wf, wdown, *c_ins)


def _mixer_bwd(dx1, x, proj, cpre, d, g1, win, wa, lg, lb, pw, ps, wout, parts, tile_rows):
    seq = x.shape[0]
    n_parts = len(parts)
    tr = tile_rows
    n = seq // tr
    row_cb, row_lg, row_lb, row_ps = 32, 33, 34, 35

    def body(dx1_ref, x_ref, proj_ref, projh_ref, c_ref, d_ref, g1_ref, win_hbm, wa_ref, lg_ref, lb_ref, pw_ref, ps_ref,
             wout_hbm, *rest):
        part_refs, rest = rest[:n_parts], rest[n_parts:]
        dproj_ref, gx_ref, sm_ref, s5_ref, sp_ref = rest[:5]
        land_refs, rest = rest[5:5 + n_parts], rest[5 + n_parts:]
        win_v, wout_v, ubuf, ushift, dcbuf, dshift, ebuf, sem = rest[:8]
        ssems = rest[8:]
        i = pl.program_id(0)
        tile = n - 1 - i

        def scatter():
            return _scatter_ops(part_refs, land_refs, n_parts, ssems[:6], ssems[6:])

        @pl.when(i == 0)
        def _():
            _handshake(SIBLING_AND_CHIPS)
            scatter()[0]()
            _load_weights(((win_hbm, win_v), (wout_hbm, wout_v)), sem)
            dcbuf[tr:tr + A_HALO, :] = jnp.zeros((A_HALO, D_CONV), F32)
            ebuf[tr:tr + P_HALO, :] = jnp.zeros((P_HALO, D_POOL), F32)
            sm_ref[...] = jnp.zeros(sm_ref.shape, F32)
            s5_ref[...] = jnp.zeros(s5_ref.shape, F32)
            sp_ref[...] = jnp.zeros(sp_ref.shape, F32)

        dx1v = dx1_ref[...]
        dm = _dot_nt(dx1v.astype(BF16), wout_v[...])
        dya, dyb = dm[:, :D_CONV], dm[:, D_CONV:]
        dbis = []
        for g, w in enumerate(POOL_WINDOWS):
            cols = slice(g * POOL_GROUP, (g + 1) * POOL_GROUP)
            dgb = d_ref[:, cols]
            pwb = pw_ref[g].astype(BF16)
            dyg = dyb[:, cols]
            s5_ref[row_ps:row_ps + 1, cols] += _colsum(dyg * _dot(dgb, pwb))
            dqb = (dyg * ps_ref[:, cols]).astype(BF16)
            sp_ref[g] += _dot_tn(dgb, dqb)
            dd = _dot_nt(dqb, pwb)
            e = dd / _pool_count(tile, tr, w)
            ebuf[0:tr, cols] = e
            s = e
            for kk in range(1, w):
                s = s + ebuf[kk:kk + tr, cols]
            dbis.append(s - dd)
        ebuf[tr:tr + P_HALO, :] = ebuf[0:P_HALO, :]
        cv = c_ref[...].astype(F32)
        xc = cv - _rowmean(cv)
        rs = lax.rsqrt(_rowmean(xc * xc) + EPS)
        z = xc * rs
        ln = z * lg_ref[...] + lb_ref[...]
        sl = _sigmoid(ln)
        dl = dya * (sl * (1.0 + ln * (1.0 - sl)))
        s5_ref[row_lg:row_lg + 1, :] += _colsum(dl * z)
        s5_ref[row_lb:row_lb + 1, :] += _colsum(dl)
        dz = dl * lg_ref[...]
        dc = rs * (dz - _rowmean(dz) - z * _rowmean(dz * z))
        s5_ref[row_cb:row_cb + 1, :] += _colsum(dc)
        dcbuf[0:tr, :] = dc
        keep = (tile > 0).astype(F32)
        avh = projh_ref[:, :D_CONV].astype(F32)
        agh = projh_ref[:, D_CONV:].astype(F32)
        ubuf[0:A_HALO, :] = avh * _sigmoid(agh) * keep
        av = proj_ref[:, :D_CONV].astype(F32)
        ag = proj_ref[:, D_CONV:2 * D_CONV].astype(F32)
        sg = _sigmoid(ag)
        ubuf[A_HALO:A_HALO + tr, :] = av * sg
        off = A_HALO - (CONV_A - 1)
        du = wa_ref[CONV_A - 1:CONV_A, :] * dc
        dview = _shifted_views(dcbuf, dshift, tr)
        uview = _shifted_views(ubuf, ushift, tr)
        for j in range(CONV_A - 1):
            du = du + wa_ref[j:j + 1, :] * dview(CONV_A - 1 - j)
        for j in range(CONV_A):
            s5_ref[j:j + 1, :] += _colsum(dc * uview(off + j))
        dcbuf[tr:tr + A_HALO, :] = dcbuf[0:A_HALO, :]
        dav = du * sg
        dag = du * av * (sg * (1.0 - sg))
        dprojb = jnp.concatenate([dav, dag] + dbis, axis=1).astype(BF16)
        dproj_ref[...] = dprojb
        dh1 = _dot_nt(dprojb, win_v[...])
        xv = x_ref[...]
        r1 = lax.rsqrt(_rowmean(xv * xv) + EPS)
        n1 = xv * r1
        sm_ref[0:1, :] += _colsum(dh1 * n1)
        dn1 = dh1 * g1_ref[...]
        gx_ref[...] = dx1v + r1 * (dn1 - n1 * _rowmean(dn1 * n1))

        @pl.when(i == max(n - 2, 0))
        def _():
            scatter()[1]()

        @pl.when(i == n - 1)
        def _():
            scatter()[2]()

    tile = lambda w: pl.BlockSpec((tr, w), lambda i: (n - 1 - i, 0))
    full = lambda a: pl.BlockSpec(a.shape, lambda i: (0,) * a.ndim)
    halo = pl.BlockSpec((A_HALO, 2 * D_CONV), lambda i: (jnp.maximum((n - 1 - i) * (tr // A_HALO) - 1, 0), 0))
    acc = lambda shape: pl.BlockSpec(shape, lambda i: (0,) * len(shape))
    return pl.pallas_call(
        body, name="mixer_bwd", grid=(n,),
        in_specs=[tile(D_MODEL), tile(D_MODEL), tile(D_IN), halo, tile(D_CONV), tile(D_POOL), full(g1), ANY, full(wa),
                  full(lg), full(lb), full(pw), full(ps), ANY] + [ANY] * n_parts,
        out_specs=[tile(D_IN), tile(D_MODEL), acc((8, D_MODEL)), acc((40, D_CONV)), acc(pw.shape)] + [ANY] * n_parts,
        out_shape=[
            jax.ShapeDtypeStruct((seq, D_IN), BF16), jax.ShapeDtypeStruct((seq, D_MODEL), F32),
            jax.ShapeDtypeStruct((8, D_MODEL), F32), jax.ShapeDtypeStruct((40, D_CONV), F32),
            jax.ShapeDtypeStruct(pw.shape, F32),
        ] + _scatter_shapes(parts, ()),
        scratch_shapes=[
            pltpu.VMEM(win.shape, BF16), pltpu.VMEM(wout.shape, BF16),
            pltpu.VMEM((tr + A_HALO, D_CONV), F32), pltpu.VMEM((7, tr + A_HALO - 8, D_CONV), F32),
            pltpu.VMEM((tr + A_HALO, D_CONV), F32), pltpu.VMEM((7, tr + A_HALO - 8, D_CONV), F32),
            pltpu.VMEM((tr + P_HALO, D_POOL), F32), pltpu.SemaphoreType.DMA((2,)),
        ] + _scatter_scratch(parts, ()),
        compiler_params=pltpu.CompilerParams(dimension_semantics=("arbitrary",), vmem_limit_bytes=VMEM_LIMIT,
                                             collective_id=SIBLING_AND_CHIPS),
    )(dx1, x, proj, proj, cpre, d, g1, win, wa, lg, lb, pw, ps, wout, *parts)


def _weight_grad(a, b, layout, k_rows, comm=None, carry=None):
    seq, m_dim = a.shape
    n_dim = b.shape[1]
    steps = seq // k_rows

    def store(o_ref, acc, index, value):
        if steps == 1:
            o_ref[index] = value.astype(BF16)
            return
        s = pl.program_id(1)

        @pl.when(s == 0)
        def _():
            acc[index] = value

        @pl.when(jnp.logical_and(s > 0, s < steps - 1))
        def _():
            acc[index] += value

        @pl.when(s == steps - 1)
        def _():
            o_ref[index] = (acc[index] + value).astype(BF16)

    if layout in ("rows1", "rows2"):
        groups = int(layout[-1])
        per_tile = N_CHIPS // groups
        rows = m_dim // N_CHIPS // 2
        a_w = m_dim // groups

        def body(a_ref, b_ref, o_ref, acc):
            r = _dot_tn(a_ref[...], b_ref[...])
            for p in range(per_tile):
                for h in range(2):
                    store(o_ref, acc, (h, p), r[(2 * p + h) * rows:(2 * p + h + 1) * rows, :])

        in_specs = [pl.BlockSpec((k_rows, a_w), lambda g, s: (s, g)), pl.BlockSpec((k_rows, n_dim), lambda g, s: (s, 0))]
        out_spec = pl.BlockSpec((2, per_tile, rows, n_dim), lambda g, s: (0, g, 0, 0))
        out_dims, acc_dims = (2, N_CHIPS, rows, n_dim), (2, per_tile, rows, n_dim)
    elif layout == "cols_chip":
        groups = N_CHIPS
        rows, cols = m_dim // 2, n_dim // N_CHIPS

        def body(a_ref, b_ref, o_ref, acc):
            r = _dot_tn(a_ref[...], b_ref[...])
            for h in range(2):
                store(o_ref, acc, h, r[h * rows:(h + 1) * rows, :])

        in_specs = [pl.BlockSpec((k_rows, m_dim), lambda g, s: (s, 0)), pl.BlockSpec((k_rows, cols), lambda g, s: (s, g))]
        out_spec = pl.BlockSpec((2, None, rows, cols), lambda g, s: (0, g, 0, 0))
        out_dims, acc_dims = (2, N_CHIPS, rows, cols), (2, rows, cols)
    else:
        groups = 2
        rows, cols = m_dim // 2, n_dim // N_CHIPS

        def body(a_ref, b_ref, o_ref, acc):
            r = _dot_tn(a_ref[...], b_ref[...])
            for k in range(N_CHIPS):
                store(o_ref, acc, k, r[:, k * cols:(k + 1) * cols])

        in_specs = [pl.BlockSpec((k_rows, rows), lambda g, s: (s, g)), pl.BlockSpec((k_rows, n_dim), lambda g, s: (s, 0))]
        out_spec = pl.BlockSpec((None, N_CHIPS, rows, cols), lambda g, s: (g, 0, 0, 0))
        out_dims, acc_dims = (2, N_CHIPS, rows, cols), (N_CHIPS, rows, cols)

    c_ins, c_shapes, c_sems, c_ops, c_id = _comm_plan(comm)
    nc = len(c_ins)
    c_specs = [ANY] * nc
    if carry is not None:
        assert comm is None and carry.shape[0] % (groups * steps) == 0
        carry_spec = pl.BlockSpec((carry.shape[0] // (groups * steps), carry.shape[1]), lambda g, s: (g * steps + s, 0))
        c_ins, c_shapes, c_specs, nc = (carry,), [jax.ShapeDtypeStruct(carry.shape, carry.dtype)], [carry_spec], 1

    def hosted(a_ref, b_ref, *rest):
        c_in, o_ref, c_out, acc, sems = rest[:nc], rest[nc], rest[nc + 1:2 * nc + 1], rest[2 * nc + 1], rest[2 * nc + 2:]
        g, s = pl.program_id(0), pl.program_id(1)
        if carry is not None:
            c_out[0][...] = c_in[0][...]
            body(a_ref, b_ref, o_ref, acc)
            return
        if nc:
            @pl.when(jnp.logical_and(g == 0, s == 0))
            def _():
                c_ops(c_in, c_out, sems)[0]()

        body(a_ref, b_ref, o_ref, acc)
        if nc:
            step = g * steps + s

            @pl.when(step == max(groups * steps - 2, 0))
            def _():
                c_ops(c_in, c_out, sems)[1]()

            @pl.when(step == groups * steps - 1)
            def _():
                c_ops(c_in, c_out, sems)[2]()

    outs = pl.pallas_call(
        hosted, name=f"weight_grad_{layout}_{m_dim}x{n_dim}", grid=(groups, steps),
        in_specs=in_specs + c_specs, out_specs=[out_spec] + c_specs,
        out_shape=[jax.ShapeDtypeStruct(out_dims, BF16)] + c_shapes,
        scratch_shapes=[pltpu.VMEM(acc_dims, F32)] + c_sems,
        compiler_params=pltpu.CompilerParams(dimension_semantics=("arbitrary", "arbitrary"), vmem_limit_bytes=VMEM_LIMIT,
                                             collective_id=c_id),
    )(a, b, *c_ins)
    return outs if nc else outs[0]


def _exchange_ops(ins, outs, n_big, sems):
    send, recv = sems
    x, y, c, _, _ = _place()
    cps = [pltpu.make_async_remote_copy(
        src_ref=ins[t].at[1 - c] if t < n_big else ins[t], dst_ref=outs[t], send_sem=send.at[t], recv_sem=recv.at[t],
        device_id=(x, y, 1 - c), device_id_type=MESH) for t in range(len(ins))]

    def start():
        for cp in cps:
            cp.start()

    def finish():
        for cp in cps:
            cp.wait()

    return start, finish


def _exchange_shapes(bigs, smalls):
    return [jax.ShapeDtypeStruct((N_CHIPS,) + b.shape[2:], b.dtype) for b in bigs] + [
        jax.ShapeDtypeStruct(s.shape, s.dtype) for s in smalls]


def _comm_plan(comm):
    if comm is None:
        return (), [], [], None, None
    kind, arrays = comm
    n = len(arrays)

    def scatter(i, o, sm):
        start, land, finish = _scatter_ops(i, o, n, sm[:6], sm[6:])
        return lambda: (_handshake(SIBLING_AND_CHIPS), start()), land, finish

    def exchange(i, o, sm):
        start, finish = _exchange_ops(i, o, n, sm)
        return lambda: (_handshake(SIBLING_ONLY), start()), lambda: None, finish

    if kind == "scatter":
        return tuple(arrays), _scatter_shapes(arrays, ()), _scatter_scratch(arrays, ()), scatter, SIBLING_AND_CHIPS
    return tuple(arrays), _exchange_shapes(arrays, ()), [pltpu.SemaphoreType.DMA((n,))] * 2, exchange, SIBLING_ONLY


def _sibling_exchange(bigs, smalls, tag):
    nb, nt = len(bigs), len(bigs) + len(smalls)

    def body(*refs):
        start, finish = _exchange_ops(refs[:nt], refs[nt:2 * nt], nb, refs[2 * nt:])
        _handshake(SIBLING_ONLY)
        start()
        finish()

    return pl.pallas_call(
        body, name=f"sibling_exchange_{tag}", out_shape=_exchange_shapes(bigs, smalls),
        in_specs=[ANY] * nt, out_specs=[ANY] * nt,
        scratch_shapes=[pltpu.SemaphoreType.DMA((nt,)), pltpu.SemaphoreType.DMA((nt,))],
        compiler_params=pltpu.CompilerParams(collective_id=SIBLING_ONLY),
    )(*bigs, *smalls)


def _pair_sum(core, mine, theirs, tag, block_rows):
    _, _, rows, cols = mine.shape
    steps = rows // block_rows

    def body(core_ref, a_ref, b_ref, o_ref):
        o_ref[...] = (a_ref[...].astype(F32) + b_ref[...].astype(F32)).astype(BF16)

    grid_spec = pltpu.PrefetchScalarGridSpec(
        num_scalar_prefetch=1, grid=(N_CHIPS, steps),
        in_specs=[pl.BlockSpec((None, None, block_rows, cols), lambda k, r, core_ref: (core_ref[0], k, r, 0)),
                  pl.BlockSpec((None, block_rows, cols), lambda k, r, core_ref: (k, r, 0))],
        out_specs=pl.BlockSpec((None, block_rows, cols), lambda k, r, core_ref: (k, r, 0)),
    )
    return pl.pallas_call(
        body, name=f"pair_sum_{tag}", grid_spec=grid_spec,
        out_shape=jax.ShapeDtypeStruct((N_CHIPS, rows, cols), BF16),
        compiler_params=pltpu.CompilerParams(dimension_semantics=("arbitrary", "arbitrary"), vmem_limit_bytes=VMEM_LIMIT),
    )(core, mine, theirs)


def _pair_sum_small(mine, theirs):
    (m_f2, m_b1, m_b2, m_sf, m_s5, m_sp) = mine

    def body(a0, a1, a2, a3, a4, a5, b0, b1, b2, b3, b4, b5, o_m, o_f, o_5, o_p):
        sm = (a0[...] + a1[...] + a2[...]) + (b0[...] + b1[...] + b2[...])
        sf = a3[...] + b3[...]
        s5 = a4[...] + b4[...]
        for h in range(2):
            o_m[h] = sm[:, h * (D_MODEL // 2):(h + 1) * (D_MODEL // 2)]
            o_f[h] = sf[:, h * (D_FF // 2):(h + 1) * (D_FF // 2)]
            o_5[h] = s5[:, h * (D_CONV // 2):(h + 1) * (D_CONV // 2)]
            for g in range(2):
                o_p[h, g] = a5[2 * h + g] + b5[2 * h + g]

    out_shape = [
        jax.ShapeDtypeStruct((2, 8, D_MODEL // 2), F32), jax.ShapeDtypeStruct((2, 8, D_FF // 2), F32),
        jax.ShapeDtypeStruct((2, 40, D_CONV // 2), F32), jax.ShapeDtypeStruct((2, 2, POOL_GROUP, POOL_GROUP), F32),
    ]
    return pl.pallas_call(body, name="pair_sum_small", out_shape=out_shape, in_specs=[VMEM] * 12, out_specs=[VMEM] * 4)(
        *mine, *theirs)


def _scatter_ops(ins, outs, n_parts, sems, stages, landed=False):
    ici_send, ici_recv, fwd_send, fwd_recv, loc_in, loc_out = sems
    nt = len(ins)
    x, y, c, k, chips = _place()

    def src_of(t, kk):
        return ins[t].at[kk] if t < n_parts else ins[t].at[c]

    def ici(t, j, kk, slot):
        return pltpu.make_async_remote_copy(
            src_ref=src_of(t, kk), dst_ref=outs[t].at[c, slot], send_sem=ici_send.at[t * 3 + j],
            recv_sem=ici_recv.at[t * 3 + j], device_id=(*chips[j], c), device_id_type=MESH)

    def fwd(t, half):
        slots = outs[t].at[half]
        return pltpu.make_async_remote_copy(
            src_ref=slots, dst_ref=slots, send_sem=fwd_send.at[t], recv_sem=fwd_recv.at[t],
            device_id=(x, y, 1 - c), device_id_type=MESH)

    local = [_staged(src_of(t, k), outs[t].at[c, k], stages[t], loc_in.at[t], loc_out.at[t]) for t in range(nt)]
    peers = [(t, j, 2 * qx + qy) for t in range(nt) for j, (qx, qy) in enumerate(chips)]
    sends = [] if landed else [ici(t, j, kq, k) for t, j, kq in peers]

    def start():
        for cp in local:
            cp[0]()
        for cp in sends:
            cp.start()

    def land():
        for cp in local:
            cp[1]()
        if not landed:
            for t, j, kq in peers:
                ici(t, j, kq, kq).wait_recv()
        for cp in local:
            cp[2]()
        for t in range(nt):
            fwd(t, c).start()

    def finish():
        for t in range(nt):
            fwd(t, 1 - c).wait_recv()
            fwd(t, c).wait_send()
        for cp in sends:
            cp.wait_send()

    return start, land, finish


def _scatter_scratch(parts, smalls):
    arrays = tuple(parts) + tuple(smalls)
    nt = len(arrays)
    return ([pltpu.SemaphoreType.DMA((3 * nt,))] * 2 + [pltpu.SemaphoreType.DMA((nt,))] * 4
            + [pltpu.VMEM(a.shape[1:], a.dtype) for a in arrays])


def _scatter_shapes(parts, smalls):
    return [jax.ShapeDtypeStruct((2, N_CHIPS) + p.shape[1:], p.dtype) for p in tuple(parts) + tuple(smalls)]


HBM_SPEC = pl.BlockSpec(memory_space=pltpu.HBM)
SEM_SPEC = pl.BlockSpec(memory_space=pltpu.SEMAPHORE)
EFFECT = pltpu.SideEffectType.DATAFLOW_SIDE_EFFECTING


def _ici_copy(ins, lands, n_parts, send, recv, t, j):
    _, _, c, k, chips = _place()
    qx, qy = chips[j]
    src = ins[t].at[2 * qx + qy] if t < n_parts else ins[t].at[c]
    return pltpu.make_async_remote_copy(
        src_ref=src, dst_ref=lands[t].at[c, k], send_sem=send.at[t * 3 + j], recv_sem=recv.at[t * 3 + j],
        device_id=(qx, qy, c), device_id_type=MESH)


def _scatter_start(parts, smalls):
    arrays = tuple(parts) + tuple(smalls)
    nt = len(arrays)

    def body(*refs):
        ins, lands = refs[:nt], refs[nt:2 * nt]
        send, recv = refs[2 * nt], refs[2 * nt + 1]
        token = refs[-1]
        for t in range(nt):
            for j in range(3):
                _ici_copy(ins, lands, len(parts), send, recv, t, j).start()
        token[...] = jnp.zeros(token.shape, F32)

    land_shapes = _scatter_shapes(parts, smalls)
    out_shape = ([pltpu.SemaphoreType.DMA((3 * nt,))] * 2 + [pltpu.HBM(a.shape, a.dtype) for a in arrays]
                 + [pltpu.HBM(a.shape, a.dtype) for a in land_shapes] + [jax.ShapeDtypeStruct((8, 128), F32)])
    operands = [pltpu.with_memory_space_constraint(a, pltpu.HBM) for a in arrays]
    operands += [pltpu.with_memory_space_constraint(lax.empty(a.shape, a.dtype), pltpu.HBM) for a in land_shapes]
    outs = pl.pallas_call(
        body, name="scatter_start", out_shape=out_shape, in_specs=[HBM_SPEC] * (2 * nt),
        out_specs=[SEM_SPEC] * 2 + [HBM_SPEC] * (2 * nt) + [VMEM],
        input_output_aliases={i: 2 + i for i in range(2 * nt)},
        compiler_params=pltpu.CompilerParams(has_side_effects=EFFECT),
    )(*operands)
    return outs[0], outs[1], outs[2:2 + nt], outs[2 + nt:2 + 2 * nt], outs[-1]


def _scatter_wait(send, recv, ins, lands, n_parts, after):
    nt = len(ins)

    def body(*refs):
        in_refs, land_refs = refs[:nt], refs[nt:2 * nt]
        send_ref, recv_ref = refs[2 * nt], refs[2 * nt + 1]
        for t in range(nt):
            for j in range(3):
                cp = _ici_copy(in_refs, land_refs, n_parts, send_ref, recv_ref, t, j)
                cp.wait_send()
                cp.wait_recv()

    outs = pl.pallas_call(
        body, name="scatter_wait", out_shape=[pltpu.HBM(a.shape, a.dtype) for a in tuple(ins) + tuple(lands)],
        in_specs=[HBM_SPEC] * (2 * nt) + [SEM_SPEC] * 2 + [ANY] * len(after), out_specs=[HBM_SPEC] * (2 * nt),
        input_output_aliases={i: i for i in range(2 * nt)},
        compiler_params=pltpu.CompilerParams(has_side_effects=EFFECT),
    )(*ins, *lands, send, recv, *after)
    return outs[:nt], outs[nt:]


def _scatter_forward(ins, lands, n_parts):
    nt = len(ins)

    def body(*refs):
        start, land, finish = _scatter_ops(
            refs[:nt], refs[2 * nt:3 * nt], n_parts, refs[3 * nt:3 * nt + 6], refs[3 * nt + 6:], landed=True)
        _handshake(SIBLING_ONLY)
        start()
        land()
        finish()

    return pl.pallas_call(
        body, name="scatter_forward", out_shape=[jax.ShapeDtypeStruct(a.shape, a.dtype) for a in lands],
        in_specs=[ANY] * (2 * nt), out_specs=[ANY] * nt, input_output_aliases={nt + i: i for i in range(nt)},
        scratch_shapes=_scatter_scratch(ins[:n_parts], ins[n_parts:]),
        compiler_params=pltpu.CompilerParams(collective_id=SIBLING_ONLY),
    )(*ins, *lands)


def _chip_scatter(parts, smalls):
    nt = len(parts) + len(smalls)

    def body(*refs):
        start, land, finish = _scatter_ops(refs[:nt], refs[nt:2 * nt], len(parts), refs[2 * nt:2 * nt + 6], refs[2 * nt + 6:])
        _handshake(SIBLING_AND_CHIPS)
        start()
        land()
        finish()

    return pl.pallas_call(
        body, name="chip_scatter", out_shape=_scatter_shapes(parts, smalls), in_specs=[ANY] * nt, out_specs=[ANY] * nt,
        scratch_shapes=_scatter_scratch(parts, smalls),
        compiler_params=pltpu.CompilerParams(collective_id=SIBLING_AND_CHIPS),
    )(*parts, *smalls)


def _adamw(w, g, m, v):
    m = ADAM_B1 * m + (1.0 - ADAM_B1) * g
    v = ADAM_B2 * v + (1.0 - ADAM_B2) * (g * g)
    m_hat = m / (1.0 - ADAM_B1 ** ADAM_STEP)
    v_hat = v / (1.0 - ADAM_B2 ** ADAM_STEP)
    delta = -ADAM_LR * (m_hat / (jnp.sqrt(v_hat) + ADAM_EPS) + ADAM_WD * w)
    return delta, m, v


def _adam_big(parts, w, m, v, tag, block_rows, token):
    _, _, rows, cols = parts.shape
    steps = rows // block_rows

    def body(p_ref, w_ref, m_ref, v_ref, token_ref, g_out, d_out, m_out, v_out):
        g = p_ref[0].astype(F32)
        for q in range(1, N_CHIPS):
            g = g + p_ref[q].astype(F32)
        delta, m_new, v_new = _adamw(w_ref[...], g, m_ref[...], v_ref[...])
        g_out[...] = g
        d_out[...] = delta
        m_out[...] = m_new
        v_out[...] = v_new

    blk = pl.BlockSpec((block_rows, cols), lambda h, r: (h * steps + r, 0))
    return pl.pallas_call(
        body, name=f"adam_{tag}", grid=(2, steps),
        in_specs=[pl.BlockSpec((None, N_CHIPS, block_rows, cols), lambda h, r: (h, 0, r, 0)), blk, blk, blk, ANY],
        out_specs=[blk] * 4, out_shape=[jax.ShapeDtypeStruct(w.shape, F32)] * 4,
        compiler_params=pltpu.CompilerParams(dimension_semantics=("arbitrary", "arbitrary"), vmem_limit_bytes=VMEM_LIMIT),
    )(parts, w, m, v, token)


def _reduce_small(l_m, l_f, l_5, l_p):
    def total(ref):
        t = ref[:, 0]
        for q in range(1, N_CHIPS):
            t = t + ref[:, q]
        return t

    def body(m_ref, f_ref, s_ref, p_ref, g1_o, g2_o, g3_o, loss_o, wf_o, fb_o, wa_o, cb_o, lg_o, lb_o, ps_o, pw_o):
        tm, tf, t5, tp = total(m_ref), total(f_ref), total(s_ref), total(p_ref)
        sm = jnp.concatenate([tm[0], tm[1]], axis=1)
        sf = jnp.concatenate([tf[0], tf[1]], axis=1)
        s5 = jnp.concatenate([t5[0], t5[1]], axis=1)
        g1_o[...] = sm[0:1]
        g2_o[...] = sm[1:2]
        g3_o[...] = sm[2:3]
        loss_o[...] = sm[3:4, 0:128]
        wf_o[...] = sf
        fb_o[...] = sf[3:4]
        wa_o[...] = s5[0:32]
        cb_o[...] = s5[32:33]
        lg_o[...] = s5[33:34]
        lb_o[...] = s5[34:35]
        ps_o[...] = s5[35:36]
        for h in range(2):
            for g in range(2):
                pw_o[2 * h + g] = tp[h, g]

    row = lambda w: jax.ShapeDtypeStruct((1, w), F32)
    out_shape = [row(D_MODEL), row(D_MODEL), row(D_MODEL), row(128), jax.ShapeDtypeStruct((8, D_FF), F32), row(D_FF),
                 jax.ShapeDtypeStruct((32, D_CONV), F32), row(D_CONV), row(D_CONV), row(D_CONV), row(D_POOL),
                 jax.ShapeDtypeStruct((4, POOL_GROUP, POOL_GROUP), F32)]
    return pl.pallas_call(body, name="reduce_small", out_shape=out_shape, in_specs=[VMEM] * 4, out_specs=[VMEM] * 12)(
        l_m, l_f, l_5, l_p)


def _adam_small(ws, gs, ms, vs):
    count = len(ws)

    def body(*refs):
        w_r, g_r, m_r, v_r = (refs[t * count:(t + 1) * count] for t in range(4))
        d_o, m_o, v_o = (refs[(4 + t) * count:(5 + t) * count] for t in range(3))
        for t in range(count):
            delta, m_new, v_new = _adamw(w_r[t][...], g_r[t][...], m_r[t][...], v_r[t][...])
            d_o[t][...] = delta
            m_o[t][...] = m_new
            v_o[t][...] = v_new

    out_shape = [jax.ShapeDtypeStruct(w.shape, F32) for w in ws] * 3
    outs = pl.pallas_call(body, name="adam_small", out_shape=out_shape, in_specs=[VMEM] * (4 * count),
                          out_specs=[VMEM] * (3 * count))(*ws, *gs, *ms, *vs)
    return outs[:count], outs[count:2 * count], outs[2 * count:]


MIX_TILE = 512
FFN_TILE = 256
GRAD_K = 2048


def kernel(x, norm_mix_g, w_in, conv_a_w, conv_a_b, ln_a_g, ln_a_b, pool_w, pool_scale, w_out, norm_ffn_g, w_up, conv_f_w, conv_f_b, w_down, norm_final_g, loss_target, m_norm_mix_g, m_w_in, m_conv_a_w, m_conv_a_b, m_ln_a_g, m_ln_a_b, m_pool_w, m_pool_scale, m_w_out, m_norm_ffn_g, m_w_up, m_conv_f_w, m_conv_f_b, m_w_down, m_norm_final_g, v_norm_mix_g, v_w_in, v_conv_a_w, v_conv_a_b, v_ln_a_g, v_ln_a_b, v_pool_w, v_pool_scale, v_w_out, v_norm_ffn_g, v_w_up, v_conv_f_w, v_conv_f_b, v_w_down, v_norm_final_g):
    seq = x.shape[1]
    xs, ts = x[0], loss_target[0]
    mix_tile, ffn_tile, grad_k = min(MIX_TILE, seq), min(FFN_TILE, seq), min(GRAD_K, seq)
    chip = 2 * lax.axis_index("x") + lax.axis_index("y")
    core = lax.axis_index("c").astype(jnp.int32).reshape(1)

    wa_s = jnp.pad(conv_a_w[0], ((0, 32 - CONV_A), (0, 0)))
    wf_s = jnp.pad(conv_f_w[0], ((0, 8 - CONV_F), (0, 0)))
    win_b, wout_b, wup_b, wdown_b = _cast_shards(w_in[0], w_out[0], w_up[0], w_down[0])
    g3 = norm_final_g.reshape(1, D_MODEL)
    pw = pool_w[0]

    h1, proj, cpre, dpool, mcat, x1, win, wout, wup, wa_g, wf_g = _mixer_fwd(
        xs, norm_mix_g, win_b, wout_b, wup_b, wa_s, wf_s, conv_a_b, ln_a_g, ln_a_b, pw, pool_scale, mix_tile)
    wa = jnp.transpose(wa_g, (1, 0, 2)).reshape(32, D_CONV)
    wf = jnp.transpose(wf_g, (1, 0, 2)).reshape(8, D_FF)
    h2, up, gcs, act, wdown = _ffn_up(x1, norm_ffn_g, wup, wf, conv_f_b, wdown_b, ffn_tile)
    dx2, dx2b, sm_f2 = _ffn_down(x1, act, wdown, g3, ts, mix_tile)
    tags = ("w_in", "w_out", "w_up", "w_down")
    blocks = (256, 128, 256, 176)
    g_wdown = _weight_grad(act, dx2b, "rows2", grad_k)
    dup, dx1, dx1b, sm_b1, sf, l_wdown = _ffn_bwd(
        dx2, up, gcs, x1, norm_ffn_g, wup, wf, wdown, ("exchange", [g_wdown]), ffn_tile)
    p_wdown = _pair_sum(core, g_wdown, l_wdown, tags[3], blocks[3])
    g_wup, s_wdown = _weight_grad(h2, dup, "cols_chip", grad_k, ("scatter", [p_wdown]))
    g_wout, l_wup = _weight_grad(mcat, dx1b, "rows1", grad_k, ("exchange", [g_wup]))
    p_wup = _pair_sum(core, g_wup, l_wup, tags[2], blocks[2])
    l_wout, = _sibling_exchange((g_wout,), (), "early")
    p_wout = _pair_sum(core, g_wout, l_wout, tags[1], blocks[1])
    dproj, gx, sm_b2, s5, sp, s_wout, s_wup = _mixer_bwd(
        dx1, xs, proj, cpre, dpool, norm_mix_g, win, wa, ln_a_g, ln_a_b, pw, pool_scale, wout, [p_wout, p_wup], mix_tile)
    g_win, grad_x = _weight_grad(h1, dproj, "cols_half", grad_k, carry=gx)

    smalls = (sm_f2, sm_b1, sm_b2, sf, s5, sp)
    landed = _sibling_exchange((g_win,), smalls, "late")
    part_win = _pair_sum(core, g_win, landed[0], tags[0], blocks[0])
    small_parts = _pair_sum_small(smalls, landed[1:])
    send, recv, late_src, late_land, token = _scatter_start([part_win], small_parts)
    big_w = (w_in[0], w_out[0], w_up[0], w_down[0])
    big_m = (m_w_in[0], m_w_out[0], m_w_up[0], m_w_down[0])
    big_v = (v_w_in[0], v_w_out[0], v_w_up[0], v_w_down[0])
    big = {}
    for t, p in ((1, s_wout), (2, s_wup), (3, s_wdown)):
        big[tags[t]] = _adam_big(p, big_w[t], big_m[t], big_v[t], tags[t], blocks[t], token)
    late_src, late_land = _scatter_wait(send, recv, late_src, late_land, 1, [big[tags[t]][3] for t in (1, 2, 3)])
    late = _scatter_forward(late_src, late_land, 1)
    big[tags[0]] = _adam_big(late[0], big_w[0], big_m[0], big_v[0], tags[0], blocks[0], token)
    big = {tag: [a[None] for a in outs] for tag, outs in big.items()}
    scattered = [None] * 4 + list(late[1:])

    (g_g1, g_g2, g_g3, loss_row, g_wf_all, g_fb, g_wa_all, g_cb, g_lg, g_lb, g_ps, g_pw) = _reduce_small(*scattered[4:])
    g_wa = lax.dynamic_slice(g_wa_all, (0, chip * (D_CONV // N_CHIPS)), (32, D_CONV // N_CHIPS))[:CONV_A]
    g_wf = lax.dynamic_slice(g_wf_all, (0, chip * (D_FF // N_CHIPS)), (8, D_FF // N_CHIPS))[:CONV_F]
    small_names = ("norm_mix_g", "conv_a_w", "conv_a_b", "ln_a_g", "ln_a_b", "pool_w", "pool_scale", "norm_ffn_g",
                   "conv_f_w", "conv_f_b", "norm_final_g")
    small_w = (norm_mix_g, conv_a_w[0], conv_a_b, ln_a_g, ln_a_b, pw, pool_scale, norm_ffn_g, conv_f_w[0], conv_f_b, g3)
    small_m = (m_norm_mix_g, m_conv_a_w[0], m_conv_a_b, m_ln_a_g, m_ln_a_b, m_pool_w[0], m_pool_scale, m_norm_ffn_g,
               m_conv_f_w[0], m_conv_f_b, m_norm_final_g.reshape(1, D_MODEL))
    small_v = (v_norm_mix_g, v_conv_a_w[0], v_conv_a_b, v_ln_a_g, v_ln_a_b, v_pool_w[0], v_pool_scale, v_norm_ffn_g,
               v_conv_f_w[0], v_conv_f_b, v_norm_final_g.reshape(1, D_MODEL))
    small_g = (g_g1, g_wa, g_cb, g_lg, g_lb, g_pw, g_ps, g_g2, g_wf, g_fb, g_g3)
    s_delta, s_m, s_v = _adam_small(small_w, small_g, small_m, small_v)
    shapes = {"conv_a_w": conv_a_w.shape, "pool_w": pool_w.shape, "conv_f_w": conv_f_w.shape, "norm_final_g": norm_final_g.shape}
    small = {}
    for t, name in enumerate(small_names):
        shp = shapes.get(name)
        small[name] = [a if shp is None else a.reshape(shp) for a in (small_g[t], s_delta[t], s_m[t], s_v[t])]

    order = ("norm_mix_g", "w_in", "conv_a_w", "conv_a_b", "ln_a_g", "ln_a_b", "pool_w", "pool_scale", "w_out", "norm_ffn_g",
             "w_up", "conv_f_w", "conv_f_b", "w_down", "norm_final_g")
    table = {**big, **small}
    loss = loss_row[0, 0]
    outs = [loss, grad_x[None]]
    for t in range(4):
        outs += [table[name][t] for name in order]
    return tuple(outs)
```

```python
import functools

import jax
import jax.numpy as jnp
from jax import lax
from jax.experimental import pallas as pl
from jax.experimental.pallas import tpu as pltpu

F32 = jnp.float32
BF16 = jnp.bfloat16
EPS = 1e-6
ADAM_LR = 0.001
ADAM_B1 = 0.9
ADAM_B2 = 0.999
ADAM_EPS = 1e-08
ADAM_WD = 0.01
ADAM_STEP = 10

D_MODEL = 1024
D_CONV = 512
D_POOL = 512
D_IN = 1536
D_FF = 2816
CONV_A = 31
CONV_F = 3
POOL_WINDOWS = (2, 4, 8, 16)
POOL_GROUP = 128
N_CHIPS = 4
FF_CHUNK = 256
N_FF_CHUNKS = D_FF // FF_CHUNK
A_HALO = 32
P_HALO = 16
VMEM_LIMIT = 56 * 1024 * 1024
MESH = pl.DeviceIdType.MESH

ANY = pl.BlockSpec(memory_space=pl.ANY)
VMEM = pl.BlockSpec(memory_space=pltpu.VMEM)


def _dot(a, b):
    return jnp.dot(a, b, preferred_element_type=F32)


def _dot_nt(a, b):
    return lax.dot_general(a, b, (((1,), (1,)), ((), ())), preferred_element_type=F32)


def _dot_tn(a, b):
    return lax.dot_general(a, b, (((0,), (0,)), ((), ())), preferred_element_type=F32)


def _sigmoid(v):
    return jax.nn.sigmoid(v)


def _colsum(v):
    return jnp.sum(v, axis=0, keepdims=True)


def _rowmean(v):
    return jnp.mean(v, axis=-1, keepdims=True)


def _place():
    x, y, c = lax.axis_index("x"), lax.axis_index("y"), lax.axis_index("c")
    chips = [(1 - x, y), (x, 1 - y), (1 - x, 1 - y)]
    return x, y, c, 2 * x + y, chips


SIBLING_ONLY, SIBLING_AND_CHIPS = 0, 1


def _handshake(collective):
    x, y, c, _, chips = _place()
    peers = [(x, y, 1 - c)] + ([(*chip, c) for chip in chips] if collective == SIBLING_AND_CHIPS else [])
    barrier = pltpu.get_barrier_semaphore()
    for peer in peers:
        pl.semaphore_signal(barrier, inc=1, device_id=peer, device_id_type=MESH)
    pl.semaphore_wait(barrier, len(peers))


def _staged(src, dst, stage, sem_in, sem_out):
    hop_in = pltpu.make_async_copy(src, stage, sem_in)
    hop_out = pltpu.make_async_copy(stage, dst, sem_out)

    def relay():
        hop_in.wait()
        hop_out.start()

    return hop_in.start, relay, hop_out.wait


def _gather_ops(bufs, fulls, col_sharded, sems, stages):
    ici_send, ici_recv, fwd_send, fwd_recv, loc_in, loc_out = sems
    n_big = len(bufs)
    x, y, c, k, chips = _place()

    def block(i, kk, half=None):
        rows, cols = bufs[i].shape
        if col_sharded[i]:
            rs = slice(None) if half is None else pl.ds(pl.multiple_of(half * (rows // 2), 16), rows // 2)
            return fulls[i].at[rs, pl.ds(pl.multiple_of(kk * cols, 128), cols)]
        if half is None:
            return fulls[i].at[pl.ds(pl.multiple_of(kk * rows, 16), rows), :]
        return fulls[i].at[pl.ds(pl.multiple_of(kk * rows + half * (rows // 2), 16), rows // 2), :]

    def my_half(i):
        rows = bufs[i].shape[0]
        return bufs[i].at[pl.ds(pl.multiple_of(c * (rows // 2), 16), rows // 2), :]

    def ici(i, j, kk):
        return pltpu.make_async_remote_copy(
            src_ref=my_half(i), dst_ref=block(i, kk, c), send_sem=ici_send.at[i * 3 + j], recv_sem=ici_recv.at[i * 3 + j],
            device_id=(*chips[j], c), device_id_type=MESH)

    def fwd(i, j, kk, half):
        return pltpu.make_async_remote_copy(
            src_ref=block(i, kk, half), dst_ref=block(i, kk, half), send_sem=fwd_send.at[i * 3 + j],
            recv_sem=fwd_recv.at[i * 3 + j], device_id=(x, y, 1 - c), device_id_type=MESH)

    local = [_staged(bufs[i], block(i, k), stages[i], loc_in.at[i], loc_out.at[i]) for i in range(n_big)]
    sends = [ici(i, j, k) for i in range(n_big) for j in range(3)]
    peers = [(i, j, 2 * qx + qy) for i in range(n_big) for j, (qx, qy) in enumerate(chips)]

    def start():
        for cp in local:
            cp[0]()
        for cp in sends:
            cp.start()

    def land():
        for cp in local:
            cp[1]()
        for i, j, kq in peers:
            ici(i, j, kq).wait_recv()
            fwd(i, j, kq, c).start()

    def finish():
        for i, j, kq in peers:
            fwd(i, j, kq, 1 - c).wait_recv()
            fwd(i, j, kq, c).wait_send()
        for cp in sends:
            cp.wait_send()
        for cp in local:
            cp[2]()

    return start, land, finish


def _gather_scratch(shards):
    n_big = len(shards)
    return ([pltpu.SemaphoreType.DMA((3 * n_big,))] * 4 + [pltpu.SemaphoreType.DMA((n_big,))] * 2
            + [pltpu.VMEM(b.shape, b.dtype) for b in shards])


def _tap_ops(srcs, dsts, sems):
    send, recv, loc = sems
    _, _, c, k, chips = _place()

    def copy(t, j, kk):
        return pltpu.make_async_remote_copy(
            src_ref=srcs[t], dst_ref=dsts[t].at[kk], send_sem=send.at[t * 3 + j], recv_sem=recv.at[t * 3 + j],
            device_id=(*chips[j], c), device_id_type=MESH)

    local = [pltpu.make_async_copy(srcs[t], dsts[t].at[k], loc.at[t]) for t in range(len(srcs))]
    sends = [[copy(t, j, k) for j in range(3)] for t in range(len(srcs))]

    def start():
        for t, cp in enumerate(local):
            cp.start()
            for sd in sends[t]:
                sd.start()

    def wait(t):
        for j, (qx, qy) in enumerate(chips):
            copy(t, j, 2 * qx + qy).wait_recv()
        for sd in sends[t]:
            sd.wait_send()
        local[t].wait()

    return start, wait


def _cast_shards(*shards):
    def body(*refs):
        for src, dst in zip(refs[:len(shards)], refs[len(shards):]):
            dst[...] = src[...].astype(BF16)

    return pl.pallas_call(
        body, name="cast_shards", out_shape=[jax.ShapeDtypeStruct(s.shape, BF16) for s in shards],
        in_specs=[VMEM] * len(shards), out_specs=[VMEM] * len(shards),
        compiler_params=pltpu.CompilerParams(vmem_limit_bytes=VMEM_LIMIT),
    )(*shards)


def _load_weights(pairs, sem, first=0):
    cps = [pltpu.make_async_copy(src, dst, sem.at[first + i]) for i, (src, dst) in enumerate(pairs)]
    for cp in cps:
        cp.start()
    for cp in cps:
        cp.wait()


def _shifted_views(buf, shifted, t_rows):
    n = t_rows + A_HALO - 8
    for b in range(1, 8):
        shifted[b - 1] = buf[b:b + n, :]

    def view(offset):
        a, b = divmod(offset, 8)
        if b == 0:
            return buf[8 * a:8 * a + t_rows, :]
        return shifted[b - 1, 8 * a:8 * a + t_rows, :]

    return view


def _pool_count(tile, t_rows, w):
    row = lax.broadcasted_iota(jnp.int32, (t_rows, POOL_GROUP), 0) + tile * t_rows
    return jnp.minimum(row + 1, w).astype(F32)


def _mixer_fwd(x, g1, win_b, wout_b, wup_b, wa_s, wf_s, cb, lg, lb, pw, ps, tile_rows):
    seq = x.shape[0]
    tr = tile_rows
    n = seq // tr

    def body(x_ref, g1_ref, win_b_hbm, wout_b_hbm, wup_b_hbm, wa_s_hbm, wf_s_hbm, cb_ref, lg_ref, lb_ref, pw_ref,
             ps_ref, h1_ref, proj_ref, c_ref, d_ref, m_ref, x1_ref, win_f, wout_f, wup_f, wa_g, wf_g,
             win_v, wout_v, wa_ref, ubuf, ushift, bbuf, sem, *csems):
        i = pl.program_id(0)
        first_sems, first_stages, second_sems, second_stages, later_sems, later_stages, tap_sems = (
            csems[0:6], csems[6:7], csems[7:13], csems[13:14], csems[14:20], csems[20:21], csems[21:24])

        def first():
            return _gather_ops((win_b_hbm,), (win_f,), (True,), first_sems, first_stages)

        def second():
            return _gather_ops((wout_b_hbm,), (wout_f,), (False,), second_sems, second_stages)

        def later():
            return _gather_ops((wup_b_hbm,), (wup_f,), (True,), later_sems, later_stages)

        def taps():
            return _tap_ops((wa_s_hbm, wf_s_hbm), (wa_g, wf_g), tap_sems)

        @pl.when(i == 0)
        def _():
            _handshake(SIBLING_AND_CHIPS)
            first()[0]()
            taps()[0]()
            second()[0]()
            later()[0]()
            first()[1]()
            first()[2]()
            _load_weights([(win_f, win_v)], sem)
            ubuf[0:A_HALO, :] = jnp.zeros((A_HALO, D_CONV), F32)
            bbuf[0:P_HALO, :] = jnp.zeros((P_HALO, D_POOL), F32)

        xv = x_ref[...]
        r = lax.rsqrt(_rowmean(xv * xv) + EPS)
        h1 = (xv * r * g1_ref[...]).astype(BF16)
        h1_ref[...] = h1
        proj = _dot(h1, win_v[...])
        proj_ref[...] = proj.astype(BF16)

        @pl.when(i == 0)
        def _():
            taps()[1](0)
            _load_weights([(wa_g.at[kk], wa_ref.at[:, kk * (D_CONV // N_CHIPS):(kk + 1) * (D_CONV // N_CHIPS)])
                           for kk in range(N_CHIPS)], sem, 2)

        av, ag, bi = proj[:, :D_CONV], proj[:, D_CONV:2 * D_CONV], proj[:, 2 * D_CONV:]
        ubuf[A_HALO:A_HALO + tr, :] = av * _sigmoid(ag)
        off = A_HALO - (CONV_A - 1)
        uview = _shifted_views(ubuf, ushift, tr)
        acc = wa_ref[0:1, :] * uview(off)
        for j in range(1, CONV_A):
            acc = acc + wa_ref[j:j + 1, :] * uview(off + j)
        cv = acc + cb_ref[...]
        ubuf[0:A_HALO, :] = ubuf[tr:tr + A_HALO, :]
        c_ref[...] = cv.astype(BF16)
        xc = cv - _rowmean(cv)
        z = xc * lax.rsqrt(_rowmean(xc * xc) + EPS)
        ln = z * lg_ref[...] + lb_ref[...]
        ya = ln * _sigmoid(ln)
        bbuf[P_HALO:P_HALO + tr, :] = bi
        ds, ybs = [], []
        for g, w in enumerate(POOL_WINDOWS):
            cols = slice(g * POOL_GROUP, (g + 1) * POOL_GROUP)
            s = bi[:, cols]
            for kk in range(1, w):
                s = s + bbuf[P_HALO - kk:P_HALO - kk + tr, cols]
            dg = s / _pool_count(i, tr, w) - bi[:, cols]
            ds.append(dg)
            ybs.append(_dot(dg.astype(BF16), pw_ref[g].astype(BF16)))
        bbuf[0:P_HALO, :] = bbuf[tr:tr + P_HALO, :]
        d_ref[...] = jnp.concatenate(ds, axis=1).astype(BF16)
        yb = jnp.concatenate(ybs, axis=1) * ps_ref[...]
        m = jnp.concatenate([ya, yb], axis=1).astype(BF16)
        m_ref[...] = m

        @pl.when(i == 0)
        def _():
            second()[1]()
            second()[2]()
            _load_weights([(wout_f, wout_v)], sem, 1)

        x1_ref[...] = xv + _dot(m, wout_v[...])

        @pl.when(i == n - 1)
        def _():
            later()[1]()
            later()[2]()
            taps()[1](1)

    tile = lambda w: pl.BlockSpec((tr, w), lambda i: (i, 0))
    full = lambda a: pl.BlockSpec(a.shape, lambda i: (0,) * a.ndim)
    return pl.pallas_call(
        body, name="mixer_fwd", grid=(n,),
        in_specs=[tile(D_MODEL), full(g1)] + [ANY] * 5 + [full(cb), full(lg), full(lb), full(pw), full(ps)],
        out_specs=[tile(D_MODEL), tile(D_IN), tile(D_CONV), tile(D_POOL), tile(D_MODEL), tile(D_MODEL)] + [ANY] * 5,
        out_shape=[
            jax.ShapeDtypeStruct((seq, D_MODEL), BF16), jax.ShapeDtypeStruct((seq, D_IN), BF16),
            jax.ShapeDtypeStruct((seq, D_CONV), BF16), jax.ShapeDtypeStruct((seq, D_POOL), BF16),
            jax.ShapeDtypeStruct((seq, D_MODEL), BF16), jax.ShapeDtypeStruct((seq, D_MODEL), F32),
            jax.ShapeDtypeStruct((D_MODEL, D_IN), BF16), jax.ShapeDtypeStruct((D_MODEL, D_MODEL), BF16),
            jax.ShapeDtypeStruct((D_MODEL, 2 * D_FF), BF16),
            jax.ShapeDtypeStruct((N_CHIPS,) + wa_s.shape, F32), jax.ShapeDtypeStruct((N_CHIPS,) + wf_s.shape, F32),
        ],
        scratch_shapes=[
            pltpu.VMEM((D_MODEL, D_IN), BF16), pltpu.VMEM((D_MODEL, D_MODEL), BF16), pltpu.VMEM((32, D_CONV), F32),
            pltpu.VMEM((tr + A_HALO, D_CONV), F32), pltpu.VMEM((7, tr + A_HALO - 8, D_CONV), F32),
            pltpu.VMEM((tr + P_HALO, D_POOL), F32), pltpu.SemaphoreType.DMA((2 + N_CHIPS,)),
        ] + _gather_scratch((win_b,)) + _gather_scratch((wout_b,)) + _gather_scratch((wup_b,)) + [
            pltpu.SemaphoreType.DMA((6,)), pltpu.SemaphoreType.DMA((6,)), pltpu.SemaphoreType.DMA((2,))],
        compiler_params=pltpu.CompilerParams(dimension_semantics=("arbitrary",), vmem_limit_bytes=VMEM_LIMIT,
                                             collective_id=SIBLING_AND_CHIPS),
    )(x, g1, win_b, wout_b, wup_b, wa_s, wf_s, cb, lg, lb, pw, ps)


def _ffn_up(x1, g2, wup, wf, fb, wdown_b, tile_rows):
    seq = x1.shape[0]
    tr = tile_rows
    n = seq // tr

    def body(x1_ref, g2_ref, wup_hbm, wf_ref, fb_ref, wdown_b_hbm,
             h2_ref, up_ref, gc_ref, act_ref, wdown_f, wup_v, gbuf, sem, *gsems):
        i = pl.program_id(0)

        def gather():
            return _gather_ops((wdown_b_hbm,), (wdown_f,), (False,), gsems[:6], gsems[6:])

        @pl.when(i == 0)
        def _():
            _handshake(SIBLING_AND_CHIPS)
            gather()[0]()
            _load_weights(((wup_hbm, wup_v),), sem)
            gbuf[0:8, :] = jnp.zeros((8, D_FF), F32)

        x1v = x1_ref[...]
        r2 = lax.rsqrt(_rowmean(x1v * x1v) + EPS)
        h2 = (x1v * r2 * g2_ref[...]).astype(BF16)
        h2_ref[...] = h2

        def up_proj(j):
            return (_dot(h2, wup_v[:, j * FF_CHUNK:(j + 1) * FF_CHUNK]),
                    _dot(h2, wup_v[:, D_FF + j * FF_CHUNK:D_FF + (j + 1) * FF_CHUNK]))

        ahead = up_proj(0)
        for j in range(N_FF_CHUNKS):
            cs = slice(j * FF_CHUNK, (j + 1) * FF_CHUNK)
            vs = slice(D_FF + j * FF_CHUNK, D_FF + (j + 1) * FF_CHUNK)
            gate, val = ahead
            if j + 1 < N_FF_CHUNKS:
                ahead = up_proj(j + 1)
            up_ref[:, cs] = gate.astype(BF16)
            up_ref[:, vs] = val.astype(BF16)
            gbuf[8:8 + tr, cs] = gate
            gc = (wf_ref[0:1, cs] * gbuf[6:6 + tr, cs] + wf_ref[1:2, cs] * gbuf[7:7 + tr, cs]
                  + wf_ref[2:3, cs] * gate + fb_ref[:, cs])
            gbuf[0:8, cs] = gbuf[tr:tr + 8, cs]
            gc_ref[:, cs] = gc.astype(BF16)
            act_ref[:, cs] = (gc * _sigmoid(gc) * val).astype(BF16)

        @pl.when(i == max(n - 2, 0))
        def _():
            gather()[1]()

        @pl.when(i == n - 1)
        def _():
            gather()[2]()

    tile = lambda w: pl.BlockSpec((tr, w), lambda i: (i, 0))
    full = lambda a: pl.BlockSpec(a.shape, lambda i: (0,) * a.ndim)
    return pl.pallas_call(
        body, name="ffn_up", grid=(n,),
        in_specs=[tile(D_MODEL), full(g2), ANY, full(wf), full(fb), ANY],
        out_specs=[tile(D_MODEL), tile(2 * D_FF), tile(D_FF), tile(D_FF), ANY],
        out_shape=[
            jax.ShapeDtypeStruct((seq, D_MODEL), BF16), jax.ShapeDtypeStruct((seq, 2 * D_FF), BF16),
            jax.ShapeDtypeStruct((seq, D_FF), BF16), jax.ShapeDtypeStruct((seq, D_FF), BF16),
            jax.ShapeDtypeStruct((D_FF, D_MODEL), BF16),
        ],
        scratch_shapes=[pltpu.VMEM(wup.shape, BF16), pltpu.VMEM((tr + 8, D_FF), F32), pltpu.SemaphoreType.DMA((1,))]
        + _gather_scratch((wdown_b,)),
        compiler_params=pltpu.CompilerParams(dimension_semantics=("arbitrary",), vmem_limit_bytes=VMEM_LIMIT,
                                             collective_id=SIBLING_AND_CHIPS),
    )(x1, g2, wup, wf, fb, wdown_b)


def _ffn_down(x1, act, wdown, g3, target, tile_rows):
    seq = x1.shape[0]
    tr = tile_rows
    n = seq // tr

    def body(x1_ref, act_ref, wdown_hbm, g3_ref, t_ref, dx2b_ref, sm_ref, wdown_v, sem):
        i = pl.program_id(0)

        @pl.when(i == 0)
        def _():
            _load_weights(((wdown_hbm, wdown_v),), sem)
            sm_ref[...] = jnp.zeros(sm_ref.shape, F32)

        x2 = x1_ref[...] + _dot(act_ref[...], wdown_v[...])
        r3 = lax.rsqrt(_rowmean(x2 * x2) + EPS)
        n3 = x2 * r3
        err = n3 * g3_ref[...] - t_ref[...]
        dy = err / D_MODEL
        sm_ref[2:3, :] += _colsum(dy * n3)
        loss = 0.5 * _colsum(_rowmean(err * err))
        sm_ref[3:4, :] += jnp.broadcast_to(loss, (1, D_MODEL))
        dn = dy * g3_ref[...]
        dx2b_ref[...] = (r3 * (dn - n3 * _rowmean(dn * n3))).astype(BF16)

    tile = lambda w: pl.BlockSpec((tr, w), lambda i: (i, 0))
    full = lambda a: pl.BlockSpec(a.shape, lambda i: (0,) * a.ndim)
    return pl.pallas_call(
        body, name="ffn_down", grid=(n,),
        in_specs=[tile(D_MODEL), tile(D_FF), ANY, full(g3), tile(D_MODEL)],
        out_specs=[tile(D_MODEL), pl.BlockSpec((8, D_MODEL), lambda i: (0, 0))],
        out_shape=[jax.ShapeDtypeStruct((seq, D_MODEL), BF16), jax.ShapeDtypeStruct((8, D_MODEL), F32)],
        scratch_shapes=[pltpu.VMEM(wdown.shape, BF16), pltpu.SemaphoreType.DMA((1,))],
        compiler_params=pltpu.CompilerParams(dimension_semantics=("arbitrary",), vmem_limit_bytes=VMEM_LIMIT),
    )(x1, act, wdown, g3, target)


def _ffn_bwd(dx2, up, gcs, x1, g2, wup, wf, wdown, comm, tile_rows):
    seq = x1.shape[0]
    c_ins, c_shapes, c_sems, c_ops, c_id = _comm_plan(comm)
    nc = len(c_ins)
    tr = tile_rows
    n = seq // tr

    def body(dx2_ref, up_ref, gc_ref, x1_ref, g2_ref, wup_hbm, wf_ref, wdown_hbm, *rest):
        c_in, rest = rest[:nc], rest[nc:]
        dup_ref, dx1_ref, dx1b_ref, sm_ref, sf_ref = rest[:5]
        c_out, rest = rest[5:5 + nc], rest[5 + nc:]
        wup_v, wdown_v, dbuf, dcar, sem = rest[:5]
        c_sem_refs = rest[5:]
        i = pl.program_id(0)

        @pl.when(i == 0)
        def _():
            c_ops(c_in, c_out, c_sem_refs)[0]()
            pltpu.make_async_copy(wup_hbm, wup_v, sem.at[0]).start()
            _load_weights(((wdown_hbm, wdown_v),), sem, 1)
            dcar[...] = jnp.zeros(dcar.shape, F32)
            sm_ref[...] = jnp.zeros(sm_ref.shape, F32)
            sf_ref[...] = jnp.zeros(sf_ref.shape, F32)

        dx2b = dx2_ref[...]
        dx2v = dx2b.astype(F32)
        dh2 = jnp.zeros((tr, D_MODEL), F32)

        def down_t(j):
            return _dot_nt(dx2b, wdown_v[j * FF_CHUNK:(j + 1) * FF_CHUNK, :])

        ahead = down_t(0)
        for j in range(N_FF_CHUNKS):
            cs = slice(j * FF_CHUNK, (j + 1) * FF_CHUNK)
            vs = slice(D_FF + j * FF_CHUNK, D_FF + (j + 1) * FF_CHUNK)
            dact = ahead
            if j + 1 < N_FF_CHUNKS:
                ahead = down_t(j + 1)
            gate = up_ref[:, cs].astype(F32)
            val = up_ref[:, vs].astype(F32)
            gc = gc_ref[:, cs].astype(F32)
            sg = _sigmoid(gc)
            dval = dact * (gc * sg)
            dgc = dact * val * (sg * (1.0 + gc * (1.0 - sg)))
            dbuf[0:tr, :] = dgc
            dbuf[tr:tr + 8, :] = dcar[:, cs]
            d_p1 = dbuf[1:1 + tr, :]
            d_p2 = dbuf[2:2 + tr, :]
            dgate = wf_ref[2:3, cs] * dgc + wf_ref[1:2, cs] * d_p1 + wf_ref[0:1, cs] * d_p2
            dcar[:, cs] = dgc[0:8, :]
            sf_ref[0:1, cs] += _colsum(d_p2 * gate)
            sf_ref[1:2, cs] += _colsum(d_p1 * gate)
            sf_ref[2:3, cs] += _colsum(dgc * gate)
            sf_ref[3:4, cs] += _colsum(dgc)
            dgb, dvb = dgate.astype(BF16), dval.astype(BF16)
            dup_ref[:, cs] = dgb
            dup_ref[:, vs] = dvb
            if j == 0:
                @pl.when(i == 0)
                def _():
                    pltpu.make_async_copy(wup_hbm, wup_v, sem.at[0]).wait()

            dh2 = dh2 + _dot_nt(dgb, wup_v[:, cs]) + _dot_nt(dvb, wup_v[:, vs])
        x1v = x1_ref[...]
        r2 = lax.rsqrt(_rowmean(x1v * x1v) + EPS)
        n2 = x1v * r2
        sm_ref[1:2, :] += _colsum(dh2 * n2)
        dn2 = dh2 * g2_ref[...]
        dx1v = dx2v + r2 * (dn2 - n2 * _rowmean(dn2 * n2))
        dx1_ref[...] = dx1v
        dx1b_ref[...] = dx1v.astype(BF16)

        @pl.when(i == n - 1)
        def _():
            c_ops(c_in, c_out, c_sem_refs)[2]()

    tile = lambda w: pl.BlockSpec((tr, w), lambda i: (n - 1 - i, 0))
    full = lambda a: pl.BlockSpec(a.shape, lambda i: (0,) * a.ndim)
    acc = lambda rows, w: pl.BlockSpec((rows, w), lambda i: (0, 0))
    return pl.pallas_call(
        body, name="ffn_bwd", grid=(n,),
        in_specs=[tile(D_MODEL), tile(2 * D_FF), tile(D_FF), tile(D_MODEL), full(g2), ANY, full(wf), ANY] + [ANY] * nc,
        out_specs=[tile(2 * D_FF), tile(D_MODEL), tile(D_MODEL), acc(8, D_MODEL), acc(8, D_FF)] + [ANY] * nc,
        out_shape=[
            jax.ShapeDtypeStruct((seq, 2 * D_FF), BF16), jax.ShapeDtypeStruct((seq, D_MODEL), F32),
            jax.ShapeDtypeStruct((seq, D_MODEL), BF16), jax.ShapeDtypeStruct((8, D_MODEL), F32),
            jax.ShapeDtypeStruct((8, D_FF), F32),
        ] + c_shapes,
        scratch_shapes=[
            pltpu.VMEM(wup.shape, BF16), pltpu.VMEM(wdown.shape, BF16),
            pltpu.VMEM((tr + 8, FF_CHUNK), F32), pltpu.VMEM((8, D_FF), F32), pltpu.SemaphoreType.DMA((2,)),
        ] + c_sems,
        compiler_params=pltpu.CompilerParams(dimension_semantics=("arbitrary",), vmem_limit_bytes=VMEM_LIMIT,
                                             collective_id=c_id),
    )(dx2, up, gcs, x1, g2, wup, wf, wdown, *c_ins)


def _mixer_bwd(dx1, x, proj, cpre, d, g1, win, wa, lg, lb, pw, ps, wout, parts, tile_rows):
    seq = x.shape[0]
    n_parts = len(parts)
    tr = tile_rows
    n = seq // tr
    row_cb, row_lg, row_lb, row_ps = 32, 33, 34, 35

    def body(dx1_ref, x_ref, proj_ref, projh_ref, c_ref, d_ref, g1_ref, win_hbm, wa_ref, lg_ref, lb_ref, pw_ref, ps_ref,
             wout_hbm, *rest):
        part_refs, rest = rest[:n_parts], rest[n_parts:]
        dproj_ref, gx_ref, sm_ref, s5_ref, sp_ref = rest[:5]
        land_refs, rest = rest[5:5 + n_parts], rest[5 + n_parts:]
        win_v, wout_v, ubuf, ushift, dcbuf, dshift, ebuf, sem = rest[:8]
        ssems = rest[8:]
        i = pl.program_id(0)
        tile = n - 1 - i

        def scatter():
            return _scatter_ops(part_refs, land_refs, n_parts, ssems[:6], ssems[6:])

        @pl.when(i == 0)
        def _():
            _handshake(SIBLING_AND_CHIPS)
            scatter()[0]()
            _load_weights(((win_hbm, win_v), (wout_hbm, wout_v)), sem)
            dcbuf[tr:tr + A_HALO, :] = jnp.zeros((A_HALO, D_CONV), F32)
            ebuf[tr:tr + P_HALO, :] = jnp.zeros((P_HALO, D_POOL), F32)
            sm_ref[...] = jnp.zeros(sm_ref.shape, F32)
            s5_ref[...] = jnp.zeros(s5_ref.shape, F32)
            sp_ref[...] = jnp.zeros(sp_ref.shape, F32)

        dx1v = dx1_ref[...]
        dm = _dot_nt(dx1v.astype(BF16), wout_v[...])
        dya, dyb = dm[:, :D_CONV], dm[:, D_CONV:]
        dbis = []
        for g, w in enumerate(POOL_WINDOWS):
            cols = slice(g * POOL_GROUP, (g + 1) * POOL_GROUP)
            dgb = d_ref[:, cols]
            pwb = pw_ref[g].astype(BF16)
            dyg = dyb[:, cols]
            s5_ref[row_ps:row_ps + 1, cols] += _colsum(dyg * _dot(dgb, pwb))
            dqb = (dyg * ps_ref[:, cols]).astype(BF16)
            sp_ref[g] += _dot_tn(dgb, dqb)
            dd = _dot_nt(dqb, pwb)
            e = dd / _pool_count(tile, tr, w)
            ebuf[0:tr, cols] = e
            s = e
            for kk in range(1, w):
                s = s + ebuf[kk:kk + tr, cols]
            dbis.append(s - dd)
        ebuf[tr:tr + P_HALO, :] = ebuf[0:P_HALO, :]
        cv = c_ref[...].astype(F32)
        xc = cv - _rowmean(cv)
        rs = lax.rsqrt(_rowmean(xc * xc) + EPS)
        z = xc * rs
        ln = z * lg_ref[...] + lb_ref[...]
        sl = _sigmoid(ln)
        dl = dya * (sl * (1.0 + ln * (1.0 - sl)))
        s5_ref[row_lg:row_lg + 1, :] += _colsum(dl * z)
        s5_ref[row_lb:row_lb + 1, :] += _colsum(dl)
        dz = dl * lg_ref[...]
        dc = rs * (dz - _rowmean(dz) - z * _rowmean(dz * z))
        s5_ref[row_cb:row_cb + 1, :] += _colsum(dc)
        dcbuf[0:tr, :] = dc
        keep = (tile > 0).astype(F32)
        avh = projh_ref[:, :D_CONV].astype(F32)
        agh = projh_ref[:, D_CONV:].astype(F32)
        ubuf[0:A_HALO, :] = avh * _sigmoid(agh) * keep
        av = proj_ref[:, :D_CONV].astype(F32)
        ag = proj_ref[:, D_CONV:2 * D_CONV].astype(F32)
        sg = _sigmoid(ag)
        ubuf[A_HALO:A_HALO + tr, :] = av * sg
        off = A_HALO - (CONV_A - 1)
        du = wa_ref[CONV_A - 1:CONV_A, :] * dc
        dview = _shifted_views(dcbuf, dshift, tr)
        uview = _shifted_views(ubuf, ushift, tr)
        for j in range(CONV_A - 1):
            du = du + wa_ref[j:j + 1, :] * dview(CONV_A - 1 - j)
        for j in range(CONV_A):
            s5_ref[j:j + 1, :] += _colsum(dc * uview(off + j))
        dcbuf[tr:tr + A_HALO, :] = dcbuf[0:A_HALO, :]
        dav = du * sg
        dag = du * av * (sg * (1.0 - sg))
        dprojb = jnp.concatenate([dav, dag] + dbis, axis=1).astype(BF16)
        dproj_ref[...] = dprojb
        dh1 = _dot_nt(dprojb, win_v[...])
        xv = x_ref[...]
        r1 = lax.rsqrt(_rowmean(xv * xv) + EPS)
        n1 = xv * r1
        sm_ref[0:1, :] += _colsum(dh1 * n1)
        dn1 = dh1 * g1_ref[...]
        gx_ref[...] = dx1v + r1 * (dn1 - n1 * _rowmean(dn1 * n1))

        @pl.when(i == max(n - 2, 0))
        def _():
            scatter()[1]()

        @pl.when(i == n - 1)
        def _():
            scatter()[2]()

    tile = lambda w: pl.BlockSpec((tr, w), lambda i: (n - 1 - i, 0))
    full = lambda a: pl.BlockSpec(a.shape, lambda i: (0,) * a.ndim)
    halo = pl.BlockSpec((A_HALO, 2 * D_CONV), lambda i: (jnp.maximum((n - 1 - i) * (tr // A_HALO) - 1, 0), 0))
    acc = lambda shape: pl.BlockSpec(shape, lambda i: (0,) * len(shape))
    return pl.pallas_call(
        body, name="mixer_bwd", grid=(n,),
        in_specs=[tile(D_MODEL), tile(D_MODEL), tile(D_IN), halo, tile(D_CONV), tile(D_POOL), full(g1), ANY, full(wa),
                  full(lg), full(lb), full(pw), full(ps), ANY] + [ANY] * n_parts,
        out_specs=[tile(D_IN), tile(D_MODEL), acc((8, D_MODEL)), acc((40, D_CONV)), acc(pw.shape)] + [ANY] * n_parts,
        out_shape=[
            jax.ShapeDtypeStruct((seq, D_IN), BF16), jax.ShapeDtypeStruct((seq, D_MODEL), F32),
            jax.ShapeDtypeStruct((8, D_MODEL), F32), jax.ShapeDtypeStruct((40, D_CONV), F32),
            jax.ShapeDtypeStruct(pw.shape, F32),
        ] + _scatter_shapes(parts, ()),
        scratch_shapes=[
            pltpu.VMEM(win.shape, BF16), pltpu.VMEM(wout.shape, BF16),
            pltpu.VMEM((tr + A_HALO, D_CONV), F32), pltpu.VMEM((7, tr + A_HALO - 8, D_CONV), F32),
            pltpu.VMEM((tr + A_HALO, D_CONV), F32), pltpu.VMEM((7, tr + A_HALO - 8, D_CONV), F32),
            pltpu.VMEM((tr + P_HALO, D_POOL), F32), pltpu.SemaphoreType.DMA((2,)),
        ] + _scatter_scratch(parts, ()),
        compiler_params=pltpu.CompilerParams(dimension_semantics=("arbitrary",), vmem_limit_bytes=VMEM_LIMIT,
                                             collective_id=SIBLING_AND_CHIPS),
    )(dx1, x, proj, proj, cpre, d, g1, win, wa, lg, lb, pw, ps, wout, *parts)


def _weight_grad(a, b, layout, k_rows, comm=None, carry=None):
    seq, m_dim = a.shape
    n_dim = b.shape[1]
    steps = seq // k_rows

    def store(o_ref, acc, index, value):
        if steps == 1:
            o_ref[index] = value.astype(BF16)
            return
        s = pl.program_id(1)

        @pl.when(s == 0)
        def _():
            acc[index] = value

        @pl.when(jnp.logical_and(s > 0, s < steps - 1))
        def _():
            acc[index] += value

        @pl.when(s == steps - 1)
        def _():
            o_ref[index] = (acc[index] + value).astype(BF16)

    if layout in ("rows1", "rows2"):
        groups = int(layout[-1])
        per_tile = N_CHIPS // groups
        rows = m_dim // N_CHIPS // 2
        a_w = m_dim // groups

        def body(a_ref, b_ref, o_ref, acc):
            r = _dot_tn(a_ref[...], b_ref[...])
            for p in range(per_tile):
                for h in range(2):
                    store(o_ref, acc, (h, p), r[(2 * p + h) * rows:(2 * p + h + 1) * rows, :])

        in_specs = [pl.BlockSpec((k_rows, a_w), lambda g, s: (s, g)), pl.BlockSpec((k_rows, n_dim), lambda g, s: (s, 0))]
        out_spec = pl.BlockSpec((2, per_tile, rows, n_dim), lambda g, s: (0, g, 0, 0))
        out_dims, acc_dims = (2, N_CHIPS, rows, n_dim), (2, per_tile, rows, n_dim)
    elif layout == "cols_chip":
        groups = N_CHIPS
        rows, cols = m_dim // 2, n_dim // N_CHIPS

        def body(a_ref, b_ref, o_ref, acc):
            r = _dot_tn(a_ref[...], b_ref[...])
            for h in range(2):
                store(o_ref, acc, h, r[h * rows:(h + 1) * rows, :])

        in_specs = [pl.BlockSpec((k_rows, m_dim), lambda g, s: (s, 0)), pl.BlockSpec((k_rows, cols), lambda g, s: (s, g))]
        out_spec = pl.BlockSpec((2, None, rows, cols), lambda g, s: (0, g, 0, 0))
        out_dims, acc_dims = (2, N_CHIPS, rows, cols), (2, rows, cols)
    else:
        groups = 2
        rows, cols = m_dim // 2, n_dim // N_CHIPS

        def body(a_ref, b_ref, o_ref, acc):
            r = _dot_tn(a_ref[...], b_ref[...])
            for k in range(N_CHIPS):
                store(o_ref, acc, k, r[:, k * cols:(k + 1) * cols])

        in_specs = [pl.BlockSpec((k_rows, rows), lambda g, s: (s, g)), pl.BlockSpec((k_rows, n_dim), lambda g, s: (s, 0))]
        out_spec = pl.BlockSpec((None, N_CHIPS, rows, cols), lambda g, s: (g, 0, 0, 0))
        out_dims, acc_dims = (2, N_CHIPS, rows, cols), (N_CHIPS, rows, cols)

    c_ins, c_shapes, c_sems, c_ops, c_id = _comm_plan(comm)
    nc = len(c_ins)
    c_specs = [ANY] * nc
    if carry is not None:
        assert comm is None and carry.shape[0] % (groups * steps) == 0
        carry_spec = pl.BlockSpec((carry.shape[0] // (groups * steps), carry.shape[1]), lambda g, s: (g * steps + s, 0))
        c_ins, c_shapes, c_specs, nc = (carry,), [jax.ShapeDtypeStruct(carry.shape, carry.dtype)], [carry_spec], 1

    def hosted(a_ref, b_ref, *rest):
        c_in, o_ref, c_out, acc, sems = rest[:nc], rest[nc], rest[nc + 1:2 * nc + 1], rest[2 * nc + 1], rest[2 * nc + 2:]
        g, s = pl.program_id(0), pl.program_id(1)
        if carry is not None:
            c_out[0][...] = c_in[0][...]
            body(a_ref, b_ref, o_ref, acc)
            return
        if nc:
            @pl.when(jnp.logical_and(g == 0, s == 0))
            def _():
                c_ops(c_in, c_out, sems)[0]()

        body(a_ref, b_ref, o_ref, acc)
        if nc:
            step = g * steps + s

            @pl.when(step == max(groups * steps - 2, 0))
            def _():
                c_ops(c_in, c_out, sems)[1]()

            @pl.when(step == groups * steps - 1)
            def _():
                c_ops(c_in, c_out, sems)[2]()

    outs = pl.pallas_call(
        hosted, name=f"weight_grad_{layout}_{m_dim}x{n_dim}", grid=(groups, steps),
        in_specs=in_specs + c_specs, out_specs=[out_spec] + c_specs,
        out_shape=[jax.ShapeDtypeStruct(out_dims, BF16)] + c_shapes,
        scratch_shapes=[pltpu.VMEM(acc_dims, F32)] + c_sems,
        compiler_params=pltpu.CompilerParams(dimension_semantics=("arbitrary", "arbitrary"), vmem_limit_bytes=VMEM_LIMIT,
                                             collective_id=c_id),
    )(a, b, *c_ins)
    return outs if nc else outs[0]


def _exchange_ops(ins, outs, n_big, sems):
    send, recv = sems
    x, y, c, _, _ = _place()
    cps = [pltpu.make_async_remote_copy(
        src_ref=ins[t].at[1 - c] if t < n_big else ins[t], dst_ref=outs[t], send_sem=send.at[t], recv_sem=recv.at[t],
        device_id=(x, y, 1 - c), device_id_type=MESH) for t in range(len(ins))]

    def start():
        for cp in cps:
            cp.start()

    def finish():
        for cp in cps:
            cp.wait()

    return start, finish


def _exchange_shapes(bigs, smalls):
    return [jax.ShapeDtypeStruct((N_CHIPS,) + b.shape[2:], b.dtype) for b in bigs] + [
        jax.ShapeDtypeStruct(s.shape, s.dtype) for s in smalls]


def _comm_plan(comm):
    if comm is None:
        return (), [], [], None, None
    kind, arrays = comm
    n = len(arrays)

    def scatter(i, o, sm):
        start, land, finish = _scatter_ops(i, o, n, sm[:6], sm[6:])
        return lambda: (_handshake(SIBLING_AND_CHIPS), start()), land, finish

    def exchange(i, o, sm):
        start, finish = _exchange_ops(i, o, n, sm)
        return lambda: (_handshake(SIBLING_ONLY), start()), lambda: None, finish

    if kind == "scatter":
        return tuple(arrays), _scatter_shapes(arrays, ()), _scatter_scratch(arrays, ()), scatter, SIBLING_AND_CHIPS
    return tuple(arrays), _exchange_shapes(arrays, ()), [pltpu.SemaphoreType.DMA((n,))] * 2, exchange, SIBLING_ONLY


def _sibling_exchange(bigs, smalls, tag):
    nb, nt = len(bigs), len(bigs) + len(smalls)

    def body(*refs):
        start, finish = _exchange_ops(refs[:nt], refs[nt:2 * nt], nb, refs[2 * nt:])
        _handshake(SIBLING_ONLY)
        start()
        finish()

    return pl.pallas_call(
        body, name=f"sibling_exchange_{tag}", out_shape=_exchange_shapes(bigs, smalls),
        in_specs=[ANY] * nt, out_specs=[ANY] * nt,
        scratch_shapes=[pltpu.SemaphoreType.DMA((nt,)), pltpu.SemaphoreType.DMA((nt,))],
        compiler_params=pltpu.CompilerParams(collective_id=SIBLING_ONLY),
    )(*bigs, *smalls)


def _pair_sum(core, mine, theirs, tag, block_rows):
    _, _, rows, cols = mine.shape
    steps = rows // block_rows

    def body(core_ref, a_ref, b_ref, o_ref):
        o_ref[...] = (a_ref[...].astype(F32) + b_ref[...].astype(F32)).astype(BF16)

    grid_spec = pltpu.PrefetchScalarGridSpec(
        num_scalar_prefetch=1, grid=(N_CHIPS, steps),
        in_specs=[pl.BlockSpec((None, None, block_rows, cols), lambda k, r, core_ref: (core_ref[0], k, r, 0)),
                  pl.BlockSpec((None, block_rows, cols), lambda k, r, core_ref: (k, r, 0))],
        out_specs=pl.BlockSpec((None, block_rows, cols), lambda k, r, core_ref: (k, r, 0)),
    )
    return pl.pallas_call(
        body, name=f"pair_sum_{tag}", grid_spec=grid_spec,
        out_shape=jax.ShapeDtypeStruct((N_CHIPS, rows, cols), BF16),
        compiler_params=pltpu.CompilerParams(dimension_semantics=("arbitrary", "arbitrary"), vmem_limit_bytes=VMEM_LIMIT),
    )(core, mine, theirs)


def _pair_sum_small(mine, theirs):
    (m_f2, m_b1, m_b2, m_sf, m_s5, m_sp) = mine

    def body(a0, a1, a2, a3, a4, a5, b0, b1, b2, b3, b4, b5, o_m, o_f, o_5, o_p):
        sm = (a0[...] + a1[...] + a2[...]) + (b0[...] + b1[...] + b2[...])
        sf = a3[...] + b3[...]
        s5 = a4[...] + b4[...]
        for h in range(2):
            o_m[h] = sm[:, h * (D_MODEL // 2):(h + 1) * (D_MODEL // 2)]
            o_f[h] = sf[:, h * (D_FF // 2):(h + 1) * (D_FF // 2)]
            o_5[h] = s5[:, h * (D_CONV // 2):(h + 1) * (D_CONV // 2)]
            for g in range(2):
                o_p[h, g] = a5[2 * h + g] + b5[2 * h + g]

    out_shape = [
        jax.ShapeDtypeStruct((2, 8, D_MODEL // 2), F32), jax.ShapeDtypeStruct((2, 8, D_FF // 2), F32),
        jax.ShapeDtypeStruct((2, 40, D_CONV // 2), F32), jax.ShapeDtypeStruct((2, 2, POOL_GROUP, POOL_GROUP), F32),
    ]
    return pl.pallas_call(body, name="pair_sum_small", out_shape=out_shape, in_specs=[VMEM] * 12, out_specs=[VMEM] * 4)(
        *mine, *theirs)


def _scatter_ops(ins, outs, n_parts, sems, stages, landed=False):
    ici_send, ici_recv, fwd_send, fwd_recv, loc_in, loc_out = sems
    nt = len(ins)
    x, y, c, k, chips = _place()

    def src_of(t, kk):
        return ins[t].at[kk] if t < n_parts else ins[t].at[c]

    def ici(t, j, kk, slot):
        return pltpu.make_async_remote_copy(
            src_ref=src_of(t, kk), dst_ref=outs[t].at[c, slot], send_sem=ici_send.at[t * 3 + j],
            recv_sem=ici_recv.at[t * 3 + j], device_id=(*chips[j], c), device_id_type=MESH)

    def fwd(t, half):
        slots = outs[t].at[half]
        return pltpu.make_async_remote_copy(
            src_ref=slots, dst_ref=slots, send_sem=fwd_send.at[t], recv_sem=fwd_recv.at[t],
            device_id=(x, y, 1 - c), device_id_type=MESH)

    local = [_staged(src_of(t, k), outs[t].at[c, k], stages[t], loc_in.at[t], loc_out.at[t]) for t in range(nt)]
    peers = [(t, j, 2 * qx + qy) for t in range(nt) for j, (qx, qy) in enumerate(chips)]
    sends = [] if landed else [ici(t, j, kq, k) for t, j, kq in peers]

    def start():
        for cp in local:
            cp[0]()
        for cp in sends:
            cp.start()

    def land():
        for cp in local:
            cp[1]()
        if not landed:
            for t, j, kq in peers:
                ici(t, j, kq, kq).wait_recv()
        for cp in local:
            cp[2]()
        for t in range(nt):
            fwd(t, c).start()

    def finish():
        for t in range(nt):
            fwd(t, 1 - c).wait_recv()
            fwd(t, c).wait_send()
        for cp in sends:
            cp.wait_send()

    return start, land, finish


def _scatter_scratch(parts, smalls):
    arrays = tuple(parts) + tuple(smalls)
    nt = len(arrays)
    return ([pltpu.SemaphoreType.DMA((3 * nt,))] * 2 + [pltpu.SemaphoreType.DMA((nt,))] * 4
            + [pltpu.VMEM(a.shape[1:], a.dtype) for a in arrays])


def _scatter_shapes(parts, smalls):
    return [jax.ShapeDtypeStruct((2, N_CHIPS) + p.shape[1:], p.dtype) for p in tuple(parts) + tuple(smalls)]


HBM_SPEC = pl.BlockSpec(memory_space=pltpu.HBM)
SEM_SPEC = pl.BlockSpec(memory_space=pltpu.SEMAPHORE)
EFFECT = pltpu.SideEffectType.DATAFLOW_SIDE_EFFECTING


def _ici_copy(ins, lands, n_parts, send, recv, t, j):
    _, _, c, k, chips = _place()
    qx, qy = chips[j]
    src = ins[t].at[2 * qx + qy] if t < n_parts else ins[t].at[c]
    return pltpu.make_async_remote_copy(
        src_ref=src, dst_ref=lands[t].at[c, k], send_sem=send.at[t * 3 + j], recv_sem=recv.at[t * 3 + j],
        device_id=(qx, qy, c), device_id_type=MESH)


def _scatter_start(parts, smalls):
    arrays = tuple(parts) + tuple(smalls)
    nt = len(arrays)

    def body(*refs):
        ins, lands = refs[:nt], refs[nt:2 * nt]
        send, recv = refs[2 * nt], refs[2 * nt + 1]
        token = refs[-1]
        for t in range(nt):
            for j in range(3):
                _ici_copy(ins, lands, len(parts), send, recv, t, j).start()
        token[...] = jnp.zeros(token.shape, F32)

    land_shapes = _scatter_shapes(parts, smalls)
    out_shape = ([pltpu.SemaphoreType.DMA((3 * nt,))] * 2 + [pltpu.HBM(a.shape, a.dtype) for a in arrays]
                 + [pltpu.HBM(a.shape, a.dtype) for a in land_shapes] + [jax.ShapeDtypeStruct((8, 128), F32)])
    operands = [pltpu.with_memory_space_constraint(a, pltpu.HBM) for a in arrays]
    operands += [pltpu.with_memory_space_constraint(lax.empty(a.shape, a.dtype), pltpu.HBM) for a in land_shapes]
    outs = pl.pallas_call(
        body, name="scatter_start", out_shape=out_shape, in_specs=[HBM_SPEC] * (2 * nt),
        out_specs=[SEM_SPEC] * 2 + [HBM_SPEC] * (2 * nt) + [VMEM],
        input_output_aliases={i: 2 + i for i in range(2 * nt)},
        compiler_params=pltpu.CompilerParams(has_side_effects=EFFECT),
    )(*operands)
    return outs[0], outs[1], outs[2:2 + nt], outs[2 + nt:2 + 2 * nt], outs[-1]


def _scatter_wait(send, recv, ins, lands, n_parts, after):
    nt = len(ins)

    def body(*refs):
        in_refs, land_refs = refs[:nt], refs[nt:2 * nt]
        send_ref, recv_ref = refs[2 * nt], refs[2 * nt + 1]
        for t in range(nt):
            for j in range(3):
                cp = _ici_copy(in_refs, land_refs, n_parts, send_ref, recv_ref, t, j)
                cp.wait_send()
                cp.wait_recv()

    outs = pl.pallas_call(
        body, name="scatter_wait", out_shape=[pltpu.HBM(a.shape, a.dtype) for a in tuple(ins) + tuple(lands)],
        in_specs=[HBM_SPEC] * (2 * nt) + [SEM_SPEC] * 2 + [ANY] * len(after), out_specs=[HBM_SPEC] * (2 * nt),
        input_output_aliases={i: i for i in range(2 * nt)},
        compiler_params=pltpu.CompilerParams(has_side_effects=EFFECT),
    )(*ins, *lands, send, recv, *after)
    return outs[:nt], outs[nt:]


def _scatter_forward(ins, lands, n_parts):
    nt = len(ins)

    def body(*refs):
        start, land, finish = _scatter_ops(
            refs[:nt], refs[2 * nt:3 * nt], n_parts, refs[3 * nt:3 * nt + 6], refs[3 * nt + 6:], landed=True)
        _handshake(SIBLING_ONLY)
        start()
        land()
        finish()

    return pl.pallas_call(
        body, name="scatter_forward", out_shape=[jax.ShapeDtypeStruct(a.shape, a.dtype) for a in lands],
        in_specs=[ANY] * (2 * nt), out_specs=[ANY] * nt, input_output_aliases={nt + i: i for i in range(nt)},
        scratch_shapes=_scatter_scratch(ins[:n_parts], ins[n_parts:]),
        compiler_params=pltpu.CompilerParams(collective_id=SIBLING_ONLY),
    )(*ins, *lands)


def _chip_scatter(parts, smalls):
    nt = len(parts) + len(smalls)

    def body(*refs):
        start, land, finish = _scatter_ops(refs[:nt], refs[nt:2 * nt], len(parts), refs[2 * nt:2 * nt + 6], refs[2 * nt + 6:])
        _handshake(SIBLING_AND_CHIPS)
        start()
        land()
        finish()

    return pl.pallas_call(
        body, name="chip_scatter", out_shape=_scatter_shapes(parts, smalls), in_specs=[ANY] * nt, out_specs=[ANY] * nt,
        scratch_shapes=_scatter_scratch(parts, smalls),
        compiler_params=pltpu.CompilerParams(collective_id=SIBLING_AND_CHIPS),
    )(*parts, *smalls)


def _adamw(w, g, m, v):
    m = ADAM_B1 * m + (1.0 - ADAM_B1) * g
    v = ADAM_B2 * v + (1.0 - ADAM_B2) * (g * g)
    m_hat = m / (1.0 - ADAM_B1 ** ADAM_STEP)
    v_hat = v / (1.0 - ADAM_B2 ** ADAM_STEP)
    delta = -ADAM_LR * (m_hat / (jnp.sqrt(v_hat) + ADAM_EPS) + ADAM_WD * w)
    return delta, m, v


def _adam_big(parts, w, m, v, tag, block_rows, token):
    _, _, rows, cols = parts.shape
    steps = rows // block_rows

    def body(p_ref, w_ref, m_ref, v_ref, token_ref, g_out, d_out, m_out, v_out):
        g = p_ref[0].astype(F32)
        for q in range(1, N_CHIPS):
            g = g + p_ref[q].astype(F32)
        delta, m_new, v_new = _adamw(w_ref[...], g, m_ref[...], v_ref[...])
        g_out[...] = g
        d_out[...] = delta
        m_out[...] = m_new
        v_out[...] = v_new

    blk = pl.BlockSpec((block_rows, cols), lambda h, r: (h * steps + r, 0))
    return pl.pallas_call(
        body, name=f"adam_{tag}", grid=(2, steps),
        in_specs=[pl.BlockSpec((None, N_CHIPS, block_rows, cols), lambda h, r: (h, 0, r, 0)), blk, blk, blk, ANY],
        out_specs=[blk] * 4, out_shape=[jax.ShapeDtypeStruct(w.shape, F32)] * 4,
        compiler_params=pltpu.CompilerParams(dimension_semantics=("arbitrary", "arbitrary"), vmem_limit_bytes=VMEM_LIMIT),
    )(parts, w, m, v, token)


def _reduce_small(l_m, l_f, l_5, l_p):
    def total(ref):
        t = ref[:, 0]
        for q in range(1, N_CHIPS):
            t = t + ref[:, q]
        return t

    def body(m_ref, f_ref, s_ref, p_ref, g1_o, g2_o, g3_o, loss_o, wf_o, fb_o, wa_o, cb_o, lg_o, lb_o, ps_o, pw_o):
        tm, tf, t5, tp = total(m_ref), total(f_ref), total(s_ref), total(p_ref)
        sm = jnp.concatenate([tm[0], tm[1]], axis=1)
        sf = jnp.concatenate([tf[0], tf[1]], axis=1)
        s5 = jnp.concatenate([t5[0], t5[1]], axis=1)
        g1_o[...] = sm[0:1]
        g2_o[...] = sm[1:2]
        g3_o[...] = sm[2:3]
        loss_o[...] = sm[3:4, 0:128]
        wf_o[...] = sf
        fb_o[...] = sf[3:4]
        wa_o[...] = s5[0:32]
        cb_o[...] = s5[32:33]
        lg_o[...] = s5[33:34]
        lb_o[...] = s5[34:35]
        ps_o[...] = s5[35:36]
        for h in range(2):
            for g in range(2):
                pw_o[2 * h + g] = tp[h, g]

    row = lambda w: jax.ShapeDtypeStruct((1, w), F32)
    out_shape = [row(D_MODEL), row(D_MODEL), row(D_MODEL), row(128), jax.ShapeDtypeStruct((8, D_FF), F32), row(D_FF),
                 jax.ShapeDtypeStruct((32, D_CONV), F32), row(D_CONV), row(D_CONV), row(D_CONV), row(D_POOL),
                 jax.ShapeDtypeStruct((4, POOL_GROUP, POOL_GROUP), F32)]
    return pl.pallas_call(body, name="reduce_small", out_shape=out_shape, in_specs=[VMEM] * 4, out_specs=[VMEM] * 12)(
        l_m, l_f, l_5, l_p)


def _adam_small(ws, gs, ms, vs):
    count = len(ws)

    def body(*refs):
        w_r, g_r, m_r, v_r = (refs[t * count:(t + 1) * count] for t in range(4))
        d_o, m_o, v_o = (refs[(4 + t) * count:(5 + t) * count] for t in range(3))
        for t in range(count):
            delta, m_new, v_new = _adamw(w_r[t][...], g_r[t][...], m_r[t][...], v_r[t][...])
            d_o[t][...] = delta
            m_o[t][...] = m_new
            v_o[t][...] = v_new

    out_shape = [jax.ShapeDtypeStruct(w.shape, F32) for w in ws] * 3
    outs = pl.pallas_call(body, name="adam_small", out_shape=out_shape, in_specs=[VMEM] * (4 * count),
                          out_specs=[VMEM] * (3 * count))(*ws, *gs, *ms, *vs)
    return outs[:count], outs[count:2 * count], outs[2 * count:]


MIX_TILE = 512
FFN_TILE = 256
GRAD_K = 2048


def kernel(x, norm_mix_g, w_in, conv_a_w, conv_a_b, ln_a_g, ln_a_b, pool_w, pool_scale, w_out, norm_ffn_g, w_up, conv_f_w, conv_f_b, w_down, norm_final_g, loss_target, m_norm_mix_g, m_w_in, m_conv_a_w, m_conv_a_b, m_ln_a_g, m_ln_a_b, m_pool_w, m_pool_scale, m_w_out, m_norm_ffn_g, m_w_up, m_conv_f_w, m_conv_f_b, m_w_down, m_norm_final_g, v_norm_mix_g, v_w_in, v_conv_a_w, v_conv_a_b, v_ln_a_g, v_ln_a_b, v_pool_w, v_pool_scale, v_w_out, v_norm_ffn_g, v_w_up, v_conv_f_w, v_conv_f_b, v_w_down, v_norm_final_g):
    seq = x.shape[1]
    xs, ts = x[0], loss_target[0]
    mix_tile, ffn_tile, grad_k = min(MIX_TILE, seq), min(FFN_TILE, seq), min(GRAD_K, seq)
    chip = 2 * lax.axis_index("x") + lax.axis_index("y")
    core = lax.axis_index("c").astype(jnp.int32).reshape(1)

    wa_s = jnp.pad(conv_a_w[0], ((0, 32 - CONV_A), (0, 0)))
    wf_s = jnp.pad(conv_f_w[0], ((0, 8 - CONV_F), (0, 0)))
    win_b, wout_b, wup_b, wdown_b = _cast_shards(w_in[0], w_out[0], w_up[0], w_down[0])
    g3 = norm_final_g.reshape(1, D_MODEL)
    pw = pool_w[0]

    h1, proj, cpre, dpool, mcat, x1, win, wout, wup, wa_g, wf_g = _mixer_fwd(
        xs, norm_mix_g, win_b, wout_b, wup_b, wa_s, wf_s, conv_a_b, ln_a_g, ln_a_b, pw, pool_scale, mix_tile)
    wa = jnp.transpose(wa_g, (1, 0, 2)).reshape(32, D_CONV)
    wf = jnp.transpose(wf_g, (1, 0, 2)).reshape(8, D_FF)
    h2, up, gcs, act, wdown = _ffn_up(x1, norm_ffn_g, wup, wf, conv_f_b, wdown_b, ffn_tile)
    dx2b, sm_f2 = _ffn_down(x1, act, wdown, g3, ts, mix_tile)
    tags = ("w_in", "w_out", "w_up", "w_down")
    blocks = (256, 128, 256, 176)
    g_wdown = _weight_grad(act, dx2b, "rows2", grad_k)
    dup, dx1, dx1b, sm_b1, sf, l_wdown = _ffn_bwd(
        dx2b, up, gcs, x1, norm_ffn_g, wup, wf, wdown, ("exchange", [g_wdown]), ffn_tile)
    p_wdown = _pair_sum(core, g_wdown, l_wdown, tags[3], g_wdown.shape[2])
    g_wup, s_wdown = _weight_grad(h2, dup, "cols_chip", grad_k, ("scatter", [p_wdown]))
    g_wout, l_wup = _weight_grad(mcat, dx1b, "rows1", grad_k, ("exchange", [g_wup]))
    p_wup = _pair_sum(core, g_wup, l_wup, tags[2], g_wup.shape[2])
    l_wout, = _sibling_exchange((g_wout,), (), "early")
    p_wout = _pair_sum(core, g_wout, l_wout, tags[1], g_wout.shape[2])
    dproj, gx, sm_b2, s5, sp, s_wout, s_wup = _mixer_bwd(
        dx1, xs, proj, cpre, dpool, norm_mix_g, win, wa, ln_a_g, ln_a_b, pw, pool_scale, wout, [p_wout, p_wup], mix_tile)
    g_win, grad_x = _weight_grad(h1, dproj, "cols_half", grad_k, carry=gx)

    smalls = (sm_f2, sm_b1, sm_b2, sf, s5, sp)
    landed = _sibling_exchange((g_win,), smalls, "late")
    part_win = _pair_sum(core, g_win, landed[0], tags[0], g_win.shape[2])
    small_parts = _pair_sum_small(smalls, landed[1:])
    send, recv, late_src, late_land, token = _scatter_start([part_win], small_parts)
    big_w = (w_in[0], w_out[0], w_up[0], w_down[0])
    big_m = (m_w_in[0], m_w_out[0], m_w_up[0], m_w_down[0])
    big_v = (v_w_in[0], v_w_out[0], v_w_up[0], v_w_down[0])
    big = {}
    for t, p in ((1, s_wout), (2, s_wup), (3, s_wdown)):
        big[tags[t]] = _adam_big(p, big_w[t], big_m[t], big_v[t], tags[t], blocks[t], token)
    late_src, late_land = _scatter_wait(send, recv, late_src, late_land, 1, [big[tags[t]][3] for t in (1, 2, 3)])
    late = _scatter_forward(late_src, late_land, 1)
    big[tags[0]] = _adam_big(late[0], big_w[0], big_m[0], big_v[0], tags[0], blocks[0], token)
    big = {tag: [a[None] for a in outs] for tag, outs in big.items()}
    scattered = [None] * 4 + list(late[1:])

    (g_g1, g_g2, g_g3, loss_row, g_wf_all, g_fb, g_wa_all, g_cb, g_lg, g_lb, g_ps, g_pw) = _reduce_small(*scattered[4:])
    g_wa = lax.dynamic_slice(g_wa_all, (0, chip * (D_CONV // N_CHIPS)), (32, D_CONV // N_CHIPS))[:CONV_A]
    g_wf = lax.dynamic_slice(g_wf_all, (0, chip * (D_FF // N_CHIPS)), (8, D_FF // N_CHIPS))[:CONV_F]
    small_names = ("norm_mix_g", "conv_a_w", "conv_a_b", "ln_a_g", "ln_a_b", "pool_w", "pool_scale", "norm_ffn_g",
                   "conv_f_w", "conv_f_b", "norm_final_g")
    small_w = (norm_mix_g, conv_a_w[0], conv_a_b, ln_a_g, ln_a_b, pw, pool_scale, norm_ffn_g, conv_f_w[0], conv_f_b, g3)
    small_m = (m_norm_mix_g, m_conv_a_w[0], m_conv_a_b, m_ln_a_g, m_ln_a_b, m_pool_w[0], m_pool_scale, m_norm_ffn_g,
               m_conv_f_w[0], m_conv_f_b, m_norm_final_g.reshape(1, D_MODEL))
    small_v = (v_norm_mix_g, v_conv_a_w[0], v_conv_a_b, v_ln_a_g, v_ln_a_b, v_pool_w[0], v_pool_scale, v_norm_ffn_g,
               v_conv_f_w[0], v_conv_f_b, v_norm_final_g.reshape(1, D_MODEL))
    small_g = (g_g1, g_wa, g_cb, g_lg, g_lb, g_pw, g_ps, g_g2, g_wf, g_fb, g_g3)
    s_delta, s_m, s_v = _adam_small(small_w, small_g, small_m, small_v)
    shapes = {"conv_a_w": conv_a_w.shape, "pool_w": pool_w.shape, "conv_f_w": conv_f_w.shape, "norm_final_g": norm_final_g.shape}
    small = {}
    for t, name in enumerate(small_names):
        shp = shapes.get(name)
        small[name] = [a if shp is None else a.reshape(shp) for a in (small_g[t], s_delta[t], s_m[t], s_v[t])]

    order = ("norm_mix_g", "w_in", "conv_a_w", "conv_a_b", "ln_a_g", "ln_a_b", "pool_w", "pool_scale", "w_out", "norm_ffn_g",
             "w_up", "conv_f_w", "conv_f_b", "w_down", "norm_final_g")
    table = {**big, **small}
    loss = loss_row[0, 0]
    outs = [loss, grad_x[None]]
    for t in range(4):
        outs += [table[name][t] for name in order]
    return tuple(outs)
```

```python
import functools

import jax
import jax.numpy as jnp
from jax import lax
from jax.experimental import pallas as pl
from jax.experimental.pallas import tpu as pltpu

F32 = jnp.float32
BF16 = jnp.bfloat16
EPS = 1e-6
ADAM_LR = 0.001
ADAM_B1 = 0.9
ADAM_B2 = 0.999
ADAM_EPS = 1e-08
ADAM_WD = 0.01
ADAM_STEP = 10

D_MODEL = 1024
D_CONV = 512
D_POOL = 512
D_IN = 1536
D_FF = 2816
CONV_A = 31
CONV_F = 3
POOL_WINDOWS = (2, 4, 8, 16)
POOL_GROUP = 128
N_CHIPS = 4
FF_CHUNK = 256
N_FF_CHUNKS = D_FF // FF_CHUNK
A_HALO = 32
P_HALO = 16
VMEM_LIMIT = 56 * 1024 * 1024
MESH = pl.DeviceIdType.MESH

ANY = pl.BlockSpec(memory_space=pl.ANY)
VMEM = pl.BlockSpec(memory_space=pltpu.VMEM)


def _dot(a, b):
    return jnp.dot(a, b, preferred_element_type=F32)


def _dot_nt(a, b):
    return lax.dot_general(a, b, (((1,), (1,)), ((), ())), preferred_element_type=F32)


def _dot_tn(a, b):
    return lax.dot_general(a, b, (((0,), (0,)), ((), ())), preferred_element_type=F32)


def _sigmoid(v):
    return jax.nn.sigmoid(v)


def _colsum(v):
    return jnp.sum(v, axis=0, keepdims=True)


def _rowmean(v):
    return jnp.mean(v, axis=-1, keepdims=True)


def _place():
    x, y, c = lax.axis_index("x"), lax.axis_index("y"), lax.axis_index("c")
    chips = [(1 - x, y), (x, 1 - y), (1 - x, 1 - y)]
    return x, y, c, 2 * x + y, chips


SIBLING_ONLY, SIBLING_AND_CHIPS = 0, 1


def _handshake(collective):
    x, y, c, _, chips = _place()
    peers = [(x, y, 1 - c)] + ([(*chip, c) for chip in chips] if collective == SIBLING_AND_CHIPS else [])
    barrier = pltpu.get_barrier_semaphore()
    for peer in peers:
        pl.semaphore_signal(barrier, inc=1, device_id=peer, device_id_type=MESH)
    pl.semaphore_wait(barrier, len(peers))


def _staged(src, dst, stage, sem_in, sem_out):
    hop_in = pltpu.make_async_copy(src, stage, sem_in)
    hop_out = pltpu.make_async_copy(stage, dst, sem_out)

    def relay():
        hop_in.wait()
        hop_out.start()

    return hop_in.start, relay, hop_out.wait


def _gather_ops(bufs, fulls, col_sharded, sems, stages):
    ici_send, ici_recv, fwd_send, fwd_recv, loc_in, loc_out = sems
    n_big = len(bufs)
    x, y, c, k, chips = _place()

    def block(i, kk, half=None):
        rows, cols = bufs[i].shape
        if col_sharded[i]:
            rs = slice(None) if half is None else pl.ds(pl.multiple_of(half * (rows // 2), 16), rows // 2)
            return fulls[i].at[rs, pl.ds(pl.multiple_of(kk * cols, 128), cols)]
        if half is None:
            return fulls[i].at[pl.ds(pl.multiple_of(kk * rows, 16), rows), :]
        return fulls[i].at[pl.ds(pl.multiple_of(kk * rows + half * (rows // 2), 16), rows // 2), :]

    def my_half(i):
        rows = bufs[i].shape[0]
        return bufs[i].at[pl.ds(pl.multiple_of(c * (rows // 2), 16), rows // 2), :]

    def ici(i, j, kk):
        return pltpu.make_async_remote_copy(
            src_ref=my_half(i), dst_ref=block(i, kk, c), send_sem=ici_send.at[i * 3 + j], recv_sem=ici_recv.at[i * 3 + j],
            device_id=(*chips[j], c), device_id_type=MESH)

    def fwd(i, j, kk, half):
        return pltpu.make_async_remote_copy(
            src_ref=block(i, kk, half), dst_ref=block(i, kk, half), send_sem=fwd_send.at[i * 3 + j],
            recv_sem=fwd_recv.at[i * 3 + j], device_id=(x, y, 1 - c), device_id_type=MESH)

    local = [_staged(bufs[i], block(i, k), stages[i], loc_in.at[i], loc_out.at[i]) for i in range(n_big)]
    sends = [ici(i, j, k) for i in range(n_big) for j in range(3)]
    peers = [(i, j, 2 * qx + qy) for i in range(n_big) for j, (qx, qy) in enumerate(chips)]

    def start():
        for cp in local:
            cp[0]()
        for cp in sends:
            cp.start()

    def land():
        for cp in local:
            cp[1]()
        for i, j, kq in peers:
            ici(i, j, kq).wait_recv()
            fwd(i, j, kq, c).start()

    def finish():
        for i, j, kq in peers:
            fwd(i, j, kq, 1 - c).wait_recv()
            fwd(i, j, kq, c).wait_send()
        for cp in sends:
            cp.wait_send()
        for cp in local:
            cp[2]()

    return start, land, finish


def _gather_scratch(shards):
    n_big = len(shards)
    return ([pltpu.SemaphoreType.DMA((3 * n_big,))] * 4 + [pltpu.SemaphoreType.DMA((n_big,))] * 2
            + [pltpu.VMEM(b.shape, b.dtype) for b in shards])


def _tap_ops(srcs, dsts, sems):
    send, recv, loc = sems
    _, _, c, k, chips = _place()

    def copy(t, j, kk):
        return pltpu.make_async_remote_copy(
            src_ref=srcs[t], dst_ref=dsts[t].at[kk], send_sem=send.at[t * 3 + j], recv_sem=recv.at[t * 3 + j],
            device_id=(*chips[j], c), device_id_type=MESH)

    local = [pltpu.make_async_copy(srcs[t], dsts[t].at[k], loc.at[t]) for t in range(len(srcs))]
    sends = [[copy(t, j, k) for j in range(3)] for t in range(len(srcs))]

    def start():
        for t, cp in enumerate(local):
            cp.start()
            for sd in sends[t]:
                sd.start()

    def wait(t):
        for j, (qx, qy) in enumerate(chips):
            copy(t, j, 2 * qx + qy).wait_recv()
        for sd in sends[t]:
            sd.wait_send()
        local[t].wait()

    return start, wait


def _cast_shards(*shards):
    def body(*refs):
        for src, dst in zip(refs[:len(shards)], refs[len(shards):]):
            dst[...] = src[...].astype(BF16)

    return pl.pallas_call(
        body, name="cast_shards", out_shape=[jax.ShapeDtypeStruct(s.shape, BF16) for s in shards],
        in_specs=[VMEM] * len(shards), out_specs=[VMEM] * len(shards),
        compiler_params=pltpu.CompilerParams(vmem_limit_bytes=VMEM_LIMIT),
    )(*shards)


def _load_weights(pairs, sem, first=0):
    cps = [pltpu.make_async_copy(src, dst, sem.at[first + i]) for i, (src, dst) in enumerate(pairs)]
    for cp in cps:
        cp.start()
    for cp in cps:
        cp.wait()


def _shifted_views(buf, shifted, t_rows):
    n = t_rows + A_HALO - 8
    for b in range(1, 8):
        shifted[b - 1] = buf[b:b + n, :]

    def view(offset):
        a, b = divmod(offset, 8)
        if b == 0:
            return buf[8 * a:8 * a + t_rows, :]
        return shifted[b - 1, 8 * a:8 * a + t_rows, :]

    return view


def _pool_count(tile, t_rows, w):
    row = lax.broadcasted_iota(jnp.int32, (t_rows, POOL_GROUP), 0) + tile * t_rows
    return jnp.minimum(row + 1, w).astype(F32)


def _mixer_fwd(x, g1, win_b, wout_b, wup_b, wa_s, wf_s, cb, lg, lb, pw, ps, tile_rows):
    seq = x.shape[0]
    tr = tile_rows
    n = seq // tr

    def body(x_ref, g1_ref, win_b_hbm, wout_b_hbm, wup_b_hbm, wa_s_hbm, wf_s_hbm, cb_ref, lg_ref, lb_ref, pw_ref,
             ps_ref, h1_ref, proj_ref, c_ref, d_ref, m_ref, x1_ref, win_f, wout_f, wup_f, wa_g, wf_g,
             win_v, wout_v, wa_ref, ubuf, ushift, bbuf, sem, *csems):
        i = pl.program_id(0)
        first_sems, first_stages, second_sems, second_stages, later_sems, later_stages, tap_sems = (
            csems[0:6], csems[6:7], csems[7:13], csems[13:14], csems[14:20], csems[20:21], csems[21:24])

        def first():
            return _gather_ops((win_b_hbm,), (win_f,), (True,), first_sems, first_stages)

        def second():
            return _gather_ops((wout_b_hbm,), (wout_f,), (False,), second_sems, second_stages)

        def later():
            return _gather_ops((wup_b_hbm,), (wup_f,), (True,), later_sems, later_stages)

        def taps():
            return _tap_ops((wa_s_hbm, wf_s_hbm), (wa_g, wf_g), tap_sems)

        @pl.when(i == 0)
        def _():
            _handshake(SIBLING_AND_CHIPS)
            first()[0]()
            taps()[0]()
            second()[0]()
            later()[0]()
            first()[1]()
            first()[2]()
            _load_weights([(win_f, win_v)], sem)
            ubuf[0:A_HALO, :] = jnp.zeros((A_HALO, D_CONV), F32)
            bbuf[0:P_HALO, :] = jnp.zeros((P_HALO, D_POOL), F32)

        xv = x_ref[...]
        r = lax.rsqrt(_rowmean(xv * xv) + EPS)
        h1 = (xv * r * g1_ref[...]).astype(BF16)
        h1_ref[...] = h1
        proj = _dot(h1, win_v[...])
        proj_ref[...] = proj.astype(BF16)

        @pl.when(i == 0)
        def _():
            taps()[1](0)
            _load_weights([(wa_g.at[kk], wa_ref.at[:, kk * (D_CONV // N_CHIPS):(kk + 1) * (D_CONV // N_CHIPS)])
                           for kk in range(N_CHIPS)], sem, 2)

        av, ag, bi = proj[:, :D_CONV], proj[:, D_CONV:2 * D_CONV], proj[:, 2 * D_CONV:]
        ubuf[A_HALO:A_HALO + tr, :] = av * _sigmoid(ag)
        off = A_HALO - (CONV_A - 1)
        uview = _shifted_views(ubuf, ushift, tr)
        acc = wa_ref[0:1, :] * uview(off)
        for j in range(1, CONV_A):
            acc = acc + wa_ref[j:j + 1, :] * uview(off + j)
        cv = acc + cb_ref[...]
        ubuf[0:A_HALO, :] = ubuf[tr:tr + A_HALO, :]
        c_ref[...] = cv.astype(BF16)
        xc = cv - _rowmean(cv)
        z = xc * lax.rsqrt(_rowmean(xc * xc) + EPS)
        ln = z * lg_ref[...] + lb_ref[...]
        ya = ln * _sigmoid(ln)
        bbuf[P_HALO:P_HALO + tr, :] = bi
        ds, ybs = [], []
        for g, w in enumerate(POOL_WINDOWS):
            cols = slice(g * POOL_GROUP, (g + 1) * POOL_GROUP)
            s = bi[:, cols]
            for kk in range(1, w):
                s = s + bbuf[P_HALO - kk:P_HALO - kk + tr, cols]
            dg = s / _pool_count(i, tr, w) - bi[:, cols]
            ds.append(dg)
            ybs.append(_dot(dg.astype(BF16), pw_ref[g].astype(BF16)))
        bbuf[0:P_HALO, :] = bbuf[tr:tr + P_HALO, :]
        d_ref[...] = jnp.concatenate(ds, axis=1).astype(BF16)
        yb = jnp.concatenate(ybs, axis=1) * ps_ref[...]
        m = jnp.concatenate([ya, yb], axis=1).astype(BF16)
        m_ref[...] = m

        @pl.when(i == 0)
        def _():
            second()[1]()
            second()[2]()
            _load_weights([(wout_f, wout_v)], sem, 1)

        x1_ref[...] = xv + _dot(m, wout_v[...])

        @pl.when(i == n - 1)
        def _():
            later()[1]()
            later()[2]()
            taps()[1](1)

    tile = lambda w: pl.BlockSpec((tr, w), lambda i: (i, 0))
    full = lambda a: pl.BlockSpec(a.shape, lambda i: (0,) * a.ndim)
    return pl.pallas_call(
        body, name="mixer_fwd", grid=(n,),
        in_specs=[tile(D_MODEL), full(g1)] + [ANY] * 5 + [full(cb), full(lg), full(lb), full(pw), full(ps)],
        out_specs=[tile(D_MODEL), tile(D_IN), tile(D_CONV), tile(D_POOL), tile(D_MODEL), tile(D_MODEL)] + [ANY] * 5,
        out_shape=[
            jax.ShapeDtypeStruct((seq, D_MODEL), BF16), jax.ShapeDtypeStruct((seq, D_IN), BF16),
            jax.ShapeDtypeStruct((seq, D_CONV), BF16), jax.ShapeDtypeStruct((seq, D_POOL), BF16),
            jax.ShapeDtypeStruct((seq, D_MODEL), BF16), jax.ShapeDtypeStruct((seq, D_MODEL), F32),
            jax.ShapeDtypeStruct((D_MODEL, D_IN), BF16), jax.ShapeDtypeStruct((D_MODEL, D_MODEL), BF16),
            jax.ShapeDtypeStruct((D_MODEL, 2 * D_FF), BF16),
            jax.ShapeDtypeStruct((N_CHIPS,) + wa_s.shape, F32), jax.ShapeDtypeStruct((N_CHIPS,) + wf_s.shape, F32),
        ],
        scratch_shapes=[
            pltpu.VMEM((D_MODEL, D_IN), BF16), pltpu.VMEM((D_MODEL, D_MODEL), BF16), pltpu.VMEM((32, D_CONV), F32),
            pltpu.VMEM((tr + A_HALO, D_CONV), F32), pltpu.VMEM((7, tr + A_HALO - 8, D_CONV), F32),
            pltpu.VMEM((tr + P_HALO, D_POOL), F32), pltpu.SemaphoreType.DMA((2 + N_CHIPS,)),
        ] + _gather_scratch((win_b,)) + _gather_scratch((wout_b,)) + _gather_scratch((wup_b,)) + [
            pltpu.SemaphoreType.DMA((6,)), pltpu.SemaphoreType.DMA((6,)), pltpu.SemaphoreType.DMA((2,))],
        compiler_params=pltpu.CompilerParams(dimension_semantics=("arbitrary",), vmem_limit_bytes=VMEM_LIMIT,
                                             collective_id=SIBLING_AND_CHIPS),
    )(x, g1, win_b, wout_b, wup_b, wa_s, wf_s, cb, lg, lb, pw, ps)


def _ffn_up(x1, g2, wup, wf, fb, wdown_b, tile_rows):
    seq = x1.shape[0]
    tr = tile_rows
    n = seq // tr

    def body(x1_ref, g2_ref, wup_hbm, wf_ref, fb_ref, wdown_b_hbm,
             h2_ref, up_ref, gc_ref, act_ref, wdown_f, wup_v, gbuf, sem, *gsems):
        i = pl.program_id(0)

        def gather():
            return _gather_ops((wdown_b_hbm,), (wdown_f,), (False,), gsems[:6], gsems[6:])

        @pl.when(i == 0)
        def _():
            _handshake(SIBLING_AND_CHIPS)
            gather()[0]()
            _load_weights(((wup_hbm, wup_v),), sem)
            gbuf[0:8, :] = jnp.zeros((8, D_FF), F32)

        x1v = x1_ref[...]
        r2 = lax.rsqrt(_rowmean(x1v * x1v) + EPS)
        h2 = (x1v * r2 * g2_ref[...]).astype(BF16)
        h2_ref[...] = h2

        def up_proj(j):
            return (_dot(h2, wup_v[:, j * FF_CHUNK:(j + 1) * FF_CHUNK]),
                    _dot(h2, wup_v[:, D_FF + j * FF_CHUNK:D_FF + (j + 1) * FF_CHUNK]))

        ahead = up_proj(0)
        for j in range(N_FF_CHUNKS):
            cs = slice(j * FF_CHUNK, (j + 1) * FF_CHUNK)
            vs = slice(D_FF + j * FF_CHUNK, D_FF + (j + 1) * FF_CHUNK)
            gate, val = ahead
            if j + 1 < N_FF_CHUNKS:
                ahead = up_proj(j + 1)
            up_ref[:, cs] = gate.astype(BF16)
            up_ref[:, vs] = val.astype(BF16)
            gbuf[8:8 + tr, cs] = gate
            gc = (wf_ref[0:1, cs] * gbuf[6:6 + tr, cs] + wf_ref[1:2, cs] * gbuf[7:7 + tr, cs]
                  + wf_ref[2:3, cs] * gate + fb_ref[:, cs])
            gbuf[0:8, cs] = gbuf[tr:tr + 8, cs]
            gc_ref[:, cs] = gc.astype(BF16)
            act_ref[:, cs] = (gc * _sigmoid(gc) * val).astype(BF16)

        @pl.when(i == max(n - 2, 0))
        def _():
            gather()[1]()

        @pl.when(i == n - 1)
        def _():
            gather()[2]()

    tile = lambda w: pl.BlockSpec((tr, w), lambda i: (i, 0))
    full = lambda a: pl.BlockSpec(a.shape, lambda i: (0,) * a.ndim)
    return pl.pallas_call(
        body, name="ffn_up", grid=(n,),
        in_specs=[tile(D_MODEL), full(g2), ANY, full(wf), full(fb), ANY],
        out_specs=[tile(D_MODEL), tile(2 * D_FF), tile(D_FF), tile(D_FF), ANY],
        out_shape=[
            jax.ShapeDtypeStruct((seq, D_MODEL), BF16), jax.ShapeDtypeStruct((seq, 2 * D_FF), BF16),
            jax.ShapeDtypeStruct((seq, D_FF), BF16), jax.ShapeDtypeStruct((seq, D_FF), BF16),
            jax.ShapeDtypeStruct((D_FF, D_MODEL), BF16),
        ],
        scratch_shapes=[pltpu.VMEM(wup.shape, BF16), pltpu.VMEM((tr + 8, D_FF), F32), pltpu.SemaphoreType.DMA((1,))]
        + _gather_scratch((wdown_b,)),
        compiler_params=pltpu.CompilerParams(dimension_semantics=("arbitrary",), vmem_limit_bytes=VMEM_LIMIT,
                                             collective_id=SIBLING_AND_CHIPS),
    )(x1, g2, wup, wf, fb, wdown_b)


def _ffn_down(x1, act, wdown, g3, target, tile_rows):
    seq = x1.shape[0]
    tr = tile_rows
    n = seq // tr

    def body(x1_ref, act_ref, wdown_hbm, g3_ref, t_ref, dx2b_ref, sm_ref, wdown_v, sem):
        i = pl.program_id(0)

        @pl.when(i == 0)
        def _():
            _load_weights(((wdown_hbm, wdown_v),), sem)
            sm_ref[...] = jnp.zeros(sm_ref.shape, F32)

        x2 = x1_ref[...] + _dot(act_ref[...], wdown_v[...])
        r3 = lax.rsqrt(_rowmean(x2 * x2) + EPS)
        n3 = x2 * r3
        err = n3 * g3_ref[...] - t_ref[...]
        dy = err / D_MODEL
        sm_ref[2:3, :] += _colsum(dy * n3)
        loss = 0.5 * _colsum(_rowmean(err * err))
        sm_ref[3:4, :] += jnp.broadcast_to(loss, (1, D_MODEL))
        dn = dy * g3_ref[...]
        dx2b_ref[...] = (r3 * (dn - n3 * _rowmean(dn * n3))).astype(BF16)

    tile = lambda w: pl.BlockSpec((tr, w), lambda i: (i, 0))
    full = lambda a: pl.BlockSpec(a.shape, lambda i: (0,) * a.ndim)
    return pl.pallas_call(
        body, name="ffn_down", grid=(n,),
        in_specs=[tile(D_MODEL), tile(D_FF), ANY, full(g3), tile(D_MODEL)],
        out_specs=[tile(D_MODEL), pl.BlockSpec((8, D_MODEL), lambda i: (0, 0))],
        out_shape=[jax.ShapeDtypeStruct((seq, D_MODEL), BF16), jax.ShapeDtypeStruct((8, D_MODEL), F32)],
        scratch_shapes=[pltpu.VMEM(wdown.shape, BF16), pltpu.SemaphoreType.DMA((1,))],
        compiler_params=pltpu.CompilerParams(dimension_semantics=("arbitrary",), vmem_limit_bytes=VMEM_LIMIT),
    )(x1, act, wdown, g3, target)


def _ffn_bwd(dx2, up, gcs, x1, g2, wup, wf, wdown, comm, tile_rows):
    seq = x1.shape[0]
    c_ins, c_shapes, c_sems, c_ops, c_id = _comm_plan(comm)
    nc = len(c_ins)
    tr = tile_rows
    n = seq // tr

    def body(dx2_ref, up_ref, gc_ref, x1_ref, g2_ref, wup_hbm, wf_ref, wdown_hbm, *rest):
        c_in, rest = rest[:nc], rest[nc:]
        dup_ref, dx1_ref, dx1b_ref, sm_ref, sf_ref = rest[:5]
        c_out, rest = rest[5:5 + nc], rest[5 + nc:]
        wup_v, wdown_v, dbuf, dcar, sem = rest[:5]
        c_sem_refs = rest[5:]
        i = pl.program_id(0)

        @pl.when(i == 0)
        def _():
            c_ops(c_in, c_out, c_sem_refs)[0]()
            _load_weights(((wup_hbm, wup_v), (wdown_hbm, wdown_v)), sem)
            dcar[...] = jnp.zeros(dcar.shape, F32)
            sm_ref[...] = jnp.zeros(sm_ref.shape, F32)
            sf_ref[...] = jnp.zeros(sf_ref.shape, F32)

        dx2b = dx2_ref[...]
        dx2v = dx2b.astype(F32)
        dh2 = jnp.zeros((tr, D_MODEL), F32)

        def down_t(j):
            return _dot_nt(dx2b, wdown_v[j * FF_CHUNK:(j + 1) * FF_CHUNK, :])

        ahead = down_t(0)
        for j in range(N_FF_CHUNKS):
            cs = slice(j * FF_CHUNK, (j + 1) * FF_CHUNK)
            vs = slice(D_FF + j * FF_CHUNK, D_FF + (j + 1) * FF_CHUNK)
            dact = ahead
            if j + 1 < N_FF_CHUNKS:
                ahead = down_t(j + 1)
            gate = up_ref[:, cs].astype(F32)
            val = up_ref[:, vs].astype(F32)
            gc = gc_ref[:, cs].astype(F32)
            sg = _sigmoid(gc)
            dval = dact * (gc * sg)
            dgc = dact * val * (sg * (1.0 + gc * (1.0 - sg)))
            dbuf[0:tr, :] = dgc
            dbuf[tr:tr + 8, :] = dcar[:, cs]
            d_p1 = dbuf[1:1 + tr, :]
            d_p2 = dbuf[2:2 + tr, :]
            dgate = wf_ref[2:3, cs] * dgc + wf_ref[1:2, cs] * d_p1 + wf_ref[0:1, cs] * d_p2
            dcar[:, cs] = dgc[0:8, :]
            sf_ref[0:1, cs] += _colsum(d_p2 * gate)
            sf_ref[1:2, cs] += _colsum(d_p1 * gate)
            sf_ref[2:3, cs] += _colsum(dgc * gate)
            sf_ref[3:4, cs] += _colsum(dgc)
            dgb, dvb = dgate.astype(BF16), dval.astype(BF16)
            dup_ref[:, cs] = dgb
            dup_ref[:, vs] = dvb
            dh2 = dh2 + _dot_nt(dgb, wup_v[:, cs]) + _dot_nt(dvb, wup_v[:, vs])
        x1v = x1_ref[...]
        r2 = lax.rsqrt(_rowmean(x1v * x1v) + EPS)
        n2 = x1v * r2
        sm_ref[1:2, :] += _colsum(dh2 * n2)
        dn2 = dh2 * g2_ref[...]
        dx1v = dx2v + r2 * (dn2 - n2 * _rowmean(dn2 * n2))
        dx1_ref[...] = dx1v
        dx1b_ref[...] = dx1v.astype(BF16)

        @pl.when(i == n - 1)
        def _():
            c_ops(c_in, c_out, c_sem_refs)[2]()

    tile = lambda w: pl.BlockSpec((tr, w), lambda i: (n - 1 - i, 0))
    full = lambda a: pl.BlockSpec(a.shape, lambda i: (0,) * a.ndim)
    acc = lambda rows, w: pl.BlockSpec((rows, w), lambda i: (0, 0))
    return pl.pallas_call(
        body, name="ffn_bwd", grid=(n,),
        in_specs=[tile(D_MODEL), tile(2 * D_FF), tile(D_FF), tile(D_MODEL), full(g2), ANY, full(wf), ANY] + [ANY] * nc,
        out_specs=[tile(2 * D_FF), tile(D_MODEL), tile(D_MODEL), acc(8, D_MODEL), acc(8, D_FF)] + [ANY] * nc,
        out_shape=[
            jax.ShapeDtypeStruct((seq, 2 * D_FF), BF16), jax.ShapeDtypeStruct((seq, D_MODEL), F32),
            jax.ShapeDtypeStruct((seq, D_MODEL), BF16), jax.ShapeDtypeStruct((8, D_MODEL), F32),
            jax.ShapeDtypeStruct((8, D_FF), F32),
        ] + c_shapes,
        scratch_shapes=[
            pltpu.VMEM(wup.shape, BF16), pltpu.VMEM(wdown.shape, BF16),
            pltpu.VMEM((tr + 8, FF_CHUNK), F32), pltpu.VMEM((8, D_FF), F32), pltpu.SemaphoreType.DMA((2,)),
        ] + c_sems,
        compiler_params=pltpu.CompilerParams(dimension_semantics=("arbitrary",), vmem_limit_bytes=VMEM_LIMIT,
                                             collective_id=c_id),
    )(dx2, up, gcs, x1, g2, wup, wf, wdown, *c_ins)


def _mixer_bwd(dx1, x, proj, cpre, d, g1, win, wa, lg, lb, pw, ps, wout, parts, tile_rows):
    seq = x.shape[0]
    n_parts = len(parts)
    tr = tile_rows
    n = seq // tr
    row_cb, row_lg, row_lb, row_ps = 32, 33, 34, 35

    def body(dx1_ref, x_ref, proj_ref, projh_ref, c_ref, d_ref, g1_ref, win_hbm, wa_ref, lg_ref, lb_ref, pw_ref, ps_ref,
             wout_hbm, *rest):
        part_refs, rest = rest[:n_parts], rest[n_parts:]
        dproj_ref, gx_ref, sm_ref, s5_ref, sp_ref = rest[:5]
        land_refs, rest = rest[5:5 + n_parts], rest[5 + n_parts:]
        win_v, wout_v, ubuf, ushift, dcbuf, dshift, ebuf, sem = rest[:8]
        ssems = rest[8:]
        i = pl.program_id(0)
        tile = n - 1 - i

        def scatter():
            return _scatter_ops(part_refs, land_refs, n_parts, ssems[:6], ssems[6:])

        @pl.when(i == 0)
        def _():
            _handshake(SIBLING_AND_CHIPS)
            scatter()[0]()
            _load_weights(((win_hbm, win_v), (wout_hbm, wout_v)), sem)
            dcbuf[tr:tr + A_HALO, :] = jnp.zeros((A_HALO, D_CONV), F32)
            ebuf[tr:tr + P_HALO, :] = jnp.zeros((P_HALO, D_POOL), F32)
            sm_ref[...] = jnp.zeros(sm_ref.shape, F32)
            s5_ref[...] = jnp.zeros(s5_ref.shape, F32)
            sp_ref[...] = jnp.zeros(sp_ref.shape, F32)

        dx1v = dx1_ref[...]
        dm = _dot_nt(dx1v.astype(BF16), wout_v[...])
        dya, dyb = dm[:, :D_CONV], dm[:, D_CONV:]
        dbis = []
        for g, w in enumerate(POOL_WINDOWS):
            cols = slice(g * POOL_GROUP, (g + 1) * POOL_GROUP)
            dgb = d_ref[:, cols]
            pwb = pw_ref[g].astype(BF16)
            dyg = dyb[:, cols]
            s5_ref[row_ps:row_ps + 1, cols] += _colsum(dyg * _dot(dgb, pwb))
            dqb = (dyg * ps_ref[:, cols]).astype(BF16)
            sp_ref[g] += _dot_tn(dgb, dqb)
            dd = _dot_nt(dqb, pwb)
            e = dd / _pool_count(tile, tr, w)
            ebuf[0:tr, cols] = e
            s = e
            for kk in range(1, w):
                s = s + ebuf[kk:kk + tr, cols]
            dbis.append(s - dd)
        ebuf[tr:tr + P_HALO, :] = ebuf[0:P_HALO, :]
        cv = c_ref[...].astype(F32)
        xc = cv - _rowmean(cv)
        rs = lax.rsqrt(_rowmean(xc * xc) + EPS)
        z = xc * rs
        ln = z * lg_ref[...] + lb_ref[...]
        sl = _sigmoid(ln)
        dl = dya * (sl * (1.0 + ln * (1.0 - sl)))
        s5_ref[row_lg:row_lg + 1, :] += _colsum(dl * z)
        s5_ref[row_lb:row_lb + 1, :] += _colsum(dl)
        dz = dl * lg_ref[...]
        dc = rs * (dz - _rowmean(dz) - z * _rowmean(dz * z))
        s5_ref[row_cb:row_cb + 1, :] += _colsum(dc)
        dcbuf[0:tr, :] = dc
        keep = (tile > 0).astype(F32)
        avh = projh_ref[:, :D_CONV].astype(F32)
        agh = projh_ref[:, D_CONV:].astype(F32)
        ubuf[0:A_HALO, :] = avh * _sigmoid(agh) * keep
        av = proj_ref[:, :D_CONV].astype(F32)
        ag = proj_ref[:, D_CONV:2 * D_CONV].astype(F32)
        sg = _sigmoid(ag)
        ubuf[A_HALO:A_HALO + tr, :] = av * sg
        off = A_HALO - (CONV_A - 1)
        du = wa_ref[CONV_A - 1:CONV_A, :] * dc
        dview = _shifted_views(dcbuf, dshift, tr)
        uview = _shifted_views(ubuf, ushift, tr)
        for j in range(CONV_A - 1):
            du = du + wa_ref[j:j + 1, :] * dview(CONV_A - 1 - j)
        for j in range(CONV_A):
            s5_ref[j:j + 1, :] += _colsum(dc * uview(off + j))
        dcbuf[tr:tr + A_HALO, :] = dcbuf[0:A_HALO, :]
        dav = du * sg
        dag = du * av * (sg * (1.0 - sg))
        dprojb = jnp.concatenate([dav, dag] + dbis, axis=1).astype(BF16)
        dproj_ref[...] = dprojb
        dh1 = _dot_nt(dprojb, win_v[...])
        xv = x_ref[...]
        r1 = lax.rsqrt(_rowmean(xv * xv) + EPS)
        n1 = xv * r1
        sm_ref[0:1, :] += _colsum(dh1 * n1)
        dn1 = dh1 * g1_ref[...]
        gx_ref[...] = dx1v + r1 * (dn1 - n1 * _rowmean(dn1 * n1))

        @pl.when(i == max(n - 2, 0))
        def _():
            scatter()[1]()

        @pl.when(i == n - 1)
        def _():
            scatter()[2]()

    tile = lambda w: pl.BlockSpec((tr, w), lambda i: (n - 1 - i, 0))
    full = lambda a: pl.BlockSpec(a.shape, lambda i: (0,) * a.ndim)
    halo = pl.BlockSpec((A_HALO, 2 * D_CONV), lambda i: (jnp.maximum((n - 1 - i) * (tr // A_HALO) - 1, 0), 0))
    acc = lambda shape: pl.BlockSpec(shape, lambda i: (0,) * len(shape))
    return pl.pallas_call(
        body, name="mixer_bwd", grid=(n,),
        in_specs=[tile(D_MODEL), tile(D_MODEL), tile(D_IN), halo, tile(D_CONV), tile(D_POOL), full(g1), ANY, full(wa),
                  full(lg), full(lb), full(pw), full(ps), ANY] + [ANY] * n_parts,
        out_specs=[tile(D_IN), tile(D_MODEL), acc((8, D_MODEL)), acc((40, D_CONV)), acc(pw.shape)] + [ANY] * n_parts,
        out_shape=[
            jax.ShapeDtypeStruct((seq, D_IN), BF16), jax.ShapeDtypeStruct((seq, D_MODEL), F32),
            jax.ShapeDtypeStruct((8, D_MODEL), F32), jax.ShapeDtypeStruct((40, D_CONV), F32),
            jax.ShapeDtypeStruct(pw.shape, F32),
        ] + _scatter_shapes(parts, ()),
        scratch_shapes=[
            pltpu.VMEM(win.shape, BF16), pltpu.VMEM(wout.shape, BF16),
            pltpu.VMEM((tr + A_HALO, D_CONV), F32), pltpu.VMEM((7, tr + A_HALO - 8, D_CONV), F32),
            pltpu.VMEM((tr + A_HALO, D_CONV), F32), pltpu.VMEM((7, tr + A_HALO - 8, D_CONV), F32),
            pltpu.VMEM((tr + P_HALO, D_POOL), F32), pltpu.SemaphoreType.DMA((2,)),
        ] + _scatter_scratch(parts, ()),
        compiler_params=pltpu.CompilerParams(dimension_semantics=("arbitrary",), vmem_limit_bytes=VMEM_LIMIT,
                                             collective_id=SIBLING_AND_CHIPS),
    )(dx1, x, proj, proj, cpre, d, g1, win, wa, lg, lb, pw, ps, wout, *parts)


def _weight_grad(a, b, layout, k_rows, comm=None, carry=None):
    seq, m_dim = a.shape
    n_dim = b.shape[1]
    steps = seq // k_rows

    def store(o_ref, acc, index, value):
        if steps == 1:
            o_ref[index] = value.astype(BF16)
            return
        s = pl.program_id(1)

        @pl.when(s == 0)
        def _():
            acc[index] = value

        @pl.when(jnp.logical_and(s > 0, s < steps - 1))
        def _():
            acc[index] += value

        @pl.when(s == steps - 1)
        def _():
            o_ref[index] = (acc[index] + value).astype(BF16)

    if layout in ("rows1", "rows2"):
        groups = int(layout[-1])
        per_tile = N_CHIPS // groups
        rows = m_dim // N_CHIPS // 2
        a_w = m_dim // groups

        def body(a_ref, b_ref, o_ref, acc):
            r = _dot_tn(a_ref[...], b_ref[...])
            for p in range(per_tile):
                for h in range(2):
                    store(o_ref, acc, (h, p), r[(2 * p + h) * rows:(2 * p + h + 1) * rows, :])

        in_specs = [pl.BlockSpec((k_rows, a_w), lambda g, s: (s, g)), pl.BlockSpec((k_rows, n_dim), lambda g, s: (s, 0))]
        out_spec = pl.BlockSpec((2, per_tile, rows, n_dim), lambda g, s: (0, g, 0, 0))
        out_dims, acc_dims = (2, N_CHIPS, rows, n_dim), (2, per_tile, rows, n_dim)
    elif layout == "cols_chip":
        groups = N_CHIPS
        rows, cols = m_dim // 2, n_dim // N_CHIPS

        def body(a_ref, b_ref, o_ref, acc):
            r = _dot_tn(a_ref[...], b_ref[...])
            for h in range(2):
                store(o_ref, acc, h, r[h * rows:(h + 1) * rows, :])

        in_specs = [pl.BlockSpec((k_rows, m_dim), lambda g, s: (s, 0)), pl.BlockSpec((k_rows, cols), lambda g, s: (s, g))]
        out_spec = pl.BlockSpec((2, None, rows, cols), lambda g, s: (0, g, 0, 0))
        out_dims, acc_dims = (2, N_CHIPS, rows, cols), (2, rows, cols)
    else:
        groups = 2
        rows, cols = m_dim // 2, n_dim // N_CHIPS

        def body(a_ref, b_ref, o_ref, acc):
            r = _dot_tn(a_ref[...], b_ref[...])
            for k in range(N_CHIPS):
                store(o_ref, acc, k, r[:, k * cols:(k + 1) * cols])

        in_specs = [pl.BlockSpec((k_rows, rows), lambda g, s: (s, g)), pl.BlockSpec((k_rows, n_dim), lambda g, s: (s, 0))]
        out_spec = pl.BlockSpec((None, N_CHIPS, rows, cols), lambda g, s: (g, 0, 0, 0))
        out_dims, acc_dims = (2, N_CHIPS, rows, cols), (N_CHIPS, rows, cols)

    c_ins, c_shapes, c_sems, c_ops, c_id = _comm_plan(comm)
    nc = len(c_ins)
    c_specs = [ANY] * nc
    if carry is not None:
        assert comm is None and carry.shape[0] % (groups * steps) == 0
        carry_spec = pl.BlockSpec((carry.shape[0] // (groups * steps), carry.shape[1]), lambda g, s: (g * steps + s, 0))
        c_ins, c_shapes, c_specs, nc = (carry,), [jax.ShapeDtypeStruct(carry.shape, carry.dtype)], [carry_spec], 1

    def hosted(a_ref, b_ref, *rest):
        c_in, o_ref, c_out, acc, sems = rest[:nc], rest[nc], rest[nc + 1:2 * nc + 1], rest[2 * nc + 1], rest[2 * nc + 2:]
        g, s = pl.program_id(0), pl.program_id(1)
        if carry is not None:
            c_out[0][...] = c_in[0][...]
            body(a_ref, b_ref, o_ref, acc)
            return
        if nc:
            @pl.when(jnp.logical_and(g == 0, s == 0))
            def _():
                c_ops(c_in, c_out, sems)[0]()

        body(a_ref, b_ref, o_ref, acc)
        if nc:
            step = g * steps + s

            @pl.when(step == max(groups * steps - 2, 0))
            def _():
                c_ops(c_in, c_out, sems)[1]()

            @pl.when(step == groups * steps - 1)
            def _():
                c_ops(c_in, c_out, sems)[2]()

    outs = pl.pallas_call(
        hosted, name=f"weight_grad_{layout}_{m_dim}x{n_dim}", grid=(groups, steps),
        in_specs=in_specs + c_specs, out_specs=[out_spec] + c_specs,
        out_shape=[jax.ShapeDtypeStruct(out_dims, BF16)] + c_shapes,
        scratch_shapes=[pltpu.VMEM(acc_dims, F32)] + c_sems,
        compiler_params=pltpu.CompilerParams(dimension_semantics=("arbitrary", "arbitrary"), vmem_limit_bytes=VMEM_LIMIT,
                                             collective_id=c_id),
    )(a, b, *c_ins)
    return outs if nc else outs[0]


def _exchange_ops(ins, outs, n_big, sems):
    send, recv = sems
    x, y, c, _, _ = _place()
    cps = [pltpu.make_async_remote_copy(
        src_ref=ins[t].at[1 - c] if t < n_big else ins[t], dst_ref=outs[t], send_sem=send.at[t], recv_sem=recv.at[t],
        device_id=(x, y, 1 - c), device_id_type=MESH) for t in range(len(ins))]

    def start():
        for cp in cps:
            cp.start()

    def finish():
        for cp in cps:
            cp.wait()

    return start, finish


def _exchange_shapes(bigs, smalls):
    return [jax.ShapeDtypeStruct((N_CHIPS,) + b.shape[2:], b.dtype) for b in bigs] + [
        jax.ShapeDtypeStruct(s.shape, s.dtype) for s in smalls]


def _comm_plan(comm):
    if comm is None:
        return (), [], [], None, None
    kind, arrays = comm
    n = len(arrays)

    def scatter(i, o, sm):
        start, land, finish = _scatter_ops(i, o, n, sm[:6], sm[6:])
        return lambda: (_handshake(SIBLING_AND_CHIPS), start()), land, finish

    def exchange(i, o, sm):
        start, finish = _exchange_ops(i, o, n, sm)
        return lambda: (_handshake(SIBLING_ONLY), start()), lambda: None, finish

    if kind == "scatter":
        return tuple(arrays), _scatter_shapes(arrays, ()), _scatter_scratch(arrays, ()), scatter, SIBLING_AND_CHIPS
    return tuple(arrays), _exchange_shapes(arrays, ()), [pltpu.SemaphoreType.DMA((n,))] * 2, exchange, SIBLING_ONLY


def _sibling_exchange(bigs, smalls, tag):
    nb, nt = len(bigs), len(bigs) + len(smalls)

    def body(*refs):
        start, finish = _exchange_ops(refs[:nt], refs[nt:2 * nt], nb, refs[2 * nt:])
        _handshake(SIBLING_ONLY)
        start()
        finish()

    return pl.pallas_call(
        body, name=f"sibling_exchange_{tag}", out_shape=_exchange_shapes(bigs, smalls),
        in_specs=[ANY] * nt, out_specs=[ANY] * nt,
        scratch_shapes=[pltpu.SemaphoreType.DMA((nt,)), pltpu.SemaphoreType.DMA((nt,))],
        compiler_params=pltpu.CompilerParams(collective_id=SIBLING_ONLY),
    )(*bigs, *smalls)


def _pair_sum(core, mine, theirs, tag, block_rows):
    _, _, rows, cols = mine.shape
    steps = rows // block_rows

    def body(core_ref, a_ref, b_ref, o_ref):
        o_ref[...] = (a_ref[...].astype(F32) + b_ref[...].astype(F32)).astype(BF16)

    grid_spec = pltpu.PrefetchScalarGridSpec(
        num_scalar_prefetch=1, grid=(N_CHIPS, steps),
        in_specs=[pl.BlockSpec((None, None, block_rows, cols), lambda k, r, core_ref: (core_ref[0], k, r, 0)),
                  pl.BlockSpec((None, block_rows, cols), lambda k, r, core_ref: (k, r, 0))],
        out_specs=pl.BlockSpec((None, block_rows, cols), lambda k, r, core_ref: (k, r, 0)),
    )
    return pl.pallas_call(
        body, name=f"pair_sum_{tag}", grid_spec=grid_spec,
        out_shape=jax.ShapeDtypeStruct((N_CHIPS, rows, cols), BF16),
        compiler_params=pltpu.CompilerParams(dimension_semantics=("arbitrary", "arbitrary"), vmem_limit_bytes=VMEM_LIMIT),
    )(core, mine, theirs)


def _pair_sum_small(mine, theirs):
    (m_f2, m_b1, m_b2, m_sf, m_s5, m_sp) = mine

    def body(a0, a1, a2, a3, a4, a5, b0, b1, b2, b3, b4, b5, o_m, o_f, o_5, o_p):
        sm = (a0[...] + a1[...] + a2[...]) + (b0[...] + b1[...] + b2[...])
        sf = a3[...] + b3[...]
        s5 = a4[...] + b4[...]
        for h in range(2):
            o_m[h] = sm[:, h * (D_MODEL // 2):(h + 1) * (D_MODEL // 2)]
            o_f[h] = sf[:, h * (D_FF // 2):(h + 1) * (D_FF // 2)]
            o_5[h] = s5[:, h * (D_CONV // 2):(h + 1) * (D_CONV // 2)]
            for g in range(2):
                o_p[h, g] = a5[2 * h + g] + b5[2 * h + g]

    out_shape = [
        jax.ShapeDtypeStruct((2, 8, D_MODEL // 2), F32), jax.ShapeDtypeStruct((2, 8, D_FF // 2), F32),
        jax.ShapeDtypeStruct((2, 40, D_CONV // 2), F32), jax.ShapeDtypeStruct((2, 2, POOL_GROUP, POOL_GROUP), F32),
    ]
    return pl.pallas_call(body, name="pair_sum_small", out_shape=out_shape, in_specs=[VMEM] * 12, out_specs=[VMEM] * 4)(
        *mine, *theirs)


def _scatter_ops(ins, outs, n_parts, sems, stages, landed=False):
    ici_send, ici_recv, fwd_send, fwd_recv, loc_in, loc_out = sems
    nt = len(ins)
    x, y, c, k, chips = _place()

    def src_of(t, kk):
        return ins[t].at[kk] if t < n_parts else ins[t].at[c]

    def ici(t, j, kk, slot):
        return pltpu.make_async_remote_copy(
            src_ref=src_of(t, kk), dst_ref=outs[t].at[c, slot], send_sem=ici_send.at[t * 3 + j],
            recv_sem=ici_recv.at[t * 3 + j], device_id=(*chips[j], c), device_id_type=MESH)

    def fwd(t, half):
        slots = outs[t].at[half]
        return pltpu.make_async_remote_copy(
            src_ref=slots, dst_ref=slots, send_sem=fwd_send.at[t], recv_sem=fwd_recv.at[t],
            device_id=(x, y, 1 - c), device_id_type=MESH)

    local = [_staged(src_of(t, k), outs[t].at[c, k], stages[t], loc_in.at[t], loc_out.at[t]) for t in range(nt)]
    peers = [(t, j, 2 * qx + qy) for t in range(nt) for j, (qx, qy) in enumerate(chips)]
    sends = [] if landed else [ici(t, j, kq, k) for t, j, kq in peers]

    def start():
        for cp in local:
            cp[0]()
        for cp in sends:
            cp.start()

    def land():
        for cp in local:
            cp[1]()
        if not landed:
            for t, j, kq in peers:
                ici(t, j, kq, kq).wait_recv()
        for cp in local:
            cp[2]()
        for t in range(nt):
            fwd(t, c).start()

    def finish():
        for t in range(nt):
            fwd(t, 1 - c).wait_recv()
            fwd(t, c).wait_send()
        for cp in sends:
            cp.wait_send()

    return start, land, finish


def _scatter_scratch(parts, smalls):
    arrays = tuple(parts) + tuple(smalls)
    nt = len(arrays)
    return ([pltpu.SemaphoreType.DMA((3 * nt,))] * 2 + [pltpu.SemaphoreType.DMA((nt,))] * 4
            + [pltpu.VMEM(a.shape[1:], a.dtype) for a in arrays])


def _scatter_shapes(parts, smalls):
    return [jax.ShapeDtypeStruct((2, N_CHIPS) + p.shape[1:], p.dtype) for p in tuple(parts) + tuple(smalls)]


HBM_SPEC = pl.BlockSpec(memory_space=pltpu.HBM)
SEM_SPEC = pl.BlockSpec(memory_space=pltpu.SEMAPHORE)
EFFECT = pltpu.SideEffectType.DATAFLOW_SIDE_EFFECTING


def _ici_copy(ins, lands, n_parts, send, recv, t, j):
    _, _, c, k, chips = _place()
    qx, qy = chips[j]
    src = ins[t].at[2 * qx + qy] if t < n_parts else ins[t].at[c]
    return pltpu.make_async_remote_copy(
        src_ref=src, dst_ref=lands[t].at[c, k], send_sem=send.at[t * 3 + j], recv_sem=recv.at[t * 3 + j],
        device_id=(qx, qy, c), device_id_type=MESH)


def _scatter_start(parts, smalls):
    arrays = tuple(parts) + tuple(smalls)
    nt = len(arrays)

    def body(*refs):
        ins, lands = refs[:nt], refs[nt:2 * nt]
        send, recv = refs[2 * nt], refs[2 * nt + 1]
        token = refs[-1]
        for t in range(nt):
            for j in range(3):
                _ici_copy(ins, lands, len(parts), send, recv, t, j).start()
        token[...] = jnp.zeros(token.shape, F32)

    land_shapes = _scatter_shapes(parts, smalls)
    out_shape = ([pltpu.SemaphoreType.DMA((3 * nt,))] * 2 + [pltpu.HBM(a.shape, a.dtype) for a in arrays]
                 + [pltpu.HBM(a.shape, a.dtype) for a in land_shapes] + [jax.ShapeDtypeStruct((8, 128), F32)])
    operands = [pltpu.with_memory_space_constraint(a, pltpu.HBM) for a in arrays]
    operands += [pltpu.with_memory_space_constraint(lax.empty(a.shape, a.dtype), pltpu.HBM) for a in land_shapes]
    outs = pl.pallas_call(
        body, name="scatter_start", out_shape=out_shape, in_specs=[HBM_SPEC] * (2 * nt),
        out_specs=[SEM_SPEC] * 2 + [HBM_SPEC] * (2 * nt) + [VMEM],
        input_output_aliases={i: 2 + i for i in range(2 * nt)},
        compiler_params=pltpu.CompilerParams(has_side_effects=EFFECT),
    )(*operands)
    return outs[0], outs[1], outs[2:2 + nt], outs[2 + nt:2 + 2 * nt], outs[-1]


def _scatter_wait(send, recv, ins, lands, n_parts, after):
    nt = len(ins)

    def body(*refs):
        in_refs, land_refs = refs[:nt], refs[nt:2 * nt]
        send_ref, recv_ref = refs[2 * nt], refs[2 * nt + 1]
        for t in range(nt):
            for j in range(3):
                cp = _ici_copy(in_refs, land_refs, n_parts, send_ref, recv_ref, t, j)
                cp.wait_send()
                cp.wait_recv()

    outs = pl.pallas_call(
        body, name="scatter_wait", out_shape=[pltpu.HBM(a.shape, a.dtype) for a in tuple(ins) + tuple(lands)],
        in_specs=[HBM_SPEC] * (2 * nt) + [SEM_SPEC] * 2 + [ANY] * len(after), out_specs=[HBM_SPEC] * (2 * nt),
        input_output_aliases={i: i for i in range(2 * nt)},
        compiler_params=pltpu.CompilerParams(has_side_effects=EFFECT),
    )(*ins, *lands, send, recv, *after)
    return outs[:nt], outs[nt:]


def _scatter_forward(ins, lands, n_parts):
    nt = len(ins)

    def body(*refs):
        start, land, finish = _scatter_ops(
            refs[:nt], refs[2 * nt:3 * nt], n_parts, refs[3 * nt:3 * nt + 6], refs[3 * nt + 6:], landed=True)
        _handshake(SIBLING_ONLY)
        start()
        land()
        finish()

    return pl.pallas_call(
        body, name="scatter_forward", out_shape=[jax.ShapeDtypeStruct(a.shape, a.dtype) for a in lands],
        in_specs=[ANY] * (2 * nt), out_specs=[ANY] * nt, input_output_aliases={nt + i: i for i in range(nt)},
        scratch_shapes=_scatter_scratch(ins[:n_parts], ins[n_parts:]),
        compiler_params=pltpu.CompilerParams(collective_id=SIBLING_ONLY),
    )(*ins, *lands)


def _chip_scatter(parts, smalls):
    nt = len(parts) + len(smalls)

    def body(*refs):
        start, land, finish = _scatter_ops(refs[:nt], refs[nt:2 * nt], len(parts), refs[2 * nt:2 * nt + 6], refs[2 * nt + 6:])
        _handshake(SIBLING_AND_CHIPS)
        start()
        land()
        finish()

    return pl.pallas_call(
        body, name="chip_scatter", out_shape=_scatter_shapes(parts, smalls), in_specs=[ANY] * nt, out_specs=[ANY] * nt,
        scratch_shapes=_scatter_scratch(parts, smalls),
        compiler_params=pltpu.CompilerParams(collective_id=SIBLING_AND_CHIPS),
    )(*parts, *smalls)


def _adamw(w, g, m, v):
    m = ADAM_B1 * m + (1.0 - ADAM_B1) * g
    v = ADAM_B2 * v + (1.0 - ADAM_B2) * (g * g)
    m_hat = m / (1.0 - ADAM_B1 ** ADAM_STEP)
    v_hat = v / (1.0 - ADAM_B2 ** ADAM_STEP)
    delta = -ADAM_LR * (m_hat / (jnp.sqrt(v_hat) + ADAM_EPS) + ADAM_WD * w)
    return delta, m, v


def _adam_big(parts, w, m, v, tag, block_rows, token):
    _, _, rows, cols = parts.shape
    steps = rows // block_rows

    def body(p_ref, w_ref, m_ref, v_ref, token_ref, g_out, d_out, m_out, v_out):
        g = p_ref[0].astype(F32)
        for q in range(1, N_CHIPS):
            g = g + p_ref[q].astype(F32)
        delta, m_new, v_new = _adamw(w_ref[...], g, m_ref[...], v_ref[...])
        g_out[...] = g
        d_out[...] = delta
        m_out[...] = m_new
        v_out[...] = v_new

    blk = pl.BlockSpec((block_rows, cols), lambda h, r: (h * steps + r, 0))
    return pl.pallas_call(
        body, name=f"adam_{tag}", grid=(2, steps),
        in_specs=[pl.BlockSpec((None, N_CHIPS, block_rows, cols), lambda h, r: (h, 0, r, 0)), blk, blk, blk, ANY],
        out_specs=[blk] * 4, out_shape=[jax.ShapeDtypeStruct(w.shape, F32)] * 4,
        compiler_params=pltpu.CompilerParams(dimension_semantics=("arbitrary", "arbitrary"), vmem_limit_bytes=VMEM_LIMIT),
    )(parts, w, m, v, token)


def _reduce_small(l_m, l_f, l_5, l_p):
    def total(ref):
        t = ref[:, 0]
        for q in range(1, N_CHIPS):
            t = t + ref[:, q]
        return t

    def body(m_ref, f_ref, s_ref, p_ref, g1_o, g2_o, g3_o, loss_o, wf_o, fb_o, wa_o, cb_o, lg_o, lb_o, ps_o, pw_o):
        tm, tf, t5, tp = total(m_ref), total(f_ref), total(s_ref), total(p_ref)
        sm = jnp.concatenate([tm[0], tm[1]], axis=1)
        sf = jnp.concatenate([tf[0], tf[1]], axis=1)
        s5 = jnp.concatenate([t5[0], t5[1]], axis=1)
        g1_o[...] = sm[0:1]
        g2_o[...] = sm[1:2]
        g3_o[...] = sm[2:3]
        loss_o[...] = sm[3:4, 0:128]
        wf_o[...] = sf
        fb_o[...] = sf[3:4]
        wa_o[...] = s5[0:32]
        cb_o[...] = s5[32:33]
        lg_o[...] = s5[33:34]
        lb_o[...] = s5[34:35]
        ps_o[...] = s5[35:36]
        for h in range(2):
            for g in range(2):
                pw_o[2 * h + g] = tp[h, g]

    row = lambda w: jax.ShapeDtypeStruct((1, w), F32)
    out_shape = [row(D_MODEL), row(D_MODEL), row(D_MODEL), row(128), jax.ShapeDtypeStruct((8, D_FF), F32), row(D_FF),
                 jax.ShapeDtypeStruct((32, D_CONV), F32), row(D_CONV), row(D_CONV), row(D_CONV), row(D_POOL),
                 jax.ShapeDtypeStruct((4, POOL_GROUP, POOL_GROUP), F32)]
    return pl.pallas_call(body, name="reduce_small", out_shape=out_shape, in_specs=[VMEM] * 4, out_specs=[VMEM] * 12)(
        l_m, l_f, l_5, l_p)


def _adam_small(ws, gs, ms, vs):
    count = len(ws)

    def body(*refs):
        w_r, g_r, m_r, v_r = (refs[t * count:(t + 1) * count] for t in range(4))
        d_o, m_o, v_o = (refs[(4 + t) * count:(5 + t) * count] for t in range(3))
        for t in range(count):
            delta, m_new, v_new = _adamw(w_r[t][...], g_r[t][...], m_r[t][...], v_r[t][...])
            d_o[t][...] = delta
            m_o[t][...] = m_new
            v_o[t][...] = v_new

    out_shape = [jax.ShapeDtypeStruct(w.shape, F32) for w in ws] * 3
    outs = pl.pallas_call(body, name="adam_small", out_shape=out_shape, in_specs=[VMEM] * (4 * count),
                          out_specs=[VMEM] * (3 * count))(*ws, *gs, *ms, *vs)
    return outs[:count], outs[count:2 * count], outs[2 * count:]


MIX_TILE = 512
FFN_TILE = 256
GRAD_K = 2048


def kernel(x, norm_mix_g, w_in, conv_a_w, conv_a_b, ln_a_g, ln_a_b, pool_w, pool_scale, w_out, norm_ffn_g, w_up, conv_f_w, conv_f_b, w_down, norm_final_g, loss_target, m_norm_mix_g, m_w_in, m_conv_a_w, m_conv_a_b, m_ln_a_g, m_ln_a_b, m_pool_w, m_pool_scale, m_w_out, m_norm_ffn_g, m_w_up, m_conv_f_w, m_conv_f_b, m_w_down, m_norm_final_g, v_norm_mix_g, v_w_in, v_conv_a_w, v_conv_a_b, v_ln_a_g, v_ln_a_b, v_pool_w, v_pool_scale, v_w_out, v_norm_ffn_g, v_w_up, v_conv_f_w, v_conv_f_b, v_w_down, v_norm_final_g):
    seq = x.shape[1]
    xs, ts = x[0], loss_target[0]
    mix_tile, ffn_tile, grad_k = min(MIX_TILE, seq), min(FFN_TILE, seq), min(GRAD_K, seq)
    chip = 2 * lax.axis_index("x") + lax.axis_index("y")
    core = lax.axis_index("c").astype(jnp.int32).reshape(1)

    wa_s = jnp.pad(conv_a_w[0], ((0, 32 - CONV_A), (0, 0)))
    wf_s = jnp.pad(conv_f_w[0], ((0, 8 - CONV_F), (0, 0)))
    win_b, wout_b, wup_b, wdown_b = _cast_shards(w_in[0], w_out[0], w_up[0], w_down[0])
    g3 = norm_final_g.reshape(1, D_MODEL)
    pw = pool_w[0]

    h1, proj, cpre, dpool, mcat, x1, win, wout, wup, wa_g, wf_g = _mixer_fwd(
        xs, norm_mix_g, win_b, wout_b, wup_b, wa_s, wf_s, conv_a_b, ln_a_g, ln_a_b, pw, pool_scale, mix_tile)
    wa = jnp.transpose(wa_g, (1, 0, 2)).reshape(32, D_CONV)
    wf = jnp.transpose(wf_g, (1, 0, 2)).reshape(8, D_FF)
    h2, up, gcs, act, wdown = _ffn_up(x1, norm_ffn_g, wup, wf, conv_f_b, wdown_b, ffn_tile)
    dx2b, sm_f2 = _ffn_down(x1, act, wdown, g3, ts, mix_tile)
    tags = ("w_in", "w_out", "w_up", "w_down")
    blocks = (256, 128, 256, 176)
    g_wdown = _weight_grad(act, dx2b, "rows2", grad_k)
    dup, dx1, dx1b, sm_b1, sf, l_wdown = _ffn_bwd(
        dx2b, up, gcs, x1, norm_ffn_g, wup, wf, wdown, ("exchange", [g_wdown]), ffn_tile)
    p_wdown = _pair_sum(core, g_wdown, l_wdown, tags[3], g_wdown.shape[2])
    g_wup, s_wdown = _weight_grad(h2, dup, "cols_chip", grad_k, ("scatter", [p_wdown]))
    g_wout, l_wup = _weight_grad(mcat, dx1b, "rows1", grad_k, ("exchange", [g_wup]))
    p_wup = _pair_sum(core, g_wup, l_wup, tags[2], g_wup.shape[2])
    l_wout, = _sibling_exchange((g_wout,), (), "early")
    p_wout = _pair_sum(core, g_wout, l_wout, tags[1], g_wout.shape[2])
    dproj, gx, sm_b2, s5, sp, s_wout, s_wup = _mixer_bwd(
        dx1, xs, proj, cpre, dpool, norm_mix_g, win, wa, ln_a_g, ln_a_b, pw, pool_scale, wout, [p_wout, p_wup], mix_tile)
    g_win, grad_x = _weight_grad(h1, dproj, "cols_half", grad_k, carry=gx)

    smalls = (sm_f2, sm_b1, sm_b2, sf, s5, sp)
    landed = _sibling_exchange((g_win,), smalls, "late")
    part_win = _pair_sum(core, g_win, landed[0], tags[0], g_win.shape[2])
    small_parts = _pair_sum_small(smalls, landed[1:])
    send, recv, late_src, late_land, token = _scatter_start([part_win], small_parts)
    big_w = (w_in[0], w_out[0], w_up[0], w_down[0])
    big_m = (m_w_in[0], m_w_out[0], m_w_up[0], m_w_down[0])
    big_v = (v_w_in[0], v_w_out[0], v_w_up[0], v_w_down[0])
    big = {}
    for t, p in ((1, s_wout), (2, s_wup), (3, s_wdown)):
        big[tags[t]] = _adam_big(p, big_w[t], big_m[t], big_v[t], tags[t], blocks[t], token)
    late_src, late_land = _scatter_wait(send, recv, late_src, late_land, 1, [big[tags[t]][3] for t in (1, 2, 3)])
    late = _scatter_forward(late_src, late_land, 1)
    big[tags[0]] = _adam_big(late[0], big_w[0], big_m[0], big_v[0], tags[0], blocks[0], token)
    big = {tag: [a[None] for a in outs] for tag, outs in big.items()}
    scattered = [None] * 4 + list(late[1:])

    (g_g1, g_g2, g_g3, loss_row, g_wf_all, g_fb, g_wa_all, g_cb, g_lg, g_lb, g_ps, g_pw) = _reduce_small(*scattered[4:])
    g_wa = lax.dynamic_slice(g_wa_all, (0, chip * (D_CONV // N_CHIPS)), (32, D_CONV // N_CHIPS))[:CONV_A]
    g_wf = lax.dynamic_slice(g_wf_all, (0, chip * (D_FF // N_CHIPS)), (8, D_FF // N_CHIPS))[:CONV_F]
    small_names = ("norm_mix_g", "conv_a_w", "conv_a_b", "ln_a_g", "ln_a_b", "pool_w", "pool_scale", "norm_ffn_g",
                   "conv_f_w", "conv_f_b", "norm_final_g")
    small_w = (norm_mix_g, conv_a_w[0], conv_a_b, ln_a_g, ln_a_b, pw, pool_scale, norm_ffn_g, conv_f_w[0], conv_f_b, g3)
    small_m = (m_norm_mix_g, m_conv_a_w[0], m_conv_a_b, m_ln_a_g, m_ln_a_b, m_pool_w[0], m_pool_scale, m_norm_ffn_g,
               m_conv_f_w[0], m_conv_f_b, m_norm_final_g.reshape(1, D_MODEL))
    small_v = (v_norm_mix_g, v_conv_a_w[0], v_conv_a_b, v_ln_a_g, v_ln_a_b, v_pool_w[0], v_pool_scale, v_norm_ffn_g,
               v_conv_f_w[0], v_conv_f_b, v_norm_final_g.reshape(1, D_MODEL))
    small_g = (g_g1, g_wa, g_cb, g_lg, g_lb, g_pw, g_ps, g_g2, g_wf, g_fb, g_g3)
    s_delta, s_m, s_v = _adam_small(small_w, small_g, small_m, small_v)
    shapes = {"conv_a_w": conv_a_w.shape, "pool_w": pool_w.shape, "conv_f_w": conv_f_w.shape, "norm_final_g": norm_final_g.shape}
    small = {}
    for t, name in enumerate(small_names):
        shp = shapes.get(name)
        small[name] = [a if shp is None else a.reshape(shp) for a in (small_g[t], s_delta[t], s_m[t], s_v[t])]

    order = ("norm_mix_g", "w_in", "conv_a_w", "conv_a_b", "ln_a_g", "ln_a_b", "pool_w", "pool_scale", "w_out", "norm_ffn_g",
             "w_up", "conv_f_w", "conv_f_b", "w_down", "norm_final_g")
    table = {**big, **small}
    loss = loss_row[0, 0]
    outs = [loss, grad_x[None]]
    for t in range(4):
        outs += [table[name][t] for name in order]
    return tuple(outs)
```

```python
import functools

import jax
import jax.numpy as jnp
from jax import lax
from jax.experimental import pallas as pl
from jax.experimental.pallas import tpu as pltpu

F32 = jnp.float32
BF16 = jnp.bfloat16
EPS = 1e-6
ADAM_LR = 0.001
ADAM_B1 = 0.9
ADAM_B2 = 0.999
ADAM_EPS = 1e-08
ADAM_WD = 0.01
ADAM_STEP = 10

D_MODEL = 1024
D_CONV = 512
D_POOL = 512
D_IN = 1536
D_FF = 2816
CONV_A = 31
CONV_F = 3
POOL_WINDOWS = (2, 4, 8, 16)
POOL_GROUP = 128
N_CHIPS = 4
FF_CHUNK = 256
N_FF_CHUNKS = D_FF // FF_CHUNK
UP_CHUNK = 1408
A_HALO = 32
P_HALO = 16
VMEM_LIMIT = 56 * 1024 * 1024
MESH = pl.DeviceIdType.MESH

ANY = pl.BlockSpec(memory_space=pl.ANY)
VMEM = pl.BlockSpec(memory_space=pltpu.VMEM)


def _dot(a, b):
    return jnp.dot(a, b, preferred_element_type=F32)


def _dot_nt(a, b):
    return lax.dot_general(a, b, (((1,), (1,)), ((), ())), preferred_element_type=F32)


def _dot_tn(a, b):
    return lax.dot_general(a, b, (((0,), (0,)), ((), ())), preferred_element_type=F32)


def _sigmoid(v):
    return jax.nn.sigmoid(v)


def _colsum(v):
    return jnp.sum(v, axis=0, keepdims=True)


def _rowmean(v):
    return jnp.mean(v, axis=-1, keepdims=True)


def _place():
    x, y, c = lax.axis_index("x"), lax.axis_index("y"), lax.axis_index("c")
    chips = [(1 - x, y), (x, 1 - y), (1 - x, 1 - y)]
    return x, y, c, 2 * x + y, chips


SIBLING_ONLY, SIBLING_AND_CHIPS = 0, 1


def _handshake(collective):
    x, y, c, _, chips = _place()
    peers = [(x, y, 1 - c)] + ([(*chip, c) for chip in chips] if collective == SIBLING_AND_CHIPS else [])
    barrier = pltpu.get_barrier_semaphore()
    for peer in peers:
        pl.semaphore_signal(barrier, inc=1, device_id=peer, device_id_type=MESH)
    pl.semaphore_wait(barrier, len(peers))


def _staged(src, dst, stage, sem_in, sem_out):
    hop_in = pltpu.make_async_copy(src, stage, sem_in)
    hop_out = pltpu.make_async_copy(stage, dst, sem_out)

    def relay():
        hop_in.wait()
        hop_out.start()

    return hop_in.start, relay, hop_out.wait


def _gather_ops(bufs, fulls, col_sharded, sems, stages):
    ici_send, ici_recv, fwd_send, fwd_recv, loc_in, loc_out = sems
    n_big = len(bufs)
    x, y, c, k, chips = _place()

    def block(i, kk, half=None):
        rows, cols = bufs[i].shape
        if col_sharded[i]:
            rs = slice(None) if half is None else pl.ds(pl.multiple_of(half * (rows // 2), 16), rows // 2)
            return fulls[i].at[rs, pl.ds(pl.multiple_of(kk * cols, 128), cols)]
        if half is None:
            return fulls[i].at[pl.ds(pl.multiple_of(kk * rows, 16), rows), :]
        return fulls[i].at[pl.ds(pl.multiple_of(kk * rows + half * (rows // 2), 16), rows // 2), :]

    def my_half(i):
        rows = bufs[i].shape[0]
        return bufs[i].at[pl.ds(pl.multiple_of(c * (rows // 2), 16), rows // 2), :]

    def ici(i, j, kk):
        return pltpu.make_async_remote_copy(
            src_ref=my_half(i), dst_ref=block(i, kk, c), send_sem=ici_send.at[i * 3 + j], recv_sem=ici_recv.at[i * 3 + j],
            device_id=(*chips[j], c), device_id_type=MESH)

    def fwd(i, j, kk, half):
        return pltpu.make_async_remote_copy(
            src_ref=block(i, kk, half), dst_ref=block(i, kk, half), send_sem=fwd_send.at[i * 3 + j],
            recv_sem=fwd_recv.at[i * 3 + j], device_id=(x, y, 1 - c), device_id_type=MESH)

    local = [_staged(bufs[i], block(i, k), stages[i], loc_in.at[i], loc_out.at[i]) for i in range(n_big)]
    sends = [ici(i, j, k) for i in range(n_big) for j in range(3)]
    peers = [(i, j, 2 * qx + qy) for i in range(n_big) for j, (qx, qy) in enumerate(chips)]

    def start():
        for cp in local:
            cp[0]()
        for cp in sends:
            cp.start()

    def land():
        for cp in local:
            cp[1]()
        for i, j, kq in peers:
            ici(i, j, kq).wait_recv()
            fwd(i, j, kq, c).start()

    def finish():
        for i, j, kq in peers:
            fwd(i, j, kq, 1 - c).wait_recv()
            fwd(i, j, kq, c).wait_send()
        for cp in sends:
            cp.wait_send()
        for cp in local:
            cp[2]()

    return start, land, finish


def _gather_scratch(shards):
    n_big = len(shards)
    return ([pltpu.SemaphoreType.DMA((3 * n_big,))] * 4 + [pltpu.SemaphoreType.DMA((n_big,))] * 2
            + [pltpu.VMEM(b.shape, b.dtype) for b in shards])


def _tap_ops(srcs, dsts, sems):
    send, recv, loc = sems
    _, _, c, k, chips = _place()

    def copy(t, j, kk):
        return pltpu.make_async_remote_copy(
            src_ref=srcs[t], dst_ref=dsts[t].at[kk], send_sem=send.at[t * 3 + j], recv_sem=recv.at[t * 3 + j],
            device_id=(*chips[j], c), device_id_type=MESH)

    local = [pltpu.make_async_copy(srcs[t], dsts[t].at[k], loc.at[t]) for t in range(len(srcs))]
    sends = [[copy(t, j, k) for j in range(3)] for t in range(len(srcs))]

    def start():
        for t, cp in enumerate(local):
            cp.start()
            for sd in sends[t]:
                sd.start()

    def wait(t):
        for j, (qx, qy) in enumerate(chips):
            copy(t, j, 2 * qx + qy).wait_recv()
        for sd in sends[t]:
            sd.wait_send()
        local[t].wait()

    return start, wait


def _cast_shards(*shards):
    def body(*refs):
        for src, dst in zip(refs[:len(shards)], refs[len(shards):]):
            dst[...] = src[...].astype(BF16)

    return pl.pallas_call(
        body, name="cast_shards", out_shape=[jax.ShapeDtypeStruct(s.shape, BF16) for s in shards],
        in_specs=[VMEM] * len(shards), out_specs=[VMEM] * len(shards),
        compiler_params=pltpu.CompilerParams(vmem_limit_bytes=VMEM_LIMIT),
    )(*shards)


def _load_weights(pairs, sem, first=0):
    cps = [pltpu.make_async_copy(src, dst, sem.at[first + i]) for i, (src, dst) in enumerate(pairs)]
    for cp in cps:
        cp.start()
    for cp in cps:
        cp.wait()


def _shifted_views(buf, shifted, t_rows):
    n = t_rows + A_HALO - 8
    for b in range(1, 8):
        shifted[b - 1] = buf[b:b + n, :]

    def view(offset):
        a, b = divmod(offset, 8)
        if b == 0:
            return buf[8 * a:8 * a + t_rows, :]
        return shifted[b - 1, 8 * a:8 * a + t_rows, :]

    return view


def _pool_count(tile, t_rows, w):
    row = lax.broadcasted_iota(jnp.int32, (t_rows, POOL_GROUP), 0) + tile * t_rows
    return jnp.minimum(row + 1, w).astype(F32)


def _mixer_fwd(x, g1, win_b, wout_b, wup_b, wa_s, wf_s, cb, lg, lb, pw, ps, tile_rows):
    seq = x.shape[0]
    tr = tile_rows
    n = seq // tr

    def body(x_ref, g1_ref, win_b_hbm, wout_b_hbm, wup_b_hbm, wa_s_hbm, wf_s_hbm, cb_ref, lg_ref, lb_ref, pw_ref,
             ps_ref, h1_ref, proj_ref, c_ref, d_ref, m_ref, x1_ref, win_f, wout_f, wup_f, wa_g, wf_g,
             win_v, wout_v, wa_ref, ubuf, ushift, bbuf, sem, *csems):
        i = pl.program_id(0)
        first_sems, first_stages, second_sems, second_stages, later_sems, later_stages, tap_sems = (
            csems[0:6], csems[6:7], csems[7:13], csems[13:14], csems[14:20], csems[20:21], csems[21:24])

        def first():
            return _gather_ops((win_b_hbm,), (win_f,), (True,), first_sems, first_stages)

        def second():
            return _gather_ops((wout_b_hbm,), (wout_f,), (False,), second_sems, second_stages)

        def later():
            return _gather_ops((wup_b_hbm,), (wup_f,), (True,), later_sems, later_stages)

        def taps():
            return _tap_ops((wa_s_hbm, wf_s_hbm), (wa_g, wf_g), tap_sems)

        @pl.when(i == 0)
        def _():
            _handshake(SIBLING_AND_CHIPS)
            first()[0]()
            taps()[0]()
            second()[0]()
            later()[0]()
            first()[1]()
            first()[2]()
            _load_weights([(win_f, win_v)], sem)
            ubuf[0:A_HALO, :] = jnp.zeros((A_HALO, D_CONV), F32)
            bbuf[0:P_HALO, :] = jnp.zeros((P_HALO, D_POOL), F32)

        xv = x_ref[...]
        r = lax.rsqrt(_rowmean(xv * xv) + EPS)
        h1 = (xv * r * g1_ref[...]).astype(BF16)
        h1_ref[...] = h1
        proj = _dot(h1, win_v[...])
        proj_ref[...] = proj.astype(BF16)

        @pl.when(i == 0)
        def _():
            taps()[1](0)
            _load_weights([(wa_g.at[kk], wa_ref.at[:, kk * (D_CONV // N_CHIPS):(kk + 1) * (D_CONV // N_CHIPS)])
                           for kk in range(N_CHIPS)], sem, 2)

        av, ag, bi = proj[:, :D_CONV], proj[:, D_CONV:2 * D_CONV], proj[:, 2 * D_CONV:]
        ubuf[A_HALO:A_HALO + tr, :] = av * _sigmoid(ag)
        off = A_HALO - (CONV_A - 1)
        uview = _shifted_views(ubuf, ushift, tr)
        acc = wa_ref[0:1, :] * uview(off)
        for j in range(1, CONV_A):
            acc = acc + wa_ref[j:j + 1, :] * uview(off + j)
        cv = acc + cb_ref[...]
        ubuf[0:A_HALO, :] = ubuf[tr:tr + A_HALO, :]
        c_ref[...] = cv.astype(BF16)
        xc = cv - _rowmean(cv)
        z = xc * lax.rsqrt(_rowmean(xc * xc) + EPS)
        ln = z * lg_ref[...] + lb_ref[...]
        ya = ln * _sigmoid(ln)
        bbuf[P_HALO:P_HALO + tr, :] = bi
        ds, ybs = [], []
        for g, w in enumerate(POOL_WINDOWS):
            cols = slice(g * POOL_GROUP, (g + 1) * POOL_GROUP)
            s = bi[:, cols]
            for kk in range(1, w):
                s = s + bbuf[P_HALO - kk:P_HALO - kk + tr, cols]
            dg = s / _pool_count(i, tr, w) - bi[:, cols]
            ds.append(dg)
            ybs.append(_dot(dg.astype(BF16), pw_ref[g].astype(BF16)))
        bbuf[0:P_HALO, :] = bbuf[tr:tr + P_HALO, :]
        d_ref[...] = jnp.concatenate(ds, axis=1).astype(BF16)
        yb = jnp.concatenate(ybs, axis=1) * ps_ref[...]
        m = jnp.concatenate([ya, yb], axis=1).astype(BF16)
        m_ref[...] = m

        @pl.when(i == 0)
        def _():
            second()[1]()
            second()[2]()
            _load_weights([(wout_f, wout_v)], sem, 1)

        x1_ref[...] = xv + _dot(m, wout_v[...])

        @pl.when(i == n - 1)
        def _():
            later()[1]()
            later()[2]()
            taps()[1](1)

    tile = lambda w: pl.BlockSpec((tr, w), lambda i: (i, 0))
    full = lambda a: pl.BlockSpec(a.shape, lambda i: (0,) * a.ndim)
    return pl.pallas_call(
        body, name="mixer_fwd", grid=(n,),
        in_specs=[tile(D_MODEL), full(g1)] + [ANY] * 5 + [full(cb), full(lg), full(lb), full(pw), full(ps)],
        out_specs=[tile(D_MODEL), tile(D_IN), tile(D_CONV), tile(D_POOL), tile(D_MODEL), tile(D_MODEL)] + [ANY] * 5,
        out_shape=[
            jax.ShapeDtypeStruct((seq, D_MODEL), BF16), jax.ShapeDtypeStruct((seq, D_IN), BF16),
            jax.ShapeDtypeStruct((seq, D_CONV), BF16), jax.ShapeDtypeStruct((seq, D_POOL), BF16),
            jax.ShapeDtypeStruct((seq, D_MODEL), BF16), jax.ShapeDtypeStruct((seq, D_MODEL), F32),
            jax.ShapeDtypeStruct((D_MODEL, D_IN), BF16), jax.ShapeDtypeStruct((D_MODEL, D_MODEL), BF16),
            jax.ShapeDtypeStruct((D_MODEL, 2 * D_FF), BF16),
            jax.ShapeDtypeStruct((N_CHIPS,) + wa_s.shape, F32), jax.ShapeDtypeStruct((N_CHIPS,) + wf_s.shape, F32),
        ],
        scratch_shapes=[
            pltpu.VMEM((D_MODEL, D_IN), BF16), pltpu.VMEM((D_MODEL, D_MODEL), BF16), pltpu.VMEM((32, D_CONV), F32),
            pltpu.VMEM((tr + A_HALO, D_CONV), F32), pltpu.VMEM((7, tr + A_HALO - 8, D_CONV), F32),
            pltpu.VMEM((tr + P_HALO, D_POOL), F32), pltpu.SemaphoreType.DMA((2 + N_CHIPS,)),
        ] + _gather_scratch((win_b,)) + _gather_scratch((wout_b,)) + _gather_scratch((wup_b,)) + [
            pltpu.SemaphoreType.DMA((6,)), pltpu.SemaphoreType.DMA((6,)), pltpu.SemaphoreType.DMA((2,))],
        compiler_params=pltpu.CompilerParams(dimension_semantics=("arbitrary",), vmem_limit_bytes=VMEM_LIMIT,
                                             collective_id=SIBLING_AND_CHIPS),
    )(x, g1, win_b, wout_b, wup_b, wa_s, wf_s, cb, lg, lb, pw, ps)


def _ffn_up(x1, g2, wup, wf, fb, wdown_b, tile_rows):
    seq = x1.shape[0]
    tr = tile_rows
    n = seq // tr

    def body(x1_ref, g2_ref, wup_hbm, wf_ref, fb_ref, wdown_b_hbm,
             h2_ref, up_ref, gc_ref, act_ref, wdown_f, wup_v, gbuf, sem, *gsems):
        i = pl.program_id(0)

        def gather():
            return _gather_ops((wdown_b_hbm,), (wdown_f,), (False,), gsems[:6], gsems[6:])

        @pl.when(i == 0)
        def _():
            _handshake(SIBLING_AND_CHIPS)
            gather()[0]()
            _load_weights(((wup_hbm, wup_v),), sem)
            gbuf[0:8, :] = jnp.zeros((8, D_FF), F32)

        x1v = x1_ref[...]
        r2 = lax.rsqrt(_rowmean(x1v * x1v) + EPS)
        h2 = (x1v * r2 * g2_ref[...]).astype(BF16)
        h2_ref[...] = h2

        def up_proj(j):
            return (_dot(h2, wup_v[:, j * UP_CHUNK:(j + 1) * UP_CHUNK]),
                    _dot(h2, wup_v[:, D_FF + j * UP_CHUNK:D_FF + (j + 1) * UP_CHUNK]))

        ahead = up_proj(0)
        for j in range(D_FF // UP_CHUNK):
            cs = slice(j * UP_CHUNK, (j + 1) * UP_CHUNK)
            vs = slice(D_FF + j * UP_CHUNK, D_FF + (j + 1) * UP_CHUNK)
            gate, val = ahead
            if j + 1 < D_FF // UP_CHUNK:
                ahead = up_proj(j + 1)
            up_ref[:, cs] = gate.astype(BF16)
            up_ref[:, vs] = val.astype(BF16)
            gbuf[8:8 + tr, cs] = gate
            gc = (wf_ref[0:1, cs] * gbuf[6:6 + tr, cs] + wf_ref[1:2, cs] * gbuf[7:7 + tr, cs]
                  + wf_ref[2:3, cs] * gate + fb_ref[:, cs])
            gbuf[0:8, cs] = gbuf[tr:tr + 8, cs]
            gc_ref[:, cs] = gc.astype(BF16)
            act_ref[:, cs] = (gc * _sigmoid(gc) * val).astype(BF16)

        @pl.when(i == max(n - 2, 0))
        def _():
            gather()[1]()

        @pl.when(i == n - 1)
        def _():
            gather()[2]()

    tile = lambda w: pl.BlockSpec((tr, w), lambda i: (i, 0))
    full = lambda a: pl.BlockSpec(a.shape, lambda i: (0,) * a.ndim)
    return pl.pallas_call(
        body, name="ffn_up", grid=(n,),
        in_specs=[tile(D_MODEL), full(g2), ANY, full(wf), full(fb), ANY],
        out_specs=[tile(D_MODEL), tile(2 * D_FF), tile(D_FF), tile(D_FF), ANY],
        out_shape=[
            jax.ShapeDtypeStruct((seq, D_MODEL), BF16), jax.ShapeDtypeStruct((seq, 2 * D_FF), BF16),
            jax.ShapeDtypeStruct((seq, D_FF), BF16), jax.ShapeDtypeStruct((seq, D_FF), BF16),
            jax.ShapeDtypeStruct((D_FF, D_MODEL), BF16),
        ],
        scratch_shapes=[pltpu.VMEM(wup.shape, BF16), pltpu.VMEM((tr + 8, D_FF), F32), pltpu.SemaphoreType.DMA((1,))]
        + _gather_scratch((wdown_b,)),
        compiler_params=pltpu.CompilerParams(dimension_semantics=("arbitrary",), vmem_limit_bytes=VMEM_LIMIT,
                                             collective_id=SIBLING_AND_CHIPS),
    )(x1, g2, wup, wf, fb, wdown_b)


def _ffn_down(x1, act, wdown, g3, target, tile_rows):
    seq = x1.shape[0]
    tr = tile_rows
    n = seq // tr

    def body(x1_ref, act_ref, wdown_hbm, g3_ref, t_ref, dx2b_ref, sm_ref, wdown_v, sem):
        i = pl.program_id(0)

        @pl.when(i == 0)
        def _():
            _load_weights(((wdown_hbm, wdown_v),), sem)
            sm_ref[...] = jnp.zeros(sm_ref.shape, F32)

        x2 = x1_ref[...] + _dot(act_ref[...], wdown_v[...])
        r3 = lax.rsqrt(_rowmean(x2 * x2) + EPS)
        n3 = x2 * r3
        err = n3 * g3_ref[...] - t_ref[...]
        dy = err / D_MODEL
        sm_ref[2:3, :] += _colsum(dy * n3)
        loss = 0.5 * _colsum(_rowmean(err * err))
        sm_ref[3:4, :] += jnp.broadcast_to(loss, (1, D_MODEL))
        dn = dy * g3_ref[...]
        dx2b_ref[...] = (r3 * (dn - n3 * _rowmean(dn * n3))).astype(BF16)

    tile = lambda w: pl.BlockSpec((tr, w), lambda i: (i, 0))
    full = lambda a: pl.BlockSpec(a.shape, lambda i: (0,) * a.ndim)
    return pl.pallas_call(
        body, name="ffn_down", grid=(n,),
        in_specs=[tile(D_MODEL), tile(D_FF), ANY, full(g3), tile(D_MODEL)],
        out_specs=[tile(D_MODEL), pl.BlockSpec((8, D_MODEL), lambda i: (0, 0))],
        out_shape=[jax.ShapeDtypeStruct((seq, D_MODEL), BF16), jax.ShapeDtypeStruct((8, D_MODEL), F32)],
        scratch_shapes=[pltpu.VMEM(wdown.shape, BF16), pltpu.SemaphoreType.DMA((1,))],
        compiler_params=pltpu.CompilerParams(dimension_semantics=("arbitrary",), vmem_limit_bytes=VMEM_LIMIT),
    )(x1, act, wdown, g3, target)


def _ffn_bwd(dx2, up, gcs, x1, g2, wup, wf, wdown, comm, tile_rows):
    seq = x1.shape[0]
    c_ins, c_shapes, c_sems, c_ops, c_id = _comm_plan(comm)
    nc = len(c_ins)
    tr = tile_rows
    n = seq // tr

    def body(dx2_ref, up_ref, gc_ref, x1_ref, g2_ref, wup_hbm, wf_ref, wdown_hbm, *rest):
        c_in, rest = rest[:nc], rest[nc:]
        dup_ref, dx1b_ref, sm_ref, sf_ref = rest[:4]
        c_out, rest = rest[4:4 + nc], rest[4 + nc:]
        wup_v, wdown_v, dbuf, dcar, sem = rest[:5]
        c_sem_refs = rest[5:]
        i = pl.program_id(0)

        @pl.when(i == 0)
        def _():
            c_ops(c_in, c_out, c_sem_refs)[0]()
            _load_weights(((wup_hbm, wup_v), (wdown_hbm, wdown_v)), sem)
            dcar[...] = jnp.zeros(dcar.shape, F32)
            sm_ref[...] = jnp.zeros(sm_ref.shape, F32)
            sf_ref[...] = jnp.zeros(sf_ref.shape, F32)

        dx2b = dx2_ref[...]
        dx2v = dx2b.astype(F32)
        dh2 = jnp.zeros((tr, D_MODEL), F32)

        def down_t(j):
            return _dot_nt(dx2b, wdown_v[j * FF_CHUNK:(j + 1) * FF_CHUNK, :])

        ahead = down_t(0)
        for j in range(N_FF_CHUNKS):
            cs = slice(j * FF_CHUNK, (j + 1) * FF_CHUNK)
            vs = slice(D_FF + j * FF_CHUNK, D_FF + (j + 1) * FF_CHUNK)
            dact = ahead
            if j + 1 < N_FF_CHUNKS:
                ahead = down_t(j + 1)
            gate = up_ref[:, cs].astype(F32)
            val = up_ref[:, vs].astype(F32)
            gc = gc_ref[:, cs].astype(F32)
            sg = _sigmoid(gc)
            dval = dact * (gc * sg)
            dgc = dact * val * (sg * (1.0 + gc * (1.0 - sg)))
            dbuf[0:tr, :] = dgc
            dbuf[tr:tr + 8, :] = dcar[:, cs]
            d_p1 = dbuf[1:1 + tr, :]
            d_p2 = dbuf[2:2 + tr, :]
            dgate = wf_ref[2:3, cs] * dgc + wf_ref[1:2, cs] * d_p1 + wf_ref[0:1, cs] * d_p2
            dcar[:, cs] = dgc[0:8, :]
            sf_ref[0:1, cs] += _colsum(d_p2 * gate)
            sf_ref[1:2, cs] += _colsum(d_p1 * gate)
            sf_ref[2:3, cs] += _colsum(dgc * gate)
            sf_ref[3:4, cs] += _colsum(dgc)
            dgb, dvb = dgate.astype(BF16), dval.astype(BF16)
            dup_ref[:, cs] = dgb
            dup_ref[:, vs] = dvb
            dh2 = dh2 + _dot_nt(dgb, wup_v[:, cs]) + _dot_nt(dvb, wup_v[:, vs])
        x1v = x1_ref[...]
        r2 = lax.rsqrt(_rowmean(x1v * x1v) + EPS)
        n2 = x1v * r2
        sm_ref[1:2, :] += _colsum(dh2 * n2)
        dn2 = dh2 * g2_ref[...]
        dx1b_ref[...] = (dx2v + r2 * (dn2 - n2 * _rowmean(dn2 * n2))).astype(BF16)

        @pl.when(i == n - 1)
        def _():
            c_ops(c_in, c_out, c_sem_refs)[2]()

    tile = lambda w: pl.BlockSpec((tr, w), lambda i: (n - 1 - i, 0))
    full = lambda a: pl.BlockSpec(a.shape, lambda i: (0,) * a.ndim)
    acc = lambda rows, w: pl.BlockSpec((rows, w), lambda i: (0, 0))
    return pl.pallas_call(
        body, name="ffn_bwd", grid=(n,),
        in_specs=[tile(D_MODEL), tile(2 * D_FF), tile(D_FF), tile(D_MODEL), full(g2), ANY, full(wf), ANY] + [ANY] * nc,
        out_specs=[tile(2 * D_FF), tile(D_MODEL), acc(8, D_MODEL), acc(8, D_FF)] + [ANY] * nc,
        out_shape=[
            jax.ShapeDtypeStruct((seq, 2 * D_FF), BF16), jax.ShapeDtypeStruct((seq, D_MODEL), BF16),
            jax.ShapeDtypeStruct((8, D_MODEL), F32), jax.ShapeDtypeStruct((8, D_FF), F32),
        ] + c_shapes,
        scratch_shapes=[
            pltpu.VMEM(wup.shape, BF16), pltpu.VMEM(wdown.shape, BF16),
            pltpu.VMEM((tr + 8, FF_CHUNK), F32), pltpu.VMEM((8, D_FF), F32), pltpu.SemaphoreType.DMA((2,)),
        ] + c_sems,
        compiler_params=pltpu.CompilerParams(dimension_semantics=("arbitrary",), vmem_limit_bytes=VMEM_LIMIT,
                                             collective_id=c_id),
    )(dx2, up, gcs, x1, g2, wup, wf, wdown, *c_ins)


def _mixer_bwd(dx1, x, proj, cpre, d, g1, win, wa, lg, lb, pw, ps, wout, parts, tile_rows):
    seq = x.shape[0]
    n_parts = len(parts)
    tr = tile_rows
    n = seq // tr
    row_cb, row_lg, row_lb, row_ps = 32, 33, 34, 35

    def body(dx1_ref, x_ref, proj_ref, projh_ref, c_ref, d_ref, g1_ref, win_hbm, wa_ref, lg_ref, lb_ref, pw_ref, ps_ref,
             wout_hbm, *rest):
        part_refs, rest = rest[:n_parts], rest[n_parts:]
        dproj_ref, gx_ref, sm_ref, s5_ref, sp_ref = rest[:5]
        land_refs, rest = rest[5:5 + n_parts], rest[5 + n_parts:]
        win_v, wout_v, ubuf, ushift, dcbuf, dshift, ebuf, sem = rest[:8]
        ssems = rest[8:]
        i = pl.program_id(0)
        tile = n - 1 - i

        def scatter():
            return _scatter_ops(part_refs, land_refs, n_parts, ssems[:6], ssems[6:])

        @pl.when(i == 0)
        def _():
            _handshake(SIBLING_AND_CHIPS)
            scatter()[0]()
            _load_weights(((win_hbm, win_v), (wout_hbm, wout_v)), sem)
            dcbuf[tr:tr + A_HALO, :] = jnp.zeros((A_HALO, D_CONV), F32)
            ebuf[tr:tr + P_HALO, :] = jnp.zeros((P_HALO, D_POOL), F32)
            sm_ref[...] = jnp.zeros(sm_ref.shape, F32)
            s5_ref[...] = jnp.zeros(s5_ref.shape, F32)
            sp_ref[...] = jnp.zeros(sp_ref.shape, F32)

        dx1b = dx1_ref[...]
        dx1v = dx1b.astype(F32)
        dm = _dot_nt(dx1b, wout_v[...])
        dya, dyb = dm[:, :D_CONV], dm[:, D_CONV:]
        dbis = []
        for g, w in enumerate(POOL_WINDOWS):
            cols = slice(g * POOL_GROUP, (g + 1) * POOL_GROUP)
            dgb = d_ref[:, cols]
            pwb = pw_ref[g].astype(BF16)
            dyg = dyb[:, cols]
            s5_ref[row_ps:row_ps + 1, cols] += _colsum(dyg * _dot(dgb, pwb))
            dqb = (dyg * ps_ref[:, cols]).astype(BF16)
            sp_ref[g] += _dot_tn(dgb, dqb)
            dd = _dot_nt(dqb, pwb)
            e = dd / _pool_count(tile, tr, w)
            ebuf[0:tr, cols] = e
            s = e
            for kk in range(1, w):
                s = s + ebuf[kk:kk + tr, cols]
            dbis.append(s - dd)
        ebuf[tr:tr + P_HALO, :] = ebuf[0:P_HALO, :]
        cv = c_ref[...].astype(F32)
        xc = cv - _rowmean(cv)
        rs = lax.rsqrt(_rowmean(xc * xc) + EPS)
        z = xc * rs
        ln = z * lg_ref[...] + lb_ref[...]
        sl = _sigmoid(ln)
        dl = dya * (sl * (1.0 + ln * (1.0 - sl)))
        s5_ref[row_lg:row_lg + 1, :] += _colsum(dl * z)
        s5_ref[row_lb:row_lb + 1, :] += _colsum(dl)
        dz = dl * lg_ref[...]
        dc = rs * (dz - _rowmean(dz) - z * _rowmean(dz * z))
        s5_ref[row_cb:row_cb + 1, :] += _colsum(dc)
        dcbuf[0:tr, :] = dc
        keep = (tile > 0).astype(F32)
        avh = projh_ref[:, :D_CONV].astype(F32)
        agh = projh_ref[:, D_CONV:].astype(F32)
        ubuf[0:A_HALO, :] = avh * _sigmoid(agh) * keep
        av = proj_ref[:, :D_CONV].astype(F32)
        ag = proj_ref[:, D_CONV:2 * D_CONV].astype(F32)
        sg = _sigmoid(ag)
        ubuf[A_HALO:A_HALO + tr, :] = av * sg
        off = A_HALO - (CONV_A - 1)
        du = wa_ref[CONV_A - 1:CONV_A, :] * dc
        dview = _shifted_views(dcbuf, dshift, tr)
        uview = _shifted_views(ubuf, ushift, tr)
        for j in range(CONV_A - 1):
            du = du + wa_ref[j:j + 1, :] * dview(CONV_A - 1 - j)
        for j in range(CONV_A):
            s5_ref[j:j + 1, :] += _colsum(dc * uview(off + j))
        dcbuf[tr:tr + A_HALO, :] = dcbuf[0:A_HALO, :]
        dav = du * sg
        dag = du * av * (sg * (1.0 - sg))
        dprojb = jnp.concatenate([dav, dag] + dbis, axis=1).astype(BF16)
        dproj_ref[...] = dprojb
        dh1 = _dot_nt(dprojb, win_v[...])
        xv = x_ref[...]
        r1 = lax.rsqrt(_rowmean(xv * xv) + EPS)
        n1 = xv * r1
        sm_ref[0:1, :] += _colsum(dh1 * n1)
        dn1 = dh1 * g1_ref[...]
        gx_ref[...] = dx1v + r1 * (dn1 - n1 * _rowmean(dn1 * n1))

        @pl.when(i == max(n - 2, 0))
        def _():
            scatter()[1]()

        @pl.when(i == n - 1)
        def _():
            scatter()[2]()

    tile = lambda w: pl.BlockSpec((tr, w), lambda i: (n - 1 - i, 0))
    full = lambda a: pl.BlockSpec(a.shape, lambda i: (0,) * a.ndim)
    halo = pl.BlockSpec((A_HALO, 2 * D_CONV), lambda i: (jnp.maximum((n - 1 - i) * (tr // A_HALO) - 1, 0), 0))
    acc = lambda shape: pl.BlockSpec(shape, lambda i: (0,) * len(shape))
    return pl.pallas_call(
        body, name="mixer_bwd", grid=(n,),
        in_specs=[tile(D_MODEL), tile(D_MODEL), tile(D_IN), halo, tile(D_CONV), tile(D_POOL), full(g1), ANY, full(wa),
                  full(lg), full(lb), full(pw), full(ps), ANY] + [ANY] * n_parts,
        out_specs=[tile(D_IN), tile(D_MODEL), acc((8, D_MODEL)), acc((40, D_CONV)), acc(pw.shape)] + [ANY] * n_parts,
        out_shape=[
            jax.ShapeDtypeStruct((seq, D_IN), BF16), jax.ShapeDtypeStruct((seq, D_MODEL), F32),
            jax.ShapeDtypeStruct((8, D_MODEL), F32), jax.ShapeDtypeStruct((40, D_CONV), F32),
            jax.ShapeDtypeStruct(pw.shape, F32),
        ] + _scatter_shapes(parts, ()),
        scratch_shapes=[
            pltpu.VMEM(win.shape, BF16), pltpu.VMEM(wout.shape, BF16),
            pltpu.VMEM((tr + A_HALO, D_CONV), F32), pltpu.VMEM((7, tr + A_HALO - 8, D_CONV), F32),
            pltpu.VMEM((tr + A_HALO, D_CONV), F32), pltpu.VMEM((7, tr + A_HALO - 8, D_CONV), F32),
            pltpu.VMEM((tr + P_HALO, D_POOL), F32), pltpu.SemaphoreType.DMA((2,)),
        ] + _scatter_scratch(parts, ()),
        compiler_params=pltpu.CompilerParams(dimension_semantics=("arbitrary",), vmem_limit_bytes=VMEM_LIMIT,
                                             collective_id=SIBLING_AND_CHIPS),
    )(dx1, x, proj, proj, cpre, d, g1, win, wa, lg, lb, pw, ps, wout, *parts)


def _weight_grad(a, b, layout, k_rows, comm=None, carry=None):
    seq, m_dim = a.shape
    n_dim = b.shape[1]
    steps = seq // k_rows

    def store(o_ref, acc, index, value):
        if steps == 1:
            o_ref[index] = value.astype(BF16)
            return
        s = pl.program_id(1)

        @pl.when(s == 0)
        def _():
            acc[index] = value

        @pl.when(jnp.logical_and(s > 0, s < steps - 1))
        def _():
            acc[index] += value

        @pl.when(s == steps - 1)
        def _():
            o_ref[index] = (acc[index] + value).astype(BF16)

    if layout in ("rows1", "rows2"):
        groups = int(layout[-1])
        per_tile = N_CHIPS // groups
        rows = m_dim // N_CHIPS // 2
        a_w = m_dim // groups

        def body(a_ref, b_ref, o_ref, acc):
            r = _dot_tn(a_ref[...], b_ref[...])
            for p in range(per_tile):
                for h in range(2):
                    store(o_ref, acc, (h, p), r[(2 * p + h) * rows:(2 * p + h + 1) * rows, :])

        in_specs = [pl.BlockSpec((k_rows, a_w), lambda g, s: (s, g)), pl.BlockSpec((k_rows, n_dim), lambda g, s: (s, 0))]
        out_spec = pl.BlockSpec((2, per_tile, rows, n_dim), lambda g, s: (0, g, 0, 0))
        out_dims, acc_dims = (2, N_CHIPS, rows, n_dim), (2, per_tile, rows, n_dim)
    elif layout == "cols_chip":
        groups = N_CHIPS
        rows, cols = m_dim // 2, n_dim // N_CHIPS

        def body(a_ref, b_ref, o_ref, acc):
            r = _dot_tn(a_ref[...], b_ref[...])
            for h in range(2):
                store(o_ref, acc, h, r[h * rows:(h + 1) * rows, :])

        in_specs = [pl.BlockSpec((k_rows, m_dim), lambda g, s: (s, 0)), pl.BlockSpec((k_rows, cols), lambda g, s: (s, g))]
        out_spec = pl.BlockSpec((2, None, rows, cols), lambda g, s: (0, g, 0, 0))
        out_dims, acc_dims = (2, N_CHIPS, rows, cols), (2, rows, cols)
    else:
        groups = 2
        rows, cols = m_dim // 2, n_dim // N_CHIPS

        def body(a_ref, b_ref, o_ref, acc):
            r = _dot_tn(a_ref[...], b_ref[...])
            for k in range(N_CHIPS):
                store(o_ref, acc, k, r[:, k * cols:(k + 1) * cols])

        in_specs = [pl.BlockSpec((k_rows, rows), lambda g, s: (s, g)), pl.BlockSpec((k_rows, n_dim), lambda g, s: (s, 0))]
        out_spec = pl.BlockSpec((None, N_CHIPS, rows, cols), lambda g, s: (g, 0, 0, 0))
        out_dims, acc_dims = (2, N_CHIPS, rows, cols), (N_CHIPS, rows, cols)

    c_ins, c_shapes, c_sems, c_ops, c_id = _comm_plan(comm)
    nc = len(c_ins)
    c_specs = [ANY] * nc
    if carry is not None:
        assert comm is None and carry.shape[0] % (groups * steps) == 0
        carry_spec = pl.BlockSpec((carry.shape[0] // (groups * steps), carry.shape[1]), lambda g, s: (g * steps + s, 0))
        c_ins, c_shapes, c_specs, nc = (carry,), [jax.ShapeDtypeStruct(carry.shape, carry.dtype)], [carry_spec], 1

    def hosted(a_ref, b_ref, *rest):
        c_in, o_ref, c_out, acc, sems = rest[:nc], rest[nc], rest[nc + 1:2 * nc + 1], rest[2 * nc + 1], rest[2 * nc + 2:]
        g, s = pl.program_id(0), pl.program_id(1)
        if carry is not None:
            c_out[0][...] = c_in[0][...]
            body(a_ref, b_ref, o_ref, acc)
            return
        if nc:
            @pl.when(jnp.logical_and(g == 0, s == 0))
            def _():
                c_ops(c_in, c_out, sems)[0]()

        body(a_ref, b_ref, o_ref, acc)
        if nc:
            step = g * steps + s

            @pl.when(step == max(groups * steps - 2, 0))
            def _():
                c_ops(c_in, c_out, sems)[1]()

            @pl.when(step == groups * steps - 1)
            def _():
                c_ops(c_in, c_out, sems)[2]()

    outs = pl.pallas_call(
        hosted, name=f"weight_grad_{layout}_{m_dim}x{n_dim}", grid=(groups, steps),
        in_specs=in_specs + c_specs, out_specs=[out_spec] + c_specs,
        out_shape=[jax.ShapeDtypeStruct(out_dims, BF16)] + c_shapes,
        scratch_shapes=[pltpu.VMEM(acc_dims, F32)] + c_sems,
        compiler_params=pltpu.CompilerParams(dimension_semantics=("arbitrary", "arbitrary"), vmem_limit_bytes=VMEM_LIMIT,
                                             collective_id=c_id),
    )(a, b, *c_ins)
    return outs if nc else outs[0]


def _exchange_ops(ins, outs, n_big, sems):
    send, recv = sems
    x, y, c, _, _ = _place()
    cps = [pltpu.make_async_remote_copy(
        src_ref=ins[t].at[1 - c] if t < n_big else ins[t], dst_ref=outs[t], send_sem=send.at[t], recv_sem=recv.at[t],
        device_id=(x, y, 1 - c), device_id_type=MESH) for t in range(len(ins))]

    def start():
        for cp in cps:
            cp.start()

    def finish():
        for cp in cps:
            cp.wait()

    return start, finish


def _exchange_shapes(bigs, smalls):
    return [jax.ShapeDtypeStruct((N_CHIPS,) + b.shape[2:], b.dtype) for b in bigs] + [
        jax.ShapeDtypeStruct(s.shape, s.dtype) for s in smalls]


def _comm_plan(comm):
    if comm is None:
        return (), [], [], None, None
    kind, arrays = comm
    n = len(arrays)

    def scatter(i, o, sm):
        start, land, finish = _scatter_ops(i, o, n, sm[:6], sm[6:])
        return lambda: (_handshake(SIBLING_AND_CHIPS), start()), land, finish

    def exchange(i, o, sm):
        start, finish = _exchange_ops(i, o, n, sm)
        return lambda: (_handshake(SIBLING_ONLY), start()), lambda: None, finish

    if kind == "scatter":
        return tuple(arrays), _scatter_shapes(arrays, ()), _scatter_scratch(arrays, ()), scatter, SIBLING_AND_CHIPS
    return tuple(arrays), _exchange_shapes(arrays, ()), [pltpu.SemaphoreType.DMA((n,))] * 2, exchange, SIBLING_ONLY


def _sibling_exchange(bigs, smalls, tag):
    nb, nt = len(bigs), len(bigs) + len(smalls)

    def body(*refs):
        start, finish = _exchange_ops(refs[:nt], refs[nt:2 * nt], nb, refs[2 * nt:])
        _handshake(SIBLING_ONLY)
        start()
        finish()

    return pl.pallas_call(
        body, name=f"sibling_exchange_{tag}", out_shape=_exchange_shapes(bigs, smalls),
        in_specs=[ANY] * nt, out_specs=[ANY] * nt,
        scratch_shapes=[pltpu.SemaphoreType.DMA((nt,)), pltpu.SemaphoreType.DMA((nt,))],
        compiler_params=pltpu.CompilerParams(collective_id=SIBLING_ONLY),
    )(*bigs, *smalls)


def _pair_sum(core, mine, theirs, tag, block_rows):
    _, _, rows, cols = mine.shape
    steps = rows // block_rows

    def body(core_ref, a_ref, b_ref, o_ref):
        o_ref[...] = (a_ref[...].astype(F32) + b_ref[...].astype(F32)).astype(BF16)

    grid_spec = pltpu.PrefetchScalarGridSpec(
        num_scalar_prefetch=1, grid=(N_CHIPS, steps),
        in_specs=[pl.BlockSpec((None, None, block_rows, cols), lambda k, r, core_ref: (core_ref[0], k, r, 0)),
                  pl.BlockSpec((None, block_rows, cols), lambda k, r, core_ref: (k, r, 0))],
        out_specs=pl.BlockSpec((None, block_rows, cols), lambda k, r, core_ref: (k, r, 0)),
    )
    return pl.pallas_call(
        body, name=f"pair_sum_{tag}", grid_spec=grid_spec,
        out_shape=jax.ShapeDtypeStruct((N_CHIPS, rows, cols), BF16),
        compiler_params=pltpu.CompilerParams(dimension_semantics=("arbitrary", "arbitrary"), vmem_limit_bytes=VMEM_LIMIT),
    )(core, mine, theirs)


def _pair_sum_small(mine, theirs):
    (m_f2, m_b1, m_b2, m_sf, m_s5, m_sp) = mine

    def body(a0, a1, a2, a3, a4, a5, b0, b1, b2, b3, b4, b5, o_m, o_f, o_5, o_p):
        sm = (a0[...] + a1[...] + a2[...]) + (b0[...] + b1[...] + b2[...])
        sf = a3[...] + b3[...]
        s5 = a4[...] + b4[...]
        for h in range(2):
            o_m[h] = sm[:, h * (D_MODEL // 2):(h + 1) * (D_MODEL // 2)]
            o_f[h] = sf[:, h * (D_FF // 2):(h + 1) * (D_FF // 2)]
            o_5[h] = s5[:, h * (D_CONV // 2):(h + 1) * (D_CONV // 2)]
            for g in range(2):
                o_p[h, g] = a5[2 * h + g] + b5[2 * h + g]

    out_shape = [
        jax.ShapeDtypeStruct((2, 8, D_MODEL // 2), F32), jax.ShapeDtypeStruct((2, 8, D_FF // 2), F32),
        jax.ShapeDtypeStruct((2, 40, D_CONV // 2), F32), jax.ShapeDtypeStruct((2, 2, POOL_GROUP, POOL_GROUP), F32),
    ]
    return pl.pallas_call(body, name="pair_sum_small", out_shape=out_shape, in_specs=[VMEM] * 12, out_specs=[VMEM] * 4)(
        *mine, *theirs)


def _scatter_ops(ins, outs, n_parts, sems, stages, landed=False):
    ici_send, ici_recv, fwd_send, fwd_recv, loc_in, loc_out = sems
    nt = len(ins)
    x, y, c, k, chips = _place()

    def src_of(t, kk):
        return ins[t].at[kk] if t < n_parts else ins[t].at[c]

    def ici(t, j, kk, slot):
        return pltpu.make_async_remote_copy(
            src_ref=src_of(t, kk), dst_ref=outs[t].at[c, slot], send_sem=ici_send.at[t * 3 + j],
            recv_sem=ici_recv.at[t * 3 + j], device_id=(*chips[j], c), device_id_type=MESH)

    def fwd(t, half):
        slots = outs[t].at[half]
        return pltpu.make_async_remote_copy(
            src_ref=slots, dst_ref=slots, send_sem=fwd_send.at[t], recv_sem=fwd_recv.at[t],
            device_id=(x, y, 1 - c), device_id_type=MESH)

    local = [_staged(src_of(t, k), outs[t].at[c, k], stages[t], loc_in.at[t], loc_out.at[t]) for t in range(nt)]
    peers = [(t, j, 2 * qx + qy) for t in range(nt) for j, (qx, qy) in enumerate(chips)]
    sends = [] if landed else [ici(t, j, kq, k) for t, j, kq in peers]

    def start():
        for cp in local:
            cp[0]()
        for cp in sends:
            cp.start()

    def land():
        for cp in local:
            cp[1]()
        if not landed:
            for t, j, kq in peers:
                ici(t, j, kq, kq).wait_recv()
        for cp in local:
            cp[2]()
        for t in range(nt):
            fwd(t, c).start()

    def finish():
        for t in range(nt):
            fwd(t, 1 - c).wait_recv()
            fwd(t, c).wait_send()
        for cp in sends:
            cp.wait_send()

    return start, land, finish


def _scatter_scratch(parts, smalls):
    arrays = tuple(parts) + tuple(smalls)
    nt = len(arrays)
    return ([pltpu.SemaphoreType.DMA((3 * nt,))] * 2 + [pltpu.SemaphoreType.DMA((nt,))] * 4
            + [pltpu.VMEM(a.shape[1:], a.dtype) for a in arrays])


def _scatter_shapes(parts, smalls):
    return [jax.ShapeDtypeStruct((2, N_CHIPS) + p.shape[1:], p.dtype) for p in tuple(parts) + tuple(smalls)]


HBM_SPEC = pl.BlockSpec(memory_space=pltpu.HBM)
SEM_SPEC = pl.BlockSpec(memory_space=pltpu.SEMAPHORE)
EFFECT = pltpu.SideEffectType.DATAFLOW_SIDE_EFFECTING


def _ici_copy(ins, lands, n_parts, send, recv, t, j):
    _, _, c, k, chips = _place()
    qx, qy = chips[j]
    src = ins[t].at[2 * qx + qy] if t < n_parts else ins[t].at[c]
    return pltpu.make_async_remote_copy(
        src_ref=src, dst_ref=lands[t].at[c, k], send_sem=send.at[t * 3 + j], recv_sem=recv.at[t * 3 + j],
        device_id=(qx, qy, c), device_id_type=MESH)


def _scatter_start(parts, smalls):
    arrays = tuple(parts) + tuple(smalls)
    nt = len(arrays)

    def body(*refs):
        ins, lands = refs[:nt], refs[nt:2 * nt]
        send, recv = refs[2 * nt], refs[2 * nt + 1]
        token = refs[-1]
        for t in range(nt):
            for j in range(3):
                _ici_copy(ins, lands, len(parts), send, recv, t, j).start()
        token[...] = jnp.zeros(token.shape, F32)

    land_shapes = _scatter_shapes(parts, smalls)
    out_shape = ([pltpu.SemaphoreType.DMA((3 * nt,))] * 2 + [pltpu.HBM(a.shape, a.dtype) for a in arrays]
                 + [pltpu.HBM(a.shape, a.dtype) for a in land_shapes] + [jax.ShapeDtypeStruct((8, 128), F32)])
    operands = [pltpu.with_memory_space_constraint(a, pltpu.HBM) for a in arrays]
    operands += [pltpu.with_memory_space_constraint(lax.empty(a.shape, a.dtype), pltpu.HBM) for a in land_shapes]
    outs = pl.pallas_call(
        body, name="scatter_start", out_shape=out_shape, in_specs=[HBM_SPEC] * (2 * nt),
        out_specs=[SEM_SPEC] * 2 + [HBM_SPEC] * (2 * nt) + [VMEM],
        input_output_aliases={i: 2 + i for i in range(2 * nt)},
        compiler_params=pltpu.CompilerParams(has_side_effects=EFFECT),
    )(*operands)
    return outs[0], outs[1], outs[2:2 + nt], outs[2 + nt:2 + 2 * nt], outs[-1]


def _scatter_wait(send, recv, ins, lands, n_parts, after):
    nt = len(ins)

    def body(*refs):
        in_refs, land_refs = refs[:nt], refs[nt:2 * nt]
        send_ref, recv_ref = refs[2 * nt], refs[2 * nt + 1]
        for t in range(nt):
            for j in range(3):
                cp = _ici_copy(in_refs, land_refs, n_parts, send_ref, recv_ref, t, j)
                cp.wait_send()
                cp.wait_recv()

    outs = pl.pallas_call(
        body, name="scatter_wait", out_shape=[pltpu.HBM(a.shape, a.dtype) for a in tuple(ins) + tuple(lands)],
        in_specs=[HBM_SPEC] * (2 * nt) + [SEM_SPEC] * 2 + [ANY] * len(after), out_specs=[HBM_SPEC] * (2 * nt),
        input_output_aliases={i: i for i in range(2 * nt)},
        compiler_params=pltpu.CompilerParams(has_side_effects=EFFECT),
    )(*ins, *lands, send, recv, *after)
    return outs[:nt], outs[nt:]


def _scatter_forward(ins, lands, n_parts):
    nt = len(ins)

    def body(*refs):
        start, land, finish = _scatter_ops(
            refs[:nt], refs[2 * nt:3 * nt], n_parts, refs[3 * nt:3 * nt + 6], refs[3 * nt + 6:], landed=True)
        _handshake(SIBLING_ONLY)
        start()
        land()
        finish()

    return pl.pallas_call(
        body, name="scatter_forward", out_shape=[jax.ShapeDtypeStruct(a.shape, a.dtype) for a in lands],
        in_specs=[ANY] * (2 * nt), out_specs=[ANY] * nt, input_output_aliases={nt + i: i for i in range(nt)},
        scratch_shapes=_scatter_scratch(ins[:n_parts], ins[n_parts:]),
        compiler_params=pltpu.CompilerParams(collective_id=SIBLING_ONLY),
    )(*ins, *lands)


def _chip_scatter(parts, smalls):
    nt = len(parts) + len(smalls)

    def body(*refs):
        start, land, finish = _scatter_ops(refs[:nt], refs[nt:2 * nt], len(parts), refs[2 * nt:2 * nt + 6], refs[2 * nt + 6:])
        _handshake(SIBLING_AND_CHIPS)
        start()
        land()
        finish()

    return pl.pallas_call(
        body, name="chip_scatter", out_shape=_scatter_shapes(parts, smalls), in_specs=[ANY] * nt, out_specs=[ANY] * nt,
        scratch_shapes=_scatter_scratch(parts, smalls),
        compiler_params=pltpu.CompilerParams(collective_id=SIBLING_AND_CHIPS),
    )(*parts, *smalls)


def _adamw(w, g, m, v):
    m = ADAM_B1 * m + (1.0 - ADAM_B1) * g
    v = ADAM_B2 * v + (1.0 - ADAM_B2) * (g * g)
    m_hat = m / (1.0 - ADAM_B1 ** ADAM_STEP)
    v_hat = v / (1.0 - ADAM_B2 ** ADAM_STEP)
    delta = -ADAM_LR * (m_hat / (jnp.sqrt(v_hat) + ADAM_EPS) + ADAM_WD * w)
    return delta, m, v


def _adam_big(parts, w, m, v, tag, block_rows, token):
    _, _, rows, cols = parts.shape
    steps = rows // block_rows

    def body(p_ref, w_ref, m_ref, v_ref, token_ref, g_out, d_out, m_out, v_out):
        g = p_ref[0].astype(F32)
        for q in range(1, N_CHIPS):
            g = g + p_ref[q].astype(F32)
        delta, m_new, v_new = _adamw(w_ref[...], g, m_ref[...], v_ref[...])
        g_out[...] = g
        d_out[...] = delta
        m_out[...] = m_new
        v_out[...] = v_new

    blk = pl.BlockSpec((block_rows, cols), lambda h, r: (h * steps + r, 0))
    return pl.pallas_call(
        body, name=f"adam_{tag}", grid=(2, steps),
        in_specs=[pl.BlockSpec((None, N_CHIPS, block_rows, cols), lambda h, r: (h, 0, r, 0)), blk, blk, blk, ANY],
        out_specs=[blk] * 4, out_shape=[jax.ShapeDtypeStruct(w.shape, F32)] * 4,
        compiler_params=pltpu.CompilerParams(dimension_semantics=("arbitrary", "arbitrary"), vmem_limit_bytes=VMEM_LIMIT),
    )(parts, w, m, v, token)


def _reduce_small(l_m, l_f, l_5, l_p):
    def total(ref):
        t = ref[:, 0]
        for q in range(1, N_CHIPS):
            t = t + ref[:, q]
        return t

    def body(m_ref, f_ref, s_ref, p_ref, g1_o, g2_o, g3_o, loss_o, wf_o, fb_o, wa_o, cb_o, lg_o, lb_o, ps_o, pw_o):
        tm, tf, t5, tp = total(m_ref), total(f_ref), total(s_ref), total(p_ref)
        sm = jnp.concatenate([tm[0], tm[1]], axis=1)
        sf = jnp.concatenate([tf[0], tf[1]], axis=1)
        s5 = jnp.concatenate([t5[0], t5[1]], axis=1)
        g1_o[...] = sm[0:1]
        g2_o[...] = sm[1:2]
        g3_o[...] = sm[2:3]
        loss_o[...] = sm[3:4, 0:128]
        wf_o[...] = sf
        fb_o[...] = sf[3:4]
        wa_o[...] = s5[0:32]
        cb_o[...] = s5[32:33]
        lg_o[...] = s5[33:34]
        lb_o[...] = s5[34:35]
        ps_o[...] = s5[35:36]
        for h in range(2):
            for g in range(2):
                pw_o[2 * h + g] = tp[h, g]

    row = lambda w: jax.ShapeDtypeStruct((1, w), F32)
    out_shape = [row(D_MODEL), row(D_MODEL), row(D_MODEL), row(128), jax.ShapeDtypeStruct((8, D_FF), F32), row(D_FF),
                 jax.ShapeDtypeStruct((32, D_CONV), F32), row(D_CONV), row(D_CONV), row(D_CONV), row(D_POOL),
                 jax.ShapeDtypeStruct((4, POOL_GROUP, POOL_GROUP), F32)]
    return pl.pallas_call(body, name="reduce_small", out_shape=out_shape, in_specs=[VMEM] * 4, out_specs=[VMEM] * 12)(
        l_m, l_f, l_5, l_p)


def _adam_small(ws, gs, ms, vs):
    count = len(ws)

    def body(*refs):
        w_r, g_r, m_r, v_r = (refs[t * count:(t + 1) * count] for t in range(4))
        d_o, m_o, v_o = (refs[(4 + t) * count:(5 + t) * count] for t in range(3))
        for t in range(count):
            delta, m_new, v_new = _adamw(w_r[t][...], g_r[t][...], m_r[t][...], v_r[t][...])
            d_o[t][...] = delta
            m_o[t][...] = m_new
            v_o[t][...] = v_new

    out_shape = [jax.ShapeDtypeStruct(w.shape, F32) for w in ws] * 3
    outs = pl.pallas_call(body, name="adam_small", out_shape=out_shape, in_specs=[VMEM] * (4 * count),
                          out_specs=[VMEM] * (3 * count))(*ws, *gs, *ms, *vs)
    return outs[:count], outs[count:2 * count], outs[2 * count:]


MIX_TILE = 512
FFN_TILE = 256
GRAD_K = 2048


def kernel(x, norm_mix_g, w_in, conv_a_w, conv_a_b, ln_a_g, ln_a_b, pool_w, pool_scale, w_out, norm_ffn_g, w_up, conv_f_w, conv_f_b, w_down, norm_final_g, loss_target, m_norm_mix_g, m_w_in, m_conv_a_w, m_conv_a_b, m_ln_a_g, m_ln_a_b, m_pool_w, m_pool_scale, m_w_out, m_norm_ffn_g, m_w_up, m_conv_f_w, m_conv_f_b, m_w_down, m_norm_final_g, v_norm_mix_g, v_w_in, v_conv_a_w, v_conv_a_b, v_ln_a_g, v_ln_a_b, v_pool_w, v_pool_scale, v_w_out, v_norm_ffn_g, v_w_up, v_conv_f_w, v_conv_f_b, v_w_down, v_norm_final_g):
    seq = x.shape[1]
    xs, ts = x[0], loss_target[0]
    mix_tile, ffn_tile, grad_k = min(MIX_TILE, seq), min(FFN_TILE, seq), min(GRAD_K, seq)
    chip = 2 * lax.axis_index("x") + lax.axis_index("y")
    core = lax.axis_index("c").astype(jnp.int32).reshape(1)

    wa_s = jnp.pad(conv_a_w[0], ((0, 32 - CONV_A), (0, 0)))
    wf_s = jnp.pad(conv_f_w[0], ((0, 8 - CONV_F), (0, 0)))
    win_b, wout_b, wup_b, wdown_b = _cast_shards(w_in[0], w_out[0], w_up[0], w_down[0])
    g3 = norm_final_g.reshape(1, D_MODEL)
    pw = pool_w[0]

    h1, proj, cpre, dpool, mcat, x1, win, wout, wup, wa_g, wf_g = _mixer_fwd(
        xs, norm_mix_g, win_b, wout_b, wup_b, wa_s, wf_s, conv_a_b, ln_a_g, ln_a_b, pw, pool_scale, mix_tile)
    wa = jnp.transpose(wa_g, (1, 0, 2)).reshape(32, D_CONV)
    wf = jnp.transpose(wf_g, (1, 0, 2)).reshape(8, D_FF)
    h2, up, gcs, act, wdown = _ffn_up(x1, norm_ffn_g, wup, wf, conv_f_b, wdown_b, ffn_tile)
    dx2b, sm_f2 = _ffn_down(x1, act, wdown, g3, ts, mix_tile)
    tags = ("w_in", "w_out", "w_up", "w_down")
    blocks = (256, 128, 256, 176)
    g_wdown = _weight_grad(act, dx2b, "rows2", grad_k)
    dup, dx1b, sm_b1, sf, l_wdown = _ffn_bwd(
        dx2b, up, gcs, x1, norm_ffn_g, wup, wf, wdown, ("exchange", [g_wdown]), ffn_tile)
    p_wdown = _pair_sum(core, g_wdown, l_wdown, tags[3], g_wdown.shape[2])
    g_wup, s_wdown = _weight_grad(h2, dup, "cols_chip", grad_k, ("scatter", [p_wdown]))
    g_wout, l_wup = _weight_grad(mcat, dx1b, "rows1", grad_k, ("exchange", [g_wup]))
    p_wup = _pair_sum(core, g_wup, l_wup, tags[2], g_wup.shape[2])
    l_wout, = _sibling_exchange((g_wout,), (), "early")
    p_wout = _pair_sum(core, g_wout, l_wout, tags[1], g_wout.shape[2])
    dproj, gx, sm_b2, s5, sp, s_wout, s_wup = _mixer_bwd(
        dx1b, xs, proj, cpre, dpool, norm_mix_g, win, wa, ln_a_g, ln_a_b, pw, pool_scale, wout, [p_wout, p_wup], mix_tile)
    g_win, grad_x = _weight_grad(h1, dproj, "cols_half", grad_k, carry=gx)

    smalls = (sm_f2, sm_b1, sm_b2, sf, s5, sp)
    landed = _sibling_exchange((g_win,), smalls, "late")
    part_win = _pair_sum(core, g_win, landed[0], tags[0], g_win.shape[2])
    small_parts = _pair_sum_small(smalls, landed[1:])
    send, recv, late_src, late_land, token = _scatter_start([part_win], small_parts)
    big_w = (w_in[0], w_out[0], w_up[0], w_down[0])
    big_m = (m_w_in[0], m_w_out[0], m_w_up[0], m_w_down[0])
    big_v = (v_w_in[0], v_w_out[0], v_w_up[0], v_w_down[0])
    big = {}
    for t, p in ((1, s_wout), (2, s_wup), (3, s_wdown)):
        big[tags[t]] = _adam_big(p, big_w[t], big_m[t], big_v[t], tags[t], blocks[t], token)
    late_src, late_land = _scatter_wait(send, recv, late_src, late_land, 1, [big[tags[t]][3] for t in (1, 2, 3)])
    late = _scatter_forward(late_src, late_land, 1)
    big[tags[0]] = _adam_big(late[0], big_w[0], big_m[0], big_v[0], tags[0], blocks[0], token)
    big = {tag: [a[None] for a in outs] for tag, outs in big.items()}
    scattered = [None] * 4 + list(late[1:])

    (g_g1, g_g2, g_g3, loss_row, g_wf_all, g_fb, g_wa_all, g_cb, g_lg, g_lb, g_ps, g_pw) = _reduce_small(*scattered[4:])
    g_wa = lax.dynamic_slice(g_wa_all, (0, chip * (D_CONV // N_CHIPS)), (32, D_CONV // N_CHIPS))[:CONV_A]
    g_wf = lax.dynamic_slice(g_wf_all, (0, chip * (D_FF // N_CHIPS)), (8, D_FF // N_CHIPS))[:CONV_F]
    small_names = ("norm_mix_g", "conv_a_w", "conv_a_b", "ln_a_g", "ln_a_b", "pool_w", "pool_scale", "norm_ffn_g",
                   "conv_f_w", "conv_f_b", "norm_final_g")
    small_w = (norm_mix_g, conv_a_w[0], conv_a_b, ln_a_g, ln_a_b, pw, pool_scale, norm_ffn_g, conv_f_w[0], conv_f_b, g3)
    small_m = (m_norm_mix_g, m_conv_a_w[0], m_conv_a_b, m_ln_a_g, m_ln_a_b, m_pool_w[0], m_pool_scale, m_norm_ffn_g,
               m_conv_f_w[0], m_conv_f_b, m_norm_final_g.reshape(1, D_MODEL))
    small_v = (v_norm_mix_g, v_conv_a_w[0], v_conv_a_b, v_ln_a_g, v_ln_a_b, v_pool_w[0], v_pool_scale, v_norm_ffn_g,
               v_conv_f_w[0], v_conv_f_b, v_norm_final_g.reshape(1, D_MODEL))
    small_g = (g_g1, g_wa, g_cb, g_lg, g_lb, g_pw, g_ps, g_g2, g_wf, g_fb, g_g3)
    s_delta, s_m, s_v = _adam_small(small_w, small_g, small_m, small_v)
    shapes = {"conv_a_w": conv_a_w.shape, "pool_w": pool_w.shape, "conv_f_w": conv_f_w.shape, "norm_final_g": norm_final_g.shape}
    small = {}
    for t, name in enumerate(small_names):
        shp = shapes.get(name)
        small[name] = [a if shp is None else a.reshape(shp) for a in (small_g[t], s_delta[t], s_m[t], s_v[t])]

    order = ("norm_mix_g", "w_in", "conv_a_w", "conv_a_b", "ln_a_g", "ln_a_b", "pool_w", "pool_scale", "w_out", "norm_ffn_g",
             "w_up", "conv_f_w", "conv_f_b", "w_down", "norm_final_g")
    table = {**big, **small}
    loss = loss_row[0, 0]
    outs = [loss, grad_x[None]]
    for t in range(4):
        outs += [table[name][t] for name in order]
    return tuple(outs)
```

```python
import functools

import jax
import jax.numpy as jnp
from jax import lax
from jax.experimental import pallas as pl
from jax.experimental.pallas import tpu as pltpu

F32 = jnp.float32
BF16 = jnp.bfloat16
EPS = 1e-6
ADAM_LR = 0.001
ADAM_B1 = 0.9
ADAM_B2 = 0.999
ADAM_EPS = 1e-08
ADAM_WD = 0.01
ADAM_STEP = 10

D_MODEL = 1024
D_CONV = 512
D_POOL = 512
D_IN = 1536
D_FF = 2816
CONV_A = 31
CONV_F = 3
POOL_WINDOWS = (2, 4, 8, 16)
POOL_GROUP = 128
N_CHIPS = 4
FF_CHUNK = 256
N_FF_CHUNKS = D_FF // FF_CHUNK
UP_CHUNK = 1408
A_HALO = 32
P_HALO = 16
VMEM_LIMIT = 56 * 1024 * 1024
MESH = pl.DeviceIdType.MESH

ANY = pl.BlockSpec(memory_space=pl.ANY)
VMEM = pl.BlockSpec(memory_space=pltpu.VMEM)


def _dot(a, b):
    return jnp.dot(a, b, preferred_element_type=F32)


def _dot_nt(a, b):
    return lax.dot_general(a, b, (((1,), (1,)), ((), ())), preferred_element_type=F32)


def _dot_tn(a, b):
    return lax.dot_general(a, b, (((0,), (0,)), ((), ())), preferred_element_type=F32)


def _sigmoid(v):
    return jax.nn.sigmoid(v)


def _colsum(v):
    return jnp.sum(v, axis=0, keepdims=True)


def _rowmean(v):
    return jnp.mean(v, axis=-1, keepdims=True)


def _place():
    x, y, c = lax.axis_index("x"), lax.axis_index("y"), lax.axis_index("c")
    chips = [(1 - x, y), (x, 1 - y), (1 - x, 1 - y)]
    return x, y, c, 2 * x + y, chips


SIBLING_ONLY, SIBLING_AND_CHIPS = 0, 1


def _handshake(collective):
    x, y, c, _, chips = _place()
    peers = [(x, y, 1 - c)] + ([(*chip, c) for chip in chips] if collective == SIBLING_AND_CHIPS else [])
    barrier = pltpu.get_barrier_semaphore()
    for peer in peers:
        pl.semaphore_signal(barrier, inc=1, device_id=peer, device_id_type=MESH)
    pl.semaphore_wait(barrier, len(peers))


def _staged(src, dst, stage, sem_in, sem_out):
    hop_in = pltpu.make_async_copy(src, stage, sem_in)
    hop_out = pltpu.make_async_copy(stage, dst, sem_out)

    def relay():
        hop_in.wait()
        hop_out.start()

    return hop_in.start, relay, hop_out.wait


def _gather_ops(bufs, fulls, col_sharded, sems, stages):
    ici_send, ici_recv, fwd_send, fwd_recv, loc_in, loc_out = sems
    n_big = len(bufs)
    x, y, c, k, chips = _place()

    def block(i, kk, half=None):
        rows, cols = bufs[i].shape
        if col_sharded[i]:
            rs = slice(None) if half is None else pl.ds(pl.multiple_of(half * (rows // 2), 16), rows // 2)
            return fulls[i].at[rs, pl.ds(pl.multiple_of(kk * cols, 128), cols)]
        if half is None:
            return fulls[i].at[pl.ds(pl.multiple_of(kk * rows, 16), rows), :]
        return fulls[i].at[pl.ds(pl.multiple_of(kk * rows + half * (rows // 2), 16), rows // 2), :]

    def my_half(i):
        rows = bufs[i].shape[0]
        return bufs[i].at[pl.ds(pl.multiple_of(c * (rows // 2), 16), rows // 2), :]

    def ici(i, j, kk):
        return pltpu.make_async_remote_copy(
            src_ref=my_half(i), dst_ref=block(i, kk, c), send_sem=ici_send.at[i * 3 + j], recv_sem=ici_recv.at[i * 3 + j],
            device_id=(*chips[j], c), device_id_type=MESH)

    def fwd(i, j, kk, half):
        return pltpu.make_async_remote_copy(
            src_ref=block(i, kk, half), dst_ref=block(i, kk, half), send_sem=fwd_send.at[i * 3 + j],
            recv_sem=fwd_recv.at[i * 3 + j], device_id=(x, y, 1 - c), device_id_type=MESH)

    local = [_staged(bufs[i], block(i, k), stages[i], loc_in.at[i], loc_out.at[i]) for i in range(n_big)]
    sends = [ici(i, j, k) for i in range(n_big) for j in range(3)]
    peers = [(i, j, 2 * qx + qy) for i in range(n_big) for j, (qx, qy) in enumerate(chips)]

    def start():
        for cp in local:
            cp[0]()
        for cp in sends:
            cp.start()

    def land():
        for cp in local:
            cp[1]()
        for i, j, kq in peers:
            ici(i, j, kq).wait_recv()
            fwd(i, j, kq, c).start()

    def finish():
        for i, j, kq in peers:
            fwd(i, j, kq, 1 - c).wait_recv()
            fwd(i, j, kq, c).wait_send()
        for cp in sends:
            cp.wait_send()
        for cp in local:
            cp[2]()

    return start, land, finish


def _gather_scratch(shards):
    n_big = len(shards)
    return ([pltpu.SemaphoreType.DMA((3 * n_big,))] * 4 + [pltpu.SemaphoreType.DMA((n_big,))] * 2
            + [pltpu.VMEM(b.shape, b.dtype) for b in shards])


def _tap_ops(srcs, dsts, sems):
    send, recv, loc = sems
    _, _, c, k, chips = _place()

    def copy(t, j, kk):
        return pltpu.make_async_remote_copy(
            src_ref=srcs[t], dst_ref=dsts[t].at[kk], send_sem=send.at[t * 3 + j], recv_sem=recv.at[t * 3 + j],
            device_id=(*chips[j], c), device_id_type=MESH)

    local = [pltpu.make_async_copy(srcs[t], dsts[t].at[k], loc.at[t]) for t in range(len(srcs))]
    sends = [[copy(t, j, k) for j in range(3)] for t in range(len(srcs))]

    def start():
        for t, cp in enumerate(local):
            cp.start()
            for sd in sends[t]:
                sd.start()

    def wait(t):
        for j, (qx, qy) in enumerate(chips):
            copy(t, j, 2 * qx + qy).wait_recv()
        for sd in sends[t]:
            sd.wait_send()
        local[t].wait()

    return start, wait


def _cast_shards(*shards):
    def body(*refs):
        for src, dst in zip(refs[:len(shards)], refs[len(shards):]):
            dst[...] = src[...].astype(BF16)

    return pl.pallas_call(
        body, name="cast_shards", out_shape=[jax.ShapeDtypeStruct(s.shape, BF16) for s in shards],
        in_specs=[VMEM] * len(shards), out_specs=[VMEM] * len(shards),
        compiler_params=pltpu.CompilerParams(vmem_limit_bytes=VMEM_LIMIT),
    )(*shards)


def _load_weights(pairs, sem, first=0):
    cps = [pltpu.make_async_copy(src, dst, sem.at[first + i]) for i, (src, dst) in enumerate(pairs)]
    for cp in cps:
        cp.start()
    for cp in cps:
        cp.wait()


def _shifted_views(buf, shifted, t_rows):
    n = t_rows + A_HALO - 8
    for b in range(1, 8):
        shifted[b - 1] = buf[b:b + n, :]

    def view(offset):
        a, b = divmod(offset, 8)
        if b == 0:
            return buf[8 * a:8 * a + t_rows, :]
        return shifted[b - 1, 8 * a:8 * a + t_rows, :]

    return view


def _pool_count(tile, t_rows, w):
    row = lax.broadcasted_iota(jnp.int32, (t_rows, POOL_GROUP), 0) + tile * t_rows
    return jnp.minimum(row + 1, w).astype(F32)


def _mixer_fwd(x, g1, win_b, wout_b, wup_b, wa_s, wf_s, cb, lg, lb, pw, ps, tile_rows):
    seq = x.shape[0]
    tr = tile_rows
    n = seq // tr

    def body(x_ref, g1_ref, win_b_hbm, wout_b_hbm, wup_b_hbm, wa_s_hbm, wf_s_hbm, cb_ref, lg_ref, lb_ref, pw_ref,
             ps_ref, h1_ref, proj_ref, c_ref, d_ref, m_ref, x1_ref, win_f, wout_f, wup_f, wa_g, wf_g,
             win_v, wout_v, wa_ref, ubuf, ushift, bbuf, sem, *csems):
        i = pl.program_id(0)
        first_sems, first_stages, second_sems, second_stages, later_sems, later_stages, tap_sems = (
            csems[0:6], csems[6:7], csems[7:13], csems[13:14], csems[14:20], csems[20:21], csems[21:24])

        def first():
            return _gather_ops((win_b_hbm,), (win_f,), (True,), first_sems, first_stages)

        def second():
            return _gather_ops((wout_b_hbm,), (wout_f,), (False,), second_sems, second_stages)

        def later():
            return _gather_ops((wup_b_hbm,), (wup_f,), (True,), later_sems, later_stages)

        def taps():
            return _tap_ops((wa_s_hbm, wf_s_hbm), (wa_g, wf_g), tap_sems)

        @pl.when(i == 0)
        def _():
            _handshake(SIBLING_AND_CHIPS)
            first()[0]()
            taps()[0]()
            second()[0]()
            later()[0]()
            first()[1]()
            first()[2]()
            _load_weights([(win_f, win_v)], sem)
            ubuf[0:A_HALO, :] = jnp.zeros((A_HALO, D_CONV), F32)
            bbuf[0:P_HALO, :] = jnp.zeros((P_HALO, D_POOL), F32)

        xv = x_ref[...]
        r = lax.rsqrt(_rowmean(xv * xv) + EPS)
        h1 = (xv * r * g1_ref[...]).astype(BF16)
        h1_ref[...] = h1
        proj = _dot(h1, win_v[...])
        proj_ref[...] = proj.astype(BF16)

        @pl.when(i == 0)
        def _():
            taps()[1](0)
            _load_weights([(wa_g.at[kk], wa_ref.at[:, kk * (D_CONV // N_CHIPS):(kk + 1) * (D_CONV // N_CHIPS)])
                           for kk in range(N_CHIPS)], sem, 2)

        av, ag, bi = proj[:, :D_CONV], proj[:, D_CONV:2 * D_CONV], proj[:, 2 * D_CONV:]
        ubuf[A_HALO:A_HALO + tr, :] = av * _sigmoid(ag)
        off = A_HALO - (CONV_A - 1)
        uview = _shifted_views(ubuf, ushift, tr)
        acc = wa_ref[0:1, :] * uview(off)
        for j in range(1, CONV_A):
            acc = acc + wa_ref[j:j + 1, :] * uview(off + j)
        cv = acc + cb_ref[...]
        ubuf[0:A_HALO, :] = ubuf[tr:tr + A_HALO, :]
        c_ref[...] = cv.astype(BF16)
        xc = cv - _rowmean(cv)
        z = xc * lax.rsqrt(_rowmean(xc * xc) + EPS)
        ln = z * lg_ref[...] + lb_ref[...]
        ya = ln * _sigmoid(ln)
        bbuf[P_HALO:P_HALO + tr, :] = bi
        ds, ybs = [], []
        for g, w in enumerate(POOL_WINDOWS):
            cols = slice(g * POOL_GROUP, (g + 1) * POOL_GROUP)
            s = bi[:, cols]
            for kk in range(1, w):
                s = s + bbuf[P_HALO - kk:P_HALO - kk + tr, cols]
            dg = s / _pool_count(i, tr, w) - bi[:, cols]
            ds.append(dg)
            ybs.append(_dot(dg.astype(BF16), pw_ref[g].astype(BF16)))
        bbuf[0:P_HALO, :] = bbuf[tr:tr + P_HALO, :]
        d_ref[...] = jnp.concatenate(ds, axis=1).astype(BF16)
        yb = jnp.concatenate(ybs, axis=1) * ps_ref[...]
        m = jnp.concatenate([ya, yb], axis=1).astype(BF16)
        m_ref[...] = m

        @pl.when(i == 0)
        def _():
            second()[1]()
            second()[2]()
            _load_weights([(wout_f, wout_v)], sem, 1)

        x1_ref[...] = xv + _dot(m, wout_v[...])

        @pl.when(i == n - 1)
        def _():
            later()[1]()
            later()[2]()
            taps()[1](1)

    tile = lambda w: pl.BlockSpec((tr, w), lambda i: (i, 0))
    full = lambda a: pl.BlockSpec(a.shape, lambda i: (0,) * a.ndim)
    return pl.pallas_call(
        body, name="mixer_fwd", grid=(n,),
        in_specs=[tile(D_MODEL), full(g1)] + [ANY] * 5 + [full(cb), full(lg), full(lb), full(pw), full(ps)],
        out_specs=[tile(D_MODEL), tile(D_IN), tile(D_CONV), tile(D_POOL), tile(D_MODEL), tile(D_MODEL)] + [ANY] * 5,
        out_shape=[
            jax.ShapeDtypeStruct((seq, D_MODEL), BF16), jax.ShapeDtypeStruct((seq, D_IN), BF16),
            jax.ShapeDtypeStruct((seq, D_CONV), BF16), jax.ShapeDtypeStruct((seq, D_POOL), BF16),
            jax.ShapeDtypeStruct((seq, D_MODEL), BF16), jax.ShapeDtypeStruct((seq, D_MODEL), F32),
            jax.ShapeDtypeStruct((D_MODEL, D_IN), BF16), jax.ShapeDtypeStruct((D_MODEL, D_MODEL), BF16),
            jax.ShapeDtypeStruct((D_MODEL, 2 * D_FF), BF16),
            jax.ShapeDtypeStruct((N_CHIPS,) + wa_s.shape, F32), jax.ShapeDtypeStruct((N_CHIPS,) + wf_s.shape, F32),
        ],
        scratch_shapes=[
            pltpu.VMEM((D_MODEL, D_IN), BF16), pltpu.VMEM((D_MODEL, D_MODEL), BF16), pltpu.VMEM((32, D_CONV), F32),
            pltpu.VMEM((tr + A_HALO, D_CONV), F32), pltpu.VMEM((7, tr + A_HALO - 8, D_CONV), F32),
            pltpu.VMEM((tr + P_HALO, D_POOL), F32), pltpu.SemaphoreType.DMA((2 + N_CHIPS,)),
        ] + _gather_scratch((win_b,)) + _gather_scratch((wout_b,)) + _gather_scratch((wup_b,)) + [
            pltpu.SemaphoreType.DMA((6,)), pltpu.SemaphoreType.DMA((6,)), pltpu.SemaphoreType.DMA((2,))],
        compiler_params=pltpu.CompilerParams(dimension_semantics=("arbitrary",), vmem_limit_bytes=VMEM_LIMIT,
                                             collective_id=SIBLING_AND_CHIPS),
    )(x, g1, win_b, wout_b, wup_b, wa_s, wf_s, cb, lg, lb, pw, ps)


def _ffn_up(x1, g2, wup, wf, fb, wdown_b, tile_rows):
    seq = x1.shape[0]
    tr = tile_rows
    n = seq // tr

    def body(x1_ref, g2_ref, wup_hbm, wf_ref, fb_ref, wdown_b_hbm,
             h2_ref, up_ref, gc_ref, act_ref, wdown_f, wup_v, gbuf, sem, *gsems):
        i = pl.program_id(0)

        def gather():
            return _gather_ops((wdown_b_hbm,), (wdown_f,), (False,), gsems[:6], gsems[6:])

        @pl.when(i == 0)
        def _():
            _handshake(SIBLING_AND_CHIPS)
            gather()[0]()
            _load_weights(((wup_hbm, wup_v),), sem)
            gbuf[0:8, :] = jnp.zeros((8, D_FF), F32)

        x1v = x1_ref[...]
        r2 = lax.rsqrt(_rowmean(x1v * x1v) + EPS)
        h2 = (x1v * r2 * g2_ref[...]).astype(BF16)
        h2_ref[...] = h2

        def up_proj(j):
            return (_dot(h2, wup_v[:, j * UP_CHUNK:(j + 1) * UP_CHUNK]),
                    _dot(h2, wup_v[:, D_FF + j * UP_CHUNK:D_FF + (j + 1) * UP_CHUNK]))

        ahead = up_proj(0)
        for j in range(D_FF // UP_CHUNK):
            cs = slice(j * UP_CHUNK, (j + 1) * UP_CHUNK)
            vs = slice(D_FF + j * UP_CHUNK, D_FF + (j + 1) * UP_CHUNK)
            gate, val = ahead
            if j + 1 < D_FF // UP_CHUNK:
                ahead = up_proj(j + 1)
            up_ref[:, cs] = gate.astype(BF16)
            up_ref[:, vs] = val.astype(BF16)
            gbuf[8:8 + tr, cs] = gate
            gc = (wf_ref[0:1, cs] * gbuf[6:6 + tr, cs] + wf_ref[1:2, cs] * gbuf[7:7 + tr, cs]
                  + wf_ref[2:3, cs] * gate + fb_ref[:, cs])
            gbuf[0:8, cs] = gbuf[tr:tr + 8, cs]
            gc_ref[:, cs] = gc.astype(BF16)
            act_ref[:, cs] = (gc * _sigmoid(gc) * val).astype(BF16)

        @pl.when(i == max(n - 2, 0))
        def _():
            gather()[1]()

        @pl.when(i == n - 1)
        def _():
            gather()[2]()

    tile = lambda w: pl.BlockSpec((tr, w), lambda i: (i, 0))
    full = lambda a: pl.BlockSpec(a.shape, lambda i: (0,) * a.ndim)
    return pl.pallas_call(
        body, name="ffn_up", grid=(n,),
        in_specs=[tile(D_MODEL), full(g2), ANY, full(wf), full(fb), ANY],
        out_specs=[tile(D_MODEL), tile(2 * D_FF), tile(D_FF), tile(D_FF), ANY],
        out_shape=[
            jax.ShapeDtypeStruct((seq, D_MODEL), BF16), jax.ShapeDtypeStruct((seq, 2 * D_FF), BF16),
            jax.ShapeDtypeStruct((seq, D_FF), BF16), jax.ShapeDtypeStruct((seq, D_FF), BF16),
            jax.ShapeDtypeStruct((D_FF, D_MODEL), BF16),
        ],
        scratch_shapes=[pltpu.VMEM(wup.shape, BF16), pltpu.VMEM((tr + 8, D_FF), F32), pltpu.SemaphoreType.DMA((1,))]
        + _gather_scratch((wdown_b,)),
        compiler_params=pltpu.CompilerParams(dimension_semantics=("arbitrary",), vmem_limit_bytes=VMEM_LIMIT,
                                             collective_id=SIBLING_AND_CHIPS),
    )(x1, g2, wup, wf, fb, wdown_b)


def _ffn_down(x1, act, wdown, g3, target, tile_rows):
    seq = x1.shape[0]
    tr = tile_rows
    n = seq // tr

    def body(x1_ref, act_ref, wdown_hbm, g3_ref, t_ref, dx2b_ref, sm_ref, wdown_v, sem):
        i = pl.program_id(0)

        @pl.when(i == 0)
        def _():
            _load_weights(((wdown_hbm, wdown_v),), sem)
            sm_ref[...] = jnp.zeros(sm_ref.shape, F32)

        quarter = tr // 4
        prods = [_dot(act_ref[h * quarter:(h + 1) * quarter, :], wdown_v[...]) for h in range(4)]
        for h in range(4):
            rows = slice(h * quarter, (h + 1) * quarter)
            x2 = x1_ref[rows, :] + prods[h]
            r3 = lax.rsqrt(_rowmean(x2 * x2) + EPS)
            n3 = x2 * r3
            err = n3 * g3_ref[...] - t_ref[rows, :]
            dy = err / D_MODEL
            sm_ref[2:3, :] += _colsum(dy * n3)
            loss = 0.5 * _colsum(_rowmean(err * err))
            sm_ref[3:4, :] += jnp.broadcast_to(loss, (1, D_MODEL))
            dn = dy * g3_ref[...]
            dx2b_ref[rows, :] = (r3 * (dn - n3 * _rowmean(dn * n3))).astype(BF16)

    tile = lambda w: pl.BlockSpec((tr, w), lambda i: (i, 0))
    full = lambda a: pl.BlockSpec(a.shape, lambda i: (0,) * a.ndim)
    return pl.pallas_call(
        body, name="ffn_down", grid=(n,),
        in_specs=[tile(D_MODEL), tile(D_FF), ANY, full(g3), tile(D_MODEL)],
        out_specs=[tile(D_MODEL), pl.BlockSpec((8, D_MODEL), lambda i: (0, 0))],
        out_shape=[jax.ShapeDtypeStruct((seq, D_MODEL), BF16), jax.ShapeDtypeStruct((8, D_MODEL), F32)],
        scratch_shapes=[pltpu.VMEM(wdown.shape, BF16), pltpu.SemaphoreType.DMA((1,))],
        compiler_params=pltpu.CompilerParams(dimension_semantics=("arbitrary",), vmem_limit_bytes=VMEM_LIMIT),
    )(x1, act, wdown, g3, target)


def _ffn_bwd(dx2, up, gcs, x1, g2, wup, wf, wdown, comm, tile_rows):
    seq = x1.shape[0]
    c_ins, c_shapes, c_sems, c_ops, c_id = _comm_plan(comm)
    nc = len(c_ins)
    tr = tile_rows
    n = seq // tr

    def body(dx2_ref, up_ref, gc_ref, x1_ref, g2_ref, wup_hbm, wf_ref, wdown_hbm, *rest):
        c_in, rest = rest[:nc], rest[nc:]
        dup_ref, dx1b_ref, sm_ref, sf_ref = rest[:4]
        c_out, rest = rest[4:4 + nc], rest[4 + nc:]
        wup_v, wdown_v, dbuf, dcar, sem = rest[:5]
        c_sem_refs = rest[5:]
        i = pl.program_id(0)

        @pl.when(i == 0)
        def _():
            c_ops(c_in, c_out, c_sem_refs)[0]()
            _load_weights(((wup_hbm, wup_v), (wdown_hbm, wdown_v)), sem)
            dcar[...] = jnp.zeros(dcar.shape, F32)
            sm_ref[...] = jnp.zeros(sm_ref.shape, F32)
            sf_ref[...] = jnp.zeros(sf_ref.shape, F32)

        dx2b = dx2_ref[...]
        dx2v = dx2b.astype(F32)
        dh2 = jnp.zeros((tr, D_MODEL), F32)

        def down_t(j):
            return _dot_nt(dx2b, wdown_v[j * FF_CHUNK:(j + 1) * FF_CHUNK, :])

        ahead = down_t(0)
        for j in range(N_FF_CHUNKS):
            cs = slice(j * FF_CHUNK, (j + 1) * FF_CHUNK)
            vs = slice(D_FF + j * FF_CHUNK, D_FF + (j + 1) * FF_CHUNK)
            dact = ahead
            if j + 1 < N_FF_CHUNKS:
                ahead = down_t(j + 1)
            gate = up_ref[:, cs].astype(F32)
            val = up_ref[:, vs].astype(F32)
            gc = gc_ref[:, cs].astype(F32)
            sg = _sigmoid(gc)
            dval = dact * (gc * sg)
            dgc = dact * val * (sg * (1.0 + gc * (1.0 - sg)))
            dbuf[0:tr, :] = dgc
            dbuf[tr:tr + 8, :] = dcar[:, cs]
            d_p1 = dbuf[1:1 + tr, :]
            d_p2 = dbuf[2:2 + tr, :]
            dgate = wf_ref[2:3, cs] * dgc + wf_ref[1:2, cs] * d_p1 + wf_ref[0:1, cs] * d_p2
            dcar[:, cs] = dgc[0:8, :]
            sf_ref[0:1, cs] += _colsum(d_p2 * gate)
            sf_ref[1:2, cs] += _colsum(d_p1 * gate)
            sf_ref[2:3, cs] += _colsum(dgc * gate)
            sf_ref[3:4, cs] += _colsum(dgc)
            dgb, dvb = dgate.astype(BF16), dval.astype(BF16)
            dup_ref[:, cs] = dgb
            dup_ref[:, vs] = dvb
            dh2 = dh2 + _dot_nt(dgb, wup_v[:, cs]) + _dot_nt(dvb, wup_v[:, vs])
        x1v = x1_ref[...]
        r2 = lax.rsqrt(_rowmean(x1v * x1v) + EPS)
        n2 = x1v * r2
        sm_ref[1:2, :] += _colsum(dh2 * n2)
        dn2 = dh2 * g2_ref[...]
        dx1b_ref[...] = (dx2v + r2 * (dn2 - n2 * _rowmean(dn2 * n2))).astype(BF16)

        @pl.when(i == n - 1)
        def _():
            c_ops(c_in, c_out, c_sem_refs)[2]()

    tile = lambda w: pl.BlockSpec((tr, w), lambda i: (n - 1 - i, 0))
    full = lambda a: pl.BlockSpec(a.shape, lambda i: (0,) * a.ndim)
    acc = lambda rows, w: pl.BlockSpec((rows, w), lambda i: (0, 0))
    return pl.pallas_call(
        body, name="ffn_bwd", grid=(n,),
        in_specs=[tile(D_MODEL), tile(2 * D_FF), tile(D_FF), tile(D_MODEL), full(g2), ANY, full(wf), ANY] + [ANY] * nc,
        out_specs=[tile(2 * D_FF), tile(D_MODEL), acc(8, D_MODEL), acc(8, D_FF)] + [ANY] * nc,
        out_shape=[
            jax.ShapeDtypeStruct((seq, 2 * D_FF), BF16), jax.ShapeDtypeStruct((seq, D_MODEL), BF16),
            jax.ShapeDtypeStruct((8, D_MODEL), F32), jax.ShapeDtypeStruct((8, D_FF), F32),
        ] + c_shapes,
        scratch_shapes=[
            pltpu.VMEM(wup.shape, BF16), pltpu.VMEM(wdown.shape, BF16),
            pltpu.VMEM((tr + 8, FF_CHUNK), F32), pltpu.VMEM((8, D_FF), F32), pltpu.SemaphoreType.DMA((2,)),
        ] + c_sems,
        compiler_params=pltpu.CompilerParams(dimension_semantics=("arbitrary",), vmem_limit_bytes=VMEM_LIMIT,
                                             collective_id=c_id),
    )(dx2, up, gcs, x1, g2, wup, wf, wdown, *c_ins)


def _mixer_bwd(dx1, x, proj, cpre, d, g1, win, wa, lg, lb, pw, ps, wout, parts, tile_rows):
    seq = x.shape[0]
    n_parts = len(parts)
    tr = tile_rows
    n = seq // tr
    row_cb, row_lg, row_lb, row_ps = 32, 33, 34, 35

    def body(dx1_ref, x_ref, proj_ref, projh_ref, c_ref, d_ref, g1_ref, win_hbm, wa_ref, lg_ref, lb_ref, pw_ref, ps_ref,
             wout_hbm, *rest):
        part_refs, rest = rest[:n_parts], rest[n_parts:]
        dproj_ref, gx_ref, sm_ref, s5_ref, sp_ref = rest[:5]
        land_refs, rest = rest[5:5 + n_parts], rest[5 + n_parts:]
        win_v, wout_v, ubuf, ushift, dcbuf, dshift, ebuf, sem = rest[:8]
        ssems = rest[8:]
        i = pl.program_id(0)
        tile = n - 1 - i

        def scatter():
            return _scatter_ops(part_refs, land_refs, n_parts, ssems[:6], ssems[6:])

        @pl.when(i == 0)
        def _():
            _handshake(SIBLING_AND_CHIPS)
            scatter()[0]()
            _load_weights(((win_hbm, win_v), (wout_hbm, wout_v)), sem)
            dcbuf[tr:tr + A_HALO, :] = jnp.zeros((A_HALO, D_CONV), F32)
            ebuf[tr:tr + P_HALO, :] = jnp.zeros((P_HALO, D_POOL), F32)
            sm_ref[...] = jnp.zeros(sm_ref.shape, F32)
            s5_ref[...] = jnp.zeros(s5_ref.shape, F32)
            sp_ref[...] = jnp.zeros(sp_ref.shape, F32)

        dx1b = dx1_ref[...]
        dx1v = dx1b.astype(F32)
        dm = _dot_nt(dx1b, wout_v[...])
        dya, dyb = dm[:, :D_CONV], dm[:, D_CONV:]
        dbis = []
        for g, w in enumerate(POOL_WINDOWS):
            cols = slice(g * POOL_GROUP, (g + 1) * POOL_GROUP)
            dgb = d_ref[:, cols]
            pwb = pw_ref[g].astype(BF16)
            dyg = dyb[:, cols]
            s5_ref[row_ps:row_ps + 1, cols] += _colsum(dyg * _dot(dgb, pwb))
            dqb = (dyg * ps_ref[:, cols]).astype(BF16)
            sp_ref[g] += _dot_tn(dgb, dqb)
            dd = _dot_nt(dqb, pwb)
            e = dd / _pool_count(tile, tr, w)
            ebuf[0:tr, cols] = e
            s = e
            for kk in range(1, w):
                s = s + ebuf[kk:kk + tr, cols]
            dbis.append(s - dd)
        ebuf[tr:tr + P_HALO, :] = ebuf[0:P_HALO, :]
        cv = c_ref[...].astype(F32)
        xc = cv - _rowmean(cv)
        rs = lax.rsqrt(_rowmean(xc * xc) + EPS)
        z = xc * rs
        ln = z * lg_ref[...] + lb_ref[...]
        sl = _sigmoid(ln)
        dl = dya * (sl * (1.0 + ln * (1.0 - sl)))
        s5_ref[row_lg:row_lg + 1, :] += _colsum(dl * z)
        s5_ref[row_lb:row_lb + 1, :] += _colsum(dl)
        dz = dl * lg_ref[...]
        dc = rs * (dz - _rowmean(dz) - z * _rowmean(dz * z))
        s5_ref[row_cb:row_cb + 1, :] += _colsum(dc)
        dcbuf[0:tr, :] = dc
        keep = (tile > 0).astype(F32)
        avh = projh_ref[:, :D_CONV].astype(F32)
        agh = projh_ref[:, D_CONV:].astype(F32)
        ubuf[0:A_HALO, :] = avh * _sigmoid(agh) * keep
        av = proj_ref[:, :D_CONV].astype(F32)
        ag = proj_ref[:, D_CONV:2 * D_CONV].astype(F32)
        sg = _sigmoid(ag)
        ubuf[A_HALO:A_HALO + tr, :] = av * sg
        off = A_HALO - (CONV_A - 1)
        du = wa_ref[CONV_A - 1:CONV_A, :] * dc
        dview = _shifted_views(dcbuf, dshift, tr)
        uview = _shifted_views(ubuf, ushift, tr)
        for j in range(CONV_A - 1):
            du = du + wa_ref[j:j + 1, :] * dview(CONV_A - 1 - j)
        for j in range(CONV_A):
            s5_ref[j:j + 1, :] += _colsum(dc * uview(off + j))
        dcbuf[tr:tr + A_HALO, :] = dcbuf[0:A_HALO, :]
        dav = du * sg
        dag = du * av * (sg * (1.0 - sg))
        dprojb = jnp.concatenate([dav, dag] + dbis, axis=1).astype(BF16)
        dproj_ref[...] = dprojb
        dh1 = _dot_nt(dprojb, win_v[...])
        xv = x_ref[...]
        r1 = lax.rsqrt(_rowmean(xv * xv) + EPS)
        n1 = xv * r1
        sm_ref[0:1, :] += _colsum(dh1 * n1)
        dn1 = dh1 * g1_ref[...]
        gx_ref[...] = dx1v + r1 * (dn1 - n1 * _rowmean(dn1 * n1))

        @pl.when(i == max(n - 2, 0))
        def _():
            scatter()[1]()

        @pl.when(i == n - 1)
        def _():
            scatter()[2]()

    tile = lambda w: pl.BlockSpec((tr, w), lambda i: (n - 1 - i, 0))
    full = lambda a: pl.BlockSpec(a.shape, lambda i: (0,) * a.ndim)
    halo = pl.BlockSpec((A_HALO, 2 * D_CONV), lambda i: (jnp.maximum((n - 1 - i) * (tr // A_HALO) - 1, 0), 0))
    acc = lambda shape: pl.BlockSpec(shape, lambda i: (0,) * len(shape))
    return pl.pallas_call(
        body, name="mixer_bwd", grid=(n,),
        in_specs=[tile(D_MODEL), tile(D_MODEL), tile(D_IN), halo, tile(D_CONV), tile(D_POOL), full(g1), ANY, full(wa),
                  full(lg), full(lb), full(pw), full(ps), ANY] + [ANY] * n_parts,
        out_specs=[tile(D_IN), tile(D_MODEL), acc((8, D_MODEL)), acc((40, D_CONV)), acc(pw.shape)] + [ANY] * n_parts,
        out_shape=[
            jax.ShapeDtypeStruct((seq, D_IN), BF16), jax.ShapeDtypeStruct((seq, D_MODEL), F32),
            jax.ShapeDtypeStruct((8, D_MODEL), F32), jax.ShapeDtypeStruct((40, D_CONV), F32),
            jax.ShapeDtypeStruct(pw.shape, F32),
        ] + _scatter_shapes(parts, ()),
        scratch_shapes=[
            pltpu.VMEM(win.shape, BF16), pltpu.VMEM(wout.shape, BF16),
            pltpu.VMEM((tr + A_HALO, D_CONV), F32), pltpu.VMEM((7, tr + A_HALO - 8, D_CONV), F32),
            pltpu.VMEM((tr + A_HALO, D_CONV), F32), pltpu.VMEM((7, tr + A_HALO - 8, D_CONV), F32),
            pltpu.VMEM((tr + P_HALO, D_POOL), F32), pltpu.SemaphoreType.DMA((2,)),
        ] + _scatter_scratch(parts, ()),
        compiler_params=pltpu.CompilerParams(dimension_semantics=("arbitrary",), vmem_limit_bytes=VMEM_LIMIT,
                                             collective_id=SIBLING_AND_CHIPS),
    )(dx1, x, proj, proj, cpre, d, g1, win, wa, lg, lb, pw, ps, wout, *parts)


def _weight_grad(a, b, layout, k_rows, comm=None, carry=None):
    seq, m_dim = a.shape
    n_dim = b.shape[1]
    steps = seq // k_rows

    def store(o_ref, acc, index, value):
        if steps == 1:
            o_ref[index] = value.astype(BF16)
            return
        s = pl.program_id(1)

        @pl.when(s == 0)
        def _():
            acc[index] = value

        @pl.when(jnp.logical_and(s > 0, s < steps - 1))
        def _():
            acc[index] += value

        @pl.when(s == steps - 1)
        def _():
            o_ref[index] = (acc[index] + value).astype(BF16)

    if layout in ("rows1", "rows2"):
        groups = int(layout[-1])
        per_tile = N_CHIPS // groups
        rows = m_dim // N_CHIPS // 2
        a_w = m_dim // groups

        def body(a_ref, b_ref, o_ref, acc):
            r = _dot_tn(a_ref[...], b_ref[...])
            for p in range(per_tile):
                for h in range(2):
                    store(o_ref, acc, (h, p), r[(2 * p + h) * rows:(2 * p + h + 1) * rows, :])

        in_specs = [pl.BlockSpec((k_rows, a_w), lambda g, s: (s, g)), pl.BlockSpec((k_rows, n_dim), lambda g, s: (s, 0))]
        out_spec = pl.BlockSpec((2, per_tile, rows, n_dim), lambda g, s: (0, g, 0, 0))
        out_dims, acc_dims = (2, N_CHIPS, rows, n_dim), (2, per_tile, rows, n_dim)
    elif layout == "cols_chip":
        groups = N_CHIPS
        rows, cols = m_dim // 2, n_dim // N_CHIPS

        def body(a_ref, b_ref, o_ref, acc):
            r = _dot_tn(a_ref[...], b_ref[...])
            for h in range(2):
                store(o_ref, acc, h, r[h * rows:(h + 1) * rows, :])

        in_specs = [pl.BlockSpec((k_rows, m_dim), lambda g, s: (s, 0)), pl.BlockSpec((k_rows, cols), lambda g, s: (s, g))]
        out_spec = pl.BlockSpec((2, None, rows, cols), lambda g, s: (0, g, 0, 0))
        out_dims, acc_dims = (2, N_CHIPS, rows, cols), (2, rows, cols)
    else:
        groups = 2
        rows, cols = m_dim // 2, n_dim // N_CHIPS

        def body(a_ref, b_ref, o_ref, acc):
            r = _dot_tn(a_ref[...], b_ref[...])
            for k in range(N_CHIPS):
                store(o_ref, acc, k, r[:, k * cols:(k + 1) * cols])

        in_specs = [pl.BlockSpec((k_rows, rows), lambda g, s: (s, g)), pl.BlockSpec((k_rows, n_dim), lambda g, s: (s, 0))]
        out_spec = pl.BlockSpec((None, N_CHIPS, rows, cols), lambda g, s: (g, 0, 0, 0))
        out_dims, acc_dims = (2, N_CHIPS, rows, cols), (N_CHIPS, rows, cols)

    c_ins, c_shapes, c_sems, c_ops, c_id = _comm_plan(comm)
    nc = len(c_ins)
    c_specs = [ANY] * nc
    if carry is not None:
        assert comm is None and carry.shape[0] % (groups * steps) == 0
        carry_spec = pl.BlockSpec((carry.shape[0] // (groups * steps), carry.shape[1]), lambda g, s: (g * steps + s, 0))
        c_ins, c_shapes, c_specs, nc = (carry,), [jax.ShapeDtypeStruct(carry.shape, carry.dtype)], [carry_spec], 1

    def hosted(a_ref, b_ref, *rest):
        c_in, o_ref, c_out, acc, sems = rest[:nc], rest[nc], rest[nc + 1:2 * nc + 1], rest[2 * nc + 1], rest[2 * nc + 2:]
        g, s = pl.program_id(0), pl.program_id(1)
        if carry is not None:
            c_out[0][...] = c_in[0][...]
            body(a_ref, b_ref, o_ref, acc)
            return
        if nc:
            @pl.when(jnp.logical_and(g == 0, s == 0))
            def _():
                c_ops(c_in, c_out, sems)[0]()

        body(a_ref, b_ref, o_ref, acc)
        if nc:
            step = g * steps + s

            @pl.when(step == max(groups * steps - 2, 0))
            def _():
                c_ops(c_in, c_out, sems)[1]()

            @pl.when(step == groups * steps - 1)
            def _():
                c_ops(c_in, c_out, sems)[2]()

    outs = pl.pallas_call(
        hosted, name=f"weight_grad_{layout}_{m_dim}x{n_dim}", grid=(groups, steps),
        in_specs=in_specs + c_specs, out_specs=[out_spec] + c_specs,
        out_shape=[jax.ShapeDtypeStruct(out_dims, BF16)] + c_shapes,
        scratch_shapes=[pltpu.VMEM(acc_dims, F32)] + c_sems,
        compiler_params=pltpu.CompilerParams(dimension_semantics=("arbitrary", "arbitrary"), vmem_limit_bytes=VMEM_LIMIT,
                                             collective_id=c_id),
    )(a, b, *c_ins)
    return outs if nc else outs[0]


def _exchange_ops(ins, outs, n_big, sems):
    send, recv = sems
    x, y, c, _, _ = _place()
    cps = [pltpu.make_async_remote_copy(
        src_ref=ins[t].at[1 - c] if t < n_big else ins[t], dst_ref=outs[t], send_sem=send.at[t], recv_sem=recv.at[t],
        device_id=(x, y, 1 - c), device_id_type=MESH) for t in range(len(ins))]

    def start():
        for cp in cps:
            cp.start()

    def finish():
        for cp in cps:
            cp.wait()

    return start, finish


def _exchange_shapes(bigs, smalls):
    return [jax.ShapeDtypeStruct((N_CHIPS,) + b.shape[2:], b.dtype) for b in bigs] + [
        jax.ShapeDtypeStruct(s.shape, s.dtype) for s in smalls]


def _comm_plan(comm):
    if comm is None:
        return (), [], [], None, None
    kind, arrays = comm
    n = len(arrays)

    def scatter(i, o, sm):
        start, land, finish = _scatter_ops(i, o, n, sm[:6], sm[6:])
        return lambda: (_handshake(SIBLING_AND_CHIPS), start()), land, finish

    def exchange(i, o, sm):
        start, finish = _exchange_ops(i, o, n, sm)
        return lambda: (_handshake(SIBLING_ONLY), start()), lambda: None, finish

    if kind == "scatter":
        return tuple(arrays), _scatter_shapes(arrays, ()), _scatter_scratch(arrays, ()), scatter, SIBLING_AND_CHIPS
    return tuple(arrays), _exchange_shapes(arrays, ()), [pltpu.SemaphoreType.DMA((n,))] * 2, exchange, SIBLING_ONLY


def _sibling_exchange(bigs, smalls, tag):
    nb, nt = len(bigs), len(bigs) + len(smalls)

    def body(*refs):
        start, finish = _exchange_ops(refs[:nt], refs[nt:2 * nt], nb, refs[2 * nt:])
        _handshake(SIBLING_ONLY)
        start()
        finish()

    return pl.pallas_call(
        body, name=f"sibling_exchange_{tag}", out_shape=_exchange_shapes(bigs, smalls),
        in_specs=[ANY] * nt, out_specs=[ANY] * nt,
        scratch_shapes=[pltpu.SemaphoreType.DMA((nt,)), pltpu.SemaphoreType.DMA((nt,))],
        compiler_params=pltpu.CompilerParams(collective_id=SIBLING_ONLY),
    )(*bigs, *smalls)


def _pair_sum(core, mine, theirs, tag, block_rows):
    _, _, rows, cols = mine.shape
    steps = rows // block_rows

    def body(core_ref, a_ref, b_ref, o_ref):
        o_ref[...] = (a_ref[...].astype(F32) + b_ref[...].astype(F32)).astype(BF16)

    grid_spec = pltpu.PrefetchScalarGridSpec(
        num_scalar_prefetch=1, grid=(N_CHIPS, steps),
        in_specs=[pl.BlockSpec((None, None, block_rows, cols), lambda k, r, core_ref: (core_ref[0], k, r, 0)),
                  pl.BlockSpec((None, block_rows, cols), lambda k, r, core_ref: (k, r, 0))],
        out_specs=pl.BlockSpec((None, block_rows, cols), lambda k, r, core_ref: (k, r, 0)),
    )
    return pl.pallas_call(
        body, name=f"pair_sum_{tag}", grid_spec=grid_spec,
        out_shape=jax.ShapeDtypeStruct((N_CHIPS, rows, cols), BF16),
        compiler_params=pltpu.CompilerParams(dimension_semantics=("arbitrary", "arbitrary"), vmem_limit_bytes=VMEM_LIMIT),
    )(core, mine, theirs)


def _pair_sum_small(mine, theirs):
    (m_f2, m_b1, m_b2, m_sf, m_s5, m_sp) = mine

    def body(a0, a1, a2, a3, a4, a5, b0, b1, b2, b3, b4, b5, o_m, o_f, o_5, o_p):
        sm = (a0[...] + a1[...] + a2[...]) + (b0[...] + b1[...] + b2[...])
        sf = a3[...] + b3[...]
        s5 = a4[...] + b4[...]
        for h in range(2):
            o_m[h] = sm[:, h * (D_MODEL // 2):(h + 1) * (D_MODEL // 2)]
            o_f[h] = sf[:, h * (D_FF // 2):(h + 1) * (D_FF // 2)]
            o_5[h] = s5[:, h * (D_CONV // 2):(h + 1) * (D_CONV // 2)]
            for g in range(2):
                o_p[h, g] = a5[2 * h + g] + b5[2 * h + g]

    out_shape = [
        jax.ShapeDtypeStruct((2, 8, D_MODEL // 2), F32), jax.ShapeDtypeStruct((2, 8, D_FF // 2), F32),
        jax.ShapeDtypeStruct((2, 40, D_CONV // 2), F32), jax.ShapeDtypeStruct((2, 2, POOL_GROUP, POOL_GROUP), F32),
    ]
    return pl.pallas_call(body, name="pair_sum_small", out_shape=out_shape, in_specs=[VMEM] * 12, out_specs=[VMEM] * 4)(
        *mine, *theirs)


def _scatter_ops(ins, outs, n_parts, sems, stages, landed=False):
    ici_send, ici_recv, fwd_send, fwd_recv, loc_in, loc_out = sems
    nt = len(ins)
    x, y, c, k, chips = _place()

    def src_of(t, kk):
        return ins[t].at[kk] if t < n_parts else ins[t].at[c]

    def ici(t, j, kk, slot):
        return pltpu.make_async_remote_copy(
            src_ref=src_of(t, kk), dst_ref=outs[t].at[c, slot], send_sem=ici_send.at[t * 3 + j],
            recv_sem=ici_recv.at[t * 3 + j], device_id=(*chips[j], c), device_id_type=MESH)

    def fwd(t, half):
        slots = outs[t].at[half]
        return pltpu.make_async_remote_copy(
            src_ref=slots, dst_ref=slots, send_sem=fwd_send.at[t], recv_sem=fwd_recv.at[t],
            device_id=(x, y, 1 - c), device_id_type=MESH)

    local = [_staged(src_of(t, k), outs[t].at[c, k], stages[t], loc_in.at[t], loc_out.at[t]) for t in range(nt)]
    peers = [(t, j, 2 * qx + qy) for t in range(nt) for j, (qx, qy) in enumerate(chips)]
    sends = [] if landed else [ici(t, j, kq, k) for t, j, kq in peers]

    def start():
        for cp in local:
            cp[0]()
        for cp in sends:
            cp.start()

    def land():
        for cp in local:
            cp[1]()
        if not landed:
            for t, j, kq in peers:
                ici(t, j, kq, kq).wait_recv()
        for cp in local:
            cp[2]()
        for t in range(nt):
            fwd(t, c).start()

    def finish():
        for t in range(nt):
            fwd(t, 1 - c).wait_recv()
            fwd(t, c).wait_send()
        for cp in sends:
            cp.wait_send()

    return start, land, finish


def _scatter_scratch(parts, smalls):
    arrays = tuple(parts) + tuple(smalls)
    nt = len(arrays)
    return ([pltpu.SemaphoreType.DMA((3 * nt,))] * 2 + [pltpu.SemaphoreType.DMA((nt,))] * 4
            + [pltpu.VMEM(a.shape[1:], a.dtype) for a in arrays])


def _scatter_shapes(parts, smalls):
    return [jax.ShapeDtypeStruct((2, N_CHIPS) + p.shape[1:], p.dtype) for p in tuple(parts) + tuple(smalls)]


HBM_SPEC = pl.BlockSpec(memory_space=pltpu.HBM)
SEM_SPEC = pl.BlockSpec(memory_space=pltpu.SEMAPHORE)
EFFECT = pltpu.SideEffectType.DATAFLOW_SIDE_EFFECTING


def _ici_copy(ins, lands, n_parts, send, recv, t, j):
    _, _, c, k, chips = _place()
    qx, qy = chips[j]
    src = ins[t].at[2 * qx + qy] if t < n_parts else ins[t].at[c]
    return pltpu.make_async_remote_copy(
        src_ref=src, dst_ref=lands[t].at[c, k], send_sem=send.at[t * 3 + j], recv_sem=recv.at[t * 3 + j],
        device_id=(qx, qy, c), device_id_type=MESH)


def _scatter_start(parts, smalls):
    arrays = tuple(parts) + tuple(smalls)
    nt = len(arrays)

    def body(*refs):
        ins, lands = refs[:nt], refs[nt:2 * nt]
        send, recv = refs[2 * nt], refs[2 * nt + 1]
        token = refs[-1]
        for t in range(nt):
            for j in range(3):
                _ici_copy(ins, lands, len(parts), send, recv, t, j).start()
        token[...] = jnp.zeros(token.shape, F32)

    land_shapes = _scatter_shapes(parts, smalls)
    out_shape = ([pltpu.SemaphoreType.DMA((3 * nt,))] * 2 + [pltpu.HBM(a.shape, a.dtype) for a in arrays]
                 + [pltpu.HBM(a.shape, a.dtype) for a in land_shapes] + [jax.ShapeDtypeStruct((8, 128), F32)])
    operands = [pltpu.with_memory_space_constraint(a, pltpu.HBM) for a in arrays]
    operands += [pltpu.with_memory_space_constraint(lax.empty(a.shape, a.dtype), pltpu.HBM) for a in land_shapes]
    outs = pl.pallas_call(
        body, name="scatter_start", out_shape=out_shape, in_specs=[HBM_SPEC] * (2 * nt),
        out_specs=[SEM_SPEC] * 2 + [HBM_SPEC] * (2 * nt) + [VMEM],
        input_output_aliases={i: 2 + i for i in range(2 * nt)},
        compiler_params=pltpu.CompilerParams(has_side_effects=EFFECT),
    )(*operands)
    return outs[0], outs[1], outs[2:2 + nt], outs[2 + nt:2 + 2 * nt], outs[-1]


def _scatter_wait(send, recv, ins, lands, n_parts, after):
    nt = len(ins)

    def body(*refs):
        in_refs, land_refs = refs[:nt], refs[nt:2 * nt]
        send_ref, recv_ref = refs[2 * nt], refs[2 * nt + 1]
        for t in range(nt):
            for j in range(3):
                cp = _ici_copy(in_refs, land_refs, n_parts, send_ref, recv_ref, t, j)
                cp.wait_send()
                cp.wait_recv()

    outs = pl.pallas_call(
        body, name="scatter_wait", out_shape=[pltpu.HBM(a.shape, a.dtype) for a in tuple(ins) + tuple(lands)],
        in_specs=[HBM_SPEC] * (2 * nt) + [SEM_SPEC] * 2 + [ANY] * len(after), out_specs=[HBM_SPEC] * (2 * nt),
        input_output_aliases={i: i for i in range(2 * nt)},
        compiler_params=pltpu.CompilerParams(has_side_effects=EFFECT),
    )(*ins, *lands, send, recv, *after)
    return outs[:nt], outs[nt:]


def _scatter_forward(ins, lands, n_parts):
    nt = len(ins)

    def body(*refs):
        start, land, finish = _scatter_ops(
            refs[:nt], refs[2 * nt:3 * nt], n_parts, refs[3 * nt:3 * nt + 6], refs[3 * nt + 6:], landed=True)
        _handshake(SIBLING_ONLY)
        start()
        land()
        finish()

    return pl.pallas_call(
        body, name="scatter_forward", out_shape=[jax.ShapeDtypeStruct(a.shape, a.dtype) for a in lands],
        in_specs=[ANY] * (2 * nt), out_specs=[ANY] * nt, input_output_aliases={nt + i: i for i in range(nt)},
        scratch_shapes=_scatter_scratch(ins[:n_parts], ins[n_parts:]),
        compiler_params=pltpu.CompilerParams(collective_id=SIBLING_ONLY),
    )(*ins, *lands)


def _chip_scatter(parts, smalls):
    nt = len(parts) + len(smalls)

    def body(*refs):
        start, land, finish = _scatter_ops(refs[:nt], refs[nt:2 * nt], len(parts), refs[2 * nt:2 * nt + 6], refs[2 * nt + 6:])
        _handshake(SIBLING_AND_CHIPS)
        start()
        land()
        finish()

    return pl.pallas_call(
        body, name="chip_scatter", out_shape=_scatter_shapes(parts, smalls), in_specs=[ANY] * nt, out_specs=[ANY] * nt,
        scratch_shapes=_scatter_scratch(parts, smalls),
        compiler_params=pltpu.CompilerParams(collective_id=SIBLING_AND_CHIPS),
    )(*parts, *smalls)


def _adamw(w, g, m, v):
    m = ADAM_B1 * m + (1.0 - ADAM_B1) * g
    v = ADAM_B2 * v + (1.0 - ADAM_B2) * (g * g)
    m_hat = m / (1.0 - ADAM_B1 ** ADAM_STEP)
    v_hat = v / (1.0 - ADAM_B2 ** ADAM_STEP)
    delta = -ADAM_LR * (m_hat / (jnp.sqrt(v_hat) + ADAM_EPS) + ADAM_WD * w)
    return delta, m, v


def _adam_big(parts, w, m, v, tag, block_rows, token):
    _, _, rows, cols = parts.shape
    steps = rows // block_rows

    def body(p_ref, w_ref, m_ref, v_ref, token_ref, g_out, d_out, m_out, v_out):
        g = p_ref[0].astype(F32)
        for q in range(1, N_CHIPS):
            g = g + p_ref[q].astype(F32)
        delta, m_new, v_new = _adamw(w_ref[...], g, m_ref[...], v_ref[...])
        g_out[...] = g
        d_out[...] = delta
        m_out[...] = m_new
        v_out[...] = v_new

    blk = pl.BlockSpec((block_rows, cols), lambda h, r: (h * steps + r, 0))
    return pl.pallas_call(
        body, name=f"adam_{tag}", grid=(2, steps),
        in_specs=[pl.BlockSpec((None, N_CHIPS, block_rows, cols), lambda h, r: (h, 0, r, 0)), blk, blk, blk, ANY],
        out_specs=[blk] * 4, out_shape=[jax.ShapeDtypeStruct(w.shape, F32)] * 4,
        compiler_params=pltpu.CompilerParams(dimension_semantics=("arbitrary", "arbitrary"), vmem_limit_bytes=VMEM_LIMIT),
    )(parts, w, m, v, token)


def _reduce_small(l_m, l_f, l_5, l_p):
    def total(ref):
        t = ref[:, 0]
        for q in range(1, N_CHIPS):
            t = t + ref[:, q]
        return t

    def body(m_ref, f_ref, s_ref, p_ref, g1_o, g2_o, g3_o, loss_o, wf_o, fb_o, wa_o, cb_o, lg_o, lb_o, ps_o, pw_o):
        tm, tf, t5, tp = total(m_ref), total(f_ref), total(s_ref), total(p_ref)
        sm = jnp.concatenate([tm[0], tm[1]], axis=1)
        sf = jnp.concatenate([tf[0], tf[1]], axis=1)
        s5 = jnp.concatenate([t5[0], t5[1]], axis=1)
        g1_o[...] = sm[0:1]
        g2_o[...] = sm[1:2]
        g3_o[...] = sm[2:3]
        loss_o[...] = sm[3:4, 0:128]
        wf_o[...] = sf
        fb_o[...] = sf[3:4]
        wa_o[...] = s5[0:32]
        cb_o[...] = s5[32:33]
        lg_o[...] = s5[33:34]
        lb_o[...] = s5[34:35]
        ps_o[...] = s5[35:36]
        for h in range(2):
            for g in range(2):
                pw_o[2 * h + g] = tp[h, g]

    row = lambda w: jax.ShapeDtypeStruct((1, w), F32)
    out_shape = [row(D_MODEL), row(D_MODEL), row(D_MODEL), row(128), jax.ShapeDtypeStruct((8, D_FF), F32), row(D_FF),
                 jax.ShapeDtypeStruct((32, D_CONV), F32), row(D_CONV), row(D_CONV), row(D_CONV), row(D_POOL),
                 jax.ShapeDtypeStruct((4, POOL_GROUP, POOL_GROUP), F32)]
    return pl.pallas_call(body, name="reduce_small", out_shape=out_shape, in_specs=[VMEM] * 4, out_specs=[VMEM] * 12)(
        l_m, l_f, l_5, l_p)


def _adam_small(ws, gs, ms, vs):
    count = len(ws)

    def body(*refs):
        w_r, g_r, m_r, v_r = (refs[t * count:(t + 1) * count] for t in range(4))
        d_o, m_o, v_o = (refs[(4 + t) * count:(5 + t) * count] for t in range(3))
        for t in range(count):
            delta, m_new, v_new = _adamw(w_r[t][...], g_r[t][...], m_r[t][...], v_r[t][...])
            d_o[t][...] = delta
            m_o[t][...] = m_new
            v_o[t][...] = v_new

    out_shape = [jax.ShapeDtypeStruct(w.shape, F32) for w in ws] * 3
    outs = pl.pallas_call(body, name="adam_small", out_shape=out_shape, in_specs=[VMEM] * (4 * count),
                          out_specs=[VMEM] * (3 * count))(*ws, *gs, *ms, *vs)
    return outs[:count], outs[count:2 * count], outs[2 * count:]


MIX_TILE = 512
FFN_TILE = 256
GRAD_K = 2048


def kernel(x, norm_mix_g, w_in, conv_a_w, conv_a_b, ln_a_g, ln_a_b, pool_w, pool_scale, w_out, norm_ffn_g, w_up, conv_f_w, conv_f_b, w_down, norm_final_g, loss_target, m_norm_mix_g, m_w_in, m_conv_a_w, m_conv_a_b, m_ln_a_g, m_ln_a_b, m_pool_w, m_pool_scale, m_w_out, m_norm_ffn_g, m_w_up, m_conv_f_w, m_conv_f_b, m_w_down, m_norm_final_g, v_norm_mix_g, v_w_in, v_conv_a_w, v_conv_a_b, v_ln_a_g, v_ln_a_b, v_pool_w, v_pool_scale, v_w_out, v_norm_ffn_g, v_w_up, v_conv_f_w, v_conv_f_b, v_w_down, v_norm_final_g):
    seq = x.shape[1]
    xs, ts = x[0], loss_target[0]
    mix_tile, ffn_tile, grad_k = min(MIX_TILE, seq), min(FFN_TILE, seq), min(GRAD_K, seq)
    chip = 2 * lax.axis_index("x") + lax.axis_index("y")
    core = lax.axis_index("c").astype(jnp.int32).reshape(1)

    wa_s = jnp.pad(conv_a_w[0], ((0, 32 - CONV_A), (0, 0)))
    wf_s = jnp.pad(conv_f_w[0], ((0, 8 - CONV_F), (0, 0)))
    win_b, wout_b, wup_b, wdown_b = _cast_shards(w_in[0], w_out[0], w_up[0], w_down[0])
    g3 = norm_final_g.reshape(1, D_MODEL)
    pw = pool_w[0]

    h1, proj, cpre, dpool, mcat, x1, win, wout, wup, wa_g, wf_g = _mixer_fwd(
        xs, norm_mix_g, win_b, wout_b, wup_b, wa_s, wf_s, conv_a_b, ln_a_g, ln_a_b, pw, pool_scale, mix_tile)
    wa = jnp.transpose(wa_g, (1, 0, 2)).reshape(32, D_CONV)
    wf = jnp.transpose(wf_g, (1, 0, 2)).reshape(8, D_FF)
    h2, up, gcs, act, wdown = _ffn_up(x1, norm_ffn_g, wup, wf, conv_f_b, wdown_b, ffn_tile)
    dx2b, sm_f2 = _ffn_down(x1, act, wdown, g3, ts, mix_tile)
    tags = ("w_in", "w_out", "w_up", "w_down")
    blocks = (256, 128, 256, 176)
    g_wdown = _weight_grad(act, dx2b, "rows2", grad_k)
    dup, dx1b, sm_b1, sf, l_wdown = _ffn_bwd(
        dx2b, up, gcs, x1, norm_ffn_g, wup, wf, wdown, ("exchange", [g_wdown]), ffn_tile)
    p_wdown = _pair_sum(core, g_wdown, l_wdown, tags[3], g_wdown.shape[2])
    g_wup, s_wdown = _weight_grad(h2, dup, "cols_chip", grad_k, ("scatter", [p_wdown]))
    g_wout, l_wup = _weight_grad(mcat, dx1b, "rows1", grad_k, ("exchange", [g_wup]))
    p_wup = _pair_sum(core, g_wup, l_wup, tags[2], g_wup.shape[2])
    l_wout, = _sibling_exchange((g_wout,), (), "early")
    p_wout = _pair_sum(core, g_wout, l_wout, tags[1], g_wout.shape[2])
    dproj, gx, sm_b2, s5, sp, s_wout, s_wup = _mixer_bwd(
        dx1b, xs, proj, cpre, dpool, norm_mix_g, win, wa, ln_a_g, ln_a_b, pw, pool_scale, wout, [p_wout, p_wup], mix_tile)
    g_win, grad_x = _weight_grad(h1, dproj, "cols_half", grad_k, carry=gx)

    smalls = (sm_f2, sm_b1, sm_b2, sf, s5, sp)
    landed = _sibling_exchange((g_win,), smalls, "late")
    part_win = _pair_sum(core, g_win, landed[0], tags[0], g_win.shape[2])
    small_parts = _pair_sum_small(smalls, landed[1:])
    send, recv, late_src, late_land, token = _scatter_start([part_win], small_parts)
    big_w = (w_in[0], w_out[0], w_up[0], w_down[0])
    big_m = (m_w_in[0], m_w_out[0], m_w_up[0], m_w_down[0])
    big_v = (v_w_in[0], v_w_out[0], v_w_up[0], v_w_down[0])
    big = {}
    for t, p in ((1, s_wout), (2, s_wup), (3, s_wdown)):
        big[tags[t]] = _adam_big(p, big_w[t], big_m[t], big_v[t], tags[t], blocks[t], token)
    late_src, late_land = _scatter_wait(send, recv, late_src, late_land, 1, [big[tags[t]][3] for t in (1, 2, 3)])
    late = _scatter_forward(late_src, late_land, 1)
    big[tags[0]] = _adam_big(late[0], big_w[0], big_m[0], big_v[0], tags[0], blocks[0], token)
    big = {tag: [a[None] for a in outs] for tag, outs in big.items()}
    scattered = [None] * 4 + list(late[1:])

    (g_g1, g_g2, g_g3, loss_row, g_wf_all, g_fb, g_wa_all, g_cb, g_lg, g_lb, g_ps, g_pw) = _reduce_small(*scattered[4:])
    g_wa = lax.dynamic_slice(g_wa_all, (0, chip * (D_CONV // N_CHIPS)), (32, D_CONV // N_CHIPS))[:CONV_A]
    g_wf = lax.dynamic_slice(g_wf_all, (0, chip * (D_FF // N_CHIPS)), (8, D_FF // N_CHIPS))[:CONV_F]
    small_names = ("norm_mix_g", "conv_a_w", "conv_a_b", "ln_a_g", "ln_a_b", "pool_w", "pool_scale", "norm_ffn_g",
                   "conv_f_w", "conv_f_b", "norm_final_g")
    small_w = (norm_mix_g, conv_a_w[0], conv_a_b, ln_a_g, ln_a_b, pw, pool_scale, norm_ffn_g, conv_f_w[0], conv_f_b, g3)
    small_m = (m_norm_mix_g, m_conv_a_w[0], m_conv_a_b, m_ln_a_g, m_ln_a_b, m_pool_w[0], m_pool_scale, m_norm_ffn_g,
               m_conv_f_w[0], m_conv_f_b, m_norm_final_g.reshape(1, D_MODEL))
    small_v = (v_norm_mix_g, v_conv_a_w[0], v_conv_a_b, v_ln_a_g, v_ln_a_b, v_pool_w[0], v_pool_scale, v_norm_ffn_g,
               v_conv_f_w[0], v_conv_f_b, v_norm_final_g.reshape(1, D_MODEL))
    small_g = (g_g1, g_wa, g_cb, g_lg, g_lb, g_pw, g_ps, g_g2, g_wf, g_fb, g_g3)
    s_delta, s_m, s_v = _adam_small(small_w, small_g, small_m, small_v)
    shapes = {"conv_a_w": conv_a_w.shape, "pool_w": pool_w.shape, "conv_f_w": conv_f_w.shape, "norm_final_g": norm_final_g.shape}
    small = {}
    for t, name in enumerate(small_names):
        shp = shapes.get(name)
        small[name] = [a if shp is None else a.reshape(shp) for a in (small_g[t], s_delta[t], s_m[t], s_v[t])]

    order = ("norm_mix_g", "w_in", "conv_a_w", "conv_a_b", "ln_a_g", "ln_a_b", "pool_w", "pool_scale", "w_out", "norm_ffn_g",
             "w_up", "conv_f_w", "conv_f_b", "w_down", "norm_final_g")
    table = {**big, **small}
    loss = loss_row[0, 0]
    outs = [loss, grad_x[None]]
    for t in range(4):
        outs += [table[name][t] for name in order]
    return tuple(outs)
```

```python
import functools

import jax
import jax.numpy as jnp
from jax import lax
from jax.experimental import pallas as pl
from jax.experimental.pallas import tpu as pltpu

F32 = jnp.float32
BF16 = jnp.bfloat16
EPS = 1e-6
ADAM_LR = 0.001
ADAM_B1 = 0.9
ADAM_B2 = 0.999
ADAM_EPS = 1e-08
ADAM_WD = 0.01
ADAM_STEP = 10

D_MODEL = 1024
D_CONV = 512
D_POOL = 512
D_IN = 1536
D_FF = 2816
CONV_A = 31
CONV_F = 3
POOL_WINDOWS = (2, 4, 8, 16)
POOL_GROUP = 128
N_CHIPS = 4
FF_CHUNK = 256
N_FF_CHUNKS = D_FF // FF_CHUNK
UP_CHUNK = 2816
A_HALO = 32
P_HALO = 16
VMEM_LIMIT = 56 * 1024 * 1024
MESH = pl.DeviceIdType.MESH

ANY = pl.BlockSpec(memory_space=pl.ANY)
VMEM = pl.BlockSpec(memory_space=pltpu.VMEM)


def _dot(a, b):
    return jnp.dot(a, b, preferred_element_type=F32)


def _dot_nt(a, b):
    return lax.dot_general(a, b, (((1,), (1,)), ((), ())), preferred_element_type=F32)


def _dot_tn(a, b):
    return lax.dot_general(a, b, (((0,), (0,)), ((), ())), preferred_element_type=F32)


def _sigmoid(v):
    return jax.nn.sigmoid(v)


def _colsum(v):
    return jnp.sum(v, axis=0, keepdims=True)


def _rowmean(v):
    return jnp.mean(v, axis=-1, keepdims=True)


def _place():
    x, y, c = lax.axis_index("x"), lax.axis_index("y"), lax.axis_index("c")
    chips = [(1 - x, y), (x, 1 - y), (1 - x, 1 - y)]
    return x, y, c, 2 * x + y, chips


SIBLING_ONLY, SIBLING_AND_CHIPS = 0, 1


def _handshake(collective):
    x, y, c, _, chips = _place()
    peers = [(x, y, 1 - c)] + ([(*chip, c) for chip in chips] if collective == SIBLING_AND_CHIPS else [])
    barrier = pltpu.get_barrier_semaphore()
    for peer in peers:
        pl.semaphore_signal(barrier, inc=1, device_id=peer, device_id_type=MESH)
    pl.semaphore_wait(barrier, len(peers))


def _staged(src, dst, stage, sem_in, sem_out):
    hop_in = pltpu.make_async_copy(src, stage, sem_in)
    hop_out = pltpu.make_async_copy(stage, dst, sem_out)

    def relay():
        hop_in.wait()
        hop_out.start()

    return hop_in.start, relay, hop_out.wait


def _gather_ops(bufs, fulls, col_sharded, sems, stages):
    ici_send, ici_recv, fwd_send, fwd_recv, loc_in, loc_out = sems
    n_big = len(bufs)
    x, y, c, k, chips = _place()

    def block(i, kk, half=None):
        rows, cols = bufs[i].shape
        if col_sharded[i]:
            rs = slice(None) if half is None else pl.ds(pl.multiple_of(half * (rows // 2), 16), rows // 2)
            return fulls[i].at[rs, pl.ds(pl.multiple_of(kk * cols, 128), cols)]
        if half is None:
            return fulls[i].at[pl.ds(pl.multiple_of(kk * rows, 16), rows), :]
        return fulls[i].at[pl.ds(pl.multiple_of(kk * rows + half * (rows // 2), 16), rows // 2), :]

    def my_half(i):
        rows = bufs[i].shape[0]
        return bufs[i].at[pl.ds(pl.multiple_of(c * (rows // 2), 16), rows // 2), :]

    def ici(i, j, kk):
        return pltpu.make_async_remote_copy(
            src_ref=my_half(i), dst_ref=block(i, kk, c), send_sem=ici_send.at[i * 3 + j], recv_sem=ici_recv.at[i * 3 + j],
            device_id=(*chips[j], c), device_id_type=MESH)

    def fwd(i, j, kk, half):
        return pltpu.make_async_remote_copy(
            src_ref=block(i, kk, half), dst_ref=block(i, kk, half), send_sem=fwd_send.at[i * 3 + j],
            recv_sem=fwd_recv.at[i * 3 + j], device_id=(x, y, 1 - c), device_id_type=MESH)

    local = [_staged(bufs[i], block(i, k), stages[i], loc_in.at[i], loc_out.at[i]) for i in range(n_big)]
    sends = [ici(i, j, k) for i in range(n_big) for j in range(3)]
    peers = [(i, j, 2 * qx + qy) for i in range(n_big) for j, (qx, qy) in enumerate(chips)]

    def start():
        for cp in local:
            cp[0]()
        for cp in sends:
            cp.start()

    def land():
        for cp in local:
            cp[1]()
        for i, j, kq in peers:
            ici(i, j, kq).wait_recv()
            fwd(i, j, kq, c).start()

    def finish():
        for i, j, kq in peers:
            fwd(i, j, kq, 1 - c).wait_recv()
            fwd(i, j, kq, c).wait_send()
        for cp in sends:
            cp.wait_send()
        for cp in local:
            cp[2]()

    return start, land, finish


def _gather_scratch(shards):
    n_big = len(shards)
    return ([pltpu.SemaphoreType.DMA((3 * n_big,))] * 4 + [pltpu.SemaphoreType.DMA((n_big,))] * 2
            + [pltpu.VMEM(b.shape, b.dtype) for b in shards])


def _tap_ops(srcs, dsts, sems):
    send, recv, loc = sems
    _, _, c, k, chips = _place()

    def copy(t, j, kk):
        return pltpu.make_async_remote_copy(
            src_ref=srcs[t], dst_ref=dsts[t].at[kk], send_sem=send.at[t * 3 + j], recv_sem=recv.at[t * 3 + j],
            device_id=(*chips[j], c), device_id_type=MESH)

    local = [pltpu.make_async_copy(srcs[t], dsts[t].at[k], loc.at[t]) for t in range(len(srcs))]
    sends = [[copy(t, j, k) for j in range(3)] for t in range(len(srcs))]

    def start():
        for t, cp in enumerate(local):
            cp.start()
            for sd in sends[t]:
                sd.start()

    def wait(t):
        for j, (qx, qy) in enumerate(chips):
            copy(t, j, 2 * qx + qy).wait_recv()
        for sd in sends[t]:
            sd.wait_send()
        local[t].wait()

    return start, wait


def _cast_shards(*shards):
    def body(*refs):
        for src, dst in zip(refs[:len(shards)], refs[len(shards):]):
            dst[...] = src[...].astype(BF16)

    return pl.pallas_call(
        body, name="cast_shards", out_shape=[jax.ShapeDtypeStruct(s.shape, BF16) for s in shards],
        in_specs=[VMEM] * len(shards), out_specs=[VMEM] * len(shards),
        compiler_params=pltpu.CompilerParams(vmem_limit_bytes=VMEM_LIMIT),
    )(*shards)


def _load_weights(pairs, sem, first=0):
    cps = [pltpu.make_async_copy(src, dst, sem.at[first + i]) for i, (src, dst) in enumerate(pairs)]
    for cp in cps:
        cp.start()
    for cp in cps:
        cp.wait()


def _shifted_views(buf, shifted, t_rows):
    n = t_rows + A_HALO - 8
    for b in range(1, 8):
        shifted[b - 1] = buf[b:b + n, :]

    def view(offset):
        a, b = divmod(offset, 8)
        if b == 0:
            return buf[8 * a:8 * a + t_rows, :]
        return shifted[b - 1, 8 * a:8 * a + t_rows, :]

    return view


def _pool_count(tile, t_rows, w):
    row = lax.broadcasted_iota(jnp.int32, (t_rows, POOL_GROUP), 0) + tile * t_rows
    return jnp.minimum(row + 1, w).astype(F32)


def _mixer_fwd(x, g1, win_b, wout_b, wup_b, wa_s, wf_s, cb, lg, lb, pw, ps, tile_rows):
    seq = x.shape[0]
    tr = tile_rows
    n = seq // tr

    def body(x_ref, g1_ref, win_b_hbm, wout_b_hbm, wup_b_hbm, wa_s_hbm, wf_s_hbm, cb_ref, lg_ref, lb_ref, pw_ref,
             ps_ref, h1_ref, proj_ref, c_ref, d_ref, m_ref, x1_ref, win_f, wout_f, wup_f, wa_g, wf_g,
             win_v, wout_v, wa_ref, ubuf, ushift, bbuf, sem, *csems):
        i = pl.program_id(0)
        first_sems, first_stages, second_sems, second_stages, later_sems, later_stages, tap_sems = (
            csems[0:6], csems[6:7], csems[7:13], csems[13:14], csems[14:20], csems[20:21], csems[21:24])

        def first():
            return _gather_ops((win_b_hbm,), (win_f,), (True,), first_sems, first_stages)

        def second():
            return _gather_ops((wout_b_hbm,), (wout_f,), (False,), second_sems, second_stages)

        def later():
            return _gather_ops((wup_b_hbm,), (wup_f,), (True,), later_sems, later_stages)

        def taps():
            return _tap_ops((wa_s_hbm, wf_s_hbm), (wa_g, wf_g), tap_sems)

        @pl.when(i == 0)
        def _():
            _handshake(SIBLING_AND_CHIPS)
            first()[0]()
            taps()[0]()
            second()[0]()
            later()[0]()
            first()[1]()
            first()[2]()
            _load_weights([(win_f, win_v)], sem)
            ubuf[0:A_HALO, :] = jnp.zeros((A_HALO, D_CONV), F32)
            bbuf[0:P_HALO, :] = jnp.zeros((P_HALO, D_POOL), F32)

        xv = x_ref[...]
        r = lax.rsqrt(_rowmean(xv * xv) + EPS)
        h1 = (xv * r * g1_ref[...]).astype(BF16)
        h1_ref[...] = h1
        proj = _dot(h1, win_v[...])
        proj_ref[...] = proj.astype(BF16)

        @pl.when(i == 0)
        def _():
            taps()[1](0)
            _load_weights([(wa_g.at[kk], wa_ref.at[:, kk * (D_CONV // N_CHIPS):(kk + 1) * (D_CONV // N_CHIPS)])
                           for kk in range(N_CHIPS)], sem, 2)

        av, ag, bi = proj[:, :D_CONV], proj[:, D_CONV:2 * D_CONV], proj[:, 2 * D_CONV:]
        ubuf[A_HALO:A_HALO + tr, :] = av * _sigmoid(ag)
        off = A_HALO - (CONV_A - 1)
        uview = _shifted_views(ubuf, ushift, tr)
        acc = wa_ref[0:1, :] * uview(off)
        for j in range(1, CONV_A):
            acc = acc + wa_ref[j:j + 1, :] * uview(off + j)
        cv = acc + cb_ref[...]
        ubuf[0:A_HALO, :] = ubuf[tr:tr + A_HALO, :]
        c_ref[...] = cv.astype(BF16)
        xc = cv - _rowmean(cv)
        z = xc * lax.rsqrt(_rowmean(xc * xc) + EPS)
        ln = z * lg_ref[...] + lb_ref[...]
        ya = ln * _sigmoid(ln)
        bbuf[P_HALO:P_HALO + tr, :] = bi
        ds, ybs = [], []
        for g, w in enumerate(POOL_WINDOWS):
            cols = slice(g * POOL_GROUP, (g + 1) * POOL_GROUP)
            s = bi[:, cols]
            for kk in range(1, w):
                s = s + bbuf[P_HALO - kk:P_HALO - kk + tr, cols]
            dg = s / _pool_count(i, tr, w) - bi[:, cols]
            ds.append(dg)
            ybs.append(_dot(dg.astype(BF16), pw_ref[g].astype(BF16)))
        bbuf[0:P_HALO, :] = bbuf[tr:tr + P_HALO, :]
        d_ref[...] = jnp.concatenate(ds, axis=1).astype(BF16)
        yb = jnp.concatenate(ybs, axis=1) * ps_ref[...]
        m = jnp.concatenate([ya, yb], axis=1).astype(BF16)
        m_ref[...] = m

        @pl.when(i == 0)
        def _():
            second()[1]()
            second()[2]()
            _load_weights([(wout_f, wout_v)], sem, 1)

        x1_ref[...] = xv + _dot(m, wout_v[...])

        @pl.when(i == n - 1)
        def _():
            later()[1]()
            later()[2]()
            taps()[1](1)

    tile = lambda w: pl.BlockSpec((tr, w), lambda i: (i, 0))
    full = lambda a: pl.BlockSpec(a.shape, lambda i: (0,) * a.ndim)
    return pl.pallas_call(
        body, name="mixer_fwd", grid=(n,),
        in_specs=[tile(D_MODEL), full(g1)] + [ANY] * 5 + [full(cb), full(lg), full(lb), full(pw), full(ps)],
        out_specs=[tile(D_MODEL), tile(D_IN), tile(D_CONV), tile(D_POOL), tile(D_MODEL), tile(D_MODEL)] + [ANY] * 5,
        out_shape=[
            jax.ShapeDtypeStruct((seq, D_MODEL), BF16), jax.ShapeDtypeStruct((seq, D_IN), BF16),
            jax.ShapeDtypeStruct((seq, D_CONV), BF16), jax.ShapeDtypeStruct((seq, D_POOL), BF16),
            jax.ShapeDtypeStruct((seq, D_MODEL), BF16), jax.ShapeDtypeStruct((seq, D_MODEL), F32),
            jax.ShapeDtypeStruct((D_MODEL, D_IN), BF16), jax.ShapeDtypeStruct((D_MODEL, D_MODEL), BF16),
            jax.ShapeDtypeStruct((D_MODEL, 2 * D_FF), BF16),
            jax.ShapeDtypeStruct((N_CHIPS,) + wa_s.shape, F32), jax.ShapeDtypeStruct((N_CHIPS,) + wf_s.shape, F32),
        ],
        scratch_shapes=[
            pltpu.VMEM((D_MODEL, D_IN), BF16), pltpu.VMEM((D_MODEL, D_MODEL), BF16), pltpu.VMEM((32, D_CONV), F32),
            pltpu.VMEM((tr + A_HALO, D_CONV), F32), pltpu.VMEM((7, tr + A_HALO - 8, D_CONV), F32),
            pltpu.VMEM((tr + P_HALO, D_POOL), F32), pltpu.SemaphoreType.DMA((2 + N_CHIPS,)),
        ] + _gather_scratch((win_b,)) + _gather_scratch((wout_b,)) + _gather_scratch((wup_b,)) + [
            pltpu.SemaphoreType.DMA((6,)), pltpu.SemaphoreType.DMA((6,)), pltpu.SemaphoreType.DMA((2,))],
        compiler_params=pltpu.CompilerParams(dimension_semantics=("arbitrary",), vmem_limit_bytes=VMEM_LIMIT,
                                             collective_id=SIBLING_AND_CHIPS),
    )(x, g1, win_b, wout_b, wup_b, wa_s, wf_s, cb, lg, lb, pw, ps)


def _ffn_up(x1, g2, wup, wf, fb, wdown_b, tile_rows):
    seq = x1.shape[0]
    tr = tile_rows
    n = seq // tr

    def body(x1_ref, g2_ref, wup_hbm, wf_ref, fb_ref, wdown_b_hbm,
             h2_ref, up_ref, gc_ref, act_ref, wdown_f, wup_v, gbuf, sem, *gsems):
        i = pl.program_id(0)

        def gather():
            return _gather_ops((wdown_b_hbm,), (wdown_f,), (False,), gsems[:6], gsems[6:])

        @pl.when(i == 0)
        def _():
            _handshake(SIBLING_AND_CHIPS)
            gather()[0]()
            _load_weights(((wup_hbm, wup_v),), sem)
            gbuf[0:8, :] = jnp.zeros((8, D_FF), F32)

        x1v = x1_ref[...]
        r2 = lax.rsqrt(_rowmean(x1v * x1v) + EPS)
        h2 = (x1v * r2 * g2_ref[...]).astype(BF16)
        h2_ref[...] = h2

        def up_proj(j):
            return (_dot(h2, wup_v[:, j * UP_CHUNK:(j + 1) * UP_CHUNK]),
                    _dot(h2, wup_v[:, D_FF + j * UP_CHUNK:D_FF + (j + 1) * UP_CHUNK]))

        ahead = up_proj(0)
        for j in range(D_FF // UP_CHUNK):
            cs = slice(j * UP_CHUNK, (j + 1) * UP_CHUNK)
            vs = slice(D_FF + j * UP_CHUNK, D_FF + (j + 1) * UP_CHUNK)
            gate, val = ahead
            if j + 1 < D_FF // UP_CHUNK:
                ahead = up_proj(j + 1)
            up_ref[:, cs] = gate.astype(BF16)
            up_ref[:, vs] = val.astype(BF16)
            gbuf[8:8 + tr, cs] = gate
            gc = (wf_ref[0:1, cs] * gbuf[6:6 + tr, cs] + wf_ref[1:2, cs] * gbuf[7:7 + tr, cs]
                  + wf_ref[2:3, cs] * gate + fb_ref[:, cs])
            gbuf[0:8, cs] = gbuf[tr:tr + 8, cs]
            gc_ref[:, cs] = gc.astype(BF16)
            act_ref[:, cs] = (gc * _sigmoid(gc) * val).astype(BF16)

        @pl.when(i == max(n - 2, 0))
        def _():
            gather()[1]()

        @pl.when(i == n - 1)
        def _():
            gather()[2]()

    tile = lambda w: pl.BlockSpec((tr, w), lambda i: (i, 0))
    full = lambda a: pl.BlockSpec(a.shape, lambda i: (0,) * a.ndim)
    return pl.pallas_call(
        body, name="ffn_up", grid=(n,),
        in_specs=[tile(D_MODEL), full(g2), ANY, full(wf), full(fb), ANY],
        out_specs=[tile(D_MODEL), tile(2 * D_FF), tile(D_FF), tile(D_FF), ANY],
        out_shape=[
            jax.ShapeDtypeStruct((seq, D_MODEL), BF16), jax.ShapeDtypeStruct((seq, 2 * D_FF), BF16),
            jax.ShapeDtypeStruct((seq, D_FF), BF16), jax.ShapeDtypeStruct((seq, D_FF), BF16),
            jax.ShapeDtypeStruct((D_FF, D_MODEL), BF16),
        ],
        scratch_shapes=[pltpu.VMEM(wup.shape, BF16), pltpu.VMEM((tr + 8, D_FF), F32), pltpu.SemaphoreType.DMA((1,))]
        + _gather_scratch((wdown_b,)),
        compiler_params=pltpu.CompilerParams(dimension_semantics=("arbitrary",), vmem_limit_bytes=VMEM_LIMIT,
                                             collective_id=SIBLING_AND_CHIPS),
    )(x1, g2, wup, wf, fb, wdown_b)


def _ffn_down(x1, act, wdown, g3, target, tile_rows):
    seq = x1.shape[0]
    tr = tile_rows
    n = seq // tr

    def body(x1_ref, act_ref, wdown_hbm, g3_ref, t_ref, dx2b_ref, sm_ref, wdown_v, sem):
        i = pl.program_id(0)

        @pl.when(i == 0)
        def _():
            _load_weights(((wdown_hbm, wdown_v),), sem)
            sm_ref[...] = jnp.zeros(sm_ref.shape, F32)

        x2 = x1_ref[...] + _dot(act_ref[...], wdown_v[...])
        r3 = lax.rsqrt(_rowmean(x2 * x2) + EPS)
        n3 = x2 * r3
        err = n3 * g3_ref[...] - t_ref[...]
        dy = err / D_MODEL
        sm_ref[2:3, :] += _colsum(dy * n3)
        loss = 0.5 * _colsum(_rowmean(err * err))
        sm_ref[3:4, :] += jnp.broadcast_to(loss, (1, D_MODEL))
        dn = dy * g3_ref[...]
        dx2b_ref[...] = (r3 * (dn - n3 * _rowmean(dn * n3))).astype(BF16)

    tile = lambda w: pl.BlockSpec((tr, w), lambda i: (i, 0))
    full = lambda a: pl.BlockSpec(a.shape, lambda i: (0,) * a.ndim)
    return pl.pallas_call(
        body, name="ffn_down", grid=(n,),
        in_specs=[tile(D_MODEL), tile(D_FF), ANY, full(g3), tile(D_MODEL)],
        out_specs=[tile(D_MODEL), pl.BlockSpec((8, D_MODEL), lambda i: (0, 0))],
        out_shape=[jax.ShapeDtypeStruct((seq, D_MODEL), BF16), jax.ShapeDtypeStruct((8, D_MODEL), F32)],
        scratch_shapes=[pltpu.VMEM(wdown.shape, BF16), pltpu.SemaphoreType.DMA((1,))],
        compiler_params=pltpu.CompilerParams(dimension_semantics=("arbitrary",), vmem_limit_bytes=VMEM_LIMIT),
    )(x1, act, wdown, g3, target)


def _ffn_bwd(dx2, up, gcs, x1, g2, wup, wf, wdown, comm, tile_rows):
    seq = x1.shape[0]
    c_ins, c_shapes, c_sems, c_ops, c_id = _comm_plan(comm)
    nc = len(c_ins)
    tr = tile_rows
    n = seq // tr

    def body(dx2_ref, up_ref, gc_ref, x1_ref, g2_ref, wup_hbm, wf_ref, wdown_hbm, *rest):
        c_in, rest = rest[:nc], rest[nc:]
        dup_ref, dx1b_ref, sm_ref, sf_ref = rest[:4]
        c_out, rest = rest[4:4 + nc], rest[4 + nc:]
        wup_v, wdown_v, dbuf, dcar, sem = rest[:5]
        c_sem_refs = rest[5:]
        i = pl.program_id(0)

        @pl.when(i == 0)
        def _():
            c_ops(c_in, c_out, c_sem_refs)[0]()
            _load_weights(((wup_hbm, wup_v), (wdown_hbm, wdown_v)), sem)
            dcar[...] = jnp.zeros(dcar.shape, F32)
            sm_ref[...] = jnp.zeros(sm_ref.shape, F32)
            sf_ref[...] = jnp.zeros(sf_ref.shape, F32)

        dx2b = dx2_ref[...]
        dx2v = dx2b.astype(F32)
        dh2 = jnp.zeros((tr, D_MODEL), F32)

        def down_t(j):
            return _dot_nt(dx2b, wdown_v[j * FF_CHUNK:(j + 1) * FF_CHUNK, :])

        ahead = down_t(0)
        for j in range(N_FF_CHUNKS):
            cs = slice(j * FF_CHUNK, (j + 1) * FF_CHUNK)
            vs = slice(D_FF + j * FF_CHUNK, D_FF + (j + 1) * FF_CHUNK)
            dact = ahead
            if j + 1 < N_FF_CHUNKS:
                ahead = down_t(j + 1)
            gate = up_ref[:, cs].astype(F32)
            val = up_ref[:, vs].astype(F32)
            gc = gc_ref[:, cs].astype(F32)
            sg = _sigmoid(gc)
            dval = dact * (gc * sg)
            dgc = dact * val * (sg * (1.0 + gc * (1.0 - sg)))
            dbuf[0:tr, :] = dgc
            dbuf[tr:tr + 8, :] = dcar[:, cs]
            d_p1 = dbuf[1:1 + tr, :]
            d_p2 = dbuf[2:2 + tr, :]
            dgate = wf_ref[2:3, cs] * dgc + wf_ref[1:2, cs] * d_p1 + wf_ref[0:1, cs] * d_p2
            dcar[:, cs] = dgc[0:8, :]
            sf_ref[0:1, cs] += _colsum(d_p2 * gate)
            sf_ref[1:2, cs] += _colsum(d_p1 * gate)
            sf_ref[2:3, cs] += _colsum(dgc * gate)
            sf_ref[3:4, cs] += _colsum(dgc)
            dgb, dvb = dgate.astype(BF16), dval.astype(BF16)
            dup_ref[:, cs] = dgb
            dup_ref[:, vs] = dvb
            dh2 = dh2 + _dot_nt(dgb, wup_v[:, cs]) + _dot_nt(dvb, wup_v[:, vs])
        x1v = x1_ref[...]
        r2 = lax.rsqrt(_rowmean(x1v * x1v) + EPS)
        n2 = x1v * r2
        sm_ref[1:2, :] += _colsum(dh2 * n2)
        dn2 = dh2 * g2_ref[...]
        dx1b_ref[...] = (dx2v + r2 * (dn2 - n2 * _rowmean(dn2 * n2))).astype(BF16)

        @pl.when(i == n - 1)
        def _():
            c_ops(c_in, c_out, c_sem_refs)[2]()

    tile = lambda w: pl.BlockSpec((tr, w), lambda i: (n - 1 - i, 0))
    full = lambda a: pl.BlockSpec(a.shape, lambda i: (0,) * a.ndim)
    acc = lambda rows, w: pl.BlockSpec((rows, w), lambda i: (0, 0))
    return pl.pallas_call(
        body, name="ffn_bwd", grid=(n,),
        in_specs=[tile(D_MODEL), tile(2 * D_FF), tile(D_FF), tile(D_MODEL), full(g2), ANY, full(wf), ANY] + [ANY] * nc,
        out_specs=[tile(2 * D_FF), tile(D_MODEL), acc(8, D_MODEL), acc(8, D_FF)] + [ANY] * nc,
        out_shape=[
            jax.ShapeDtypeStruct((seq, 2 * D_FF), BF16), jax.ShapeDtypeStruct((seq, D_MODEL), BF16),
            jax.ShapeDtypeStruct((8, D_MODEL), F32), jax.ShapeDtypeStruct((8, D_FF), F32),
        ] + c_shapes,
        scratch_shapes=[
            pltpu.VMEM(wup.shape, BF16), pltpu.VMEM(wdown.shape, BF16),
            pltpu.VMEM((tr + 8, FF_CHUNK), F32), pltpu.VMEM((8, D_FF), F32), pltpu.SemaphoreType.DMA((2,)),
        ] + c_sems,
        compiler_params=pltpu.CompilerParams(dimension_semantics=("arbitrary",), vmem_limit_bytes=VMEM_LIMIT,
                                             collective_id=c_id),
    )(dx2, up, gcs, x1, g2, wup, wf, wdown, *c_ins)


def _mixer_bwd(dx1, x, proj, cpre, d, g1, win, wa, lg, lb, pw, ps, wout, parts, tile_rows):
    seq = x.shape[0]
    n_parts = len(parts)
    tr = tile_rows
    n = seq // tr
    row_cb, row_lg, row_lb, row_ps = 32, 33, 34, 35

    def body(dx1_ref, x_ref, proj_ref, projh_ref, c_ref, d_ref, g1_ref, win_hbm, wa_ref, lg_ref, lb_ref, pw_ref, ps_ref,
             wout_hbm, *rest):
        part_refs, rest = rest[:n_parts], rest[n_parts:]
        dproj_ref, gx_ref, sm_ref, s5_ref, sp_ref = rest[:5]
        land_refs, rest = rest[5:5 + n_parts], rest[5 + n_parts:]
        win_v, wout_v, ubuf, ushift, dcbuf, dshift, ebuf, sem = rest[:8]
        ssems = rest[8:]
        i = pl.program_id(0)
        tile = n - 1 - i

        def scatter():
            return _scatter_ops(part_refs, land_refs, n_parts, ssems[:6], ssems[6:])

        @pl.when(i == 0)
        def _():
            _handshake(SIBLING_AND_CHIPS)
            scatter()[0]()
            _load_weights(((win_hbm, win_v), (wout_hbm, wout_v)), sem)
            dcbuf[tr:tr + A_HALO, :] = jnp.zeros((A_HALO, D_CONV), F32)
            ebuf[tr:tr + P_HALO, :] = jnp.zeros((P_HALO, D_POOL), F32)
            sm_ref[...] = jnp.zeros(sm_ref.shape, F32)
            s5_ref[...] = jnp.zeros(s5_ref.shape, F32)
            sp_ref[...] = jnp.zeros(sp_ref.shape, F32)

        dx1b = dx1_ref[...]
        dx1v = dx1b.astype(F32)
        dm = _dot_nt(dx1b, wout_v[...])
        dya, dyb = dm[:, :D_CONV], dm[:, D_CONV:]
        dbis = []
        for g, w in enumerate(POOL_WINDOWS):
            cols = slice(g * POOL_GROUP, (g + 1) * POOL_GROUP)
            dgb = d_ref[:, cols]
            pwb = pw_ref[g].astype(BF16)
            dyg = dyb[:, cols]
            s5_ref[row_ps:row_ps + 1, cols] += _colsum(dyg * _dot(dgb, pwb))
            dqb = (dyg * ps_ref[:, cols]).astype(BF16)
            sp_ref[g] += _dot_tn(dgb, dqb)
            dd = _dot_nt(dqb, pwb)
            e = dd / _pool_count(tile, tr, w)
            ebuf[0:tr, cols] = e
            s = e
            for kk in range(1, w):
                s = s + ebuf[kk:kk + tr, cols]
            dbis.append(s - dd)
        ebuf[tr:tr + P_HALO, :] = ebuf[0:P_HALO, :]
        cv = c_ref[...].astype(F32)
        xc = cv - _rowmean(cv)
        rs = lax.rsqrt(_rowmean(xc * xc) + EPS)
        z = xc * rs
        ln = z * lg_ref[...] + lb_ref[...]
        sl = _sigmoid(ln)
        dl = dya * (sl * (1.0 + ln * (1.0 - sl)))
        s5_ref[row_lg:row_lg + 1, :] += _colsum(dl * z)
        s5_ref[row_lb:row_lb + 1, :] += _colsum(dl)
        dz = dl * lg_ref[...]
        dc = rs * (dz - _rowmean(dz) - z * _rowmean(dz * z))
        s5_ref[row_cb:row_cb + 1, :] += _colsum(dc)
        dcbuf[0:tr, :] = dc
        keep = (tile > 0).astype(F32)
        avh = projh_ref[:, :D_CONV].astype(F32)
        agh = projh_ref[:, D_CONV:].astype(F32)
        ubuf[0:A_HALO, :] = avh * _sigmoid(agh) * keep
        av = proj_ref[:, :D_CONV].astype(F32)
        ag = proj_ref[:, D_CONV:2 * D_CONV].astype(F32)
        sg = _sigmoid(ag)
        ubuf[A_HALO:A_HALO + tr, :] = av * sg
        off = A_HALO - (CONV_A - 1)
        du = wa_ref[CONV_A - 1:CONV_A, :] * dc
        dview = _shifted_views(dcbuf, dshift, tr)
        uview = _shifted_views(ubuf, ushift, tr)
        for j in range(CONV_A - 1):
            du = du + wa_ref[j:j + 1, :] * dview(CONV_A - 1 - j)
        for j in range(CONV_A):
            s5_ref[j:j + 1, :] += _colsum(dc * uview(off + j))
        dcbuf[tr:tr + A_HALO, :] = dcbuf[0:A_HALO, :]
        dav = du * sg
        dag = du * av * (sg * (1.0 - sg))
        dprojb = jnp.concatenate([dav, dag] + dbis, axis=1).astype(BF16)
        dproj_ref[...] = dprojb
        dh1 = _dot_nt(dprojb, win_v[...])
        xv = x_ref[...]
        r1 = lax.rsqrt(_rowmean(xv * xv) + EPS)
        n1 = xv * r1
        sm_ref[0:1, :] += _colsum(dh1 * n1)
        dn1 = dh1 * g1_ref[...]
        gx_ref[...] = dx1v + r1 * (dn1 - n1 * _rowmean(dn1 * n1))

        @pl.when(i == max(n - 2, 0))
        def _():
            scatter()[1]()

        @pl.when(i == n - 1)
        def _():
            scatter()[2]()

    tile = lambda w: pl.BlockSpec((tr, w), lambda i: (n - 1 - i, 0))
    full = lambda a: pl.BlockSpec(a.shape, lambda i: (0,) * a.ndim)
    halo = pl.BlockSpec((A_HALO, 2 * D_CONV), lambda i: (jnp.maximum((n - 1 - i) * (tr // A_HALO) - 1, 0), 0))
    acc = lambda shape: pl.BlockSpec(shape, lambda i: (0,) * len(shape))
    return pl.pallas_call(
        body, name="mixer_bwd", grid=(n,),
        in_specs=[tile(D_MODEL), tile(D_MODEL), tile(D_IN), halo, tile(D_CONV), tile(D_POOL), full(g1), ANY, full(wa),
                  full(lg), full(lb), full(pw), full(ps), ANY] + [ANY] * n_parts,
        out_specs=[tile(D_IN), tile(D_MODEL), acc((8, D_MODEL)), acc((40, D_CONV)), acc(pw.shape)] + [ANY] * n_parts,
        out_shape=[
            jax.ShapeDtypeStruct((seq, D_IN), BF16), jax.ShapeDtypeStruct((seq, D_MODEL), F32),
            jax.ShapeDtypeStruct((8, D_MODEL), F32), jax.ShapeDtypeStruct((40, D_CONV), F32),
            jax.ShapeDtypeStruct(pw.shape, F32),
        ] + _scatter_shapes(parts, ()),
        scratch_shapes=[
            pltpu.VMEM(win.shape, BF16), pltpu.VMEM(wout.shape, BF16),
            pltpu.VMEM((tr + A_HALO, D_CONV), F32), pltpu.VMEM((7, tr + A_HALO - 8, D_CONV), F32),
            pltpu.VMEM((tr + A_HALO, D_CONV), F32), pltpu.VMEM((7, tr + A_HALO - 8, D_CONV), F32),
            pltpu.VMEM((tr + P_HALO, D_POOL), F32), pltpu.SemaphoreType.DMA((2,)),
        ] + _scatter_scratch(parts, ()),
        compiler_params=pltpu.CompilerParams(dimension_semantics=("arbitrary",), vmem_limit_bytes=VMEM_LIMIT,
                                             collective_id=SIBLING_AND_CHIPS),
    )(dx1, x, proj, proj, cpre, d, g1, win, wa, lg, lb, pw, ps, wout, *parts)


def _weight_grad(a, b, layout, k_rows, comm=None, carry=None):
    seq, m_dim = a.shape
    n_dim = b.shape[1]
    steps = seq // k_rows

    def store(o_ref, acc, index, value):
        if steps == 1:
            o_ref[index] = value.astype(BF16)
            return
        s = pl.program_id(1)

        @pl.when(s == 0)
        def _():
            acc[index] = value

        @pl.when(jnp.logical_and(s > 0, s < steps - 1))
        def _():
            acc[index] += value

        @pl.when(s == steps - 1)
        def _():
            o_ref[index] = (acc[index] + value).astype(BF16)

    if layout in ("rows1", "rows2"):
        groups = int(layout[-1])
        per_tile = N_CHIPS // groups
        rows = m_dim // N_CHIPS // 2
        a_w = m_dim // groups

        def body(a_ref, b_ref, o_ref, acc):
            r = _dot_tn(a_ref[...], b_ref[...])
            for p in range(per_tile):
                for h in range(2):
                    store(o_ref, acc, (h, p), r[(2 * p + h) * rows:(2 * p + h + 1) * rows, :])

        in_specs = [pl.BlockSpec((k_rows, a_w), lambda g, s: (s, g)), pl.BlockSpec((k_rows, n_dim), lambda g, s: (s, 0))]
        out_spec = pl.BlockSpec((2, per_tile, rows, n_dim), lambda g, s: (0, g, 0, 0))
        out_dims, acc_dims = (2, N_CHIPS, rows, n_dim), (2, per_tile, rows, n_dim)
    elif layout == "cols_chip":
        groups = N_CHIPS
        rows, cols = m_dim // 2, n_dim // N_CHIPS

        def body(a_ref, b_ref, o_ref, acc):
            r = _dot_tn(a_ref[...], b_ref[...])
            for h in range(2):
                store(o_ref, acc, h, r[h * rows:(h + 1) * rows, :])

        in_specs = [pl.BlockSpec((k_rows, m_dim), lambda g, s: (s, 0)), pl.BlockSpec((k_rows, cols), lambda g, s: (s, g))]
        out_spec = pl.BlockSpec((2, None, rows, cols), lambda g, s: (0, g, 0, 0))
        out_dims, acc_dims = (2, N_CHIPS, rows, cols), (2, rows, cols)
    else:
        groups = 2
        rows, cols = m_dim // 2, n_dim // N_CHIPS

        def body(a_ref, b_ref, o_ref, acc):
            r = _dot_tn(a_ref[...], b_ref[...])
            for k in range(N_CHIPS):
                store(o_ref, acc, k, r[:, k * cols:(k + 1) * cols])

        in_specs = [pl.BlockSpec((k_rows, rows), lambda g, s: (s, g)), pl.BlockSpec((k_rows, n_dim), lambda g, s: (s, 0))]
        out_spec = pl.BlockSpec((None, N_CHIPS, rows, cols), lambda g, s: (g, 0, 0, 0))
        out_dims, acc_dims = (2, N_CHIPS, rows, cols), (N_CHIPS, rows, cols)

    c_ins, c_shapes, c_sems, c_ops, c_id = _comm_plan(comm)
    nc = len(c_ins)
    c_specs = [ANY] * nc
    if carry is not None:
        assert comm is None and carry.shape[0] % (groups * steps) == 0
        carry_spec = pl.BlockSpec((carry.shape[0] // (groups * steps), carry.shape[1]), lambda g, s: (g * steps + s, 0))
        c_ins, c_shapes, c_specs, nc = (carry,), [jax.ShapeDtypeStruct(carry.shape, carry.dtype)], [carry_spec], 1

    def hosted(a_ref, b_ref, *rest):
        c_in, o_ref, c_out, acc, sems = rest[:nc], rest[nc], rest[nc + 1:2 * nc + 1], rest[2 * nc + 1], rest[2 * nc + 2:]
        g, s = pl.program_id(0), pl.program_id(1)
        if carry is not None:
            c_out[0][...] = c_in[0][...]
            body(a_ref, b_ref, o_ref, acc)
            return
        if nc:
            @pl.when(jnp.logical_and(g == 0, s == 0))
            def _():
                c_ops(c_in, c_out, sems)[0]()

        body(a_ref, b_ref, o_ref, acc)
        if nc:
            step = g * steps + s

            @pl.when(step == max(groups * steps - 2, 0))
            def _():
                c_ops(c_in, c_out, sems)[1]()

            @pl.when(step == groups * steps - 1)
            def _():
                c_ops(c_in, c_out, sems)[2]()

    outs = pl.pallas_call(
        hosted, name=f"weight_grad_{layout}_{m_dim}x{n_dim}", grid=(groups, steps),
        in_specs=in_specs + c_specs, out_specs=[out_spec] + c_specs,
        out_shape=[jax.ShapeDtypeStruct(out_dims, BF16)] + c_shapes,
        scratch_shapes=[pltpu.VMEM(acc_dims, F32)] + c_sems,
        compiler_params=pltpu.CompilerParams(dimension_semantics=("arbitrary", "arbitrary"), vmem_limit_bytes=VMEM_LIMIT,
                                             collective_id=c_id),
    )(a, b, *c_ins)
    return outs if nc else outs[0]


def _exchange_ops(ins, outs, n_big, sems):
    send, recv = sems
    x, y, c, _, _ = _place()
    cps = [pltpu.make_async_remote_copy(
        src_ref=ins[t].at[1 - c] if t < n_big else ins[t], dst_ref=outs[t], send_sem=send.at[t], recv_sem=recv.at[t],
        device_id=(x, y, 1 - c), device_id_type=MESH) for t in range(len(ins))]

    def start():
        for cp in cps:
            cp.start()

    def finish():
        for cp in cps:
            cp.wait()

    return start, finish


def _exchange_shapes(bigs, smalls):
    return [jax.ShapeDtypeStruct((N_CHIPS,) + b.shape[2:], b.dtype) for b in bigs] + [
        jax.ShapeDtypeStruct(s.shape, s.dtype) for s in smalls]


def _comm_plan(comm):
    if comm is None:
        return (), [], [], None, None
    kind, arrays = comm
    n = len(arrays)

    def scatter(i, o, sm):
        start, land, finish = _scatter_ops(i, o, n, sm[:6], sm[6:])
        return lambda: (_handshake(SIBLING_AND_CHIPS), start()), land, finish

    def exchange(i, o, sm):
        start, finish = _exchange_ops(i, o, n, sm)
        return lambda: (_handshake(SIBLING_ONLY), start()), lambda: None, finish

    if kind == "scatter":
        return tuple(arrays), _scatter_shapes(arrays, ()), _scatter_scratch(arrays, ()), scatter, SIBLING_AND_CHIPS
    return tuple(arrays), _exchange_shapes(arrays, ()), [pltpu.SemaphoreType.DMA((n,))] * 2, exchange, SIBLING_ONLY


def _sibling_exchange(bigs, smalls, tag):
    nb, nt = len(bigs), len(bigs) + len(smalls)

    def body(*refs):
        start, finish = _exchange_ops(refs[:nt], refs[nt:2 * nt], nb, refs[2 * nt:])
        _handshake(SIBLING_ONLY)
        start()
        finish()

    return pl.pallas_call(
        body, name=f"sibling_exchange_{tag}", out_shape=_exchange_shapes(bigs, smalls),
        in_specs=[ANY] * nt, out_specs=[ANY] * nt,
        scratch_shapes=[pltpu.SemaphoreType.DMA((nt,)), pltpu.SemaphoreType.DMA((nt,))],
        compiler_params=pltpu.CompilerParams(collective_id=SIBLING_ONLY),
    )(*bigs, *smalls)


def _pair_sum(core, mine, theirs, tag, block_rows):
    _, _, rows, cols = mine.shape
    steps = rows // block_rows

    def body(core_ref, a_ref, b_ref, o_ref):
        o_ref[...] = (a_ref[...].astype(F32) + b_ref[...].astype(F32)).astype(BF16)

    grid_spec = pltpu.PrefetchScalarGridSpec(
        num_scalar_prefetch=1, grid=(N_CHIPS, steps),
        in_specs=[pl.BlockSpec((None, None, block_rows, cols), lambda k, r, core_ref: (core_ref[0], k, r, 0)),
                  pl.BlockSpec((None, block_rows, cols), lambda k, r, core_ref: (k, r, 0))],
        out_specs=pl.BlockSpec((None, block_rows, cols), lambda k, r, core_ref: (k, r, 0)),
    )
    return pl.pallas_call(
        body, name=f"pair_sum_{tag}", grid_spec=grid_spec,
        out_shape=jax.ShapeDtypeStruct((N_CHIPS, rows, cols), BF16),
        compiler_params=pltpu.CompilerParams(dimension_semantics=("arbitrary", "arbitrary"), vmem_limit_bytes=VMEM_LIMIT),
    )(core, mine, theirs)


def _pair_sum_small(mine, theirs):
    (m_f2, m_b1, m_b2, m_sf, m_s5, m_sp) = mine

    def body(a0, a1, a2, a3, a4, a5, b0, b1, b2, b3, b4, b5, o_m, o_f, o_5, o_p):
        sm = (a0[...] + a1[...] + a2[...]) + (b0[...] + b1[...] + b2[...])
        sf = a3[...] + b3[...]
        s5 = a4[...] + b4[...]
        for h in range(2):
            o_m[h] = sm[:, h * (D_MODEL // 2):(h + 1) * (D_MODEL // 2)]
            o_f[h] = sf[:, h * (D_FF // 2):(h + 1) * (D_FF // 2)]
            o_5[h] = s5[:, h * (D_CONV // 2):(h + 1) * (D_CONV // 2)]
            for g in range(2):
                o_p[h, g] = a5[2 * h + g] + b5[2 * h + g]

    out_shape = [
        jax.ShapeDtypeStruct((2, 8, D_MODEL // 2), F32), jax.ShapeDtypeStruct((2, 8, D_FF // 2), F32),
        jax.ShapeDtypeStruct((2, 40, D_CONV // 2), F32), jax.ShapeDtypeStruct((2, 2, POOL_GROUP, POOL_GROUP), F32),
    ]
    return pl.pallas_call(body, name="pair_sum_small", out_shape=out_shape, in_specs=[VMEM] * 12, out_specs=[VMEM] * 4)(
        *mine, *theirs)


def _scatter_ops(ins, outs, n_parts, sems, stages, landed=False):
    ici_send, ici_recv, fwd_send, fwd_recv, loc_in, loc_out = sems
    nt = len(ins)
    x, y, c, k, chips = _place()

    def src_of(t, kk):
        return ins[t].at[kk] if t < n_parts else ins[t].at[c]

    def ici(t, j, kk, slot):
        return pltpu.make_async_remote_copy(
            src_ref=src_of(t, kk), dst_ref=outs[t].at[c, slot], send_sem=ici_send.at[t * 3 + j],
            recv_sem=ici_recv.at[t * 3 + j], device_id=(*chips[j], c), device_id_type=MESH)

    def fwd(t, half):
        slots = outs[t].at[half]
        return pltpu.make_async_remote_copy(
            src_ref=slots, dst_ref=slots, send_sem=fwd_send.at[t], recv_sem=fwd_recv.at[t],
            device_id=(x, y, 1 - c), device_id_type=MESH)

    local = [_staged(src_of(t, k), outs[t].at[c, k], stages[t], loc_in.at[t], loc_out.at[t]) for t in range(nt)]
    peers = [(t, j, 2 * qx + qy) for t in range(nt) for j, (qx, qy) in enumerate(chips)]
    sends = [] if landed else [ici(t, j, kq, k) for t, j, kq in peers]

    def start():
        for cp in local:
            cp[0]()
        for cp in sends:
            cp.start()

    def land():
        for cp in local:
            cp[1]()
        if not landed:
            for t, j, kq in peers:
                ici(t, j, kq, kq).wait_recv()
        for cp in local:
            cp[2]()
        for t in range(nt):
            fwd(t, c).start()

    def finish():
        for t in range(nt):
            fwd(t, 1 - c).wait_recv()
            fwd(t, c).wait_send()
        for cp in sends:
            cp.wait_send()

    return start, land, finish


def _scatter_scratch(parts, smalls):
    arrays = tuple(parts) + tuple(smalls)
    nt = len(arrays)
    return ([pltpu.SemaphoreType.DMA((3 * nt,))] * 2 + [pltpu.SemaphoreType.DMA((nt,))] * 4
            + [pltpu.VMEM(a.shape[1:], a.dtype) for a in arrays])


def _scatter_shapes(parts, smalls):
    return [jax.ShapeDtypeStruct((2, N_CHIPS) + p.shape[1:], p.dtype) for p in tuple(parts) + tuple(smalls)]


HBM_SPEC = pl.BlockSpec(memory_space=pltpu.HBM)
SEM_SPEC = pl.BlockSpec(memory_space=pltpu.SEMAPHORE)
EFFECT = pltpu.SideEffectType.DATAFLOW_SIDE_EFFECTING


def _ici_copy(ins, lands, n_parts, send, recv, t, j):
    _, _, c, k, chips = _place()
    qx, qy = chips[j]
    src = ins[t].at[2 * qx + qy] if t < n_parts else ins[t].at[c]
    return pltpu.make_async_remote_copy(
        src_ref=src, dst_ref=lands[t].at[c, k], send_sem=send.at[t * 3 + j], recv_sem=recv.at[t * 3 + j],
        device_id=(qx, qy, c), device_id_type=MESH)


def _scatter_start(parts, smalls):
    arrays = tuple(parts) + tuple(smalls)
    nt = len(arrays)

    def body(*refs):
        ins, lands = refs[:nt], refs[nt:2 * nt]
        send, recv = refs[2 * nt], refs[2 * nt + 1]
        token = refs[-1]
        for t in range(nt):
            for j in range(3):
                _ici_copy(ins, lands, len(parts), send, recv, t, j).start()
        token[...] = jnp.zeros(token.shape, F32)

    land_shapes = _scatter_shapes(parts, smalls)
    out_shape = ([pltpu.SemaphoreType.DMA((3 * nt,))] * 2 + [pltpu.HBM(a.shape, a.dtype) for a in arrays]
                 + [pltpu.HBM(a.shape, a.dtype) for a in land_shapes] + [jax.ShapeDtypeStruct((8, 128), F32)])
    operands = [pltpu.with_memory_space_constraint(a, pltpu.HBM) for a in arrays]
    operands += [pltpu.with_memory_space_constraint(lax.empty(a.shape, a.dtype), pltpu.HBM) for a in land_shapes]
    outs = pl.pallas_call(
        body, name="scatter_start", out_shape=out_shape, in_specs=[HBM_SPEC] * (2 * nt),
        out_specs=[SEM_SPEC] * 2 + [HBM_SPEC] * (2 * nt) + [VMEM],
        input_output_aliases={i: 2 + i for i in range(2 * nt)},
        compiler_params=pltpu.CompilerParams(has_side_effects=EFFECT),
    )(*operands)
    return outs[0], outs[1], outs[2:2 + nt], outs[2 + nt:2 + 2 * nt], outs[-1]


def _scatter_wait(send, recv, ins, lands, n_parts, after):
    nt = len(ins)

    def body(*refs):
        in_refs, land_refs = refs[:nt], refs[nt:2 * nt]
        send_ref, recv_ref = refs[2 * nt], refs[2 * nt + 1]
        for t in range(nt):
            for j in range(3):
                cp = _ici_copy(in_refs, land_refs, n_parts, send_ref, recv_ref, t, j)
                cp.wait_send()
                cp.wait_recv()

    outs = pl.pallas_call(
        body, name="scatter_wait", out_shape=[pltpu.HBM(a.shape, a.dtype) for a in tuple(ins) + tuple(lands)],
        in_specs=[HBM_SPEC] * (2 * nt) + [SEM_SPEC] * 2 + [ANY] * len(after), out_specs=[HBM_SPEC] * (2 * nt),
        input_output_aliases={i: i for i in range(2 * nt)},
        compiler_params=pltpu.CompilerParams(has_side_effects=EFFECT),
    )(*ins, *lands, send, recv, *after)
    return outs[:nt], outs[nt:]


def _scatter_forward(ins, lands, n_parts):
    nt = len(ins)

    def body(*refs):
        start, land, finish = _scatter_ops(
            refs[:nt], refs[2 * nt:3 * nt], n_parts, refs[3 * nt:3 * nt + 6], refs[3 * nt + 6:], landed=True)
        _handshake(SIBLING_ONLY)
        start()
        land()
        finish()

    return pl.pallas_call(
        body, name="scatter_forward", out_shape=[jax.ShapeDtypeStruct(a.shape, a.dtype) for a in lands],
        in_specs=[ANY] * (2 * nt), out_specs=[ANY] * nt, input_output_aliases={nt + i: i for i in range(nt)},
        scratch_shapes=_scatter_scratch(ins[:n_parts], ins[n_parts:]),
        compiler_params=pltpu.CompilerParams(collective_id=SIBLING_ONLY),
    )(*ins, *lands)


def _chip_scatter(parts, smalls):
    nt = len(parts) + len(smalls)

    def body(*refs):
        start, land, finish = _scatter_ops(refs[:nt], refs[nt:2 * nt], len(parts), refs[2 * nt:2 * nt + 6], refs[2 * nt + 6:])
        _handshake(SIBLING_AND_CHIPS)
        start()
        land()
        finish()

    return pl.pallas_call(
        body, name="chip_scatter", out_shape=_scatter_shapes(parts, smalls), in_specs=[ANY] * nt, out_specs=[ANY] * nt,
        scratch_shapes=_scatter_scratch(parts, smalls),
        compiler_params=pltpu.CompilerParams(collective_id=SIBLING_AND_CHIPS),
    )(*parts, *smalls)


def _adamw(w, g, m, v):
    m = ADAM_B1 * m + (1.0 - ADAM_B1) * g
    v = ADAM_B2 * v + (1.0 - ADAM_B2) * (g * g)
    m_hat = m / (1.0 - ADAM_B1 ** ADAM_STEP)
    v_hat = v / (1.0 - ADAM_B2 ** ADAM_STEP)
    delta = -ADAM_LR * (m_hat / (jnp.sqrt(v_hat) + ADAM_EPS) + ADAM_WD * w)
    return delta, m, v


def _adam_big(parts, w, m, v, tag, block_rows, token):
    _, _, rows, cols = parts.shape
    steps = rows // block_rows

    def body(p_ref, w_ref, m_ref, v_ref, token_ref, g_out, d_out, m_out, v_out):
        g = p_ref[0].astype(F32)
        for q in range(1, N_CHIPS):
            g = g + p_ref[q].astype(F32)
        delta, m_new, v_new = _adamw(w_ref[...], g, m_ref[...], v_ref[...])
        g_out[...] = g
        d_out[...] = delta
        m_out[...] = m_new
        v_out[...] = v_new

    blk = pl.BlockSpec((block_rows, cols), lambda h, r: (h * steps + r, 0))
    return pl.pallas_call(
        body, name=f"adam_{tag}", grid=(2, steps),
        in_specs=[pl.BlockSpec((None, N_CHIPS, block_rows, cols), lambda h, r: (h, 0, r, 0)), blk, blk, blk, ANY],
        out_specs=[blk] * 4, out_shape=[jax.ShapeDtypeStruct(w.shape, F32)] * 4,
        compiler_params=pltpu.CompilerParams(dimension_semantics=("arbitrary", "arbitrary"), vmem_limit_bytes=VMEM_LIMIT),
    )(parts, w, m, v, token)


def _reduce_small(l_m, l_f, l_5, l_p):
    def total(ref):
        t = ref[:, 0]
        for q in range(1, N_CHIPS):
            t = t + ref[:, q]
        return t

    def body(m_ref, f_ref, s_ref, p_ref, g1_o, g2_o, g3_o, loss_o, wf_o, fb_o, wa_o, cb_o, lg_o, lb_o, ps_o, pw_o):
        tm, tf, t5, tp = total(m_ref), total(f_ref), total(s_ref), total(p_ref)
        sm = jnp.concatenate([tm[0], tm[1]], axis=1)
        sf = jnp.concatenate([tf[0], tf[1]], axis=1)
        s5 = jnp.concatenate([t5[0], t5[1]], axis=1)
        g1_o[...] = sm[0:1]
        g2_o[...] = sm[1:2]
        g3_o[...] = sm[2:3]
        loss_o[...] = sm[3:4, 0:128]
        wf_o[...] = sf
        fb_o[...] = sf[3:4]
        wa_o[...] = s5[0:32]
        cb_o[...] = s5[32:33]
        lg_o[...] = s5[33:34]
        lb_o[...] = s5[34:35]
        ps_o[...] = s5[35:36]
        for h in range(2):
            for g in range(2):
                pw_o[2 * h + g] = tp[h, g]

    row = lambda w: jax.ShapeDtypeStruct((1, w), F32)
    out_shape = [row(D_MODEL), row(D_MODEL), row(D_MODEL), row(128), jax.ShapeDtypeStruct((8, D_FF), F32), row(D_FF),
                 jax.ShapeDtypeStruct((32, D_CONV), F32), row(D_CONV), row(D_CONV), row(D_CONV), row(D_POOL),
                 jax.ShapeDtypeStruct((4, POOL_GROUP, POOL_GROUP), F32)]
    return pl.pallas_call(body, name="reduce_small", out_shape=out_shape, in_specs=[VMEM] * 4, out_specs=[VMEM] * 12)(
        l_m, l_f, l_5, l_p)


def _adam_small(ws, gs, ms, vs):
    count = len(ws)

    def body(*refs):
        w_r, g_r, m_r, v_r = (refs[t * count:(t + 1) * count] for t in range(4))
        d_o, m_o, v_o = (refs[(4 + t) * count:(5 + t) * count] for t in range(3))
        for t in range(count):
            delta, m_new, v_new = _adamw(w_r[t][...], g_r[t][...], m_r[t][...], v_r[t][...])
            d_o[t][...] = delta
            m_o[t][...] = m_new
            v_o[t][...] = v_new

    out_shape = [jax.ShapeDtypeStruct(w.shape, F32) for w in ws] * 3
    outs = pl.pallas_call(body, name="adam_small", out_shape=out_shape, in_specs=[VMEM] * (4 * count),
                          out_specs=[VMEM] * (3 * count))(*ws, *gs, *ms, *vs)
    return outs[:count], outs[count:2 * count], outs[2 * count:]


MIX_TILE = 512
UP_TILE = 512
FFN_TILE = 256
GRAD_K = 2048


def kernel(x, norm_mix_g, w_in, conv_a_w, conv_a_b, ln_a_g, ln_a_b, pool_w, pool_scale, w_out, norm_ffn_g, w_up, conv_f_w, conv_f_b, w_down, norm_final_g, loss_target, m_norm_mix_g, m_w_in, m_conv_a_w, m_conv_a_b, m_ln_a_g, m_ln_a_b, m_pool_w, m_pool_scale, m_w_out, m_norm_ffn_g, m_w_up, m_conv_f_w, m_conv_f_b, m_w_down, m_norm_final_g, v_norm_mix_g, v_w_in, v_conv_a_w, v_conv_a_b, v_ln_a_g, v_ln_a_b, v_pool_w, v_pool_scale, v_w_out, v_norm_ffn_g, v_w_up, v_conv_f_w, v_conv_f_b, v_w_down, v_norm_final_g):
    seq = x.shape[1]
    xs, ts = x[0], loss_target[0]
    mix_tile, ffn_tile, grad_k = min(MIX_TILE, seq), min(FFN_TILE, seq), min(GRAD_K, seq)
    chip = 2 * lax.axis_index("x") + lax.axis_index("y")
    core = lax.axis_index("c").astype(jnp.int32).reshape(1)

    wa_s = jnp.pad(conv_a_w[0], ((0, 32 - CONV_A), (0, 0)))
    wf_s = jnp.pad(conv_f_w[0], ((0, 8 - CONV_F), (0, 0)))
    win_b, wout_b, wup_b, wdown_b = _cast_shards(w_in[0], w_out[0], w_up[0], w_down[0])
    g3 = norm_final_g.reshape(1, D_MODEL)
    pw = pool_w[0]

    h1, proj, cpre, dpool, mcat, x1, win, wout, wup, wa_g, wf_g = _mixer_fwd(
        xs, norm_mix_g, win_b, wout_b, wup_b, wa_s, wf_s, conv_a_b, ln_a_g, ln_a_b, pw, pool_scale, mix_tile)
    wa = jnp.transpose(wa_g, (1, 0, 2)).reshape(32, D_CONV)
    wf = jnp.transpose(wf_g, (1, 0, 2)).reshape(8, D_FF)
    h2, up, gcs, act, wdown = _ffn_up(x1, norm_ffn_g, wup, wf, conv_f_b, wdown_b, min(UP_TILE, seq))
    dx2b, sm_f2 = _ffn_down(x1, act, wdown, g3, ts, mix_tile)
    tags = ("w_in", "w_out", "w_up", "w_down")
    blocks = (256, 128, 256, 176)
    g_wdown = _weight_grad(act, dx2b, "rows2", grad_k)
    dup, dx1b, sm_b1, sf, l_wdown = _ffn_bwd(
        dx2b, up, gcs, x1, norm_ffn_g, wup, wf, wdown, ("exchange", [g_wdown]), ffn_tile)
    p_wdown = _pair_sum(core, g_wdown, l_wdown, tags[3], g_wdown.shape[2])
    g_wup, s_wdown = _weight_grad(h2, dup, "cols_chip", grad_k, ("scatter", [p_wdown]))
    g_wout, l_wup = _weight_grad(mcat, dx1b, "rows1", grad_k, ("exchange", [g_wup]))
    p_wup = _pair_sum(core, g_wup, l_wup, tags[2], g_wup.shape[2])
    l_wout, = _sibling_exchange((g_wout,), (), "early")
    p_wout = _pair_sum(core, g_wout, l_wout, tags[1], g_wout.shape[2])
    dproj, gx, sm_b2, s5, sp, s_wout, s_wup = _mixer_bwd(
        dx1b, xs, proj, cpre, dpool, norm_mix_g, win, wa, ln_a_g, ln_a_b, pw, pool_scale, wout, [p_wout, p_wup], mix_tile)
    g_win, grad_x = _weight_grad(h1, dproj, "cols_half", grad_k, carry=gx)

    smalls = (sm_f2, sm_b1, sm_b2, sf, s5, sp)
    landed = _sibling_exchange((g_win,), smalls, "late")
    part_win = _pair_sum(core, g_win, landed[0], tags[0], g_win.shape[2])
    small_parts = _pair_sum_small(smalls, landed[1:])
    send, recv, late_src, late_land, token = _scatter_start([part_win], small_parts)
    big_w = (w_in[0], w_out[0], w_up[0], w_down[0])
    big_m = (m_w_in[0], m_w_out[0], m_w_up[0], m_w_down[0])
    big_v = (v_w_in[0], v_w_out[0], v_w_up[0], v_w_down[0])
    big = {}
    for t, p in ((1, s_wout), (2, s_wup), (3, s_wdown)):
        big[tags[t]] = _adam_big(p, big_w[t], big_m[t], big_v[t], tags[t], blocks[t], token)
    late_src, late_land = _scatter_wait(send, recv, late_src, late_land, 1, [big[tags[t]][3] for t in (1, 2, 3)])
    late = _scatter_forward(late_src, late_land, 1)
    big[tags[0]] = _adam_big(late[0], big_w[0], big_m[0], big_v[0], tags[0], blocks[0], token)
    big = {tag: [a[None] for a in outs] for tag, outs in big.items()}
    scattered = [None] * 4 + list(late[1:])

    (g_g1, g_g2, g_g3, loss_row, g_wf_all, g_fb, g_wa_all, g_cb, g_lg, g_lb, g_ps, g_pw) = _reduce_small(*scattered[4:])
    g_wa = lax.dynamic_slice(g_wa_all, (0, chip * (D_CONV // N_CHIPS)), (32, D_CONV // N_CHIPS))[:CONV_A]
    g_wf = lax.dynamic_slice(g_wf_all, (0, chip * (D_FF // N_CHIPS)), (8, D_FF // N_CHIPS))[:CONV_F]
    small_names = ("norm_mix_g", "conv_a_w", "conv_a_b", "ln_a_g", "ln_a_b", "pool_w", "pool_scale", "norm_ffn_g",
                   "conv_f_w", "conv_f_b", "norm_final_g")
    small_w = (norm_mix_g, conv_a_w[0], conv_a_b, ln_a_g, ln_a_b, pw, pool_scale, norm_ffn_g, conv_f_w[0], conv_f_b, g3)
    small_m = (m_norm_mix_g, m_conv_a_w[0], m_conv_a_b, m_ln_a_g, m_ln_a_b, m_pool_w[0], m_pool_scale, m_norm_ffn_g,
               m_conv_f_w[0], m_conv_f_b, m_norm_final_g.reshape(1, D_MODEL))
    small_v = (v_norm_mix_g, v_conv_a_w[0], v_conv_a_b, v_ln_a_g, v_ln_a_b, v_pool_w[0], v_pool_scale, v_norm_ffn_g,
               v_conv_f_w[0], v_conv_f_b, v_norm_final_g.reshape(1, D_MODEL))
    small_g = (g_g1, g_wa, g_cb, g_lg, g_lb, g_pw, g_ps, g_g2, g_wf, g_fb, g_g3)
    s_delta, s_m, s_v = _adam_small(small_w, small_g, small_m, small_v)
    shapes = {"conv_a_w": conv_a_w.shape, "pool_w": pool_w.shape, "conv_f_w": conv_f_w.shape, "norm_final_g": norm_final_g.shape}
    small = {}
    for t, name in enumerate(small_names):
        shp = shapes.get(name)
        small[name] = [a if shp is None else a.reshape(shp) for a in (small_g[t], s_delta[t], s_m[t], s_v[t])]

    order = ("norm_mix_g", "w_in", "conv_a_w", "conv_a_b", "ln_a_g", "ln_a_b", "pool_w", "pool_scale", "w_out", "norm_ffn_g",
             "w_up", "conv_f_w", "conv_f_b", "w_down", "norm_final_g")
    table = {**big, **small}
    loss = loss_row[0, 0]
    outs = [loss, grad_x[None]]
    for t in range(4):
        outs += [table[name][t] for name in order]
    return tuple(outs)
```

```python
import functools

import jax
import jax.numpy as jnp
from jax import lax
from jax.experimental import pallas as pl
from jax.experimental.pallas import tpu as pltpu

F32 = jnp.float32
BF16 = jnp.bfloat16
EPS = 1e-6
ADAM_LR = 0.001
ADAM_B1 = 0.9
ADAM_B2 = 0.999
ADAM_EPS = 1e-08
ADAM_WD = 0.01
ADAM_STEP = 10

D_MODEL = 1024
D_CONV = 512
D_POOL = 512
D_IN = 1536
D_FF = 2816
CONV_A = 31
CONV_F = 3
POOL_WINDOWS = (2, 4, 8, 16)
POOL_GROUP = 128
N_CHIPS = 4
FF_CHUNK = 256
N_FF_CHUNKS = D_FF // FF_CHUNK
UP_CHUNK = 2816
A_HALO = 32
P_HALO = 16
VMEM_LIMIT = 56 * 1024 * 1024
MESH = pl.DeviceIdType.MESH

ANY = pl.BlockSpec(memory_space=pl.ANY)
VMEM = pl.BlockSpec(memory_space=pltpu.VMEM)


def _dot(a, b):
    return jnp.dot(a, b, preferred_element_type=F32)


def _dot_nt(a, b):
    return lax.dot_general(a, b, (((1,), (1,)), ((), ())), preferred_element_type=F32)


def _dot_tn(a, b):
    return lax.dot_general(a, b, (((0,), (0,)), ((), ())), preferred_element_type=F32)


def _sigmoid(v):
    return jax.nn.sigmoid(v)


def _colsum(v):
    return jnp.sum(v, axis=0, keepdims=True)


def _rowmean(v):
    return jnp.mean(v, axis=-1, keepdims=True)


def _place():
    x, y, c = lax.axis_index("x"), lax.axis_index("y"), lax.axis_index("c")
    chips = [(1 - x, y), (x, 1 - y), (1 - x, 1 - y)]
    return x, y, c, 2 * x + y, chips


SIBLING_ONLY, SIBLING_AND_CHIPS = 0, 1


def _handshake(collective):
    x, y, c, _, chips = _place()
    peers = [(x, y, 1 - c)] + ([(*chip, c) for chip in chips] if collective == SIBLING_AND_CHIPS else [])
    barrier = pltpu.get_barrier_semaphore()
    for peer in peers:
        pl.semaphore_signal(barrier, inc=1, device_id=peer, device_id_type=MESH)
    pl.semaphore_wait(barrier, len(peers))


def _staged(src, dst, stage, sem_in, sem_out):
    hop_in = pltpu.make_async_copy(src, stage, sem_in)
    hop_out = pltpu.make_async_copy(stage, dst, sem_out)

    def relay():
        hop_in.wait()
        hop_out.start()

    return hop_in.start, relay, hop_out.wait


def _gather_ops(bufs, fulls, col_sharded, sems, stages):
    ici_send, ici_recv, fwd_send, fwd_recv, loc_in, loc_out = sems
    n_big = len(bufs)
    x, y, c, k, chips = _place()

    def block(i, kk, half=None):
        rows, cols = bufs[i].shape
        if col_sharded[i]:
            rs = slice(None) if half is None else pl.ds(pl.multiple_of(half * (rows // 2), 16), rows // 2)
            return fulls[i].at[rs, pl.ds(pl.multiple_of(kk * cols, 128), cols)]
        if half is None:
            return fulls[i].at[pl.ds(pl.multiple_of(kk * rows, 16), rows), :]
        return fulls[i].at[pl.ds(pl.multiple_of(kk * rows + half * (rows // 2), 16), rows // 2), :]

    def my_half(i):
        rows = bufs[i].shape[0]
        return bufs[i].at[pl.ds(pl.multiple_of(c * (rows // 2), 16), rows // 2), :]

    def ici(i, j, kk):
        return pltpu.make_async_remote_copy(
            src_ref=my_half(i), dst_ref=block(i, kk, c), send_sem=ici_send.at[i * 3 + j], recv_sem=ici_recv.at[i * 3 + j],
            device_id=(*chips[j], c), device_id_type=MESH)

    def fwd(i, j, kk, half):
        return pltpu.make_async_remote_copy(
            src_ref=block(i, kk, half), dst_ref=block(i, kk, half), send_sem=fwd_send.at[i * 3 + j],
            recv_sem=fwd_recv.at[i * 3 + j], device_id=(x, y, 1 - c), device_id_type=MESH)

    local = [_staged(bufs[i], block(i, k), stages[i], loc_in.at[i], loc_out.at[i]) for i in range(n_big)]
    sends = [ici(i, j, k) for i in range(n_big) for j in range(3)]
    peers = [(i, j, 2 * qx + qy) for i in range(n_big) for j, (qx, qy) in enumerate(chips)]

    def start():
        for cp in local:
            cp[0]()
        for cp in sends:
            cp.start()

    def land():
        for cp in local:
            cp[1]()
        for i, j, kq in peers:
            ici(i, j, kq).wait_recv()
            fwd(i, j, kq, c).start()

    def finish():
        for i, j, kq in peers:
            fwd(i, j, kq, 1 - c).wait_recv()
            fwd(i, j, kq, c).wait_send()
        for cp in sends:
            cp.wait_send()
        for cp in local:
            cp[2]()

    return start, land, finish


def _gather_scratch(shards):
    n_big = len(shards)
    return ([pltpu.SemaphoreType.DMA((3 * n_big,))] * 4 + [pltpu.SemaphoreType.DMA((n_big,))] * 2
            + [pltpu.VMEM(b.shape, b.dtype) for b in shards])


def _tap_ops(srcs, dsts, sems):
    send, recv, loc = sems
    _, _, c, k, chips = _place()

    def copy(t, j, kk):
        return pltpu.make_async_remote_copy(
            src_ref=srcs[t], dst_ref=dsts[t].at[kk], send_sem=send.at[t * 3 + j], recv_sem=recv.at[t * 3 + j],
            device_id=(*chips[j], c), device_id_type=MESH)

    local = [pltpu.make_async_copy(srcs[t], dsts[t].at[k], loc.at[t]) for t in range(len(srcs))]
    sends = [[copy(t, j, k) for j in range(3)] for t in range(len(srcs))]

    def start():
        for t, cp in enumerate(local):
            cp.start()
            for sd in sends[t]:
                sd.start()

    def wait(t):
        for j, (qx, qy) in enumerate(chips):
            copy(t, j, 2 * qx + qy).wait_recv()
        for sd in sends[t]:
            sd.wait_send()
        local[t].wait()

    return start, wait


def _cast_shards(*shards):
    def body(*refs):
        for src, dst in zip(refs[:len(shards)], refs[len(shards):]):
            dst[...] = src[...].astype(BF16)

    return pl.pallas_call(
        body, name="cast_shards", out_shape=[jax.ShapeDtypeStruct(s.shape, BF16) for s in shards],
        in_specs=[VMEM] * len(shards), out_specs=[VMEM] * len(shards),
        compiler_params=pltpu.CompilerParams(vmem_limit_bytes=VMEM_LIMIT),
    )(*shards)


def _load_weights(pairs, sem, first=0):
    cps = [pltpu.make_async_copy(src, dst, sem.at[first + i]) for i, (src, dst) in enumerate(pairs)]
    for cp in cps:
        cp.start()
    for cp in cps:
        cp.wait()


def _shifted_views(buf, shifted, t_rows):
    n = t_rows + A_HALO - 8
    for b in range(1, 8):
        shifted[b - 1] = buf[b:b + n, :]

    def view(offset):
        a, b = divmod(offset, 8)
        if b == 0:
            return buf[8 * a:8 * a + t_rows, :]
        return shifted[b - 1, 8 * a:8 * a + t_rows, :]

    return view


def _pool_count(tile, t_rows, w):
    row = lax.broadcasted_iota(jnp.int32, (t_rows, POOL_GROUP), 0) + tile * t_rows
    return jnp.minimum(row + 1, w).astype(F32)


def _mixer_fwd(x, g1, win_b, wout_b, wup_b, wa_s, wf_s, cb, lg, lb, pw, ps, tile_rows):
    seq = x.shape[0]
    tr = tile_rows
    n = seq // tr

    def body(x_ref, g1_ref, win_b_hbm, wout_b_hbm, wup_b_hbm, wa_s_hbm, wf_s_hbm, cb_ref, lg_ref, lb_ref, pw_ref,
             ps_ref, h1_ref, proj_ref, c_ref, d_ref, m_ref, x1_ref, win_f, wout_f, wup_f, wa_g, wf_g,
             win_v, wout_v, wa_ref, ubuf, ushift, bbuf, sem, *csems):
        i = pl.program_id(0)
        first_sems, first_stages, second_sems, second_stages, later_sems, later_stages, tap_sems = (
            csems[0:6], csems[6:7], csems[7:13], csems[13:14], csems[14:20], csems[20:21], csems[21:24])

        def first():
            return _gather_ops((win_b_hbm,), (win_f,), (True,), first_sems, first_stages)

        def second():
            return _gather_ops((wout_b_hbm,), (wout_f,), (False,), second_sems, second_stages)

        def later():
            return _gather_ops((wup_b_hbm,), (wup_f,), (True,), later_sems, later_stages)

        def taps():
            return _tap_ops((wa_s_hbm, wf_s_hbm), (wa_g, wf_g), tap_sems)

        @pl.when(i == 0)
        def _():
            _handshake(SIBLING_AND_CHIPS)
            first()[0]()
            taps()[0]()
            second()[0]()
            later()[0]()
            first()[1]()
            first()[2]()
            _load_weights([(win_f, win_v)], sem)
            ubuf[0:A_HALO, :] = jnp.zeros((A_HALO, D_CONV), F32)
            bbuf[0:P_HALO, :] = jnp.zeros((P_HALO, D_POOL), F32)

        xv = x_ref[...]
        r = lax.rsqrt(_rowmean(xv * xv) + EPS)
        h1 = (xv * r * g1_ref[...]).astype(BF16)
        h1_ref[...] = h1
        proj = _dot(h1, win_v[...])
        proj_ref[...] = proj.astype(BF16)

        @pl.when(i == 0)
        def _():
            taps()[1](0)
            _load_weights([(wa_g.at[kk], wa_ref.at[:, kk * (D_CONV // N_CHIPS):(kk + 1) * (D_CONV // N_CHIPS)])
                           for kk in range(N_CHIPS)], sem, 2)

        av, ag, bi = proj[:, :D_CONV], proj[:, D_CONV:2 * D_CONV], proj[:, 2 * D_CONV:]
        ubuf[A_HALO:A_HALO + tr, :] = av * _sigmoid(ag)
        off = A_HALO - (CONV_A - 1)
        uview = _shifted_views(ubuf, ushift, tr)
        acc = wa_ref[0:1, :] * uview(off)
        for j in range(1, CONV_A):
            acc = acc + wa_ref[j:j + 1, :] * uview(off + j)
        cv = acc + cb_ref[...]
        ubuf[0:A_HALO, :] = ubuf[tr:tr + A_HALO, :]
        c_ref[...] = cv.astype(BF16)
        xc = cv - _rowmean(cv)
        z = xc * lax.rsqrt(_rowmean(xc * xc) + EPS)
        ln = z * lg_ref[...] + lb_ref[...]
        ya = ln * _sigmoid(ln)
        bbuf[P_HALO:P_HALO + tr, :] = bi
        ds, ybs = [], []
        for g, w in enumerate(POOL_WINDOWS):
            cols = slice(g * POOL_GROUP, (g + 1) * POOL_GROUP)
            s = bi[:, cols]
            for kk in range(1, w):
                s = s + bbuf[P_HALO - kk:P_HALO - kk + tr, cols]
            dg = s / _pool_count(i, tr, w) - bi[:, cols]
            ds.append(dg)
            ybs.append(_dot(dg.astype(BF16), pw_ref[g].astype(BF16)))
        bbuf[0:P_HALO, :] = bbuf[tr:tr + P_HALO, :]
        d_ref[...] = jnp.concatenate(ds, axis=1).astype(BF16)
        yb = jnp.concatenate(ybs, axis=1) * ps_ref[...]
        m = jnp.concatenate([ya, yb], axis=1).astype(BF16)
        m_ref[...] = m

        @pl.when(i == 0)
        def _():
            second()[1]()
            second()[2]()
            _load_weights([(wout_f, wout_v)], sem, 1)

        x1_ref[...] = xv + _dot(m, wout_v[...])

        @pl.when(i == n - 1)
        def _():
            later()[1]()
            later()[2]()
            taps()[1](1)

    tile = lambda w: pl.BlockSpec((tr, w), lambda i: (i, 0))
    full = lambda a: pl.BlockSpec(a.shape, lambda i: (0,) * a.ndim)
    return pl.pallas_call(
        body, name="mixer_fwd", grid=(n,),
        in_specs=[tile(D_MODEL), full(g1)] + [ANY] * 5 + [full(cb), full(lg), full(lb), full(pw), full(ps)],
        out_specs=[tile(D_MODEL), tile(D_IN), tile(D_CONV), tile(D_POOL), tile(D_MODEL), tile(D_MODEL)] + [ANY] * 5,
        out_shape=[
            jax.ShapeDtypeStruct((seq, D_MODEL), BF16), jax.ShapeDtypeStruct((seq, D_IN), BF16),
            jax.ShapeDtypeStruct((seq, D_CONV), BF16), jax.ShapeDtypeStruct((seq, D_POOL), BF16),
            jax.ShapeDtypeStruct((seq, D_MODEL), BF16), jax.ShapeDtypeStruct((seq, D_MODEL), F32),
            jax.ShapeDtypeStruct((D_MODEL, D_IN), BF16), jax.ShapeDtypeStruct((D_MODEL, D_MODEL), BF16),
            jax.ShapeDtypeStruct((D_MODEL, 2 * D_FF), BF16),
            jax.ShapeDtypeStruct((N_CHIPS,) + wa_s.shape, F32), jax.ShapeDtypeStruct((N_CHIPS,) + wf_s.shape, F32),
        ],
        scratch_shapes=[
            pltpu.VMEM((D_MODEL, D_IN), BF16), pltpu.VMEM((D_MODEL, D_MODEL), BF16), pltpu.VMEM((32, D_CONV), F32),
            pltpu.VMEM((tr + A_HALO, D_CONV), F32), pltpu.VMEM((7, tr + A_HALO - 8, D_CONV), F32),
            pltpu.VMEM((tr + P_HALO, D_POOL), F32), pltpu.SemaphoreType.DMA((2 + N_CHIPS,)),
        ] + _gather_scratch((win_b,)) + _gather_scratch((wout_b,)) + _gather_scratch((wup_b,)) + [
            pltpu.SemaphoreType.DMA((6,)), pltpu.SemaphoreType.DMA((6,)), pltpu.SemaphoreType.DMA((2,))],
        compiler_params=pltpu.CompilerParams(dimension_semantics=("arbitrary",), vmem_limit_bytes=VMEM_LIMIT,
                                             collective_id=SIBLING_AND_CHIPS),
    )(x, g1, win_b, wout_b, wup_b, wa_s, wf_s, cb, lg, lb, pw, ps)


def _ffn_up(x1, g2, wup, wf, fb, wdown_b, tile_rows):
    seq = x1.shape[0]
    tr = tile_rows
    n = seq // tr

    def body(x1_ref, g2_ref, wup_hbm, wf_ref, fb_ref, wdown_b_hbm,
             h2_ref, up_ref, gc_ref, act_ref, wdown_f, wup_v, gbuf, sem, *gsems):
        i = pl.program_id(0)

        def gather():
            return _gather_ops((wdown_b_hbm,), (wdown_f,), (False,), gsems[:6], gsems[6:])

        @pl.when(i == 0)
        def _():
            _handshake(SIBLING_AND_CHIPS)
            gather()[0]()
            _load_weights(((wup_hbm, wup_v),), sem)
            gbuf[0:8, :] = jnp.zeros((8, D_FF), F32)

        x1v = x1_ref[...]
        r2 = lax.rsqrt(_rowmean(x1v * x1v) + EPS)
        h2 = (x1v * r2 * g2_ref[...]).astype(BF16)
        h2_ref[...] = h2

        def up_proj(j):
            return (_dot(h2, wup_v[:, j * UP_CHUNK:(j + 1) * UP_CHUNK]),
                    _dot(h2, wup_v[:, D_FF + j * UP_CHUNK:D_FF + (j + 1) * UP_CHUNK]))

        ahead = up_proj(0)
        for j in range(D_FF // UP_CHUNK):
            cs = slice(j * UP_CHUNK, (j + 1) * UP_CHUNK)
            vs = slice(D_FF + j * UP_CHUNK, D_FF + (j + 1) * UP_CHUNK)
            gate, val = ahead
            if j + 1 < D_FF // UP_CHUNK:
                ahead = up_proj(j + 1)
            up_ref[:, cs] = gate.astype(BF16)
            up_ref[:, vs] = val.astype(BF16)
            gbuf[8:8 + tr, cs] = gate
            gc = (wf_ref[0:1, cs] * gbuf[6:6 + tr, cs] + wf_ref[1:2, cs] * gbuf[7:7 + tr, cs]
                  + wf_ref[2:3, cs] * gate + fb_ref[:, cs])
            gbuf[0:8, cs] = gbuf[tr:tr + 8, cs]
            gc_ref[:, cs] = gc.astype(BF16)
            act_ref[:, cs] = (gc * _sigmoid(gc) * val).astype(BF16)

        @pl.when(i == max(n - 2, 0))
        def _():
            gather()[1]()

        @pl.when(i == n - 1)
        def _():
            gather()[2]()

    tile = lambda w: pl.BlockSpec((tr, w), lambda i: (i, 0))
    full = lambda a: pl.BlockSpec(a.shape, lambda i: (0,) * a.ndim)
    return pl.pallas_call(
        body, name="ffn_up", grid=(n,),
        in_specs=[tile(D_MODEL), full(g2), ANY, full(wf), full(fb), ANY],
        out_specs=[tile(D_MODEL), tile(2 * D_FF), tile(D_FF), tile(D_FF), ANY],
        out_shape=[
            jax.ShapeDtypeStruct((seq, D_MODEL), BF16), jax.ShapeDtypeStruct((seq, 2 * D_FF), BF16),
            jax.ShapeDtypeStruct((seq, D_FF), BF16), jax.ShapeDtypeStruct((seq, D_FF), BF16),
            jax.ShapeDtypeStruct((D_FF, D_MODEL), BF16),
        ],
        scratch_shapes=[pltpu.VMEM(wup.shape, BF16), pltpu.VMEM((tr + 8, D_FF), F32), pltpu.SemaphoreType.DMA((1,))]
        + _gather_scratch((wdown_b,)),
        compiler_params=pltpu.CompilerParams(dimension_semantics=("arbitrary",), vmem_limit_bytes=VMEM_LIMIT,
                                             collective_id=SIBLING_AND_CHIPS),
    )(x1, g2, wup, wf, fb, wdown_b)


def _ffn_down(x1, act, wdown, g3, target, tile_rows):
    seq = x1.shape[0]
    tr = tile_rows
    n = seq // tr

    def body(x1_ref, act_ref, wdown_hbm, g3_ref, t_ref, dx2b_ref, sm_ref, wdown_v, sem):
        i = pl.program_id(0)

        @pl.when(i == 0)
        def _():
            _load_weights(((wdown_hbm, wdown_v),), sem)
            sm_ref[...] = jnp.zeros(sm_ref.shape, F32)

        x2 = x1_ref[...] + _dot(act_ref[...], wdown_v[...])
        r3 = lax.rsqrt(_rowmean(x2 * x2) + EPS)
        n3 = x2 * r3
        err = n3 * g3_ref[...] - t_ref[...]
        dy = err / D_MODEL
        sm_ref[2:3, :] += _colsum(dy * n3)
        loss = 0.5 * _colsum(_rowmean(err * err))
        sm_ref[3:4, :] += jnp.broadcast_to(loss, (1, D_MODEL))
        dn = dy * g3_ref[...]
        dx2b_ref[...] = (r3 * (dn - n3 * _rowmean(dn * n3))).astype(BF16)

    tile = lambda w: pl.BlockSpec((tr, w), lambda i: (i, 0))
    full = lambda a: pl.BlockSpec(a.shape, lambda i: (0,) * a.ndim)
    return pl.pallas_call(
        body, name="ffn_down", grid=(n,),
        in_specs=[tile(D_MODEL), tile(D_FF), ANY, full(g3), tile(D_MODEL)],
        out_specs=[tile(D_MODEL), pl.BlockSpec((8, D_MODEL), lambda i: (0, 0))],
        out_shape=[jax.ShapeDtypeStruct((seq, D_MODEL), BF16), jax.ShapeDtypeStruct((8, D_MODEL), F32)],
        scratch_shapes=[pltpu.VMEM(wdown.shape, BF16), pltpu.SemaphoreType.DMA((1,))],
        compiler_params=pltpu.CompilerParams(dimension_semantics=("arbitrary",), vmem_limit_bytes=VMEM_LIMIT),
    )(x1, act, wdown, g3, target)


def _ffn_bwd(dx2, up, gcs, x1, g2, wup, wf, wdown, comm, tile_rows):
    seq = x1.shape[0]
    c_ins, c_shapes, c_sems, c_ops, c_id = _comm_plan(comm)
    nc = len(c_ins)
    tr = tile_rows
    n = seq // tr

    def body(dx2_ref, up_ref, gc_ref, x1_ref, g2_ref, wup_hbm, wf_ref, wdown_hbm, *rest):
        c_in, rest = rest[:nc], rest[nc:]
        dup_ref, dx1b_ref, sm_ref, sf_ref = rest[:4]
        c_out, rest = rest[4:4 + nc], rest[4 + nc:]
        wup_v, wdown_v, dbuf, dcar, sem = rest[:5]
        c_sem_refs = rest[5:]
        i = pl.program_id(0)

        @pl.when(i == 0)
        def _():
            c_ops(c_in, c_out, c_sem_refs)[0]()
            _load_weights(((wup_hbm, wup_v), (wdown_hbm, wdown_v)), sem)
            dcar[...] = jnp.zeros(dcar.shape, F32)
            sm_ref[...] = jnp.zeros(sm_ref.shape, F32)
            sf_ref[...] = jnp.zeros(sf_ref.shape, F32)

        dx2b = dx2_ref[...]
        dx2v = dx2b.astype(F32)
        dh2 = jnp.zeros((tr, D_MODEL), F32)

        def down_t(j):
            return _dot_nt(dx2b, wdown_v[j * FF_CHUNK:(j + 1) * FF_CHUNK, :])

        ahead = down_t(0)
        for j in range(N_FF_CHUNKS):
            cs = slice(j * FF_CHUNK, (j + 1) * FF_CHUNK)
            vs = slice(D_FF + j * FF_CHUNK, D_FF + (j + 1) * FF_CHUNK)
            dact = ahead
            if j + 1 < N_FF_CHUNKS:
                ahead = down_t(j + 1)
            gate = up_ref[:, cs].astype(F32)
            val = up_ref[:, vs].astype(F32)
            gc = gc_ref[:, cs].astype(F32)
            sg = _sigmoid(gc)
            dval = dact * (gc * sg)
            dgc = dact * val * (sg * (1.0 + gc * (1.0 - sg)))
            dbuf[0:tr, :] = dgc
            dbuf[tr:tr + 8, :] = dcar[:, cs]
            d_p1 = dbuf[1:1 + tr, :]
            d_p2 = dbuf[2:2 + tr, :]
            dgate = wf_ref[2:3, cs] * dgc + wf_ref[1:2, cs] * d_p1 + wf_ref[0:1, cs] * d_p2
            dcar[:, cs] = dgc[0:8, :]
            sf_ref[0:1, cs] += _colsum(d_p2 * gate)
            sf_ref[1:2, cs] += _colsum(d_p1 * gate)
            sf_ref[2:3, cs] += _colsum(dgc * gate)
            sf_ref[3:4, cs] += _colsum(dgc)
            dgb, dvb = dgate.astype(BF16), dval.astype(BF16)
            dup_ref[:, cs] = dgb
            dup_ref[:, vs] = dvb
            dh2 = dh2 + _dot_nt(dgb, wup_v[:, cs]) + _dot_nt(dvb, wup_v[:, vs])
        x1v = x1_ref[...]
        r2 = lax.rsqrt(_rowmean(x1v * x1v) + EPS)
        n2 = x1v * r2
        sm_ref[1:2, :] += _colsum(dh2 * n2)
        dn2 = dh2 * g2_ref[...]
        dx1b_ref[...] = (dx2v + r2 * (dn2 - n2 * _rowmean(dn2 * n2))).astype(BF16)

        @pl.when(i == n - 1)
        def _():
            c_ops(c_in, c_out, c_sem_refs)[2]()

    tile = lambda w: pl.BlockSpec((tr, w), lambda i: (n - 1 - i, 0))
    full = lambda a: pl.BlockSpec(a.shape, lambda i: (0,) * a.ndim)
    acc = lambda rows, w: pl.BlockSpec((rows, w), lambda i: (0, 0))
    return pl.pallas_call(
        body, name="ffn_bwd", grid=(n,),
        in_specs=[tile(D_MODEL), tile(2 * D_FF), tile(D_FF), tile(D_MODEL), full(g2), ANY, full(wf), ANY] + [ANY] * nc,
        out_specs=[tile(2 * D_FF), tile(D_MODEL), acc(8, D_MODEL), acc(8, D_FF)] + [ANY] * nc,
        out_shape=[
            jax.ShapeDtypeStruct((seq, 2 * D_FF), BF16), jax.ShapeDtypeStruct((seq, D_MODEL), BF16),
            jax.ShapeDtypeStruct((8, D_MODEL), F32), jax.ShapeDtypeStruct((8, D_FF), F32),
        ] + c_shapes,
        scratch_shapes=[
            pltpu.VMEM(wup.shape, BF16), pltpu.VMEM(wdown.shape, BF16),
            pltpu.VMEM((tr + 8, FF_CHUNK), F32), pltpu.VMEM((8, D_FF), F32), pltpu.SemaphoreType.DMA((2,)),
        ] + c_sems,
        compiler_params=pltpu.CompilerParams(dimension_semantics=("arbitrary",), vmem_limit_bytes=VMEM_LIMIT,
                                             collective_id=c_id),
    )(dx2, up, gcs, x1, g2, wup, wf, wdown, *c_ins)


def _mixer_bwd(dx1, x, proj, cpre, d, g1, win, wa, lg, lb, pw, ps, wout, parts, tile_rows):
    seq = x.shape[0]
    n_parts = len(parts)
    tr = tile_rows
    n = seq // tr
    row_cb, row_lg, row_lb, row_ps = 32, 33, 34, 35

    def body(dx1_ref, x_ref, proj_ref, projh_ref, c_ref, d_ref, g1_ref, win_hbm, wa_ref, lg_ref, lb_ref, pw_ref, ps_ref,
             wout_hbm, *rest):
        part_refs, rest = rest[:n_parts], rest[n_parts:]
        dproj_ref, gx_ref, sm_ref, s5_ref, sp_ref = rest[:5]
        land_refs, rest = rest[5:5 + n_parts], rest[5 + n_parts:]
        win_v, wout_v, ubuf, ushift, dcbuf, dshift, ebuf, sem = rest[:8]
        ssems = rest[8:]
        i = pl.program_id(0)
        tile = n - 1 - i

        def scatter():
            return _scatter_ops(part_refs, land_refs, n_parts, ssems[:6], ssems[6:])

        @pl.when(i == 0)
        def _():
            _handshake(SIBLING_AND_CHIPS)
            scatter()[0]()
            _load_weights(((win_hbm, win_v), (wout_hbm, wout_v)), sem)
            dcbuf[tr:tr + A_HALO, :] = jnp.zeros((A_HALO, D_CONV), F32)
            ebuf[tr:tr + P_HALO, :] = jnp.zeros((P_HALO, D_POOL), F32)
            sm_ref[...] = jnp.zeros(sm_ref.shape, F32)
            s5_ref[...] = jnp.zeros(s5_ref.shape, F32)
            sp_ref[...] = jnp.zeros(sp_ref.shape, F32)

        dx1b = dx1_ref[...]
        dx1v = dx1b.astype(F32)
        dm = _dot_nt(dx1b, wout_v[...])
        dya, dyb = dm[:, :D_CONV], dm[:, D_CONV:]
        dbis = []
        for g, w in enumerate(POOL_WINDOWS):
            cols = slice(g * POOL_GROUP, (g + 1) * POOL_GROUP)
            dgb = d_ref[:, cols]
            pwb = pw_ref[g].astype(BF16)
            dyg = dyb[:, cols]
            s5_ref[row_ps:row_ps + 1, cols] += _colsum(dyg * _dot(dgb, pwb))
            dqb = (dyg * ps_ref[:, cols]).astype(BF16)
            sp_ref[g] += _dot_tn(dgb, dqb)
            dd = _dot_nt(dqb, pwb)
            e = dd / _pool_count(tile, tr, w)
            ebuf[0:tr, cols] = e
            s = e
            for kk in range(1, w):
                s = s + ebuf[kk:kk + tr, cols]
            dbis.append(s - dd)
        ebuf[tr:tr + P_HALO, :] = ebuf[0:P_HALO, :]
        cv = c_ref[...].astype(F32)
        xc = cv - _rowmean(cv)
        rs = lax.rsqrt(_rowmean(xc * xc) + EPS)
        z = xc * rs
        ln = z * lg_ref[...] + lb_ref[...]
        sl = _sigmoid(ln)
        dl = dya * (sl * (1.0 + ln * (1.0 - sl)))
        s5_ref[row_lg:row_lg + 1, :] += _colsum(dl * z)
        s5_ref[row_lb:row_lb + 1, :] += _colsum(dl)
        dz = dl * lg_ref[...]
        dc = rs * (dz - _rowmean(dz) - z * _rowmean(dz * z))
        s5_ref[row_cb:row_cb + 1, :] += _colsum(dc)
        dcbuf[0:tr, :] = dc
        keep = (tile > 0).astype(F32)
        avh = projh_ref[:, :D_CONV].astype(F32)
        agh = projh_ref[:, D_CONV:].astype(F32)
        ubuf[0:A_HALO, :] = avh * _sigmoid(agh) * keep
        av = proj_ref[:, :D_CONV].astype(F32)
        ag = proj_ref[:, D_CONV:2 * D_CONV].astype(F32)
        sg = _sigmoid(ag)
        ubuf[A_HALO:A_HALO + tr, :] = av * sg
        off = A_HALO - (CONV_A - 1)
        du = wa_ref[CONV_A - 1:CONV_A, :] * dc
        dview = _shifted_views(dcbuf, dshift, tr)
        uview = _shifted_views(ubuf, ushift, tr)
        for j in range(CONV_A - 1):
            du = du + wa_ref[j:j + 1, :] * dview(CONV_A - 1 - j)
        for j in range(CONV_A):
            s5_ref[j:j + 1, :] += _colsum(dc * uview(off + j))
        dcbuf[tr:tr + A_HALO, :] = dcbuf[0:A_HALO, :]
        dav = du * sg
        dag = du * av * (sg * (1.0 - sg))
        dprojb = jnp.concatenate([dav, dag] + dbis, axis=1).astype(BF16)
        dproj_ref[...] = dprojb
        dh1 = _dot_nt(dprojb, win_v[...])
        xv = x_ref[...]
        r1 = lax.rsqrt(_rowmean(xv * xv) + EPS)
        n1 = xv * r1
        sm_ref[0:1, :] += _colsum(dh1 * n1)
        dn1 = dh1 * g1_ref[...]
        gx_ref[...] = dx1v + r1 * (dn1 - n1 * _rowmean(dn1 * n1))

        @pl.when(i == max(n - 2, 0))
        def _():
            scatter()[1]()

        @pl.when(i == n - 1)
        def _():
            scatter()[2]()

    tile = lambda w: pl.BlockSpec((tr, w), lambda i: (n - 1 - i, 0))
    full = lambda a: pl.BlockSpec(a.shape, lambda i: (0,) * a.ndim)
    halo = pl.BlockSpec((A_HALO, 2 * D_CONV), lambda i: (jnp.maximum((n - 1 - i) * (tr // A_HALO) - 1, 0), 0))
    acc = lambda shape: pl.BlockSpec(shape, lambda i: (0,) * len(shape))
    return pl.pallas_call(
        body, name="mixer_bwd", grid=(n,),
        in_specs=[tile(D_MODEL), tile(D_MODEL), tile(D_IN), halo, tile(D_CONV), tile(D_POOL), full(g1), ANY, full(wa),
                  full(lg), full(lb), full(pw), full(ps), ANY] + [ANY] * n_parts,
        out_specs=[tile(D_IN), tile(D_MODEL), acc((8, D_MODEL)), acc((40, D_CONV)), acc(pw.shape)] + [ANY] * n_parts,
        out_shape=[
            jax.ShapeDtypeStruct((seq, D_IN), BF16), jax.ShapeDtypeStruct((seq, D_MODEL), F32),
            jax.ShapeDtypeStruct((8, D_MODEL), F32), jax.ShapeDtypeStruct((40, D_CONV), F32),
            jax.ShapeDtypeStruct(pw.shape, F32),
        ] + _scatter_shapes(parts, ()),
        scratch_shapes=[
            pltpu.VMEM(win.shape, BF16), pltpu.VMEM(wout.shape, BF16),
            pltpu.VMEM((tr + A_HALO, D_CONV), F32), pltpu.VMEM((7, tr + A_HALO - 8, D_CONV), F32),
            pltpu.VMEM((tr + A_HALO, D_CONV), F32), pltpu.VMEM((7, tr + A_HALO - 8, D_CONV), F32),
            pltpu.VMEM((tr + P_HALO, D_POOL), F32), pltpu.SemaphoreType.DMA((2,)),
        ] + _scatter_scratch(parts, ()),
        compiler_params=pltpu.CompilerParams(dimension_semantics=("arbitrary",), vmem_limit_bytes=VMEM_LIMIT,
                                             collective_id=SIBLING_AND_CHIPS),
    )(dx1, x, proj, proj, cpre, d, g1, win, wa, lg, lb, pw, ps, wout, *parts)


def _weight_grad(a, b, layout, k_rows, comm=None, carry=None):
    seq, m_dim = a.shape
    n_dim = b.shape[1]
    steps = seq // k_rows

    def store(o_ref, acc, index, value):
        if steps == 1:
            o_ref[index] = value.astype(BF16)
            return
        s = pl.program_id(1)

        @pl.when(s == 0)
        def _():
            acc[index] = value

        @pl.when(jnp.logical_and(s > 0, s < steps - 1))
        def _():
            acc[index] += value

        @pl.when(s == steps - 1)
        def _():
            o_ref[index] = (acc[index] + value).astype(BF16)

    if layout in ("rows1", "rows2"):
        groups = int(layout[-1])
        per_tile = N_CHIPS // groups
        rows = m_dim // N_CHIPS // 2
        a_w = m_dim // groups

        def body(a_ref, b_ref, o_ref, acc):
            r = _dot_tn(a_ref[...], b_ref[...])
            for p in range(per_tile):
                for h in range(2):
                    store(o_ref, acc, (h, p), r[(2 * p + h) * rows:(2 * p + h + 1) * rows, :])

        in_specs = [pl.BlockSpec((k_rows, a_w), lambda g, s: (s, g)), pl.BlockSpec((k_rows, n_dim), lambda g, s: (s, 0))]
        out_spec = pl.BlockSpec((2, per_tile, rows, n_dim), lambda g, s: (0, g, 0, 0))
        out_dims, acc_dims = (2, N_CHIPS, rows, n_dim), (2, per_tile, rows, n_dim)
    elif layout == "cols_chip":
        groups = N_CHIPS
        rows, cols = m_dim // 2, n_dim // N_CHIPS

        def body(a_ref, b_ref, o_ref, acc):
            r = _dot_tn(a_ref[...], b_ref[...])
            for h in range(2):
                store(o_ref, acc, h, r[h * rows:(h + 1) * rows, :])

        in_specs = [pl.BlockSpec((k_rows, m_dim), lambda g, s: (s, 0)), pl.BlockSpec((k_rows, cols), lambda g, s: (s, g))]
        out_spec = pl.BlockSpec((2, None, rows, cols), lambda g, s: (0, g, 0, 0))
        out_dims, acc_dims = (2, N_CHIPS, rows, cols), (2, rows, cols)
    else:
        groups = 2
        rows, cols = m_dim // 2, n_dim // N_CHIPS

        def body(a_ref, b_ref, o_ref, acc):
            r = _dot_tn(a_ref[...], b_ref[...])
            for k in range(N_CHIPS):
                store(o_ref, acc, k, r[:, k * cols:(k + 1) * cols])

        in_specs = [pl.BlockSpec((k_rows, rows), lambda g, s: (s, g)), pl.BlockSpec((k_rows, n_dim), lambda g, s: (s, 0))]
        out_spec = pl.BlockSpec((None, N_CHIPS, rows, cols), lambda g, s: (g, 0, 0, 0))
        out_dims, acc_dims = (2, N_CHIPS, rows, cols), (N_CHIPS, rows, cols)

    c_ins, c_shapes, c_sems, c_ops, c_id = _comm_plan(comm)
    nc = len(c_ins)
    c_specs = [ANY] * nc
    if carry is not None:
        assert comm is None and carry.shape[0] % (groups * steps) == 0
        carry_spec = pl.BlockSpec((carry.shape[0] // (groups * steps), carry.shape[1]), lambda g, s: (g * steps + s, 0))
        c_ins, c_shapes, c_specs, nc = (carry,), [jax.ShapeDtypeStruct(carry.shape, carry.dtype)], [carry_spec], 1

    def hosted(a_ref, b_ref, *rest):
        c_in, o_ref, c_out, acc, sems = rest[:nc], rest[nc], rest[nc + 1:2 * nc + 1], rest[2 * nc + 1], rest[2 * nc + 2:]
        g, s = pl.program_id(0), pl.program_id(1)
        if carry is not None:
            c_out[0][...] = c_in[0][...]
            body(a_ref, b_ref, o_ref, acc)
            return
        if nc:
            @pl.when(jnp.logical_and(g == 0, s == 0))
            def _():
                c_ops(c_in, c_out, sems)[0]()

        body(a_ref, b_ref, o_ref, acc)
        if nc:
            step = g * steps + s

            @pl.when(step == max(groups * steps - 2, 0))
            def _():
                c_ops(c_in, c_out, sems)[1]()

            @pl.when(step == groups * steps - 1)
            def _():
                c_ops(c_in, c_out, sems)[2]()

    outs = pl.pallas_call(
        hosted, name=f"weight_grad_{layout}_{m_dim}x{n_dim}", grid=(groups, steps),
        in_specs=in_specs + c_specs, out_specs=[out_spec] + c_specs,
        out_shape=[jax.ShapeDtypeStruct(out_dims, BF16)] + c_shapes,
        scratch_shapes=[pltpu.VMEM(acc_dims, F32)] + c_sems,
        compiler_params=pltpu.CompilerParams(dimension_semantics=("arbitrary", "arbitrary"), vmem_limit_bytes=VMEM_LIMIT,
                                             collective_id=c_id),
    )(a, b, *c_ins)
    return outs if nc else outs[0]


def _exchange_ops(ins, outs, n_big, sems):
    send, recv = sems
    x, y, c, _, _ = _place()
    cps = [pltpu.make_async_remote_copy(
        src_ref=ins[t].at[1 - c] if t < n_big else ins[t], dst_ref=outs[t], send_sem=send.at[t], recv_sem=recv.at[t],
        device_id=(x, y, 1 - c), device_id_type=MESH) for t in range(len(ins))]

    def start():
        for cp in cps:
            cp.start()

    def finish():
        for cp in cps:
            cp.wait()

    return start, finish


def _exchange_shapes(bigs, smalls):
    return [jax.ShapeDtypeStruct((N_CHIPS,) + b.shape[2:], b.dtype) for b in bigs] + [
        jax.ShapeDtypeStruct(s.shape, s.dtype) for s in smalls]


def _comm_plan(comm):
    if comm is None:
        return (), [], [], None, None
    kind, arrays = comm
    n = len(arrays)

    def scatter(i, o, sm):
        start, land, finish = _scatter_ops(i, o, n, sm[:6], sm[6:])
        return lambda: (_handshake(SIBLING_AND_CHIPS), start()), land, finish

    def exchange(i, o, sm):
        start, finish = _exchange_ops(i, o, n, sm)
        return lambda: (_handshake(SIBLING_ONLY), start()), lambda: None, finish

    if kind == "scatter":
        return tuple(arrays), _scatter_shapes(arrays, ()), _scatter_scratch(arrays, ()), scatter, SIBLING_AND_CHIPS
    return tuple(arrays), _exchange_shapes(arrays, ()), [pltpu.SemaphoreType.DMA((n,))] * 2, exchange, SIBLING_ONLY


def _sibling_exchange(bigs, smalls, tag):
    nb, nt = len(bigs), len(bigs) + len(smalls)

    def body(*refs):
        start, finish = _exchange_ops(refs[:nt], refs[nt:2 * nt], nb, refs[2 * nt:])
        _handshake(SIBLING_ONLY)
        start()
        finish()

    return pl.pallas_call(
        body, name=f"sibling_exchange_{tag}", out_shape=_exchange_shapes(bigs, smalls),
        in_specs=[ANY] * nt, out_specs=[ANY] * nt,
        scratch_shapes=[pltpu.SemaphoreType.DMA((nt,)), pltpu.SemaphoreType.DMA((nt,))],
        compiler_params=pltpu.CompilerParams(collective_id=SIBLING_ONLY),
    )(*bigs, *smalls)


def _pair_sum(core, mine, theirs, tag, block_rows):
    _, _, rows, cols = mine.shape
    steps = rows // block_rows

    def body(core_ref, a_ref, b_ref, o_ref):
        o_ref[...] = (a_ref[...].astype(F32) + b_ref[...].astype(F32)).astype(BF16)

    grid_spec = pltpu.PrefetchScalarGridSpec(
        num_scalar_prefetch=1, grid=(N_CHIPS, steps),
        in_specs=[pl.BlockSpec((None, None, block_rows, cols), lambda k, r, core_ref: (core_ref[0], k, r, 0)),
                  pl.BlockSpec((None, block_rows, cols), lambda k, r, core_ref: (k, r, 0))],
        out_specs=pl.BlockSpec((None, block_rows, cols), lambda k, r, core_ref: (k, r, 0)),
    )
    return pl.pallas_call(
        body, name=f"pair_sum_{tag}", grid_spec=grid_spec,
        out_shape=jax.ShapeDtypeStruct((N_CHIPS, rows, cols), BF16),
        compiler_params=pltpu.CompilerParams(dimension_semantics=("arbitrary", "arbitrary"), vmem_limit_bytes=VMEM_LIMIT),
    )(core, mine, theirs)


def _pair_sum_small(mine, theirs):
    (m_f2, m_b1, m_b2, m_sf, m_s5, m_sp) = mine

    def body(a0, a1, a2, a3, a4, a5, b0, b1, b2, b3, b4, b5, o_m, o_f, o_5, o_p):
        sm = (a0[...] + a1[...] + a2[...]) + (b0[...] + b1[...] + b2[...])
        sf = a3[...] + b3[...]
        s5 = a4[...] + b4[...]
        for h in range(2):
            o_m[h] = sm[:, h * (D_MODEL // 2):(h + 1) * (D_MODEL // 2)]
            o_f[h] = sf[:, h * (D_FF // 2):(h + 1) * (D_FF // 2)]
            o_5[h] = s5[:, h * (D_CONV // 2):(h + 1) * (D_CONV // 2)]
            for g in range(2):
                o_p[h, g] = a5[2 * h + g] + b5[2 * h + g]

    out_shape = [
        jax.ShapeDtypeStruct((2, 8, D_MODEL // 2), F32), jax.ShapeDtypeStruct((2, 8, D_FF // 2), F32),
        jax.ShapeDtypeStruct((2, 40, D_CONV // 2), F32), jax.ShapeDtypeStruct((2, 2, POOL_GROUP, POOL_GROUP), F32),
    ]
    return pl.pallas_call(body, name="pair_sum_small", out_shape=out_shape, in_specs=[VMEM] * 12, out_specs=[VMEM] * 4)(
        *mine, *theirs)


def _scatter_ops(ins, outs, n_parts, sems, stages, landed=False):
    ici_send, ici_recv, fwd_send, fwd_recv, loc_in, loc_out = sems
    nt = len(ins)
    x, y, c, k, chips = _place()

    def src_of(t, kk):
        return ins[t].at[kk] if t < n_parts else ins[t].at[c]

    def ici(t, j, kk, slot):
        return pltpu.make_async_remote_copy(
            src_ref=src_of(t, kk), dst_ref=outs[t].at[c, slot], send_sem=ici_send.at[t * 3 + j],
            recv_sem=ici_recv.at[t * 3 + j], device_id=(*chips[j], c), device_id_type=MESH)

    def fwd(t, half):
        slots = outs[t].at[half]
        return pltpu.make_async_remote_copy(
            src_ref=slots, dst_ref=slots, send_sem=fwd_send.at[t], recv_sem=fwd_recv.at[t],
            device_id=(x, y, 1 - c), device_id_type=MESH)

    local = [_staged(src_of(t, k), outs[t].at[c, k], stages[t], loc_in.at[t], loc_out.at[t]) for t in range(nt)]
    peers = [(t, j, 2 * qx + qy) for t in range(nt) for j, (qx, qy) in enumerate(chips)]
    sends = [] if landed else [ici(t, j, kq, k) for t, j, kq in peers]

    def start():
        for cp in local:
            cp[0]()
        for cp in sends:
            cp.start()

    def land():
        for cp in local:
            cp[1]()
        if not landed:
            for t, j, kq in peers:
                ici(t, j, kq, kq).wait_recv()
        for cp in local:
            cp[2]()
        for t in range(nt):
            fwd(t, c).start()

    def finish():
        for t in range(nt):
            fwd(t, 1 - c).wait_recv()
            fwd(t, c).wait_send()
        for cp in sends:
            cp.wait_send()

    return start, land, finish


def _scatter_scratch(parts, smalls):
    arrays = tuple(parts) + tuple(smalls)
    nt = len(arrays)
    return ([pltpu.SemaphoreType.DMA((3 * nt,))] * 2 + [pltpu.SemaphoreType.DMA((nt,))] * 4
            + [pltpu.VMEM(a.shape[1:], a.dtype) for a in arrays])


def _scatter_shapes(parts, smalls):
    return [jax.ShapeDtypeStruct((2, N_CHIPS) + p.shape[1:], p.dtype) for p in tuple(parts) + tuple(smalls)]


HBM_SPEC = pl.BlockSpec(memory_space=pltpu.HBM)
SEM_SPEC = pl.BlockSpec(memory_space=pltpu.SEMAPHORE)
EFFECT = pltpu.SideEffectType.DATAFLOW_SIDE_EFFECTING


def _ici_copy(ins, lands, n_parts, send, recv, t, j):
    _, _, c, k, chips = _place()
    qx, qy = chips[j]
    src = ins[t].at[2 * qx + qy] if t < n_parts else ins[t].at[c]
    return pltpu.make_async_remote_copy(
        src_ref=src, dst_ref=lands[t].at[c, k], send_sem=send.at[t * 3 + j], recv_sem=recv.at[t * 3 + j],
        device_id=(qx, qy, c), device_id_type=MESH)


def _scatter_start(parts, smalls):
    arrays = tuple(parts) + tuple(smalls)
    nt = len(arrays)

    def body(*refs):
        ins, lands = refs[:nt], refs[nt:2 * nt]
        send, recv = refs[2 * nt], refs[2 * nt + 1]
        token = refs[-1]
        for t in range(nt):
            for j in range(3):
                _ici_copy(ins, lands, len(parts), send, recv, t, j).start()
        token[...] = jnp.zeros(token.shape, F32)

    land_shapes = _scatter_shapes(parts, smalls)
    out_shape = ([pltpu.SemaphoreType.DMA((3 * nt,))] * 2 + [pltpu.HBM(a.shape, a.dtype) for a in arrays]
                 + [pltpu.HBM(a.shape, a.dtype) for a in land_shapes] + [jax.ShapeDtypeStruct((8, 128), F32)])
    operands = [pltpu.with_memory_space_constraint(a, pltpu.HBM) for a in arrays]
    operands += [pltpu.with_memory_space_constraint(lax.empty(a.shape, a.dtype), pltpu.HBM) for a in land_shapes]
    outs = pl.pallas_call(
        body, name="scatter_start", out_shape=out_shape, in_specs=[HBM_SPEC] * (2 * nt),
        out_specs=[SEM_SPEC] * 2 + [HBM_SPEC] * (2 * nt) + [VMEM],
        input_output_aliases={i: 2 + i for i in range(2 * nt)},
        compiler_params=pltpu.CompilerParams(has_side_effects=EFFECT),
    )(*operands)
    return outs[0], outs[1], outs[2:2 + nt], outs[2 + nt:2 + 2 * nt], outs[-1]


def _scatter_wait(send, recv, ins, lands, n_parts, after):
    nt = len(ins)

    def body(*refs):
        in_refs, land_refs = refs[:nt], refs[nt:2 * nt]
        send_ref, recv_ref = refs[2 * nt], refs[2 * nt + 1]
        for t in range(nt):
            for j in range(3):
                cp = _ici_copy(in_refs, land_refs, n_parts, send_ref, recv_ref, t, j)
                cp.wait_send()
                cp.wait_recv()

    outs = pl.pallas_call(
        body, name="scatter_wait", out_shape=[pltpu.HBM(a.shape, a.dtype) for a in tuple(ins) + tuple(lands)],
        in_specs=[HBM_SPEC] * (2 * nt) + [SEM_SPEC] * 2 + [ANY] * len(after), out_specs=[HBM_SPEC] * (2 * nt),
        input_output_aliases={i: i for i in range(2 * nt)},
        compiler_params=pltpu.CompilerParams(has_side_effects=EFFECT),
    )(*ins, *lands, send, recv, *after)
    return outs[:nt], outs[nt:]


def _scatter_forward(ins, lands, n_parts):
    nt = len(ins)

    def body(*refs):
        start, land, finish = _scatter_ops(
            refs[:nt], refs[2 * nt:3 * nt], n_parts, refs[3 * nt:3 * nt + 6], refs[3 * nt + 6:], landed=True)
        _handshake(SIBLING_ONLY)
        start()
        land()
        finish()

    return pl.pallas_call(
        body, name="scatter_forward", out_shape=[jax.ShapeDtypeStruct(a.shape, a.dtype) for a in lands],
        in_specs=[ANY] * (2 * nt), out_specs=[ANY] * nt, input_output_aliases={nt + i: i for i in range(nt)},
        scratch_shapes=_scatter_scratch(ins[:n_parts], ins[n_parts:]),
        compiler_params=pltpu.CompilerParams(collective_id=SIBLING_ONLY),
    )(*ins, *lands)


def _chip_scatter(parts, smalls):
    nt = len(parts) + len(smalls)

    def body(*refs):
        start, land, finish = _scatter_ops(refs[:nt], refs[nt:2 * nt], len(parts), refs[2 * nt:2 * nt + 6], refs[2 * nt + 6:])
        _handshake(SIBLING_AND_CHIPS)
        start()
        land()
        finish()

    return pl.pallas_call(
        body, name="chip_scatter", out_shape=_scatter_shapes(parts, smalls), in_specs=[ANY] * nt, out_specs=[ANY] * nt,
        scratch_shapes=_scatter_scratch(parts, smalls),
        compiler_params=pltpu.CompilerParams(collective_id=SIBLING_AND_CHIPS),
    )(*parts, *smalls)


def _adamw(w, g, m, v):
    m = ADAM_B1 * m + (1.0 - ADAM_B1) * g
    v = ADAM_B2 * v + (1.0 - ADAM_B2) * (g * g)
    m_hat = m / (1.0 - ADAM_B1 ** ADAM_STEP)
    v_hat = v / (1.0 - ADAM_B2 ** ADAM_STEP)
    delta = -ADAM_LR * (m_hat / (jnp.sqrt(v_hat) + ADAM_EPS) + ADAM_WD * w)
    return delta, m, v


def _adam_big(parts, w, m, v, tag, block_rows, token):
    _, _, rows, cols = parts.shape
    steps = rows // block_rows

    def body(p_ref, w_ref, m_ref, v_ref, token_ref, g_out, d_out, m_out, v_out):
        g = p_ref[0].astype(F32)
        for q in range(1, N_CHIPS):
            g = g + p_ref[q].astype(F32)
        delta, m_new, v_new = _adamw(w_ref[...], g, m_ref[...], v_ref[...])
        g_out[...] = g
        d_out[...] = delta
        m_out[...] = m_new
        v_out[...] = v_new

    blk = pl.BlockSpec((block_rows, cols), lambda h, r: (h * steps + r, 0))
    return pl.pallas_call(
        body, name=f"adam_{tag}", grid=(2, steps),
        in_specs=[pl.BlockSpec((None, N_CHIPS, block_rows, cols), lambda h, r: (h, 0, r, 0)), blk, blk, blk, ANY],
        out_specs=[blk] * 4, out_shape=[jax.ShapeDtypeStruct(w.shape, F32)] * 4,
        compiler_params=pltpu.CompilerParams(dimension_semantics=("arbitrary", "arbitrary"), vmem_limit_bytes=VMEM_LIMIT),
    )(parts, w, m, v, token)


def _reduce_small(l_m, l_f, l_5, l_p):
    def total(ref):
        t = ref[:, 0]
        for q in range(1, N_CHIPS):
            t = t + ref[:, q]
        return t

    def body(m_ref, f_ref, s_ref, p_ref, g1_o, g2_o, g3_o, loss_o, wf_o, fb_o, wa_o, cb_o, lg_o, lb_o, ps_o, pw_o):
        tm, tf, t5, tp = total(m_ref), total(f_ref), total(s_ref), total(p_ref)
        sm = jnp.concatenate([tm[0], tm[1]], axis=1)
        sf = jnp.concatenate([tf[0], tf[1]], axis=1)
        s5 = jnp.concatenate([t5[0], t5[1]], axis=1)
        g1_o[...] = sm[0:1]
        g2_o[...] = sm[1:2]
        g3_o[...] = sm[2:3]
        loss_o[...] = sm[3:4, 0:128]
        wf_o[...] = sf
        fb_o[...] = sf[3:4]
        wa_o[...] = s5[0:32]
        cb_o[...] = s5[32:33]
        lg_o[...] = s5[33:34]
        lb_o[...] = s5[34:35]
        ps_o[...] = s5[35:36]
        for h in range(2):
            for g in range(2):
                pw_o[2 * h + g] = tp[h, g]

    row = lambda w: jax.ShapeDtypeStruct((1, w), F32)
    out_shape = [row(D_MODEL), row(D_MODEL), row(D_MODEL), row(128), jax.ShapeDtypeStruct((8, D_FF), F32), row(D_FF),
                 jax.ShapeDtypeStruct((32, D_CONV), F32), row(D_CONV), row(D_CONV), row(D_CONV), row(D_POOL),
                 jax.ShapeDtypeStruct((4, POOL_GROUP, POOL_GROUP), F32)]
    return pl.pallas_call(body, name="reduce_small", out_shape=out_shape, in_specs=[VMEM] * 4, out_specs=[VMEM] * 12)(
        l_m, l_f, l_5, l_p)


def _adam_small(ws, gs, ms, vs, by_row):
    count = len(ws)

    def body(*refs):
        w_r, g_r, m_r, v_r = (refs[t * count:(t + 1) * count] for t in range(4))
        outs = [refs[(4 + t) * count:(5 + t) * count] for t in range(4)]
        for t in range(count):
            g = g_r[t][...]
            values = (g,) + _adamw(w_r[t][...], g, m_r[t][...], v_r[t][...])
            for o, value in zip(outs, values):
                if t in by_row:
                    for r in range(value.shape[0]):
                        o[t][r] = value[r:r + 1, :]
                else:
                    o[t][...] = value

    shape = lambda t, w: (w.shape[0], 1, w.shape[1]) if t in by_row else w.shape
    out_shape = [jax.ShapeDtypeStruct(shape(t, w), F32) for t, w in enumerate(ws)] * 4
    outs = pl.pallas_call(body, name="adam_small", out_shape=out_shape, in_specs=[VMEM] * (4 * count),
                          out_specs=[VMEM] * (4 * count))(*ws, *gs, *ms, *vs)
    return [outs[t * count:(t + 1) * count] for t in range(4)]


MIX_TILE = 512
UP_TILE = 512
FFN_TILE = 256
GRAD_K = 2048


def kernel(x, norm_mix_g, w_in, conv_a_w, conv_a_b, ln_a_g, ln_a_b, pool_w, pool_scale, w_out, norm_ffn_g, w_up, conv_f_w, conv_f_b, w_down, norm_final_g, loss_target, m_norm_mix_g, m_w_in, m_conv_a_w, m_conv_a_b, m_ln_a_g, m_ln_a_b, m_pool_w, m_pool_scale, m_w_out, m_norm_ffn_g, m_w_up, m_conv_f_w, m_conv_f_b, m_w_down, m_norm_final_g, v_norm_mix_g, v_w_in, v_conv_a_w, v_conv_a_b, v_ln_a_g, v_ln_a_b, v_pool_w, v_pool_scale, v_w_out, v_norm_ffn_g, v_w_up, v_conv_f_w, v_conv_f_b, v_w_down, v_norm_final_g):
    seq = x.shape[1]
    xs, ts = x[0], loss_target[0]
    mix_tile, ffn_tile, grad_k = min(MIX_TILE, seq), min(FFN_TILE, seq), min(GRAD_K, seq)
    chip = 2 * lax.axis_index("x") + lax.axis_index("y")
    core = lax.axis_index("c").astype(jnp.int32).reshape(1)

    wa_s = jnp.pad(conv_a_w[0], ((0, 32 - CONV_A), (0, 0)))
    wf_s = jnp.pad(conv_f_w[0], ((0, 8 - CONV_F), (0, 0)))
    win_b, wout_b, wup_b, wdown_b = _cast_shards(w_in[0], w_out[0], w_up[0], w_down[0])
    g3 = norm_final_g.reshape(1, D_MODEL)
    pw = pool_w[0]

    h1, proj, cpre, dpool, mcat, x1, win, wout, wup, wa_g, wf_g = _mixer_fwd(
        xs, norm_mix_g, win_b, wout_b, wup_b, wa_s, wf_s, conv_a_b, ln_a_g, ln_a_b, pw, pool_scale, mix_tile)
    wa = jnp.transpose(wa_g, (1, 0, 2)).reshape(32, D_CONV)
    wf = jnp.transpose(wf_g, (1, 0, 2)).reshape(8, D_FF)
    h2, up, gcs, act, wdown = _ffn_up(x1, norm_ffn_g, wup, wf, conv_f_b, wdown_b, min(UP_TILE, seq))
    dx2b, sm_f2 = _ffn_down(x1, act, wdown, g3, ts, mix_tile)
    tags = ("w_in", "w_out", "w_up", "w_down")
    blocks = (256, 128, 256, 176)
    g_wdown = _weight_grad(act, dx2b, "rows2", grad_k)
    dup, dx1b, sm_b1, sf, l_wdown = _ffn_bwd(
        dx2b, up, gcs, x1, norm_ffn_g, wup, wf, wdown, ("exchange", [g_wdown]), ffn_tile)
    p_wdown = _pair_sum(core, g_wdown, l_wdown, tags[3], g_wdown.shape[2])
    g_wup, s_wdown = _weight_grad(h2, dup, "cols_chip", grad_k, ("scatter", [p_wdown]))
    g_wout, l_wup = _weight_grad(mcat, dx1b, "rows1", grad_k, ("exchange", [g_wup]))
    p_wup = _pair_sum(core, g_wup, l_wup, tags[2], g_wup.shape[2])
    l_wout, = _sibling_exchange((g_wout,), (), "early")
    p_wout = _pair_sum(core, g_wout, l_wout, tags[1], g_wout.shape[2])
    dproj, gx, sm_b2, s5, sp, s_wout, s_wup = _mixer_bwd(
        dx1b, xs, proj, cpre, dpool, norm_mix_g, win, wa, ln_a_g, ln_a_b, pw, pool_scale, wout, [p_wout, p_wup], mix_tile)
    g_win, grad_x = _weight_grad(h1, dproj, "cols_half", grad_k, carry=gx)

    smalls = (sm_f2, sm_b1, sm_b2, sf, s5, sp)
    landed = _sibling_exchange((g_win,), smalls, "late")
    part_win = _pair_sum(core, g_win, landed[0], tags[0], g_win.shape[2])
    small_parts = _pair_sum_small(smalls, landed[1:])
    send, recv, late_src, late_land, token = _scatter_start([part_win], small_parts)
    big_w = (w_in[0], w_out[0], w_up[0], w_down[0])
    big_m = (m_w_in[0], m_w_out[0], m_w_up[0], m_w_down[0])
    big_v = (v_w_in[0], v_w_out[0], v_w_up[0], v_w_down[0])
    big = {}
    for t, p in ((1, s_wout), (2, s_wup), (3, s_wdown)):
        big[tags[t]] = _adam_big(p, big_w[t], big_m[t], big_v[t], tags[t], blocks[t], token)
    late_src, late_land = _scatter_wait(send, recv, late_src, late_land, 1, [big[tags[t]][3] for t in (1, 2, 3)])
    late = _scatter_forward(late_src, late_land, 1)
    big[tags[0]] = _adam_big(late[0], big_w[0], big_m[0], big_v[0], tags[0], blocks[0], token)
    big = {tag: [a[None] for a in outs] for tag, outs in big.items()}
    scattered = [None] * 4 + list(late[1:])

    (g_g1, g_g2, g_g3, loss_row, g_wf_all, g_fb, g_wa_all, g_cb, g_lg, g_lb, g_ps, g_pw) = _reduce_small(*scattered[4:])
    g_wa = lax.dynamic_slice(g_wa_all, (0, chip * (D_CONV // N_CHIPS)), (32, D_CONV // N_CHIPS))[:CONV_A]
    g_wf = lax.dynamic_slice(g_wf_all, (0, chip * (D_FF // N_CHIPS)), (8, D_FF // N_CHIPS))[:CONV_F]
    small_names = ("norm_mix_g", "conv_a_w", "conv_a_b", "ln_a_g", "ln_a_b", "pool_w", "pool_scale", "norm_ffn_g",
                   "conv_f_w", "conv_f_b", "norm_final_g")
    small_w = (norm_mix_g, conv_a_w[0], conv_a_b, ln_a_g, ln_a_b, pw, pool_scale, norm_ffn_g, conv_f_w[0], conv_f_b, g3)
    small_m = (m_norm_mix_g, m_conv_a_w[0], m_conv_a_b, m_ln_a_g, m_ln_a_b, m_pool_w[0], m_pool_scale, m_norm_ffn_g,
               m_conv_f_w[0], m_conv_f_b, m_norm_final_g.reshape(1, D_MODEL))
    small_v = (v_norm_mix_g, v_conv_a_w[0], v_conv_a_b, v_ln_a_g, v_ln_a_b, v_pool_w[0], v_pool_scale, v_norm_ffn_g,
               v_conv_f_w[0], v_conv_f_b, v_norm_final_g.reshape(1, D_MODEL))
    small_g = (g_g1, g_wa, g_cb, g_lg, g_lb, g_pw, g_ps, g_g2, g_wf, g_fb, g_g3)
    by_row = (small_names.index("conv_a_w"), small_names.index("conv_f_w"))
    s_g, s_delta, s_m, s_v = _adam_small(small_w, small_g, small_m, small_v, by_row)
    shapes = {"pool_w": pool_w.shape, "norm_final_g": norm_final_g.shape}
    small = {}
    for t, name in enumerate(small_names):
        shp = shapes.get(name)
        if t in by_row:
            small[name] = [jnp.transpose(a, (1, 0, 2)) for a in (s_g[t], s_delta[t], s_m[t], s_v[t])]
        else:
            small[name] = [a if shp is None else a.reshape(shp) for a in (s_g[t], s_delta[t], s_m[t], s_v[t])]

    order = ("norm_mix_g", "w_in", "conv_a_w", "conv_a_b", "ln_a_g", "ln_a_b", "pool_w", "pool_scale", "w_out", "norm_ffn_g",
             "w_up", "conv_f_w", "conv_f_b", "w_down", "norm_final_g")
    table = {**big, **small}
    loss = loss_row[0, 0]
    outs = [loss, grad_x[None]]
    for t in range(4):
        outs += [table[name][t] for name in order]
    return tuple(outs)
```

```python
import functools

import jax
import jax.numpy as jnp
from jax import lax
from jax.experimental import pallas as pl
from jax.experimental.pallas import tpu as pltpu

F32 = jnp.float32
BF16 = jnp.bfloat16
EPS = 1e-6
ADAM_LR = 0.001
ADAM_B1 = 0.9
ADAM_B2 = 0.999
ADAM_EPS = 1e-08
ADAM_WD = 0.01
ADAM_STEP = 10

D_MODEL = 1024
D_CONV = 512
D_POOL = 512
D_IN = 1536
D_FF = 2816
CONV_A = 31
CONV_F = 3
POOL_WINDOWS = (2, 4, 8, 16)
POOL_GROUP = 128
N_CHIPS = 4
FF_CHUNK = 256
N_FF_CHUNKS = D_FF // FF_CHUNK
UP_CHUNK = 2816
A_HALO = 32
P_HALO = 16
VMEM_LIMIT = 56 * 1024 * 1024
MESH = pl.DeviceIdType.MESH

ANY = pl.BlockSpec(memory_space=pl.ANY)
VMEM = pl.BlockSpec(memory_space=pltpu.VMEM)


def _dot(a, b):
    return jnp.dot(a, b, preferred_element_type=F32)


def _dot_nt(a, b):
    return lax.dot_general(a, b, (((1,), (1,)), ((), ())), preferred_element_type=F32)


def _dot_tn(a, b):
    return lax.dot_general(a, b, (((0,), (0,)), ((), ())), preferred_element_type=F32)


def _sigmoid(v):
    return jax.nn.sigmoid(v)


def _colsum(v):
    return jnp.sum(v, axis=0, keepdims=True)


def _rowmean(v):
    return jnp.mean(v, axis=-1, keepdims=True)


def _place():
    x, y, c = lax.axis_index("x"), lax.axis_index("y"), lax.axis_index("c")
    chips = [(1 - x, y), (x, 1 - y), (1 - x, 1 - y)]
    return x, y, c, 2 * x + y, chips


SIBLING_ONLY, SIBLING_AND_CHIPS = 0, 1


def _handshake(collective):
    x, y, c, _, chips = _place()
    peers = [(x, y, 1 - c)] + ([(*chip, c) for chip in chips] if collective == SIBLING_AND_CHIPS else [])
    barrier = pltpu.get_barrier_semaphore()
    for peer in peers:
        pl.semaphore_signal(barrier, inc=1, device_id=peer, device_id_type=MESH)
    pl.semaphore_wait(barrier, len(peers))


def _staged(src, dst, stage, sem_in, sem_out):
    hop_in = pltpu.make_async_copy(src, stage, sem_in)
    hop_out = pltpu.make_async_copy(stage, dst, sem_out)

    def relay():
        hop_in.wait()
        hop_out.start()

    return hop_in.start, relay, hop_out.wait


def _gather_ops(bufs, fulls, col_sharded, sems, stages):
    ici_send, ici_recv, fwd_send, fwd_recv, loc_in, loc_out = sems
    n_big = len(bufs)
    x, y, c, k, chips = _place()

    def block(i, kk, half=None):
        rows, cols = bufs[i].shape
        if col_sharded[i]:
            rs = slice(None) if half is None else pl.ds(pl.multiple_of(half * (rows // 2), 16), rows // 2)
            return fulls[i].at[rs, pl.ds(pl.multiple_of(kk * cols, 128), cols)]
        if half is None:
            return fulls[i].at[pl.ds(pl.multiple_of(kk * rows, 16), rows), :]
        return fulls[i].at[pl.ds(pl.multiple_of(kk * rows + half * (rows // 2), 16), rows // 2), :]

    def my_half(i):
        rows = bufs[i].shape[0]
        return bufs[i].at[pl.ds(pl.multiple_of(c * (rows // 2), 16), rows // 2), :]

    def ici(i, j, kk):
        return pltpu.make_async_remote_copy(
            src_ref=my_half(i), dst_ref=block(i, kk, c), send_sem=ici_send.at[i * 3 + j], recv_sem=ici_recv.at[i * 3 + j],
            device_id=(*chips[j], c), device_id_type=MESH)

    def fwd(i, j, kk, half):
        return pltpu.make_async_remote_copy(
            src_ref=block(i, kk, half), dst_ref=block(i, kk, half), send_sem=fwd_send.at[i * 3 + j],
            recv_sem=fwd_recv.at[i * 3 + j], device_id=(x, y, 1 - c), device_id_type=MESH)

    local = [_staged(bufs[i], block(i, k), stages[i], loc_in.at[i], loc_out.at[i]) for i in range(n_big)]
    sends = [ici(i, j, k) for i in range(n_big) for j in range(3)]
    peers = [(i, j, 2 * qx + qy) for i in range(n_big) for j, (qx, qy) in enumerate(chips)]

    def start():
        for cp in local:
            cp[0]()
        for cp in sends:
            cp.start()

    def land():
        for cp in local:
            cp[1]()
        for i, j, kq in peers:
            ici(i, j, kq).wait_recv()
            fwd(i, j, kq, c).start()

    def finish():
        for i, j, kq in peers:
            fwd(i, j, kq, 1 - c).wait_recv()
            fwd(i, j, kq, c).wait_send()
        for cp in sends:
            cp.wait_send()
        for cp in local:
            cp[2]()

    return start, land, finish


def _gather_scratch(shards):
    n_big = len(shards)
    return ([pltpu.SemaphoreType.DMA((3 * n_big,))] * 4 + [pltpu.SemaphoreType.DMA((n_big,))] * 2
            + [pltpu.VMEM(b.shape, b.dtype) for b in shards])


def _tap_ops(srcs, dsts, sems):
    send, recv, loc = sems
    _, _, c, k, chips = _place()

    def copy(t, j, kk):
        return pltpu.make_async_remote_copy(
            src_ref=srcs[t], dst_ref=dsts[t].at[kk], send_sem=send.at[t * 3 + j], recv_sem=recv.at[t * 3 + j],
            device_id=(*chips[j], c), device_id_type=MESH)

    local = [pltpu.make_async_copy(srcs[t], dsts[t].at[k], loc.at[t]) for t in range(len(srcs))]
    sends = [[copy(t, j, k) for j in range(3)] for t in range(len(srcs))]

    def start():
        for t, cp in enumerate(local):
            cp.start()
            for sd in sends[t]:
                sd.start()

    def wait(t):
        for j, (qx, qy) in enumerate(chips):
            copy(t, j, 2 * qx + qy).wait_recv()
        for sd in sends[t]:
            sd.wait_send()
        local[t].wait()

    return start, wait


def _cast_shards(*shards):
    def body(*refs):
        for src, dst in zip(refs[:len(shards)], refs[len(shards):]):
            dst[...] = src[...].astype(BF16)

    return pl.pallas_call(
        body, name="cast_shards", out_shape=[jax.ShapeDtypeStruct(s.shape, BF16) for s in shards],
        in_specs=[VMEM] * len(shards), out_specs=[VMEM] * len(shards),
        compiler_params=pltpu.CompilerParams(vmem_limit_bytes=VMEM_LIMIT),
    )(*shards)


def _load_weights(pairs, sem, first=0):
    cps = [pltpu.make_async_copy(src, dst, sem.at[first + i]) for i, (src, dst) in enumerate(pairs)]
    for cp in cps:
        cp.start()
    for cp in cps:
        cp.wait()


def _shifted_views(buf, shifted, t_rows):
    n = t_rows + A_HALO - 8
    for b in range(1, 8):
        shifted[b - 1] = buf[b:b + n, :]

    def view(offset):
        a, b = divmod(offset, 8)
        if b == 0:
            return buf[8 * a:8 * a + t_rows, :]
        return shifted[b - 1, 8 * a:8 * a + t_rows, :]

    return view


def _pool_count(tile, t_rows, w):
    row = lax.broadcasted_iota(jnp.int32, (t_rows, POOL_GROUP), 0) + tile * t_rows
    return jnp.minimum(row + 1, w).astype(F32)


def _mixer_fwd(x, g1, win_b, wout_b, wup_b, wa_s, wf_s, cb, lg, lb, pw, ps, tile_rows):
    seq = x.shape[0]
    tr = tile_rows
    n = seq // tr

    def body(x_ref, g1_ref, win_b_hbm, wout_b_hbm, wup_b_hbm, wa_s_hbm, wf_s_hbm, cb_ref, lg_ref, lb_ref, pw_ref,
             ps_ref, h1_ref, proj_ref, c_ref, d_ref, m_ref, x1_ref, win_f, wout_f, wup_f, wa_g, wf_g,
             win_v, wout_v, wa_ref, ubuf, ushift, bbuf, sem, *csems):
        i = pl.program_id(0)
        first_sems, first_stages, second_sems, second_stages, later_sems, later_stages, tap_sems = (
            csems[0:6], csems[6:7], csems[7:13], csems[13:14], csems[14:20], csems[20:21], csems[21:24])

        def first():
            return _gather_ops((win_b_hbm,), (win_f,), (True,), first_sems, first_stages)

        def second():
            return _gather_ops((wout_b_hbm,), (wout_f,), (False,), second_sems, second_stages)

        def later():
            return _gather_ops((wup_b_hbm,), (wup_f,), (True,), later_sems, later_stages)

        def taps():
            return _tap_ops((wa_s_hbm, wf_s_hbm), (wa_g, wf_g), tap_sems)

        @pl.when(i == 0)
        def _():
            _handshake(SIBLING_AND_CHIPS)
            first()[0]()
            taps()[0]()
            second()[0]()
            later()[0]()
            first()[1]()
            first()[2]()
            _load_weights([(win_f, win_v)], sem)
            ubuf[0:A_HALO, :] = jnp.zeros((A_HALO, D_CONV), F32)
            bbuf[0:P_HALO, :] = jnp.zeros((P_HALO, D_POOL), F32)

        xv = x_ref[...]
        r = lax.rsqrt(_rowmean(xv * xv) + EPS)
        h1 = (xv * r * g1_ref[...]).astype(BF16)
        h1_ref[...] = h1
        proj = _dot(h1, win_v[...])
        proj_ref[...] = proj.astype(BF16)

        @pl.when(i == 0)
        def _():
            taps()[1](0)
            _load_weights([(wa_g.at[kk], wa_ref.at[:, kk * (D_CONV // N_CHIPS):(kk + 1) * (D_CONV // N_CHIPS)])
                           for kk in range(N_CHIPS)], sem, 2)

        av, ag, bi = proj[:, :D_CONV], proj[:, D_CONV:2 * D_CONV], proj[:, 2 * D_CONV:]
        ubuf[A_HALO:A_HALO + tr, :] = av * _sigmoid(ag)
        off = A_HALO - (CONV_A - 1)
        uview = _shifted_views(ubuf, ushift, tr)
        acc = wa_ref[0:1, :] * uview(off)
        for j in range(1, CONV_A):
            acc = acc + wa_ref[j:j + 1, :] * uview(off + j)
        cv = acc + cb_ref[...]
        ubuf[0:A_HALO, :] = ubuf[tr:tr + A_HALO, :]
        c_ref[...] = cv.astype(BF16)
        xc = cv - _rowmean(cv)
        z = xc * lax.rsqrt(_rowmean(xc * xc) + EPS)
        ln = z * lg_ref[...] + lb_ref[...]
        ya = ln * _sigmoid(ln)
        bbuf[P_HALO:P_HALO + tr, :] = bi
        ds, ybs = [], []
        for g, w in enumerate(POOL_WINDOWS):
            cols = slice(g * POOL_GROUP, (g + 1) * POOL_GROUP)
            s = bi[:, cols]
            for kk in range(1, w):
                s = s + bbuf[P_HALO - kk:P_HALO - kk + tr, cols]
            dg = s / _pool_count(i, tr, w) - bi[:, cols]
            ds.append(dg)
            ybs.append(_dot(dg.astype(BF16), pw_ref[g].astype(BF16)))
        bbuf[0:P_HALO, :] = bbuf[tr:tr + P_HALO, :]
        d_ref[...] = jnp.concatenate(ds, axis=1).astype(BF16)
        yb = jnp.concatenate(ybs, axis=1) * ps_ref[...]
        m = jnp.concatenate([ya, yb], axis=1).astype(BF16)
        m_ref[...] = m

        @pl.when(i == 0)
        def _():
            second()[1]()
            second()[2]()
            _load_weights([(wout_f, wout_v)], sem, 1)

        x1_ref[...] = xv + _dot(m, wout_v[...])

        @pl.when(i == n - 1)
        def _():
            later()[1]()
            later()[2]()
            taps()[1](1)

    tile = lambda w: pl.BlockSpec((tr, w), lambda i: (i, 0))
    full = lambda a: pl.BlockSpec(a.shape, lambda i: (0,) * a.ndim)
    return pl.pallas_call(
        body, name="mixer_fwd", grid=(n,),
        in_specs=[tile(D_MODEL), full(g1)] + [ANY] * 5 + [full(cb), full(lg), full(lb), full(pw), full(ps)],
        out_specs=[tile(D_MODEL), tile(D_IN), tile(D_CONV), tile(D_POOL), tile(D_MODEL), tile(D_MODEL)] + [ANY] * 5,
        out_shape=[
            jax.ShapeDtypeStruct((seq, D_MODEL), BF16), jax.ShapeDtypeStruct((seq, D_IN), BF16),
            jax.ShapeDtypeStruct((seq, D_CONV), BF16), jax.ShapeDtypeStruct((seq, D_POOL), BF16),
            jax.ShapeDtypeStruct((seq, D_MODEL), BF16), jax.ShapeDtypeStruct((seq, D_MODEL), F32),
            jax.ShapeDtypeStruct((D_MODEL, D_IN), BF16), jax.ShapeDtypeStruct((D_MODEL, D_MODEL), BF16),
            jax.ShapeDtypeStruct((D_MODEL, 2 * D_FF), BF16),
            jax.ShapeDtypeStruct((N_CHIPS,) + wa_s.shape, F32), jax.ShapeDtypeStruct((N_CHIPS,) + wf_s.shape, F32),
        ],
        scratch_shapes=[
            pltpu.VMEM((D_MODEL, D_IN), BF16), pltpu.VMEM((D_MODEL, D_MODEL), BF16), pltpu.VMEM((32, D_CONV), F32),
            pltpu.VMEM((tr + A_HALO, D_CONV), F32), pltpu.VMEM((7, tr + A_HALO - 8, D_CONV), F32),
            pltpu.VMEM((tr + P_HALO, D_POOL), F32), pltpu.SemaphoreType.DMA((2 + N_CHIPS,)),
        ] + _gather_scratch((win_b,)) + _gather_scratch((wout_b,)) + _gather_scratch((wup_b,)) + [
            pltpu.SemaphoreType.DMA((6,)), pltpu.SemaphoreType.DMA((6,)), pltpu.SemaphoreType.DMA((2,))],
        compiler_params=pltpu.CompilerParams(dimension_semantics=("arbitrary",), vmem_limit_bytes=VMEM_LIMIT,
                                             collective_id=SIBLING_AND_CHIPS),
    )(x, g1, win_b, wout_b, wup_b, wa_s, wf_s, cb, lg, lb, pw, ps)


def _ffn_up(x1, g2, wup, wf, fb, wdown_b, tile_rows):
    seq = x1.shape[0]
    tr = tile_rows
    n = seq // tr

    def body(x1_ref, g2_ref, wup_hbm, wf_ref, fb_ref, wdown_b_hbm,
             h2_ref, up_ref, gc_ref, act_ref, wdown_f, wup_v, gbuf, sem, *gsems):
        i = pl.program_id(0)

        def gather():
            return _gather_ops((wdown_b_hbm,), (wdown_f,), (False,), gsems[:6], gsems[6:])

        @pl.when(i == 0)
        def _():
            _handshake(SIBLING_AND_CHIPS)
            gather()[0]()
            _load_weights(((wup_hbm, wup_v),), sem)
            gbuf[0:8, :] = jnp.zeros((8, D_FF), F32)

        x1v = x1_ref[...]
        r2 = lax.rsqrt(_rowmean(x1v * x1v) + EPS)
        h2 = (x1v * r2 * g2_ref[...]).astype(BF16)
        h2_ref[...] = h2

        def up_proj(j):
            return (_dot(h2, wup_v[:, j * UP_CHUNK:(j + 1) * UP_CHUNK]),
                    _dot(h2, wup_v[:, D_FF + j * UP_CHUNK:D_FF + (j + 1) * UP_CHUNK]))

        ahead = up_proj(0)
        for j in range(D_FF // UP_CHUNK):
            cs = slice(j * UP_CHUNK, (j + 1) * UP_CHUNK)
            vs = slice(D_FF + j * UP_CHUNK, D_FF + (j + 1) * UP_CHUNK)
            gate, val = ahead
            if j + 1 < D_FF // UP_CHUNK:
                ahead = up_proj(j + 1)
            up_ref[:, cs] = gate.astype(BF16)
            up_ref[:, vs] = val.astype(BF16)
            gbuf[8:8 + tr, cs] = gate
            gc = (wf_ref[0:1, cs] * gbuf[6:6 + tr, cs] + wf_ref[1:2, cs] * gbuf[7:7 + tr, cs]
                  + wf_ref[2:3, cs] * gate + fb_ref[:, cs])
            gbuf[0:8, cs] = gbuf[tr:tr + 8, cs]
            gc_ref[:, cs] = gc.astype(BF16)
            act_ref[:, cs] = (gc * _sigmoid(gc) * val).astype(BF16)

        @pl.when(i == max(n - 2, 0))
        def _():
            gather()[1]()

        @pl.when(i == n - 1)
        def _():
            gather()[2]()

    tile = lambda w: pl.BlockSpec((tr, w), lambda i: (i, 0))
    full = lambda a: pl.BlockSpec(a.shape, lambda i: (0,) * a.ndim)
    return pl.pallas_call(
        body, name="ffn_up", grid=(n,),
        in_specs=[tile(D_MODEL), full(g2), ANY, full(wf), full(fb), ANY],
        out_specs=[tile(D_MODEL), tile(2 * D_FF), tile(D_FF), tile(D_FF), ANY],
        out_shape=[
            jax.ShapeDtypeStruct((seq, D_MODEL), BF16), jax.ShapeDtypeStruct((seq, 2 * D_FF), BF16),
            jax.ShapeDtypeStruct((seq, D_FF), BF16), jax.ShapeDtypeStruct((seq, D_FF), BF16),
            jax.ShapeDtypeStruct((D_FF, D_MODEL), BF16),
        ],
        scratch_shapes=[pltpu.VMEM(wup.shape, BF16), pltpu.VMEM((tr + 8, D_FF), F32), pltpu.SemaphoreType.DMA((1,))]
        + _gather_scratch((wdown_b,)),
        compiler_params=pltpu.CompilerParams(dimension_semantics=("arbitrary",), vmem_limit_bytes=VMEM_LIMIT,
                                             collective_id=SIBLING_AND_CHIPS),
    )(x1, g2, wup, wf, fb, wdown_b)


def _ffn_down(x1, act, wdown, g3, target, tile_rows):
    seq = x1.shape[0]
    tr = tile_rows
    n = seq // tr

    def body(x1_ref, act_ref, wdown_hbm, g3_ref, t_ref, dx2b_ref, sm_ref, wdown_v, sem):
        i = pl.program_id(0)

        @pl.when(i == 0)
        def _():
            _load_weights(((wdown_hbm, wdown_v),), sem)
            sm_ref[...] = jnp.zeros(sm_ref.shape, F32)

        x2 = x1_ref[...] + _dot(act_ref[...], wdown_v[...])
        r3 = lax.rsqrt(_rowmean(x2 * x2) + EPS)
        n3 = x2 * r3
        err = n3 * g3_ref[...] - t_ref[...]
        dy = err / D_MODEL
        sm_ref[2:3, :] += _colsum(dy * n3)
        loss = 0.5 * _colsum(_rowmean(err * err))
        sm_ref[3:4, :] += jnp.broadcast_to(loss, (1, D_MODEL))
        dn = dy * g3_ref[...]
        dx2b_ref[...] = (r3 * (dn - n3 * _rowmean(dn * n3))).astype(BF16)

    tile = lambda w: pl.BlockSpec((tr, w), lambda i: (i, 0))
    full = lambda a: pl.BlockSpec(a.shape, lambda i: (0,) * a.ndim)
    return pl.pallas_call(
        body, name="ffn_down", grid=(n,),
        in_specs=[tile(D_MODEL), tile(D_FF), ANY, full(g3), tile(D_MODEL)],
        out_specs=[tile(D_MODEL), pl.BlockSpec((8, D_MODEL), lambda i: (0, 0))],
        out_shape=[jax.ShapeDtypeStruct((seq, D_MODEL), BF16), jax.ShapeDtypeStruct((8, D_MODEL), F32)],
        scratch_shapes=[pltpu.VMEM(wdown.shape, BF16), pltpu.SemaphoreType.DMA((1,))],
        compiler_params=pltpu.CompilerParams(dimension_semantics=("arbitrary",), vmem_limit_bytes=VMEM_LIMIT),
    )(x1, act, wdown, g3, target)


def _ffn_bwd(dx2, up, gcs, x1, g2, wup, wf, wdown, comm, tile_rows):
    seq = x1.shape[0]
    c_ins, c_shapes, c_sems, c_ops, c_id = _comm_plan(comm)
    nc = len(c_ins)
    tr = tile_rows
    n = seq // tr

    def body(dx2_ref, up_ref, gc_ref, x1_ref, g2_ref, wup_hbm, wf_ref, wdown_hbm, *rest):
        c_in, rest = rest[:nc], rest[nc:]
        dup_ref, dx1b_ref, sm_ref, sf_ref = rest[:4]
        c_out, rest = rest[4:4 + nc], rest[4 + nc:]
        wup_v, wdown_v, dbuf, dcar, sem = rest[:5]
        c_sem_refs = rest[5:]
        i = pl.program_id(0)

        @pl.when(i == 0)
        def _():
            c_ops(c_in, c_out, c_sem_refs)[0]()
            _load_weights(((wup_hbm, wup_v), (wdown_hbm, wdown_v)), sem)
            dcar[...] = jnp.zeros(dcar.shape, F32)
            sm_ref[...] = jnp.zeros(sm_ref.shape, F32)
            sf_ref[...] = jnp.zeros(sf_ref.shape, F32)

        dx2b = dx2_ref[...]
        dx2v = dx2b.astype(F32)
        dh2 = jnp.zeros((tr, D_MODEL), F32)

        def down_t(j):
            return _dot_nt(dx2b, wdown_v[j * FF_CHUNK:(j + 1) * FF_CHUNK, :])

        ahead = down_t(0)
        for j in range(N_FF_CHUNKS):
            cs = slice(j * FF_CHUNK, (j + 1) * FF_CHUNK)
            vs = slice(D_FF + j * FF_CHUNK, D_FF + (j + 1) * FF_CHUNK)
            dact = ahead
            if j + 1 < N_FF_CHUNKS:
                ahead = down_t(j + 1)
            gate = up_ref[:, cs].astype(F32)
            val = up_ref[:, vs].astype(F32)
            gc = gc_ref[:, cs].astype(F32)
            sg = _sigmoid(gc)
            dval = dact * (gc * sg)
            dgc = dact * val * (sg * (1.0 + gc * (1.0 - sg)))
            dbuf[0:tr, :] = dgc
            dbuf[tr:tr + 8, :] = dcar[:, cs]
            d_p1 = dbuf[1:1 + tr, :]
            d_p2 = dbuf[2:2 + tr, :]
            dgate = wf_ref[2:3, cs] * dgc + wf_ref[1:2, cs] * d_p1 + wf_ref[0:1, cs] * d_p2
            dcar[:, cs] = dgc[0:8, :]
            sf_ref[0:1, cs] += _colsum(d_p2 * gate)
            sf_ref[1:2, cs] += _colsum(d_p1 * gate)
            sf_ref[2:3, cs] += _colsum(dgc * gate)
            sf_ref[3:4, cs] += _colsum(dgc)
            dgb, dvb = dgate.astype(BF16), dval.astype(BF16)
            dup_ref[:, cs] = dgb
            dup_ref[:, vs] = dvb
            dh2 = dh2 + _dot_nt(dgb, wup_v[:, cs]) + _dot_nt(dvb, wup_v[:, vs])
        x1v = x1_ref[...]
        r2 = lax.rsqrt(_rowmean(x1v * x1v) + EPS)
        n2 = x1v * r2
        sm_ref[1:2, :] += _colsum(dh2 * n2)
        dn2 = dh2 * g2_ref[...]
        dx1b_ref[...] = (dx2v + r2 * (dn2 - n2 * _rowmean(dn2 * n2))).astype(BF16)

        @pl.when(i == n - 1)
        def _():
            c_ops(c_in, c_out, c_sem_refs)[2]()

    tile = lambda w: pl.BlockSpec((tr, w), lambda i: (n - 1 - i, 0))
    full = lambda a: pl.BlockSpec(a.shape, lambda i: (0,) * a.ndim)
    acc = lambda rows, w: pl.BlockSpec((rows, w), lambda i: (0, 0))
    return pl.pallas_call(
        body, name="ffn_bwd", grid=(n,),
        in_specs=[tile(D_MODEL), tile(2 * D_FF), tile(D_FF), tile(D_MODEL), full(g2), ANY, full(wf), ANY] + [ANY] * nc,
        out_specs=[tile(2 * D_FF), tile(D_MODEL), acc(8, D_MODEL), acc(8, D_FF)] + [ANY] * nc,
        out_shape=[
            jax.ShapeDtypeStruct((seq, 2 * D_FF), BF16), jax.ShapeDtypeStruct((seq, D_MODEL), BF16),
            jax.ShapeDtypeStruct((8, D_MODEL), F32), jax.ShapeDtypeStruct((8, D_FF), F32),
        ] + c_shapes,
        scratch_shapes=[
            pltpu.VMEM(wup.shape, BF16), pltpu.VMEM(wdown.shape, BF16),
            pltpu.VMEM((tr + 8, FF_CHUNK), F32), pltpu.VMEM((8, D_FF), F32), pltpu.SemaphoreType.DMA((2,)),
        ] + c_sems,
        compiler_params=pltpu.CompilerParams(dimension_semantics=("arbitrary",), vmem_limit_bytes=VMEM_LIMIT,
                                             collective_id=c_id),
    )(dx2, up, gcs, x1, g2, wup, wf, wdown, *c_ins)


def _mixer_bwd(dx1, x, proj, cpre, d, g1, win, wa, lg, lb, pw, ps, wout, parts, tile_rows):
    seq = x.shape[0]
    n_parts = len(parts)
    tr = tile_rows
    n = seq // tr
    row_cb, row_lg, row_lb, row_ps = 32, 33, 34, 35

    def body(dx1_ref, x_ref, proj_ref, projh_ref, c_ref, d_ref, g1_ref, win_hbm, wa_ref, lg_ref, lb_ref, pw_ref, ps_ref,
             wout_hbm, *rest):
        part_refs, rest = rest[:n_parts], rest[n_parts:]
        dproj_ref, gx_ref, sm_ref, s5_ref, sp_ref = rest[:5]
        land_refs, rest = rest[5:5 + n_parts], rest[5 + n_parts:]
        win_v, wout_v, ubuf, ushift, dcbuf, dshift, ebuf, sem = rest[:8]
        ssems = rest[8:]
        i = pl.program_id(0)
        tile = n - 1 - i

        def scatter():
            return _scatter_ops(part_refs, land_refs, n_parts, ssems[:6], ssems[6:])

        @pl.when(i == 0)
        def _():
            _handshake(SIBLING_AND_CHIPS)
            scatter()[0]()
            _load_weights(((win_hbm, win_v), (wout_hbm, wout_v)), sem)
            dcbuf[tr:tr + A_HALO, :] = jnp.zeros((A_HALO, D_CONV), F32)
            ebuf[tr:tr + P_HALO, :] = jnp.zeros((P_HALO, D_POOL), F32)
            sm_ref[...] = jnp.zeros(sm_ref.shape, F32)
            s5_ref[...] = jnp.zeros(s5_ref.shape, F32)
            sp_ref[...] = jnp.zeros(sp_ref.shape, F32)

        dx1b = dx1_ref[...]
        dx1v = dx1b.astype(F32)
        dm = _dot_nt(dx1b, wout_v[...])
        dya, dyb = dm[:, :D_CONV], dm[:, D_CONV:]
        dbis = []
        for g, w in enumerate(POOL_WINDOWS):
            cols = slice(g * POOL_GROUP, (g + 1) * POOL_GROUP)
            dgb = d_ref[:, cols]
            pwb = pw_ref[g].astype(BF16)
            dyg = dyb[:, cols]
            s5_ref[row_ps:row_ps + 1, cols] += _colsum(dyg * _dot(dgb, pwb))
            dqb = (dyg * ps_ref[:, cols]).astype(BF16)
            sp_ref[g] += _dot_tn(dgb, dqb)
            dd = _dot_nt(dqb, pwb)
            e = dd / _pool_count(tile, tr, w)
            ebuf[0:tr, cols] = e
            s = e
            for kk in range(1, w):
                s = s + ebuf[kk:kk + tr, cols]
            dbis.append(s - dd)
        ebuf[tr:tr + P_HALO, :] = ebuf[0:P_HALO, :]
        cv = c_ref[...].astype(F32)
        xc = cv - _rowmean(cv)
        rs = lax.rsqrt(_rowmean(xc * xc) + EPS)
        z = xc * rs
        ln = z * lg_ref[...] + lb_ref[...]
        sl = _sigmoid(ln)
        dl = dya * (sl * (1.0 + ln * (1.0 - sl)))
        s5_ref[row_lg:row_lg + 1, :] += _colsum(dl * z)
        s5_ref[row_lb:row_lb + 1, :] += _colsum(dl)
        dz = dl * lg_ref[...]
        dc = rs * (dz - _rowmean(dz) - z * _rowmean(dz * z))
        s5_ref[row_cb:row_cb + 1, :] += _colsum(dc)
        dcbuf[0:tr, :] = dc
        keep = (tile > 0).astype(F32)
        avh = projh_ref[:, :D_CONV].astype(F32)
        agh = projh_ref[:, D_CONV:].astype(F32)
        ubuf[0:A_HALO, :] = avh * _sigmoid(agh) * keep
        av = proj_ref[:, :D_CONV].astype(F32)
        ag = proj_ref[:, D_CONV:2 * D_CONV].astype(F32)
        sg = _sigmoid(ag)
        ubuf[A_HALO:A_HALO + tr, :] = av * sg
        off = A_HALO - (CONV_A - 1)
        du = wa_ref[CONV_A - 1:CONV_A, :] * dc
        dview = _shifted_views(dcbuf, dshift, tr)
        uview = _shifted_views(ubuf, ushift, tr)
        for j in range(CONV_A - 1):
            du = du + wa_ref[j:j + 1, :] * dview(CONV_A - 1 - j)
        for j in range(CONV_A):
            s5_ref[j:j + 1, :] += _colsum(dc * uview(off + j))
        dcbuf[tr:tr + A_HALO, :] = dcbuf[0:A_HALO, :]
        dav = du * sg
        dag = du * av * (sg * (1.0 - sg))
        dprojb = jnp.concatenate([dav, dag] + dbis, axis=1).astype(BF16)
        dproj_ref[...] = dprojb
        dh1 = _dot_nt(dprojb, win_v[...])
        xv = x_ref[...]
        r1 = lax.rsqrt(_rowmean(xv * xv) + EPS)
        n1 = xv * r1
        sm_ref[0:1, :] += _colsum(dh1 * n1)
        dn1 = dh1 * g1_ref[...]
        gx_ref[...] = dx1v + r1 * (dn1 - n1 * _rowmean(dn1 * n1))

        @pl.when(i == max(n - 2, 0))
        def _():
            scatter()[1]()

        @pl.when(i == n - 1)
        def _():
            scatter()[2]()

    tile = lambda w: pl.BlockSpec((tr, w), lambda i: (n - 1 - i, 0))
    full = lambda a: pl.BlockSpec(a.shape, lambda i: (0,) * a.ndim)
    halo = pl.BlockSpec((A_HALO, 2 * D_CONV), lambda i: (jnp.maximum((n - 1 - i) * (tr // A_HALO) - 1, 0), 0))
    acc = lambda shape: pl.BlockSpec(shape, lambda i: (0,) * len(shape))
    return pl.pallas_call(
        body, name="mixer_bwd", grid=(n,),
        in_specs=[tile(D_MODEL), tile(D_MODEL), tile(D_IN), halo, tile(D_CONV), tile(D_POOL), full(g1), ANY, full(wa),
                  full(lg), full(lb), full(pw), full(ps), ANY] + [ANY] * n_parts,
        out_specs=[tile(D_IN), tile(D_MODEL), acc((8, D_MODEL)), acc((40, D_CONV)), acc(pw.shape)] + [ANY] * n_parts,
        out_shape=[
            jax.ShapeDtypeStruct((seq, D_IN), BF16), jax.ShapeDtypeStruct((seq, D_MODEL), F32),
            jax.ShapeDtypeStruct((8, D_MODEL), F32), jax.ShapeDtypeStruct((40, D_CONV), F32),
            jax.ShapeDtypeStruct(pw.shape, F32),
        ] + _scatter_shapes(parts, ()),
        scratch_shapes=[
            pltpu.VMEM(win.shape, BF16), pltpu.VMEM(wout.shape, BF16),
            pltpu.VMEM((tr + A_HALO, D_CONV), F32), pltpu.VMEM((7, tr + A_HALO - 8, D_CONV), F32),
            pltpu.VMEM((tr + A_HALO, D_CONV), F32), pltpu.VMEM((7, tr + A_HALO - 8, D_CONV), F32),
            pltpu.VMEM((tr + P_HALO, D_POOL), F32), pltpu.SemaphoreType.DMA((2,)),
        ] + _scatter_scratch(parts, ()),
        compiler_params=pltpu.CompilerParams(dimension_semantics=("arbitrary",), vmem_limit_bytes=VMEM_LIMIT,
                                             collective_id=SIBLING_AND_CHIPS),
    )(dx1, x, proj, proj, cpre, d, g1, win, wa, lg, lb, pw, ps, wout, *parts)


def _weight_grad(a, b, layout, k_rows, comm=None, carry=None):
    seq, m_dim = a.shape
    n_dim = b.shape[1]
    steps = seq // k_rows

    def store(o_ref, acc, index, value):
        if steps == 1:
            o_ref[index] = value.astype(BF16)
            return
        s = pl.program_id(1)

        @pl.when(s == 0)
        def _():
            acc[index] = value

        @pl.when(jnp.logical_and(s > 0, s < steps - 1))
        def _():
            acc[index] += value

        @pl.when(s == steps - 1)
        def _():
            o_ref[index] = (acc[index] + value).astype(BF16)

    if layout in ("rows1", "rows2"):
        groups = int(layout[-1])
        per_tile = N_CHIPS // groups
        rows = m_dim // N_CHIPS // 2
        a_w = m_dim // groups

        def body(a_ref, b_ref, o_ref, acc):
            r = _dot_tn(a_ref[...], b_ref[...])
            for p in range(per_tile):
                for h in range(2):
                    store(o_ref, acc, (h, p), r[(2 * p + h) * rows:(2 * p + h + 1) * rows, :])

        in_specs = [pl.BlockSpec((k_rows, a_w), lambda g, s: (s, g)), pl.BlockSpec((k_rows, n_dim), lambda g, s: (s, 0))]
        out_spec = pl.BlockSpec((2, per_tile, rows, n_dim), lambda g, s: (0, g, 0, 0))
        out_dims, acc_dims = (2, N_CHIPS, rows, n_dim), (2, per_tile, rows, n_dim)
    elif layout == "cols_chip":
        groups = N_CHIPS
        rows, cols = m_dim // 2, n_dim // N_CHIPS

        def body(a_ref, b_ref, o_ref, acc):
            r = _dot_tn(a_ref[...], b_ref[...])
            for h in range(2):
                store(o_ref, acc, h, r[h * rows:(h + 1) * rows, :])

        in_specs = [pl.BlockSpec((k_rows, m_dim), lambda g, s: (s, 0)), pl.BlockSpec((k_rows, cols), lambda g, s: (s, g))]
        out_spec = pl.BlockSpec((2, None, rows, cols), lambda g, s: (0, g, 0, 0))
        out_dims, acc_dims = (2, N_CHIPS, rows, cols), (2, rows, cols)
    else:
        groups = 2
        rows, cols = m_dim // 2, n_dim // N_CHIPS

        def body(a_ref, b_ref, o_ref, acc):
            r = _dot_tn(a_ref[...], b_ref[...])
            for k in range(N_CHIPS):
                store(o_ref, acc, k, r[:, k * cols:(k + 1) * cols])

        in_specs = [pl.BlockSpec((k_rows, rows), lambda g, s: (s, g)), pl.BlockSpec((k_rows, n_dim), lambda g, s: (s, 0))]
        out_spec = pl.BlockSpec((None, N_CHIPS, rows, cols), lambda g, s: (g, 0, 0, 0))
        out_dims, acc_dims = (2, N_CHIPS, rows, cols), (N_CHIPS, rows, cols)

    c_ins, c_shapes, c_sems, c_ops, c_id = _comm_plan(comm)
    nc = len(c_ins)
    c_specs = [ANY] * nc
    if carry is not None:
        assert comm is None and carry.shape[0] % (groups * steps) == 0
        carry_spec = pl.BlockSpec((carry.shape[0] // (groups * steps), carry.shape[1]), lambda g, s: (g * steps + s, 0))
        c_ins, c_shapes, c_specs, nc = (carry,), [jax.ShapeDtypeStruct(carry.shape, carry.dtype)], [carry_spec], 1

    def hosted(a_ref, b_ref, *rest):
        c_in, o_ref, c_out, acc, sems = rest[:nc], rest[nc], rest[nc + 1:2 * nc + 1], rest[2 * nc + 1], rest[2 * nc + 2:]
        g, s = pl.program_id(0), pl.program_id(1)
        if carry is not None:
            c_out[0][...] = c_in[0][...]
            body(a_ref, b_ref, o_ref, acc)
            return
        if nc:
            @pl.when(jnp.logical_and(g == 0, s == 0))
            def _():
                c_ops(c_in, c_out, sems)[0]()

        body(a_ref, b_ref, o_ref, acc)
        if nc:
            step = g * steps + s

            @pl.when(step == max(groups * steps - 2, 0))
            def _():
                c_ops(c_in, c_out, sems)[1]()

            @pl.when(step == groups * steps - 1)
            def _():
                c_ops(c_in, c_out, sems)[2]()

    outs = pl.pallas_call(
        hosted, name=f"weight_grad_{layout}_{m_dim}x{n_dim}", grid=(groups, steps),
        in_specs=in_specs + c_specs, out_specs=[out_spec] + c_specs,
        out_shape=[jax.ShapeDtypeStruct(out_dims, BF16)] + c_shapes,
        scratch_shapes=[pltpu.VMEM(acc_dims, F32)] + c_sems,
        compiler_params=pltpu.CompilerParams(dimension_semantics=("arbitrary", "arbitrary"), vmem_limit_bytes=VMEM_LIMIT,
                                             collective_id=c_id),
    )(a, b, *c_ins)
    return outs if nc else outs[0]


def _exchange_ops(ins, outs, n_big, sems):
    send, recv = sems
    x, y, c, _, _ = _place()
    cps = [pltpu.make_async_remote_copy(
        src_ref=ins[t].at[1 - c] if t < n_big else ins[t], dst_ref=outs[t], send_sem=send.at[t], recv_sem=recv.at[t],
        device_id=(x, y, 1 - c), device_id_type=MESH) for t in range(len(ins))]

    def start():
        for cp in cps:
            cp.start()

    def finish():
        for cp in cps:
            cp.wait()

    return start, finish


def _exchange_shapes(bigs, smalls):
    return [jax.ShapeDtypeStruct((N_CHIPS,) + b.shape[2:], b.dtype) for b in bigs] + [
        jax.ShapeDtypeStruct(s.shape, s.dtype) for s in smalls]


def _comm_plan(comm):
    if comm is None:
        return (), [], [], None, None
    kind, arrays = comm
    n = len(arrays)

    def scatter(i, o, sm):
        start, land, finish = _scatter_ops(i, o, n, sm[:6], sm[6:])
        return lambda: (_handshake(SIBLING_AND_CHIPS), start()), land, finish

    def exchange(i, o, sm):
        start, finish = _exchange_ops(i, o, n, sm)
        return lambda: (_handshake(SIBLING_ONLY), start()), lambda: None, finish

    if kind == "scatter":
        return tuple(arrays), _scatter_shapes(arrays, ()), _scatter_scratch(arrays, ()), scatter, SIBLING_AND_CHIPS
    return tuple(arrays), _exchange_shapes(arrays, ()), [pltpu.SemaphoreType.DMA((n,))] * 2, exchange, SIBLING_ONLY


def _sibling_exchange(bigs, smalls, tag):
    nb, nt = len(bigs), len(bigs) + len(smalls)

    def body(*refs):
        start, finish = _exchange_ops(refs[:nt], refs[nt:2 * nt], nb, refs[2 * nt:])
        _handshake(SIBLING_ONLY)
        start()
        finish()

    return pl.pallas_call(
        body, name=f"sibling_exchange_{tag}", out_shape=_exchange_shapes(bigs, smalls),
        in_specs=[ANY] * nt, out_specs=[ANY] * nt,
        scratch_shapes=[pltpu.SemaphoreType.DMA((nt,)), pltpu.SemaphoreType.DMA((nt,))],
        compiler_params=pltpu.CompilerParams(collective_id=SIBLING_ONLY),
    )(*bigs, *smalls)


def _pair_sum(core, mine, theirs, tag, block_rows):
    _, _, rows, cols = mine.shape
    steps = rows // block_rows

    def body(core_ref, a_ref, b_ref, o_ref):
        o_ref[...] = (a_ref[...].astype(F32) + b_ref[...].astype(F32)).astype(BF16)

    grid_spec = pltpu.PrefetchScalarGridSpec(
        num_scalar_prefetch=1, grid=(N_CHIPS, steps),
        in_specs=[pl.BlockSpec((None, None, block_rows, cols), lambda k, r, core_ref: (core_ref[0], k, r, 0)),
                  pl.BlockSpec((None, block_rows, cols), lambda k, r, core_ref: (k, r, 0))],
        out_specs=pl.BlockSpec((None, block_rows, cols), lambda k, r, core_ref: (k, r, 0)),
    )
    return pl.pallas_call(
        body, name=f"pair_sum_{tag}", grid_spec=grid_spec,
        out_shape=jax.ShapeDtypeStruct((N_CHIPS, rows, cols), BF16),
        compiler_params=pltpu.CompilerParams(dimension_semantics=("arbitrary", "arbitrary"), vmem_limit_bytes=VMEM_LIMIT),
    )(core, mine, theirs)


def _pair_sum_small(mine, theirs):
    (m_f2, m_b1, m_b2, m_sf, m_s5, m_sp) = mine

    def body(a0, a1, a2, a3, a4, a5, b0, b1, b2, b3, b4, b5, o_m, o_f, o_5, o_p):
        sm = (a0[...] + a1[...] + a2[...]) + (b0[...] + b1[...] + b2[...])
        sf = a3[...] + b3[...]
        s5 = a4[...] + b4[...]
        for h in range(2):
            o_m[h] = sm[:, h * (D_MODEL // 2):(h + 1) * (D_MODEL // 2)]
            o_f[h] = sf[:, h * (D_FF // 2):(h + 1) * (D_FF // 2)]
            o_5[h] = s5[:, h * (D_CONV // 2):(h + 1) * (D_CONV // 2)]
            for g in range(2):
                o_p[h, g] = a5[2 * h + g] + b5[2 * h + g]

    out_shape = [
        jax.ShapeDtypeStruct((2, 8, D_MODEL // 2), F32), jax.ShapeDtypeStruct((2, 8, D_FF // 2), F32),
        jax.ShapeDtypeStruct((2, 40, D_CONV // 2), F32), jax.ShapeDtypeStruct((2, 2, POOL_GROUP, POOL_GROUP), F32),
    ]
    return pl.pallas_call(body, name="pair_sum_small", out_shape=out_shape, in_specs=[VMEM] * 12, out_specs=[VMEM] * 4)(
        *mine, *theirs)


def _scatter_ops(ins, outs, n_parts, sems, stages, landed=False):
    ici_send, ici_recv, fwd_send, fwd_recv, loc_in, loc_out = sems
    nt = len(ins)
    x, y, c, k, chips = _place()

    def src_of(t, kk):
        return ins[t].at[kk] if t < n_parts else ins[t].at[c]

    def ici(t, j, kk, slot):
        return pltpu.make_async_remote_copy(
            src_ref=src_of(t, kk), dst_ref=outs[t].at[c, slot], send_sem=ici_send.at[t * 3 + j],
            recv_sem=ici_recv.at[t * 3 + j], device_id=(*chips[j], c), device_id_type=MESH)

    def fwd(t, half):
        slots = outs[t].at[half]
        return pltpu.make_async_remote_copy(
            src_ref=slots, dst_ref=slots, send_sem=fwd_send.at[t], recv_sem=fwd_recv.at[t],
            device_id=(x, y, 1 - c), device_id_type=MESH)

    local = [_staged(src_of(t, k), outs[t].at[c, k], stages[t], loc_in.at[t], loc_out.at[t]) for t in range(nt)]
    peers = [(t, j, 2 * qx + qy) for t in range(nt) for j, (qx, qy) in enumerate(chips)]
    sends = [] if landed else [ici(t, j, kq, k) for t, j, kq in peers]

    def start():
        for cp in local:
            cp[0]()
        for cp in sends:
            cp.start()

    def land():
        for cp in local:
            cp[1]()
        if not landed:
            for t, j, kq in peers:
                ici(t, j, kq, kq).wait_recv()
        for cp in local:
            cp[2]()
        for t in range(nt):
            fwd(t, c).start()

    def finish():
        for t in range(nt):
            fwd(t, 1 - c).wait_recv()
            fwd(t, c).wait_send()
        for cp in sends:
            cp.wait_send()

    return start, land, finish


def _scatter_scratch(parts, smalls):
    arrays = tuple(parts) + tuple(smalls)
    nt = len(arrays)
    return ([pltpu.SemaphoreType.DMA((3 * nt,))] * 2 + [pltpu.SemaphoreType.DMA((nt,))] * 4
            + [pltpu.VMEM(a.shape[1:], a.dtype) for a in arrays])


def _scatter_shapes(parts, smalls):
    return [jax.ShapeDtypeStruct((2, N_CHIPS) + p.shape[1:], p.dtype) for p in tuple(parts) + tuple(smalls)]


HBM_SPEC = pl.BlockSpec(memory_space=pltpu.HBM)
SEM_SPEC = pl.BlockSpec(memory_space=pltpu.SEMAPHORE)
EFFECT = pltpu.SideEffectType.DATAFLOW_SIDE_EFFECTING


def _ici_copy(ins, lands, n_parts, send, recv, t, j):
    _, _, c, k, chips = _place()
    qx, qy = chips[j]
    src = ins[t].at[2 * qx + qy] if t < n_parts else ins[t].at[c]
    return pltpu.make_async_remote_copy(
        src_ref=src, dst_ref=lands[t].at[c, k], send_sem=send.at[t * 3 + j], recv_sem=recv.at[t * 3 + j],
        device_id=(qx, qy, c), device_id_type=MESH)


def _scatter_start(parts, smalls):
    arrays = tuple(parts) + tuple(smalls)
    nt = len(arrays)

    def body(*refs):
        ins, lands = refs[:nt], refs[nt:2 * nt]
        send, recv = refs[2 * nt], refs[2 * nt + 1]
        token = refs[-1]
        for t in range(nt):
            for j in range(3):
                _ici_copy(ins, lands, len(parts), send, recv, t, j).start()
        token[...] = jnp.zeros(token.shape, F32)

    land_shapes = _scatter_shapes(parts, smalls)
    out_shape = ([pltpu.SemaphoreType.DMA((3 * nt,))] * 2 + [pltpu.HBM(a.shape, a.dtype) for a in arrays]
                 + [pltpu.HBM(a.shape, a.dtype) for a in land_shapes] + [jax.ShapeDtypeStruct((8, 128), F32)])
    operands = [pltpu.with_memory_space_constraint(a, pltpu.HBM) for a in arrays]
    operands += [pltpu.with_memory_space_constraint(lax.empty(a.shape, a.dtype), pltpu.HBM) for a in land_shapes]
    outs = pl.pallas_call(
        body, name="scatter_start", out_shape=out_shape, in_specs=[HBM_SPEC] * (2 * nt),
        out_specs=[SEM_SPEC] * 2 + [HBM_SPEC] * (2 * nt) + [VMEM],
        input_output_aliases={i: 2 + i for i in range(2 * nt)},
        compiler_params=pltpu.CompilerParams(has_side_effects=EFFECT),
    )(*operands)
    return outs[0], outs[1], outs[2:2 + nt], outs[2 + nt:2 + 2 * nt], outs[-1]


def _scatter_wait(send, recv, ins, lands, n_parts, after):
    nt = len(ins)

    def body(*refs):
        in_refs, land_refs = refs[:nt], refs[nt:2 * nt]
        send_ref, recv_ref = refs[2 * nt], refs[2 * nt + 1]
        for t in range(nt):
            for j in range(3):
                cp = _ici_copy(in_refs, land_refs, n_parts, send_ref, recv_ref, t, j)
                cp.wait_send()
                cp.wait_recv()

    outs = pl.pallas_call(
        body, name="scatter_wait", out_shape=[pltpu.HBM(a.shape, a.dtype) for a in tuple(ins) + tuple(lands)],
        in_specs=[HBM_SPEC] * (2 * nt) + [SEM_SPEC] * 2 + [ANY] * len(after), out_specs=[HBM_SPEC] * (2 * nt),
        input_output_aliases={i: i for i in range(2 * nt)},
        compiler_params=pltpu.CompilerParams(has_side_effects=EFFECT),
    )(*ins, *lands, send, recv, *after)
    return outs[:nt], outs[nt:]


def _scatter_forward(ins, lands, n_parts):
    nt = len(ins)

    def body(*refs):
        start, land, finish = _scatter_ops(
            refs[:nt], refs[2 * nt:3 * nt], n_parts, refs[3 * nt:3 * nt + 6], refs[3 * nt + 6:], landed=True)
        _handshake(SIBLING_ONLY)
        start()
        land()
        finish()

    return pl.pallas_call(
        body, name="scatter_forward", out_shape=[jax.ShapeDtypeStruct(a.shape, a.dtype) for a in lands],
        in_specs=[ANY] * (2 * nt), out_specs=[ANY] * nt, input_output_aliases={nt + i: i for i in range(nt)},
        scratch_shapes=_scatter_scratch(ins[:n_parts], ins[n_parts:]),
        compiler_params=pltpu.CompilerParams(collective_id=SIBLING_ONLY),
    )(*ins, *lands)


def _chip_scatter(parts, smalls):
    nt = len(parts) + len(smalls)

    def body(*refs):
        start, land, finish = _scatter_ops(refs[:nt], refs[nt:2 * nt], len(parts), refs[2 * nt:2 * nt + 6], refs[2 * nt + 6:])
        _handshake(SIBLING_AND_CHIPS)
        start()
        land()
        finish()

    return pl.pallas_call(
        body, name="chip_scatter", out_shape=_scatter_shapes(parts, smalls), in_specs=[ANY] * nt, out_specs=[ANY] * nt,
        scratch_shapes=_scatter_scratch(parts, smalls),
        compiler_params=pltpu.CompilerParams(collective_id=SIBLING_AND_CHIPS),
    )(*parts, *smalls)


def _adamw(w, g, m, v):
    m = ADAM_B1 * m + (1.0 - ADAM_B1) * g
    v = ADAM_B2 * v + (1.0 - ADAM_B2) * (g * g)
    m_hat = m / (1.0 - ADAM_B1 ** ADAM_STEP)
    v_hat = v / (1.0 - ADAM_B2 ** ADAM_STEP)
    delta = -ADAM_LR * (m_hat / (jnp.sqrt(v_hat) + ADAM_EPS) + ADAM_WD * w)
    return delta, m, v


def _adam_big(parts, w, m, v, tag, block_rows, token):
    _, _, rows, cols = parts.shape
    steps = rows // block_rows

    def body(p_ref, w_ref, m_ref, v_ref, token_ref, g_out, d_out, m_out, v_out):
        g = p_ref[0].astype(F32)
        for q in range(1, N_CHIPS):
            g = g + p_ref[q].astype(F32)
        delta, m_new, v_new = _adamw(w_ref[...], g, m_ref[...], v_ref[...])
        g_out[...] = g
        d_out[...] = delta
        m_out[...] = m_new
        v_out[...] = v_new

    blk = pl.BlockSpec((block_rows, cols), lambda h, r: (h * steps + r, 0))
    return pl.pallas_call(
        body, name=f"adam_{tag}", grid=(2, steps),
        in_specs=[pl.BlockSpec((None, N_CHIPS, block_rows, cols), lambda h, r: (h, 0, r, 0)), blk, blk, blk, ANY],
        out_specs=[blk] * 4, out_shape=[jax.ShapeDtypeStruct(w.shape, F32)] * 4,
        compiler_params=pltpu.CompilerParams(dimension_semantics=("arbitrary", "arbitrary"), vmem_limit_bytes=VMEM_LIMIT),
    )(parts, w, m, v, token)


def _reduce_small(l_m, l_f, l_5, l_p):
    def total(ref):
        t = ref[:, 0]
        for q in range(1, N_CHIPS):
            t = t + ref[:, q]
        return t

    def body(m_ref, f_ref, s_ref, p_ref, g1_o, g2_o, g3_o, loss_o, wf_o, fb_o, wa_o, cb_o, lg_o, lb_o, ps_o, pw_o):
        tm, tf, t5, tp = total(m_ref), total(f_ref), total(s_ref), total(p_ref)
        sm = jnp.concatenate([tm[0], tm[1]], axis=1)
        sf = jnp.concatenate([tf[0], tf[1]], axis=1)
        s5 = jnp.concatenate([t5[0], t5[1]], axis=1)
        g1_o[...] = sm[0:1]
        g2_o[...] = sm[1:2]
        g3_o[...] = sm[2:3]
        loss_o[...] = sm[3:4, 0:128]
        wf_o[...] = sf
        fb_o[...] = sf[3:4]
        wa_o[...] = s5[0:32]
        cb_o[...] = s5[32:33]
        lg_o[...] = s5[33:34]
        lb_o[...] = s5[34:35]
        ps_o[...] = s5[35:36]
        for h in range(2):
            for g in range(2):
                pw_o[2 * h + g] = tp[h, g]

    row = lambda w: jax.ShapeDtypeStruct((1, w), F32)
    out_shape = [row(D_MODEL), row(D_MODEL), row(D_MODEL), row(128), jax.ShapeDtypeStruct((8, D_FF), F32), row(D_FF),
                 jax.ShapeDtypeStruct((32, D_CONV), F32), row(D_CONV), row(D_CONV), row(D_CONV), row(D_POOL),
                 jax.ShapeDtypeStruct((4, POOL_GROUP, POOL_GROUP), F32)]
    return pl.pallas_call(body, name="reduce_small", out_shape=out_shape, in_specs=[VMEM] * 4, out_specs=[VMEM] * 12)(
        l_m, l_f, l_5, l_p)


def _adam_small(ws, gs, ms, vs, by_row):
    count = len(ws)

    def body(*refs):
        w_r, g_r, m_r, v_r = (refs[t * count:(t + 1) * count] for t in range(4))
        outs = [refs[(4 + t) * count:(5 + t) * count] for t in range(4)]
        for t in range(count):
            g = g_r[t][...]
            values = (g,) + _adamw(w_r[t][...], g, m_r[t][...], v_r[t][...])
            for o, value in zip(outs, values):
                if t in by_row:
                    for r in range(value.shape[0]):
                        o[t][r] = value[r:r + 1, :]
                else:
                    o[t][...] = value

    shape = lambda t, w: (w.shape[0], 1, w.shape[1]) if t in by_row else w.shape
    out_shape = [jax.ShapeDtypeStruct(shape(t, w), F32) for t, w in enumerate(ws)] * 4
    outs = pl.pallas_call(body, name="adam_small", out_shape=out_shape, in_specs=[VMEM] * (4 * count),
                          out_specs=[VMEM] * (4 * count))(*ws, *gs, *ms, *vs)
    return [outs[t * count:(t + 1) * count] for t in range(4)]


MIX_TILE = 512
UP_TILE = 512
DOWN_TILE = 1024
FFN_TILE = 256
GRAD_K = 2048


def kernel(x, norm_mix_g, w_in, conv_a_w, conv_a_b, ln_a_g, ln_a_b, pool_w, pool_scale, w_out, norm_ffn_g, w_up, conv_f_w, conv_f_b, w_down, norm_final_g, loss_target, m_norm_mix_g, m_w_in, m_conv_a_w, m_conv_a_b, m_ln_a_g, m_ln_a_b, m_pool_w, m_pool_scale, m_w_out, m_norm_ffn_g, m_w_up, m_conv_f_w, m_conv_f_b, m_w_down, m_norm_final_g, v_norm_mix_g, v_w_in, v_conv_a_w, v_conv_a_b, v_ln_a_g, v_ln_a_b, v_pool_w, v_pool_scale, v_w_out, v_norm_ffn_g, v_w_up, v_conv_f_w, v_conv_f_b, v_w_down, v_norm_final_g):
    seq = x.shape[1]
    xs, ts = x[0], loss_target[0]
    mix_tile, ffn_tile, grad_k = min(MIX_TILE, seq), min(FFN_TILE, seq), min(GRAD_K, seq)
    chip = 2 * lax.axis_index("x") + lax.axis_index("y")
    core = lax.axis_index("c").astype(jnp.int32).reshape(1)

    wa_s = jnp.pad(conv_a_w[0], ((0, 32 - CONV_A), (0, 0)))
    wf_s = jnp.pad(conv_f_w[0], ((0, 8 - CONV_F), (0, 0)))
    win_b, wout_b, wup_b, wdown_b = _cast_shards(w_in[0], w_out[0], w_up[0], w_down[0])
    g3 = norm_final_g.reshape(1, D_MODEL)
    pw = pool_w[0]

    h1, proj, cpre, dpool, mcat, x1, win, wout, wup, wa_g, wf_g = _mixer_fwd(
        xs, norm_mix_g, win_b, wout_b, wup_b, wa_s, wf_s, conv_a_b, ln_a_g, ln_a_b, pw, pool_scale, mix_tile)
    wa = jnp.transpose(wa_g, (1, 0, 2)).reshape(32, D_CONV)
    wf = jnp.transpose(wf_g, (1, 0, 2)).reshape(8, D_FF)
    h2, up, gcs, act, wdown = _ffn_up(x1, norm_ffn_g, wup, wf, conv_f_b, wdown_b, min(UP_TILE, seq))
    dx2b, sm_f2 = _ffn_down(x1, act, wdown, g3, ts, min(DOWN_TILE, seq))
    tags = ("w_in", "w_out", "w_up", "w_down")
    blocks = (256, 128, 256, 176)
    g_wdown = _weight_grad(act, dx2b, "rows2", grad_k)
    dup, dx1b, sm_b1, sf, l_wdown = _ffn_bwd(
        dx2b, up, gcs, x1, norm_ffn_g, wup, wf, wdown, ("exchange", [g_wdown]), ffn_tile)
    p_wdown = _pair_sum(core, g_wdown, l_wdown, tags[3], g_wdown.shape[2])
    g_wup, s_wdown = _weight_grad(h2, dup, "cols_chip", grad_k, ("scatter", [p_wdown]))
    g_wout, l_wup = _weight_grad(mcat, dx1b, "rows1", grad_k, ("exchange", [g_wup]))
    p_wup = _pair_sum(core, g_wup, l_wup, tags[2], g_wup.shape[2])
    l_wout, = _sibling_exchange((g_wout,), (), "early")
    p_wout = _pair_sum(core, g_wout, l_wout, tags[1], g_wout.shape[2])
    dproj, gx, sm_b2, s5, sp, s_wout, s_wup = _mixer_bwd(
        dx1b, xs, proj, cpre, dpool, norm_mix_g, win, wa, ln_a_g, ln_a_b, pw, pool_scale, wout, [p_wout, p_wup], mix_tile)
    g_win, grad_x = _weight_grad(h1, dproj, "cols_half", grad_k, carry=gx)

    smalls = (sm_f2, sm_b1, sm_b2, sf, s5, sp)
    landed = _sibling_exchange((g_win,), smalls, "late")
    part_win = _pair_sum(core, g_win, landed[0], tags[0], g_win.shape[2])
    small_parts = _pair_sum_small(smalls, landed[1:])
    send, recv, late_src, late_land, token = _scatter_start([part_win], small_parts)
    big_w = (w_in[0], w_out[0], w_up[0], w_down[0])
    big_m = (m_w_in[0], m_w_out[0], m_w_up[0], m_w_down[0])
    big_v = (v_w_in[0], v_w_out[0], v_w_up[0], v_w_down[0])
    big = {}
    for t, p in ((1, s_wout), (2, s_wup), (3, s_wdown)):
        big[tags[t]] = _adam_big(p, big_w[t], big_m[t], big_v[t], tags[t], blocks[t], token)
    late_src, late_land = _scatter_wait(send, recv, late_src, late_land, 1, [big[tags[t]][3] for t in (1, 2, 3)])
    late = _scatter_forward(late_src, late_land, 1)
    big[tags[0]] = _adam_big(late[0], big_w[0], big_m[0], big_v[0], tags[0], blocks[0], token)
    big = {tag: [a[None] for a in outs] for tag, outs in big.items()}
    scattered = [None] * 4 + list(late[1:])

    (g_g1, g_g2, g_g3, loss_row, g_wf_all, g_fb, g_wa_all, g_cb, g_lg, g_lb, g_ps, g_pw) = _reduce_small(*scattered[4:])
    g_wa = lax.dynamic_slice(g_wa_all, (0, chip * (D_CONV // N_CHIPS)), (32, D_CONV // N_CHIPS))[:CONV_A]
    g_wf = lax.dynamic_slice(g_wf_all, (0, chip * (D_FF // N_CHIPS)), (8, D_FF // N_CHIPS))[:CONV_F]
    small_names = ("norm_mix_g", "conv_a_w", "conv_a_b", "ln_a_g", "ln_a_b", "pool_w", "pool_scale", "norm_ffn_g",
                   "conv_f_w", "conv_f_b", "norm_final_g")
    small_w = (norm_mix_g, conv_a_w[0], conv_a_b, ln_a_g, ln_a_b, pw, pool_scale, norm_ffn_g, conv_f_w[0], conv_f_b, g3)
    small_m = (m_norm_mix_g, m_conv_a_w[0], m_conv_a_b, m_ln_a_g, m_ln_a_b, m_pool_w[0], m_pool_scale, m_norm_ffn_g,
               m_conv_f_w[0], m_conv_f_b, m_norm_final_g.reshape(1, D_MODEL))
    small_v = (v_norm_mix_g, v_conv_a_w[0], v_conv_a_b, v_ln_a_g, v_ln_a_b, v_pool_w[0], v_pool_scale, v_norm_ffn_g,
               v_conv_f_w[0], v_conv_f_b, v_norm_final_g.reshape(1, D_MODEL))
    small_g = (g_g1, g_wa, g_cb, g_lg, g_lb, g_pw, g_ps, g_g2, g_wf, g_fb, g_g3)
    by_row = (small_names.index("conv_a_w"), small_names.index("conv_f_w"))
    s_g, s_delta, s_m, s_v = _adam_small(small_w, small_g, small_m, small_v, by_row)
    shapes = {"pool_w": pool_w.shape, "norm_final_g": norm_final_g.shape}
    small = {}
    for t, name in enumerate(small_names):
        shp = shapes.get(name)
        if t in by_row:
            small[name] = [jnp.transpose(a, (1, 0, 2)) for a in (s_g[t], s_delta[t], s_m[t], s_v[t])]
        else:
            small[name] = [a if shp is None else a.reshape(shp) for a in (s_g[t], s_delta[t], s_m[t], s_v[t])]

    order = ("norm_mix_g", "w_in", "conv_a_w", "conv_a_b", "ln_a_g", "ln_a_b", "pool_w", "pool_scale", "w_out", "norm_ffn_g",
             "w_up", "conv_f_w", "conv_f_b", "w_down", "norm_final_g")
    table = {**big, **small}
    loss = loss_row[0, 0]
    outs = [loss, grad_x[None]]
    for t in range(4):
        outs += [table[name][t] for name in order]
    return tuple(outs)
```

```python
import functools

import jax
import jax.numpy as jnp
from jax import lax
from jax.experimental import pallas as pl
from jax.experimental.pallas import tpu as pltpu

F32 = jnp.float32
BF16 = jnp.bfloat16
EPS = 1e-6
ADAM_LR = 0.001
ADAM_B1 = 0.9
ADAM_B2 = 0.999
ADAM_EPS = 1e-08
ADAM_WD = 0.01
ADAM_STEP = 10

D_MODEL = 1024
D_CONV = 512
D_POOL = 512
D_IN = 1536
D_FF = 2816
CONV_A = 31
CONV_F = 3
POOL_WINDOWS = (2, 4, 8, 16)
POOL_GROUP = 128
N_CHIPS = 4
FF_CHUNK = 256
N_FF_CHUNKS = D_FF // FF_CHUNK
UP_CHUNK = 2816
A_HALO = 32
P_HALO = 16
VMEM_LIMIT = 56 * 1024 * 1024
MESH = pl.DeviceIdType.MESH

ANY = pl.BlockSpec(memory_space=pl.ANY)
VMEM = pl.BlockSpec(memory_space=pltpu.VMEM)


def _dot(a, b):
    return jnp.dot(a, b, preferred_element_type=F32)


def _dot_nt(a, b):
    return lax.dot_general(a, b, (((1,), (1,)), ((), ())), preferred_element_type=F32)


def _dot_tn(a, b):
    return lax.dot_general(a, b, (((0,), (0,)), ((), ())), preferred_element_type=F32)


def _sigmoid(v):
    return jax.nn.sigmoid(v)


def _colsum(v):
    return jnp.sum(v, axis=0, keepdims=True)


def _rowmean(v):
    return jnp.mean(v, axis=-1, keepdims=True)


def _place():
    x, y, c = lax.axis_index("x"), lax.axis_index("y"), lax.axis_index("c")
    chips = [(1 - x, y), (x, 1 - y), (1 - x, 1 - y)]
    return x, y, c, 2 * x + y, chips


SIBLING_ONLY, SIBLING_AND_CHIPS = 0, 1


def _handshake(collective):
    x, y, c, _, chips = _place()
    peers = [(x, y, 1 - c)] + ([(*chip, c) for chip in chips] if collective == SIBLING_AND_CHIPS else [])
    barrier = pltpu.get_barrier_semaphore()
    for peer in peers:
        pl.semaphore_signal(barrier, inc=1, device_id=peer, device_id_type=MESH)
    pl.semaphore_wait(barrier, len(peers))


def _staged(src, dst, stage, sem_in, sem_out):
    hop_in = pltpu.make_async_copy(src, stage, sem_in)
    hop_out = pltpu.make_async_copy(stage, dst, sem_out)

    def relay():
        hop_in.wait()
        hop_out.start()

    return hop_in.start, relay, hop_out.wait


def _gather_ops(bufs, fulls, col_sharded, sems, stages):
    ici_send, ici_recv, fwd_send, fwd_recv, loc_in, loc_out = sems
    n_big = len(bufs)
    x, y, c, k, chips = _place()

    def block(i, kk, half=None):
        rows, cols = bufs[i].shape
        if col_sharded[i]:
            rs = slice(None) if half is None else pl.ds(pl.multiple_of(half * (rows // 2), 16), rows // 2)
            return fulls[i].at[rs, pl.ds(pl.multiple_of(kk * cols, 128), cols)]
        if half is None:
            return fulls[i].at[pl.ds(pl.multiple_of(kk * rows, 16), rows), :]
        return fulls[i].at[pl.ds(pl.multiple_of(kk * rows + half * (rows // 2), 16), rows // 2), :]

    def my_half(i):
        rows = bufs[i].shape[0]
        return bufs[i].at[pl.ds(pl.multiple_of(c * (rows // 2), 16), rows // 2), :]

    def ici(i, j, kk):
        return pltpu.make_async_remote_copy(
            src_ref=my_half(i), dst_ref=block(i, kk, c), send_sem=ici_send.at[i * 3 + j], recv_sem=ici_recv.at[i * 3 + j],
            device_id=(*chips[j], c), device_id_type=MESH)

    def fwd(i, j, kk, half):
        return pltpu.make_async_remote_copy(
            src_ref=block(i, kk, half), dst_ref=block(i, kk, half), send_sem=fwd_send.at[i * 3 + j],
            recv_sem=fwd_recv.at[i * 3 + j], device_id=(x, y, 1 - c), device_id_type=MESH)

    local = [_staged(bufs[i], block(i, k), stages[i], loc_in.at[i], loc_out.at[i]) for i in range(n_big)]
    sends = [ici(i, j, k) for i in range(n_big) for j in range(3)]
    peers = [(i, j, 2 * qx + qy) for i in range(n_big) for j, (qx, qy) in enumerate(chips)]

    def start():
        for cp in local:
            cp[0]()
        for cp in sends:
            cp.start()

    def land():
        for cp in local:
            cp[1]()
        for i, j, kq in peers:
            ici(i, j, kq).wait_recv()
            fwd(i, j, kq, c).start()

    def finish():
        for i, j, kq in peers:
            fwd(i, j, kq, 1 - c).wait_recv()
            fwd(i, j, kq, c).wait_send()
        for cp in sends:
            cp.wait_send()
        for cp in local:
            cp[2]()

    return start, land, finish


def _gather_scratch(shards):
    n_big = len(shards)
    return ([pltpu.SemaphoreType.DMA((3 * n_big,))] * 4 + [pltpu.SemaphoreType.DMA((n_big,))] * 2
            + [pltpu.VMEM(b.shape, b.dtype) for b in shards])


def _tap_ops(srcs, dsts, sems):
    send, recv, loc = sems
    _, _, c, k, chips = _place()

    def copy(t, j, kk):
        return pltpu.make_async_remote_copy(
            src_ref=srcs[t], dst_ref=dsts[t].at[kk], send_sem=send.at[t * 3 + j], recv_sem=recv.at[t * 3 + j],
            device_id=(*chips[j], c), device_id_type=MESH)

    local = [pltpu.make_async_copy(srcs[t], dsts[t].at[k], loc.at[t]) for t in range(len(srcs))]
    sends = [[copy(t, j, k) for j in range(3)] for t in range(len(srcs))]

    def start():
        for t, cp in enumerate(local):
            cp.start()
            for sd in sends[t]:
                sd.start()

    def wait(t):
        for j, (qx, qy) in enumerate(chips):
            copy(t, j, 2 * qx + qy).wait_recv()
        for sd in sends[t]:
            sd.wait_send()
        local[t].wait()

    return start, wait


def _cast_shards(*shards):
    def body(*refs):
        for src, dst in zip(refs[:len(shards)], refs[len(shards):]):
            dst[...] = src[...].astype(BF16)

    return pl.pallas_call(
        body, name="cast_shards", out_shape=[jax.ShapeDtypeStruct(s.shape, BF16) for s in shards],
        in_specs=[VMEM] * len(shards), out_specs=[VMEM] * len(shards),
        compiler_params=pltpu.CompilerParams(vmem_limit_bytes=VMEM_LIMIT),
    )(*shards)


def _load_weights(pairs, sem, first=0):
    cps = [pltpu.make_async_copy(src, dst, sem.at[first + i]) for i, (src, dst) in enumerate(pairs)]
    for cp in cps:
        cp.start()
    for cp in cps:
        cp.wait()


def _shifted_views(buf, shifted, t_rows):
    n = t_rows + A_HALO - 8
    for b in range(1, 8):
        shifted[b - 1] = buf[b:b + n, :]

    def view(offset):
        a, b = divmod(offset, 8)
        if b == 0:
            return buf[8 * a:8 * a + t_rows, :]
        return shifted[b - 1, 8 * a:8 * a + t_rows, :]

    return view


def _pool_count(tile, t_rows, w):
    row = lax.broadcasted_iota(jnp.int32, (t_rows, POOL_GROUP), 0) + tile * t_rows
    return jnp.minimum(row + 1, w).astype(F32)


def _mixer_fwd(x, g1, win_b, wout_b, wup_b, wa_s, wf_s, cb, lg, lb, pw, ps, tile_rows):
    seq = x.shape[0]
    tr = tile_rows
    n = seq // tr

    def body(x_ref, g1_ref, win_b_hbm, wout_b_hbm, wup_b_hbm, wa_s_hbm, wf_s_hbm, cb_ref, lg_ref, lb_ref, pw_ref,
             ps_ref, h1_ref, proj_ref, c_ref, d_ref, m_ref, x1_ref, win_f, wout_f, wup_f, wa_g, wf_g,
             win_v, wout_v, wa_ref, ubuf, ushift, bbuf, sem, *csems):
        i = pl.program_id(0)
        first_sems, first_stages, second_sems, second_stages, later_sems, later_stages, tap_sems = (
            csems[0:6], csems[6:7], csems[7:13], csems[13:14], csems[14:20], csems[20:21], csems[21:24])

        def first():
            return _gather_ops((win_b_hbm,), (win_f,), (True,), first_sems, first_stages)

        def second():
            return _gather_ops((wout_b_hbm,), (wout_f,), (False,), second_sems, second_stages)

        def later():
            return _gather_ops((wup_b_hbm,), (wup_f,), (True,), later_sems, later_stages)

        def taps():
            return _tap_ops((wa_s_hbm, wf_s_hbm), (wa_g, wf_g), tap_sems)

        @pl.when(i == 0)
        def _():
            _handshake(SIBLING_AND_CHIPS)
            first()[0]()
            taps()[0]()
            second()[0]()
            later()[0]()
            first()[1]()
            first()[2]()
            _load_weights([(win_f, win_v)], sem)
            ubuf[0:A_HALO, :] = jnp.zeros((A_HALO, D_CONV), F32)
            bbuf[0:P_HALO, :] = jnp.zeros((P_HALO, D_POOL), F32)

        xv = x_ref[...]
        r = lax.rsqrt(_rowmean(xv * xv) + EPS)
        h1 = (xv * r * g1_ref[...]).astype(BF16)
        h1_ref[...] = h1
        proj = _dot(h1, win_v[...])
        proj_ref[...] = proj.astype(BF16)

        @pl.when(i == 0)
        def _():
            taps()[1](0)
            _load_weights([(wa_g.at[kk], wa_ref.at[:, kk * (D_CONV // N_CHIPS):(kk + 1) * (D_CONV // N_CHIPS)])
                           for kk in range(N_CHIPS)], sem, 2)

        av, ag, bi = proj[:, :D_CONV], proj[:, D_CONV:2 * D_CONV], proj[:, 2 * D_CONV:]
        ubuf[A_HALO:A_HALO + tr, :] = av * _sigmoid(ag)
        off = A_HALO - (CONV_A - 1)
        uview = _shifted_views(ubuf, ushift, tr)
        acc = wa_ref[0:1, :] * uview(off)
        for j in range(1, CONV_A):
            acc = acc + wa_ref[j:j + 1, :] * uview(off + j)
        cv = acc + cb_ref[...]
        ubuf[0:A_HALO, :] = ubuf[tr:tr + A_HALO, :]
        c_ref[...] = cv.astype(BF16)
        xc = cv - _rowmean(cv)
        z = xc * lax.rsqrt(_rowmean(xc * xc) + EPS)
        ln = z * lg_ref[...] + lb_ref[...]
        ya = ln * _sigmoid(ln)
        bbuf[P_HALO:P_HALO + tr, :] = bi
        ds, ybs = [], []
        for g, w in enumerate(POOL_WINDOWS):
            cols = slice(g * POOL_GROUP, (g + 1) * POOL_GROUP)
            s = bi[:, cols]
            for kk in range(1, w):
                s = s + bbuf[P_HALO - kk:P_HALO - kk + tr, cols]
            dg = s / _pool_count(i, tr, w) - bi[:, cols]
            ds.append(dg)
            ybs.append(_dot(dg.astype(BF16), pw_ref[g].astype(BF16)))
        bbuf[0:P_HALO, :] = bbuf[tr:tr + P_HALO, :]
        d_ref[...] = jnp.concatenate(ds, axis=1).astype(BF16)
        yb = jnp.concatenate(ybs, axis=1) * ps_ref[...]
        m = jnp.concatenate([ya, yb], axis=1).astype(BF16)
        m_ref[...] = m

        @pl.when(i == 0)
        def _():
            second()[1]()
            second()[2]()
            _load_weights([(wout_f, wout_v)], sem, 1)

        x1_ref[...] = xv + _dot(m, wout_v[...])

        @pl.when(i == n - 1)
        def _():
            later()[1]()
            later()[2]()
            taps()[1](1)

    tile = lambda w: pl.BlockSpec((tr, w), lambda i: (i, 0))
    full = lambda a: pl.BlockSpec(a.shape, lambda i: (0,) * a.ndim)
    return pl.pallas_call(
        body, name="mixer_fwd", grid=(n,),
        in_specs=[tile(D_MODEL), full(g1)] + [ANY] * 5 + [full(cb), full(lg), full(lb), full(pw), full(ps)],
        out_specs=[tile(D_MODEL), tile(D_IN), tile(D_CONV), tile(D_POOL), tile(D_MODEL), tile(D_MODEL)] + [ANY] * 5,
        out_shape=[
            jax.ShapeDtypeStruct((seq, D_MODEL), BF16), jax.ShapeDtypeStruct((seq, D_IN), BF16),
            jax.ShapeDtypeStruct((seq, D_CONV), BF16), jax.ShapeDtypeStruct((seq, D_POOL), BF16),
            jax.ShapeDtypeStruct((seq, D_MODEL), BF16), jax.ShapeDtypeStruct((seq, D_MODEL), F32),
            jax.ShapeDtypeStruct((D_MODEL, D_IN), BF16), jax.ShapeDtypeStruct((D_MODEL, D_MODEL), BF16),
            jax.ShapeDtypeStruct((D_MODEL, 2 * D_FF), BF16),
            jax.ShapeDtypeStruct((N_CHIPS,) + wa_s.shape, F32), jax.ShapeDtypeStruct((N_CHIPS,) + wf_s.shape, F32),
        ],
        scratch_shapes=[
            pltpu.VMEM((D_MODEL, D_IN), BF16), pltpu.VMEM((D_MODEL, D_MODEL), BF16), pltpu.VMEM((32, D_CONV), F32),
            pltpu.VMEM((tr + A_HALO, D_CONV), F32), pltpu.VMEM((7, tr + A_HALO - 8, D_CONV), F32),
            pltpu.VMEM((tr + P_HALO, D_POOL), F32), pltpu.SemaphoreType.DMA((2 + N_CHIPS,)),
        ] + _gather_scratch((win_b,)) + _gather_scratch((wout_b,)) + _gather_scratch((wup_b,)) + [
            pltpu.SemaphoreType.DMA((6,)), pltpu.SemaphoreType.DMA((6,)), pltpu.SemaphoreType.DMA((2,))],
        compiler_params=pltpu.CompilerParams(dimension_semantics=("arbitrary",), vmem_limit_bytes=VMEM_LIMIT,
                                             collective_id=SIBLING_AND_CHIPS),
    )(x, g1, win_b, wout_b, wup_b, wa_s, wf_s, cb, lg, lb, pw, ps)


def _ffn_up(x1, g2, wup, wf, fb, wdown_b, tile_rows):
    seq = x1.shape[0]
    tr = tile_rows
    n = seq // tr

    def body(x1_ref, g2_ref, wup_hbm, wf_ref, fb_ref, wdown_b_hbm,
             h2_ref, up_ref, gc_ref, act_ref, wdown_f, wup_v, gbuf, sem, *gsems):
        i = pl.program_id(0)

        def gather():
            return _gather_ops((wdown_b_hbm,), (wdown_f,), (False,), gsems[:6], gsems[6:])

        @pl.when(i == 0)
        def _():
            _handshake(SIBLING_AND_CHIPS)
            gather()[0]()
            _load_weights(((wup_hbm, wup_v),), sem)
            gbuf[0:8, :] = jnp.zeros((8, D_FF), F32)

        x1v = x1_ref[...]
        r2 = lax.rsqrt(_rowmean(x1v * x1v) + EPS)
        h2 = (x1v * r2 * g2_ref[...]).astype(BF16)
        h2_ref[...] = h2

        def up_proj(j):
            return (_dot(h2, wup_v[:, j * UP_CHUNK:(j + 1) * UP_CHUNK]),
                    _dot(h2, wup_v[:, D_FF + j * UP_CHUNK:D_FF + (j + 1) * UP_CHUNK]))

        ahead = up_proj(0)
        for j in range(D_FF // UP_CHUNK):
            cs = slice(j * UP_CHUNK, (j + 1) * UP_CHUNK)
            vs = slice(D_FF + j * UP_CHUNK, D_FF + (j + 1) * UP_CHUNK)
            gate, val = ahead
            if j + 1 < D_FF // UP_CHUNK:
                ahead = up_proj(j + 1)
            up_ref[:, cs] = gate.astype(BF16)
            up_ref[:, vs] = val.astype(BF16)
            gbuf[8:8 + tr, cs] = gate
            gc = (wf_ref[0:1, cs] * gbuf[6:6 + tr, cs] + wf_ref[1:2, cs] * gbuf[7:7 + tr, cs]
                  + wf_ref[2:3, cs] * gate + fb_ref[:, cs])
            gbuf[0:8, cs] = gbuf[tr:tr + 8, cs]
            gc_ref[:, cs] = gc.astype(BF16)
            act_ref[:, cs] = (gc * _sigmoid(gc) * val).astype(BF16)

        @pl.when(i == max(n - 2, 0))
        def _():
            gather()[1]()

        @pl.when(i == n - 1)
        def _():
            gather()[2]()

    tile = lambda w: pl.BlockSpec((tr, w), lambda i: (i, 0))
    full = lambda a: pl.BlockSpec(a.shape, lambda i: (0,) * a.ndim)
    return pl.pallas_call(
        body, name="ffn_up", grid=(n,),
        in_specs=[tile(D_MODEL), full(g2), ANY, full(wf), full(fb), ANY],
        out_specs=[tile(D_MODEL), tile(2 * D_FF), tile(D_FF), tile(D_FF), ANY],
        out_shape=[
            jax.ShapeDtypeStruct((seq, D_MODEL), BF16), jax.ShapeDtypeStruct((seq, 2 * D_FF), BF16),
            jax.ShapeDtypeStruct((seq, D_FF), BF16), jax.ShapeDtypeStruct((seq, D_FF), BF16),
            jax.ShapeDtypeStruct((D_FF, D_MODEL), BF16),
        ],
        scratch_shapes=[pltpu.VMEM(wup.shape, BF16), pltpu.VMEM((tr + 8, D_FF), F32), pltpu.SemaphoreType.DMA((1,))]
        + _gather_scratch((wdown_b,)),
        compiler_params=pltpu.CompilerParams(dimension_semantics=("arbitrary",), vmem_limit_bytes=VMEM_LIMIT,
                                             collective_id=SIBLING_AND_CHIPS),
    )(x1, g2, wup, wf, fb, wdown_b)


def _ffn_down(x1, act, wdown, g3, target, tile_rows):
    seq = x1.shape[0]
    tr = tile_rows
    n = seq // tr

    def body(x1_ref, act_ref, wdown_hbm, g3_ref, t_ref, dx2b_ref, sm_ref, wdown_v, sem):
        i = pl.program_id(0)

        @pl.when(i == 0)
        def _():
            _load_weights(((wdown_hbm, wdown_v),), sem)
            sm_ref[...] = jnp.zeros(sm_ref.shape, F32)

        x2 = x1_ref[...] + _dot(act_ref[...], wdown_v[...])
        r3 = lax.rsqrt(_rowmean(x2 * x2) + EPS)
        n3 = x2 * r3
        err = n3 * g3_ref[...] - t_ref[...]
        dy = err / D_MODEL
        sm_ref[2:3, :] += _colsum(dy * n3)
        loss = 0.5 * _colsum(_rowmean(err * err))
        sm_ref[3:4, :] += jnp.broadcast_to(loss, (1, D_MODEL))
        dn = dy * g3_ref[...]
        dx2b_ref[...] = (r3 * (dn - n3 * _rowmean(dn * n3))).astype(BF16)

    tile = lambda w: pl.BlockSpec((tr, w), lambda i: (i, 0))
    full = lambda a: pl.BlockSpec(a.shape, lambda i: (0,) * a.ndim)
    return pl.pallas_call(
        body, name="ffn_down", grid=(n,),
        in_specs=[tile(D_MODEL), tile(D_FF), ANY, full(g3), tile(D_MODEL)],
        out_specs=[tile(D_MODEL), pl.BlockSpec((8, D_MODEL), lambda i: (0, 0))],
        out_shape=[jax.ShapeDtypeStruct((seq, D_MODEL), BF16), jax.ShapeDtypeStruct((8, D_MODEL), F32)],
        scratch_shapes=[pltpu.VMEM(wdown.shape, BF16), pltpu.SemaphoreType.DMA((1,))],
        compiler_params=pltpu.CompilerParams(dimension_semantics=("arbitrary",), vmem_limit_bytes=VMEM_LIMIT),
    )(x1, act, wdown, g3, target)


def _ffn_bwd(dx2, up, gcs, x1, g2, wup, wf, wdown, comm, tile_rows):
    seq = x1.shape[0]
    c_ins, c_shapes, c_sems, c_ops, c_id = _comm_plan(comm)
    nc = len(c_ins)
    tr = tile_rows
    n = seq // tr

    def body(dx2_ref, up_ref, gc_ref, x1_ref, g2_ref, wup_hbm, wf_ref, wdown_hbm, *rest):
        c_in, rest = rest[:nc], rest[nc:]
        dup_ref, dx1b_ref, sm_ref, sf_ref = rest[:4]
        c_out, rest = rest[4:4 + nc], rest[4 + nc:]
        wup_v, wdown_v, dbuf, dcar, sem = rest[:5]
        c_sem_refs = rest[5:]
        i = pl.program_id(0)

        @pl.when(i == 0)
        def _():
            c_ops(c_in, c_out, c_sem_refs)[0]()
            _load_weights(((wup_hbm, wup_v), (wdown_hbm, wdown_v)), sem)
            dcar[...] = jnp.zeros(dcar.shape, F32)
            sm_ref[...] = jnp.zeros(sm_ref.shape, F32)
            sf_ref[...] = jnp.zeros(sf_ref.shape, F32)

        dx2b = dx2_ref[...]
        dx2v = dx2b.astype(F32)
        dh2 = jnp.zeros((tr, D_MODEL), F32)

        def down_t(j):
            return _dot_nt(dx2b, wdown_v[j * FF_CHUNK:(j + 1) * FF_CHUNK, :])

        ahead = down_t(0)
        for j in range(N_FF_CHUNKS):
            cs = slice(j * FF_CHUNK, (j + 1) * FF_CHUNK)
            vs = slice(D_FF + j * FF_CHUNK, D_FF + (j + 1) * FF_CHUNK)
            dact = ahead
            if j + 1 < N_FF_CHUNKS:
                ahead = down_t(j + 1)
            gate = up_ref[:, cs].astype(F32)
            val = up_ref[:, vs].astype(F32)
            gc = gc_ref[:, cs].astype(F32)
            sg = _sigmoid(gc)
            dval = dact * (gc * sg)
            dgc = dact * val * (sg * (1.0 + gc * (1.0 - sg)))
            dbuf[0:tr, :] = dgc
            dbuf[tr:tr + 8, :] = dcar[:, cs]
            d_p1 = dbuf[1:1 + tr, :]
            d_p2 = dbuf[2:2 + tr, :]
            dgate = wf_ref[2:3, cs] * dgc + wf_ref[1:2, cs] * d_p1 + wf_ref[0:1, cs] * d_p2
            dcar[:, cs] = dgc[0:8, :]
            sf_ref[0:1, cs] += _colsum(d_p2 * gate)
            sf_ref[1:2, cs] += _colsum(d_p1 * gate)
            sf_ref[2:3, cs] += _colsum(dgc * gate)
            sf_ref[3:4, cs] += _colsum(dgc)
            dgb, dvb = dgate.astype(BF16), dval.astype(BF16)
            dup_ref[:, cs] = dgb
            dup_ref[:, vs] = dvb
            dh2 = dh2 + _dot_nt(dgb, wup_v[:, cs]) + _dot_nt(dvb, wup_v[:, vs])
        x1v = x1_ref[...]
        r2 = lax.rsqrt(_rowmean(x1v * x1v) + EPS)
        n2 = x1v * r2
        sm_ref[1:2, :] += _colsum(dh2 * n2)
        dn2 = dh2 * g2_ref[...]
        dx1b_ref[...] = (dx2v + r2 * (dn2 - n2 * _rowmean(dn2 * n2))).astype(BF16)

        @pl.when(i == n - 1)
        def _():
            c_ops(c_in, c_out, c_sem_refs)[2]()

    tile = lambda w: pl.BlockSpec((tr, w), lambda i: (n - 1 - i, 0))
    full = lambda a: pl.BlockSpec(a.shape, lambda i: (0,) * a.ndim)
    acc = lambda rows, w: pl.BlockSpec((rows, w), lambda i: (0, 0))
    return pl.pallas_call(
        body, name="ffn_bwd", grid=(n,),
        in_specs=[tile(D_MODEL), tile(2 * D_FF), tile(D_FF), tile(D_MODEL), full(g2), ANY, full(wf), ANY] + [ANY] * nc,
        out_specs=[tile(2 * D_FF), tile(D_MODEL), acc(8, D_MODEL), acc(8, D_FF)] + [ANY] * nc,
        out_shape=[
            jax.ShapeDtypeStruct((seq, 2 * D_FF), BF16), jax.ShapeDtypeStruct((seq, D_MODEL), BF16),
            jax.ShapeDtypeStruct((8, D_MODEL), F32), jax.ShapeDtypeStruct((8, D_FF), F32),
        ] + c_shapes,
        scratch_shapes=[
            pltpu.VMEM(wup.shape, BF16), pltpu.VMEM(wdown.shape, BF16),
            pltpu.VMEM((tr + 8, FF_CHUNK), F32), pltpu.VMEM((8, D_FF), F32), pltpu.SemaphoreType.DMA((2,)),
        ] + c_sems,
        compiler_params=pltpu.CompilerParams(dimension_semantics=("arbitrary",), vmem_limit_bytes=VMEM_LIMIT,
                                             collective_id=c_id),
    )(dx2, up, gcs, x1, g2, wup, wf, wdown, *c_ins)


def _mixer_bwd(dx1, x, proj, cpre, d, g1, win, wa, lg, lb, pw, ps, wout, parts, tile_rows):
    seq = x.shape[0]
    n_parts = len(parts)
    tr = tile_rows
    n = seq // tr
    row_cb, row_lg, row_lb, row_ps = 32, 33, 34, 35

    def body(dx1_ref, x_ref, proj_ref, projh_ref, c_ref, d_ref, g1_ref, win_hbm, wa_ref, lg_ref, lb_ref, pw_ref, ps_ref,
             wout_hbm, *rest):
        part_refs, rest = rest[:n_parts], rest[n_parts:]
        dproj_ref, gx_ref, sm_ref, s5_ref, sp_ref = rest[:5]
        land_refs, rest = rest[5:5 + n_parts], rest[5 + n_parts:]
        win_v, wout_v, ubuf, ushift, dcbuf, dshift, ebuf, sem = rest[:8]
        ssems = rest[8:]
        i = pl.program_id(0)
        tile = n - 1 - i

        def scatter():
            return _scatter_ops(part_refs, land_refs, n_parts, ssems[:6], ssems[6:])

        @pl.when(i == 0)
        def _():
            _handshake(SIBLING_AND_CHIPS)
            scatter()[0]()
            _load_weights(((win_hbm, win_v), (wout_hbm, wout_v)), sem)
            dcbuf[tr:tr + A_HALO, :] = jnp.zeros((A_HALO, D_CONV), F32)
            ebuf[tr:tr + P_HALO, :] = jnp.zeros((P_HALO, D_POOL), F32)
            sm_ref[...] = jnp.zeros(sm_ref.shape, F32)
            s5_ref[...] = jnp.zeros(s5_ref.shape, F32)
            sp_ref[...] = jnp.zeros(sp_ref.shape, F32)

        dx1b = dx1_ref[...]
        dx1v = dx1b.astype(F32)
        dm = _dot_nt(dx1b, wout_v[...])
        dya, dyb = dm[:, :D_CONV], dm[:, D_CONV:]
        dbis = []
        for g, w in enumerate(POOL_WINDOWS):
            cols = slice(g * POOL_GROUP, (g + 1) * POOL_GROUP)
            dgb = d_ref[:, cols]
            pwb = pw_ref[g].astype(BF16)
            dyg = dyb[:, cols]
            s5_ref[row_ps:row_ps + 1, cols] += _colsum(dyg * _dot(dgb, pwb))
            dqb = (dyg * ps_ref[:, cols]).astype(BF16)
            sp_ref[g] += _dot_tn(dgb, dqb)
            dd = _dot_nt(dqb, pwb)
            e = dd / _pool_count(tile, tr, w)
            ebuf[0:tr, cols] = e
            s = e
            for kk in range(1, w):
                s = s + ebuf[kk:kk + tr, cols]
            dbis.append(s - dd)
        ebuf[tr:tr + P_HALO, :] = ebuf[0:P_HALO, :]
        cv = c_ref[...].astype(F32)
        xc = cv - _rowmean(cv)
        rs = lax.rsqrt(_rowmean(xc * xc) + EPS)
        z = xc * rs
        ln = z * lg_ref[...] + lb_ref[...]
        sl = _sigmoid(ln)
        dl = dya * (sl * (1.0 + ln * (1.0 - sl)))
        s5_ref[row_lg:row_lg + 1, :] += _colsum(dl * z)
        s5_ref[row_lb:row_lb + 1, :] += _colsum(dl)
        dz = dl * lg_ref[...]
        dc = rs * (dz - _rowmean(dz) - z * _rowmean(dz * z))
        s5_ref[row_cb:row_cb + 1, :] += _colsum(dc)
        dcbuf[0:tr, :] = dc
        keep = (tile > 0).astype(F32)
        avh = projh_ref[:, :D_CONV].astype(F32)
        agh = projh_ref[:, D_CONV:].astype(F32)
        ubuf[0:A_HALO, :] = avh * _sigmoid(agh) * keep
        av = proj_ref[:, :D_CONV].astype(F32)
        ag = proj_ref[:, D_CONV:2 * D_CONV].astype(F32)
        sg = _sigmoid(ag)
        ubuf[A_HALO:A_HALO + tr, :] = av * sg
        off = A_HALO - (CONV_A - 1)
        du = wa_ref[CONV_A - 1:CONV_A, :] * dc
        dview = _shifted_views(dcbuf, dshift, tr)
        uview = _shifted_views(ubuf, ushift, tr)
        for j in range(CONV_A - 1):
            du = du + wa_ref[j:j + 1, :] * dview(CONV_A - 1 - j)
        for j in range(CONV_A):
            s5_ref[j:j + 1, :] += _colsum(dc * uview(off + j))
        dcbuf[tr:tr + A_HALO, :] = dcbuf[0:A_HALO, :]
        dav = du * sg
        dag = du * av * (sg * (1.0 - sg))
        dprojb = jnp.concatenate([dav, dag] + dbis, axis=1).astype(BF16)
        dproj_ref[...] = dprojb
        dh1 = _dot_nt(dprojb, win_v[...])
        xv = x_ref[...]
        r1 = lax.rsqrt(_rowmean(xv * xv) + EPS)
        n1 = xv * r1
        sm_ref[0:1, :] += _colsum(dh1 * n1)
        dn1 = dh1 * g1_ref[...]
        gx_ref[...] = dx1v + r1 * (dn1 - n1 * _rowmean(dn1 * n1))

        @pl.when(i == max(n - 2, 0))
        def _():
            scatter()[1]()

        @pl.when(i == n - 1)
        def _():
            scatter()[2]()

    tile = lambda w: pl.BlockSpec((tr, w), lambda i: (n - 1 - i, 0))
    full = lambda a: pl.BlockSpec(a.shape, lambda i: (0,) * a.ndim)
    halo = pl.BlockSpec((A_HALO, 2 * D_CONV), lambda i: (jnp.maximum((n - 1 - i) * (tr // A_HALO) - 1, 0), 0))
    acc = lambda shape: pl.BlockSpec(shape, lambda i: (0,) * len(shape))
    return pl.pallas_call(
        body, name="mixer_bwd", grid=(n,),
        in_specs=[tile(D_MODEL), tile(D_MODEL), tile(D_IN), halo, tile(D_CONV), tile(D_POOL), full(g1), ANY, full(wa),
                  full(lg), full(lb), full(pw), full(ps), ANY] + [ANY] * n_parts,
        out_specs=[tile(D_IN), tile(D_MODEL), acc((8, D_MODEL)), acc((40, D_CONV)), acc(pw.shape)] + [ANY] * n_parts,
        out_shape=[
            jax.ShapeDtypeStruct((seq, D_IN), BF16), jax.ShapeDtypeStruct((seq, D_MODEL), F32),
            jax.ShapeDtypeStruct((8, D_MODEL), F32), jax.ShapeDtypeStruct((40, D_CONV), F32),
            jax.ShapeDtypeStruct(pw.shape, F32),
        ] + _scatter_shapes(parts, ()),
        scratch_shapes=[
            pltpu.VMEM(win.shape, BF16), pltpu.VMEM(wout.shape, BF16),
            pltpu.VMEM((tr + A_HALO, D_CONV), F32), pltpu.VMEM((7, tr + A_HALO - 8, D_CONV), F32),
            pltpu.VMEM((tr + A_HALO, D_CONV), F32), pltpu.VMEM((7, tr + A_HALO - 8, D_CONV), F32),
            pltpu.VMEM((tr + P_HALO, D_POOL), F32), pltpu.SemaphoreType.DMA((2,)),
        ] + _scatter_scratch(parts, ()),
        compiler_params=pltpu.CompilerParams(dimension_semantics=("arbitrary",), vmem_limit_bytes=VMEM_LIMIT,
                                             collective_id=SIBLING_AND_CHIPS),
    )(dx1, x, proj, proj, cpre, d, g1, win, wa, lg, lb, pw, ps, wout, *parts)


def _weight_grad(a, b, layout, k_rows, comm=None, carry=None):
    seq, m_dim = a.shape
    n_dim = b.shape[1]
    steps = seq // k_rows

    def store(o_ref, acc, index, value):
        if steps == 1:
            o_ref[index] = value.astype(BF16)
            return
        s = pl.program_id(1)

        @pl.when(s == 0)
        def _():
            acc[index] = value

        @pl.when(jnp.logical_and(s > 0, s < steps - 1))
        def _():
            acc[index] += value

        @pl.when(s == steps - 1)
        def _():
            o_ref[index] = (acc[index] + value).astype(BF16)

    if layout in ("rows1", "rows2"):
        groups = int(layout[-1])
        per_tile = N_CHIPS // groups
        rows = m_dim // N_CHIPS // 2
        a_w = m_dim // groups

        def body(a_ref, b_ref, o_ref, acc):
            r = _dot_tn(a_ref[...], b_ref[...])
            for p in range(per_tile):
                for h in range(2):
                    store(o_ref, acc, (h, p), r[(2 * p + h) * rows:(2 * p + h + 1) * rows, :])

        in_specs = [pl.BlockSpec((k_rows, a_w), lambda g, s: (s, g)), pl.BlockSpec((k_rows, n_dim), lambda g, s: (s, 0))]
        out_spec = pl.BlockSpec((2, per_tile, rows, n_dim), lambda g, s: (0, g, 0, 0))
        out_dims, acc_dims = (2, N_CHIPS, rows, n_dim), (2, per_tile, rows, n_dim)
    elif layout == "cols_chip":
        groups = N_CHIPS
        rows, cols = m_dim // 2, n_dim // N_CHIPS

        def body(a_ref, b_ref, o_ref, acc):
            r = _dot_tn(a_ref[...], b_ref[...])
            for h in range(2):
                store(o_ref, acc, h, r[h * rows:(h + 1) * rows, :])

        in_specs = [pl.BlockSpec((k_rows, m_dim), lambda g, s: (s, 0)), pl.BlockSpec((k_rows, cols), lambda g, s: (s, g))]
        out_spec = pl.BlockSpec((2, None, rows, cols), lambda g, s: (0, g, 0, 0))
        out_dims, acc_dims = (2, N_CHIPS, rows, cols), (2, rows, cols)
    else:
        groups = 2
        rows, cols = m_dim // 2, n_dim // N_CHIPS

        def body(a_ref, b_ref, o_ref, acc):
            r = _dot_tn(a_ref[...], b_ref[...])
            for k in range(N_CHIPS):
                store(o_ref, acc, k, r[:, k * cols:(k + 1) * cols])

        in_specs = [pl.BlockSpec((k_rows, rows), lambda g, s: (s, g)), pl.BlockSpec((k_rows, n_dim), lambda g, s: (s, 0))]
        out_spec = pl.BlockSpec((None, N_CHIPS, rows, cols), lambda g, s: (g, 0, 0, 0))
        out_dims, acc_dims = (2, N_CHIPS, rows, cols), (N_CHIPS, rows, cols)

    c_ins, c_shapes, c_sems, c_ops, c_id = _comm_plan(comm)
    nc = len(c_ins)
    c_specs = [ANY] * nc
    if carry is not None:
        assert comm is None and carry.shape[0] % (groups * steps) == 0
        carry_spec = pl.BlockSpec((carry.shape[0] // (groups * steps), carry.shape[1]), lambda g, s: (g * steps + s, 0))
        c_ins, c_shapes, c_specs, nc = (carry,), [jax.ShapeDtypeStruct(carry.shape, carry.dtype)], [carry_spec], 1

    def hosted(a_ref, b_ref, *rest):
        c_in, o_ref, c_out, acc, sems = rest[:nc], rest[nc], rest[nc + 1:2 * nc + 1], rest[2 * nc + 1], rest[2 * nc + 2:]
        g, s = pl.program_id(0), pl.program_id(1)
        if carry is not None:
            c_out[0][...] = c_in[0][...]
            body(a_ref, b_ref, o_ref, acc)
            return
        if nc:
            @pl.when(jnp.logical_and(g == 0, s == 0))
            def _():
                c_ops(c_in, c_out, sems)[0]()

        body(a_ref, b_ref, o_ref, acc)
        if nc:
            step = g * steps + s

            @pl.when(step == max(groups * steps - 2, 0))
            def _():
                c_ops(c_in, c_out, sems)[1]()

            @pl.when(step == groups * steps - 1)
            def _():
                c_ops(c_in, c_out, sems)[2]()

    outs = pl.pallas_call(
        hosted, name=f"weight_grad_{layout}_{m_dim}x{n_dim}", grid=(groups, steps),
        in_specs=in_specs + c_specs, out_specs=[out_spec] + c_specs,
        out_shape=[jax.ShapeDtypeStruct(out_dims, BF16)] + c_shapes,
        scratch_shapes=[pltpu.VMEM(acc_dims, F32)] + c_sems,
        compiler_params=pltpu.CompilerParams(dimension_semantics=("arbitrary", "arbitrary"), vmem_limit_bytes=VMEM_LIMIT,
                                             collective_id=c_id),
    )(a, b, *c_ins)
    return outs if nc else outs[0]


def _exchange_ops(ins, outs, n_big, sems):
    send, recv = sems
    x, y, c, _, _ = _place()
    cps = [pltpu.make_async_remote_copy(
        src_ref=ins[t].at[1 - c] if t < n_big else ins[t], dst_ref=outs[t], send_sem=send.at[t], recv_sem=recv.at[t],
        device_id=(x, y, 1 - c), device_id_type=MESH) for t in range(len(ins))]

    def start():
        for cp in cps:
            cp.start()

    def finish():
        for cp in cps:
            cp.wait()

    return start, finish


def _exchange_shapes(bigs, smalls):
    return [jax.ShapeDtypeStruct((N_CHIPS,) + b.shape[2:], b.dtype) for b in bigs] + [
        jax.ShapeDtypeStruct(s.shape, s.dtype) for s in smalls]


def _comm_plan(comm):
    if comm is None:
        return (), [], [], None, None
    kind, arrays = comm
    n = len(arrays)

    def scatter(i, o, sm):
        start, land, finish = _scatter_ops(i, o, n, sm[:6], sm[6:])
        return lambda: (_handshake(SIBLING_AND_CHIPS), start()), land, finish

    def exchange(i, o, sm):
        start, finish = _exchange_ops(i, o, n, sm)
        return lambda: (_handshake(SIBLING_ONLY), start()), lambda: None, finish

    if kind == "scatter":
        return tuple(arrays), _scatter_shapes(arrays, ()), _scatter_scratch(arrays, ()), scatter, SIBLING_AND_CHIPS
    return tuple(arrays), _exchange_shapes(arrays, ()), [pltpu.SemaphoreType.DMA((n,))] * 2, exchange, SIBLING_ONLY


def _sibling_exchange(bigs, smalls, tag):
    nb, nt = len(bigs), len(bigs) + len(smalls)

    def body(*refs):
        start, finish = _exchange_ops(refs[:nt], refs[nt:2 * nt], nb, refs[2 * nt:])
        _handshake(SIBLING_ONLY)
        start()
        finish()

    return pl.pallas_call(
        body, name=f"sibling_exchange_{tag}", out_shape=_exchange_shapes(bigs, smalls),
        in_specs=[ANY] * nt, out_specs=[ANY] * nt,
        scratch_shapes=[pltpu.SemaphoreType.DMA((nt,)), pltpu.SemaphoreType.DMA((nt,))],
        compiler_params=pltpu.CompilerParams(collective_id=SIBLING_ONLY),
    )(*bigs, *smalls)


def _pair_sum(core, mine, theirs, tag, block_rows):
    _, _, rows, cols = mine.shape
    steps = rows // block_rows

    def body(core_ref, a_ref, b_ref, o_ref):
        o_ref[...] = (a_ref[...].astype(F32) + b_ref[...].astype(F32)).astype(BF16)

    grid_spec = pltpu.PrefetchScalarGridSpec(
        num_scalar_prefetch=1, grid=(N_CHIPS, steps),
        in_specs=[pl.BlockSpec((None, None, block_rows, cols), lambda k, r, core_ref: (core_ref[0], k, r, 0)),
                  pl.BlockSpec((None, block_rows, cols), lambda k, r, core_ref: (k, r, 0))],
        out_specs=pl.BlockSpec((None, block_rows, cols), lambda k, r, core_ref: (k, r, 0)),
    )
    return pl.pallas_call(
        body, name=f"pair_sum_{tag}", grid_spec=grid_spec,
        out_shape=jax.ShapeDtypeStruct((N_CHIPS, rows, cols), BF16),
        compiler_params=pltpu.CompilerParams(dimension_semantics=("arbitrary", "arbitrary"), vmem_limit_bytes=VMEM_LIMIT),
    )(core, mine, theirs)


def _pair_sum_small(mine, theirs):
    (m_f2, m_b1, m_b2, m_sf, m_s5, m_sp) = mine

    def body(a0, a1, a2, a3, a4, a5, b0, b1, b2, b3, b4, b5, o_m, o_f, o_5, o_p):
        sm = (a0[...] + a1[...] + a2[...]) + (b0[...] + b1[...] + b2[...])
        sf = a3[...] + b3[...]
        s5 = a4[...] + b4[...]
        for h in range(2):
            o_m[h] = sm[:, h * (D_MODEL // 2):(h + 1) * (D_MODEL // 2)]
            o_f[h] = sf[:, h * (D_FF // 2):(h + 1) * (D_FF // 2)]
            o_5[h] = s5[:, h * (D_CONV // 2):(h + 1) * (D_CONV // 2)]
            for g in range(2):
                o_p[h, g] = a5[2 * h + g] + b5[2 * h + g]

    out_shape = [
        jax.ShapeDtypeStruct((2, 8, D_MODEL // 2), F32), jax.ShapeDtypeStruct((2, 8, D_FF // 2), F32),
        jax.ShapeDtypeStruct((2, 40, D_CONV // 2), F32), jax.ShapeDtypeStruct((2, 2, POOL_GROUP, POOL_GROUP), F32),
    ]
    return pl.pallas_call(body, name="pair_sum_small", out_shape=out_shape, in_specs=[VMEM] * 12, out_specs=[VMEM] * 4)(
        *mine, *theirs)


def _scatter_ops(ins, outs, n_parts, sems, stages, landed=False):
    ici_send, ici_recv, fwd_send, fwd_recv, loc_in, loc_out = sems
    nt = len(ins)
    x, y, c, k, chips = _place()

    def src_of(t, kk):
        return ins[t].at[kk] if t < n_parts else ins[t].at[c]

    def ici(t, j, kk, slot):
        return pltpu.make_async_remote_copy(
            src_ref=src_of(t, kk), dst_ref=outs[t].at[c, slot], send_sem=ici_send.at[t * 3 + j],
            recv_sem=ici_recv.at[t * 3 + j], device_id=(*chips[j], c), device_id_type=MESH)

    def fwd(t, half):
        slots = outs[t].at[half]
        return pltpu.make_async_remote_copy(
            src_ref=slots, dst_ref=slots, send_sem=fwd_send.at[t], recv_sem=fwd_recv.at[t],
            device_id=(x, y, 1 - c), device_id_type=MESH)

    local = [_staged(src_of(t, k), outs[t].at[c, k], stages[t], loc_in.at[t], loc_out.at[t]) for t in range(nt)]
    peers = [(t, j, 2 * qx + qy) for t in range(nt) for j, (qx, qy) in enumerate(chips)]
    sends = [] if landed else [ici(t, j, kq, k) for t, j, kq in peers]

    def start():
        for cp in local:
            cp[0]()
        for cp in sends:
            cp.start()

    def land():
        for cp in local:
            cp[1]()
        if not landed:
            for t, j, kq in peers:
                ici(t, j, kq, kq).wait_recv()
        for cp in local:
            cp[2]()
        for t in range(nt):
            fwd(t, c).start()

    def finish():
        for t in range(nt):
            fwd(t, 1 - c).wait_recv()
            fwd(t, c).wait_send()
        for cp in sends:
            cp.wait_send()

    return start, land, finish


def _scatter_scratch(parts, smalls):
    arrays = tuple(parts) + tuple(smalls)
    nt = len(arrays)
    return ([pltpu.SemaphoreType.DMA((3 * nt,))] * 2 + [pltpu.SemaphoreType.DMA((nt,))] * 4
            + [pltpu.VMEM(a.shape[1:], a.dtype) for a in arrays])


def _scatter_shapes(parts, smalls):
    return [jax.ShapeDtypeStruct((2, N_CHIPS) + p.shape[1:], p.dtype) for p in tuple(parts) + tuple(smalls)]


HBM_SPEC = pl.BlockSpec(memory_space=pltpu.HBM)
SEM_SPEC = pl.BlockSpec(memory_space=pltpu.SEMAPHORE)
EFFECT = pltpu.SideEffectType.DATAFLOW_SIDE_EFFECTING


def _ici_copy(ins, lands, n_parts, send, recv, t, j):
    _, _, c, k, chips = _place()
    qx, qy = chips[j]
    src = ins[t].at[2 * qx + qy] if t < n_parts else ins[t].at[c]
    return pltpu.make_async_remote_copy(
        src_ref=src, dst_ref=lands[t].at[c, k], send_sem=send.at[t * 3 + j], recv_sem=recv.at[t * 3 + j],
        device_id=(qx, qy, c), device_id_type=MESH)


def _scatter_start(parts, smalls):
    arrays = tuple(parts) + tuple(smalls)
    nt = len(arrays)

    def body(*refs):
        ins, lands = refs[:nt], refs[nt:2 * nt]
        send, recv = refs[2 * nt], refs[2 * nt + 1]
        token = refs[-1]
        for t in range(nt):
            for j in range(3):
                _ici_copy(ins, lands, len(parts), send, recv, t, j).start()
        token[...] = jnp.zeros(token.shape, F32)

    land_shapes = _scatter_shapes(parts, smalls)
    out_shape = ([pltpu.SemaphoreType.DMA((3 * nt,))] * 2 + [pltpu.HBM(a.shape, a.dtype) for a in arrays]
                 + [pltpu.HBM(a.shape, a.dtype) for a in land_shapes] + [jax.ShapeDtypeStruct((8, 128), F32)])
    operands = [pltpu.with_memory_space_constraint(a, pltpu.HBM) for a in arrays]
    operands += [pltpu.with_memory_space_constraint(lax.empty(a.shape, a.dtype), pltpu.HBM) for a in land_shapes]
    outs = pl.pallas_call(
        body, name="scatter_start", out_shape=out_shape, in_specs=[HBM_SPEC] * (2 * nt),
        out_specs=[SEM_SPEC] * 2 + [HBM_SPEC] * (2 * nt) + [VMEM],
        input_output_aliases={i: 2 + i for i in range(2 * nt)},
        compiler_params=pltpu.CompilerParams(has_side_effects=EFFECT),
    )(*operands)
    return outs[0], outs[1], outs[2:2 + nt], outs[2 + nt:2 + 2 * nt], outs[-1]


def _scatter_wait(send, recv, ins, lands, n_parts, after):
    nt = len(ins)

    def body(*refs):
        in_refs, land_refs = refs[:nt], refs[nt:2 * nt]
        send_ref, recv_ref = refs[2 * nt], refs[2 * nt + 1]
        for t in range(nt):
            for j in range(3):
                cp = _ici_copy(in_refs, land_refs, n_parts, send_ref, recv_ref, t, j)
                cp.wait_send()
                cp.wait_recv()

    outs = pl.pallas_call(
        body, name="scatter_wait", out_shape=[pltpu.HBM(a.shape, a.dtype) for a in tuple(ins) + tuple(lands)],
        in_specs=[HBM_SPEC] * (2 * nt) + [SEM_SPEC] * 2 + [ANY] * len(after), out_specs=[HBM_SPEC] * (2 * nt),
        input_output_aliases={i: i for i in range(2 * nt)},
        compiler_params=pltpu.CompilerParams(has_side_effects=EFFECT),
    )(*ins, *lands, send, recv, *after)
    return outs[:nt], outs[nt:]


def _scatter_forward(ins, lands, n_parts):
    nt = len(ins)

    def body(*refs):
        start, land, finish = _scatter_ops(
            refs[:nt], refs[2 * nt:3 * nt], n_parts, refs[3 * nt:3 * nt + 6], refs[3 * nt + 6:], landed=True)
        _handshake(SIBLING_ONLY)
        start()
        land()
        finish()

    return pl.pallas_call(
        body, name="scatter_forward", out_shape=[jax.ShapeDtypeStruct(a.shape, a.dtype) for a in lands],
        in_specs=[ANY] * (2 * nt), out_specs=[ANY] * nt, input_output_aliases={nt + i: i for i in range(nt)},
        scratch_shapes=_scatter_scratch(ins[:n_parts], ins[n_parts:]),
        compiler_params=pltpu.CompilerParams(collective_id=SIBLING_ONLY),
    )(*ins, *lands)


def _chip_scatter(parts, smalls):
    nt = len(parts) + len(smalls)

    def body(*refs):
        start, land, finish = _scatter_ops(refs[:nt], refs[nt:2 * nt], len(parts), refs[2 * nt:2 * nt + 6], refs[2 * nt + 6:])
        _handshake(SIBLING_AND_CHIPS)
        start()
        land()
        finish()

    return pl.pallas_call(
        body, name="chip_scatter", out_shape=_scatter_shapes(parts, smalls), in_specs=[ANY] * nt, out_specs=[ANY] * nt,
        scratch_shapes=_scatter_scratch(parts, smalls),
        compiler_params=pltpu.CompilerParams(collective_id=SIBLING_AND_CHIPS),
    )(*parts, *smalls)


def _adamw(w, g, m, v):
    m = ADAM_B1 * m + (1.0 - ADAM_B1) * g
    v = ADAM_B2 * v + (1.0 - ADAM_B2) * (g * g)
    m_hat = m / (1.0 - ADAM_B1 ** ADAM_STEP)
    v_hat = v / (1.0 - ADAM_B2 ** ADAM_STEP)
    delta = -ADAM_LR * (m_hat / (jnp.sqrt(v_hat) + ADAM_EPS) + ADAM_WD * w)
    return delta, m, v


def _adam_big(parts, w, m, v, tag, block_rows, token):
    _, _, rows, cols = parts.shape
    steps = rows // block_rows

    def body(p_ref, w_ref, m_ref, v_ref, token_ref, g_out, d_out, m_out, v_out):
        g = p_ref[0].astype(F32)
        for q in range(1, N_CHIPS):
            g = g + p_ref[q].astype(F32)
        delta, m_new, v_new = _adamw(w_ref[...], g, m_ref[...], v_ref[...])
        g_out[...] = g
        d_out[...] = delta
        m_out[...] = m_new
        v_out[...] = v_new

    blk = pl.BlockSpec((block_rows, cols), lambda h, r: (h * steps + r, 0))
    return pl.pallas_call(
        body, name=f"adam_{tag}", grid=(2, steps),
        in_specs=[pl.BlockSpec((None, N_CHIPS, block_rows, cols), lambda h, r: (h, 0, r, 0)), blk, blk, blk, ANY],
        out_specs=[blk] * 4, out_shape=[jax.ShapeDtypeStruct(w.shape, F32)] * 4,
        compiler_params=pltpu.CompilerParams(dimension_semantics=("arbitrary", "arbitrary"), vmem_limit_bytes=VMEM_LIMIT),
    )(parts, w, m, v, token)


def _reduce_small(l_m, l_f, l_5, l_p):
    def total(ref):
        t = ref[:, 0]
        for q in range(1, N_CHIPS):
            t = t + ref[:, q]
        return t

    def body(m_ref, f_ref, s_ref, p_ref, g1_o, g2_o, g3_o, loss_o, wf_o, fb_o, wa_o, cb_o, lg_o, lb_o, ps_o, pw_o):
        tm, tf, t5, tp = total(m_ref), total(f_ref), total(s_ref), total(p_ref)
        sm = jnp.concatenate([tm[0], tm[1]], axis=1)
        sf = jnp.concatenate([tf[0], tf[1]], axis=1)
        s5 = jnp.concatenate([t5[0], t5[1]], axis=1)
        g1_o[...] = sm[0:1]
        g2_o[...] = sm[1:2]
        g3_o[...] = sm[2:3]
        loss_o[...] = sm[3:4, 0:128]
        wf_o[...] = sf
        fb_o[...] = sf[3:4]
        wa_o[...] = s5[0:32]
        cb_o[...] = s5[32:33]
        lg_o[...] = s5[33:34]
        lb_o[...] = s5[34:35]
        ps_o[...] = s5[35:36]
        for h in range(2):
            for g in range(2):
                pw_o[2 * h + g] = tp[h, g]

    row = lambda w: jax.ShapeDtypeStruct((1, w), F32)
    out_shape = [row(D_MODEL), row(D_MODEL), row(D_MODEL), row(128), jax.ShapeDtypeStruct((8, D_FF), F32), row(D_FF),
                 jax.ShapeDtypeStruct((32, D_CONV), F32), row(D_CONV), row(D_CONV), row(D_CONV), row(D_POOL),
                 jax.ShapeDtypeStruct((4, POOL_GROUP, POOL_GROUP), F32)]
    return pl.pallas_call(body, name="reduce_small", out_shape=out_shape, in_specs=[VMEM] * 4, out_specs=[VMEM] * 12)(
        l_m, l_f, l_5, l_p)


def _adam_small(ws, gs, ms, vs, by_row):
    count = len(ws)

    def body(*refs):
        w_r, g_r, m_r, v_r = (refs[t * count:(t + 1) * count] for t in range(4))
        outs = [refs[(4 + t) * count:(5 + t) * count] for t in range(4)]
        for t in range(count):
            g = g_r[t][...]
            values = (g,) + _adamw(w_r[t][...], g, m_r[t][...], v_r[t][...])
            for o, value in zip(outs, values):
                if t in by_row:
                    for r in range(value.shape[0]):
                        o[t][r] = value[r:r + 1, :]
                else:
                    o[t][...] = value

    shape = lambda t, w: (w.shape[0], 1, w.shape[1]) if t in by_row else w.shape
    out_shape = [jax.ShapeDtypeStruct(shape(t, w), F32) for t, w in enumerate(ws)] * 4
    outs = pl.pallas_call(body, name="adam_small", out_shape=out_shape, in_specs=[VMEM] * (4 * count),
                          out_specs=[VMEM] * (4 * count))(*ws, *gs, *ms, *vs)
    return [outs[t * count:(t + 1) * count] for t in range(4)]


MIX_TILE = 512
UP_TILE = 512
FFN_TILE = 256
GRAD_K = 2048


def kernel(x, norm_mix_g, w_in, conv_a_w, conv_a_b, ln_a_g, ln_a_b, pool_w, pool_scale, w_out, norm_ffn_g, w_up, conv_f_w, conv_f_b, w_down, norm_final_g, loss_target, m_norm_mix_g, m_w_in, m_conv_a_w, m_conv_a_b, m_ln_a_g, m_ln_a_b, m_pool_w, m_pool_scale, m_w_out, m_norm_ffn_g, m_w_up, m_conv_f_w, m_conv_f_b, m_w_down, m_norm_final_g, v_norm_mix_g, v_w_in, v_conv_a_w, v_conv_a_b, v_ln_a_g, v_ln_a_b, v_pool_w, v_pool_scale, v_w_out, v_norm_ffn_g, v_w_up, v_conv_f_w, v_conv_f_b, v_w_down, v_norm_final_g):
    seq = x.shape[1]
    xs, ts = x[0], loss_target[0]
    mix_tile, ffn_tile, grad_k = min(MIX_TILE, seq), min(FFN_TILE, seq), min(GRAD_K, seq)
    chip = 2 * lax.axis_index("x") + lax.axis_index("y")
    core = lax.axis_index("c").astype(jnp.int32).reshape(1)

    wa_s = jnp.pad(conv_a_w[0], ((0, 32 - CONV_A), (0, 0)))
    wf_s = jnp.pad(conv_f_w[0], ((0, 8 - CONV_F), (0, 0)))
    win_b, wout_b, wup_b, wdown_b = _cast_shards(w_in[0], w_out[0], w_up[0], w_down[0])
    g3 = norm_final_g.reshape(1, D_MODEL)
    pw = pool_w[0]

    h1, proj, cpre, dpool, mcat, x1, win, wout, wup, wa_g, wf_g = _mixer_fwd(
        xs, norm_mix_g, win_b, wout_b, wup_b, wa_s, wf_s, conv_a_b, ln_a_g, ln_a_b, pw, pool_scale, mix_tile)
    wa = jnp.transpose(wa_g, (1, 0, 2)).reshape(32, D_CONV)
    wf = jnp.transpose(wf_g, (1, 0, 2)).reshape(8, D_FF)
    h2, up, gcs, act, wdown = _ffn_up(x1, norm_ffn_g, wup, wf, conv_f_b, wdown_b, min(UP_TILE, seq))
    dx2b, sm_f2 = _ffn_down(x1, act, wdown, g3, ts, mix_tile)
    tags = ("w_in", "w_out", "w_up", "w_down")
    blocks = (512, 128, 256, 352)
    g_wdown = _weight_grad(act, dx2b, "rows2", grad_k)
    dup, dx1b, sm_b1, sf, l_wdown = _ffn_bwd(
        dx2b, up, gcs, x1, norm_ffn_g, wup, wf, wdown, ("exchange", [g_wdown]), ffn_tile)
    p_wdown = _pair_sum(core, g_wdown, l_wdown, tags[3], g_wdown.shape[2])
    g_wup, s_wdown = _weight_grad(h2, dup, "cols_chip", grad_k, ("scatter", [p_wdown]))
    g_wout, l_wup = _weight_grad(mcat, dx1b, "rows1", grad_k, ("exchange", [g_wup]))
    p_wup = _pair_sum(core, g_wup, l_wup, tags[2], g_wup.shape[2])
    l_wout, = _sibling_exchange((g_wout,), (), "early")
    p_wout = _pair_sum(core, g_wout, l_wout, tags[1], g_wout.shape[2])
    dproj, gx, sm_b2, s5, sp, s_wout, s_wup = _mixer_bwd(
        dx1b, xs, proj, cpre, dpool, norm_mix_g, win, wa, ln_a_g, ln_a_b, pw, pool_scale, wout, [p_wout, p_wup], mix_tile)
    g_win, grad_x = _weight_grad(h1, dproj, "cols_half", grad_k, carry=gx)

    smalls = (sm_f2, sm_b1, sm_b2, sf, s5, sp)
    landed = _sibling_exchange((g_win,), smalls, "late")
    part_win = _pair_sum(core, g_win, landed[0], tags[0], g_win.shape[2])
    small_parts = _pair_sum_small(smalls, landed[1:])
    send, recv, late_src, late_land, token = _scatter_start([part_win], small_parts)
    big_w = (w_in[0], w_out[0], w_up[0], w_down[0])
    big_m = (m_w_in[0], m_w_out[0], m_w_up[0], m_w_down[0])
    big_v = (v_w_in[0], v_w_out[0], v_w_up[0], v_w_down[0])
    big = {}
    for t, p in ((1, s_wout), (2, s_wup), (3, s_wdown)):
        big[tags[t]] = _adam_big(p, big_w[t], big_m[t], big_v[t], tags[t], blocks[t], token)
    late_src, late_land = _scatter_wait(send, recv, late_src, late_land, 1, [big[tags[t]][3] for t in (1, 2, 3)])
    late = _scatter_forward(late_src, late_land, 1)
    big[tags[0]] = _adam_big(late[0], big_w[0], big_m[0], big_v[0], tags[0], blocks[0], token)
    big = {tag: [a[None] for a in outs] for tag, outs in big.items()}
    scattered = [None] * 4 + list(late[1:])

    (g_g1, g_g2, g_g3, loss_row, g_wf_all, g_fb, g_wa_all, g_cb, g_lg, g_lb, g_ps, g_pw) = _reduce_small(*scattered[4:])
    g_wa = lax.dynamic_slice(g_wa_all, (0, chip * (D_CONV // N_CHIPS)), (32, D_CONV // N_CHIPS))[:CONV_A]
    g_wf = lax.dynamic_slice(g_wf_all, (0, chip * (D_FF // N_CHIPS)), (8, D_FF // N_CHIPS))[:CONV_F]
    small_names = ("norm_mix_g", "conv_a_w", "conv_a_b", "ln_a_g", "ln_a_b", "pool_w", "pool_scale", "norm_ffn_g",
                   "conv_f_w", "conv_f_b", "norm_final_g")
    small_w = (norm_mix_g, conv_a_w[0], conv_a_b, ln_a_g, ln_a_b, pw, pool_scale, norm_ffn_g, conv_f_w[0], conv_f_b, g3)
    small_m = (m_norm_mix_g, m_conv_a_w[0], m_conv_a_b, m_ln_a_g, m_ln_a_b, m_pool_w[0], m_pool_scale, m_norm_ffn_g,
               m_conv_f_w[0], m_conv_f_b, m_norm_final_g.reshape(1, D_MODEL))
    small_v = (v_norm_mix_g, v_conv_a_w[0], v_conv_a_b, v_ln_a_g, v_ln_a_b, v_pool_w[0], v_pool_scale, v_norm_ffn_g,
               v_conv_f_w[0], v_conv_f_b, v_norm_final_g.reshape(1, D_MODEL))
    small_g = (g_g1, g_wa, g_cb, g_lg, g_lb, g_pw, g_ps, g_g2, g_wf, g_fb, g_g3)
    by_row = (small_names.index("conv_a_w"), small_names.index("conv_f_w"))
    s_g, s_delta, s_m, s_v = _adam_small(small_w, small_g, small_m, small_v, by_row)
    shapes = {"pool_w": pool_w.shape, "norm_final_g": norm_final_g.shape}
    small = {}
    for t, name in enumerate(small_names):
        shp = shapes.get(name)
        if t in by_row:
            small[name] = [jnp.transpose(a, (1, 0, 2)) for a in (s_g[t], s_delta[t], s_m[t], s_v[t])]
        else:
            small[name] = [a if shp is None else a.reshape(shp) for a in (s_g[t], s_delta[t], s_m[t], s_v[t])]

    order = ("norm_mix_g", "w_in", "conv_a_w", "conv_a_b", "ln_a_g", "ln_a_b", "pool_w", "pool_scale", "w_out", "norm_ffn_g",
             "w_up", "conv_f_w", "conv_f_b", "w_down", "norm_final_g")
    table = {**big, **small}
    loss = loss_row[0, 0]
    outs = [loss, grad_x[None]]
    for t in range(4):
        outs += [table[name][t] for name in order]
    return tuple(outs)
```

```python
import functools

import jax
import jax.numpy as jnp
from jax import lax
from jax.experimental import pallas as pl
from jax.experimental.pallas import tpu as pltpu

F32 = jnp.float32
BF16 = jnp.bfloat16
EPS = 1e-6
ADAM_LR = 0.001
ADAM_B1 = 0.9
ADAM_B2 = 0.999
ADAM_EPS = 1e-08
ADAM_WD = 0.01
ADAM_STEP = 10

D_MODEL = 1024
D_CONV = 512
D_POOL = 512
D_IN = 1536
D_FF = 2816
CONV_A = 31
CONV_F = 3
POOL_WINDOWS = (2, 4, 8, 16)
POOL_GROUP = 128
N_CHIPS = 4
FF_CHUNK = 256
N_FF_CHUNKS = D_FF // FF_CHUNK
UP_CHUNK = 2816
A_HALO = 32
P_HALO = 16
VMEM_LIMIT = 56 * 1024 * 1024
MESH = pl.DeviceIdType.MESH

ANY = pl.BlockSpec(memory_space=pl.ANY)
VMEM = pl.BlockSpec(memory_space=pltpu.VMEM)


def _dot(a, b):
    return jnp.dot(a, b, preferred_element_type=F32)


def _dot_nt(a, b):
    return lax.dot_general(a, b, (((1,), (1,)), ((), ())), preferred_element_type=F32)


def _dot_tn(a, b):
    return lax.dot_general(a, b, (((0,), (0,)), ((), ())), preferred_element_type=F32)


def _sigmoid(v):
    return jax.nn.sigmoid(v)


def _colsum(v):
    return jnp.sum(v, axis=0, keepdims=True)


def _rowmean(v):
    return jnp.mean(v, axis=-1, keepdims=True)


def _place():
    x, y, c = lax.axis_index("x"), lax.axis_index("y"), lax.axis_index("c")
    chips = [(1 - x, y), (x, 1 - y), (1 - x, 1 - y)]
    return x, y, c, 2 * x + y, chips


SIBLING_ONLY, SIBLING_AND_CHIPS = 0, 1


def _handshake(collective):
    x, y, c, _, chips = _place()
    peers = [(x, y, 1 - c)] + ([(*chip, c) for chip in chips] if collective == SIBLING_AND_CHIPS else [])
    barrier = pltpu.get_barrier_semaphore()
    for peer in peers:
        pl.semaphore_signal(barrier, inc=1, device_id=peer, device_id_type=MESH)
    pl.semaphore_wait(barrier, len(peers))


def _staged(src, dst, stage, sem_in, sem_out):
    hop_in = pltpu.make_async_copy(src, stage, sem_in)
    hop_out = pltpu.make_async_copy(stage, dst, sem_out)

    def relay():
        hop_in.wait()
        hop_out.start()

    return hop_in.start, relay, hop_out.wait


def _gather_ops(bufs, fulls, col_sharded, sems, stages):
    ici_send, ici_recv, fwd_send, fwd_recv, loc_in, loc_out = sems
    n_big = len(bufs)
    x, y, c, k, chips = _place()

    def block(i, kk, half=None):
        rows, cols = bufs[i].shape
        if col_sharded[i]:
            rs = slice(None) if half is None else pl.ds(pl.multiple_of(half * (rows // 2), 16), rows // 2)
            return fulls[i].at[rs, pl.ds(pl.multiple_of(kk * cols, 128), cols)]
        if half is None:
            return fulls[i].at[pl.ds(pl.multiple_of(kk * rows, 16), rows), :]
        return fulls[i].at[pl.ds(pl.multiple_of(kk * rows + half * (rows // 2), 16), rows // 2), :]

    def my_half(i):
        rows = bufs[i].shape[0]
        return bufs[i].at[pl.ds(pl.multiple_of(c * (rows // 2), 16), rows // 2), :]

    def ici(i, j, kk):
        return pltpu.make_async_remote_copy(
            src_ref=my_half(i), dst_ref=block(i, kk, c), send_sem=ici_send.at[i * 3 + j], recv_sem=ici_recv.at[i * 3 + j],
            device_id=(*chips[j], c), device_id_type=MESH)

    def fwd(i, j, kk, half):
        return pltpu.make_async_remote_copy(
            src_ref=block(i, kk, half), dst_ref=block(i, kk, half), send_sem=fwd_send.at[i * 3 + j],
            recv_sem=fwd_recv.at[i * 3 + j], device_id=(x, y, 1 - c), device_id_type=MESH)

    local = [_staged(bufs[i], block(i, k), stages[i], loc_in.at[i], loc_out.at[i]) for i in range(n_big)]
    sends = [ici(i, j, k) for i in range(n_big) for j in range(3)]
    peers = [(i, j, 2 * qx + qy) for i in range(n_big) for j, (qx, qy) in enumerate(chips)]

    def start():
        for cp in local:
            cp[0]()
        for cp in sends:
            cp.start()

    def land():
        for cp in local:
            cp[1]()
        for i, j, kq in peers:
            ici(i, j, kq).wait_recv()
            fwd(i, j, kq, c).start()

    def finish():
        for i, j, kq in peers:
            fwd(i, j, kq, 1 - c).wait_recv()
            fwd(i, j, kq, c).wait_send()
        for cp in sends:
            cp.wait_send()
        for cp in local:
            cp[2]()

    return start, land, finish


def _gather_scratch(shards):
    n_big = len(shards)
    return ([pltpu.SemaphoreType.DMA((3 * n_big,))] * 4 + [pltpu.SemaphoreType.DMA((n_big,))] * 2
            + [pltpu.VMEM(b.shape, b.dtype) for b in shards])


def _tap_ops(srcs, dsts, sems):
    send, recv, loc = sems
    _, _, c, k, chips = _place()

    def copy(t, j, kk):
        return pltpu.make_async_remote_copy(
            src_ref=srcs[t], dst_ref=dsts[t].at[kk], send_sem=send.at[t * 3 + j], recv_sem=recv.at[t * 3 + j],
            device_id=(*chips[j], c), device_id_type=MESH)

    local = [pltpu.make_async_copy(srcs[t], dsts[t].at[k], loc.at[t]) for t in range(len(srcs))]
    sends = [[copy(t, j, k) for j in range(3)] for t in range(len(srcs))]

    def start():
        for t, cp in enumerate(local):
            cp.start()
            for sd in sends[t]:
                sd.start()

    def wait(t):
        for j, (qx, qy) in enumerate(chips):
            copy(t, j, 2 * qx + qy).wait_recv()
        for sd in sends[t]:
            sd.wait_send()
        local[t].wait()

    return start, wait


def _cast_shards(*shards):
    steps = 4

    def body(*refs):
        for src, dst in zip(refs[:len(shards)], refs[len(shards):]):
            dst[...] = src[...].astype(BF16)

    specs = [pl.BlockSpec((s.shape[0] // steps, s.shape[1]), lambda i: (i, 0)) for s in shards]
    return pl.pallas_call(
        body, name="cast_shards", grid=(steps,), out_shape=[jax.ShapeDtypeStruct(s.shape, BF16) for s in shards],
        in_specs=specs, out_specs=specs,
        compiler_params=pltpu.CompilerParams(dimension_semantics=("arbitrary",), vmem_limit_bytes=VMEM_LIMIT),
    )(*shards)


def _load_weights(pairs, sem, first=0):
    cps = [pltpu.make_async_copy(src, dst, sem.at[first + i]) for i, (src, dst) in enumerate(pairs)]
    for cp in cps:
        cp.start()
    for cp in cps:
        cp.wait()


def _shifted_views(buf, shifted, t_rows):
    n = t_rows + A_HALO - 8
    for b in range(1, 8):
        shifted[b - 1] = buf[b:b + n, :]

    def view(offset):
        a, b = divmod(offset, 8)
        if b == 0:
            return buf[8 * a:8 * a + t_rows, :]
        return shifted[b - 1, 8 * a:8 * a + t_rows, :]

    return view


def _pool_count(tile, t_rows, w):
    row = lax.broadcasted_iota(jnp.int32, (t_rows, POOL_GROUP), 0) + tile * t_rows
    return jnp.minimum(row + 1, w).astype(F32)


def _mixer_fwd(x, g1, win_b, wout_b, wup_b, wa_s, wf_s, cb, lg, lb, pw, ps, tile_rows):
    seq = x.shape[0]
    tr = tile_rows
    n = seq // tr

    def body(x_ref, g1_ref, win_b_hbm, wout_b_hbm, wup_b_hbm, wa_s_hbm, wf_s_hbm, cb_ref, lg_ref, lb_ref, pw_ref,
             ps_ref, h1_ref, proj_ref, c_ref, d_ref, m_ref, x1_ref, win_f, wout_f, wup_f, wa_g, wf_g,
             win_v, wout_v, wa_ref, ubuf, ushift, bbuf, sem, *csems):
        i = pl.program_id(0)
        first_sems, first_stages, second_sems, second_stages, later_sems, later_stages, tap_sems = (
            csems[0:6], csems[6:7], csems[7:13], csems[13:14], csems[14:20], csems[20:21], csems[21:24])

        def first():
            return _gather_ops((win_b_hbm,), (win_f,), (True,), first_sems, first_stages)

        def second():
            return _gather_ops((wout_b_hbm,), (wout_f,), (False,), second_sems, second_stages)

        def later():
            return _gather_ops((wup_b_hbm,), (wup_f,), (True,), later_sems, later_stages)

        def taps():
            return _tap_ops((wa_s_hbm, wf_s_hbm), (wa_g, wf_g), tap_sems)

        @pl.when(i == 0)
        def _():
            _handshake(SIBLING_AND_CHIPS)
            first()[0]()
            taps()[0]()
            second()[0]()
            later()[0]()
            first()[1]()
            first()[2]()
            _load_weights([(win_f, win_v)], sem)
            ubuf[0:A_HALO, :] = jnp.zeros((A_HALO, D_CONV), F32)
            bbuf[0:P_HALO, :] = jnp.zeros((P_HALO, D_POOL), F32)

        xv = x_ref[...]
        r = lax.rsqrt(_rowmean(xv * xv) + EPS)
        h1 = (xv * r * g1_ref[...]).astype(BF16)
        h1_ref[...] = h1
        proj = _dot(h1, win_v[...])
        proj_ref[...] = proj.astype(BF16)

        @pl.when(i == 0)
        def _():
            taps()[1](0)
            _load_weights([(wa_g.at[kk], wa_ref.at[:, kk * (D_CONV // N_CHIPS):(kk + 1) * (D_CONV // N_CHIPS)])
                           for kk in range(N_CHIPS)], sem, 2)

        av, ag, bi = proj[:, :D_CONV], proj[:, D_CONV:2 * D_CONV], proj[:, 2 * D_CONV:]
        ubuf[A_HALO:A_HALO + tr, :] = av * _sigmoid(ag)
        off = A_HALO - (CONV_A - 1)
        uview = _shifted_views(ubuf, ushift, tr)
        acc = wa_ref[0:1, :] * uview(off)
        for j in range(1, CONV_A):
            acc = acc + wa_ref[j:j + 1, :] * uview(off + j)
        cv = acc + cb_ref[...]
        ubuf[0:A_HALO, :] = ubuf[tr:tr + A_HALO, :]
        c_ref[...] = cv.astype(BF16)
        xc = cv - _rowmean(cv)
        z = xc * lax.rsqrt(_rowmean(xc * xc) + EPS)
        ln = z * lg_ref[...] + lb_ref[...]
        ya = ln * _sigmoid(ln)
        bbuf[P_HALO:P_HALO + tr, :] = bi
        ds, ybs = [], []
        for g, w in enumerate(POOL_WINDOWS):
            cols = slice(g * POOL_GROUP, (g + 1) * POOL_GROUP)
            s = bi[:, cols]
            for kk in range(1, w):
                s = s + bbuf[P_HALO - kk:P_HALO - kk + tr, cols]
            dg = s / _pool_count(i, tr, w) - bi[:, cols]
            ds.append(dg)
            ybs.append(_dot(dg.astype(BF16), pw_ref[g].astype(BF16)))
        bbuf[0:P_HALO, :] = bbuf[tr:tr + P_HALO, :]
        d_ref[...] = jnp.concatenate(ds, axis=1).astype(BF16)
        yb = jnp.concatenate(ybs, axis=1) * ps_ref[...]
        m = jnp.concatenate([ya, yb], axis=1).astype(BF16)
        m_ref[...] = m

        @pl.when(i == 0)
        def _():
            second()[1]()
            second()[2]()
            _load_weights([(wout_f, wout_v)], sem, 1)

        x1_ref[...] = xv + _dot(m, wout_v[...])

        @pl.when(i == n - 1)
        def _():
            later()[1]()
            later()[2]()
            taps()[1](1)

    tile = lambda w: pl.BlockSpec((tr, w), lambda i: (i, 0))
    full = lambda a: pl.BlockSpec(a.shape, lambda i: (0,) * a.ndim)
    return pl.pallas_call(
        body, name="mixer_fwd", grid=(n,),
        in_specs=[tile(D_MODEL), full(g1)] + [ANY] * 5 + [full(cb), full(lg), full(lb), full(pw), full(ps)],
        out_specs=[tile(D_MODEL), tile(D_IN), tile(D_CONV), tile(D_POOL), tile(D_MODEL), tile(D_MODEL)] + [ANY] * 5,
        out_shape=[
            jax.ShapeDtypeStruct((seq, D_MODEL), BF16), jax.ShapeDtypeStruct((seq, D_IN), BF16),
            jax.ShapeDtypeStruct((seq, D_CONV), BF16), jax.ShapeDtypeStruct((seq, D_POOL), BF16),
            jax.ShapeDtypeStruct((seq, D_MODEL), BF16), jax.ShapeDtypeStruct((seq, D_MODEL), F32),
            jax.ShapeDtypeStruct((D_MODEL, D_IN), BF16), jax.ShapeDtypeStruct((D_MODEL, D_MODEL), BF16),
            jax.ShapeDtypeStruct((D_MODEL, 2 * D_FF), BF16),
            jax.ShapeDtypeStruct((N_CHIPS,) + wa_s.shape, F32), jax.ShapeDtypeStruct((N_CHIPS,) + wf_s.shape, F32),
        ],
        scratch_shapes=[
            pltpu.VMEM((D_MODEL, D_IN), BF16), pltpu.VMEM((D_MODEL, D_MODEL), BF16), pltpu.VMEM((32, D_CONV), F32),
            pltpu.VMEM((tr + A_HALO, D_CONV), F32), pltpu.VMEM((7, tr + A_HALO - 8, D_CONV), F32),
            pltpu.VMEM((tr + P_HALO, D_POOL), F32), pltpu.SemaphoreType.DMA((2 + N_CHIPS,)),
        ] + _gather_scratch((win_b,)) + _gather_scratch((wout_b,)) + _gather_scratch((wup_b,)) + [
            pltpu.SemaphoreType.DMA((6,)), pltpu.SemaphoreType.DMA((6,)), pltpu.SemaphoreType.DMA((2,))],
        compiler_params=pltpu.CompilerParams(dimension_semantics=("arbitrary",), vmem_limit_bytes=VMEM_LIMIT,
                                             collective_id=SIBLING_AND_CHIPS),
    )(x, g1, win_b, wout_b, wup_b, wa_s, wf_s, cb, lg, lb, pw, ps)


def _ffn_up(x1, g2, wup, wf, fb, wdown_b, tile_rows):
    seq = x1.shape[0]
    tr = tile_rows
    n = seq // tr

    def body(x1_ref, g2_ref, wup_hbm, wf_ref, fb_ref, wdown_b_hbm,
             h2_ref, up_ref, gc_ref, act_ref, wdown_f, wup_v, gbuf, sem, *gsems):
        i = pl.program_id(0)

        def gather():
            return _gather_ops((wdown_b_hbm,), (wdown_f,), (False,), gsems[:6], gsems[6:])

        @pl.when(i == 0)
        def _():
            _handshake(SIBLING_AND_CHIPS)
            gather()[0]()
            _load_weights(((wup_hbm, wup_v),), sem)
            gbuf[0:8, :] = jnp.zeros((8, D_FF), F32)

        x1v = x1_ref[...]
        r2 = lax.rsqrt(_rowmean(x1v * x1v) + EPS)
        h2 = (x1v * r2 * g2_ref[...]).astype(BF16)
        h2_ref[...] = h2

        def up_proj(j):
            return (_dot(h2, wup_v[:, j * UP_CHUNK:(j + 1) * UP_CHUNK]),
                    _dot(h2, wup_v[:, D_FF + j * UP_CHUNK:D_FF + (j + 1) * UP_CHUNK]))

        ahead = up_proj(0)
        for j in range(D_FF // UP_CHUNK):
            cs = slice(j * UP_CHUNK, (j + 1) * UP_CHUNK)
            vs = slice(D_FF + j * UP_CHUNK, D_FF + (j + 1) * UP_CHUNK)
            gate, val = ahead
            if j + 1 < D_FF // UP_CHUNK:
                ahead = up_proj(j + 1)
            up_ref[:, cs] = gate.astype(BF16)
            up_ref[:, vs] = val.astype(BF16)
            gbuf[8:8 + tr, cs] = gate
            gc = (wf_ref[0:1, cs] * gbuf[6:6 + tr, cs] + wf_ref[1:2, cs] * gbuf[7:7 + tr, cs]
                  + wf_ref[2:3, cs] * gate + fb_ref[:, cs])
            gbuf[0:8, cs] = gbuf[tr:tr + 8, cs]
            gc_ref[:, cs] = gc.astype(BF16)
            act_ref[:, cs] = (gc * _sigmoid(gc) * val).astype(BF16)

        @pl.when(i == max(n - 2, 0))
        def _():
            gather()[1]()

        @pl.when(i == n - 1)
        def _():
            gather()[2]()

    tile = lambda w: pl.BlockSpec((tr, w), lambda i: (i, 0))
    full = lambda a: pl.BlockSpec(a.shape, lambda i: (0,) * a.ndim)
    return pl.pallas_call(
        body, name="ffn_up", grid=(n,),
        in_specs=[tile(D_MODEL), full(g2), ANY, full(wf), full(fb), ANY],
        out_specs=[tile(D_MODEL), tile(2 * D_FF), tile(D_FF), tile(D_FF), ANY],
        out_shape=[
            jax.ShapeDtypeStruct((seq, D_MODEL), BF16), jax.ShapeDtypeStruct((seq, 2 * D_FF), BF16),
            jax.ShapeDtypeStruct((seq, D_FF), BF16), jax.ShapeDtypeStruct((seq, D_FF), BF16),
            jax.ShapeDtypeStruct((D_FF, D_MODEL), BF16),
        ],
        scratch_shapes=[pltpu.VMEM(wup.shape, BF16), pltpu.VMEM((tr + 8, D_FF), F32), pltpu.SemaphoreType.DMA((1,))]
        + _gather_scratch((wdown_b,)),
        compiler_params=pltpu.CompilerParams(dimension_semantics=("arbitrary",), vmem_limit_bytes=VMEM_LIMIT,
                                             collective_id=SIBLING_AND_CHIPS),
    )(x1, g2, wup, wf, fb, wdown_b)


def _ffn_down(x1, act, wdown, g3, target, tile_rows):
    seq = x1.shape[0]
    tr = tile_rows
    n = seq // tr

    def body(x1_ref, act_ref, wdown_hbm, g3_ref, t_ref, dx2b_ref, sm_ref, wdown_v, sem):
        i = pl.program_id(0)

        @pl.when(i == 0)
        def _():
            _load_weights(((wdown_hbm, wdown_v),), sem)
            sm_ref[...] = jnp.zeros(sm_ref.shape, F32)

        x2 = x1_ref[...] + _dot(act_ref[...], wdown_v[...])
        r3 = lax.rsqrt(_rowmean(x2 * x2) + EPS)
        n3 = x2 * r3
        err = n3 * g3_ref[...] - t_ref[...]
        dy = err / D_MODEL
        sm_ref[2:3, :] += _colsum(dy * n3)
        loss = 0.5 * _colsum(_rowmean(err * err))
        sm_ref[3:4, :] += jnp.broadcast_to(loss, (1, D_MODEL))
        dn = dy * g3_ref[...]
        dx2b_ref[...] = (r3 * (dn - n3 * _rowmean(dn * n3))).astype(BF16)

    tile = lambda w: pl.BlockSpec((tr, w), lambda i: (i, 0))
    full = lambda a: pl.BlockSpec(a.shape, lambda i: (0,) * a.ndim)
    return pl.pallas_call(
        body, name="ffn_down", grid=(n,),
        in_specs=[tile(D_MODEL), tile(D_FF), ANY, full(g3), tile(D_MODEL)],
        out_specs=[tile(D_MODEL), pl.BlockSpec((8, D_MODEL), lambda i: (0, 0))],
        out_shape=[jax.ShapeDtypeStruct((seq, D_MODEL), BF16), jax.ShapeDtypeStruct((8, D_MODEL), F32)],
        scratch_shapes=[pltpu.VMEM(wdown.shape, BF16), pltpu.SemaphoreType.DMA((1,))],
        compiler_params=pltpu.CompilerParams(dimension_semantics=("arbitrary",), vmem_limit_bytes=VMEM_LIMIT),
    )(x1, act, wdown, g3, target)


def _ffn_bwd(dx2, up, gcs, x1, g2, wup, wf, wdown, comm, tile_rows):
    seq = x1.shape[0]
    c_ins, c_shapes, c_sems, c_ops, c_id = _comm_plan(comm)
    nc = len(c_ins)
    tr = tile_rows
    n = seq // tr

    def body(dx2_ref, up_ref, gc_ref, x1_ref, g2_ref, wup_hbm, wf_ref, wdown_hbm, *rest):
        c_in, rest = rest[:nc], rest[nc:]
        dup_ref, dx1b_ref, sm_ref, sf_ref = rest[:4]
        c_out, rest = rest[4:4 + nc], rest[4 + nc:]
        wup_v, wdown_v, dbuf, dcar, sem = rest[:5]
        c_sem_refs = rest[5:]
        i = pl.program_id(0)

        @pl.when(i == 0)
        def _():
            c_ops(c_in, c_out, c_sem_refs)[0]()
            _load_weights(((wup_hbm, wup_v), (wdown_hbm, wdown_v)), sem)
            dcar[...] = jnp.zeros(dcar.shape, F32)
            sm_ref[...] = jnp.zeros(sm_ref.shape, F32)
            sf_ref[...] = jnp.zeros(sf_ref.shape, F32)

        dx2b = dx2_ref[...]
        dx2v = dx2b.astype(F32)
        dh2 = jnp.zeros((tr, D_MODEL), F32)

        def down_t(j):
            return _dot_nt(dx2b, wdown_v[j * FF_CHUNK:(j + 1) * FF_CHUNK, :])

        ahead = down_t(0)
        for j in range(N_FF_CHUNKS):
            cs = slice(j * FF_CHUNK, (j + 1) * FF_CHUNK)
            vs = slice(D_FF + j * FF_CHUNK, D_FF + (j + 1) * FF_CHUNK)
            dact = ahead
            if j + 1 < N_FF_CHUNKS:
                ahead = down_t(j + 1)
            gate = up_ref[:, cs].astype(F32)
            val = up_ref[:, vs].astype(F32)
            gc = gc_ref[:, cs].astype(F32)
            sg = _sigmoid(gc)
            dval = dact * (gc * sg)
            dgc = dact * val * (sg * (1.0 + gc * (1.0 - sg)))
            dbuf[0:tr, :] = dgc
            dbuf[tr:tr + 8, :] = dcar[:, cs]
            d_p1 = dbuf[1:1 + tr, :]
            d_p2 = dbuf[2:2 + tr, :]
            dgate = wf_ref[2:3, cs] * dgc + wf_ref[1:2, cs] * d_p1 + wf_ref[0:1, cs] * d_p2
            dcar[:, cs] = dgc[0:8, :]
            sf_ref[0:1, cs] += _colsum(d_p2 * gate)
            sf_ref[1:2, cs] += _colsum(d_p1 * gate)
            sf_ref[2:3, cs] += _colsum(dgc * gate)
            sf_ref[3:4, cs] += _colsum(dgc)
            dgb, dvb = dgate.astype(BF16), dval.astype(BF16)
            dup_ref[:, cs] = dgb
            dup_ref[:, vs] = dvb
            dh2 = dh2 + _dot_nt(dgb, wup_v[:, cs]) + _dot_nt(dvb, wup_v[:, vs])
        x1v = x1_ref[...]
        r2 = lax.rsqrt(_rowmean(x1v * x1v) + EPS)
        n2 = x1v * r2
        sm_ref[1:2, :] += _colsum(dh2 * n2)
        dn2 = dh2 * g2_ref[...]
        dx1b_ref[...] = (dx2v + r2 * (dn2 - n2 * _rowmean(dn2 * n2))).astype(BF16)

        @pl.when(i == n - 1)
        def _():
            c_ops(c_in, c_out, c_sem_refs)[2]()

    tile = lambda w: pl.BlockSpec((tr, w), lambda i: (n - 1 - i, 0))
    full = lambda a: pl.BlockSpec(a.shape, lambda i: (0,) * a.ndim)
    acc = lambda rows, w: pl.BlockSpec((rows, w), lambda i: (0, 0))
    return pl.pallas_call(
        body, name="ffn_bwd", grid=(n,),
        in_specs=[tile(D_MODEL), tile(2 * D_FF), tile(D_FF), tile(D_MODEL), full(g2), ANY, full(wf), ANY] + [ANY] * nc,
        out_specs=[tile(2 * D_FF), tile(D_MODEL), acc(8, D_MODEL), acc(8, D_FF)] + [ANY] * nc,
        out_shape=[
            jax.ShapeDtypeStruct((seq, 2 * D_FF), BF16), jax.ShapeDtypeStruct((seq, D_MODEL), BF16),
            jax.ShapeDtypeStruct((8, D_MODEL), F32), jax.ShapeDtypeStruct((8, D_FF), F32),
        ] + c_shapes,
        scratch_shapes=[
            pltpu.VMEM(wup.shape, BF16), pltpu.VMEM(wdown.shape, BF16),
            pltpu.VMEM((tr + 8, FF_CHUNK), F32), pltpu.VMEM((8, D_FF), F32), pltpu.SemaphoreType.DMA((2,)),
        ] + c_sems,
        compiler_params=pltpu.CompilerParams(dimension_semantics=("arbitrary",), vmem_limit_bytes=VMEM_LIMIT,
                                             collective_id=c_id),
    )(dx2, up, gcs, x1, g2, wup, wf, wdown, *c_ins)


def _mixer_bwd(dx1, x, proj, cpre, d, g1, win, wa, lg, lb, pw, ps, wout, parts, tile_rows):
    seq = x.shape[0]
    n_parts = len(parts)
    tr = tile_rows
    n = seq // tr
    row_cb, row_lg, row_lb, row_ps = 32, 33, 34, 35

    def body(dx1_ref, x_ref, proj_ref, projh_ref, c_ref, d_ref, g1_ref, win_hbm, wa_ref, lg_ref, lb_ref, pw_ref, ps_ref,
             wout_hbm, *rest):
        part_refs, rest = rest[:n_parts], rest[n_parts:]
        dproj_ref, gx_ref, sm_ref, s5_ref, sp_ref = rest[:5]
        land_refs, rest = rest[5:5 + n_parts], rest[5 + n_parts:]
        win_v, wout_v, ubuf, ushift, dcbuf, dshift, ebuf, sem = rest[:8]
        ssems = rest[8:]
        i = pl.program_id(0)
        tile = n - 1 - i

        def scatter():
            return _scatter_ops(part_refs, land_refs, n_parts, ssems[:6], ssems[6:])

        @pl.when(i == 0)
        def _():
            _handshake(SIBLING_AND_CHIPS)
            scatter()[0]()
            _load_weights(((win_hbm, win_v), (wout_hbm, wout_v)), sem)
            dcbuf[tr:tr + A_HALO, :] = jnp.zeros((A_HALO, D_CONV), F32)
            ebuf[tr:tr + P_HALO, :] = jnp.zeros((P_HALO, D_POOL), F32)
            sm_ref[...] = jnp.zeros(sm_ref.shape, F32)
            s5_ref[...] = jnp.zeros(s5_ref.shape, F32)
            sp_ref[...] = jnp.zeros(sp_ref.shape, F32)

        dx1b = dx1_ref[...]
        dx1v = dx1b.astype(F32)
        dm = _dot_nt(dx1b, wout_v[...])
        dya, dyb = dm[:, :D_CONV], dm[:, D_CONV:]
        dbis = []
        for g, w in enumerate(POOL_WINDOWS):
            cols = slice(g * POOL_GROUP, (g + 1) * POOL_GROUP)
            dgb = d_ref[:, cols]
            pwb = pw_ref[g].astype(BF16)
            dyg = dyb[:, cols]
            s5_ref[row_ps:row_ps + 1, cols] += _colsum(dyg * _dot(dgb, pwb))
            dqb = (dyg * ps_ref[:, cols]).astype(BF16)
            sp_ref[g] += _dot_tn(dgb, dqb)
            dd = _dot_nt(dqb, pwb)
            e = dd / _pool_count(tile, tr, w)
            ebuf[0:tr, cols] = e
            s = e
            for kk in range(1, w):
                s = s + ebuf[kk:kk + tr, cols]
            dbis.append(s - dd)
        ebuf[tr:tr + P_HALO, :] = ebuf[0:P_HALO, :]
        cv = c_ref[...].astype(F32)
        xc = cv - _rowmean(cv)
        rs = lax.rsqrt(_rowmean(xc * xc) + EPS)
        z = xc * rs
        ln = z * lg_ref[...] + lb_ref[...]
        sl = _sigmoid(ln)
        dl = dya * (sl * (1.0 + ln * (1.0 - sl)))
        s5_ref[row_lg:row_lg + 1, :] += _colsum(dl * z)
        s5_ref[row_lb:row_lb + 1, :] += _colsum(dl)
        dz = dl * lg_ref[...]
        dc = rs * (dz - _rowmean(dz) - z * _rowmean(dz * z))
        s5_ref[row_cb:row_cb + 1, :] += _colsum(dc)
        dcbuf[0:tr, :] = dc
        keep = (tile > 0).astype(F32)
        avh = projh_ref[:, :D_CONV].astype(F32)
        agh = projh_ref[:, D_CONV:].astype(F32)
        ubuf[0:A_HALO, :] = avh * _sigmoid(agh) * keep
        av = proj_ref[:, :D_CONV].astype(F32)
        ag = proj_ref[:, D_CONV:2 * D_CONV].astype(F32)
        sg = _sigmoid(ag)
        ubuf[A_HALO:A_HALO + tr, :] = av * sg
        off = A_HALO - (CONV_A - 1)
        du = wa_ref[CONV_A - 1:CONV_A, :] * dc
        dview = _shifted_views(dcbuf, dshift, tr)
        uview = _shifted_views(ubuf, ushift, tr)
        for j in range(CONV_A - 1):
            du = du + wa_ref[j:j + 1, :] * dview(CONV_A - 1 - j)
        for j in range(CONV_A):
            s5_ref[j:j + 1, :] += _colsum(dc * uview(off + j))
        dcbuf[tr:tr + A_HALO, :] = dcbuf[0:A_HALO, :]
        dav = du * sg
        dag = du * av * (sg * (1.0 - sg))
        dprojb = jnp.concatenate([dav, dag] + dbis, axis=1).astype(BF16)
        dproj_ref[...] = dprojb
        dh1 = _dot_nt(dprojb, win_v[...])
        xv = x_ref[...]
        r1 = lax.rsqrt(_rowmean(xv * xv) + EPS)
        n1 = xv * r1
        sm_ref[0:1, :] += _colsum(dh1 * n1)
        dn1 = dh1 * g1_ref[...]
        gx_ref[...] = dx1v + r1 * (dn1 - n1 * _rowmean(dn1 * n1))

        @pl.when(i == max(n - 2, 0))
        def _():
            scatter()[1]()

        @pl.when(i == n - 1)
        def _():
            scatter()[2]()

    tile = lambda w: pl.BlockSpec((tr, w), lambda i: (n - 1 - i, 0))
    full = lambda a: pl.BlockSpec(a.shape, lambda i: (0,) * a.ndim)
    halo = pl.BlockSpec((A_HALO, 2 * D_CONV), lambda i: (jnp.maximum((n - 1 - i) * (tr // A_HALO) - 1, 0), 0))
    acc = lambda shape: pl.BlockSpec(shape, lambda i: (0,) * len(shape))
    return pl.pallas_call(
        body, name="mixer_bwd", grid=(n,),
        in_specs=[tile(D_MODEL), tile(D_MODEL), tile(D_IN), halo, tile(D_CONV), tile(D_POOL), full(g1), ANY, full(wa),
                  full(lg), full(lb), full(pw), full(ps), ANY] + [ANY] * n_parts,
        out_specs=[tile(D_IN), tile(D_MODEL), acc((8, D_MODEL)), acc((40, D_CONV)), acc(pw.shape)] + [ANY] * n_parts,
        out_shape=[
            jax.ShapeDtypeStruct((seq, D_IN), BF16), jax.ShapeDtypeStruct((seq, D_MODEL), F32),
            jax.ShapeDtypeStruct((8, D_MODEL), F32), jax.ShapeDtypeStruct((40, D_CONV), F32),
            jax.ShapeDtypeStruct(pw.shape, F32),
        ] + _scatter_shapes(parts, ()),
        scratch_shapes=[
            pltpu.VMEM(win.shape, BF16), pltpu.VMEM(wout.shape, BF16),
            pltpu.VMEM((tr + A_HALO, D_CONV), F32), pltpu.VMEM((7, tr + A_HALO - 8, D_CONV), F32),
            pltpu.VMEM((tr + A_HALO, D_CONV), F32), pltpu.VMEM((7, tr + A_HALO - 8, D_CONV), F32),
            pltpu.VMEM((tr + P_HALO, D_POOL), F32), pltpu.SemaphoreType.DMA((2,)),
        ] + _scatter_scratch(parts, ()),
        compiler_params=pltpu.CompilerParams(dimension_semantics=("arbitrary",), vmem_limit_bytes=VMEM_LIMIT,
                                             collective_id=SIBLING_AND_CHIPS),
    )(dx1, x, proj, proj, cpre, d, g1, win, wa, lg, lb, pw, ps, wout, *parts)


def _weight_grad(a, b, layout, k_rows, comm=None, carry=None):
    seq, m_dim = a.shape
    n_dim = b.shape[1]
    steps = seq // k_rows

    def store(o_ref, acc, index, value):
        if steps == 1:
            o_ref[index] = value.astype(BF16)
            return
        s = pl.program_id(1)

        @pl.when(s == 0)
        def _():
            acc[index] = value

        @pl.when(jnp.logical_and(s > 0, s < steps - 1))
        def _():
            acc[index] += value

        @pl.when(s == steps - 1)
        def _():
            o_ref[index] = (acc[index] + value).astype(BF16)

    if layout in ("rows1", "rows2"):
        groups = int(layout[-1])
        per_tile = N_CHIPS // groups
        rows = m_dim // N_CHIPS // 2
        a_w = m_dim // groups

        def body(a_ref, b_ref, o_ref, acc):
            r = _dot_tn(a_ref[...], b_ref[...])
            for p in range(per_tile):
                for h in range(2):
                    store(o_ref, acc, (h, p), r[(2 * p + h) * rows:(2 * p + h + 1) * rows, :])

        in_specs = [pl.BlockSpec((k_rows, a_w), lambda g, s: (s, g)), pl.BlockSpec((k_rows, n_dim), lambda g, s: (s, 0))]
        out_spec = pl.BlockSpec((2, per_tile, rows, n_dim), lambda g, s: (0, g, 0, 0))
        out_dims, acc_dims = (2, N_CHIPS, rows, n_dim), (2, per_tile, rows, n_dim)
    elif layout == "cols_chip":
        groups = N_CHIPS
        rows, cols = m_dim // 2, n_dim // N_CHIPS

        def body(a_ref, b_ref, o_ref, acc):
            r = _dot_tn(a_ref[...], b_ref[...])
            for h in range(2):
                store(o_ref, acc, h, r[h * rows:(h + 1) * rows, :])

        in_specs = [pl.BlockSpec((k_rows, m_dim), lambda g, s: (s, 0)), pl.BlockSpec((k_rows, cols), lambda g, s: (s, g))]
        out_spec = pl.BlockSpec((2, None, rows, cols), lambda g, s: (0, g, 0, 0))
        out_dims, acc_dims = (2, N_CHIPS, rows, cols), (2, rows, cols)
    else:
        groups = 2
        rows, cols = m_dim // 2, n_dim // N_CHIPS

        def body(a_ref, b_ref, o_ref, acc):
            r = _dot_tn(a_ref[...], b_ref[...])
            for k in range(N_CHIPS):
                store(o_ref, acc, k, r[:, k * cols:(k + 1) * cols])

        in_specs = [pl.BlockSpec((k_rows, rows), lambda g, s: (s, g)), pl.BlockSpec((k_rows, n_dim), lambda g, s: (s, 0))]
        out_spec = pl.BlockSpec((None, N_CHIPS, rows, cols), lambda g, s: (g, 0, 0, 0))
        out_dims, acc_dims = (2, N_CHIPS, rows, cols), (N_CHIPS, rows, cols)

    c_ins, c_shapes, c_sems, c_ops, c_id = _comm_plan(comm)
    nc = len(c_ins)
    c_specs = [ANY] * nc
    if carry is not None:
        assert comm is None and carry.shape[0] % (groups * steps) == 0
        carry_spec = pl.BlockSpec((carry.shape[0] // (groups * steps), carry.shape[1]), lambda g, s: (g * steps + s, 0))
        c_ins, c_shapes, c_specs, nc = (carry,), [jax.ShapeDtypeStruct(carry.shape, carry.dtype)], [carry_spec], 1

    def hosted(a_ref, b_ref, *rest):
        c_in, o_ref, c_out, acc, sems = rest[:nc], rest[nc], rest[nc + 1:2 * nc + 1], rest[2 * nc + 1], rest[2 * nc + 2:]
        g, s = pl.program_id(0), pl.program_id(1)
        if carry is not None:
            c_out[0][...] = c_in[0][...]
            body(a_ref, b_ref, o_ref, acc)
            return
        if nc:
            @pl.when(jnp.logical_and(g == 0, s == 0))
            def _():
                c_ops(c_in, c_out, sems)[0]()

        body(a_ref, b_ref, o_ref, acc)
        if nc:
            step = g * steps + s

            @pl.when(step == max(groups * steps - 2, 0))
            def _():
                c_ops(c_in, c_out, sems)[1]()

            @pl.when(step == groups * steps - 1)
            def _():
                c_ops(c_in, c_out, sems)[2]()

    outs = pl.pallas_call(
        hosted, name=f"weight_grad_{layout}_{m_dim}x{n_dim}", grid=(groups, steps),
        in_specs=in_specs + c_specs, out_specs=[out_spec] + c_specs,
        out_shape=[jax.ShapeDtypeStruct(out_dims, BF16)] + c_shapes,
        scratch_shapes=[pltpu.VMEM(acc_dims, F32)] + c_sems,
        compiler_params=pltpu.CompilerParams(dimension_semantics=("arbitrary", "arbitrary"), vmem_limit_bytes=VMEM_LIMIT,
                                             collective_id=c_id),
    )(a, b, *c_ins)
    return outs if nc else outs[0]


def _exchange_ops(ins, outs, n_big, sems):
    send, recv = sems
    x, y, c, _, _ = _place()
    cps = [pltpu.make_async_remote_copy(
        src_ref=ins[t].at[1 - c] if t < n_big else ins[t], dst_ref=outs[t], send_sem=send.at[t], recv_sem=recv.at[t],
        device_id=(x, y, 1 - c), device_id_type=MESH) for t in range(len(ins))]

    def start():
        for cp in cps:
            cp.start()

    def finish():
        for cp in cps:
            cp.wait()

    return start, finish


def _exchange_shapes(bigs, smalls):
    return [jax.ShapeDtypeStruct((N_CHIPS,) + b.shape[2:], b.dtype) for b in bigs] + [
        jax.ShapeDtypeStruct(s.shape, s.dtype) for s in smalls]


def _comm_plan(comm):
    if comm is None:
        return (), [], [], None, None
    kind, arrays = comm
    n = len(arrays)

    def scatter(i, o, sm):
        start, land, finish = _scatter_ops(i, o, n, sm[:6], sm[6:])
        return lambda: (_handshake(SIBLING_AND_CHIPS), start()), land, finish

    def exchange(i, o, sm):
        start, finish = _exchange_ops(i, o, n, sm)
        return lambda: (_handshake(SIBLING_ONLY), start()), lambda: None, finish

    if kind == "scatter":
        return tuple(arrays), _scatter_shapes(arrays, ()), _scatter_scratch(arrays, ()), scatter, SIBLING_AND_CHIPS
    return tuple(arrays), _exchange_shapes(arrays, ()), [pltpu.SemaphoreType.DMA((n,))] * 2, exchange, SIBLING_ONLY


def _sibling_exchange(bigs, smalls, tag):
    nb, nt = len(bigs), len(bigs) + len(smalls)

    def body(*refs):
        start, finish = _exchange_ops(refs[:nt], refs[nt:2 * nt], nb, refs[2 * nt:])
        _handshake(SIBLING_ONLY)
        start()
        finish()

    return pl.pallas_call(
        body, name=f"sibling_exchange_{tag}", out_shape=_exchange_shapes(bigs, smalls),
        in_specs=[ANY] * nt, out_specs=[ANY] * nt,
        scratch_shapes=[pltpu.SemaphoreType.DMA((nt,)), pltpu.SemaphoreType.DMA((nt,))],
        compiler_params=pltpu.CompilerParams(collective_id=SIBLING_ONLY),
    )(*bigs, *smalls)


def _pair_sum(core, mine, theirs, tag, block_rows):
    _, _, rows, cols = mine.shape
    steps = rows // block_rows

    def body(core_ref, a_ref, b_ref, o_ref):
        o_ref[...] = (a_ref[...].astype(F32) + b_ref[...].astype(F32)).astype(BF16)

    grid_spec = pltpu.PrefetchScalarGridSpec(
        num_scalar_prefetch=1, grid=(N_CHIPS, steps),
        in_specs=[pl.BlockSpec((None, None, block_rows, cols), lambda k, r, core_ref: (core_ref[0], k, r, 0)),
                  pl.BlockSpec((None, block_rows, cols), lambda k, r, core_ref: (k, r, 0))],
        out_specs=pl.BlockSpec((None, block_rows, cols), lambda k, r, core_ref: (k, r, 0)),
    )
    return pl.pallas_call(
        body, name=f"pair_sum_{tag}", grid_spec=grid_spec,
        out_shape=jax.ShapeDtypeStruct((N_CHIPS, rows, cols), BF16),
        compiler_params=pltpu.CompilerParams(dimension_semantics=("arbitrary", "arbitrary"), vmem_limit_bytes=VMEM_LIMIT),
    )(core, mine, theirs)


def _pair_sum_small(mine, theirs):
    (m_f2, m_b1, m_b2, m_sf, m_s5, m_sp) = mine

    def body(a0, a1, a2, a3, a4, a5, b0, b1, b2, b3, b4, b5, o_m, o_f, o_5, o_p):
        sm = (a0[...] + a1[...] + a2[...]) + (b0[...] + b1[...] + b2[...])
        sf = a3[...] + b3[...]
        s5 = a4[...] + b4[...]
        for h in range(2):
            o_m[h] = sm[:, h * (D_MODEL // 2):(h + 1) * (D_MODEL // 2)]
            o_f[h] = sf[:, h * (D_FF // 2):(h + 1) * (D_FF // 2)]
            o_5[h] = s5[:, h * (D_CONV // 2):(h + 1) * (D_CONV // 2)]
            for g in range(2):
                o_p[h, g] = a5[2 * h + g] + b5[2 * h + g]

    out_shape = [
        jax.ShapeDtypeStruct((2, 8, D_MODEL // 2), F32), jax.ShapeDtypeStruct((2, 8, D_FF // 2), F32),
        jax.ShapeDtypeStruct((2, 40, D_CONV // 2), F32), jax.ShapeDtypeStruct((2, 2, POOL_GROUP, POOL_GROUP), F32),
    ]
    return pl.pallas_call(body, name="pair_sum_small", out_shape=out_shape, in_specs=[VMEM] * 12, out_specs=[VMEM] * 4)(
        *mine, *theirs)


def _scatter_ops(ins, outs, n_parts, sems, stages, landed=False):
    ici_send, ici_recv, fwd_send, fwd_recv, loc_in, loc_out = sems
    nt = len(ins)
    x, y, c, k, chips = _place()

    def src_of(t, kk):
        return ins[t].at[kk] if t < n_parts else ins[t].at[c]

    def ici(t, j, kk, slot):
        return pltpu.make_async_remote_copy(
            src_ref=src_of(t, kk), dst_ref=outs[t].at[c, slot], send_sem=ici_send.at[t * 3 + j],
            recv_sem=ici_recv.at[t * 3 + j], device_id=(*chips[j], c), device_id_type=MESH)

    def fwd(t, half):
        slots = outs[t].at[half]
        return pltpu.make_async_remote_copy(
            src_ref=slots, dst_ref=slots, send_sem=fwd_send.at[t], recv_sem=fwd_recv.at[t],
            device_id=(x, y, 1 - c), device_id_type=MESH)

    local = [_staged(src_of(t, k), outs[t].at[c, k], stages[t], loc_in.at[t], loc_out.at[t]) for t in range(nt)]
    peers = [(t, j, 2 * qx + qy) for t in range(nt) for j, (qx, qy) in enumerate(chips)]
    sends = [] if landed else [ici(t, j, kq, k) for t, j, kq in peers]

    def start():
        for cp in local:
            cp[0]()
        for cp in sends:
            cp.start()

    def land():
        for cp in local:
            cp[1]()
        if not landed:
            for t, j, kq in peers:
                ici(t, j, kq, kq).wait_recv()
        for cp in local:
            cp[2]()
        for t in range(nt):
            fwd(t, c).start()

    def finish():
        for t in range(nt):
            fwd(t, 1 - c).wait_recv()
            fwd(t, c).wait_send()
        for cp in sends:
            cp.wait_send()

    return start, land, finish


def _scatter_scratch(parts, smalls):
    arrays = tuple(parts) + tuple(smalls)
    nt = len(arrays)
    return ([pltpu.SemaphoreType.DMA((3 * nt,))] * 2 + [pltpu.SemaphoreType.DMA((nt,))] * 4
            + [pltpu.VMEM(a.shape[1:], a.dtype) for a in arrays])


def _scatter_shapes(parts, smalls):
    return [jax.ShapeDtypeStruct((2, N_CHIPS) + p.shape[1:], p.dtype) for p in tuple(parts) + tuple(smalls)]


HBM_SPEC = pl.BlockSpec(memory_space=pltpu.HBM)
SEM_SPEC = pl.BlockSpec(memory_space=pltpu.SEMAPHORE)
EFFECT = pltpu.SideEffectType.DATAFLOW_SIDE_EFFECTING


def _ici_copy(ins, lands, n_parts, send, recv, t, j):
    _, _, c, k, chips = _place()
    qx, qy = chips[j]
    src = ins[t].at[2 * qx + qy] if t < n_parts else ins[t].at[c]
    return pltpu.make_async_remote_copy(
        src_ref=src, dst_ref=lands[t].at[c, k], send_sem=send.at[t * 3 + j], recv_sem=recv.at[t * 3 + j],
        device_id=(qx, qy, c), device_id_type=MESH)


def _scatter_start(parts, smalls):
    arrays = tuple(parts) + tuple(smalls)
    nt = len(arrays)

    def body(*refs):
        ins, lands = refs[:nt], refs[nt:2 * nt]
        send, recv = refs[2 * nt], refs[2 * nt + 1]
        token = refs[-1]
        for t in range(nt):
            for j in range(3):
                _ici_copy(ins, lands, len(parts), send, recv, t, j).start()
        token[...] = jnp.zeros(token.shape, F32)

    land_shapes = _scatter_shapes(parts, smalls)
    out_shape = ([pltpu.SemaphoreType.DMA((3 * nt,))] * 2 + [pltpu.HBM(a.shape, a.dtype) for a in arrays]
                 + [pltpu.HBM(a.shape, a.dtype) for a in land_shapes] + [jax.ShapeDtypeStruct((8, 128), F32)])
    operands = [pltpu.with_memory_space_constraint(a, pltpu.HBM) for a in arrays]
    operands += [pltpu.with_memory_space_constraint(lax.empty(a.shape, a.dtype), pltpu.HBM) for a in land_shapes]
    outs = pl.pallas_call(
        body, name="scatter_start", out_shape=out_shape, in_specs=[HBM_SPEC] * (2 * nt),
        out_specs=[SEM_SPEC] * 2 + [HBM_SPEC] * (2 * nt) + [VMEM],
        input_output_aliases={i: 2 + i for i in range(2 * nt)},
        compiler_params=pltpu.CompilerParams(has_side_effects=EFFECT),
    )(*operands)
    return outs[0], outs[1], outs[2:2 + nt], outs[2 + nt:2 + 2 * nt], outs[-1]


def _scatter_wait(send, recv, ins, lands, n_parts, after):
    nt = len(ins)

    def body(*refs):
        in_refs, land_refs = refs[:nt], refs[nt:2 * nt]
        send_ref, recv_ref = refs[2 * nt], refs[2 * nt + 1]
        for t in range(nt):
            for j in range(3):
                cp = _ici_copy(in_refs, land_refs, n_parts, send_ref, recv_ref, t, j)
                cp.wait_send()
                cp.wait_recv()

    outs = pl.pallas_call(
        body, name="scatter_wait", out_shape=[pltpu.HBM(a.shape, a.dtype) for a in tuple(ins) + tuple(lands)],
        in_specs=[HBM_SPEC] * (2 * nt) + [SEM_SPEC] * 2 + [ANY] * len(after), out_specs=[HBM_SPEC] * (2 * nt),
        input_output_aliases={i: i for i in range(2 * nt)},
        compiler_params=pltpu.CompilerParams(has_side_effects=EFFECT),
    )(*ins, *lands, send, recv, *after)
    return outs[:nt], outs[nt:]


def _scatter_forward(ins, lands, n_parts):
    nt = len(ins)

    def body(*refs):
        start, land, finish = _scatter_ops(
            refs[:nt], refs[2 * nt:3 * nt], n_parts, refs[3 * nt:3 * nt + 6], refs[3 * nt + 6:], landed=True)
        _handshake(SIBLING_ONLY)
        start()
        land()
        finish()

    return pl.pallas_call(
        body, name="scatter_forward", out_shape=[jax.ShapeDtypeStruct(a.shape, a.dtype) for a in lands],
        in_specs=[ANY] * (2 * nt), out_specs=[ANY] * nt, input_output_aliases={nt + i: i for i in range(nt)},
        scratch_shapes=_scatter_scratch(ins[:n_parts], ins[n_parts:]),
        compiler_params=pltpu.CompilerParams(collective_id=SIBLING_ONLY),
    )(*ins, *lands)


def _chip_scatter(parts, smalls):
    nt = len(parts) + len(smalls)

    def body(*refs):
        start, land, finish = _scatter_ops(refs[:nt], refs[nt:2 * nt], len(parts), refs[2 * nt:2 * nt + 6], refs[2 * nt + 6:])
        _handshake(SIBLING_AND_CHIPS)
        start()
        land()
        finish()

    return pl.pallas_call(
        body, name="chip_scatter", out_shape=_scatter_shapes(parts, smalls), in_specs=[ANY] * nt, out_specs=[ANY] * nt,
        scratch_shapes=_scatter_scratch(parts, smalls),
        compiler_params=pltpu.CompilerParams(collective_id=SIBLING_AND_CHIPS),
    )(*parts, *smalls)


def _adamw(w, g, m, v):
    m = ADAM_B1 * m + (1.0 - ADAM_B1) * g
    v = ADAM_B2 * v + (1.0 - ADAM_B2) * (g * g)
    m_hat = m / (1.0 - ADAM_B1 ** ADAM_STEP)
    v_hat = v / (1.0 - ADAM_B2 ** ADAM_STEP)
    delta = -ADAM_LR * (m_hat / (jnp.sqrt(v_hat) + ADAM_EPS) + ADAM_WD * w)
    return delta, m, v


def _adam_big(parts, w, m, v, tag, block_rows, token):
    _, _, rows, cols = parts.shape
    steps = rows // block_rows

    def body(p_ref, w_ref, m_ref, v_ref, token_ref, g_out, d_out, m_out, v_out):
        g = p_ref[0].astype(F32)
        for q in range(1, N_CHIPS):
            g = g + p_ref[q].astype(F32)
        delta, m_new, v_new = _adamw(w_ref[...], g, m_ref[...], v_ref[...])
        g_out[...] = g
        d_out[...] = delta
        m_out[...] = m_new
        v_out[...] = v_new

    blk = pl.BlockSpec((block_rows, cols), lambda h, r: (h * steps + r, 0))
    return pl.pallas_call(
        body, name=f"adam_{tag}", grid=(2, steps),
        in_specs=[pl.BlockSpec((None, N_CHIPS, block_rows, cols), lambda h, r: (h, 0, r, 0)), blk, blk, blk, ANY],
        out_specs=[blk] * 4, out_shape=[jax.ShapeDtypeStruct(w.shape, F32)] * 4,
        compiler_params=pltpu.CompilerParams(dimension_semantics=("arbitrary", "arbitrary"), vmem_limit_bytes=VMEM_LIMIT),
    )(parts, w, m, v, token)


def _reduce_small(l_m, l_f, l_5, l_p):
    def total(ref):
        t = ref[:, 0]
        for q in range(1, N_CHIPS):
            t = t + ref[:, q]
        return t

    def body(m_ref, f_ref, s_ref, p_ref, g1_o, g2_o, g3_o, loss_o, wf_o, fb_o, wa_o, cb_o, lg_o, lb_o, ps_o, pw_o):
        tm, tf, t5, tp = total(m_ref), total(f_ref), total(s_ref), total(p_ref)
        sm = jnp.concatenate([tm[0], tm[1]], axis=1)
        sf = jnp.concatenate([tf[0], tf[1]], axis=1)
        s5 = jnp.concatenate([t5[0], t5[1]], axis=1)
        g1_o[...] = sm[0:1]
        g2_o[...] = sm[1:2]
        g3_o[...] = sm[2:3]
        loss_o[...] = sm[3:4, 0:128]
        wf_o[...] = sf
        fb_o[...] = sf[3:4]
        wa_o[...] = s5[0:32]
        cb_o[...] = s5[32:33]
        lg_o[...] = s5[33:34]
        lb_o[...] = s5[34:35]
        ps_o[...] = s5[35:36]
        for h in range(2):
            for g in range(2):
                pw_o[2 * h + g] = tp[h, g]

    row = lambda w: jax.ShapeDtypeStruct((1, w), F32)
    out_shape = [row(D_MODEL), row(D_MODEL), row(D_MODEL), row(128), jax.ShapeDtypeStruct((8, D_FF), F32), row(D_FF),
                 jax.ShapeDtypeStruct((32, D_CONV), F32), row(D_CONV), row(D_CONV), row(D_CONV), row(D_POOL),
                 jax.ShapeDtypeStruct((4, POOL_GROUP, POOL_GROUP), F32)]
    return pl.pallas_call(body, name="reduce_small", out_shape=out_shape, in_specs=[VMEM] * 4, out_specs=[VMEM] * 12)(
        l_m, l_f, l_5, l_p)


def _adam_small(ws, gs, ms, vs, by_row):
    count = len(ws)

    def body(*refs):
        w_r, g_r, m_r, v_r = (refs[t * count:(t + 1) * count] for t in range(4))
        outs = [refs[(4 + t) * count:(5 + t) * count] for t in range(4)]
        for t in range(count):
            g = g_r[t][...]
            values = (g,) + _adamw(w_r[t][...], g, m_r[t][...], v_r[t][...])
            for o, value in zip(outs, values):
                if t in by_row:
                    for r in range(value.shape[0]):
                        o[t][r] = value[r:r + 1, :]
                else:
                    o[t][...] = value

    shape = lambda t, w: (w.shape[0], 1, w.shape[1]) if t in by_row else w.shape
    out_shape = [jax.ShapeDtypeStruct(shape(t, w), F32) for t, w in enumerate(ws)] * 4
    outs = pl.pallas_call(body, name="adam_small", out_shape=out_shape, in_specs=[VMEM] * (4 * count),
                          out_specs=[VMEM] * (4 * count))(*ws, *gs, *ms, *vs)
    return [outs[t * count:(t + 1) * count] for t in range(4)]


MIX_TILE = 512
UP_TILE = 512
FFN_TILE = 256
GRAD_K = 2048


def kernel(x, norm_mix_g, w_in, conv_a_w, conv_a_b, ln_a_g, ln_a_b, pool_w, pool_scale, w_out, norm_ffn_g, w_up, conv_f_w, conv_f_b, w_down, norm_final_g, loss_target, m_norm_mix_g, m_w_in, m_conv_a_w, m_conv_a_b, m_ln_a_g, m_ln_a_b, m_pool_w, m_pool_scale, m_w_out, m_norm_ffn_g, m_w_up, m_conv_f_w, m_conv_f_b, m_w_down, m_norm_final_g, v_norm_mix_g, v_w_in, v_conv_a_w, v_conv_a_b, v_ln_a_g, v_ln_a_b, v_pool_w, v_pool_scale, v_w_out, v_norm_ffn_g, v_w_up, v_conv_f_w, v_conv_f_b, v_w_down, v_norm_final_g):
    seq = x.shape[1]
    xs, ts = x[0], loss_target[0]
    mix_tile, ffn_tile, grad_k = min(MIX_TILE, seq), min(FFN_TILE, seq), min(GRAD_K, seq)
    chip = 2 * lax.axis_index("x") + lax.axis_index("y")
    core = lax.axis_index("c").astype(jnp.int32).reshape(1)

    wa_s = jnp.pad(conv_a_w[0], ((0, 32 - CONV_A), (0, 0)))
    wf_s = jnp.pad(conv_f_w[0], ((0, 8 - CONV_F), (0, 0)))
    win_b, wout_b, wup_b, wdown_b = _cast_shards(w_in[0], w_out[0], w_up[0], w_down[0])
    g3 = norm_final_g.reshape(1, D_MODEL)
    pw = pool_w[0]

    h1, proj, cpre, dpool, mcat, x1, win, wout, wup, wa_g, wf_g = _mixer_fwd(
        xs, norm_mix_g, win_b, wout_b, wup_b, wa_s, wf_s, conv_a_b, ln_a_g, ln_a_b, pw, pool_scale, mix_tile)
    wa = jnp.transpose(wa_g, (1, 0, 2)).reshape(32, D_CONV)
    wf = jnp.transpose(wf_g, (1, 0, 2)).reshape(8, D_FF)
    h2, up, gcs, act, wdown = _ffn_up(x1, norm_ffn_g, wup, wf, conv_f_b, wdown_b, min(UP_TILE, seq))
    dx2b, sm_f2 = _ffn_down(x1, act, wdown, g3, ts, mix_tile)
    tags = ("w_in", "w_out", "w_up", "w_down")
    blocks = (512, 128, 256, 352)
    g_wdown = _weight_grad(act, dx2b, "rows2", grad_k)
    dup, dx1b, sm_b1, sf, l_wdown = _ffn_bwd(
        dx2b, up, gcs, x1, norm_ffn_g, wup, wf, wdown, ("exchange", [g_wdown]), ffn_tile)
    p_wdown = _pair_sum(core, g_wdown, l_wdown, tags[3], g_wdown.shape[2])
    g_wup, s_wdown = _weight_grad(h2, dup, "cols_chip", grad_k, ("scatter", [p_wdown]))
    g_wout, l_wup = _weight_grad(mcat, dx1b, "rows1", grad_k, ("exchange", [g_wup]))
    p_wup = _pair_sum(core, g_wup, l_wup, tags[2], g_wup.shape[2])
    l_wout, = _sibling_exchange((g_wout,), (), "early")
    p_wout = _pair_sum(core, g_wout, l_wout, tags[1], g_wout.shape[2])
    dproj, gx, sm_b2, s5, sp, s_wout, s_wup = _mixer_bwd(
        dx1b, xs, proj, cpre, dpool, norm_mix_g, win, wa, ln_a_g, ln_a_b, pw, pool_scale, wout, [p_wout, p_wup], mix_tile)
    g_win, grad_x = _weight_grad(h1, dproj, "cols_half", grad_k, carry=gx)

    smalls = (sm_f2, sm_b1, sm_b2, sf, s5, sp)
    landed = _sibling_exchange((g_win,), smalls, "late")
    part_win = _pair_sum(core, g_win, landed[0], tags[0], g_win.shape[2])
    small_parts = _pair_sum_small(smalls, landed[1:])
    send, recv, late_src, late_land, token = _scatter_start([part_win], small_parts)
    big_w = (w_in[0], w_out[0], w_up[0], w_down[0])
    big_m = (m_w_in[0], m_w_out[0], m_w_up[0], m_w_down[0])
    big_v = (v_w_in[0], v_w_out[0], v_w_up[0], v_w_down[0])
    big = {}
    for t, p in ((1, s_wout), (2, s_wup), (3, s_wdown)):
        big[tags[t]] = _adam_big(p, big_w[t], big_m[t], big_v[t], tags[t], blocks[t], token)
    late_src, late_land = _scatter_wait(send, recv, late_src, late_land, 1, [big[tags[t]][3] for t in (1, 2, 3)])
    late = _scatter_forward(late_src, late_land, 1)
    big[tags[0]] = _adam_big(late[0], big_w[0], big_m[0], big_v[0], tags[0], blocks[0], token)
    big = {tag: [a[None] for a in outs] for tag, outs in big.items()}
    scattered = [None] * 4 + list(late[1:])

    (g_g1, g_g2, g_g3, loss_row, g_wf_all, g_fb, g_wa_all, g_cb, g_lg, g_lb, g_ps, g_pw) = _reduce_small(*scattered[4:])
    g_wa = lax.dynamic_slice(g_wa_all, (0, chip * (D_CONV // N_CHIPS)), (32, D_CONV // N_CHIPS))[:CONV_A]
    g_wf = lax.dynamic_slice(g_wf_all, (0, chip * (D_FF // N_CHIPS)), (8, D_FF // N_CHIPS))[:CONV_F]
    small_names = ("norm_mix_g", "conv_a_w", "conv_a_b", "ln_a_g", "ln_a_b", "pool_w", "pool_scale", "norm_ffn_g",
                   "conv_f_w", "conv_f_b", "norm_final_g")
    small_w = (norm_mix_g, conv_a_w[0], conv_a_b, ln_a_g, ln_a_b, pw, pool_scale, norm_ffn_g, conv_f_w[0], conv_f_b, g3)
    small_m = (m_norm_mix_g, m_conv_a_w[0], m_conv_a_b, m_ln_a_g, m_ln_a_b, m_pool_w[0], m_pool_scale, m_norm_ffn_g,
               m_conv_f_w[0], m_conv_f_b, m_norm_final_g.reshape(1, D_MODEL))
    small_v = (v_norm_mix_g, v_conv_a_w[0], v_conv_a_b, v_ln_a_g, v_ln_a_b, v_pool_w[0], v_pool_scale, v_norm_ffn_g,
               v_conv_f_w[0], v_conv_f_b, v_norm_final_g.reshape(1, D_MODEL))
    small_g = (g_g1, g_wa, g_cb, g_lg, g_lb, g_pw, g_ps, g_g2, g_wf, g_fb, g_g3)
    by_row = (small_names.index("conv_a_w"), small_names.index("conv_f_w"))
    s_g, s_delta, s_m, s_v = _adam_small(small_w, small_g, small_m, small_v, by_row)
    shapes = {"pool_w": pool_w.shape, "norm_final_g": norm_final_g.shape}
    small = {}
    for t, name in enumerate(small_names):
        shp = shapes.get(name)
        if t in by_row:
            small[name] = [jnp.transpose(a, (1, 0, 2)) for a in (s_g[t], s_delta[t], s_m[t], s_v[t])]
        else:
            small[name] = [a if shp is None else a.reshape(shp) for a in (s_g[t], s_delta[t], s_m[t], s_v[t])]

    order = ("norm_mix_g", "w_in", "conv_a_w", "conv_a_b", "ln_a_g", "ln_a_b", "pool_w", "pool_scale", "w_out", "norm_ffn_g",
             "w_up", "conv_f_w", "conv_f_b", "w_down", "norm_final_g")
    table = {**big, **small}
    loss = loss_row[0, 0]
    outs = [loss, grad_x[None]]
    for t in range(4):
        outs += [table[name][t] for name in order]
    return tuple(outs)
```
